```python
import jax, jax.numpy as jnp
from jax import lax
import numpy as np

D_MODEL = 1024
BATCH = 8
SEQ = 4096
DEPTH = 2

LRU_HEADS = 16
LRU_HEAD_DIM = 64
D_LRU = LRU_HEADS * LRU_HEAD_DIM
SC_GROUPS = 8
SC_GROUP_DIM = 64
D_SC = SC_GROUPS * SC_GROUP_DIM
D_MIX = D_LRU + D_SC
D_IN = 2 * D_LRU + 3 * D_SC
LRU_CONV_WIDTH = 4
SC_CONV_WIDTH = 3
RG_C = 8.0
D_FF = 3 * D_MODEL
FFN_CONV_WIDTH = 3
EPS = 1e-6

kernel_name = "hymba_style_rglru_shortconv_convffn"


def rms_norm(x, g):
    xf = x.astype(jnp.float32)
    y = xf * lax.rsqrt(jnp.mean(xf * xf, axis=-1, keepdims=True) + EPS)
    return (y * g.astype(jnp.float32)).astype(x.dtype)


def causal_dwconv(x, w):
    k_width = w.shape[0]
    s = x.shape[1]
    xp = jnp.pad(x, ((0, 0), (k_width - 1, 0), (0, 0)))
    y = xp[:, 0:s] * w[0]
    for k in range(1, k_width):
        y = y + xp[:, k:k + s] * w[k]
    return y


def rg_lru(x, wa, ba, wx, bx, lam):
    bsz, s, c = x.shape
    xh = x.reshape(bsz, s, LRU_HEADS, LRU_HEAD_DIM)
    r = jax.nn.sigmoid(jnp.einsum('bshi,hij->bshj', xh, wa).reshape(bsz, s, c) + ba)
    i = jax.nn.sigmoid(jnp.einsum('bshi,hij->bshj', xh, wx).reshape(bsz, s, c) + bx)
    log_a = -RG_C * r.astype(jnp.float32) * jax.nn.softplus(-lam.astype(jnp.float32))
    a = jnp.exp(log_a)
    mult = jnp.sqrt(-jnp.expm1(2.0 * log_a))
    b = mult * (i * x).astype(jnp.float32)

    def combine(left, right):
        a1, b1 = left
        a2, b2 = right
        return a1 * a2, a2 * b1 + b2

    _, h = lax.associative_scan(combine, (a, b), axis=1)
    return h.astype(x.dtype)


def _fwd_setup_inputs(seed: int = 0) -> dict:
    key = jax.random.key(seed)
    ks = jax.random.split(key, 20)
    f32 = jnp.float32
    res_scale = (2.0 * DEPTH) ** -0.5
    x = jax.random.normal(ks[0], (BATCH, SEQ, D_MODEL), f32)
    norm1_g = 1.0 + 0.02 * jax.random.normal(ks[1], (DEPTH, D_MODEL), f32)
    w_in = jax.random.normal(ks[2], (DEPTH, D_MODEL, D_IN), f32) * D_MODEL ** -0.5
    lru_conv_w = jax.random.normal(ks[3], (DEPTH, LRU_CONV_WIDTH, D_LRU), f32) * LRU_CONV_WIDTH ** -0.5
    lru_conv_b = 0.02 * jax.random.normal(ks[4], (DEPTH, D_LRU), f32)
    lru_wa = jax.random.normal(ks[5], (DEPTH, LRU_HEADS, LRU_HEAD_DIM, LRU_HEAD_DIM), f32) * LRU_HEAD_DIM ** -0.5
    lru_ba = 0.02 * jax.random.normal(ks[6], (DEPTH, D_LRU), f32)
    lru_wx = jax.random.normal(ks[7], (DEPTH, LRU_HEADS, LRU_HEAD_DIM, LRU_HEAD_DIM), f32) * LRU_HEAD_DIM ** -0.5
    lru_bx = 0.02 * jax.random.normal(ks[8], (DEPTH, D_LRU), f32)
    u = jax.random.uniform(ks[9], (DEPTH, D_LRU), f32, minval=0.9, maxval=0.999)
    a0 = u ** (1.0 / RG_C)
    lru_lambda = jnp.log(a0) - jnp.log1p(-a0)
    sc_conv_w = jax.random.normal(ks[10], (DEPTH, SC_CONV_WIDTH, D_SC), f32) * SC_CONV_WIDTH ** -0.5
    w_out = jax.random.normal(ks[11], (DEPTH, D_MIX, D_MODEL), f32) * D_MIX ** -0.5 * res_scale
    norm2_g = 1.0 + 0.02 * jax.random.normal(ks[12], (DEPTH, D_MODEL), f32)
    w_up = jax.random.normal(ks[13], (DEPTH, D_MODEL, 2 * D_FF), f32) * D_MODEL ** -0.5
    ffn_conv_w = jax.random.normal(ks[14], (DEPTH, FFN_CONV_WIDTH, 2 * D_FF), f32) * FFN_CONV_WIDTH ** -0.5
    w_down = jax.random.normal(ks[15], (DEPTH, D_FF, D_MODEL), f32) * D_FF ** -0.5 * res_scale
    final_g = 1.0 + 0.02 * jax.random.normal(ks[16], (D_MODEL,), f32)
    return {"x": x, "norm1_g": norm1_g, "w_in": w_in, "lru_conv_w": lru_conv_w,
            "lru_conv_b": lru_conv_b, "lru_wa": lru_wa, "lru_ba": lru_ba, "lru_wx": lru_wx,
            "lru_bx": lru_bx, "lru_lambda": lru_lambda, "sc_conv_w": sc_conv_w, "w_out": w_out,
            "norm2_g": norm2_g, "w_up": w_up, "ffn_conv_w": ffn_conv_w, "w_down": w_down,
            "final_g": final_g}


def _fwd_reference(x, norm1_g, w_in, lru_conv_w, lru_conv_b, lru_wa, lru_ba, lru_wx, lru_bx,
              lru_lambda, sc_conv_w, w_out, norm2_g, w_up, ffn_conv_w, w_down, final_g):
    splits = [D_LRU, 2 * D_LRU, 2 * D_LRU + D_SC, 2 * D_LRU + 2 * D_SC]
    for l in range(DEPTH):
        h = rms_norm(x, norm1_g[l])
        z = jnp.einsum('bsd,de->bse', h, w_in[l])
        lru_x, lru_gate, sc_b, sc_c, sc_x = jnp.split(z, splits, axis=-1)
        lru_x = causal_dwconv(lru_x, lru_conv_w[l]) + lru_conv_b[l]
        y_lru = rg_lru(lru_x, lru_wa[l], lru_ba[l], lru_wx[l], lru_bx[l], lru_lambda[l]) \
            * jax.nn.gelu(lru_gate)
        y_sc = sc_b * causal_dwconv(sc_c * sc_x, sc_conv_w[l])
        y_mix = jnp.concatenate([y_lru, y_sc], axis=-1)
        x = x + jnp.einsum('bse,ed->bsd', y_mix, w_out[l])
        h = rms_norm(x, norm2_g[l])
        u = causal_dwconv(jnp.einsum('bsd,df->bsf', h, w_up[l]), ffn_conv_w[l])
        gate, up = jnp.split(u, 2, axis=-1)
        x = x + jnp.einsum('bsf,fd->bsd', jax.nn.gelu(gate) * up, w_down[l])
    return rms_norm(x, final_g)


import jax as _jax
import jax.numpy as _jnp

TWIN_FORMAT = 'train_step'
FWD_PARAMS = ['x', 'norm1_g', 'w_in', 'lru_conv_w', 'lru_conv_b', 'lru_wa', 'lru_ba', 'lru_wx', 'lru_bx', 'lru_lambda', 'sc_conv_w', 'w_out', 'norm2_g', 'w_up', 'ffn_conv_w', 'w_down', 'final_g']
TWIN_WEIGHTS = ['norm1_g', 'w_in', 'lru_conv_w', 'lru_conv_b', 'lru_wa', 'lru_ba', 'lru_wx', 'lru_bx', 'lru_lambda', 'sc_conv_w', 'w_out', 'norm2_g', 'w_up', 'ffn_conv_w', 'w_down', 'final_g']
TWIN_DIFF_INPUT = 'x'
TWIN_INPUTS = ['x', 'norm1_g', 'w_in', 'lru_conv_w', 'lru_conv_b', 'lru_wa', 'lru_ba', 'lru_wx', 'lru_bx', 'lru_lambda', 'sc_conv_w', 'w_out', 'norm2_g', 'w_up', 'ffn_conv_w', 'w_down', 'final_g', 'loss_target', 'm_norm1_g', 'm_w_in', 'm_lru_conv_w', 'm_lru_conv_b', 'm_lru_wa', 'm_lru_ba', 'm_lru_wx', 'm_lru_bx', 'm_lru_lambda', 'm_sc_conv_w', 'm_w_out', 'm_norm2_g', 'm_w_up', 'm_ffn_conv_w', 'm_w_down', 'm_final_g', 'v_norm1_g', 'v_w_in', 'v_lru_conv_w', 'v_lru_conv_b', 'v_lru_wa', 'v_lru_ba', 'v_lru_wx', 'v_lru_bx', 'v_lru_lambda', 'v_sc_conv_w', 'v_w_out', 'v_norm2_g', 'v_w_up', 'v_ffn_conv_w', 'v_w_down', 'v_final_g']
TWIN_OUTPUTS = ['loss', 'grad_x', 'grad_norm1_g', 'grad_w_in', 'grad_lru_conv_w', 'grad_lru_conv_b', 'grad_lru_wa', 'grad_lru_ba', 'grad_lru_wx', 'grad_lru_bx', 'grad_lru_lambda', 'grad_sc_conv_w', 'grad_w_out', 'grad_norm2_g', 'grad_w_up', 'grad_ffn_conv_w', 'grad_w_down', 'grad_final_g', 'delta_norm1_g', 'delta_w_in', 'delta_lru_conv_w', 'delta_lru_conv_b', 'delta_lru_wa', 'delta_lru_ba', 'delta_lru_wx', 'delta_lru_bx', 'delta_lru_lambda', 'delta_sc_conv_w', 'delta_w_out', 'delta_norm2_g', 'delta_w_up', 'delta_ffn_conv_w', 'delta_w_down', 'delta_final_g', 'new_m_norm1_g', 'new_m_w_in', 'new_m_lru_conv_w', 'new_m_lru_conv_b', 'new_m_lru_wa', 'new_m_lru_ba', 'new_m_lru_wx', 'new_m_lru_bx', 'new_m_lru_lambda', 'new_m_sc_conv_w', 'new_m_w_out', 'new_m_norm2_g', 'new_m_w_up', 'new_m_ffn_conv_w', 'new_m_w_down', 'new_m_final_g', 'new_v_norm1_g', 'new_v_w_in', 'new_v_lru_conv_w', 'new_v_lru_conv_b', 'new_v_lru_wa', 'new_v_lru_ba', 'new_v_lru_wx', 'new_v_lru_bx', 'new_v_lru_lambda', 'new_v_sc_conv_w', 'new_v_w_out', 'new_v_norm2_g', 'new_v_w_up', 'new_v_ffn_conv_w', 'new_v_w_down', 'new_v_final_g']
TWIN_LEAF_KINDS = {'loss': 'loss', 'grad_x': 'grad_x', 'grad_norm1_g': 'grad_w', 'grad_w_in': 'grad_w', 'grad_lru_conv_w': 'grad_w', 'grad_lru_conv_b': 'grad_w', 'grad_lru_wa': 'grad_w', 'grad_lru_ba': 'grad_w', 'grad_lru_wx': 'grad_w', 'grad_lru_bx': 'grad_w', 'grad_lru_lambda': 'grad_w', 'grad_sc_conv_w': 'grad_w', 'grad_w_out': 'grad_w', 'grad_norm2_g': 'grad_w', 'grad_w_up': 'grad_w', 'grad_ffn_conv_w': 'grad_w', 'grad_w_down': 'grad_w', 'grad_final_g': 'grad_w', 'delta_norm1_g': 'delta_w', 'delta_w_in': 'delta_w', 'delta_lru_conv_w': 'delta_w', 'delta_lru_conv_b': 'delta_w', 'delta_lru_wa': 'delta_w', 'delta_lru_ba': 'delta_w', 'delta_lru_wx': 'delta_w', 'delta_lru_bx': 'delta_w', 'delta_lru_lambda': 'delta_w', 'delta_sc_conv_w': 'delta_w', 'delta_w_out': 'delta_w', 'delta_norm2_g': 'delta_w', 'delta_w_up': 'delta_w', 'delta_ffn_conv_w': 'delta_w', 'delta_w_down': 'delta_w', 'delta_final_g': 'delta_w', 'new_m_norm1_g': 'new_m', 'new_m_w_in': 'new_m', 'new_m_lru_conv_w': 'new_m', 'new_m_lru_conv_b': 'new_m', 'new_m_lru_wa': 'new_m', 'new_m_lru_ba': 'new_m', 'new_m_lru_wx': 'new_m', 'new_m_lru_bx': 'new_m', 'new_m_lru_lambda': 'new_m', 'new_m_sc_conv_w': 'new_m', 'new_m_w_out': 'new_m', 'new_m_norm2_g': 'new_m', 'new_m_w_up': 'new_m', 'new_m_ffn_conv_w': 'new_m', 'new_m_w_down': 'new_m', 'new_m_final_g': 'new_m', 'new_v_norm1_g': 'new_v', 'new_v_w_in': 'new_v', 'new_v_lru_conv_w': 'new_v', 'new_v_lru_conv_b': 'new_v', 'new_v_lru_wa': 'new_v', 'new_v_lru_ba': 'new_v', 'new_v_lru_wx': 'new_v', 'new_v_lru_bx': 'new_v', 'new_v_lru_lambda': 'new_v', 'new_v_sc_conv_w': 'new_v', 'new_v_w_out': 'new_v', 'new_v_norm2_g': 'new_v', 'new_v_w_up': 'new_v', 'new_v_ffn_conv_w': 'new_v', 'new_v_w_down': 'new_v', 'new_v_final_g': 'new_v'}


def _forward(args):
    return _fwd_reference(*[args[k] for k in FWD_PARAMS])


def _output_shape():
    out = _jax.eval_shape(lambda: _forward(_fwd_setup_inputs(0)))
    return out.shape, out.dtype

N_MICROBATCH = 1
ADAM_LR = 0.001
ADAM_B1 = 0.9
ADAM_B2 = 0.999
ADAM_EPS = 1e-08
ADAM_WD = 0.01
ADAM_STEP = 10
PER_EXAMPLE_BATCH_AXIS = {'x': 0, 'loss_target': 0}
SHARED_INPUTS = []
_WEIGHT_DTYPES = {'norm1_g': _jnp.float32, 'w_in': _jnp.float32, 'lru_conv_w': _jnp.float32, 'lru_conv_b': _jnp.float32, 'lru_wa': _jnp.float32, 'lru_ba': _jnp.float32, 'lru_wx': _jnp.float32, 'lru_bx': _jnp.float32, 'lru_lambda': _jnp.float32, 'sc_conv_w': _jnp.float32, 'w_out': _jnp.float32, 'norm2_g': _jnp.float32, 'w_up': _jnp.float32, 'ffn_conv_w': _jnp.float32, 'w_down': _jnp.float32, 'final_g': _jnp.float32}
MOMENT_SCALE = {'norm1_g': 1.044274e-01, 'w_in': 5.452906e-02, 'lru_conv_w': 3.830739e-02, 'lru_conv_b': 4.154163e-01, 'lru_wa': 1.415753e-02, 'lru_ba': 1.082436e-02, 'lru_wx': 2.559040e-02, 'lru_bx': 1.448412e-02, 'lru_lambda': 2.071311e-02, 'sc_conv_w': 7.442254e-02, 'w_out': 1.281441e-01, 'norm2_g': 8.084124e-02, 'w_up': 3.168222e-02, 'ffn_conv_w': 3.182043e-02, 'w_down': 1.080423e-01, 'final_g': 3.199303e+01}


def _to_microbatches(a, axis):
    t = _jnp.moveaxis(a, axis, 0)
    t = t.reshape((N_MICROBATCH, t.shape[0] // N_MICROBATCH) + t.shape[1:])
    return _jnp.moveaxis(t, 1, axis + 1)


def setup_inputs(seed: int = 0) -> dict:
    inp = _fwd_setup_inputs(seed)
    key = _jax.random.fold_in(_jax.random.key(seed), 7919)
    shape, _ = _output_shape()
    out = dict(inp)
    out["loss_target"] = _jax.random.normal(_jax.random.fold_in(key, 0), shape, _jnp.float32)
    for i, name in enumerate(TWIN_WEIGHTS):
        w = inp[name].astype(_jnp.float32)
        if MOMENT_SCALE is None:
            s = _jnp.sqrt(_jnp.mean(_jnp.square(w)) + 1e-30)
        else:
            s = MOMENT_SCALE[name]
        km, kv = _jax.random.split(_jax.random.fold_in(key, i + 1))
        out[name] = w
        out["m_" + name] = s * _jax.random.normal(km, w.shape, _jnp.float32)
        out["v_" + name] = (s * s) * _jax.random.uniform(kv, w.shape, _jnp.float32, 0.5, 1.5)
    if N_MICROBATCH > 1:
        for name, axis in PER_EXAMPLE_BATCH_AXIS.items():
            out[name] = _to_microbatches(out[name], axis)
    return {'x': out['x'], 'norm1_g': out['norm1_g'], 'w_in': out['w_in'], 'lru_conv_w': out['lru_conv_w'], 'lru_conv_b': out['lru_conv_b'], 'lru_wa': out['lru_wa'], 'lru_ba': out['lru_ba'], 'lru_wx': out['lru_wx'], 'lru_bx': out['lru_bx'], 'lru_lambda': out['lru_lambda'], 'sc_conv_w': out['sc_conv_w'], 'w_out': out['w_out'], 'norm2_g': out['norm2_g'], 'w_up': out['w_up'], 'ffn_conv_w': out['ffn_conv_w'], 'w_down': out['w_down'], 'final_g': out['final_g'], 'loss_target': out['loss_target'], 'm_norm1_g': out['m_norm1_g'], 'm_w_in': out['m_w_in'], 'm_lru_conv_w': out['m_lru_conv_w'], 'm_lru_conv_b': out['m_lru_conv_b'], 'm_lru_wa': out['m_lru_wa'], 'm_lru_ba': out['m_lru_ba'], 'm_lru_wx': out['m_lru_wx'], 'm_lru_bx': out['m_lru_bx'], 'm_lru_lambda': out['m_lru_lambda'], 'm_sc_conv_w': out['m_sc_conv_w'], 'm_w_out': out['m_w_out'], 'm_norm2_g': out['m_norm2_g'], 'm_w_up': out['m_w_up'], 'm_ffn_conv_w': out['m_ffn_conv_w'], 'm_w_down': out['m_w_down'], 'm_final_g': out['m_final_g'], 'v_norm1_g': out['v_norm1_g'], 'v_w_in': out['v_w_in'], 'v_lru_conv_w': out['v_lru_conv_w'], 'v_lru_conv_b': out['v_lru_conv_b'], 'v_lru_wa': out['v_lru_wa'], 'v_lru_ba': out['v_lru_ba'], 'v_lru_wx': out['v_lru_wx'], 'v_lru_bx': out['v_lru_bx'], 'v_lru_lambda': out['v_lru_lambda'], 'v_sc_conv_w': out['v_sc_conv_w'], 'v_w_out': out['v_w_out'], 'v_norm2_g': out['v_norm2_g'], 'v_w_up': out['v_w_up'], 'v_ffn_conv_w': out['v_ffn_conv_w'], 'v_w_down': out['v_w_down'], 'v_final_g': out['v_final_g']}


def _loss(weights, diff, rest, loss_target):
    with _jax.named_scope("forward"):
        args = {**rest, TWIN_DIFF_INPUT: diff, **{k: w.astype(_WEIGHT_DTYPES[k]) for k, w in weights.items()}}
        y = _forward(args)
    with _jax.named_scope("loss_head"):
        err = _jnp.square(y.astype(_jnp.float32) - loss_target)
        return 0.5 * _jnp.sum(_jnp.mean(err, axis=-1)) if err.ndim else 0.5 * err


def _adamw(w, g, m, v):
    m = ADAM_B1 * m + (1.0 - ADAM_B1) * g
    v = ADAM_B2 * v + (1.0 - ADAM_B2) * _jnp.square(g)
    m_hat = m / (1.0 - ADAM_B1 ** ADAM_STEP)
    v_hat = v / (1.0 - ADAM_B2 ** ADAM_STEP)
    delta = -ADAM_LR * (m_hat / (_jnp.sqrt(v_hat) + ADAM_EPS) + ADAM_WD * w)
    return delta, m, v


def reference(x, norm1_g, w_in, lru_conv_w, lru_conv_b, lru_wa, lru_ba, lru_wx, lru_bx, lru_lambda, sc_conv_w, w_out, norm2_g, w_up, ffn_conv_w, w_down, final_g, loss_target, m_norm1_g, m_w_in, m_lru_conv_w, m_lru_conv_b, m_lru_wa, m_lru_ba, m_lru_wx, m_lru_bx, m_lru_lambda, m_sc_conv_w, m_w_out, m_norm2_g, m_w_up, m_ffn_conv_w, m_w_down, m_final_g, v_norm1_g, v_w_in, v_lru_conv_w, v_lru_conv_b, v_lru_wa, v_lru_ba, v_lru_wx, v_lru_bx, v_lru_lambda, v_sc_conv_w, v_w_out, v_norm2_g, v_w_up, v_ffn_conv_w, v_w_down, v_final_g):
    given = dict(x=x, norm1_g=norm1_g, w_in=w_in, lru_conv_w=lru_conv_w, lru_conv_b=lru_conv_b, lru_wa=lru_wa, lru_ba=lru_ba, lru_wx=lru_wx, lru_bx=lru_bx, lru_lambda=lru_lambda, sc_conv_w=sc_conv_w, w_out=w_out, norm2_g=norm2_g, w_up=w_up, ffn_conv_w=ffn_conv_w, w_down=w_down, final_g=final_g, loss_target=loss_target, m_norm1_g=m_norm1_g, m_w_in=m_w_in, m_lru_conv_w=m_lru_conv_w, m_lru_conv_b=m_lru_conv_b, m_lru_wa=m_lru_wa, m_lru_ba=m_lru_ba, m_lru_wx=m_lru_wx, m_lru_bx=m_lru_bx, m_lru_lambda=m_lru_lambda, m_sc_conv_w=m_sc_conv_w, m_w_out=m_w_out, m_norm2_g=m_norm2_g, m_w_up=m_w_up, m_ffn_conv_w=m_ffn_conv_w, m_w_down=m_w_down, m_final_g=m_final_g, v_norm1_g=v_norm1_g, v_w_in=v_w_in, v_lru_conv_w=v_lru_conv_w, v_lru_conv_b=v_lru_conv_b, v_lru_wa=v_lru_wa, v_lru_ba=v_lru_ba, v_lru_wx=v_lru_wx, v_lru_bx=v_lru_bx, v_lru_lambda=v_lru_lambda, v_sc_conv_w=v_sc_conv_w, v_w_out=v_w_out, v_norm2_g=v_norm2_g, v_w_up=v_w_up, v_ffn_conv_w=v_ffn_conv_w, v_w_down=v_w_down, v_final_g=v_final_g)
    weights = {n: given[n] for n in TWIN_WEIGHTS}
    shared = {n: given[n] for n in SHARED_INPUTS}
    per_example = {n: given[n] for n in ['x']}
    grad_fn = _jax.value_and_grad(_loss, argnums=(0, 1))

    def one_microbatch(ex, loss_target):
        ex = dict(ex)
        diff = ex.pop(TWIN_DIFF_INPUT)
        return grad_fn(weights, diff, {**shared, **ex}, loss_target)

    if N_MICROBATCH == 1:
        loss, (grad_w, grad_x) = one_microbatch(per_example, given["loss_target"])
    else:
        def body(carry, xs):
            loss_sum, grad_sum = carry
            l_k, (gw_k, gx_k) = one_microbatch(xs[0], xs[1])
            with _jax.named_scope("update"):
                return (loss_sum + l_k, _jax.tree.map(_jnp.add, grad_sum, gw_k)), gx_k

        init = (_jnp.zeros((), _jnp.float32), _jax.tree.map(_jnp.zeros_like, weights))
        (loss, grad_w), grad_x = _jax.lax.scan(body, init, (per_example, given["loss_target"]))
    with _jax.named_scope("update"):
        delta_w, new_m, new_v = {}, {}, {}
        for n in TWIN_WEIGHTS:
            delta_w[n], new_m[n], new_v[n] = _adamw(weights[n], grad_w[n], given["m_" + n], given["v_" + n])
    return (loss, grad_x, *[grad_w[n] for n in TWIN_WEIGHTS], *[delta_w[n] for n in TWIN_WEIGHTS],
            *[new_m[n] for n in TWIN_WEIGHTS], *[new_v[n] for n in TWIN_WEIGHTS])
```

```python
import functools
import math

import jax
import jax.numpy as jnp
from jax import lax
from jax.experimental import pallas as pl
from jax.experimental.pallas import tpu as pltpu

F32 = jnp.float32
BF16 = jnp.bfloat16

N_DEV = 8
DEPTH = 2
D_MODEL = 1024
D_LRU = 1024
D_SC = 512
D_MIX = D_LRU + D_SC
D_IN = 2 * D_LRU + 3 * D_SC
D_FF = 3072
LRU_HEADS = 16
LRU_HEAD_DIM = 64
LRU_GROUP = 256
N_GROUPS = D_LRU // LRU_GROUP
HEADS_PER_GROUP = LRU_GROUP // LRU_HEAD_DIM
RG_C = 8.0
EPS = 1e-6
HALO = 8

ADAM_LR = 0.001
ADAM_B1 = 0.9
ADAM_B2 = 0.999
ADAM_EPS = 1e-08
ADAM_WD = 0.01
ADAM_STEP = 10

GELU_C = math.sqrt(2.0 / math.pi)
GELU_A = 0.044715

VMEM_LIMIT = 56 * 1024 * 1024
MESH = pl.DeviceIdType.MESH


def _params(*sem):
    return pltpu.CompilerParams(dimension_semantics=tuple(sem) if sem else None,
                                vmem_limit_bytes=VMEM_LIMIT)


def _gelu_parts(x):
    x2 = x * x
    t = jnp.tanh(GELU_C * (x + GELU_A * x * x2))
    half = 0.5 * (1.0 + t)
    g = x * half
    dg = half + 0.5 * x * (1.0 - t * t) * (GELU_C * (1.0 + 3.0 * GELU_A * x2))
    return g, dg


def _gelu(x):
    t = jnp.tanh(GELU_C * (x + GELU_A * x * x * x))
    return 0.5 * x * (1.0 + t)


def _sigmoid(x):
    return 1.0 / (1.0 + jnp.exp(-x))


def _softplus(x):
    e = jnp.exp(-jnp.abs(x))
    u = 1.0 + e
    log1p_e = jnp.where(u == 1.0, e, jnp.log(u) * (e / (u - 1.0)))
    return jnp.maximum(x, 0.0) + log1p_e


def _rms(x):
    ms = jnp.mean(x * x, axis=-1, keepdims=True)
    return lax.rsqrt(ms + EPS)


def _dot(a, b, dims):
    return lax.dot_general(a, b, (dims, ((), ())), preferred_element_type=F32)


NN = ((1,), (0,))
NT = ((1,), (1,))
TN = ((0,), (0,))


def _matmul(a, b, *, dims, grid, a_spec, b_spec, o_spec, out_shape, acc_shape, name,
            residual=None, r_spec=None):
    nk = grid[2]

    def body(*refs):
        if residual is None:
            a_ref, b_ref, o_ref, acc_ref = refs
            r_ref = None
        else:
            a_ref, b_ref, r_ref, o_ref, acc_ref = refs
        k = pl.program_id(2)

        @pl.when(k == 0)
        def _():
            acc_ref[...] = jnp.zeros_like(acc_ref)

        acc_ref[...] += _dot(a_ref[...].astype(BF16), b_ref[...].astype(BF16), dims)

        @pl.when(k == nk - 1)
        def _():
            res = acc_ref[...]
            if r_ref is not None:
                res = res + r_ref[...]
            o_ref[...] = res.astype(o_ref.dtype)

    in_specs = [a_spec, b_spec]
    args = [a, b]
    if residual is not None:
        in_specs.append(r_spec)
        args.append(residual)
    return pl.pallas_call(
        body, name=name, grid=grid, in_specs=in_specs, out_specs=o_spec, out_shape=out_shape,
        scratch_shapes=[pltpu.VMEM(acc_shape, F32)],
        compiler_params=_params("parallel", "parallel", "arbitrary"),
    )(*args)


def _mm_nn(a, b, *, tm, tn, tk, out_dtype, name, residual=None):
    m, kd = a.shape
    n = b.shape[1]
    return _matmul(
        a, b, dims=NN, grid=(m // tm, n // tn, kd // tk),
        a_spec=pl.BlockSpec((tm, tk), lambda i, j, k: (i, k)),
        b_spec=pl.BlockSpec((tk, tn), lambda i, j, k: (k, j)),
        o_spec=pl.BlockSpec((tm, tn), lambda i, j, k: (i, j)),
        out_shape=jax.ShapeDtypeStruct((m, n), out_dtype), acc_shape=(tm, tn), name=name,
        residual=residual, r_spec=pl.BlockSpec((tm, tn), lambda i, j, k: (i, j)))


def _mm_nt(a, b, *, tm, tn, tk, out_dtype, name):
    m, kd = a.shape
    n = b.shape[0]
    return _matmul(
        a, b, dims=NT, grid=(m // tm, n // tn, kd // tk),
        a_spec=pl.BlockSpec((tm, tk), lambda i, j, k: (i, k)),
        b_spec=pl.BlockSpec((tn, tk), lambda i, j, k: (j, k)),
        o_spec=pl.BlockSpec((tm, tn), lambda i, j, k: (i, j)),
        out_shape=jax.ShapeDtypeStruct((m, n), out_dtype), acc_shape=(tm, tn), name=name)


def _mm_tn(a, b, *, tm, tn, tk, out_dtype, name):
    kd, m = a.shape
    n = b.shape[1]
    return _matmul(
        a, b, dims=TN, grid=(m // tm, n // tn, kd // tk),
        a_spec=pl.BlockSpec((tk, tm), lambda i, j, k: (k, i)),
        b_spec=pl.BlockSpec((tk, tn), lambda i, j, k: (k, j)),
        o_spec=pl.BlockSpec((tm, tn), lambda i, j, k: (i, j)),
        out_shape=jax.ShapeDtypeStruct((m, n), out_dtype), acc_shape=(tm, tn), name=name)


def _mm_up(h2, w_up_b, *, tm, name):
    s = h2.shape[0]
    nb = w_up_b.shape[2]
    per_half = D_FF // nb
    return _matmul(
        h2, w_up_b, dims=NN, grid=(s // tm, N_DEV, 1),
        a_spec=pl.BlockSpec((tm, D_MODEL), lambda i, j, k: (i, 0)),
        b_spec=pl.BlockSpec((None, D_MODEL, nb), lambda i, j, k: (j, 0, 0)),
        o_spec=pl.BlockSpec((None, tm, nb), lambda i, j, k: (j // per_half, i, j % per_half)),
        out_shape=jax.ShapeDtypeStruct((2, s, D_FF), F32), acc_shape=(tm, nb), name=name)


def _mm_up_bwd_x(dp, w_up_b, *, tm, name):
    s = dp.shape[1]
    nb = w_up_b.shape[2]
    per_half = D_FF // nb
    return _matmul(
        dp, w_up_b, dims=NT, grid=(s // tm, 1, N_DEV),
        a_spec=pl.BlockSpec((None, tm, nb), lambda i, j, k: (k // per_half, i, k % per_half)),
        b_spec=pl.BlockSpec((None, D_MODEL, nb), lambda i, j, k: (k, 0, 0)),
        o_spec=pl.BlockSpec((tm, D_MODEL), lambda i, j, k: (i, 0)),
        out_shape=jax.ShapeDtypeStruct((s, D_MODEL), F32), acc_shape=(tm, D_MODEL), name=name)


def _mm_up_bwd_w(h2, dp, *, tm, tk, name):
    s = h2.shape[0]
    nb = D_FF * 2 // N_DEV
    per_half = D_FF // nb
    return _matmul(
        h2, dp, dims=TN, grid=(D_MODEL // tm, N_DEV, s // tk),
        a_spec=pl.BlockSpec((tk, tm), lambda i, j, k: (k, i)),
        b_spec=pl.BlockSpec((None, tk, nb), lambda i, j, k: (j // per_half, k, j % per_half)),
        o_spec=pl.BlockSpec((None, tm, nb), lambda i, j, k: (j, i, 0)),
        out_shape=jax.ShapeDtypeStruct((N_DEV, D_MODEL, nb), BF16), acc_shape=(tm, nb), name=name)


def _norm_fwd(x, g, *, tm, name):
    s, d = x.shape

    def body(x_ref, g_ref, h_ref):
        xv = x_ref[...]
        h_ref[...] = (xv * _rms(xv) * g_ref[...]).astype(BF16)

    return pl.pallas_call(
        body, name=name, grid=(s // tm,),
        in_specs=[pl.BlockSpec((tm, d), lambda i: (i, 0)), pl.BlockSpec((1, d), lambda i: (0, 0))],
        out_specs=pl.BlockSpec((tm, d), lambda i: (i, 0)),
        out_shape=jax.ShapeDtypeStruct((s, d), BF16),
        compiler_params=_params("parallel"),
    )(x, g.reshape(1, d))


def _norm_bwd(dh, x, g, dres, *, tm, name):
    s, d = x.shape

    def body(dh_ref, x_ref, g_ref, dres_ref, dx_ref, dg_ref):
        @pl.when(pl.program_id(0) == 0)
        def _():
            dg_ref[...] = jnp.zeros_like(dg_ref)

        xv = x_ref[...]
        rstd = _rms(xv)
        n = xv * rstd
        dhv = dh_ref[...]
        dn = dhv * g_ref[...]
        dx = rstd * (dn - n * jnp.mean(dn * n, axis=-1, keepdims=True))
        dx_ref[...] = dres_ref[...] + dx
        dg_ref[0:1, :] += jnp.sum(dhv * n, axis=0, keepdims=True)

    return pl.pallas_call(
        body, name=name, grid=(s // tm,),
        in_specs=[pl.BlockSpec((tm, d), lambda i: (i, 0)), pl.BlockSpec((tm, d), lambda i: (i, 0)),
                  pl.BlockSpec((1, d), lambda i: (0, 0)), pl.BlockSpec((tm, d), lambda i: (i, 0))],
        out_specs=[pl.BlockSpec((tm, d), lambda i: (i, 0)), pl.BlockSpec((8, d), lambda i: (0, 0))],
        out_shape=[jax.ShapeDtypeStruct((s, d), F32), jax.ShapeDtypeStruct((8, d), F32)],
        compiler_params=_params("arbitrary"),
    )(dh, x, g.reshape(1, d), dres)


def _loss_head(x, g, tgt, *, tm, name):
    s, d = x.shape

    def body(x_ref, g_ref, t_ref, loss_ref, dx_ref, dg_ref):
        @pl.when(pl.program_id(0) == 0)
        def _():
            dg_ref[...] = jnp.zeros_like(dg_ref)
            loss_ref[...] = jnp.zeros_like(loss_ref)

        xv = x_ref[...]
        gv = g_ref[...]
        rstd = _rms(xv)
        n = xv * rstd
        e = n * gv - t_ref[...]
        part = 0.5 * jnp.sum(jnp.mean(e * e, axis=-1, keepdims=True), axis=0, keepdims=True)
        loss_ref[...] += jnp.broadcast_to(part, loss_ref.shape)
        dy = e * (1.0 / d)
        dn = dy * gv
        dx_ref[...] = rstd * (dn - n * jnp.mean(dn * n, axis=-1, keepdims=True))
        dg_ref[0:1, :] += jnp.sum(dy * n, axis=0, keepdims=True)

    return pl.pallas_call(
        body, name=name, grid=(s // tm,),
        in_specs=[pl.BlockSpec((tm, d), lambda i: (i, 0)), pl.BlockSpec((1, d), lambda i: (0, 0)),
                  pl.BlockSpec((tm, d), lambda i: (i, 0))],
        out_specs=[pl.BlockSpec((8, 128), lambda i: (0, 0)), pl.BlockSpec((tm, d), lambda i: (i, 0)),
                   pl.BlockSpec((8, d), lambda i: (0, 0))],
        out_shape=[jax.ShapeDtypeStruct((8, 128), F32), jax.ShapeDtypeStruct((s, d), F32),
                   jax.ShapeDtypeStruct((8, d), F32)],
        compiler_params=_params("arbitrary"),
    )(x, g.reshape(1, d), tgt)


def _scan_rows(a_ref, b_ref, h_ref, carry, *, rows, reverse):
    width = a_ref.shape[1]
    n_chunks = rows // 8
    row = lax.broadcasted_iota(jnp.int32, (8, width), 0)

    def step(ci, carry):
        chunk = (n_chunks - 1 - ci) if reverse else ci
        off = pl.multiple_of(chunk * 8, 8)
        av = a_ref[pl.ds(off, 8), :]
        bv = b_ref[pl.ds(off, 8), :]
        for sh in (1, 2, 4):
            if reverse:
                a_sh = pltpu.roll(av, 8 - sh, 0)
                b_sh = pltpu.roll(bv, 8 - sh, 0)
                m = row < 8 - sh
            else:
                a_sh = pltpu.roll(av, sh, 0)
                b_sh = pltpu.roll(bv, sh, 0)
                m = row >= sh
            bv = jnp.where(m, av * b_sh + bv, bv)
            av = jnp.where(m, av * a_sh, av)
        h = av * carry + bv
        h_ref[pl.ds(off, 8), :] = h
        return h[0:1, :] if reverse else h[7:8, :]

    return lax.fori_loop(0, n_chunks, step, carry)


def _lru_gates(lx, wa_ref, wx_ref, ba, bx, sp):
    lxb = lx.astype(BF16)
    pre_r = jnp.concatenate(
        [_dot(lxb[:, g * LRU_GROUP:(g + 1) * LRU_GROUP], wa_ref[g], NN) for g in range(N_GROUPS)], axis=1)
    pre_i = jnp.concatenate(
        [_dot(lxb[:, g * LRU_GROUP:(g + 1) * LRU_GROUP], wx_ref[g], NN) for g in range(N_GROUPS)], axis=1)
    r = _sigmoid(pre_r + ba)
    ig = _sigmoid(pre_i + bx)
    log_a = (-RG_C * r) * sp
    a = jnp.exp(log_a)
    mult = jnp.sqrt(-jnp.tanh(log_a) * (a * a + 1.0))
    return lxb, r, ig, a, mult


def _mixer_fwd(z, cw, cb, wa_bd, wx_bd, ba, bx, lam, scw, *, tile, name):
    s = z.shape[0]
    n_tiles = s // tile

    def body(z_ref, cw_ref, cb_ref, wa_ref, wx_ref, ba_ref, bx_ref, lam_ref, scw_ref,
             y_ref, hs_ref, ext_lx, ext_q, a_s, b_s, h_car):
        i = pl.program_id(0)

        @pl.when(i == 0)
        def _():
            ext_lx[0:HALO, :] = jnp.zeros((HALO, D_LRU), F32)
            ext_q[0:HALO, :] = jnp.zeros((HALO, D_SC), F32)
            h_car[...] = jnp.zeros_like(h_car)

        ext_lx[HALO:HALO + tile, :] = z_ref[:, 0:D_LRU]
        ext_q[HALO:HALO + tile, :] = z_ref[:, 2 * D_LRU + D_SC:2 * D_LRU + 2 * D_SC] * z_ref[:, 2 * D_LRU + 2 * D_SC:D_IN]
        lx = cb_ref[...] + cw_ref[0:1, :] * ext_lx[pl.ds(HALO - 3, tile), :]
        for k in range(1, 4):
            lx = lx + cw_ref[k:k + 1, :] * ext_lx[pl.ds(HALO - 3 + k, tile), :]
        cq = scw_ref[0:1, :] * ext_q[pl.ds(HALO - 2, tile), :]
        for k in range(1, 3):
            cq = cq + scw_ref[k:k + 1, :] * ext_q[pl.ds(HALO - 2 + k, tile), :]
        ext_lx[0:HALO, :] = ext_lx[tile:tile + HALO, :]
        ext_q[0:HALO, :] = ext_q[tile:tile + HALO, :]

        sp = _softplus(-lam_ref[...])
        _, _, ig, a, mult = _lru_gates(lx, wa_ref, wx_ref, ba_ref[...], bx_ref[...], sp)
        a_s[...] = a
        b_s[...] = mult * (ig * lx)
        h_car[0:1, :] = _scan_rows(a_s, b_s, hs_ref, h_car[0:1, :], rows=tile, reverse=False)

        y_ref[:, 0:D_LRU] = (hs_ref[...] * _gelu(z_ref[:, D_LRU:2 * D_LRU])).astype(BF16)
        y_ref[:, D_LRU:D_MIX] = (z_ref[:, 2 * D_LRU:2 * D_LRU + D_SC] * cq).astype(BF16)

    full = lambda shape: pl.BlockSpec(shape, lambda i: (0,) * len(shape))
    return pl.pallas_call(
        body, name=name, grid=(n_tiles,),
        in_specs=[pl.BlockSpec((tile, D_IN), lambda i: (i, 0)),
                  full((4, D_LRU)), full((1, D_LRU)),
                  full((N_GROUPS, LRU_GROUP, LRU_GROUP)), full((N_GROUPS, LRU_GROUP, LRU_GROUP)),
                  full((1, D_LRU)), full((1, D_LRU)), full((1, D_LRU)), full((3, D_SC))],
        out_specs=[pl.BlockSpec((tile, D_MIX), lambda i: (i, 0)), pl.BlockSpec((tile, D_LRU), lambda i: (i, 0))],
        out_shape=[jax.ShapeDtypeStruct((s, D_MIX), BF16), jax.ShapeDtypeStruct((s, D_LRU), F32)],
        scratch_shapes=[pltpu.VMEM((tile + HALO, D_LRU), F32), pltpu.VMEM((tile + HALO, D_SC), F32),
                        pltpu.VMEM((tile, D_LRU), F32), pltpu.VMEM((tile, D_LRU), F32),
                        pltpu.VMEM((8, D_LRU), F32)],
        compiler_params=_params("arbitrary"),
    )(z, cw, cb.reshape(1, -1), wa_bd, wx_bd, ba.reshape(1, -1), bx.reshape(1, -1), lam.reshape(1, -1), scw)


def _mixer_bwd(z, hs, dy, cw, cb, wa_bd, wx_bd, ba, bx, lam, scw, *, tile, name):
    s = z.shape[0]
    n_tiles = s // tile
    per8 = tile // 8

    def body(z_ref, zp_ref, hs_ref, hsp_ref, dy_ref, cw_ref, cb_ref, wa_ref, wx_ref, ba_ref, bx_ref, lam_ref, scw_ref,
             dz_ref, dcw_ref, dvec_ref, dwa_ref, dwx_ref, dscw_ref,
             ext_lx, ext_q, ext_h, ext_a, ext_dlx, ext_dcq, a_s, b_s, lam_s, l_car):
        i = pl.program_id(0)
        first_tile = i == n_tiles - 1

        @pl.when(i == 0)
        def _():
            for ref in (dcw_ref, dvec_ref, dwa_ref, dwx_ref, dscw_ref, l_car):
                ref[...] = jnp.zeros_like(ref)
            ext_a[tile:tile + HALO, :] = jnp.zeros((HALO, D_LRU), F32)
            ext_dlx[tile:tile + HALO, :] = jnp.zeros((HALO, D_LRU), F32)
            ext_dcq[tile:tile + HALO, :] = jnp.zeros((HALO, D_SC), F32)

        keep = jnp.where(first_tile, 0.0, 1.0)
        sb = z_ref[:, 2 * D_LRU:2 * D_LRU + D_SC]
        sc = z_ref[:, 2 * D_LRU + D_SC:2 * D_LRU + 2 * D_SC]
        sx = z_ref[:, 2 * D_LRU + 2 * D_SC:D_IN]
        ext_lx[0:HALO, :] = zp_ref[:, 0:D_LRU] * keep
        ext_lx[HALO:HALO + tile, :] = z_ref[:, 0:D_LRU]
        ext_q[0:HALO, :] = zp_ref[:, 2 * D_LRU + D_SC:2 * D_LRU + 2 * D_SC] * zp_ref[:, 2 * D_LRU + 2 * D_SC:D_IN] * keep
        ext_q[HALO:HALO + tile, :] = sc * sx
        ext_h[0:HALO, :] = hsp_ref[...] * keep
        ext_h[HALO:HALO + tile, :] = hs_ref[...]

        lx = cb_ref[...] + cw_ref[0:1, :] * ext_lx[pl.ds(HALO - 3, tile), :]
        for k in range(1, 4):
            lx = lx + cw_ref[k:k + 1, :] * ext_lx[pl.ds(HALO - 3 + k, tile), :]
        cq = scw_ref[0:1, :] * ext_q[pl.ds(HALO - 2, tile), :]
        for k in range(1, 3):
            cq = cq + scw_ref[k:k + 1, :] * ext_q[pl.ds(HALO - 2 + k, tile), :]

        sp = _softplus(-lam_ref[...])
        lxb, r, ig, a, mult = _lru_gates(lx, wa_ref, wx_ref, ba_ref[...], bx_ref[...], sp)

        ge, dge = _gelu_parts(z_ref[:, D_LRU:2 * D_LRU])
        dy_lru = dy_ref[:, 0:D_LRU]
        dz_ref[:, D_LRU:2 * D_LRU] = (dy_lru * hs_ref[...] * dge).astype(BF16)

        ext_a[0:tile, :] = a
        a_s[...] = ext_a[pl.ds(1, tile), :]
        b_s[...] = dy_lru * ge
        l_car[0:1, :] = _scan_rows(a_s, b_s, lam_s, l_car[0:1, :], rows=tile, reverse=True)
        ext_a[tile:tile + HALO, :] = ext_a[0:HALO, :]
        lv = lam_s[...]

        da = lv * ext_h[pl.ds(HALO - 1, tile), :]
        d_mult = lv * ig * lx
        d_i = lv * mult * lx
        dlx = lv * mult * ig
        dlog_a = da * a - d_mult * (a * a) / mult
        d_r = dlog_a * (-RG_C * sp)
        dpre_r = d_r * r * (1.0 - r)
        dpre_i = d_i * ig * (1.0 - ig)
        dvec_ref[1:2, :] += jnp.sum(dpre_r, axis=0, keepdims=True)
        dvec_ref[2:3, :] += jnp.sum(dpre_i, axis=0, keepdims=True)
        dvec_ref[3:4, :] += jnp.sum(dlog_a * (-RG_C * r), axis=0, keepdims=True)
        dpr_b = dpre_r.astype(BF16)
        dpi_b = dpre_i.astype(BF16)
        back = []
        for g in range(N_GROUPS):
            cols = slice(g * LRU_GROUP, (g + 1) * LRU_GROUP)
            dwa_ref[g] += _dot(lxb[:, cols], dpr_b[:, cols], TN)
            dwx_ref[g] += _dot(lxb[:, cols], dpi_b[:, cols], TN)
            back.append(_dot(dpr_b[:, cols], wa_ref[g], NT) + _dot(dpi_b[:, cols], wx_ref[g], NT))
        dlx = dlx + jnp.concatenate(back, axis=1)
        dvec_ref[0:1, :] += jnp.sum(dlx, axis=0, keepdims=True)

        ext_dlx[0:tile, :] = dlx
        for k in range(4):
            dcw_ref[k:k + 1, :] += jnp.sum(dlx * ext_lx[pl.ds(HALO - 3 + k, tile), :], axis=0, keepdims=True)
        dlxp = cw_ref[3:4, :] * dlx
        for k in range(3):
            dlxp = dlxp + cw_ref[k:k + 1, :] * ext_dlx[pl.ds(3 - k, tile), :]
        dz_ref[:, 0:D_LRU] = dlxp.astype(BF16)
        ext_dlx[tile:tile + HALO, :] = ext_dlx[0:HALO, :]

        dy_sc = dy_ref[:, D_LRU:D_MIX]
        dz_ref[:, 2 * D_LRU:2 * D_LRU + D_SC] = (dy_sc * cq).astype(BF16)
        dcq = dy_sc * sb
        ext_dcq[0:tile, :] = dcq
        for k in range(3):
            dscw_ref[k:k + 1, :] += jnp.sum(dcq * ext_q[pl.ds(HALO - 2 + k, tile), :], axis=0, keepdims=True)
        dq = scw_ref[2:3, :] * dcq
        for k in range(2):
            dq = dq + scw_ref[k:k + 1, :] * ext_dcq[pl.ds(2 - k, tile), :]
        dz_ref[:, 2 * D_LRU + D_SC:2 * D_LRU + 2 * D_SC] = (dq * sx).astype(BF16)
        dz_ref[:, 2 * D_LRU + 2 * D_SC:D_IN] = (dq * sc).astype(BF16)
        ext_dcq[tile:tile + HALO, :] = ext_dcq[0:HALO, :]

        @pl.when(i == n_tiles - 1)
        def _():
            dvec_ref[3:4, :] = dvec_ref[3:4, :] * (-_sigmoid(-lam_ref[...]))

    rev = lambda i: n_tiles - 1 - i
    prev8 = lambda i: jnp.maximum(rev(i) * per8 - 1, 0)
    full = lambda shape: pl.BlockSpec(shape, lambda i: (0,) * len(shape))
    return pl.pallas_call(
        body, name=name, grid=(n_tiles,),
        in_specs=[pl.BlockSpec((tile, D_IN), lambda i: (rev(i), 0)),
                  pl.BlockSpec((HALO, D_IN), lambda i: (prev8(i), 0)),
                  pl.BlockSpec((tile, D_LRU), lambda i: (rev(i), 0)),
                  pl.BlockSpec((HALO, D_LRU), lambda i: (prev8(i), 0)),
                  pl.BlockSpec((tile, D_MIX), lambda i: (rev(i), 0)),
                  full((4, D_LRU)), full((1, D_LRU)),
                  full((N_GROUPS, LRU_GROUP, LRU_GROUP)), full((N_GROUPS, LRU_GROUP, LRU_GROUP)),
                  full((1, D_LRU)), full((1, D_LRU)), full((1, D_LRU)), full((3, D_SC))],
        out_specs=[pl.BlockSpec((tile, D_IN), lambda i: (rev(i), 0)),
                   full((8, D_LRU)), full((8, D_LRU)),
                   full((N_GROUPS, LRU_GROUP, LRU_GROUP)), full((N_GROUPS, LRU_GROUP, LRU_GROUP)),
                   full((8, D_SC))],
        out_shape=[jax.ShapeDtypeStruct((s, D_IN), BF16),
                   jax.ShapeDtypeStruct((8, D_LRU), F32), jax.ShapeDtypeStruct((8, D_LRU), F32),
                   jax.ShapeDtypeStruct((N_GROUPS, LRU_GROUP, LRU_GROUP), F32),
                   jax.ShapeDtypeStruct((N_GROUPS, LRU_GROUP, LRU_GROUP), F32),
                   jax.ShapeDtypeStruct((8, D_SC), F32)],
        scratch_shapes=[pltpu.VMEM((tile + HALO, D_LRU), F32), pltpu.VMEM((tile + HALO, D_SC), F32),
                        pltpu.VMEM((tile + HALO, D_LRU), F32), pltpu.VMEM((tile + HALO, D_LRU), F32),
                        pltpu.VMEM((tile + HALO, D_LRU), F32), pltpu.VMEM((tile + HALO, D_SC), F32),
                        pltpu.VMEM((tile, D_LRU), F32), pltpu.VMEM((tile, D_LRU), F32),
                        pltpu.VMEM((tile, D_LRU), F32), pltpu.VMEM((8, D_LRU), F32)],
        compiler_params=_params("arbitrary"),
    )(z, z, hs, hs, dy, cw, cb.reshape(1, -1), wa_bd, wx_bd, ba.reshape(1, -1), bx.reshape(1, -1),
      lam.reshape(1, -1), scw)


def _ffn_fwd(p, fcw, *, tile, tc, name):
    s = p.shape[1]
    per8 = tile // 8

    def body(p_ref, pp_ref, w_ref, act_ref, ext_p):
        i = pl.program_id(0)
        keep = jnp.where(i == 0, 0.0, 1.0)
        ext_p[:, 0:HALO, :] = pp_ref[...] * keep
        ext_p[:, HALO:HALO + tile, :] = p_ref[...]
        u = []
        for half in range(2):
            acc = w_ref[half, 0:1, :] * ext_p[half, pl.ds(HALO - 2, tile), :]
            for k in range(1, 3):
                acc = acc + w_ref[half, k:k + 1, :] * ext_p[half, pl.ds(HALO - 2 + k, tile), :]
            u.append(acc)
        act_ref[...] = (_gelu(u[0]) * u[1]).astype(BF16)

    return pl.pallas_call(
        body, name=name, grid=(s // tile, D_FF // tc),
        in_specs=[pl.BlockSpec((2, tile, tc), lambda i, j: (0, i, j)),
                  pl.BlockSpec((2, HALO, tc), lambda i, j: (0, jnp.maximum(i * per8 - 1, 0), j)),
                  pl.BlockSpec((2, 3, tc), lambda i, j: (0, 0, j))],
        out_specs=pl.BlockSpec((tile, tc), lambda i, j: (i, j)),
        out_shape=jax.ShapeDtypeStruct((s, D_FF), BF16),
        scratch_shapes=[pltpu.VMEM((2, tile + HALO, tc), F32)],
        compiler_params=_params("parallel", "parallel"),
    )(p, p, fcw)


def _ffn_bwd(p, dact, fcw, *, tile, tc, name):
    s = p.shape[1]
    n_tiles = s // tile
    per8 = tile // 8
    ext_rows = tile + HALO

    def body(p_ref, pp_ref, pn_ref, da_ref, dan_ref, w_ref, dp_ref, dw_ref, ext_p, ext_du):
        i = pl.program_id(1)

        @pl.when(i == 0)
        def _():
            dw_ref[...] = jnp.zeros_like(dw_ref)

        keep_prev = jnp.where(i == 0, 0.0, 1.0)
        keep_next = jnp.where(i == n_tiles - 1, 0.0, 1.0)
        ext_p[:, 0:HALO, :] = pp_ref[...] * keep_prev
        ext_p[:, HALO:HALO + tile, :] = p_ref[...]
        ext_p[:, HALO + tile:2 * HALO + tile, :] = pn_ref[...] * keep_next
        u = []
        for half in range(2):
            acc = w_ref[half, 0:1, :] * ext_p[half, pl.ds(HALO - 2, ext_rows), :]
            for k in range(1, 3):
                acc = acc + w_ref[half, k:k + 1, :] * ext_p[half, pl.ds(HALO - 2 + k, ext_rows), :]
            u.append(acc)
        ge, dge = _gelu_parts(u[0])
        da = jnp.concatenate([da_ref[...], dan_ref[...] * keep_next], axis=0)
        ext_du[0, :, :] = da * u[1] * dge
        ext_du[1, :, :] = da * ge
        for half in range(2):
            du = ext_du[half, 0:tile, :]
            acc = w_ref[half, 2:3, :] * du
            for k in range(2):
                acc = acc + w_ref[half, k:k + 1, :] * ext_du[half, pl.ds(2 - k, tile), :]
            dp_ref[half, :, :] = acc.astype(BF16)
            for k in range(3):
                dw_ref[half, k:k + 1, :] += jnp.sum(du * ext_p[half, pl.ds(HALO - 2 + k, tile), :], axis=0, keepdims=True)

    return pl.pallas_call(
        body, name=name, grid=(D_FF // tc, n_tiles),
        in_specs=[pl.BlockSpec((2, tile, tc), lambda j, i: (0, i, j)),
                  pl.BlockSpec((2, HALO, tc), lambda j, i: (0, jnp.maximum(i * per8 - 1, 0), j)),
                  pl.BlockSpec((2, HALO, tc), lambda j, i: (0, jnp.minimum((i + 1) * per8, n_tiles * per8 - 1), j)),
                  pl.BlockSpec((tile, tc), lambda j, i: (i, j)),
                  pl.BlockSpec((HALO, tc), lambda j, i: (jnp.minimum((i + 1) * per8, n_tiles * per8 - 1), j)),
                  pl.BlockSpec((2, 3, tc), lambda j, i: (0, 0, j))],
        out_specs=[pl.BlockSpec((2, tile, tc), lambda j, i: (0, i, j)),
                   pl.BlockSpec((2, 8, tc), lambda j, i: (0, 0, j))],
        out_shape=[jax.ShapeDtypeStruct((2, s, D_FF), BF16), jax.ShapeDtypeStruct((2, 8, D_FF), F32)],
        scratch_shapes=[pltpu.VMEM((2, tile + 2 * HALO, tc), F32), pltpu.VMEM((2, ext_rows, tc), F32)],
        compiler_params=_params("parallel", "arbitrary"),
    )(p, p, p, dact, dact, fcw)


def _adamw_math(w, g, m, v):
    m = ADAM_B1 * m + (1.0 - ADAM_B1) * g
    v = ADAM_B2 * v + (1.0 - ADAM_B2) * (g * g)
    m_hat = m / (1.0 - ADAM_B1 ** ADAM_STEP)
    v_hat = v / (1.0 - ADAM_B2 ** ADAM_STEP)
    delta = -ADAM_LR * (m_hat / (jnp.sqrt(v_hat) + ADAM_EPS) + ADAM_WD * w)
    return delta, m, v


def _adamw(w, g, m, v, *, name):
    rows, cols = w.shape
    tr = rows
    for cand in (512, 256, 128, 64, 32, 16, 8):
        if rows % cand == 0 and rows > cand:
            tr = cand
            break

    def body(w_ref, g_ref, m_ref, v_ref, d_ref, nm_ref, nv_ref):
        d, nm, nv = _adamw_math(w_ref[...], g_ref[...], m_ref[...], v_ref[...])
        d_ref[...] = d
        nm_ref[...] = nm
        nv_ref[...] = nv

    spec = pl.BlockSpec((tr, cols), lambda i: (i, 0))
    return pl.pallas_call(
        body, name=name, grid=(rows // tr,), in_specs=[spec] * 4, out_specs=[spec] * 3,
        out_shape=[jax.ShapeDtypeStruct((rows, cols), F32)] * 3,
        compiler_params=_params("parallel"),
    )(w, g, m, v)


def _sum_parts(parts, *, name):
    _, rows, cols = parts.shape
    tr = rows
    for cand in (256, 128, 64, 32, 16):
        if rows % cand == 0 and rows > cand:
            tr = cand
            break

    def body(p_ref, o_ref):
        acc = p_ref[0].astype(F32)
        for d in range(1, N_DEV):
            acc = acc + p_ref[d].astype(F32)
        o_ref[...] = acc

    return pl.pallas_call(
        body, name=name, grid=(rows // tr,),
        in_specs=[pl.BlockSpec((N_DEV, tr, cols), lambda i: (0, i, 0))],
        out_specs=pl.BlockSpec((tr, cols), lambda i: (i, 0)),
        out_shape=jax.ShapeDtypeStruct((rows, cols), F32),
        compiler_params=_params("parallel"),
    )(parts)


def _place():
    return lax.axis_index("x"), lax.axis_index("y"), lax.axis_index("c")


def _flip(v, bit):
    return 1 - v if bit else v


def _all_gather(shards, *, name):
    n = len(shards)

    def body(*refs):
        ins, outs = refs[:n], refs[n:2 * n]
        send_sems, recv_sems, local_sems = refs[2 * n:]
        x, y, c = _place()
        me, sibling = (x, y, c), (x, y, 1 - c)
        chips = [(1 - x, y), (x, 1 - y), (1 - x, 1 - y)]

        def slot(px, py, pc):
            return 4 * px + 2 * py + pc

        def copy(t, k, block, to, src=None):
            dst = outs[t].at[slot(*block)]
            return pltpu.make_async_remote_copy(
                src_ref=dst if src is None else src, dst_ref=dst,
                send_sem=send_sems.at[t, k], recv_sem=recv_sems.at[t, k],
                device_id=to, device_id_type=MESH)

        mine = [pltpu.make_async_copy(ins[t], outs[t].at[slot(*me)], local_sems.at[t]) for t in range(n)]
        for cp in mine:
            cp.start()
        first = []
        for t in range(n):
            first.append(copy(t, 0, me, sibling, src=ins[t]))
            first += [copy(t, 1 + j, me, (*chip, c), src=ins[t]) for j, chip in enumerate(chips)]
        for cp in first:
            cp.start()
        passed = []
        for j, chip in enumerate(chips):
            for t in range(n):
                copy(t, 1 + j, (*chip, c), me).wait_recv()
                fwd = copy(t, 4 + j, (*chip, c), sibling)
                fwd.start()
                passed.append(fwd)
        for t in range(n):
            copy(t, 0, sibling, me).wait_recv()
            for j, chip in enumerate(chips):
                copy(t, 4 + j, (*chip, 1 - c), me).wait_recv()
        for cp in first + passed:
            cp.wait_send()
        for cp in mine:
            cp.wait()

    any_spec = pl.BlockSpec(memory_space=pl.ANY)
    return pl.pallas_call(
        body, name=name,
        in_specs=[any_spec] * n, out_specs=[any_spec] * n,
        out_shape=[jax.ShapeDtypeStruct((N_DEV,) + sh.shape, sh.dtype) for sh in shards],
        scratch_shapes=[pltpu.SemaphoreType.DMA((n, 7)), pltpu.SemaphoreType.DMA((n, 7)),
                        pltpu.SemaphoreType.DMA((n,))],
    )(*shards)


def _exchange_blocks(grads, *, name):
    n = len(grads)

    def body(*refs):
        ins, outs = refs[:n], refs[n:2 * n]
        send_sems, recv_sems, local_sems = refs[2 * n:]
        x, y, c = _place()
        my_id = 4 * x + 2 * y + c
        mine = [pltpu.make_async_copy(ins[t].at[my_id], outs[t].at[my_id], local_sems.at[t]) for t in range(n)]
        for cp in mine:
            cp.start()
        copies = []
        for k in range(1, N_DEV):
            px, py, pc = _flip(x, k & 4), _flip(y, k & 2), _flip(c, k & 1)
            peer_id = 4 * px + 2 * py + pc
            for t in range(n):
                copies.append(pltpu.make_async_remote_copy(
                    src_ref=ins[t].at[peer_id], dst_ref=outs[t].at[my_id],
                    send_sem=send_sems.at[t, k - 1], recv_sem=recv_sems.at[t, k - 1],
                    device_id=(px, py, pc), device_id_type=MESH))
        for cp in copies:
            cp.start()
        for cp in copies:
            cp.wait()
        for cp in mine:
            cp.wait()

    any_spec = pl.BlockSpec(memory_space=pl.ANY)
    return pl.pallas_call(
        body, name=name,
        in_specs=[any_spec] * n, out_specs=[any_spec] * n,
        out_shape=[jax.ShapeDtypeStruct(g.shape, g.dtype) for g in grads],
        scratch_shapes=[pltpu.SemaphoreType.DMA((n, 7)), pltpu.SemaphoreType.DMA((n, 7)),
                        pltpu.SemaphoreType.DMA((n,))],
    )(*grads)


def _all_reduce_small(buf, *, name):
    _, rows, lanes = buf.shape

    def body(in_ref, out_ref, parts, send_sems, recv_sems):
        x, y, c = _place()
        my_id = 4 * x + 2 * y + c
        peers = []
        for k in range(1, N_DEV):
            px, py, pc = _flip(x, k & 4), _flip(y, k & 2), _flip(c, k & 1)
            peers.append(((px, py, pc), 4 * px + 2 * py + pc))
        scatter = [pltpu.make_async_remote_copy(
            src_ref=in_ref.at[pid], dst_ref=parts.at[my_id],
            send_sem=send_sems.at[0, k], recv_sem=recv_sems.at[0, k],
            device_id=peer, device_id_type=MESH) for k, (peer, pid) in enumerate(peers)]
        for cp in scatter:
            cp.start()
        parts[my_id] = in_ref[my_id]
        for cp in scatter:
            cp.wait()
        total = parts[0]
        for d in range(1, N_DEV):
            total = total + parts[d]
        out_ref[my_id] = total
        gather = [pltpu.make_async_remote_copy(
            src_ref=out_ref.at[my_id], dst_ref=out_ref.at[my_id],
            send_sem=send_sems.at[1, k], recv_sem=recv_sems.at[1, k],
            device_id=peer, device_id_type=MESH) for k, (peer, pid) in enumerate(peers)]
        for cp in gather:
            cp.start()
        for k, (peer, pid) in enumerate(peers):
            pltpu.make_async_remote_copy(
                src_ref=out_ref.at[pid], dst_ref=out_ref.at[pid],
                send_sem=send_sems.at[1, k], recv_sem=recv_sems.at[1, k],
                device_id=peer, device_id_type=MESH).wait()

    vmem = pl.BlockSpec(memory_space=pltpu.VMEM)
    return pl.pallas_call(
        body, name=name, in_specs=[vmem], out_specs=vmem,
        out_shape=jax.ShapeDtypeStruct(buf.shape, F32),
        scratch_shapes=[pltpu.VMEM(buf.shape, F32),
                        pltpu.SemaphoreType.DMA((2, 7)), pltpu.SemaphoreType.DMA((2, 7))],
        compiler_params=pltpu.CompilerParams(vmem_limit_bytes=VMEM_LIMIT),
    )(buf)


TM = 512
MIX_TILE = 128
FFN_TILE = 256
FFN_TC = 512


def _block_diag(w):
    wg = w.reshape(N_GROUPS, HEADS_PER_GROUP, LRU_HEAD_DIM, LRU_HEAD_DIM)
    eye = jnp.eye(HEADS_PER_GROUP, dtype=w.dtype)
    bd = wg[:, :, :, None, :] * eye[None, :, None, :, None]
    return bd.reshape(N_GROUPS, LRU_GROUP, LRU_GROUP).astype(BF16)


def _head_blocks(bd):
    b5 = bd.reshape(N_GROUPS, HEADS_PER_GROUP, LRU_HEAD_DIM, HEADS_PER_GROUP, LRU_HEAD_DIM)
    blocks = [b5[:, h, :, h, :] for h in range(HEADS_PER_GROUP)]
    return jnp.stack(blocks, axis=1).reshape(LRU_HEADS, LRU_HEAD_DIM, LRU_HEAD_DIM)


def _layer_fwd(x, lw, tag):
    h1 = _norm_fwd(x, lw["g1"], tm=TM, name=f"norm1_fwd_{tag}")
    z = _mm_nt(h1, lw["w_in_t"], tm=TM, tn=512, tk=D_MODEL, out_dtype=F32, name=f"in_proj_{tag}")
    y_mix, hs = _mixer_fwd(z, lw["cw"], lw["cb"], lw["wa_bd"], lw["wx_bd"], lw["ba"], lw["bx"], lw["lam"],
                           lw["scw"], tile=MIX_TILE, name=f"mixer_fwd_{tag}")
    x2 = _mm_nn(y_mix, lw["w_out"], tm=TM, tn=D_MODEL, tk=D_MIX, out_dtype=F32, name=f"out_proj_{tag}", residual=x)
    h2 = _norm_fwd(x2, lw["g2"], tm=TM, name=f"norm2_fwd_{tag}")
    p = _mm_up(h2, lw["w_up_b"], tm=TM, name=f"up_proj_{tag}")
    act = _ffn_fwd(p, lw["fcw"], tile=FFN_TILE, tc=FFN_TC, name=f"ffn_fwd_{tag}")
    x3 = _mm_nn(act, lw["w_down"], tm=TM, tn=D_MODEL, tk=1024, out_dtype=F32, name=f"down_proj_{tag}", residual=x2)
    saved = dict(x=x, h1=h1, z=z, y_mix=y_mix, hs=hs, x2=x2, h2=h2, p=p, act=act)
    return x3, saved


def _layer_bwd(dx3, lw, sv, tag):
    dact = _mm_nt(dx3, lw["w_down"], tm=TM, tn=512, tk=D_MODEL, out_dtype=F32, name=f"down_bwd_x_{tag}")
    g_down = _mm_tn(sv["act"], dx3, tm=512, tn=D_MODEL, tk=512, out_dtype=BF16, name=f"down_bwd_w_{tag}")
    dp, dfcw = _ffn_bwd(sv["p"], dact, lw["fcw"], tile=FFN_TILE, tc=FFN_TC, name=f"ffn_bwd_{tag}")
    dh2 = _mm_up_bwd_x(dp, lw["w_up_b"], tm=TM, name=f"up_bwd_x_{tag}")
    g_up = _mm_up_bwd_w(sv["h2"], dp, tm=512, tk=512, name=f"up_bwd_w_{tag}")
    dx2, dg2 = _norm_bwd(dh2, sv["x2"], lw["g2"], dx3, tm=TM, name=f"norm2_bwd_{tag}")
    dy = _mm_nt(dx2, lw["w_out"], tm=TM, tn=512, tk=D_MODEL, out_dtype=F32, name=f"out_bwd_x_{tag}")
    g_out = _mm_tn(sv["y_mix"], dx2, tm=512, tn=D_MODEL, tk=512, out_dtype=BF16, name=f"out_bwd_w_{tag}")
    dz, dcw, dvec, dwa, dwx, dscw = _mixer_bwd(
        sv["z"], sv["hs"], dy, lw["cw"], lw["cb"], lw["wa_bd"], lw["wx_bd"], lw["ba"], lw["bx"], lw["lam"],
        lw["scw"], tile=MIX_TILE, name=f"mixer_bwd_{tag}")
    dh1 = _mm_nn(dz, lw["w_in_t"], tm=TM, tn=D_MODEL, tk=512, out_dtype=F32, name=f"in_bwd_x_{tag}")
    g_in_t = _mm_tn(dz, sv["h1"], tm=512, tn=D_MODEL, tk=512, out_dtype=BF16, name=f"in_bwd_w_{tag}")
    dx, dg1 = _norm_bwd(dh1, sv["x"], lw["g1"], dx2, tm=TM, name=f"norm1_bwd_{tag}")
    big = dict(w_in_t=g_in_t, w_out=g_out, w_up_b=g_up, w_down=g_down)
    small = dict(norm1_g=dg1[0], lru_conv_w=dcw[0:4], lru_conv_b=dvec[0], lru_wa=_head_blocks(dwa),
                 lru_ba=dvec[1], lru_wx=_head_blocks(dwx), lru_bx=dvec[2], lru_lambda=dvec[3],
                 sc_conv_w=dscw[0:3], norm2_g=dg2[0], ffn_conv_w=dfcw[:, 0:3, :])
    return dx, big, small


SMALL_ORDER = ("norm1_g", "lru_conv_w", "lru_conv_b", "lru_wa", "lru_ba", "lru_wx", "lru_bx", "lru_lambda",
               "sc_conv_w", "norm2_g", "ffn_conv_w")


def _local_step(x, tgt, layers, final_g):
    saved = []
    h = x
    for l in range(DEPTH):
        h, sv = _layer_fwd(h, layers[l], f"l{l}")
        saved.append(sv)
    loss_blk, dx, dgf = _loss_head(h, final_g, tgt, tm=TM, name="loss_head")
    bigs, smalls = [None] * DEPTH, [None] * DEPTH
    for l in reversed(range(DEPTH)):
        dx, bigs[l], smalls[l] = _layer_bwd(dx, layers[l], saved[l], f"l{l}")
    return loss_blk[0, 0], dx, bigs, smalls, dgf[0]


def kernel(x, norm1_g, w_in, lru_conv_w, lru_conv_b, lru_wa, lru_ba, lru_wx, lru_bx, lru_lambda, sc_conv_w, w_out, norm2_g, w_up, ffn_conv_w, w_down, final_g, loss_target, m_norm1_g, m_w_in, m_lru_conv_w, m_lru_conv_b, m_lru_wa, m_lru_ba, m_lru_wx, m_lru_bx, m_lru_lambda, m_sc_conv_w, m_w_out, m_norm2_g, m_w_up, m_ffn_conv_w, m_w_down, m_final_g, v_norm1_g, v_w_in, v_lru_conv_w, v_lru_conv_b, v_lru_wa, v_lru_ba, v_lru_wx, v_lru_bx, v_lru_lambda, v_sc_conv_w, v_w_out, v_norm2_g, v_w_up, v_ffn_conv_w, v_w_down, v_final_g):
    names = ["norm1_g", "w_in", "lru_conv_w", "lru_conv_b", "lru_wa", "lru_ba", "lru_wx", "lru_bx", "lru_lambda",
             "sc_conv_w", "w_out", "norm2_g", "w_up", "ffn_conv_w", "w_down", "final_g"]
    w = dict(zip(names, [norm1_g, w_in, lru_conv_w, lru_conv_b, lru_wa, lru_ba, lru_wx, lru_bx, lru_lambda,
                         sc_conv_w, w_out, norm2_g, w_up, ffn_conv_w, w_down, final_g]))
    m = dict(zip(names, [m_norm1_g, m_w_in, m_lru_conv_w, m_lru_conv_b, m_lru_wa, m_lru_ba, m_lru_wx, m_lru_bx,
                         m_lru_lambda, m_sc_conv_w, m_w_out, m_norm2_g, m_w_up, m_ffn_conv_w, m_w_down, m_final_g]))
    v = dict(zip(names, [v_norm1_g, v_w_in, v_lru_conv_w, v_lru_conv_b, v_lru_wa, v_lru_ba, v_lru_wx, v_lru_bx,
                         v_lru_lambda, v_sc_conv_w, v_w_out, v_norm2_g, v_w_up, v_ffn_conv_w, v_w_down, v_final_g]))
    my_id = 4 * lax.axis_index("x") + 2 * lax.axis_index("y") + lax.axis_index("c")

    big_shards = []
    for l in range(DEPTH):
        big_shards += [jnp.swapaxes(w_in[l], 0, 1).astype(BF16), w_out[l].astype(BF16),
                       w_up[l].astype(BF16), w_down[l].astype(BF16)]
    taps = jnp.zeros((DEPTH, 16, 768), F32)
    taps = taps.at[:, 0:4, 0:128].set(lru_conv_w).at[:, 4:7, 0:64].set(sc_conv_w).at[:, 8:11, :].set(ffn_conv_w)
    gathered = _all_gather(big_shards + [taps.reshape(DEPTH * 16, 768)], name="gather_weights")
    taps_all = gathered[-1].reshape(N_DEV, DEPTH, 16, 768)

    layers = []
    for l in range(DEPTH):
        g_in_t, g_out, g_up, g_down = gathered[4 * l:4 * l + 4]
        tl = taps_all[:, l]
        cw = jnp.transpose(tl[:, 0:4, 0:128], (1, 0, 2)).reshape(4, D_LRU)
        scw = jnp.transpose(tl[:, 4:7, 0:64], (1, 0, 2)).reshape(3, D_SC)
        fcw = jnp.transpose(tl[:, 8:11, :], (1, 0, 2)).reshape(3, 2, D_FF).transpose(1, 0, 2)
        layers.append(dict(
            g1=norm1_g[l], g2=norm2_g[l], cb=lru_conv_b[l], ba=lru_ba[l], bx=lru_bx[l], lam=lru_lambda[l],
            wa_bd=_block_diag(lru_wa[l]), wx_bd=_block_diag(lru_wx[l]),
            cw=cw, scw=scw, fcw=fcw,
            w_in_t=g_in_t.reshape(D_IN, D_MODEL), w_out=g_out.reshape(D_MIX, D_MODEL),
            w_up_b=g_up, w_down=g_down.reshape(D_FF, D_MODEL)))

    loss_local, dx, bigs, smalls, dgf = _local_step(x[0], loss_target[0], layers, final_g)
    loss = lax.psum(loss_local, ("x", "y", "c"))

    send = []
    for l in range(DEPTH):
        b = bigs[l]
        send += [b["w_in_t"].reshape(N_DEV, D_IN // N_DEV, D_MODEL), b["w_out"].reshape(N_DEV, D_MIX // N_DEV, D_MODEL),
                 b["w_up_b"], b["w_down"].reshape(N_DEV, D_FF // N_DEV, D_MODEL)]
    parts = _exchange_blocks(send, name="scatter_grads")

    flat = [smalls[l][k].reshape(-1) for l in range(DEPTH) for k in SMALL_ORDER] + [dgf.reshape(-1)]
    sizes = [f.shape[0] for f in flat]
    total = sum(sizes)
    rows = -(-total // (N_DEV * 128 * 8)) * 8
    flat.append(jnp.zeros((N_DEV * rows * 128 - total,), F32))
    small_sum = _all_reduce_small(jnp.concatenate(flat).reshape(N_DEV, rows, 128), name="reduce_small").reshape(-1)
    small_g, off = [], 0
    for sz in sizes:
        small_g.append(small_sum[off:off + sz])
        off += sz
    gs = {}
    for l in range(DEPTH):
        for i, k in enumerate(SMALL_ORDER):
            gs.setdefault(k, []).append(small_g[l * len(SMALL_ORDER) + i])
    g_final = small_g[-1]

    grads = {}
    per_layer = {k: [] for k in ("w_in", "w_out", "w_up", "w_down")}
    for l in range(DEPTH):
        p_in, p_out, p_up, p_down = parts[4 * l:4 * l + 4]
        per_layer["w_in"].append(jnp.swapaxes(_sum_parts(p_in, name=f"sum_w_in_l{l}"), 0, 1))
        per_layer["w_out"].append(_sum_parts(p_out, name=f"sum_w_out_l{l}"))
        per_layer["w_up"].append(_sum_parts(p_up, name=f"sum_w_up_l{l}"))
        per_layer["w_down"].append(_sum_parts(p_down, name=f"sum_w_down_l{l}"))
    for k, lst in per_layer.items():
        grads[k] = jnp.stack(lst)
    for k in ("norm1_g", "lru_conv_b", "lru_ba", "lru_bx", "lru_lambda", "norm2_g"):
        grads[k] = jnp.stack(gs[k]).reshape(DEPTH, -1)
    for k in ("lru_wa", "lru_wx"):
        grads[k] = jnp.stack(gs[k]).reshape(DEPTH, LRU_HEADS, LRU_HEAD_DIM, LRU_HEAD_DIM)
    grads["final_g"] = g_final
    cw_full = jnp.stack(gs["lru_conv_w"]).reshape(DEPTH, 4, N_DEV, 128)
    grads["lru_conv_w"] = lax.dynamic_index_in_dim(cw_full, my_id, axis=2, keepdims=False)
    scw_full = jnp.stack(gs["sc_conv_w"]).reshape(DEPTH, 3, N_DEV, 64)
    grads["sc_conv_w"] = lax.dynamic_index_in_dim(scw_full, my_id, axis=2, keepdims=False)
    fcw_full = jnp.stack(gs["ffn_conv_w"]).reshape(DEPTH, 2, 3, D_FF).transpose(0, 2, 1, 3).reshape(DEPTH, 3, N_DEV, 768)
    grads["ffn_conv_w"] = lax.dynamic_index_in_dim(fcw_full, my_id, axis=2, keepdims=False)

    deltas, new_m, new_v = {}, {}, {}
    for k in names:
        shape = w[k].shape
        cols = shape[-1]
        as2d = lambda a: a.reshape(-1, cols)
        d, nm, nv = _adamw(as2d(w[k]), as2d(grads[k]), as2d(m[k]), as2d(v[k]), name=f"adamw_{k}")
        deltas[k], new_m[k], new_v[k] = d.reshape(shape), nm.reshape(shape), nv.reshape(shape)

    return (loss, dx[None], *[grads[k] for k in names], *[deltas[k] for k in names],
            *[new_m[k] for k in names], *[new_v[k] for k in names])
```

```python
import functools
import math

import jax
import jax.numpy as jnp
from jax import lax
from jax.experimental import pallas as pl
from jax.experimental.pallas import tpu as pltpu

F32 = jnp.float32
BF16 = jnp.bfloat16

N_DEV = 8
DEPTH = 2
D_MODEL = 1024
D_LRU = 1024
D_SC = 512
D_MIX = D_LRU + D_SC
D_IN = 2 * D_LRU + 3 * D_SC
D_FF = 3072
LRU_HEADS = 16
LRU_HEAD_DIM = 64
LRU_GROUP = 256
N_GROUPS = D_LRU // LRU_GROUP
HEADS_PER_GROUP = LRU_GROUP // LRU_HEAD_DIM
RG_C = 8.0
EPS = 1e-6
HALO = 8

ADAM_LR = 0.001
ADAM_B1 = 0.9
ADAM_B2 = 0.999
ADAM_EPS = 1e-08
ADAM_WD = 0.01
ADAM_STEP = 10

GELU_C = math.sqrt(2.0 / math.pi)
GELU_A = 0.044715

VMEM_LIMIT = 56 * 1024 * 1024
MESH = pl.DeviceIdType.MESH


def _params(*sem):
    return pltpu.CompilerParams(dimension_semantics=tuple(sem) if sem else None,
                                vmem_limit_bytes=VMEM_LIMIT)


def _gelu_parts(x):
    x2 = x * x
    t = jnp.tanh(GELU_C * (x + GELU_A * x * x2))
    half = 0.5 * (1.0 + t)
    g = x * half
    dg = half + 0.5 * x * (1.0 - t * t) * (GELU_C * (1.0 + 3.0 * GELU_A * x2))
    return g, dg


def _gelu(x):
    t = jnp.tanh(GELU_C * (x + GELU_A * x * x * x))
    return 0.5 * x * (1.0 + t)


def _sigmoid(x):
    return 1.0 / (1.0 + jnp.exp(-x))


def _softplus(x):
    e = jnp.exp(-jnp.abs(x))
    u = 1.0 + e
    log1p_e = jnp.where(u == 1.0, e, jnp.log(u) * (e / (u - 1.0)))
    return jnp.maximum(x, 0.0) + log1p_e


def _rms(x):
    ms = jnp.mean(x * x, axis=-1, keepdims=True)
    return lax.rsqrt(ms + EPS)


def _dot(a, b, dims):
    return lax.dot_general(a, b, (dims, ((), ())), preferred_element_type=F32)


NN = ((1,), (0,))
NT = ((1,), (1,))
TN = ((0,), (0,))


def _matmul(a, b, *, dims, grid, a_spec, b_spec, o_spec, out_shape, acc_shape, name,
            residual=None, r_spec=None):
    nk = grid[2]

    def body(*refs):
        if residual is None:
            a_ref, b_ref, o_ref, acc_ref = refs
            r_ref = None
        else:
            a_ref, b_ref, r_ref, o_ref, acc_ref = refs
        k = pl.program_id(2)

        @pl.when(k == 0)
        def _():
            acc_ref[...] = jnp.zeros_like(acc_ref)

        acc_ref[...] += _dot(a_ref[...].astype(BF16), b_ref[...].astype(BF16), dims)

        @pl.when(k == nk - 1)
        def _():
            res = acc_ref[...]
            if r_ref is not None:
                res = res + r_ref[...]
            o_ref[...] = res.astype(o_ref.dtype)

    in_specs = [a_spec, b_spec]
    args = [a, b]
    if residual is not None:
        in_specs.append(r_spec)
        args.append(residual)
    return pl.pallas_call(
        body, name=name, grid=grid, in_specs=in_specs, out_specs=o_spec, out_shape=out_shape,
        scratch_shapes=[pltpu.VMEM(acc_shape, F32)],
        compiler_params=_params("parallel", "parallel", "arbitrary"),
    )(*args)


def _mm_nn(a, b, *, tm, tn, tk, out_dtype, name, residual=None):
    m, kd = a.shape
    n = b.shape[1]
    return _matmul(
        a, b, dims=NN, grid=(m // tm, n // tn, kd // tk),
        a_spec=pl.BlockSpec((tm, tk), lambda i, j, k: (i, k)),
        b_spec=pl.BlockSpec((tk, tn), lambda i, j, k: (k, j)),
        o_spec=pl.BlockSpec((tm, tn), lambda i, j, k: (i, j)),
        out_shape=jax.ShapeDtypeStruct((m, n), out_dtype), acc_shape=(tm, tn), name=name,
        residual=residual, r_spec=pl.BlockSpec((tm, tn), lambda i, j, k: (i, j)))


def _mm_nt(a, b, *, tm, tn, tk, out_dtype, name):
    m, kd = a.shape
    n = b.shape[0]
    return _matmul(
        a, b, dims=NT, grid=(m // tm, n // tn, kd // tk),
        a_spec=pl.BlockSpec((tm, tk), lambda i, j, k: (i, k)),
        b_spec=pl.BlockSpec((tn, tk), lambda i, j, k: (j, k)),
        o_spec=pl.BlockSpec((tm, tn), lambda i, j, k: (i, j)),
        out_shape=jax.ShapeDtypeStruct((m, n), out_dtype), acc_shape=(tm, tn), name=name)


def _mm_tn(a, b, *, tm, tn, tk, out_dtype, name):
    kd, m = a.shape
    n = b.shape[1]
    return _matmul(
        a, b, dims=TN, grid=(m // tm, n // tn, kd // tk),
        a_spec=pl.BlockSpec((tk, tm), lambda i, j, k: (k, i)),
        b_spec=pl.BlockSpec((tk, tn), lambda i, j, k: (k, j)),
        o_spec=pl.BlockSpec((tm, tn), lambda i, j, k: (i, j)),
        out_shape=jax.ShapeDtypeStruct((m, n), out_dtype), acc_shape=(tm, tn), name=name)


def _mm_up(h2, w_up_b, *, tm, name):
    s = h2.shape[0]
    nb = w_up_b.shape[2]
    per_half = D_FF // nb
    return _matmul(
        h2, w_up_b, dims=NN, grid=(s // tm, N_DEV, 1),
        a_spec=pl.BlockSpec((tm, D_MODEL), lambda i, j, k: (i, 0)),
        b_spec=pl.BlockSpec((None, D_MODEL, nb), lambda i, j, k: (j, 0, 0)),
        o_spec=pl.BlockSpec((None, tm, nb), lambda i, j, k: (j // per_half, i, j % per_half)),
        out_shape=jax.ShapeDtypeStruct((2, s, D_FF), F32), acc_shape=(tm, nb), name=name)


def _mm_up_bwd_x(dp, w_up_b, *, tm, name):
    s = dp.shape[1]
    nb = w_up_b.shape[2]
    per_half = D_FF // nb
    return _matmul(
        dp, w_up_b, dims=NT, grid=(s // tm, 1, N_DEV),
        a_spec=pl.BlockSpec((None, tm, nb), lambda i, j, k: (k // per_half, i, k % per_half)),
        b_spec=pl.BlockSpec((None, D_MODEL, nb), lambda i, j, k: (k, 0, 0)),
        o_spec=pl.BlockSpec((tm, D_MODEL), lambda i, j, k: (i, 0)),
        out_shape=jax.ShapeDtypeStruct((s, D_MODEL), F32), acc_shape=(tm, D_MODEL), name=name)


def _mm_up_bwd_w(h2, dp, *, tm, tk, name):
    s = h2.shape[0]
    nb = D_FF * 2 // N_DEV
    per_half = D_FF // nb
    return _matmul(
        h2, dp, dims=TN, grid=(D_MODEL // tm, N_DEV, s // tk),
        a_spec=pl.BlockSpec((tk, tm), lambda i, j, k: (k, i)),
        b_spec=pl.BlockSpec((None, tk, nb), lambda i, j, k: (j // per_half, k, j % per_half)),
        o_spec=pl.BlockSpec((None, tm, nb), lambda i, j, k: (j, i, 0)),
        out_shape=jax.ShapeDtypeStruct((N_DEV, D_MODEL, nb), BF16), acc_shape=(tm, nb), name=name)


def _norm_fwd(x, g, *, tm, name):
    s, d = x.shape

    def body(x_ref, g_ref, h_ref):
        xv = x_ref[...]
        h_ref[...] = (xv * _rms(xv) * g_ref[...]).astype(BF16)

    return pl.pallas_call(
        body, name=name, grid=(s // tm,),
        in_specs=[pl.BlockSpec((tm, d), lambda i: (i, 0)), pl.BlockSpec((1, d), lambda i: (0, 0))],
        out_specs=pl.BlockSpec((tm, d), lambda i: (i, 0)),
        out_shape=jax.ShapeDtypeStruct((s, d), BF16),
        compiler_params=_params("parallel"),
    )(x, g.reshape(1, d))


def _norm_bwd(dh, x, g, dres, *, tm, name):
    s, d = x.shape

    def body(dh_ref, x_ref, g_ref, dres_ref, dx_ref, dg_ref):
        @pl.when(pl.program_id(0) == 0)
        def _():
            dg_ref[...] = jnp.zeros_like(dg_ref)

        xv = x_ref[...]
        rstd = _rms(xv)
        n = xv * rstd
        dhv = dh_ref[...]
        dn = dhv * g_ref[...]
        dx = rstd * (dn - n * jnp.mean(dn * n, axis=-1, keepdims=True))
        dx_ref[...] = dres_ref[...] + dx
        dg_ref[0:1, :] += jnp.sum(dhv * n, axis=0, keepdims=True)

    return pl.pallas_call(
        body, name=name, grid=(s // tm,),
        in_specs=[pl.BlockSpec((tm, d), lambda i: (i, 0)), pl.BlockSpec((tm, d), lambda i: (i, 0)),
                  pl.BlockSpec((1, d), lambda i: (0, 0)), pl.BlockSpec((tm, d), lambda i: (i, 0))],
        out_specs=[pl.BlockSpec((tm, d), lambda i: (i, 0)), pl.BlockSpec((8, d), lambda i: (0, 0))],
        out_shape=[jax.ShapeDtypeStruct((s, d), F32), jax.ShapeDtypeStruct((8, d), F32)],
        compiler_params=_params("arbitrary"),
    )(dh, x, g.reshape(1, d), dres)


def _loss_head(x, g, tgt, *, tm, name):
    s, d = x.shape

    def body(x_ref, g_ref, t_ref, loss_ref, dx_ref, dg_ref):
        @pl.when(pl.program_id(0) == 0)
        def _():
            dg_ref[...] = jnp.zeros_like(dg_ref)
            loss_ref[...] = jnp.zeros_like(loss_ref)

        xv = x_ref[...]
        gv = g_ref[...]
        rstd = _rms(xv)
        n = xv * rstd
        e = n * gv - t_ref[...]
        part = 0.5 * jnp.sum(jnp.mean(e * e, axis=-1, keepdims=True), axis=0, keepdims=True)
        loss_ref[...] += jnp.broadcast_to(part, loss_ref.shape)
        dy = e * (1.0 / d)
        dn = dy * gv
        dx_ref[...] = rstd * (dn - n * jnp.mean(dn * n, axis=-1, keepdims=True))
        dg_ref[0:1, :] += jnp.sum(dy * n, axis=0, keepdims=True)

    return pl.pallas_call(
        body, name=name, grid=(s // tm,),
        in_specs=[pl.BlockSpec((tm, d), lambda i: (i, 0)), pl.BlockSpec((1, d), lambda i: (0, 0)),
                  pl.BlockSpec((tm, d), lambda i: (i, 0))],
        out_specs=[pl.BlockSpec((8, 128), lambda i: (0, 0)), pl.BlockSpec((tm, d), lambda i: (i, 0)),
                   pl.BlockSpec((8, d), lambda i: (0, 0))],
        out_shape=[jax.ShapeDtypeStruct((8, 128), F32), jax.ShapeDtypeStruct((s, d), F32),
                   jax.ShapeDtypeStruct((8, d), F32)],
        compiler_params=_params("arbitrary"),
    )(x, g.reshape(1, d), tgt)


def _scan_rows(a_ref, b_ref, h_ref, carry, *, rows, reverse):
    width = a_ref.shape[1]
    n_chunks = rows // 8
    row = lax.broadcasted_iota(jnp.int32, (8, width), 0)

    def step(ci, carry):
        chunk = (n_chunks - 1 - ci) if reverse else ci
        off = pl.multiple_of(chunk * 8, 8)
        av = a_ref[pl.ds(off, 8), :]
        bv = b_ref[pl.ds(off, 8), :]
        for sh in (1, 2, 4):
            if reverse:
                a_sh = pltpu.roll(av, 8 - sh, 0)
                b_sh = pltpu.roll(bv, 8 - sh, 0)
                m = row < 8 - sh
            else:
                a_sh = pltpu.roll(av, sh, 0)
                b_sh = pltpu.roll(bv, sh, 0)
                m = row >= sh
            bv = jnp.where(m, av * b_sh + bv, bv)
            av = jnp.where(m, av * a_sh, av)
        h = av * carry + bv
        h_ref[pl.ds(off, 8), :] = h
        return h[0:1, :] if reverse else h[7:8, :]

    return lax.fori_loop(0, n_chunks, step, carry)


def _lru_gates(lx, wa_ref, wx_ref, ba, bx, sp):
    lxb = lx.astype(BF16)
    pre_r = jnp.concatenate(
        [_dot(lxb[:, g * LRU_GROUP:(g + 1) * LRU_GROUP], wa_ref[g], NN) for g in range(N_GROUPS)], axis=1)
    pre_i = jnp.concatenate(
        [_dot(lxb[:, g * LRU_GROUP:(g + 1) * LRU_GROUP], wx_ref[g], NN) for g in range(N_GROUPS)], axis=1)
    r = _sigmoid(pre_r + ba)
    ig = _sigmoid(pre_i + bx)
    log_a = (-RG_C * r) * sp
    a = jnp.exp(log_a)
    mult = jnp.sqrt(-jnp.tanh(log_a) * (a * a + 1.0))
    return lxb, r, ig, a, mult


def _mixer_fwd(z, cw, cb, wa_bd, wx_bd, ba, bx, lam, scw, *, tile, name):
    s = z.shape[0]
    n_tiles = s // tile

    def body(z_ref, cw_ref, cb_ref, wa_ref, wx_ref, ba_ref, bx_ref, lam_ref, scw_ref,
             y_ref, hs_ref, ext_lx, ext_q, a_s, b_s, h_car):
        i = pl.program_id(0)

        @pl.when(i == 0)
        def _():
            ext_lx[0:HALO, :] = jnp.zeros((HALO, D_LRU), F32)
            ext_q[0:HALO, :] = jnp.zeros((HALO, D_SC), F32)
            h_car[...] = jnp.zeros_like(h_car)

        ext_lx[HALO:HALO + tile, :] = z_ref[:, 0:D_LRU]
        ext_q[HALO:HALO + tile, :] = z_ref[:, 2 * D_LRU + D_SC:2 * D_LRU + 2 * D_SC] * z_ref[:, 2 * D_LRU + 2 * D_SC:D_IN]
        lx = cb_ref[...] + cw_ref[0:1, :] * ext_lx[pl.ds(HALO - 3, tile), :]
        for k in range(1, 4):
            lx = lx + cw_ref[k:k + 1, :] * ext_lx[pl.ds(HALO - 3 + k, tile), :]
        cq = scw_ref[0:1, :] * ext_q[pl.ds(HALO - 2, tile), :]
        for k in range(1, 3):
            cq = cq + scw_ref[k:k + 1, :] * ext_q[pl.ds(HALO - 2 + k, tile), :]
        ext_lx[0:HALO, :] = ext_lx[tile:tile + HALO, :]
        ext_q[0:HALO, :] = ext_q[tile:tile + HALO, :]

        sp = _softplus(-lam_ref[...])
        _, _, ig, a, mult = _lru_gates(lx, wa_ref, wx_ref, ba_ref[...], bx_ref[...], sp)
        a_s[...] = a
        b_s[...] = mult * (ig * lx)
        h_car[0:1, :] = _scan_rows(a_s, b_s, hs_ref, h_car[0:1, :], rows=tile, reverse=False)

        y_ref[:, 0:D_LRU] = (hs_ref[...] * _gelu(z_ref[:, D_LRU:2 * D_LRU])).astype(BF16)
        y_ref[:, D_LRU:D_MIX] = (z_ref[:, 2 * D_LRU:2 * D_LRU + D_SC] * cq).astype(BF16)

    full = lambda shape: pl.BlockSpec(shape, lambda i: (0,) * len(shape))
    return pl.pallas_call(
        body, name=name, grid=(n_tiles,),
        in_specs=[pl.BlockSpec((tile, D_IN), lambda i: (i, 0)),
                  full((4, D_LRU)), full((1, D_LRU)),
                  full((N_GROUPS, LRU_GROUP, LRU_GROUP)), full((N_GROUPS, LRU_GROUP, LRU_GROUP)),
                  full((1, D_LRU)), full((1, D_LRU)), full((1, D_LRU)), full((3, D_SC))],
        out_specs=[pl.BlockSpec((tile, D_MIX), lambda i: (i, 0)), pl.BlockSpec((tile, D_LRU), lambda i: (i, 0))],
        out_shape=[jax.ShapeDtypeStruct((s, D_MIX), BF16), jax.ShapeDtypeStruct((s, D_LRU), F32)],
        scratch_shapes=[pltpu.VMEM((tile + HALO, D_LRU), F32), pltpu.VMEM((tile + HALO, D_SC), F32),
                        pltpu.VMEM((tile, D_LRU), F32), pltpu.VMEM((tile, D_LRU), F32),
                        pltpu.VMEM((8, D_LRU), F32)],
        compiler_params=_params("arbitrary"),
    )(z, cw, cb.reshape(1, -1), wa_bd, wx_bd, ba.reshape(1, -1), bx.reshape(1, -1), lam.reshape(1, -1), scw)


def _mixer_bwd(z, hs, dy, cw, cb, wa_bd, wx_bd, ba, bx, lam, scw, *, tile, name):
    s = z.shape[0]
    n_tiles = s // tile
    per8 = tile // 8

    def body(z_ref, zp_ref, hs_ref, hsp_ref, dy_ref, cw_ref, cb_ref, wa_ref, wx_ref, ba_ref, bx_ref, lam_ref, scw_ref,
             dz_ref, dcw_ref, dvec_ref, dwa_ref, dwx_ref, dscw_ref,
             ext_lx, ext_q, ext_h, ext_a, ext_dlx, ext_dcq, a_s, b_s, lam_s, l_car):
        i = pl.program_id(0)
        first_tile = i == n_tiles - 1

        @pl.when(i == 0)
        def _():
            for ref in (dcw_ref, dvec_ref, dwa_ref, dwx_ref, dscw_ref, l_car):
                ref[...] = jnp.zeros_like(ref)
            ext_a[tile:tile + HALO, :] = jnp.zeros((HALO, D_LRU), F32)
            ext_dlx[tile:tile + HALO, :] = jnp.zeros((HALO, D_LRU), F32)
            ext_dcq[tile:tile + HALO, :] = jnp.zeros((HALO, D_SC), F32)

        keep = jnp.where(first_tile, 0.0, 1.0)
        sb = z_ref[:, 2 * D_LRU:2 * D_LRU + D_SC]
        sc = z_ref[:, 2 * D_LRU + D_SC:2 * D_LRU + 2 * D_SC]
        sx = z_ref[:, 2 * D_LRU + 2 * D_SC:D_IN]
        ext_lx[0:HALO, :] = zp_ref[:, 0:D_LRU] * keep
        ext_lx[HALO:HALO + tile, :] = z_ref[:, 0:D_LRU]
        ext_q[0:HALO, :] = zp_ref[:, 2 * D_LRU + D_SC:2 * D_LRU + 2 * D_SC] * zp_ref[:, 2 * D_LRU + 2 * D_SC:D_IN] * keep
        ext_q[HALO:HALO + tile, :] = sc * sx
        ext_h[0:HALO, :] = hsp_ref[...] * keep
        ext_h[HALO:HALO + tile, :] = hs_ref[...]

        lx = cb_ref[...] + cw_ref[0:1, :] * ext_lx[pl.ds(HALO - 3, tile), :]
        for k in range(1, 4):
            lx = lx + cw_ref[k:k + 1, :] * ext_lx[pl.ds(HALO - 3 + k, tile), :]
        cq = scw_ref[0:1, :] * ext_q[pl.ds(HALO - 2, tile), :]
        for k in range(1, 3):
            cq = cq + scw_ref[k:k + 1, :] * ext_q[pl.ds(HALO - 2 + k, tile), :]

        sp = _softplus(-lam_ref[...])
        lxb, r, ig, a, mult = _lru_gates(lx, wa_ref, wx_ref, ba_ref[...], bx_ref[...], sp)

        ge, dge = _gelu_parts(z_ref[:, D_LRU:2 * D_LRU])
        dy_lru = dy_ref[:, 0:D_LRU]
        dz_ref[:, D_LRU:2 * D_LRU] = (dy_lru * hs_ref[...] * dge).astype(BF16)

        ext_a[0:tile, :] = a
        a_s[...] = ext_a[pl.ds(1, tile), :]
        b_s[...] = dy_lru * ge
        l_car[0:1, :] = _scan_rows(a_s, b_s, lam_s, l_car[0:1, :], rows=tile, reverse=True)
        ext_a[tile:tile + HALO, :] = ext_a[0:HALO, :]
        lv = lam_s[...]

        da = lv * ext_h[pl.ds(HALO - 1, tile), :]
        d_mult = lv * ig * lx
        d_i = lv * mult * lx
        dlx = lv * mult * ig
        dlog_a = da * a - d_mult * (a * a) / mult
        d_r = dlog_a * (-RG_C * sp)
        dpre_r = d_r * r * (1.0 - r)
        dpre_i = d_i * ig * (1.0 - ig)
        dvec_ref[1:2, :] += jnp.sum(dpre_r, axis=0, keepdims=True)
        dvec_ref[2:3, :] += jnp.sum(dpre_i, axis=0, keepdims=True)
        dvec_ref[3:4, :] += jnp.sum(dlog_a * (-RG_C * r), axis=0, keepdims=True)
        dpr_b = dpre_r.astype(BF16)
        dpi_b = dpre_i.astype(BF16)
        back = []
        for g in range(N_GROUPS):
            cols = slice(g * LRU_GROUP, (g + 1) * LRU_GROUP)
            dwa_ref[g] += _dot(lxb[:, cols], dpr_b[:, cols], TN)
            dwx_ref[g] += _dot(lxb[:, cols], dpi_b[:, cols], TN)
            back.append(_dot(dpr_b[:, cols], wa_ref[g], NT) + _dot(dpi_b[:, cols], wx_ref[g], NT))
        dlx = dlx + jnp.concatenate(back, axis=1)
        dvec_ref[0:1, :] += jnp.sum(dlx, axis=0, keepdims=True)

        ext_dlx[0:tile, :] = dlx
        for k in range(4):
            dcw_ref[k:k + 1, :] += jnp.sum(dlx * ext_lx[pl.ds(HALO - 3 + k, tile), :], axis=0, keepdims=True)
        dlxp = cw_ref[3:4, :] * dlx
        for k in range(3):
            dlxp = dlxp + cw_ref[k:k + 1, :] * ext_dlx[pl.ds(3 - k, tile), :]
        dz_ref[:, 0:D_LRU] = dlxp.astype(BF16)
        ext_dlx[tile:tile + HALO, :] = ext_dlx[0:HALO, :]

        dy_sc = dy_ref[:, D_LRU:D_MIX]
        dz_ref[:, 2 * D_LRU:2 * D_LRU + D_SC] = (dy_sc * cq).astype(BF16)
        dcq = dy_sc * sb
        ext_dcq[0:tile, :] = dcq
        for k in range(3):
            dscw_ref[k:k + 1, :] += jnp.sum(dcq * ext_q[pl.ds(HALO - 2 + k, tile), :], axis=0, keepdims=True)
        dq = scw_ref[2:3, :] * dcq
        for k in range(2):
            dq = dq + scw_ref[k:k + 1, :] * ext_dcq[pl.ds(2 - k, tile), :]
        dz_ref[:, 2 * D_LRU + D_SC:2 * D_LRU + 2 * D_SC] = (dq * sx).astype(BF16)
        dz_ref[:, 2 * D_LRU + 2 * D_SC:D_IN] = (dq * sc).astype(BF16)
        ext_dcq[tile:tile + HALO, :] = ext_dcq[0:HALO, :]

        @pl.when(i == n_tiles - 1)
        def _():
            dvec_ref[3:4, :] = dvec_ref[3:4, :] * (-_sigmoid(-lam_ref[...]))

    rev = lambda i: n_tiles - 1 - i
    prev8 = lambda i: jnp.maximum(rev(i) * per8 - 1, 0)
    full = lambda shape: pl.BlockSpec(shape, lambda i: (0,) * len(shape))
    return pl.pallas_call(
        body, name=name, grid=(n_tiles,),
        in_specs=[pl.BlockSpec((tile, D_IN), lambda i: (rev(i), 0)),
                  pl.BlockSpec((HALO, D_IN), lambda i: (prev8(i), 0)),
                  pl.BlockSpec((tile, D_LRU), lambda i: (rev(i), 0)),
                  pl.BlockSpec((HALO, D_LRU), lambda i: (prev8(i), 0)),
                  pl.BlockSpec((tile, D_MIX), lambda i: (rev(i), 0)),
                  full((4, D_LRU)), full((1, D_LRU)),
                  full((N_GROUPS, LRU_GROUP, LRU_GROUP)), full((N_GROUPS, LRU_GROUP, LRU_GROUP)),
                  full((1, D_LRU)), full((1, D_LRU)), full((1, D_LRU)), full((3, D_SC))],
        out_specs=[pl.BlockSpec((tile, D_IN), lambda i: (rev(i), 0)),
                   full((8, D_LRU)), full((8, D_LRU)),
                   full((N_GROUPS, LRU_GROUP, LRU_GROUP)), full((N_GROUPS, LRU_GROUP, LRU_GROUP)),
                   full((8, D_SC))],
        out_shape=[jax.ShapeDtypeStruct((s, D_IN), BF16),
                   jax.ShapeDtypeStruct((8, D_LRU), F32), jax.ShapeDtypeStruct((8, D_LRU), F32),
                   jax.ShapeDtypeStruct((N_GROUPS, LRU_GROUP, LRU_GROUP), F32),
                   jax.ShapeDtypeStruct((N_GROUPS, LRU_GROUP, LRU_GROUP), F32),
                   jax.ShapeDtypeStruct((8, D_SC), F32)],
        scratch_shapes=[pltpu.VMEM((tile + HALO, D_LRU), F32), pltpu.VMEM((tile + HALO, D_SC), F32),
                        pltpu.VMEM((tile + HALO, D_LRU), F32), pltpu.VMEM((tile + HALO, D_LRU), F32),
                        pltpu.VMEM((tile + HALO, D_LRU), F32), pltpu.VMEM((tile + HALO, D_SC), F32),
                        pltpu.VMEM((tile, D_LRU), F32), pltpu.VMEM((tile, D_LRU), F32),
                        pltpu.VMEM((tile, D_LRU), F32), pltpu.VMEM((8, D_LRU), F32)],
        compiler_params=_params("arbitrary"),
    )(z, z, hs, hs, dy, cw, cb.reshape(1, -1), wa_bd, wx_bd, ba.reshape(1, -1), bx.reshape(1, -1),
      lam.reshape(1, -1), scw)


def _ffn_fwd(p, fcw, *, tile, tc, name):
    s = p.shape[1]
    per8 = tile // 8

    def body(p_ref, pp_ref, w_ref, act_ref, ext_p):
        i = pl.program_id(0)
        keep = jnp.where(i == 0, 0.0, 1.0)
        ext_p[:, 0:HALO, :] = pp_ref[...] * keep
        ext_p[:, HALO:HALO + tile, :] = p_ref[...]
        u = []
        for half in range(2):
            acc = w_ref[half, 0:1, :] * ext_p[half, pl.ds(HALO - 2, tile), :]
            for k in range(1, 3):
                acc = acc + w_ref[half, k:k + 1, :] * ext_p[half, pl.ds(HALO - 2 + k, tile), :]
            u.append(acc)
        act_ref[...] = (_gelu(u[0]) * u[1]).astype(BF16)

    return pl.pallas_call(
        body, name=name, grid=(s // tile, D_FF // tc),
        in_specs=[pl.BlockSpec((2, tile, tc), lambda i, j: (0, i, j)),
                  pl.BlockSpec((2, HALO, tc), lambda i, j: (0, jnp.maximum(i * per8 - 1, 0), j)),
                  pl.BlockSpec((2, 3, tc), lambda i, j: (0, 0, j))],
        out_specs=pl.BlockSpec((tile, tc), lambda i, j: (i, j)),
        out_shape=jax.ShapeDtypeStruct((s, D_FF), BF16),
        scratch_shapes=[pltpu.VMEM((2, tile + HALO, tc), F32)],
        compiler_params=_params("parallel", "parallel"),
    )(p, p, fcw)


def _ffn_bwd(p, dact, fcw, *, tile, tc, name):
    s = p.shape[1]
    n_tiles = s // tile
    per8 = tile // 8
    ext_rows = tile + HALO

    def body(p_ref, pp_ref, pn_ref, da_ref, dan_ref, w_ref, dp_ref, dw_ref, ext_p, ext_du):
        i = pl.program_id(1)

        @pl.when(i == 0)
        def _():
            dw_ref[...] = jnp.zeros_like(dw_ref)

        keep_prev = jnp.where(i == 0, 0.0, 1.0)
        keep_next = jnp.where(i == n_tiles - 1, 0.0, 1.0)
        ext_p[:, 0:HALO, :] = pp_ref[...] * keep_prev
        ext_p[:, HALO:HALO + tile, :] = p_ref[...]
        ext_p[:, HALO + tile:2 * HALO + tile, :] = pn_ref[...] * keep_next
        u = []
        for half in range(2):
            acc = w_ref[half, 0:1, :] * ext_p[half, pl.ds(HALO - 2, ext_rows), :]
            for k in range(1, 3):
                acc = acc + w_ref[half, k:k + 1, :] * ext_p[half, pl.ds(HALO - 2 + k, ext_rows), :]
            u.append(acc)
        ge, dge = _gelu_parts(u[0])
        da = jnp.concatenate([da_ref[...], dan_ref[...] * keep_next], axis=0)
        ext_du[0, :, :] = da * u[1] * dge
        ext_du[1, :, :] = da * ge
        for half in range(2):
            du = ext_du[half, 0:tile, :]
            acc = w_ref[half, 2:3, :] * du
            for k in range(2):
                acc = acc + w_ref[half, k:k + 1, :] * ext_du[half, pl.ds(2 - k, tile), :]
            dp_ref[half, :, :] = acc.astype(BF16)
            for k in range(3):
                dw_ref[half, k:k + 1, :] += jnp.sum(du * ext_p[half, pl.ds(HALO - 2 + k, tile), :], axis=0, keepdims=True)

    return pl.pallas_call(
        body, name=name, grid=(D_FF // tc, n_tiles),
        in_specs=[pl.BlockSpec((2, tile, tc), lambda j, i: (0, i, j)),
                  pl.BlockSpec((2, HALO, tc), lambda j, i: (0, jnp.maximum(i * per8 - 1, 0), j)),
                  pl.BlockSpec((2, HALO, tc), lambda j, i: (0, jnp.minimum((i + 1) * per8, n_tiles * per8 - 1), j)),
                  pl.BlockSpec((tile, tc), lambda j, i: (i, j)),
                  pl.BlockSpec((HALO, tc), lambda j, i: (jnp.minimum((i + 1) * per8, n_tiles * per8 - 1), j)),
                  pl.BlockSpec((2, 3, tc), lambda j, i: (0, 0, j))],
        out_specs=[pl.BlockSpec((2, tile, tc), lambda j, i: (0, i, j)),
                   pl.BlockSpec((2, 8, tc), lambda j, i: (0, 0, j))],
        out_shape=[jax.ShapeDtypeStruct((2, s, D_FF), BF16), jax.ShapeDtypeStruct((2, 8, D_FF), F32)],
        scratch_shapes=[pltpu.VMEM((2, tile + 2 * HALO, tc), F32), pltpu.VMEM((2, ext_rows, tc), F32)],
        compiler_params=_params("parallel", "arbitrary"),
    )(p, p, p, dact, dact, fcw)


def _adamw_math(w, g, m, v):
    m = ADAM_B1 * m + (1.0 - ADAM_B1) * g
    v = ADAM_B2 * v + (1.0 - ADAM_B2) * (g * g)
    m_hat = m / (1.0 - ADAM_B1 ** ADAM_STEP)
    v_hat = v / (1.0 - ADAM_B2 ** ADAM_STEP)
    delta = -ADAM_LR * (m_hat / (jnp.sqrt(v_hat) + ADAM_EPS) + ADAM_WD * w)
    return delta, m, v


def _adamw(w, g, m, v, *, name):
    rows, cols = w.shape
    tr = rows
    for cand in (512, 256, 128, 64, 32, 16, 8):
        if rows % cand == 0 and rows > cand:
            tr = cand
            break

    def body(w_ref, g_ref, m_ref, v_ref, d_ref, nm_ref, nv_ref):
        d, nm, nv = _adamw_math(w_ref[...], g_ref[...], m_ref[...], v_ref[...])
        d_ref[...] = d
        nm_ref[...] = nm
        nv_ref[...] = nv

    spec = pl.BlockSpec((tr, cols), lambda i: (i, 0))
    return pl.pallas_call(
        body, name=name, grid=(rows // tr,), in_specs=[spec] * 4, out_specs=[spec] * 3,
        out_shape=[jax.ShapeDtypeStruct((rows, cols), F32)] * 3,
        compiler_params=_params("parallel"),
    )(w, g, m, v)


def _sum_parts(parts, *, name):
    _, rows, cols = parts.shape
    tr = rows
    for cand in (256, 128, 64, 32, 16):
        if rows % cand == 0 and rows > cand:
            tr = cand
            break

    def body(p_ref, o_ref):
        acc = p_ref[0].astype(F32)
        for d in range(1, N_DEV):
            acc = acc + p_ref[d].astype(F32)
        o_ref[...] = acc

    return pl.pallas_call(
        body, name=name, grid=(rows // tr,),
        in_specs=[pl.BlockSpec((N_DEV, tr, cols), lambda i: (0, i, 0))],
        out_specs=pl.BlockSpec((tr, cols), lambda i: (i, 0)),
        out_shape=jax.ShapeDtypeStruct((rows, cols), F32),
        compiler_params=_params("parallel"),
    )(parts)


def _place():
    return lax.axis_index("x"), lax.axis_index("y"), lax.axis_index("c")


def _flip(v, bit):
    return 1 - v if bit else v


HBM_SPEC = pl.BlockSpec(memory_space=pltpu.HBM)
SEM_SPEC = pl.BlockSpec(memory_space=pltpu.SEMAPHORE)
ANY_SPEC = pl.BlockSpec(memory_space=pl.ANY)
DATAFLOW = pltpu.SideEffectType.DATAFLOW_SIDE_EFFECTING
N_PEERS = N_DEV - 1


def _peer_copy(k, src_ref, land_ref, send_sem, recv_sem, gather):
    x, y, c = _place()
    my_id = 4 * x + 2 * y + c
    px, py, pc = _flip(x, k & 4), _flip(y, k & 2), _flip(c, k & 1)
    peer_id = 4 * px + 2 * py + pc
    return pltpu.make_async_remote_copy(
        src_ref=src_ref if gather else src_ref.at[peer_id], dst_ref=land_ref.at[my_id],
        send_sem=send_sem.at[k - 1], recv_sem=recv_sem.at[k - 1],
        device_id=(px, py, pc), device_id_type=MESH)


def _copies_start(srcs, *, gather, name):
    n = len(srcs)
    lands = [lax.empty(((N_DEV,) + s.shape) if gather else s.shape, s.dtype) for s in srcs]

    def body(*refs):
        src_refs, land_refs = refs[:n], refs[n:2 * n]
        send_sems, recv_sems = refs[2 * n:3 * n], refs[3 * n:4 * n]
        token, local_sems = refs[6 * n], refs[6 * n + 1]
        x, y, c = _place()
        my_id = 4 * x + 2 * y + c
        own = [pltpu.make_async_copy(src_refs[t] if gather else src_refs[t].at[my_id], land_refs[t].at[my_id],
                                     local_sems.at[t]) for t in range(n)]
        for t in range(n):
            own[t].start()
            for k in range(1, N_DEV):
                _peer_copy(k, src_refs[t], land_refs[t], send_sems[t], recv_sems[t], gather).start()
        for cp in own:
            cp.wait()
        token[...] = jnp.zeros_like(token)

    sem = pltpu.SemaphoreType.DMA((N_PEERS,))
    outs = pl.pallas_call(
        body, name=name,
        in_specs=[HBM_SPEC] * (2 * n),
        out_specs=[SEM_SPEC] * (2 * n) + [HBM_SPEC] * (2 * n) + [pl.BlockSpec(memory_space=pltpu.VMEM)],
        out_shape=[sem] * (2 * n) + [pltpu.HBM(s.shape, s.dtype) for s in srcs]
        + [pltpu.HBM(l.shape, l.dtype) for l in lands] + [jax.ShapeDtypeStruct((8, 128), F32)],
        input_output_aliases={i: 2 * n + i for i in range(2 * n)},
        scratch_shapes=[pltpu.SemaphoreType.DMA((n,))],
        compiler_params=pltpu.CompilerParams(has_side_effects=DATAFLOW),
    )(*[pltpu.with_memory_space_constraint(s, pltpu.HBM) for s in srcs],
      *[pltpu.with_memory_space_constraint(l, pltpu.HBM) for l in lands])
    handles = [(outs[t], outs[n + t], outs[2 * n + t], outs[3 * n + t]) for t in range(n)]
    return handles, outs[4 * n]


def _copies_wait(handle, after, *, gather, name):
    send_sem, recv_sem, src, land = handle

    def body(src_ref, land_ref, send_ref, recv_ref, after_ref, src_out, land_out):
        for k in range(1, N_DEV):
            cp = _peer_copy(k, src_ref, land_ref, send_ref, recv_ref, gather)
            cp.wait_send()
            cp.wait_recv()

    return pl.pallas_call(
        body, name=name,
        in_specs=[HBM_SPEC, HBM_SPEC, SEM_SPEC, SEM_SPEC, ANY_SPEC],
        out_specs=[HBM_SPEC, HBM_SPEC],
        out_shape=[pltpu.HBM(src.shape, src.dtype), pltpu.HBM(land.shape, land.dtype)],
        input_output_aliases={0: 0, 1: 1},
        compiler_params=pltpu.CompilerParams(has_side_effects=DATAFLOW),
    )(src, land, send_sem, recv_sem, after)[1]


def _tie(value, token):
    if token is None:
        return value
    return lax.optimization_barrier((value, token))[0]


def _all_reduce_small(buf, *, name):
    _, rows, lanes = buf.shape

    def body(in_ref, out_ref, parts, send_sems, recv_sems):
        x, y, c = _place()
        my_id = 4 * x + 2 * y + c
        peers = []
        for k in range(1, N_DEV):
            px, py, pc = _flip(x, k & 4), _flip(y, k & 2), _flip(c, k & 1)
            peers.append(((px, py, pc), 4 * px + 2 * py + pc))
        scatter = [pltpu.make_async_remote_copy(
            src_ref=in_ref.at[pid], dst_ref=parts.at[my_id],
            send_sem=send_sems.at[0, k], recv_sem=recv_sems.at[0, k],
            device_id=peer, device_id_type=MESH) for k, (peer, pid) in enumerate(peers)]
        for cp in scatter:
            cp.start()
        parts[my_id] = in_ref[my_id]
        for cp in scatter:
            cp.wait()
        total = parts[0]
        for d in range(1, N_DEV):
            total = total + parts[d]
        out_ref[my_id] = total
        gather = [pltpu.make_async_remote_copy(
            src_ref=out_ref.at[my_id], dst_ref=out_ref.at[my_id],
            send_sem=send_sems.at[1, k], recv_sem=recv_sems.at[1, k],
            device_id=peer, device_id_type=MESH) for k, (peer, pid) in enumerate(peers)]
        for cp in gather:
            cp.start()
        for k, (peer, pid) in enumerate(peers):
            pltpu.make_async_remote_copy(
                src_ref=out_ref.at[pid], dst_ref=out_ref.at[pid],
                send_sem=send_sems.at[1, k], recv_sem=recv_sems.at[1, k],
                device_id=peer, device_id_type=MESH).wait()

    vmem = pl.BlockSpec(memory_space=pltpu.VMEM)
    return pl.pallas_call(
        body, name=name, in_specs=[vmem], out_specs=vmem,
        out_shape=jax.ShapeDtypeStruct(buf.shape, F32),
        scratch_shapes=[pltpu.VMEM(buf.shape, F32),
                        pltpu.SemaphoreType.DMA((2, 7)), pltpu.SemaphoreType.DMA((2, 7))],
        compiler_params=pltpu.CompilerParams(vmem_limit_bytes=VMEM_LIMIT),
    )(buf)


TM = 512
MIX_TILE = 128
FFN_TILE = 256
FFN_TC = 512


def _block_diag(w):
    wg = w.reshape(N_GROUPS, HEADS_PER_GROUP, LRU_HEAD_DIM, LRU_HEAD_DIM)
    eye = jnp.eye(HEADS_PER_GROUP, dtype=w.dtype)
    bd = wg[:, :, :, None, :] * eye[None, :, None, :, None]
    return bd.reshape(N_GROUPS, LRU_GROUP, LRU_GROUP).astype(BF16)


def _head_blocks(bd):
    b5 = bd.reshape(N_GROUPS, HEADS_PER_GROUP, LRU_HEAD_DIM, HEADS_PER_GROUP, LRU_HEAD_DIM)
    blocks = [b5[:, h, :, h, :] for h in range(HEADS_PER_GROUP)]
    return jnp.stack(blocks, axis=1).reshape(LRU_HEADS, LRU_HEAD_DIM, LRU_HEAD_DIM)


def _w(lw, key, after):
    value = lw[key]
    return value(after) if callable(value) else value


def _layer_fwd(x, lw, tag):
    h1 = _norm_fwd(x, lw["g1"], tm=TM, name=f"norm1_fwd_{tag}")
    z = _mm_nt(h1, _w(lw, "w_in_t", h1), tm=TM, tn=512, tk=D_MODEL, out_dtype=F32, name=f"in_proj_{tag}")
    y_mix, hs = _mixer_fwd(z, _w(lw, "cw", z), lw["cb"], lw["wa_bd"], lw["wx_bd"], lw["ba"], lw["bx"], lw["lam"],
                           _w(lw, "scw", z), tile=MIX_TILE, name=f"mixer_fwd_{tag}")
    x2 = _mm_nn(y_mix, _w(lw, "w_out", y_mix), tm=TM, tn=D_MODEL, tk=D_MIX, out_dtype=F32, name=f"out_proj_{tag}",
                residual=x)
    h2 = _norm_fwd(x2, lw["g2"], tm=TM, name=f"norm2_fwd_{tag}")
    p = _mm_up(h2, _w(lw, "w_up_b", h2), tm=TM, name=f"up_proj_{tag}")
    act = _ffn_fwd(p, _w(lw, "fcw", p), tile=FFN_TILE, tc=FFN_TC, name=f"ffn_fwd_{tag}")
    x3 = _mm_nn(act, _w(lw, "w_down", act), tm=TM, tn=D_MODEL, tk=1024, out_dtype=F32, name=f"down_proj_{tag}",
                residual=x2)
    saved = dict(x=x, h1=h1, z=z, y_mix=y_mix, hs=hs, x2=x2, h2=h2, p=p, act=act)
    return x3, saved


def _layer_bwd(dx3, lw, sv, tag, put):
    w_in_t, w_out, w_up_b, w_down = (_w(lw, k, dx3) for k in ("w_in_t", "w_out", "w_up_b", "w_down"))
    cw, scw, fcw = (_w(lw, k, dx3) for k in ("cw", "scw", "fcw"))
    dact = _mm_nt(dx3, w_down, tm=TM, tn=512, tk=D_MODEL, out_dtype=F32, name=f"down_bwd_x_{tag}")
    g_down = _mm_tn(sv["act"], dx3, tm=512, tn=D_MODEL, tk=512, out_dtype=BF16, name=f"down_bwd_w_{tag}")
    dact = _tie(dact, put("w_down", g_down))
    dp, dfcw = _ffn_bwd(sv["p"], dact, fcw, tile=FFN_TILE, tc=FFN_TC, name=f"ffn_bwd_{tag}")
    dh2 = _mm_up_bwd_x(dp, w_up_b, tm=TM, name=f"up_bwd_x_{tag}")
    g_up = _mm_up_bwd_w(sv["h2"], dp, tm=512, tk=512, name=f"up_bwd_w_{tag}")
    dh2 = _tie(dh2, put("w_up_b", g_up))
    dx2, dg2 = _norm_bwd(dh2, sv["x2"], lw["g2"], dx3, tm=TM, name=f"norm2_bwd_{tag}")
    dy = _mm_nt(dx2, w_out, tm=TM, tn=512, tk=D_MODEL, out_dtype=F32, name=f"out_bwd_x_{tag}")
    g_out = _mm_tn(sv["y_mix"], dx2, tm=512, tn=D_MODEL, tk=512, out_dtype=BF16, name=f"out_bwd_w_{tag}")
    dy = _tie(dy, put("w_out", g_out))
    dz, dcw, dvec, dwa, dwx, dscw = _mixer_bwd(
        sv["z"], sv["hs"], dy, cw, lw["cb"], lw["wa_bd"], lw["wx_bd"], lw["ba"], lw["bx"], lw["lam"],
        scw, tile=MIX_TILE, name=f"mixer_bwd_{tag}")
    dh1 = _mm_nn(dz, w_in_t, tm=TM, tn=D_MODEL, tk=512, out_dtype=F32, name=f"in_bwd_x_{tag}")
    g_in_t = _mm_tn(dz, sv["h1"], tm=512, tn=D_MODEL, tk=512, out_dtype=BF16, name=f"in_bwd_w_{tag}")
    dh1 = _tie(dh1, put("w_in_t", g_in_t))
    dx, dg1 = _norm_bwd(dh1, sv["x"], lw["g1"], dx2, tm=TM, name=f"norm1_bwd_{tag}")
    small = dict(norm1_g=dg1[0], lru_conv_w=dcw[0:4], lru_conv_b=dvec[0], lru_wa=_head_blocks(dwa),
                 lru_ba=dvec[1], lru_wx=_head_blocks(dwx), lru_bx=dvec[2], lru_lambda=dvec[3],
                 sc_conv_w=dscw[0:3], norm2_g=dg2[0], ffn_conv_w=dfcw[:, 0:3, :])
    return dx, small


SMALL_ORDER = ("norm1_g", "lru_conv_w", "lru_conv_b", "lru_wa", "lru_ba", "lru_wx", "lru_bx", "lru_lambda",
               "sc_conv_w", "norm2_g", "ffn_conv_w")


def _local_step(x, tgt, layers, final_g, put):
    saved = []
    h = x
    for l in range(DEPTH):
        h, sv = _layer_fwd(h, layers[l], f"l{l}")
        saved.append(sv)
    loss_blk, dx, dgf = _loss_head(h, final_g, tgt, tm=TM, name="loss_head")
    smalls = [None] * DEPTH
    for l in reversed(range(DEPTH)):
        dx, smalls[l] = _layer_bwd(dx, layers[l], saved[l], f"l{l}", functools.partial(put, l))
    return loss_blk[0, 0], dx, smalls, dgf[0]


def kernel(x, norm1_g, w_in, lru_conv_w, lru_conv_b, lru_wa, lru_ba, lru_wx, lru_bx, lru_lambda, sc_conv_w, w_out, norm2_g, w_up, ffn_conv_w, w_down, final_g, loss_target, m_norm1_g, m_w_in, m_lru_conv_w, m_lru_conv_b, m_lru_wa, m_lru_ba, m_lru_wx, m_lru_bx, m_lru_lambda, m_sc_conv_w, m_w_out, m_norm2_g, m_w_up, m_ffn_conv_w, m_w_down, m_final_g, v_norm1_g, v_w_in, v_lru_conv_w, v_lru_conv_b, v_lru_wa, v_lru_ba, v_lru_wx, v_lru_bx, v_lru_lambda, v_sc_conv_w, v_w_out, v_norm2_g, v_w_up, v_ffn_conv_w, v_w_down, v_final_g):
    names = ["norm1_g", "w_in", "lru_conv_w", "lru_conv_b", "lru_wa", "lru_ba", "lru_wx", "lru_bx", "lru_lambda",
             "sc_conv_w", "w_out", "norm2_g", "w_up", "ffn_conv_w", "w_down", "final_g"]
    w = dict(zip(names, [norm1_g, w_in, lru_conv_w, lru_conv_b, lru_wa, lru_ba, lru_wx, lru_bx, lru_lambda,
                         sc_conv_w, w_out, norm2_g, w_up, ffn_conv_w, w_down, final_g]))
    m = dict(zip(names, [m_norm1_g, m_w_in, m_lru_conv_w, m_lru_conv_b, m_lru_wa, m_lru_ba, m_lru_wx, m_lru_bx,
                         m_lru_lambda, m_sc_conv_w, m_w_out, m_norm2_g, m_w_up, m_ffn_conv_w, m_w_down, m_final_g]))
    v = dict(zip(names, [v_norm1_g, v_w_in, v_lru_conv_w, v_lru_conv_b, v_lru_wa, v_lru_ba, v_lru_wx, v_lru_bx,
                         v_lru_lambda, v_sc_conv_w, v_w_out, v_norm2_g, v_w_up, v_ffn_conv_w, v_w_down, v_final_g]))
    my_id = 4 * lax.axis_index("x") + 2 * lax.axis_index("y") + lax.axis_index("c")

    taps = jnp.zeros((DEPTH, 16, 768), F32)
    taps = taps.at[:, 0:4, 0:128].set(lru_conv_w).at[:, 4:7, 0:64].set(sc_conv_w).at[:, 8:11, :].set(ffn_conv_w)
    shards = {}
    for l in range(DEPTH):
        shards[f"w_in_t{l}"] = jnp.swapaxes(w_in[l], 0, 1).astype(BF16)
        if l == 0:
            shards["taps"] = taps.reshape(DEPTH * 16, 768)
        shards[f"w_out{l}"] = w_out[l].astype(BF16)
        shards[f"w_up_b{l}"] = w_up[l].astype(BF16)
        shards[f"w_down{l}"] = w_down[l].astype(BF16)
    handles, gather_token = _copies_start(list(shards.values()), gather=True, name="gather_start")
    handle = dict(zip(shards, handles))
    got = {}

    def fetch(key, after):
        if key not in got:
            got[key] = _copies_wait(handle[key], after, gather=True, name=f"gather_wait_{key}")
        return got[key]

    def tap_rows(l, lo, hi, width, after):
        tl = fetch("taps", after).reshape(N_DEV, DEPTH, 16, 768)[:, l, lo:hi, 0:width]
        return jnp.transpose(tl, (1, 0, 2)).reshape(hi - lo, N_DEV * width)

    layers = []
    for l in range(DEPTH):
        layers.append(dict(
            g1=norm1_g[l], g2=norm2_g[l], cb=lru_conv_b[l], ba=lru_ba[l], bx=lru_bx[l], lam=lru_lambda[l],
            wa_bd=_block_diag(lru_wa[l]), wx_bd=_block_diag(lru_wx[l]),
            cw=functools.partial(tap_rows, l, 0, 4, 128), scw=functools.partial(tap_rows, l, 4, 7, 64),
            fcw=lambda after, l=l: tap_rows(l, 8, 11, 768, after).reshape(3, 2, D_FF).transpose(1, 0, 2),
            w_in_t=lambda after, l=l: fetch(f"w_in_t{l}", after).reshape(D_IN, D_MODEL),
            w_out=lambda after, l=l: fetch(f"w_out{l}", after).reshape(D_MIX, D_MODEL),
            w_up_b=lambda after, l=l: fetch(f"w_up_b{l}", after),
            w_down=lambda after, l=l: fetch(f"w_down{l}", after).reshape(D_FF, D_MODEL)))

    scatter_handles = {}

    def put(l, key, grad):
        blocks = grad if grad.ndim == 3 else grad.reshape(N_DEV, grad.shape[0] // N_DEV, grad.shape[1])
        (h,), token = _copies_start([blocks], gather=False, name=f"scatter_start_{key}{l}")
        scatter_handles[(l, key)] = h
        return token

    loss_local, dx, smalls, dgf = _local_step(_tie(x[0], gather_token), loss_target[0], layers, final_g, put)
    loss = lax.psum(loss_local, ("x", "y", "c"))

    parts = []
    for l in range(DEPTH):
        for key in ("w_in_t", "w_out", "w_up_b", "w_down"):
            parts.append(_copies_wait(scatter_handles[(l, key)], dx, gather=False, name=f"scatter_wait_{key}{l}"))

    flat = [smalls[l][k].reshape(-1) for l in range(DEPTH) for k in SMALL_ORDER] + [dgf.reshape(-1)]
    sizes = [f.shape[0] for f in flat]
    total = sum(sizes)
    rows = -(-total // (N_DEV * 128 * 8)) * 8
    flat.append(jnp.zeros((N_DEV * rows * 128 - total,), F32))
    small_sum = _all_reduce_small(jnp.concatenate(flat).reshape(N_DEV, rows, 128), name="reduce_small").reshape(-1)
    small_g, off = [], 0
    for sz in sizes:
        small_g.append(small_sum[off:off + sz])
        off += sz
    gs = {}
    for l in range(DEPTH):
        for i, k in enumerate(SMALL_ORDER):
            gs.setdefault(k, []).append(small_g[l * len(SMALL_ORDER) + i])
    g_final = small_g[-1]

    grads = {}
    per_layer = {k: [] for k in ("w_in", "w_out", "w_up", "w_down")}
    for l in range(DEPTH):
        p_in, p_out, p_up, p_down = parts[4 * l:4 * l + 4]
        per_layer["w_in"].append(jnp.swapaxes(_sum_parts(p_in, name=f"sum_w_in_l{l}"), 0, 1))
        per_layer["w_out"].append(_sum_parts(p_out, name=f"sum_w_out_l{l}"))
        per_layer["w_up"].append(_sum_parts(p_up, name=f"sum_w_up_l{l}"))
        per_layer["w_down"].append(_sum_parts(p_down, name=f"sum_w_down_l{l}"))
    for k, lst in per_layer.items():
        grads[k] = jnp.stack(lst)
    for k in ("norm1_g", "lru_conv_b", "lru_ba", "lru_bx", "lru_lambda", "norm2_g"):
        grads[k] = jnp.stack(gs[k]).reshape(DEPTH, -1)
    for k in ("lru_wa", "lru_wx"):
        grads[k] = jnp.stack(gs[k]).reshape(DEPTH, LRU_HEADS, LRU_HEAD_DIM, LRU_HEAD_DIM)
    grads["final_g"] = g_final
    cw_full = jnp.stack(gs["lru_conv_w"]).reshape(DEPTH, 4, N_DEV, 128)
    grads["lru_conv_w"] = lax.dynamic_index_in_dim(cw_full, my_id, axis=2, keepdims=False)
    scw_full = jnp.stack(gs["sc_conv_w"]).reshape(DEPTH, 3, N_DEV, 64)
    grads["sc_conv_w"] = lax.dynamic_index_in_dim(scw_full, my_id, axis=2, keepdims=False)
    fcw_full = jnp.stack(gs["ffn_conv_w"]).reshape(DEPTH, 2, 3, D_FF).transpose(0, 2, 1, 3).reshape(DEPTH, 3, N_DEV, 768)
    grads["ffn_conv_w"] = lax.dynamic_index_in_dim(fcw_full, my_id, axis=2, keepdims=False)

    deltas, new_m, new_v = {}, {}, {}
    for k in names:
        shape = w[k].shape
        cols = shape[-1]
        as2d = lambda a: a.reshape(-1, cols)
        d, nm, nv = _adamw(as2d(w[k]), as2d(grads[k]), as2d(m[k]), as2d(v[k]), name=f"adamw_{k}")
        deltas[k], new_m[k], new_v[k] = d.reshape(shape), nm.reshape(shape), nv.reshape(shape)

    return (loss, dx[None], *[grads[k] for k in names], *[deltas[k] for k in names],
            *[new_m[k] for k in names], *[new_v[k] for k in names])
```

```python
import functools
import math

import jax
import jax.numpy as jnp
from jax import lax
from jax.experimental import pallas as pl
from jax.experimental.pallas import tpu as pltpu

F32 = jnp.float32
BF16 = jnp.bfloat16

N_DEV = 8
DEPTH = 2
D_MODEL = 1024
D_LRU = 1024
D_SC = 512
D_MIX = D_LRU + D_SC
D_IN = 2 * D_LRU + 3 * D_SC
D_FF = 3072
LRU_HEADS = 16
LRU_HEAD_DIM = 64
LRU_GROUP = 256
N_GROUPS = D_LRU // LRU_GROUP
HEADS_PER_GROUP = LRU_GROUP // LRU_HEAD_DIM
RG_C = 8.0
EPS = 1e-6
HALO = 8

ADAM_LR = 0.001
ADAM_B1 = 0.9
ADAM_B2 = 0.999
ADAM_EPS = 1e-08
ADAM_WD = 0.01
ADAM_STEP = 10

GELU_C = math.sqrt(2.0 / math.pi)
GELU_A = 0.044715

VMEM_LIMIT = 56 * 1024 * 1024
MESH = pl.DeviceIdType.MESH


def _params(*sem):
    return pltpu.CompilerParams(dimension_semantics=tuple(sem) if sem else None,
                                vmem_limit_bytes=VMEM_LIMIT)


def _gelu_parts(x):
    x2 = x * x
    t = jnp.tanh(GELU_C * (x + GELU_A * x * x2))
    half = 0.5 * (1.0 + t)
    g = x * half
    dg = half + 0.5 * x * (1.0 - t * t) * (GELU_C * (1.0 + 3.0 * GELU_A * x2))
    return g, dg


def _gelu(x):
    t = jnp.tanh(GELU_C * (x + GELU_A * x * x * x))
    return 0.5 * x * (1.0 + t)


def _sigmoid(x):
    return 1.0 / (1.0 + jnp.exp(-x))


def _softplus(x):
    e = jnp.exp(-jnp.abs(x))
    u = 1.0 + e
    log1p_e = jnp.where(u == 1.0, e, jnp.log(u) * (e / (u - 1.0)))
    return jnp.maximum(x, 0.0) + log1p_e


def _rms(x):
    ms = jnp.mean(x * x, axis=-1, keepdims=True)
    return lax.rsqrt(ms + EPS)


def _dot(a, b, dims):
    return lax.dot_general(a, b, (dims, ((), ())), preferred_element_type=F32)


NN = ((1,), (0,))
NT = ((1,), (1,))
TN = ((0,), (0,))


def _matmul(a, b, *, dims, grid, a_spec, b_spec, o_spec, out_shape, acc_shape, name,
            residual=None, r_spec=None):
    nk = grid[2]

    def body(*refs):
        if residual is None:
            a_ref, b_ref, o_ref, acc_ref = refs
            r_ref = None
        else:
            a_ref, b_ref, r_ref, o_ref, acc_ref = refs
        k = pl.program_id(2)

        @pl.when(k == 0)
        def _():
            acc_ref[...] = jnp.zeros_like(acc_ref)

        acc_ref[...] += _dot(a_ref[...].astype(BF16), b_ref[...].astype(BF16), dims)

        @pl.when(k == nk - 1)
        def _():
            res = acc_ref[...]
            if r_ref is not None:
                res = res + r_ref[...]
            o_ref[...] = res.astype(o_ref.dtype)

    in_specs = [a_spec, b_spec]
    args = [a, b]
    if residual is not None:
        in_specs.append(r_spec)
        args.append(residual)
    return pl.pallas_call(
        body, name=name, grid=grid, in_specs=in_specs, out_specs=o_spec, out_shape=out_shape,
        scratch_shapes=[pltpu.VMEM(acc_shape, F32)],
        compiler_params=_params("parallel", "parallel", "arbitrary"),
    )(*args)


def _mm_nn(a, b, *, tm, tn, tk, out_dtype, name, residual=None):
    m, kd = a.shape
    n = b.shape[1]
    return _matmul(
        a, b, dims=NN, grid=(m // tm, n // tn, kd // tk),
        a_spec=pl.BlockSpec((tm, tk), lambda i, j, k: (i, k)),
        b_spec=pl.BlockSpec((tk, tn), lambda i, j, k: (k, j)),
        o_spec=pl.BlockSpec((tm, tn), lambda i, j, k: (i, j)),
        out_shape=jax.ShapeDtypeStruct((m, n), out_dtype), acc_shape=(tm, tn), name=name,
        residual=residual, r_spec=pl.BlockSpec((tm, tn), lambda i, j, k: (i, j)))


def _mm_nt(a, b, *, tm, tn, tk, out_dtype, name):
    m, kd = a.shape
    n = b.shape[0]
    return _matmul(
        a, b, dims=NT, grid=(m // tm, n // tn, kd // tk),
        a_spec=pl.BlockSpec((tm, tk), lambda i, j, k: (i, k)),
        b_spec=pl.BlockSpec((tn, tk), lambda i, j, k: (j, k)),
        o_spec=pl.BlockSpec((tm, tn), lambda i, j, k: (i, j)),
        out_shape=jax.ShapeDtypeStruct((m, n), out_dtype), acc_shape=(tm, tn), name=name)


def _mm_tn(a, b, *, tm, tn, tk, out_dtype, name):
    kd, m = a.shape
    n = b.shape[1]
    return _matmul(
        a, b, dims=TN, grid=(m // tm, n // tn, kd // tk),
        a_spec=pl.BlockSpec((tk, tm), lambda i, j, k: (k, i)),
        b_spec=pl.BlockSpec((tk, tn), lambda i, j, k: (k, j)),
        o_spec=pl.BlockSpec((tm, tn), lambda i, j, k: (i, j)),
        out_shape=jax.ShapeDtypeStruct((m, n), out_dtype), acc_shape=(tm, tn), name=name)


def _mm_up(h2, w_up_b, *, tm, name):
    s = h2.shape[0]
    nb = w_up_b.shape[2]
    per_half = D_FF // nb
    return _matmul(
        h2, w_up_b, dims=NN, grid=(s // tm, N_DEV, 1),
        a_spec=pl.BlockSpec((tm, D_MODEL), lambda i, j, k: (i, 0)),
        b_spec=pl.BlockSpec((None, D_MODEL, nb), lambda i, j, k: (j, 0, 0)),
        o_spec=pl.BlockSpec((None, tm, nb), lambda i, j, k: (j // per_half, i, j % per_half)),
        out_shape=jax.ShapeDtypeStruct((2, s, D_FF), F32), acc_shape=(tm, nb), name=name)


def _mm_up_bwd_x(dp, w_up_b, *, tm, name):
    s = dp.shape[1]
    nb = w_up_b.shape[2]
    per_half = D_FF // nb
    return _matmul(
        dp, w_up_b, dims=NT, grid=(s // tm, 1, N_DEV),
        a_spec=pl.BlockSpec((None, tm, nb), lambda i, j, k: (k // per_half, i, k % per_half)),
        b_spec=pl.BlockSpec((None, D_MODEL, nb), lambda i, j, k: (k, 0, 0)),
        o_spec=pl.BlockSpec((tm, D_MODEL), lambda i, j, k: (i, 0)),
        out_shape=jax.ShapeDtypeStruct((s, D_MODEL), F32), acc_shape=(tm, D_MODEL), name=name)


def _mm_up_bwd_w(h2, dp, *, tm, tk, name):
    s = h2.shape[0]
    nb = D_FF * 2 // N_DEV
    per_half = D_FF // nb
    return _matmul(
        h2, dp, dims=TN, grid=(D_MODEL // tm, N_DEV, s // tk),
        a_spec=pl.BlockSpec((tk, tm), lambda i, j, k: (k, i)),
        b_spec=pl.BlockSpec((None, tk, nb), lambda i, j, k: (j // per_half, k, j % per_half)),
        o_spec=pl.BlockSpec((None, tm, nb), lambda i, j, k: (j, i, 0)),
        out_shape=jax.ShapeDtypeStruct((N_DEV, D_MODEL, nb), BF16), acc_shape=(tm, nb), name=name)


def _norm_fwd(x, g, *, tm, name):
    s, d = x.shape

    def body(x_ref, g_ref, h_ref):
        xv = x_ref[...]
        h_ref[...] = (xv * _rms(xv) * g_ref[...]).astype(BF16)

    return pl.pallas_call(
        body, name=name, grid=(s // tm,),
        in_specs=[pl.BlockSpec((tm, d), lambda i: (i, 0)), pl.BlockSpec((1, d), lambda i: (0, 0))],
        out_specs=pl.BlockSpec((tm, d), lambda i: (i, 0)),
        out_shape=jax.ShapeDtypeStruct((s, d), BF16),
        compiler_params=_params("parallel"),
    )(x, g.reshape(1, d))


def _behind(token):
    return jnp.zeros((8, 128), F32) if token is None else token


def _norm_bwd(dh, x, g, dres, *, tm, name, token=None):
    s, d = x.shape

    def body(dh_ref, x_ref, g_ref, dres_ref, token_ref, dx_ref, dg_ref):
        @pl.when(pl.program_id(0) == 0)
        def _():
            dg_ref[...] = jnp.zeros_like(dg_ref)

        xv = x_ref[...]
        rstd = _rms(xv)
        n = xv * rstd
        dhv = dh_ref[...]
        dn = dhv * g_ref[...]
        dx = rstd * (dn - n * jnp.mean(dn * n, axis=-1, keepdims=True))
        dx_ref[...] = dres_ref[...] + dx
        dg_ref[0:1, :] += jnp.sum(dhv * n, axis=0, keepdims=True)

    return pl.pallas_call(
        body, name=name, grid=(s // tm,),
        in_specs=[pl.BlockSpec((tm, d), lambda i: (i, 0)), pl.BlockSpec((tm, d), lambda i: (i, 0)),
                  pl.BlockSpec((1, d), lambda i: (0, 0)), pl.BlockSpec((tm, d), lambda i: (i, 0)),
                  pl.BlockSpec(memory_space=pl.ANY)],
        out_specs=[pl.BlockSpec((tm, d), lambda i: (i, 0)), pl.BlockSpec((8, d), lambda i: (0, 0))],
        out_shape=[jax.ShapeDtypeStruct((s, d), F32), jax.ShapeDtypeStruct((8, d), F32)],
        compiler_params=_params("arbitrary"),
    )(dh, x, g.reshape(1, d), dres, _behind(token))


def _loss_head(x, g, tgt, *, tm, name):
    s, d = x.shape

    def body(x_ref, g_ref, t_ref, loss_ref, dx_ref, dg_ref):
        @pl.when(pl.program_id(0) == 0)
        def _():
            dg_ref[...] = jnp.zeros_like(dg_ref)
            loss_ref[...] = jnp.zeros_like(loss_ref)

        xv = x_ref[...]
        gv = g_ref[...]
        rstd = _rms(xv)
        n = xv * rstd
        e = n * gv - t_ref[...]
        part = 0.5 * jnp.sum(jnp.mean(e * e, axis=-1, keepdims=True), axis=0, keepdims=True)
        loss_ref[...] += jnp.broadcast_to(part, loss_ref.shape)
        dy = e * (1.0 / d)
        dn = dy * gv
        dx_ref[...] = rstd * (dn - n * jnp.mean(dn * n, axis=-1, keepdims=True))
        dg_ref[0:1, :] += jnp.sum(dy * n, axis=0, keepdims=True)

    return pl.pallas_call(
        body, name=name, grid=(s // tm,),
        in_specs=[pl.BlockSpec((tm, d), lambda i: (i, 0)), pl.BlockSpec((1, d), lambda i: (0, 0)),
                  pl.BlockSpec((tm, d), lambda i: (i, 0))],
        out_specs=[pl.BlockSpec((8, 128), lambda i: (0, 0)), pl.BlockSpec((tm, d), lambda i: (i, 0)),
                   pl.BlockSpec((8, d), lambda i: (0, 0))],
        out_shape=[jax.ShapeDtypeStruct((8, 128), F32), jax.ShapeDtypeStruct((s, d), F32),
                   jax.ShapeDtypeStruct((8, d), F32)],
        compiler_params=_params("arbitrary"),
    )(x, g.reshape(1, d), tgt)


def _scan_rows(a_ref, b_ref, h_ref, carry, *, rows, reverse):
    width = a_ref.shape[1]
    n_chunks = rows // 8
    row = lax.broadcasted_iota(jnp.int32, (8, width), 0)

    def step(ci, carry):
        chunk = (n_chunks - 1 - ci) if reverse else ci
        off = pl.multiple_of(chunk * 8, 8)
        av = a_ref[pl.ds(off, 8), :]
        bv = b_ref[pl.ds(off, 8), :]
        for sh in (1, 2, 4):
            if reverse:
                a_sh = pltpu.roll(av, 8 - sh, 0)
                b_sh = pltpu.roll(bv, 8 - sh, 0)
                m = row < 8 - sh
            else:
                a_sh = pltpu.roll(av, sh, 0)
                b_sh = pltpu.roll(bv, sh, 0)
                m = row >= sh
            bv = jnp.where(m, av * b_sh + bv, bv)
            av = jnp.where(m, av * a_sh, av)
        h = av * carry + bv
        h_ref[pl.ds(off, 8), :] = h
        return h[0:1, :] if reverse else h[7:8, :]

    return lax.fori_loop(0, n_chunks, step, carry)


def _lru_gates(lx, wa_ref, wx_ref, ba, bx, sp):
    lxb = lx.astype(BF16)
    pre_r = jnp.concatenate(
        [_dot(lxb[:, g * LRU_GROUP:(g + 1) * LRU_GROUP], wa_ref[g], NN) for g in range(N_GROUPS)], axis=1)
    pre_i = jnp.concatenate(
        [_dot(lxb[:, g * LRU_GROUP:(g + 1) * LRU_GROUP], wx_ref[g], NN) for g in range(N_GROUPS)], axis=1)
    r = _sigmoid(pre_r + ba)
    ig = _sigmoid(pre_i + bx)
    log_a = (-RG_C * r) * sp
    a = jnp.exp(log_a)
    mult = jnp.sqrt(-jnp.tanh(log_a) * (a * a + 1.0))
    return lxb, r, ig, a, mult


def _mixer_fwd(z, cw, cb, wa_bd, wx_bd, ba, bx, lam, scw, *, tile, name):
    s = z.shape[0]
    n_tiles = s // tile

    def body(z_ref, cw_ref, cb_ref, wa_ref, wx_ref, ba_ref, bx_ref, lam_ref, scw_ref,
             y_ref, hs_ref, ext_lx, ext_q, a_s, b_s, h_car):
        i = pl.program_id(0)

        @pl.when(i == 0)
        def _():
            ext_lx[0:HALO, :] = jnp.zeros((HALO, D_LRU), F32)
            ext_q[0:HALO, :] = jnp.zeros((HALO, D_SC), F32)
            h_car[...] = jnp.zeros_like(h_car)

        ext_lx[HALO:HALO + tile, :] = z_ref[:, 0:D_LRU]
        ext_q[HALO:HALO + tile, :] = z_ref[:, 2 * D_LRU + D_SC:2 * D_LRU + 2 * D_SC] * z_ref[:, 2 * D_LRU + 2 * D_SC:D_IN]
        lx = cb_ref[...] + cw_ref[0:1, :] * ext_lx[pl.ds(HALO - 3, tile), :]
        for k in range(1, 4):
            lx = lx + cw_ref[k:k + 1, :] * ext_lx[pl.ds(HALO - 3 + k, tile), :]
        cq = scw_ref[0:1, :] * ext_q[pl.ds(HALO - 2, tile), :]
        for k in range(1, 3):
            cq = cq + scw_ref[k:k + 1, :] * ext_q[pl.ds(HALO - 2 + k, tile), :]
        ext_lx[0:HALO, :] = ext_lx[tile:tile + HALO, :]
        ext_q[0:HALO, :] = ext_q[tile:tile + HALO, :]

        sp = _softplus(-lam_ref[...])
        _, _, ig, a, mult = _lru_gates(lx, wa_ref, wx_ref, ba_ref[...], bx_ref[...], sp)
        a_s[...] = a
        b_s[...] = mult * (ig * lx)
        h_car[0:1, :] = _scan_rows(a_s, b_s, hs_ref, h_car[0:1, :], rows=tile, reverse=False)

        y_ref[:, 0:D_LRU] = (hs_ref[...] * _gelu(z_ref[:, D_LRU:2 * D_LRU])).astype(BF16)
        y_ref[:, D_LRU:D_MIX] = (z_ref[:, 2 * D_LRU:2 * D_LRU + D_SC] * cq).astype(BF16)

    full = lambda shape: pl.BlockSpec(shape, lambda i: (0,) * len(shape))
    return pl.pallas_call(
        body, name=name, grid=(n_tiles,),
        in_specs=[pl.BlockSpec((tile, D_IN), lambda i: (i, 0)),
                  full((4, D_LRU)), full((1, D_LRU)),
                  full((N_GROUPS, LRU_GROUP, LRU_GROUP)), full((N_GROUPS, LRU_GROUP, LRU_GROUP)),
                  full((1, D_LRU)), full((1, D_LRU)), full((1, D_LRU)), full((3, D_SC))],
        out_specs=[pl.BlockSpec((tile, D_MIX), lambda i: (i, 0)), pl.BlockSpec((tile, D_LRU), lambda i: (i, 0))],
        out_shape=[jax.ShapeDtypeStruct((s, D_MIX), BF16), jax.ShapeDtypeStruct((s, D_LRU), F32)],
        scratch_shapes=[pltpu.VMEM((tile + HALO, D_LRU), F32), pltpu.VMEM((tile + HALO, D_SC), F32),
                        pltpu.VMEM((tile, D_LRU), F32), pltpu.VMEM((tile, D_LRU), F32),
                        pltpu.VMEM((8, D_LRU), F32)],
        compiler_params=_params("arbitrary"),
    )(z, cw, cb.reshape(1, -1), wa_bd, wx_bd, ba.reshape(1, -1), bx.reshape(1, -1), lam.reshape(1, -1), scw)


def _mixer_bwd(z, hs, dy, cw, cb, wa_bd, wx_bd, ba, bx, lam, scw, *, tile, name, token=None):
    s = z.shape[0]
    n_tiles = s // tile
    per8 = tile // 8

    def body(z_ref, zp_ref, hs_ref, hsp_ref, dy_ref, cw_ref, cb_ref, wa_ref, wx_ref, ba_ref, bx_ref, lam_ref, scw_ref,
             token_ref, dz_ref, dcw_ref, dvec_ref, dwa_ref, dwx_ref, dscw_ref,
             ext_lx, ext_q, ext_h, ext_a, ext_dlx, ext_dcq, a_s, b_s, lam_s, l_car):
        i = pl.program_id(0)
        first_tile = i == n_tiles - 1

        @pl.when(i == 0)
        def _():
            for ref in (dcw_ref, dvec_ref, dwa_ref, dwx_ref, dscw_ref, l_car):
                ref[...] = jnp.zeros_like(ref)
            ext_a[tile:tile + HALO, :] = jnp.zeros((HALO, D_LRU), F32)
            ext_dlx[tile:tile + HALO, :] = jnp.zeros((HALO, D_LRU), F32)
            ext_dcq[tile:tile + HALO, :] = jnp.zeros((HALO, D_SC), F32)

        keep = jnp.where(first_tile, 0.0, 1.0)
        sb = z_ref[:, 2 * D_LRU:2 * D_LRU + D_SC]
        sc = z_ref[:, 2 * D_LRU + D_SC:2 * D_LRU + 2 * D_SC]
        sx = z_ref[:, 2 * D_LRU + 2 * D_SC:D_IN]
        ext_lx[0:HALO, :] = zp_ref[:, 0:D_LRU] * keep
        ext_lx[HALO:HALO + tile, :] = z_ref[:, 0:D_LRU]
        ext_q[0:HALO, :] = zp_ref[:, 2 * D_LRU + D_SC:2 * D_LRU + 2 * D_SC] * zp_ref[:, 2 * D_LRU + 2 * D_SC:D_IN] * keep
        ext_q[HALO:HALO + tile, :] = sc * sx
        ext_h[0:HALO, :] = hsp_ref[...] * keep
        ext_h[HALO:HALO + tile, :] = hs_ref[...]

        lx = cb_ref[...] + cw_ref[0:1, :] * ext_lx[pl.ds(HALO - 3, tile), :]
        for k in range(1, 4):
            lx = lx + cw_ref[k:k + 1, :] * ext_lx[pl.ds(HALO - 3 + k, tile), :]
        cq = scw_ref[0:1, :] * ext_q[pl.ds(HALO - 2, tile), :]
        for k in range(1, 3):
            cq = cq + scw_ref[k:k + 1, :] * ext_q[pl.ds(HALO - 2 + k, tile), :]

        sp = _softplus(-lam_ref[...])
        lxb, r, ig, a, mult = _lru_gates(lx, wa_ref, wx_ref, ba_ref[...], bx_ref[...], sp)

        ge, dge = _gelu_parts(z_ref[:, D_LRU:2 * D_LRU])
        dy_lru = dy_ref[:, 0:D_LRU]
        dz_ref[:, D_LRU:2 * D_LRU] = (dy_lru * hs_ref[...] * dge).astype(BF16)

        ext_a[0:tile, :] = a
        a_s[...] = ext_a[pl.ds(1, tile), :]
        b_s[...] = dy_lru * ge
        l_car[0:1, :] = _scan_rows(a_s, b_s, lam_s, l_car[0:1, :], rows=tile, reverse=True)
        ext_a[tile:tile + HALO, :] = ext_a[0:HALO, :]
        lv = lam_s[...]

        da = lv * ext_h[pl.ds(HALO - 1, tile), :]
        d_mult = lv * ig * lx
        d_i = lv * mult * lx
        dlx = lv * mult * ig
        dlog_a = da * a - d_mult * (a * a) / mult
        d_r = dlog_a * (-RG_C * sp)
        dpre_r = d_r * r * (1.0 - r)
        dpre_i = d_i * ig * (1.0 - ig)
        dvec_ref[1:2, :] += jnp.sum(dpre_r, axis=0, keepdims=True)
        dvec_ref[2:3, :] += jnp.sum(dpre_i, axis=0, keepdims=True)
        dvec_ref[3:4, :] += jnp.sum(dlog_a * (-RG_C * r), axis=0, keepdims=True)
        dpr_b = dpre_r.astype(BF16)
        dpi_b = dpre_i.astype(BF16)
        back = []
        for g in range(N_GROUPS):
            cols = slice(g * LRU_GROUP, (g + 1) * LRU_GROUP)
            dwa_ref[g] += _dot(lxb[:, cols], dpr_b[:, cols], TN)
            dwx_ref[g] += _dot(lxb[:, cols], dpi_b[:, cols], TN)
            back.append(_dot(dpr_b[:, cols], wa_ref[g], NT) + _dot(dpi_b[:, cols], wx_ref[g], NT))
        dlx = dlx + jnp.concatenate(back, axis=1)
        dvec_ref[0:1, :] += jnp.sum(dlx, axis=0, keepdims=True)

        ext_dlx[0:tile, :] = dlx
        for k in range(4):
            dcw_ref[k:k + 1, :] += jnp.sum(dlx * ext_lx[pl.ds(HALO - 3 + k, tile), :], axis=0, keepdims=True)
        dlxp = cw_ref[3:4, :] * dlx
        for k in range(3):
            dlxp = dlxp + cw_ref[k:k + 1, :] * ext_dlx[pl.ds(3 - k, tile), :]
        dz_ref[:, 0:D_LRU] = dlxp.astype(BF16)
        ext_dlx[tile:tile + HALO, :] = ext_dlx[0:HALO, :]

        dy_sc = dy_ref[:, D_LRU:D_MIX]
        dz_ref[:, 2 * D_LRU:2 * D_LRU + D_SC] = (dy_sc * cq).astype(BF16)
        dcq = dy_sc * sb
        ext_dcq[0:tile, :] = dcq
        for k in range(3):
            dscw_ref[k:k + 1, :] += jnp.sum(dcq * ext_q[pl.ds(HALO - 2 + k, tile), :], axis=0, keepdims=True)
        dq = scw_ref[2:3, :] * dcq
        for k in range(2):
            dq = dq + scw_ref[k:k + 1, :] * ext_dcq[pl.ds(2 - k, tile), :]
        dz_ref[:, 2 * D_LRU + D_SC:2 * D_LRU + 2 * D_SC] = (dq * sx).astype(BF16)
        dz_ref[:, 2 * D_LRU + 2 * D_SC:D_IN] = (dq * sc).astype(BF16)
        ext_dcq[tile:tile + HALO, :] = ext_dcq[0:HALO, :]

        @pl.when(i == n_tiles - 1)
        def _():
            dvec_ref[3:4, :] = dvec_ref[3:4, :] * (-_sigmoid(-lam_ref[...]))

    rev = lambda i: n_tiles - 1 - i
    prev8 = lambda i: jnp.maximum(rev(i) * per8 - 1, 0)
    full = lambda shape: pl.BlockSpec(shape, lambda i: (0,) * len(shape))
    return pl.pallas_call(
        body, name=name, grid=(n_tiles,),
        in_specs=[pl.BlockSpec((tile, D_IN), lambda i: (rev(i), 0)),
                  pl.BlockSpec((HALO, D_IN), lambda i: (prev8(i), 0)),
                  pl.BlockSpec((tile, D_LRU), lambda i: (rev(i), 0)),
                  pl.BlockSpec((HALO, D_LRU), lambda i: (prev8(i), 0)),
                  pl.BlockSpec((tile, D_MIX), lambda i: (rev(i), 0)),
                  full((4, D_LRU)), full((1, D_LRU)),
                  full((N_GROUPS, LRU_GROUP, LRU_GROUP)), full((N_GROUPS, LRU_GROUP, LRU_GROUP)),
                  full((1, D_LRU)), full((1, D_LRU)), full((1, D_LRU)), full((3, D_SC)),
                  pl.BlockSpec(memory_space=pl.ANY)],
        out_specs=[pl.BlockSpec((tile, D_IN), lambda i: (rev(i), 0)),
                   full((8, D_LRU)), full((8, D_LRU)),
                   full((N_GROUPS, LRU_GROUP, LRU_GROUP)), full((N_GROUPS, LRU_GROUP, LRU_GROUP)),
                   full((8, D_SC))],
        out_shape=[jax.ShapeDtypeStruct((s, D_IN), BF16),
                   jax.ShapeDtypeStruct((8, D_LRU), F32), jax.ShapeDtypeStruct((8, D_LRU), F32),
                   jax.ShapeDtypeStruct((N_GROUPS, LRU_GROUP, LRU_GROUP), F32),
                   jax.ShapeDtypeStruct((N_GROUPS, LRU_GROUP, LRU_GROUP), F32),
                   jax.ShapeDtypeStruct((8, D_SC), F32)],
        scratch_shapes=[pltpu.VMEM((tile + HALO, D_LRU), F32), pltpu.VMEM((tile + HALO, D_SC), F32),
                        pltpu.VMEM((tile + HALO, D_LRU), F32), pltpu.VMEM((tile + HALO, D_LRU), F32),
                        pltpu.VMEM((tile + HALO, D_LRU), F32), pltpu.VMEM((tile + HALO, D_SC), F32),
                        pltpu.VMEM((tile, D_LRU), F32), pltpu.VMEM((tile, D_LRU), F32),
                        pltpu.VMEM((tile, D_LRU), F32), pltpu.VMEM((8, D_LRU), F32)],
        compiler_params=_params("arbitrary"),
    )(z, z, hs, hs, dy, cw, cb.reshape(1, -1), wa_bd, wx_bd, ba.reshape(1, -1), bx.reshape(1, -1),
      lam.reshape(1, -1), scw, _behind(token))


def _ffn_fwd(p, fcw, *, tile, tc, name):
    s = p.shape[1]
    per8 = tile // 8

    def body(p_ref, pp_ref, w_ref, act_ref, ext_p):
        i = pl.program_id(0)
        keep = jnp.where(i == 0, 0.0, 1.0)
        ext_p[:, 0:HALO, :] = pp_ref[...] * keep
        ext_p[:, HALO:HALO + tile, :] = p_ref[...]
        u = []
        for half in range(2):
            acc = w_ref[half, 0:1, :] * ext_p[half, pl.ds(HALO - 2, tile), :]
            for k in range(1, 3):
                acc = acc + w_ref[half, k:k + 1, :] * ext_p[half, pl.ds(HALO - 2 + k, tile), :]
            u.append(acc)
        act_ref[...] = (_gelu(u[0]) * u[1]).astype(BF16)

    return pl.pallas_call(
        body, name=name, grid=(s // tile, D_FF // tc),
        in_specs=[pl.BlockSpec((2, tile, tc), lambda i, j: (0, i, j)),
                  pl.BlockSpec((2, HALO, tc), lambda i, j: (0, jnp.maximum(i * per8 - 1, 0), j)),
                  pl.BlockSpec((2, 3, tc), lambda i, j: (0, 0, j))],
        out_specs=pl.BlockSpec((tile, tc), lambda i, j: (i, j)),
        out_shape=jax.ShapeDtypeStruct((s, D_FF), BF16),
        scratch_shapes=[pltpu.VMEM((2, tile + HALO, tc), F32)],
        compiler_params=_params("parallel", "parallel"),
    )(p, p, fcw)


def _ffn_bwd(p, dact, fcw, *, tile, tc, name, token=None):
    s = p.shape[1]
    n_tiles = s // tile
    per8 = tile // 8
    ext_rows = tile + HALO

    def body(p_ref, pp_ref, pn_ref, da_ref, dan_ref, w_ref, token_ref, dp_ref, dw_ref, ext_p, ext_du):
        i = pl.program_id(1)

        @pl.when(i == 0)
        def _():
            dw_ref[...] = jnp.zeros_like(dw_ref)

        keep_prev = jnp.where(i == 0, 0.0, 1.0)
        keep_next = jnp.where(i == n_tiles - 1, 0.0, 1.0)
        ext_p[:, 0:HALO, :] = pp_ref[...] * keep_prev
        ext_p[:, HALO:HALO + tile, :] = p_ref[...]
        ext_p[:, HALO + tile:2 * HALO + tile, :] = pn_ref[...] * keep_next
        u = []
        for half in range(2):
            acc = w_ref[half, 0:1, :] * ext_p[half, pl.ds(HALO - 2, ext_rows), :]
            for k in range(1, 3):
                acc = acc + w_ref[half, k:k + 1, :] * ext_p[half, pl.ds(HALO - 2 + k, ext_rows), :]
            u.append(acc)
        ge, dge = _gelu_parts(u[0])
        da = jnp.concatenate([da_ref[...], dan_ref[...] * keep_next], axis=0)
        ext_du[0, :, :] = da * u[1] * dge
        ext_du[1, :, :] = da * ge
        for half in range(2):
            du = ext_du[half, 0:tile, :]
            acc = w_ref[half, 2:3, :] * du
            for k in range(2):
                acc = acc + w_ref[half, k:k + 1, :] * ext_du[half, pl.ds(2 - k, tile), :]
            dp_ref[half, :, :] = acc.astype(BF16)
            for k in range(3):
                dw_ref[half, k:k + 1, :] += jnp.sum(du * ext_p[half, pl.ds(HALO - 2 + k, tile), :], axis=0, keepdims=True)

    return pl.pallas_call(
        body, name=name, grid=(D_FF // tc, n_tiles),
        in_specs=[pl.BlockSpec((2, tile, tc), lambda j, i: (0, i, j)),
                  pl.BlockSpec((2, HALO, tc), lambda j, i: (0, jnp.maximum(i * per8 - 1, 0), j)),
                  pl.BlockSpec((2, HALO, tc), lambda j, i: (0, jnp.minimum((i + 1) * per8, n_tiles * per8 - 1), j)),
                  pl.BlockSpec((tile, tc), lambda j, i: (i, j)),
                  pl.BlockSpec((HALO, tc), lambda j, i: (jnp.minimum((i + 1) * per8, n_tiles * per8 - 1), j)),
                  pl.BlockSpec((2, 3, tc), lambda j, i: (0, 0, j)), pl.BlockSpec(memory_space=pl.ANY)],
        out_specs=[pl.BlockSpec((2, tile, tc), lambda j, i: (0, i, j)),
                   pl.BlockSpec((2, 8, tc), lambda j, i: (0, 0, j))],
        out_shape=[jax.ShapeDtypeStruct((2, s, D_FF), BF16), jax.ShapeDtypeStruct((2, 8, D_FF), F32)],
        scratch_shapes=[pltpu.VMEM((2, tile + 2 * HALO, tc), F32), pltpu.VMEM((2, ext_rows, tc), F32)],
        compiler_params=_params("parallel", "arbitrary"),
    )(p, p, p, dact, dact, fcw, _behind(token))


def _adamw_math(w, g, m, v):
    m = ADAM_B1 * m + (1.0 - ADAM_B1) * g
    v = ADAM_B2 * v + (1.0 - ADAM_B2) * (g * g)
    m_hat = m / (1.0 - ADAM_B1 ** ADAM_STEP)
    v_hat = v / (1.0 - ADAM_B2 ** ADAM_STEP)
    delta = -ADAM_LR * (m_hat / (jnp.sqrt(v_hat) + ADAM_EPS) + ADAM_WD * w)
    return delta, m, v


def _adamw(w, g, m, v, *, name):
    rows, cols = w.shape
    tr = rows
    for cand in (512, 256, 128, 64, 32, 16, 8):
        if rows % cand == 0 and rows > cand:
            tr = cand
            break

    def body(w_ref, g_ref, m_ref, v_ref, d_ref, nm_ref, nv_ref):
        d, nm, nv = _adamw_math(w_ref[...], g_ref[...], m_ref[...], v_ref[...])
        d_ref[...] = d
        nm_ref[...] = nm
        nv_ref[...] = nv

    spec = pl.BlockSpec((tr, cols), lambda i: (i, 0))
    return pl.pallas_call(
        body, name=name, grid=(rows // tr,), in_specs=[spec] * 4, out_specs=[spec] * 3,
        out_shape=[jax.ShapeDtypeStruct((rows, cols), F32)] * 3,
        compiler_params=_params("parallel"),
    )(w, g, m, v)


def _sum_parts(parts, *, name):
    _, rows, cols = parts.shape
    tr = rows
    for cand in (256, 128, 64, 32, 16):
        if rows % cand == 0 and rows > cand:
            tr = cand
            break

    def body(p_ref, o_ref):
        acc = p_ref[0].astype(F32)
        for d in range(1, N_DEV):
            acc = acc + p_ref[d].astype(F32)
        o_ref[...] = acc

    return pl.pallas_call(
        body, name=name, grid=(rows // tr,),
        in_specs=[pl.BlockSpec((N_DEV, tr, cols), lambda i: (0, i, 0))],
        out_specs=pl.BlockSpec((tr, cols), lambda i: (i, 0)),
        out_shape=jax.ShapeDtypeStruct((rows, cols), F32),
        compiler_params=_params("parallel"),
    )(parts)


def _place():
    return lax.axis_index("x"), lax.axis_index("y"), lax.axis_index("c")


def _flip(v, bit):
    return 1 - v if bit else v


HBM_SPEC = pl.BlockSpec(memory_space=pltpu.HBM)
SEM_SPEC = pl.BlockSpec(memory_space=pltpu.SEMAPHORE)
ANY_SPEC = pl.BlockSpec(memory_space=pl.ANY)
DATAFLOW = pltpu.SideEffectType.DATAFLOW_SIDE_EFFECTING
N_PEERS = N_DEV - 1


def _peer_copy(k, src_ref, land_ref, send_sem, recv_sem, gather):
    x, y, c = _place()
    my_id = 4 * x + 2 * y + c
    px, py, pc = _flip(x, k & 4), _flip(y, k & 2), _flip(c, k & 1)
    peer_id = 4 * px + 2 * py + pc
    return pltpu.make_async_remote_copy(
        src_ref=src_ref if gather else src_ref.at[peer_id], dst_ref=land_ref.at[my_id],
        send_sem=send_sem.at[k - 1], recv_sem=recv_sem.at[k - 1],
        device_id=(px, py, pc), device_id_type=MESH)


def _copies_start(srcs, *, gather, name):
    n = len(srcs)
    lands = [lax.empty(((N_DEV,) + s.shape) if gather else s.shape, s.dtype) for s in srcs]

    def body(*refs):
        src_refs, land_refs = refs[:n], refs[n:2 * n]
        send_sems, recv_sems = refs[2 * n:3 * n], refs[3 * n:4 * n]
        token, local_sems = refs[6 * n], refs[6 * n + 1]
        x, y, c = _place()
        my_id = 4 * x + 2 * y + c
        own = [pltpu.make_async_copy(src_refs[t] if gather else src_refs[t].at[my_id], land_refs[t].at[my_id],
                                     local_sems.at[t]) for t in range(n)]
        for cp in own:
            cp.start()
        for cp in own:
            cp.wait()
        for t in range(n):
            for k in range(1, N_DEV):
                _peer_copy(k, src_refs[t], land_refs[t], send_sems[t], recv_sems[t], gather).start()
        token[...] = jnp.zeros_like(token)

    sem = pltpu.SemaphoreType.DMA((N_PEERS,))
    outs = pl.pallas_call(
        body, name=name,
        in_specs=[HBM_SPEC] * (2 * n),
        out_specs=[SEM_SPEC] * (2 * n) + [HBM_SPEC] * (2 * n) + [pl.BlockSpec(memory_space=pltpu.VMEM)],
        out_shape=[sem] * (2 * n) + [pltpu.HBM(s.shape, s.dtype) for s in srcs]
        + [pltpu.HBM(l.shape, l.dtype) for l in lands] + [jax.ShapeDtypeStruct((8, 128), F32)],
        input_output_aliases={i: 2 * n + i for i in range(2 * n)},
        scratch_shapes=[pltpu.SemaphoreType.DMA((n,))],
        compiler_params=pltpu.CompilerParams(has_side_effects=DATAFLOW),
    )(*[pltpu.with_memory_space_constraint(s, pltpu.HBM) for s in srcs],
      *[pltpu.with_memory_space_constraint(l, pltpu.HBM) for l in lands])
    handles = [(outs[t], outs[n + t], outs[2 * n + t], outs[3 * n + t]) for t in range(n)]
    return handles, outs[4 * n]


def _copies_wait(handle, after, *, gather, name):
    send_sem, recv_sem, src, land = handle

    def body(src_ref, land_ref, send_ref, recv_ref, after_ref, src_out, land_out):
        for k in range(1, N_DEV):
            cp = _peer_copy(k, src_ref, land_ref, send_ref, recv_ref, gather)
            cp.wait_send()
            cp.wait_recv()

    return pl.pallas_call(
        body, name=name,
        in_specs=[HBM_SPEC, HBM_SPEC, SEM_SPEC, SEM_SPEC, ANY_SPEC],
        out_specs=[HBM_SPEC, HBM_SPEC],
        out_shape=[pltpu.HBM(src.shape, src.dtype), pltpu.HBM(land.shape, land.dtype)],
        input_output_aliases={0: 0, 1: 1},
        compiler_params=pltpu.CompilerParams(has_side_effects=DATAFLOW),
    )(src, land, send_sem, recv_sem, after)[1]


def _all_reduce_small(buf, *, name):
    _, rows, lanes = buf.shape

    def body(in_ref, out_ref, parts, send_sems, recv_sems):
        x, y, c = _place()
        my_id = 4 * x + 2 * y + c
        peers = []
        for k in range(1, N_DEV):
            px, py, pc = _flip(x, k & 4), _flip(y, k & 2), _flip(c, k & 1)
            peers.append(((px, py, pc), 4 * px + 2 * py + pc))
        scatter = [pltpu.make_async_remote_copy(
            src_ref=in_ref.at[pid], dst_ref=parts.at[my_id],
            send_sem=send_sems.at[0, k], recv_sem=recv_sems.at[0, k],
            device_id=peer, device_id_type=MESH) for k, (peer, pid) in enumerate(peers)]
        for cp in scatter:
            cp.start()
        parts[my_id] = in_ref[my_id]
        for cp in scatter:
            cp.wait()
        total = parts[0]
        for d in range(1, N_DEV):
            total = total + parts[d]
        out_ref[my_id] = total
        gather = [pltpu.make_async_remote_copy(
            src_ref=out_ref.at[my_id], dst_ref=out_ref.at[my_id],
            send_sem=send_sems.at[1, k], recv_sem=recv_sems.at[1, k],
            device_id=peer, device_id_type=MESH) for k, (peer, pid) in enumerate(peers)]
        for cp in gather:
            cp.start()
        for k, (peer, pid) in enumerate(peers):
            pltpu.make_async_remote_copy(
                src_ref=out_ref.at[pid], dst_ref=out_ref.at[pid],
                send_sem=send_sems.at[1, k], recv_sem=recv_sems.at[1, k],
                device_id=peer, device_id_type=MESH).wait()

    vmem = pl.BlockSpec(memory_space=pltpu.VMEM)
    return pl.pallas_call(
        body, name=name, in_specs=[vmem], out_specs=vmem,
        out_shape=jax.ShapeDtypeStruct(buf.shape, F32),
        scratch_shapes=[pltpu.VMEM(buf.shape, F32),
                        pltpu.SemaphoreType.DMA((2, 7)), pltpu.SemaphoreType.DMA((2, 7))],
        compiler_params=pltpu.CompilerParams(vmem_limit_bytes=VMEM_LIMIT),
    )(buf)


TM = 512
MIX_TILE = 128
FFN_TILE = 256
FFN_TC = 512


def _block_diag(w):
    wg = w.reshape(N_GROUPS, HEADS_PER_GROUP, LRU_HEAD_DIM, LRU_HEAD_DIM)
    eye = jnp.eye(HEADS_PER_GROUP, dtype=w.dtype)
    bd = wg[:, :, :, None, :] * eye[None, :, None, :, None]
    return bd.reshape(N_GROUPS, LRU_GROUP, LRU_GROUP).astype(BF16)


def _head_blocks(bd):
    b5 = bd.reshape(N_GROUPS, HEADS_PER_GROUP, LRU_HEAD_DIM, HEADS_PER_GROUP, LRU_HEAD_DIM)
    blocks = [b5[:, h, :, h, :] for h in range(HEADS_PER_GROUP)]
    return jnp.stack(blocks, axis=1).reshape(LRU_HEADS, LRU_HEAD_DIM, LRU_HEAD_DIM)


def _w(lw, key, after):
    value = lw[key]
    return value(after) if callable(value) else value


def _layer_fwd(x, lw, tag):
    h1 = _norm_fwd(x, lw["g1"], tm=TM, name=f"norm1_fwd_{tag}")
    z = _mm_nt(h1, _w(lw, "w_in_t", h1), tm=TM, tn=512, tk=D_MODEL, out_dtype=F32, name=f"in_proj_{tag}")
    y_mix, hs = _mixer_fwd(z, _w(lw, "cw", z), lw["cb"], lw["wa_bd"], lw["wx_bd"], lw["ba"], lw["bx"], lw["lam"],
                           _w(lw, "scw", z), tile=MIX_TILE, name=f"mixer_fwd_{tag}")
    x2 = _mm_nn(y_mix, _w(lw, "w_out", y_mix), tm=TM, tn=D_MODEL, tk=D_MIX, out_dtype=F32, name=f"out_proj_{tag}",
                residual=x)
    h2 = _norm_fwd(x2, lw["g2"], tm=TM, name=f"norm2_fwd_{tag}")
    p = _mm_up(h2, _w(lw, "w_up_b", h2), tm=TM, name=f"up_proj_{tag}")
    act = _ffn_fwd(p, _w(lw, "fcw", p), tile=FFN_TILE, tc=FFN_TC, name=f"ffn_fwd_{tag}")
    x3 = _mm_nn(act, _w(lw, "w_down", act), tm=TM, tn=D_MODEL, tk=1024, out_dtype=F32, name=f"down_proj_{tag}",
                residual=x2)
    saved = dict(x=x, h1=h1, z=z, y_mix=y_mix, hs=hs, x2=x2, h2=h2, p=p, act=act)
    return x3, saved


def _layer_bwd(dx3, lw, sv, tag, put):
    w_in_t, w_out, w_up_b, w_down = (_w(lw, k, dx3) for k in ("w_in_t", "w_out", "w_up_b", "w_down"))
    cw, scw, fcw = (_w(lw, k, dx3) for k in ("cw", "scw", "fcw"))
    dact = _mm_nt(dx3, w_down, tm=TM, tn=512, tk=D_MODEL, out_dtype=F32, name=f"down_bwd_x_{tag}")
    g_down = _mm_tn(sv["act"], dx3, tm=512, tn=D_MODEL, tk=512, out_dtype=BF16, name=f"down_bwd_w_{tag}")
    dp, dfcw = _ffn_bwd(sv["p"], dact, fcw, tile=FFN_TILE, tc=FFN_TC, name=f"ffn_bwd_{tag}",
                        token=put("w_down", g_down))
    dh2 = _mm_up_bwd_x(dp, w_up_b, tm=TM, name=f"up_bwd_x_{tag}")
    g_up = _mm_up_bwd_w(sv["h2"], dp, tm=512, tk=512, name=f"up_bwd_w_{tag}")
    dx2, dg2 = _norm_bwd(dh2, sv["x2"], lw["g2"], dx3, tm=TM, name=f"norm2_bwd_{tag}", token=put("w_up_b", g_up))
    dy = _mm_nt(dx2, w_out, tm=TM, tn=512, tk=D_MODEL, out_dtype=F32, name=f"out_bwd_x_{tag}")
    g_out = _mm_tn(sv["y_mix"], dx2, tm=512, tn=D_MODEL, tk=512, out_dtype=BF16, name=f"out_bwd_w_{tag}")
    out_token = put("w_out", g_out)
    dz, dcw, dvec, dwa, dwx, dscw = _mixer_bwd(
        sv["z"], sv["hs"], dy, cw, lw["cb"], lw["wa_bd"], lw["wx_bd"], lw["ba"], lw["bx"], lw["lam"],
        scw, tile=MIX_TILE, name=f"mixer_bwd_{tag}", token=out_token)
    dh1 = _mm_nn(dz, w_in_t, tm=TM, tn=D_MODEL, tk=512, out_dtype=F32, name=f"in_bwd_x_{tag}")
    g_in_t = _mm_tn(dz, sv["h1"], tm=512, tn=D_MODEL, tk=512, out_dtype=BF16, name=f"in_bwd_w_{tag}")
    dx, dg1 = _norm_bwd(dh1, sv["x"], lw["g1"], dx2, tm=TM, name=f"norm1_bwd_{tag}", token=put("w_in_t", g_in_t))
    small = dict(norm1_g=dg1[0], lru_conv_w=dcw[0:4], lru_conv_b=dvec[0], lru_wa=_head_blocks(dwa),
                 lru_ba=dvec[1], lru_wx=_head_blocks(dwx), lru_bx=dvec[2], lru_lambda=dvec[3],
                 sc_conv_w=dscw[0:3], norm2_g=dg2[0], ffn_conv_w=dfcw[:, 0:3, :])
    return dx, small


SMALL_ORDER = ("norm1_g", "lru_conv_w", "lru_conv_b", "lru_wa", "lru_ba", "lru_wx", "lru_bx", "lru_lambda",
               "sc_conv_w", "norm2_g", "ffn_conv_w")


def _local_step(x, tgt, layers, final_g, put):
    saved = []
    h = x
    for l in range(DEPTH):
        h, sv = _layer_fwd(h, layers[l], f"l{l}")
        saved.append(sv)
    loss_blk, dx, dgf = _loss_head(h, final_g, tgt, tm=TM, name="loss_head")
    smalls = [None] * DEPTH
    for l in reversed(range(DEPTH)):
        dx, smalls[l] = _layer_bwd(dx, layers[l], saved[l], f"l{l}", functools.partial(put, l))
    return loss_blk[0, 0], dx, smalls, dgf[0]


def kernel(x, norm1_g, w_in, lru_conv_w, lru_conv_b, lru_wa, lru_ba, lru_wx, lru_bx, lru_lambda, sc_conv_w, w_out, norm2_g, w_up, ffn_conv_w, w_down, final_g, loss_target, m_norm1_g, m_w_in, m_lru_conv_w, m_lru_conv_b, m_lru_wa, m_lru_ba, m_lru_wx, m_lru_bx, m_lru_lambda, m_sc_conv_w, m_w_out, m_norm2_g, m_w_up, m_ffn_conv_w, m_w_down, m_final_g, v_norm1_g, v_w_in, v_lru_conv_w, v_lru_conv_b, v_lru_wa, v_lru_ba, v_lru_wx, v_lru_bx, v_lru_lambda, v_sc_conv_w, v_w_out, v_norm2_g, v_w_up, v_ffn_conv_w, v_w_down, v_final_g):
    names = ["norm1_g", "w_in", "lru_conv_w", "lru_conv_b", "lru_wa", "lru_ba", "lru_wx", "lru_bx", "lru_lambda",
             "sc_conv_w", "w_out", "norm2_g", "w_up", "ffn_conv_w", "w_down", "final_g"]
    w = dict(zip(names, [norm1_g, w_in, lru_conv_w, lru_conv_b, lru_wa, lru_ba, lru_wx, lru_bx, lru_lambda,
                         sc_conv_w, w_out, norm2_g, w_up, ffn_conv_w, w_down, final_g]))
    m = dict(zip(names, [m_norm1_g, m_w_in, m_lru_conv_w, m_lru_conv_b, m_lru_wa, m_lru_ba, m_lru_wx, m_lru_bx,
                         m_lru_lambda, m_sc_conv_w, m_w_out, m_norm2_g, m_w_up, m_ffn_conv_w, m_w_down, m_final_g]))
    v = dict(zip(names, [v_norm1_g, v_w_in, v_lru_conv_w, v_lru_conv_b, v_lru_wa, v_lru_ba, v_lru_wx, v_lru_bx,
                         v_lru_lambda, v_sc_conv_w, v_w_out, v_norm2_g, v_w_up, v_ffn_conv_w, v_w_down, v_final_g]))
    my_id = 4 * lax.axis_index("x") + 2 * lax.axis_index("y") + lax.axis_index("c")

    taps = jnp.zeros((DEPTH, 16, 768), F32)
    taps = taps.at[:, 0:4, 0:128].set(lru_conv_w).at[:, 4:7, 0:64].set(sc_conv_w).at[:, 8:11, :].set(ffn_conv_w)
    shards = {}
    for l in range(DEPTH):
        shards[f"w_in_t{l}"] = jnp.swapaxes(w_in[l], 0, 1).astype(BF16)
        if l == 0:
            shards["taps"] = taps.reshape(DEPTH * 16, 768)
        shards[f"w_out{l}"] = w_out[l].astype(BF16)
        shards[f"w_up_b{l}"] = w_up[l].astype(BF16)
        shards[f"w_down{l}"] = w_down[l].astype(BF16)
    handles, _ = _copies_start(list(shards.values()), gather=True, name="gather_start")
    handle = dict(zip(shards, handles))
    got = {}

    def fetch(key, after):
        if key not in got:
            got[key] = _copies_wait(handle[key], after, gather=True, name=f"gather_wait_{key}")
        return got[key]

    def tap_rows(l, lo, hi, width, after):
        tl = fetch("taps", after).reshape(N_DEV, DEPTH, 16, 768)[:, l, lo:hi, 0:width]
        return jnp.transpose(tl, (1, 0, 2)).reshape(hi - lo, N_DEV * width)

    layers = []
    for l in range(DEPTH):
        layers.append(dict(
            g1=norm1_g[l], g2=norm2_g[l], cb=lru_conv_b[l], ba=lru_ba[l], bx=lru_bx[l], lam=lru_lambda[l],
            wa_bd=_block_diag(lru_wa[l]), wx_bd=_block_diag(lru_wx[l]),
            cw=functools.partial(tap_rows, l, 0, 4, 128), scw=functools.partial(tap_rows, l, 4, 7, 64),
            fcw=lambda after, l=l: tap_rows(l, 8, 11, 768, after).reshape(3, 2, D_FF).transpose(1, 0, 2),
            w_in_t=lambda after, l=l: fetch(f"w_in_t{l}", after).reshape(D_IN, D_MODEL),
            w_out=lambda after, l=l: fetch(f"w_out{l}", after).reshape(D_MIX, D_MODEL),
            w_up_b=lambda after, l=l: fetch(f"w_up_b{l}", after),
            w_down=lambda after, l=l: fetch(f"w_down{l}", after).reshape(D_FF, D_MODEL)))

    scatter_handles = {}

    def put(l, key, grad):
        blocks = grad if grad.ndim == 3 else grad.reshape(N_DEV, grad.shape[0] // N_DEV, grad.shape[1])
        (h,), token = _copies_start([blocks], gather=False, name=f"scatter_start_{key}{l}")
        scatter_handles[(l, key)] = h
        return token

    loss_local, dx, smalls, dgf = _local_step(x[0], loss_target[0], layers, final_g, put)
    loss = lax.psum(loss_local, ("x", "y", "c"))

    parts = []
    for l in range(DEPTH):
        for key in ("w_in_t", "w_out", "w_up_b", "w_down"):
            parts.append(_copies_wait(scatter_handles[(l, key)], dx, gather=False, name=f"scatter_wait_{key}{l}"))

    flat = [smalls[l][k].reshape(-1) for l in range(DEPTH) for k in SMALL_ORDER] + [dgf.reshape(-1)]
    sizes = [f.shape[0] for f in flat]
    total = sum(sizes)
    rows = -(-total // (N_DEV * 128 * 8)) * 8
    flat.append(jnp.zeros((N_DEV * rows * 128 - total,), F32))
    small_sum = _all_reduce_small(jnp.concatenate(flat).reshape(N_DEV, rows, 128), name="reduce_small").reshape(-1)
    small_g, off = [], 0
    for sz in sizes:
        small_g.append(small_sum[off:off + sz])
        off += sz
    gs = {}
    for l in range(DEPTH):
        for i, k in enumerate(SMALL_ORDER):
            gs.setdefault(k, []).append(small_g[l * len(SMALL_ORDER) + i])
    g_final = small_g[-1]

    grads = {}
    per_layer = {k: [] for k in ("w_in", "w_out", "w_up", "w_down")}
    for l in range(DEPTH):
        p_in, p_out, p_up, p_down = parts[4 * l:4 * l + 4]
        per_layer["w_in"].append(jnp.swapaxes(_sum_parts(p_in, name=f"sum_w_in_l{l}"), 0, 1))
        per_layer["w_out"].append(_sum_parts(p_out, name=f"sum_w_out_l{l}"))
        per_layer["w_up"].append(_sum_parts(p_up, name=f"sum_w_up_l{l}"))
        per_layer["w_down"].append(_sum_parts(p_down, name=f"sum_w_down_l{l}"))
    for k, lst in per_layer.items():
        grads[k] = jnp.stack(lst)
    for k in ("norm1_g", "lru_conv_b", "lru_ba", "lru_bx", "lru_lambda", "norm2_g"):
        grads[k] = jnp.stack(gs[k]).reshape(DEPTH, -1)
    for k in ("lru_wa", "lru_wx"):
        grads[k] = jnp.stack(gs[k]).reshape(DEPTH, LRU_HEADS, LRU_HEAD_DIM, LRU_HEAD_DIM)
    grads["final_g"] = g_final
    cw_full = jnp.stack(gs["lru_conv_w"]).reshape(DEPTH, 4, N_DEV, 128)
    grads["lru_conv_w"] = lax.dynamic_index_in_dim(cw_full, my_id, axis=2, keepdims=False)
    scw_full = jnp.stack(gs["sc_conv_w"]).reshape(DEPTH, 3, N_DEV, 64)
    grads["sc_conv_w"] = lax.dynamic_index_in_dim(scw_full, my_id, axis=2, keepdims=False)
    fcw_full = jnp.stack(gs["ffn_conv_w"]).reshape(DEPTH, 2, 3, D_FF).transpose(0, 2, 1, 3).reshape(DEPTH, 3, N_DEV, 768)
    grads["ffn_conv_w"] = lax.dynamic_index_in_dim(fcw_full, my_id, axis=2, keepdims=False)

    deltas, new_m, new_v = {}, {}, {}
    for k in names:
        shape = w[k].shape
        cols = shape[-1]
        as2d = lambda a: a.reshape(-1, cols)
        d, nm, nv = _adamw(as2d(w[k]), as2d(grads[k]), as2d(m[k]), as2d(v[k]), name=f"adamw_{k}")
        deltas[k], new_m[k], new_v[k] = d.reshape(shape), nm.reshape(shape), nv.reshape(shape)

    return (loss, dx[None], *[grads[k] for k in names], *[deltas[k] for k in names],
            *[new_m[k] for k in names], *[new_v[k] for k in names])
```

```python
import functools
import math

import jax
import jax.numpy as jnp
from jax import lax
from jax.experimental import pallas as pl
from jax.experimental.pallas import tpu as pltpu
from jax.experimental.pallas import tpu_sc as plsc

F32 = jnp.float32
BF16 = jnp.bfloat16

N_DEV = 8
DEPTH = 2
D_MODEL = 1024
D_LRU = 1024
D_SC = 512
D_MIX = D_LRU + D_SC
D_IN = 2 * D_LRU + 3 * D_SC
D_FF = 3072
LRU_HEADS = 16
LRU_HEAD_DIM = 64
LRU_GROUP = 256
N_GROUPS = D_LRU // LRU_GROUP
HEADS_PER_GROUP = LRU_GROUP // LRU_HEAD_DIM
RG_C = 8.0
EPS = 1e-6
HALO = 8

ADAM_LR = 0.001
ADAM_B1 = 0.9
ADAM_B2 = 0.999
ADAM_EPS = 1e-08
ADAM_WD = 0.01
ADAM_STEP = 10

GELU_C = math.sqrt(2.0 / math.pi)
GELU_A = 0.044715

VMEM_LIMIT = 56 * 1024 * 1024
MESH = pl.DeviceIdType.MESH


def _params(*sem):
    return pltpu.CompilerParams(dimension_semantics=tuple(sem) if sem else None,
                                vmem_limit_bytes=VMEM_LIMIT)


def _gelu_parts(x):
    x2 = x * x
    t = jnp.tanh(GELU_C * (x + GELU_A * x * x2))
    half = 0.5 * (1.0 + t)
    g = x * half
    dg = half + 0.5 * x * (1.0 - t * t) * (GELU_C * (1.0 + 3.0 * GELU_A * x2))
    return g, dg


def _gelu(x):
    t = jnp.tanh(GELU_C * (x + GELU_A * x * x * x))
    return 0.5 * x * (1.0 + t)


def _sigmoid(x):
    return 1.0 / (1.0 + jnp.exp(-x))


def _softplus(x):
    e = jnp.exp(-jnp.abs(x))
    u = 1.0 + e
    log1p_e = jnp.where(u == 1.0, e, jnp.log(u) * (e / (u - 1.0)))
    return jnp.maximum(x, 0.0) + log1p_e


def _rms(x):
    ms = jnp.mean(x * x, axis=-1, keepdims=True)
    return lax.rsqrt(ms + EPS)


def _dot(a, b, dims):
    return lax.dot_general(a, b, (dims, ((), ())), preferred_element_type=F32)


NN = ((1,), (0,))
NT = ((1,), (1,))
TN = ((0,), (0,))


def _matmul(a, b, *, dims, grid, a_spec, b_spec, o_spec, out_shape, acc_shape, name,
            residual=None, r_spec=None):
    nk = grid[2]

    def body(*refs):
        if residual is None:
            a_ref, b_ref, o_ref, acc_ref = refs
            r_ref = None
        else:
            a_ref, b_ref, r_ref, o_ref, acc_ref = refs
        k = pl.program_id(2)

        @pl.when(k == 0)
        def _():
            acc_ref[...] = jnp.zeros_like(acc_ref)

        acc_ref[...] += _dot(a_ref[...].astype(BF16), b_ref[...].astype(BF16), dims)

        @pl.when(k == nk - 1)
        def _():
            res = acc_ref[...]
            if r_ref is not None:
                res = res + r_ref[...]
            o_ref[...] = res.astype(o_ref.dtype)

    in_specs = [a_spec, b_spec]
    args = [a, b]
    if residual is not None:
        in_specs.append(r_spec)
        args.append(residual)
    return pl.pallas_call(
        body, name=name, grid=grid, in_specs=in_specs, out_specs=o_spec, out_shape=out_shape,
        scratch_shapes=[pltpu.VMEM(acc_shape, F32)],
        compiler_params=_params("parallel", "parallel", "arbitrary"),
    )(*args)


def _mm_nn(a, b, *, tm, tn, tk, out_dtype, name, residual=None):
    m, kd = a.shape
    n = b.shape[1]
    return _matmul(
        a, b, dims=NN, grid=(m // tm, n // tn, kd // tk),
        a_spec=pl.BlockSpec((tm, tk), lambda i, j, k: (i, k)),
        b_spec=pl.BlockSpec((tk, tn), lambda i, j, k: (k, j)),
        o_spec=pl.BlockSpec((tm, tn), lambda i, j, k: (i, j)),
        out_shape=jax.ShapeDtypeStruct((m, n), out_dtype), acc_shape=(tm, tn), name=name,
        residual=residual, r_spec=pl.BlockSpec((tm, tn), lambda i, j, k: (i, j)))


def _mm_nt(a, b, *, tm, tn, tk, out_dtype, name):
    m, kd = a.shape
    n = b.shape[0]
    return _matmul(
        a, b, dims=NT, grid=(m // tm, n // tn, kd // tk),
        a_spec=pl.BlockSpec((tm, tk), lambda i, j, k: (i, k)),
        b_spec=pl.BlockSpec((tn, tk), lambda i, j, k: (j, k)),
        o_spec=pl.BlockSpec((tm, tn), lambda i, j, k: (i, j)),
        out_shape=jax.ShapeDtypeStruct((m, n), out_dtype), acc_shape=(tm, tn), name=name)


def _mm_tn(a, b, *, tm, tn, tk, out_dtype, name):
    kd, m = a.shape
    n = b.shape[1]
    return _matmul(
        a, b, dims=TN, grid=(m // tm, n // tn, kd // tk),
        a_spec=pl.BlockSpec((tk, tm), lambda i, j, k: (k, i)),
        b_spec=pl.BlockSpec((tk, tn), lambda i, j, k: (k, j)),
        o_spec=pl.BlockSpec((tm, tn), lambda i, j, k: (i, j)),
        out_shape=jax.ShapeDtypeStruct((m, n), out_dtype), acc_shape=(tm, tn), name=name)


def _mm_up(h2, w_up_b, *, tm, name):
    s = h2.shape[0]
    nb = w_up_b.shape[2]
    per_half = D_FF // nb
    return _matmul(
        h2, w_up_b, dims=NN, grid=(s // tm, N_DEV, 1),
        a_spec=pl.BlockSpec((tm, D_MODEL), lambda i, j, k: (i, 0)),
        b_spec=pl.BlockSpec((None, D_MODEL, nb), lambda i, j, k: (j, 0, 0)),
        o_spec=pl.BlockSpec((None, tm, nb), lambda i, j, k: (j // per_half, i, j % per_half)),
        out_shape=jax.ShapeDtypeStruct((2, s, D_FF), F32), acc_shape=(tm, nb), name=name)


def _mm_up_bwd_x(dp, w_up_b, *, tm, name):
    s = dp.shape[1]
    nb = w_up_b.shape[2]
    per_half = D_FF // nb
    return _matmul(
        dp, w_up_b, dims=NT, grid=(s // tm, 1, N_DEV),
        a_spec=pl.BlockSpec((None, tm, nb), lambda i, j, k: (k // per_half, i, k % per_half)),
        b_spec=pl.BlockSpec((None, D_MODEL, nb), lambda i, j, k: (k, 0, 0)),
        o_spec=pl.BlockSpec((tm, D_MODEL), lambda i, j, k: (i, 0)),
        out_shape=jax.ShapeDtypeStruct((s, D_MODEL), F32), acc_shape=(tm, D_MODEL), name=name)


def _mm_up_bwd_w(h2, dp, *, tm, tk, name):
    s = h2.shape[0]
    nb = D_FF * 2 // N_DEV
    per_half = D_FF // nb
    return _matmul(
        h2, dp, dims=TN, grid=(D_MODEL // tm, N_DEV, s // tk),
        a_spec=pl.BlockSpec((tk, tm), lambda i, j, k: (k, i)),
        b_spec=pl.BlockSpec((None, tk, nb), lambda i, j, k: (j // per_half, k, j % per_half)),
        o_spec=pl.BlockSpec((None, tm, nb), lambda i, j, k: (j, i, 0)),
        out_shape=jax.ShapeDtypeStruct((N_DEV, D_MODEL, nb), BF16), acc_shape=(tm, nb), name=name)


def _norm_fwd(x, g, *, tm, name):
    s, d = x.shape

    def body(x_ref, g_ref, h_ref):
        xv = x_ref[...]
        h_ref[...] = (xv * _rms(xv) * g_ref[...]).astype(BF16)

    return pl.pallas_call(
        body, name=name, grid=(s // tm,),
        in_specs=[pl.BlockSpec((tm, d), lambda i: (i, 0)), pl.BlockSpec((1, d), lambda i: (0, 0))],
        out_specs=pl.BlockSpec((tm, d), lambda i: (i, 0)),
        out_shape=jax.ShapeDtypeStruct((s, d), BF16),
        compiler_params=_params("parallel"),
    )(x, g.reshape(1, d))


def _behind(token):
    return jnp.zeros((8, 128), F32) if token is None else token


def _norm_bwd(dh, x, g, dres, *, tm, name, token=None):
    s, d = x.shape

    def body(dh_ref, x_ref, g_ref, dres_ref, token_ref, dx_ref, dg_ref):
        @pl.when(pl.program_id(0) == 0)
        def _():
            dg_ref[...] = jnp.zeros_like(dg_ref)

        xv = x_ref[...]
        rstd = _rms(xv)
        n = xv * rstd
        dhv = dh_ref[...]
        dn = dhv * g_ref[...]
        dx = rstd * (dn - n * jnp.mean(dn * n, axis=-1, keepdims=True))
        dx_ref[...] = dres_ref[...] + dx
        dg_ref[0:1, :] += jnp.sum(dhv * n, axis=0, keepdims=True)

    return pl.pallas_call(
        body, name=name, grid=(s // tm,),
        in_specs=[pl.BlockSpec((tm, d), lambda i: (i, 0)), pl.BlockSpec((tm, d), lambda i: (i, 0)),
                  pl.BlockSpec((1, d), lambda i: (0, 0)), pl.BlockSpec((tm, d), lambda i: (i, 0)),
                  pl.BlockSpec(memory_space=pl.ANY)],
        out_specs=[pl.BlockSpec((tm, d), lambda i: (i, 0)), pl.BlockSpec((8, d), lambda i: (0, 0))],
        out_shape=[jax.ShapeDtypeStruct((s, d), F32), jax.ShapeDtypeStruct((8, d), F32)],
        compiler_params=_params("arbitrary"),
    )(dh, x, g.reshape(1, d), dres, _behind(token))


def _loss_head(x, g, tgt, *, tm, name):
    s, d = x.shape

    def body(x_ref, g_ref, t_ref, loss_ref, dx_ref, dg_ref):
        @pl.when(pl.program_id(0) == 0)
        def _():
            dg_ref[...] = jnp.zeros_like(dg_ref)
            loss_ref[...] = jnp.zeros_like(loss_ref)

        xv = x_ref[...]
        gv = g_ref[...]
        rstd = _rms(xv)
        n = xv * rstd
        e = n * gv - t_ref[...]
        part = 0.5 * jnp.sum(jnp.mean(e * e, axis=-1, keepdims=True), axis=0, keepdims=True)
        loss_ref[...] += jnp.broadcast_to(part, loss_ref.shape)
        dy = e * (1.0 / d)
        dn = dy * gv
        dx_ref[...] = rstd * (dn - n * jnp.mean(dn * n, axis=-1, keepdims=True))
        dg_ref[0:1, :] += jnp.sum(dy * n, axis=0, keepdims=True)

    return pl.pallas_call(
        body, name=name, grid=(s // tm,),
        in_specs=[pl.BlockSpec((tm, d), lambda i: (i, 0)), pl.BlockSpec((1, d), lambda i: (0, 0)),
                  pl.BlockSpec((tm, d), lambda i: (i, 0))],
        out_specs=[pl.BlockSpec((8, 128), lambda i: (0, 0)), pl.BlockSpec((tm, d), lambda i: (i, 0)),
                   pl.BlockSpec((8, d), lambda i: (0, 0))],
        out_shape=[jax.ShapeDtypeStruct((8, 128), F32), jax.ShapeDtypeStruct((s, d), F32),
                   jax.ShapeDtypeStruct((8, d), F32)],
        compiler_params=_params("arbitrary"),
    )(x, g.reshape(1, d), tgt)


def _scan_rows(a_ref, b_ref, h_ref, carry, *, rows, reverse):
    width = a_ref.shape[1]
    n_chunks = rows // 8
    row = lax.broadcasted_iota(jnp.int32, (8, width), 0)

    def step(ci, carry):
        chunk = (n_chunks - 1 - ci) if reverse else ci
        off = pl.multiple_of(chunk * 8, 8)
        av = a_ref[pl.ds(off, 8), :]
        bv = b_ref[pl.ds(off, 8), :]
        for sh in (1, 2, 4):
            if reverse:
                a_sh = pltpu.roll(av, 8 - sh, 0)
                b_sh = pltpu.roll(bv, 8 - sh, 0)
                m = row < 8 - sh
            else:
                a_sh = pltpu.roll(av, sh, 0)
                b_sh = pltpu.roll(bv, sh, 0)
                m = row >= sh
            bv = jnp.where(m, av * b_sh + bv, bv)
            av = jnp.where(m, av * a_sh, av)
        h = av * carry + bv
        h_ref[pl.ds(off, 8), :] = h
        return h[0:1, :] if reverse else h[7:8, :]

    return lax.fori_loop(0, n_chunks, step, carry)


def _lru_gates(lx, wa_ref, wx_ref, ba, bx, sp):
    lxb = lx.astype(BF16)
    pre_r = jnp.concatenate(
        [_dot(lxb[:, g * LRU_GROUP:(g + 1) * LRU_GROUP], wa_ref[g], NN) for g in range(N_GROUPS)], axis=1)
    pre_i = jnp.concatenate(
        [_dot(lxb[:, g * LRU_GROUP:(g + 1) * LRU_GROUP], wx_ref[g], NN) for g in range(N_GROUPS)], axis=1)
    r = _sigmoid(pre_r + ba)
    ig = _sigmoid(pre_i + bx)
    log_a = (-RG_C * r) * sp
    a = jnp.exp(log_a)
    mult = jnp.sqrt(-jnp.tanh(log_a) * (a * a + 1.0))
    return lxb, r, ig, a, mult


def _mixer_fwd(z, cw, cb, wa_bd, wx_bd, ba, bx, lam, scw, *, tile, name):
    s = z.shape[0]
    n_tiles = s // tile

    def body(z_ref, cw_ref, cb_ref, wa_ref, wx_ref, ba_ref, bx_ref, lam_ref, scw_ref,
             y_ref, hs_ref, ext_lx, ext_q, a_s, b_s, h_car):
        i = pl.program_id(0)

        @pl.when(i == 0)
        def _():
            ext_lx[0:HALO, :] = jnp.zeros((HALO, D_LRU), F32)
            ext_q[0:HALO, :] = jnp.zeros((HALO, D_SC), F32)
            h_car[...] = jnp.zeros_like(h_car)

        ext_lx[HALO:HALO + tile, :] = z_ref[:, 0:D_LRU]
        ext_q[HALO:HALO + tile, :] = z_ref[:, 2 * D_LRU + D_SC:2 * D_LRU + 2 * D_SC] * z_ref[:, 2 * D_LRU + 2 * D_SC:D_IN]
        lx = cb_ref[...] + cw_ref[0:1, :] * ext_lx[pl.ds(HALO - 3, tile), :]
        for k in range(1, 4):
            lx = lx + cw_ref[k:k + 1, :] * ext_lx[pl.ds(HALO - 3 + k, tile), :]
        cq = scw_ref[0:1, :] * ext_q[pl.ds(HALO - 2, tile), :]
        for k in range(1, 3):
            cq = cq + scw_ref[k:k + 1, :] * ext_q[pl.ds(HALO - 2 + k, tile), :]
        ext_lx[0:HALO, :] = ext_lx[tile:tile + HALO, :]
        ext_q[0:HALO, :] = ext_q[tile:tile + HALO, :]

        sp = _softplus(-lam_ref[...])
        _, _, ig, a, mult = _lru_gates(lx, wa_ref, wx_ref, ba_ref[...], bx_ref[...], sp)
        a_s[...] = a
        b_s[...] = mult * (ig * lx)
        h_car[0:1, :] = _scan_rows(a_s, b_s, hs_ref, h_car[0:1, :], rows=tile, reverse=False)

        y_ref[:, 0:D_LRU] = (hs_ref[...] * _gelu(z_ref[:, D_LRU:2 * D_LRU])).astype(BF16)
        y_ref[:, D_LRU:D_MIX] = (z_ref[:, 2 * D_LRU:2 * D_LRU + D_SC] * cq).astype(BF16)

    full = lambda shape: pl.BlockSpec(shape, lambda i: (0,) * len(shape))
    return pl.pallas_call(
        body, name=name, grid=(n_tiles,),
        in_specs=[pl.BlockSpec((tile, D_IN), lambda i: (i, 0)),
                  full((4, D_LRU)), full((1, D_LRU)),
                  full((N_GROUPS, LRU_GROUP, LRU_GROUP)), full((N_GROUPS, LRU_GROUP, LRU_GROUP)),
                  full((1, D_LRU)), full((1, D_LRU)), full((1, D_LRU)), full((3, D_SC))],
        out_specs=[pl.BlockSpec((tile, D_MIX), lambda i: (i, 0)), pl.BlockSpec((tile, D_LRU), lambda i: (i, 0))],
        out_shape=[jax.ShapeDtypeStruct((s, D_MIX), BF16), jax.ShapeDtypeStruct((s, D_LRU), F32)],
        scratch_shapes=[pltpu.VMEM((tile + HALO, D_LRU), F32), pltpu.VMEM((tile + HALO, D_SC), F32),
                        pltpu.VMEM((tile, D_LRU), F32), pltpu.VMEM((tile, D_LRU), F32),
                        pltpu.VMEM((8, D_LRU), F32)],
        compiler_params=_params("arbitrary"),
    )(z, cw, cb.reshape(1, -1), wa_bd, wx_bd, ba.reshape(1, -1), bx.reshape(1, -1), lam.reshape(1, -1), scw)


def _mixer_bwd(z, hs, dy, cw, cb, wa_bd, wx_bd, ba, bx, lam, scw, *, tile, name, token=None):
    s = z.shape[0]
    n_tiles = s // tile
    per8 = tile // 8

    def body(z_ref, zp_ref, hs_ref, hsp_ref, dy_ref, cw_ref, cb_ref, wa_ref, wx_ref, ba_ref, bx_ref, lam_ref, scw_ref,
             token_ref, dz_ref, dcw_ref, dvec_ref, dwa_ref, dwx_ref, dscw_ref,
             ext_lx, ext_q, ext_h, ext_a, ext_dlx, ext_dcq, a_s, b_s, lam_s, l_car):
        i = pl.program_id(0)
        first_tile = i == n_tiles - 1

        @pl.when(i == 0)
        def _():
            for ref in (dcw_ref, dvec_ref, dwa_ref, dwx_ref, dscw_ref, l_car):
                ref[...] = jnp.zeros_like(ref)
            ext_a[tile:tile + HALO, :] = jnp.zeros((HALO, D_LRU), F32)
            ext_dlx[tile:tile + HALO, :] = jnp.zeros((HALO, D_LRU), F32)
            ext_dcq[tile:tile + HALO, :] = jnp.zeros((HALO, D_SC), F32)

        keep = jnp.where(first_tile, 0.0, 1.0)
        sb = z_ref[:, 2 * D_LRU:2 * D_LRU + D_SC]
        sc = z_ref[:, 2 * D_LRU + D_SC:2 * D_LRU + 2 * D_SC]
        sx = z_ref[:, 2 * D_LRU + 2 * D_SC:D_IN]
        ext_lx[0:HALO, :] = zp_ref[:, 0:D_LRU] * keep
        ext_lx[HALO:HALO + tile, :] = z_ref[:, 0:D_LRU]
        ext_q[0:HALO, :] = zp_ref[:, 2 * D_LRU + D_SC:2 * D_LRU + 2 * D_SC] * zp_ref[:, 2 * D_LRU + 2 * D_SC:D_IN] * keep
        ext_q[HALO:HALO + tile, :] = sc * sx
        ext_h[0:HALO, :] = hsp_ref[...] * keep
        ext_h[HALO:HALO + tile, :] = hs_ref[...]

        lx = cb_ref[...] + cw_ref[0:1, :] * ext_lx[pl.ds(HALO - 3, tile), :]
        for k in range(1, 4):
            lx = lx + cw_ref[k:k + 1, :] * ext_lx[pl.ds(HALO - 3 + k, tile), :]
        cq = scw_ref[0:1, :] * ext_q[pl.ds(HALO - 2, tile), :]
        for k in range(1, 3):
            cq = cq + scw_ref[k:k + 1, :] * ext_q[pl.ds(HALO - 2 + k, tile), :]

        sp = _softplus(-lam_ref[...])
        lxb, r, ig, a, mult = _lru_gates(lx, wa_ref, wx_ref, ba_ref[...], bx_ref[...], sp)

        ge, dge = _gelu_parts(z_ref[:, D_LRU:2 * D_LRU])
        dy_lru = dy_ref[:, 0:D_LRU]
        dz_ref[:, D_LRU:2 * D_LRU] = (dy_lru * hs_ref[...] * dge).astype(BF16)

        ext_a[0:tile, :] = a
        a_s[...] = ext_a[pl.ds(1, tile), :]
        b_s[...] = dy_lru * ge
        l_car[0:1, :] = _scan_rows(a_s, b_s, lam_s, l_car[0:1, :], rows=tile, reverse=True)
        ext_a[tile:tile + HALO, :] = ext_a[0:HALO, :]
        lv = lam_s[...]

        da = lv * ext_h[pl.ds(HALO - 1, tile), :]
        d_mult = lv * ig * lx
        d_i = lv * mult * lx
        dlx = lv * mult * ig
        dlog_a = da * a - d_mult * (a * a) / mult
        d_r = dlog_a * (-RG_C * sp)
        dpre_r = d_r * r * (1.0 - r)
        dpre_i = d_i * ig * (1.0 - ig)
        dvec_ref[1:2, :] += jnp.sum(dpre_r, axis=0, keepdims=True)
        dvec_ref[2:3, :] += jnp.sum(dpre_i, axis=0, keepdims=True)
        dvec_ref[3:4, :] += jnp.sum(dlog_a * (-RG_C * r), axis=0, keepdims=True)
        dpr_b = dpre_r.astype(BF16)
        dpi_b = dpre_i.astype(BF16)
        back = []
        for g in range(N_GROUPS):
            cols = slice(g * LRU_GROUP, (g + 1) * LRU_GROUP)
            dwa_ref[g] += _dot(lxb[:, cols], dpr_b[:, cols], TN)
            dwx_ref[g] += _dot(lxb[:, cols], dpi_b[:, cols], TN)
            back.append(_dot(dpr_b[:, cols], wa_ref[g], NT) + _dot(dpi_b[:, cols], wx_ref[g], NT))
        dlx = dlx + jnp.concatenate(back, axis=1)
        dvec_ref[0:1, :] += jnp.sum(dlx, axis=0, keepdims=True)

        ext_dlx[0:tile, :] = dlx
        for k in range(4):
            dcw_ref[k:k + 1, :] += jnp.sum(dlx * ext_lx[pl.ds(HALO - 3 + k, tile), :], axis=0, keepdims=True)
        dlxp = cw_ref[3:4, :] * dlx
        for k in range(3):
            dlxp = dlxp + cw_ref[k:k + 1, :] * ext_dlx[pl.ds(3 - k, tile), :]
        dz_ref[:, 0:D_LRU] = dlxp.astype(BF16)
        ext_dlx[tile:tile + HALO, :] = ext_dlx[0:HALO, :]

        dy_sc = dy_ref[:, D_LRU:D_MIX]
        dz_ref[:, 2 * D_LRU:2 * D_LRU + D_SC] = (dy_sc * cq).astype(BF16)
        dcq = dy_sc * sb
        ext_dcq[0:tile, :] = dcq
        for k in range(3):
            dscw_ref[k:k + 1, :] += jnp.sum(dcq * ext_q[pl.ds(HALO - 2 + k, tile), :], axis=0, keepdims=True)
        dq = scw_ref[2:3, :] * dcq
        for k in range(2):
            dq = dq + scw_ref[k:k + 1, :] * ext_dcq[pl.ds(2 - k, tile), :]
        dz_ref[:, 2 * D_LRU + D_SC:2 * D_LRU + 2 * D_SC] = (dq * sx).astype(BF16)
        dz_ref[:, 2 * D_LRU + 2 * D_SC:D_IN] = (dq * sc).astype(BF16)
        ext_dcq[tile:tile + HALO, :] = ext_dcq[0:HALO, :]

        @pl.when(i == n_tiles - 1)
        def _():
            dvec_ref[3:4, :] = dvec_ref[3:4, :] * (-_sigmoid(-lam_ref[...]))

    rev = lambda i: n_tiles - 1 - i
    prev8 = lambda i: jnp.maximum(rev(i) * per8 - 1, 0)
    full = lambda shape: pl.BlockSpec(shape, lambda i: (0,) * len(shape))
    return pl.pallas_call(
        body, name=name, grid=(n_tiles,),
        in_specs=[pl.BlockSpec((tile, D_IN), lambda i: (rev(i), 0)),
                  pl.BlockSpec((HALO, D_IN), lambda i: (prev8(i), 0)),
                  pl.BlockSpec((tile, D_LRU), lambda i: (rev(i), 0)),
                  pl.BlockSpec((HALO, D_LRU), lambda i: (prev8(i), 0)),
                  pl.BlockSpec((tile, D_MIX), lambda i: (rev(i), 0)),
                  full((4, D_LRU)), full((1, D_LRU)),
                  full((N_GROUPS, LRU_GROUP, LRU_GROUP)), full((N_GROUPS, LRU_GROUP, LRU_GROUP)),
                  full((1, D_LRU)), full((1, D_LRU)), full((1, D_LRU)), full((3, D_SC)),
                  pl.BlockSpec(memory_space=pl.ANY)],
        out_specs=[pl.BlockSpec((tile, D_IN), lambda i: (rev(i), 0)),
                   full((8, D_LRU)), full((8, D_LRU)),
                   full((N_GROUPS, LRU_GROUP, LRU_GROUP)), full((N_GROUPS, LRU_GROUP, LRU_GROUP)),
                   full((8, D_SC))],
        out_shape=[jax.ShapeDtypeStruct((s, D_IN), BF16),
                   jax.ShapeDtypeStruct((8, D_LRU), F32), jax.ShapeDtypeStruct((8, D_LRU), F32),
                   jax.ShapeDtypeStruct((N_GROUPS, LRU_GROUP, LRU_GROUP), F32),
                   jax.ShapeDtypeStruct((N_GROUPS, LRU_GROUP, LRU_GROUP), F32),
                   jax.ShapeDtypeStruct((8, D_SC), F32)],
        scratch_shapes=[pltpu.VMEM((tile + HALO, D_LRU), F32), pltpu.VMEM((tile + HALO, D_SC), F32),
                        pltpu.VMEM((tile + HALO, D_LRU), F32), pltpu.VMEM((tile + HALO, D_LRU), F32),
                        pltpu.VMEM((tile + HALO, D_LRU), F32), pltpu.VMEM((tile + HALO, D_SC), F32),
                        pltpu.VMEM((tile, D_LRU), F32), pltpu.VMEM((tile, D_LRU), F32),
                        pltpu.VMEM((tile, D_LRU), F32), pltpu.VMEM((8, D_LRU), F32)],
        compiler_params=_params("arbitrary"),
    )(z, z, hs, hs, dy, cw, cb.reshape(1, -1), wa_bd, wx_bd, ba.reshape(1, -1), bx.reshape(1, -1),
      lam.reshape(1, -1), scw, _behind(token))


def _ffn_fwd(p, fcw, *, tile, tc, name):
    s = p.shape[1]
    per8 = tile // 8

    def body(p_ref, pp_ref, w_ref, act_ref, ext_p):
        i = pl.program_id(0)
        keep = jnp.where(i == 0, 0.0, 1.0)
        ext_p[:, 0:HALO, :] = pp_ref[...] * keep
        ext_p[:, HALO:HALO + tile, :] = p_ref[...]
        u = []
        for half in range(2):
            acc = w_ref[half, 0:1, :] * ext_p[half, pl.ds(HALO - 2, tile), :]
            for k in range(1, 3):
                acc = acc + w_ref[half, k:k + 1, :] * ext_p[half, pl.ds(HALO - 2 + k, tile), :]
            u.append(acc)
        act_ref[...] = (_gelu(u[0]) * u[1]).astype(BF16)

    return pl.pallas_call(
        body, name=name, grid=(s // tile, D_FF // tc),
        in_specs=[pl.BlockSpec((2, tile, tc), lambda i, j: (0, i, j)),
                  pl.BlockSpec((2, HALO, tc), lambda i, j: (0, jnp.maximum(i * per8 - 1, 0), j)),
                  pl.BlockSpec((2, 3, tc), lambda i, j: (0, 0, j))],
        out_specs=pl.BlockSpec((tile, tc), lambda i, j: (i, j)),
        out_shape=jax.ShapeDtypeStruct((s, D_FF), BF16),
        scratch_shapes=[pltpu.VMEM((2, tile + HALO, tc), F32)],
        compiler_params=_params("parallel", "parallel"),
    )(p, p, fcw)


def _ffn_bwd(p, dact, fcw, *, tile, tc, name, token=None):
    s = p.shape[1]
    n_tiles = s // tile
    per8 = tile // 8
    ext_rows = tile + HALO

    def body(p_ref, pp_ref, pn_ref, da_ref, dan_ref, w_ref, token_ref, dp_ref, dw_ref, ext_p, ext_du):
        i = pl.program_id(1)

        @pl.when(i == 0)
        def _():
            dw_ref[...] = jnp.zeros_like(dw_ref)

        keep_prev = jnp.where(i == 0, 0.0, 1.0)
        keep_next = jnp.where(i == n_tiles - 1, 0.0, 1.0)
        ext_p[:, 0:HALO, :] = pp_ref[...] * keep_prev
        ext_p[:, HALO:HALO + tile, :] = p_ref[...]
        ext_p[:, HALO + tile:2 * HALO + tile, :] = pn_ref[...] * keep_next
        u = []
        for half in range(2):
            acc = w_ref[half, 0:1, :] * ext_p[half, pl.ds(HALO - 2, ext_rows), :]
            for k in range(1, 3):
                acc = acc + w_ref[half, k:k + 1, :] * ext_p[half, pl.ds(HALO - 2 + k, ext_rows), :]
            u.append(acc)
        ge, dge = _gelu_parts(u[0])
        da = jnp.concatenate([da_ref[...], dan_ref[...] * keep_next], axis=0)
        ext_du[0, :, :] = da * u[1] * dge
        ext_du[1, :, :] = da * ge
        for half in range(2):
            du = ext_du[half, 0:tile, :]
            acc = w_ref[half, 2:3, :] * du
            for k in range(2):
                acc = acc + w_ref[half, k:k + 1, :] * ext_du[half, pl.ds(2 - k, tile), :]
            dp_ref[half, :, :] = acc.astype(BF16)
            for k in range(3):
                dw_ref[half, k:k + 1, :] += jnp.sum(du * ext_p[half, pl.ds(HALO - 2 + k, tile), :], axis=0, keepdims=True)

    return pl.pallas_call(
        body, name=name, grid=(D_FF // tc, n_tiles),
        in_specs=[pl.BlockSpec((2, tile, tc), lambda j, i: (0, i, j)),
                  pl.BlockSpec((2, HALO, tc), lambda j, i: (0, jnp.maximum(i * per8 - 1, 0), j)),
                  pl.BlockSpec((2, HALO, tc), lambda j, i: (0, jnp.minimum((i + 1) * per8, n_tiles * per8 - 1), j)),
                  pl.BlockSpec((tile, tc), lambda j, i: (i, j)),
                  pl.BlockSpec((HALO, tc), lambda j, i: (jnp.minimum((i + 1) * per8, n_tiles * per8 - 1), j)),
                  pl.BlockSpec((2, 3, tc), lambda j, i: (0, 0, j)), pl.BlockSpec(memory_space=pl.ANY)],
        out_specs=[pl.BlockSpec((2, tile, tc), lambda j, i: (0, i, j)),
                   pl.BlockSpec((2, 8, tc), lambda j, i: (0, 0, j))],
        out_shape=[jax.ShapeDtypeStruct((2, s, D_FF), BF16), jax.ShapeDtypeStruct((2, 8, D_FF), F32)],
        scratch_shapes=[pltpu.VMEM((2, tile + 2 * HALO, tc), F32), pltpu.VMEM((2, ext_rows, tc), F32)],
        compiler_params=_params("parallel", "arbitrary"),
    )(p, p, p, dact, dact, fcw, _behind(token))


def _adamw_math(w, g, m, v):
    m = ADAM_B1 * m + (1.0 - ADAM_B1) * g
    v = ADAM_B2 * v + (1.0 - ADAM_B2) * (g * g)
    m_hat = m / (1.0 - ADAM_B1 ** ADAM_STEP)
    v_hat = v / (1.0 - ADAM_B2 ** ADAM_STEP)
    delta = -ADAM_LR * (m_hat / (jnp.sqrt(v_hat) + ADAM_EPS) + ADAM_WD * w)
    return delta, m, v


def _adamw(w, g, m, v, *, name):
    rows, cols = w.shape
    tr = rows
    for cand in (512, 256, 128, 64, 32, 16, 8):
        if rows % cand == 0 and rows > cand:
            tr = cand
            break

    def body(w_ref, g_ref, m_ref, v_ref, d_ref, nm_ref, nv_ref):
        d, nm, nv = _adamw_math(w_ref[...], g_ref[...], m_ref[...], v_ref[...])
        d_ref[...] = d
        nm_ref[...] = nm
        nv_ref[...] = nv

    spec = pl.BlockSpec((tr, cols), lambda i: (i, 0))
    return pl.pallas_call(
        body, name=name, grid=(rows // tr,), in_specs=[spec] * 4, out_specs=[spec] * 3,
        out_shape=[jax.ShapeDtypeStruct((rows, cols), F32)] * 3,
        compiler_params=_params("parallel"),
    )(w, g, m, v)


def _sum_parts(parts, *, name):
    _, rows, cols = parts.shape
    tr = rows
    for cand in (256, 128, 64, 32, 16):
        if rows % cand == 0 and rows > cand:
            tr = cand
            break

    def body(p_ref, o_ref):
        acc = p_ref[0].astype(F32)
        for d in range(1, N_DEV):
            acc = acc + p_ref[d].astype(F32)
        o_ref[...] = acc

    return pl.pallas_call(
        body, name=name, grid=(rows // tr,),
        in_specs=[pl.BlockSpec((N_DEV, tr, cols), lambda i: (0, i, 0))],
        out_specs=pl.BlockSpec((tr, cols), lambda i: (i, 0)),
        out_shape=jax.ShapeDtypeStruct((rows, cols), F32),
        compiler_params=_params("parallel"),
    )(parts)


def _place():
    return lax.axis_index("x"), lax.axis_index("y"), lax.axis_index("c")


def _flip(v, bit):
    return 1 - v if bit else v


HBM_SPEC = pl.BlockSpec(memory_space=pltpu.HBM)
SEM_SPEC = pl.BlockSpec(memory_space=pltpu.SEMAPHORE)
ANY_SPEC = pl.BlockSpec(memory_space=pl.ANY)
DATAFLOW = pltpu.SideEffectType.DATAFLOW_SIDE_EFFECTING
N_PEERS = N_DEV - 1


def _peer_copy(k, src_ref, land_ref, send_sem, recv_sem, gather):
    x, y, c = _place()
    my_id = 4 * x + 2 * y + c
    px, py, pc = _flip(x, k & 4), _flip(y, k & 2), _flip(c, k & 1)
    peer_id = 4 * px + 2 * py + pc
    return pltpu.make_async_remote_copy(
        src_ref=src_ref if gather else src_ref.at[peer_id], dst_ref=land_ref.at[my_id],
        send_sem=send_sem.at[k - 1], recv_sem=recv_sem.at[k - 1],
        device_id=(px, py, pc), device_id_type=MESH)


def _copies_start(srcs, *, gather, name):
    n = len(srcs)
    lands = [lax.empty(((N_DEV,) + s.shape) if gather else s.shape, s.dtype) for s in srcs]

    def body(*refs):
        src_refs, land_refs = refs[:n], refs[n:2 * n]
        send_sems, recv_sems = refs[2 * n:3 * n], refs[3 * n:4 * n]
        token, local_sems = refs[6 * n], refs[6 * n + 1]
        x, y, c = _place()
        my_id = 4 * x + 2 * y + c
        own = [pltpu.make_async_copy(src_refs[t] if gather else src_refs[t].at[my_id], land_refs[t].at[my_id],
                                     local_sems.at[t]) for t in range(n)]
        for cp in own:
            cp.start()
        for cp in own:
            cp.wait()
        for t in range(n):
            for k in range(1, N_DEV):
                _peer_copy(k, src_refs[t], land_refs[t], send_sems[t], recv_sems[t], gather).start()
        token[...] = jnp.zeros_like(token)

    sem = pltpu.SemaphoreType.DMA((N_PEERS,))
    outs = pl.pallas_call(
        body, name=name,
        in_specs=[HBM_SPEC] * (2 * n),
        out_specs=[SEM_SPEC] * (2 * n) + [HBM_SPEC] * (2 * n) + [pl.BlockSpec(memory_space=pltpu.VMEM)],
        out_shape=[sem] * (2 * n) + [pltpu.HBM(s.shape, s.dtype) for s in srcs]
        + [pltpu.HBM(l.shape, l.dtype) for l in lands] + [jax.ShapeDtypeStruct((8, 128), F32)],
        input_output_aliases={i: 2 * n + i for i in range(2 * n)},
        scratch_shapes=[pltpu.SemaphoreType.DMA((n,))],
        compiler_params=pltpu.CompilerParams(has_side_effects=DATAFLOW),
    )(*[pltpu.with_memory_space_constraint(s, pltpu.HBM) for s in srcs],
      *[pltpu.with_memory_space_constraint(l, pltpu.HBM) for l in lands])
    handles = [(outs[t], outs[n + t], outs[2 * n + t], outs[3 * n + t]) for t in range(n)]
    return handles, outs[4 * n]


def _copies_wait(handle, after, *, gather, name):
    send_sem, recv_sem, src, land = handle

    def body(src_ref, land_ref, send_ref, recv_ref, after_ref, src_out, land_out):
        for k in range(1, N_DEV):
            cp = _peer_copy(k, src_ref, land_ref, send_ref, recv_ref, gather)
            cp.wait_send()
            cp.wait_recv()

    return pl.pallas_call(
        body, name=name,
        in_specs=[HBM_SPEC, HBM_SPEC, SEM_SPEC, SEM_SPEC, ANY_SPEC],
        out_specs=[HBM_SPEC, HBM_SPEC],
        out_shape=[pltpu.HBM(src.shape, src.dtype), pltpu.HBM(land.shape, land.dtype)],
        input_output_aliases={0: 0, 1: 1},
        compiler_params=pltpu.CompilerParams(has_side_effects=DATAFLOW),
    )(src, land, send_sem, recv_sem, after)[1]


def _sequencer_copies(srcs, *, gather, name, collective_id):
    n = len(srcs)
    hbm = pltpu.MemorySpace.HBM
    src_refs = [jax.new_ref(s, memory_space=hbm) for s in srcs]
    land_refs = [jax.empty_ref(jax.ShapeDtypeStruct(((N_DEV,) + s.shape) if gather else s.shape, s.dtype),
                               memory_space=hbm) for s in srcs]

    @pl.kernel(mesh=plsc.ScalarSubcoreMesh(axis_name="seq", num_cores=1), name=name,
               scratch_types=(pltpu.SemaphoreType.DMA((n, N_PEERS)), pltpu.SemaphoreType.DMA((n, N_PEERS)),
                              pltpu.SemaphoreType.DMA((n,))),
               compiler_params=pltpu.CompilerParams(collective_id=collective_id))
    def launch(send_sems, recv_sems, local_sems):
        x, y, c = _place()
        my_id = 4 * x + 2 * y + c
        barrier = pltpu.get_barrier_semaphore()
        for k in range(1, N_DEV):
            peer = (_flip(x, k & 4), _flip(y, k & 2), _flip(c, k & 1))
            pl.semaphore_signal(barrier, inc=1, device_id=peer, device_id_type=MESH)
        pl.semaphore_wait(barrier, N_PEERS)
        own = [pltpu.make_async_copy(src_refs[t] if gather else src_refs[t].at[my_id], land_refs[t].at[my_id],
                                     local_sems.at[t]) for t in range(n)]
        for cp in own:
            cp.start()
        copies = [_peer_copy(k, src_refs[t], land_refs[t], send_sems.at[t], recv_sems.at[t], gather)
                  for t in range(n) for k in range(1, N_DEV)]
        for cp in copies:
            cp.start()
        for cp in own:
            cp.wait()
        for cp in copies:
            cp.wait()

    launch()
    return [ref[...] for ref in land_refs]


def _all_reduce_small(buf, *, name):
    _, rows, lanes = buf.shape

    def body(in_ref, out_ref, parts, send_sems, recv_sems):
        x, y, c = _place()
        my_id = 4 * x + 2 * y + c
        peers = []
        for k in range(1, N_DEV):
            px, py, pc = _flip(x, k & 4), _flip(y, k & 2), _flip(c, k & 1)
            peers.append(((px, py, pc), 4 * px + 2 * py + pc))
        scatter = [pltpu.make_async_remote_copy(
            src_ref=in_ref.at[pid], dst_ref=parts.at[my_id],
            send_sem=send_sems.at[0, k], recv_sem=recv_sems.at[0, k],
            device_id=peer, device_id_type=MESH) for k, (peer, pid) in enumerate(peers)]
        for cp in scatter:
            cp.start()
        parts[my_id] = in_ref[my_id]
        for cp in scatter:
            cp.wait()
        total = parts[0]
        for d in range(1, N_DEV):
            total = total + parts[d]
        out_ref[my_id] = total
        gather = [pltpu.make_async_remote_copy(
            src_ref=out_ref.at[my_id], dst_ref=out_ref.at[my_id],
            send_sem=send_sems.at[1, k], recv_sem=recv_sems.at[1, k],
            device_id=peer, device_id_type=MESH) for k, (peer, pid) in enumerate(peers)]
        for cp in gather:
            cp.start()
        for k, (peer, pid) in enumerate(peers):
            pltpu.make_async_remote_copy(
                src_ref=out_ref.at[pid], dst_ref=out_ref.at[pid],
                send_sem=send_sems.at[1, k], recv_sem=recv_sems.at[1, k],
                device_id=peer, device_id_type=MESH).wait()

    vmem = pl.BlockSpec(memory_space=pltpu.VMEM)
    return pl.pallas_call(
        body, name=name, in_specs=[vmem], out_specs=vmem,
        out_shape=jax.ShapeDtypeStruct(buf.shape, F32),
        scratch_shapes=[pltpu.VMEM(buf.shape, F32),
                        pltpu.SemaphoreType.DMA((2, 7)), pltpu.SemaphoreType.DMA((2, 7))],
        compiler_params=pltpu.CompilerParams(vmem_limit_bytes=VMEM_LIMIT),
    )(buf)


TM = 512
MIX_TILE = 128
FFN_TILE = 256
FFN_TC = 512


def _block_diag(w):
    wg = w.reshape(N_GROUPS, HEADS_PER_GROUP, LRU_HEAD_DIM, LRU_HEAD_DIM)
    eye = jnp.eye(HEADS_PER_GROUP, dtype=w.dtype)
    bd = wg[:, :, :, None, :] * eye[None, :, None, :, None]
    return bd.reshape(N_GROUPS, LRU_GROUP, LRU_GROUP).astype(BF16)


def _head_blocks(bd):
    b5 = bd.reshape(N_GROUPS, HEADS_PER_GROUP, LRU_HEAD_DIM, HEADS_PER_GROUP, LRU_HEAD_DIM)
    blocks = [b5[:, h, :, h, :] for h in range(HEADS_PER_GROUP)]
    return jnp.stack(blocks, axis=1).reshape(LRU_HEADS, LRU_HEAD_DIM, LRU_HEAD_DIM)


def _w(lw, key, after):
    value = lw[key]
    return value(after) if callable(value) else value


def _layer_fwd(x, lw, tag):
    h1 = _norm_fwd(x, lw["g1"], tm=TM, name=f"norm1_fwd_{tag}")
    z = _mm_nt(h1, _w(lw, "w_in_t", h1), tm=TM, tn=512, tk=D_MODEL, out_dtype=F32, name=f"in_proj_{tag}")
    y_mix, hs = _mixer_fwd(z, _w(lw, "cw", z), lw["cb"], lw["wa_bd"], lw["wx_bd"], lw["ba"], lw["bx"], lw["lam"],
                           _w(lw, "scw", z), tile=MIX_TILE, name=f"mixer_fwd_{tag}")
    x2 = _mm_nn(y_mix, _w(lw, "w_out", y_mix), tm=TM, tn=D_MODEL, tk=D_MIX, out_dtype=F32, name=f"out_proj_{tag}",
                residual=x)
    h2 = _norm_fwd(x2, lw["g2"], tm=TM, name=f"norm2_fwd_{tag}")
    p = _mm_up(h2, _w(lw, "w_up_b", h2), tm=TM, name=f"up_proj_{tag}")
    act = _ffn_fwd(p, _w(lw, "fcw", p), tile=FFN_TILE, tc=FFN_TC, name=f"ffn_fwd_{tag}")
    x3 = _mm_nn(act, _w(lw, "w_down", act), tm=TM, tn=D_MODEL, tk=1024, out_dtype=F32, name=f"down_proj_{tag}",
                residual=x2)
    saved = dict(x=x, h1=h1, z=z, y_mix=y_mix, hs=hs, x2=x2, h2=h2, p=p, act=act)
    return x3, saved


def _layer_bwd(dx3, lw, sv, tag, put):
    w_in_t, w_out, w_up_b, w_down = (_w(lw, k, dx3) for k in ("w_in_t", "w_out", "w_up_b", "w_down"))
    cw, scw, fcw = (_w(lw, k, dx3) for k in ("cw", "scw", "fcw"))
    dact = _mm_nt(dx3, w_down, tm=TM, tn=512, tk=D_MODEL, out_dtype=F32, name=f"down_bwd_x_{tag}")
    g_down = _mm_tn(sv["act"], dx3, tm=512, tn=D_MODEL, tk=512, out_dtype=BF16, name=f"down_bwd_w_{tag}")
    dp, dfcw = _ffn_bwd(sv["p"], dact, fcw, tile=FFN_TILE, tc=FFN_TC, name=f"ffn_bwd_{tag}",
                        token=put("w_down", g_down))
    dh2 = _mm_up_bwd_x(dp, w_up_b, tm=TM, name=f"up_bwd_x_{tag}")
    g_up = _mm_up_bwd_w(sv["h2"], dp, tm=512, tk=512, name=f"up_bwd_w_{tag}")
    dx2, dg2 = _norm_bwd(dh2, sv["x2"], lw["g2"], dx3, tm=TM, name=f"norm2_bwd_{tag}", token=put("w_up_b", g_up))
    dy = _mm_nt(dx2, w_out, tm=TM, tn=512, tk=D_MODEL, out_dtype=F32, name=f"out_bwd_x_{tag}")
    g_out = _mm_tn(sv["y_mix"], dx2, tm=512, tn=D_MODEL, tk=512, out_dtype=BF16, name=f"out_bwd_w_{tag}")
    out_token = put("w_out", g_out)
    dz, dcw, dvec, dwa, dwx, dscw = _mixer_bwd(
        sv["z"], sv["hs"], dy, cw, lw["cb"], lw["wa_bd"], lw["wx_bd"], lw["ba"], lw["bx"], lw["lam"],
        scw, tile=MIX_TILE, name=f"mixer_bwd_{tag}", token=out_token)
    dh1 = _mm_nn(dz, w_in_t, tm=TM, tn=D_MODEL, tk=512, out_dtype=F32, name=f"in_bwd_x_{tag}")
    g_in_t = _mm_tn(dz, sv["h1"], tm=512, tn=D_MODEL, tk=512, out_dtype=BF16, name=f"in_bwd_w_{tag}")
    dx, dg1 = _norm_bwd(dh1, sv["x"], lw["g1"], dx2, tm=TM, name=f"norm1_bwd_{tag}", token=put("w_in_t", g_in_t))
    small = dict(norm1_g=dg1[0], lru_conv_w=dcw[0:4], lru_conv_b=dvec[0], lru_wa=_head_blocks(dwa),
                 lru_ba=dvec[1], lru_wx=_head_blocks(dwx), lru_bx=dvec[2], lru_lambda=dvec[3],
                 sc_conv_w=dscw[0:3], norm2_g=dg2[0], ffn_conv_w=dfcw[:, 0:3, :])
    return dx, small


SMALL_ORDER = ("norm1_g", "lru_conv_w", "lru_conv_b", "lru_wa", "lru_ba", "lru_wx", "lru_bx", "lru_lambda",
               "sc_conv_w", "norm2_g", "ffn_conv_w")


def _local_step(x, tgt, layers, final_g, put):
    saved = []
    h = x
    for l in range(DEPTH):
        h, sv = _layer_fwd(h, layers[l], f"l{l}")
        saved.append(sv)
    loss_blk, dx, dgf = _loss_head(h, final_g, tgt, tm=TM, name="loss_head")
    smalls = [None] * DEPTH
    for l in reversed(range(DEPTH)):
        dx, smalls[l] = _layer_bwd(dx, layers[l], saved[l], f"l{l}", functools.partial(put, l))
    return loss_blk[0, 0], dx, smalls, dgf[0]


def kernel(x, norm1_g, w_in, lru_conv_w, lru_conv_b, lru_wa, lru_ba, lru_wx, lru_bx, lru_lambda, sc_conv_w, w_out, norm2_g, w_up, ffn_conv_w, w_down, final_g, loss_target, m_norm1_g, m_w_in, m_lru_conv_w, m_lru_conv_b, m_lru_wa, m_lru_ba, m_lru_wx, m_lru_bx, m_lru_lambda, m_sc_conv_w, m_w_out, m_norm2_g, m_w_up, m_ffn_conv_w, m_w_down, m_final_g, v_norm1_g, v_w_in, v_lru_conv_w, v_lru_conv_b, v_lru_wa, v_lru_ba, v_lru_wx, v_lru_bx, v_lru_lambda, v_sc_conv_w, v_w_out, v_norm2_g, v_w_up, v_ffn_conv_w, v_w_down, v_final_g):
    names = ["norm1_g", "w_in", "lru_conv_w", "lru_conv_b", "lru_wa", "lru_ba", "lru_wx", "lru_bx", "lru_lambda",
             "sc_conv_w", "w_out", "norm2_g", "w_up", "ffn_conv_w", "w_down", "final_g"]
    w = dict(zip(names, [norm1_g, w_in, lru_conv_w, lru_conv_b, lru_wa, lru_ba, lru_wx, lru_bx, lru_lambda,
                         sc_conv_w, w_out, norm2_g, w_up, ffn_conv_w, w_down, final_g]))
    m = dict(zip(names, [m_norm1_g, m_w_in, m_lru_conv_w, m_lru_conv_b, m_lru_wa, m_lru_ba, m_lru_wx, m_lru_bx,
                         m_lru_lambda, m_sc_conv_w, m_w_out, m_norm2_g, m_w_up, m_ffn_conv_w, m_w_down, m_final_g]))
    v = dict(zip(names, [v_norm1_g, v_w_in, v_lru_conv_w, v_lru_conv_b, v_lru_wa, v_lru_ba, v_lru_wx, v_lru_bx,
                         v_lru_lambda, v_sc_conv_w, v_w_out, v_norm2_g, v_w_up, v_ffn_conv_w, v_w_down, v_final_g]))
    my_id = 4 * lax.axis_index("x") + 2 * lax.axis_index("y") + lax.axis_index("c")

    taps = jnp.zeros((DEPTH, 16, 768), F32)
    taps = taps.at[:, 0:4, 0:128].set(lru_conv_w).at[:, 4:7, 0:64].set(sc_conv_w).at[:, 8:11, :].set(ffn_conv_w)
    shards = {}
    for l in range(DEPTH):
        shards[f"w_in_t{l}"] = jnp.swapaxes(w_in[l], 0, 1).astype(BF16)
        if l == 0:
            shards["taps"] = taps.reshape(DEPTH * 16, 768)
        shards[f"w_out{l}"] = w_out[l].astype(BF16)
        shards[f"w_up_b{l}"] = w_up[l].astype(BF16)
        shards[f"w_down{l}"] = w_down[l].astype(BF16)
    ids = iter(range(16))
    got = {}
    for group in (("w_in_t0", "taps", "w_out0"), ("w_up_b0", "w_down0"),
                  ("w_in_t1", "w_out1", "w_up_b1", "w_down1")):
        lands = _sequencer_copies([shards[k] for k in group], gather=True, name=f"gather_{group[0]}",
                                  collective_id=next(ids))
        got.update(zip(group, lands))

    def fetch(key, after):
        return got[key]

    def tap_rows(l, lo, hi, width, after):
        tl = fetch("taps", after).reshape(N_DEV, DEPTH, 16, 768)[:, l, lo:hi, 0:width]
        return jnp.transpose(tl, (1, 0, 2)).reshape(hi - lo, N_DEV * width)

    layers = []
    for l in range(DEPTH):
        layers.append(dict(
            g1=norm1_g[l], g2=norm2_g[l], cb=lru_conv_b[l], ba=lru_ba[l], bx=lru_bx[l], lam=lru_lambda[l],
            wa_bd=_block_diag(lru_wa[l]), wx_bd=_block_diag(lru_wx[l]),
            cw=functools.partial(tap_rows, l, 0, 4, 128), scw=functools.partial(tap_rows, l, 4, 7, 64),
            fcw=lambda after, l=l: tap_rows(l, 8, 11, 768, after).reshape(3, 2, D_FF).transpose(1, 0, 2),
            w_in_t=lambda after, l=l: fetch(f"w_in_t{l}", after).reshape(D_IN, D_MODEL),
            w_out=lambda after, l=l: fetch(f"w_out{l}", after).reshape(D_MIX, D_MODEL),
            w_up_b=lambda after, l=l: fetch(f"w_up_b{l}", after),
            w_down=lambda after, l=l: fetch(f"w_down{l}", after).reshape(D_FF, D_MODEL)))

    scatter_handles = {}

    def put(l, key, grad):
        blocks = grad if grad.ndim == 3 else grad.reshape(N_DEV, grad.shape[0] // N_DEV, grad.shape[1])
        (scatter_handles[(l, key)],) = _sequencer_copies([blocks], gather=False, name=f"scatter_{key}{l}",
                                                         collective_id=next(ids))
        return blocks

    loss_local, dx, smalls, dgf = _local_step(x[0], loss_target[0], layers, final_g, put)
    loss = lax.psum(loss_local, ("x", "y", "c"))

    parts = []
    for l in range(DEPTH):
        for key in ("w_in_t", "w_out", "w_up_b", "w_down"):
            parts.append(scatter_handles[(l, key)])

    flat = [smalls[l][k].reshape(-1) for l in range(DEPTH) for k in SMALL_ORDER] + [dgf.reshape(-1)]
    sizes = [f.shape[0] for f in flat]
    total = sum(sizes)
    rows = -(-total // (N_DEV * 128 * 8)) * 8
    flat.append(jnp.zeros((N_DEV * rows * 128 - total,), F32))
    small_sum = _all_reduce_small(jnp.concatenate(flat).reshape(N_DEV, rows, 128), name="reduce_small").reshape(-1)
    small_g, off = [], 0
    for sz in sizes:
        small_g.append(small_sum[off:off + sz])
        off += sz
    gs = {}
    for l in range(DEPTH):
        for i, k in enumerate(SMALL_ORDER):
            gs.setdefault(k, []).append(small_g[l * len(SMALL_ORDER) + i])
    g_final = small_g[-1]

    grads = {}
    per_layer = {k: [] for k in ("w_in", "w_out", "w_up", "w_down")}
    for l in range(DEPTH):
        p_in, p_out, p_up, p_down = parts[4 * l:4 * l + 4]
        per_layer["w_in"].append(jnp.swapaxes(_sum_parts(p_in, name=f"sum_w_in_l{l}"), 0, 1))
        per_layer["w_out"].append(_sum_parts(p_out, name=f"sum_w_out_l{l}"))
        per_layer["w_up"].append(_sum_parts(p_up, name=f"sum_w_up_l{l}"))
        per_layer["w_down"].append(_sum_parts(p_down, name=f"sum_w_down_l{l}"))
    for k, lst in per_layer.items():
        grads[k] = jnp.stack(lst)
    for k in ("norm1_g", "lru_conv_b", "lru_ba", "lru_bx", "lru_lambda", "norm2_g"):
        grads[k] = jnp.stack(gs[k]).reshape(DEPTH, -1)
    for k in ("lru_wa", "lru_wx"):
        grads[k] = jnp.stack(gs[k]).reshape(DEPTH, LRU_HEADS, LRU_HEAD_DIM, LRU_HEAD_DIM)
    grads["final_g"] = g_final
    cw_full = jnp.stack(gs["lru_conv_w"]).reshape(DEPTH, 4, N_DEV, 128)
    grads["lru_conv_w"] = lax.dynamic_index_in_dim(cw_full, my_id, axis=2, keepdims=False)
    scw_full = jnp.stack(gs["sc_conv_w"]).reshape(DEPTH, 3, N_DEV, 64)
    grads["sc_conv_w"] = lax.dynamic_index_in_dim(scw_full, my_id, axis=2, keepdims=False)
    fcw_full = jnp.stack(gs["ffn_conv_w"]).reshape(DEPTH, 2, 3, D_FF).transpose(0, 2, 1, 3).reshape(DEPTH, 3, N_DEV, 768)
    grads["ffn_conv_w"] = lax.dynamic_index_in_dim(fcw_full, my_id, axis=2, keepdims=False)

    deltas, new_m, new_v = {}, {}, {}
    for k in names:
        shape = w[k].shape
        cols = shape[-1]
        as2d = lambda a: a.reshape(-1, cols)
        d, nm, nv = _adamw(as2d(w[k]), as2d(grads[k]), as2d(m[k]), as2d(v[k]), name=f"adamw_{k}")
        deltas[k], new_m[k], new_v[k] = d.reshape(shape), nm.reshape(shape), nv.reshape(shape)

    return (loss, dx[None], *[grads[k] for k in names], *[deltas[k] for k in names],
            *[new_m[k] for k in names], *[new_v[k] for k in names])
```

```python
import functools
import math

import jax
import jax.numpy as jnp
from jax import lax
from jax.experimental import pallas as pl
from jax.experimental.pallas import tpu as pltpu
from jax.experimental.pallas import tpu_sc as plsc

F32 = jnp.float32
BF16 = jnp.bfloat16

N_DEV = 8
DEPTH = 2
D_MODEL = 1024
D_LRU = 1024
D_SC = 512
D_MIX = D_LRU + D_SC
D_IN = 2 * D_LRU + 3 * D_SC
D_FF = 3072
LRU_HEADS = 16
LRU_HEAD_DIM = 64
LRU_GROUP = 256
N_GROUPS = D_LRU // LRU_GROUP
HEADS_PER_GROUP = LRU_GROUP // LRU_HEAD_DIM
RG_C = 8.0
EPS = 1e-6
HALO = 8

ADAM_LR = 0.001
ADAM_B1 = 0.9
ADAM_B2 = 0.999
ADAM_EPS = 1e-08
ADAM_WD = 0.01
ADAM_STEP = 10

GELU_C = math.sqrt(2.0 / math.pi)
GELU_A = 0.044715

VMEM_LIMIT = 56 * 1024 * 1024
MESH = pl.DeviceIdType.MESH


def _params(*sem):
    return pltpu.CompilerParams(dimension_semantics=tuple(sem) if sem else None,
                                vmem_limit_bytes=VMEM_LIMIT)


def _gelu_parts(x):
    x2 = x * x
    t = jnp.tanh(GELU_C * (x + GELU_A * x * x2))
    half = 0.5 * (1.0 + t)
    g = x * half
    dg = half + 0.5 * x * (1.0 - t * t) * (GELU_C * (1.0 + 3.0 * GELU_A * x2))
    return g, dg


def _gelu(x):
    t = jnp.tanh(GELU_C * (x + GELU_A * x * x * x))
    return 0.5 * x * (1.0 + t)


def _sigmoid(x):
    return 1.0 / (1.0 + jnp.exp(-x))


def _softplus(x):
    e = jnp.exp(-jnp.abs(x))
    u = 1.0 + e
    log1p_e = jnp.where(u == 1.0, e, jnp.log(u) * (e / (u - 1.0)))
    return jnp.maximum(x, 0.0) + log1p_e


def _rms(x):
    ms = jnp.mean(x * x, axis=-1, keepdims=True)
    return lax.rsqrt(ms + EPS)


def _dot(a, b, dims):
    return lax.dot_general(a, b, (dims, ((), ())), preferred_element_type=F32)


NN = ((1,), (0,))
NT = ((1,), (1,))
TN = ((0,), (0,))


def _matmul(a, b, *, dims, grid, a_spec, b_spec, o_spec, out_shape, acc_shape, name,
            residual=None, r_spec=None):
    nk = grid[2]

    def body(*refs):
        if residual is None:
            a_ref, b_ref, o_ref, acc_ref = refs
            r_ref = None
        else:
            a_ref, b_ref, r_ref, o_ref, acc_ref = refs
        k = pl.program_id(2)

        @pl.when(k == 0)
        def _():
            acc_ref[...] = jnp.zeros_like(acc_ref)

        acc_ref[...] += _dot(a_ref[...].astype(BF16), b_ref[...].astype(BF16), dims)

        @pl.when(k == nk - 1)
        def _():
            res = acc_ref[...]
            if r_ref is not None:
                res = res + r_ref[...]
            o_ref[...] = res.astype(o_ref.dtype)

    in_specs = [a_spec, b_spec]
    args = [a, b]
    if residual is not None:
        in_specs.append(r_spec)
        args.append(residual)
    return pl.pallas_call(
        body, name=name, grid=grid, in_specs=in_specs, out_specs=o_spec, out_shape=out_shape,
        scratch_shapes=[pltpu.VMEM(acc_shape, F32)],
        compiler_params=_params("parallel", "parallel", "arbitrary"),
    )(*args)


def _mm_nn(a, b, *, tm, tn, tk, out_dtype, name, residual=None):
    m, kd = a.shape
    n = b.shape[1]
    return _matmul(
        a, b, dims=NN, grid=(m // tm, n // tn, kd // tk),
        a_spec=pl.BlockSpec((tm, tk), lambda i, j, k: (i, k)),
        b_spec=pl.BlockSpec((tk, tn), lambda i, j, k: (k, j)),
        o_spec=pl.BlockSpec((tm, tn), lambda i, j, k: (i, j)),
        out_shape=jax.ShapeDtypeStruct((m, n), out_dtype), acc_shape=(tm, tn), name=name,
        residual=residual, r_spec=pl.BlockSpec((tm, tn), lambda i, j, k: (i, j)))


def _mm_nt(a, b, *, tm, tn, tk, out_dtype, name):
    m, kd = a.shape
    n = b.shape[0]
    return _matmul(
        a, b, dims=NT, grid=(m // tm, n // tn, kd // tk),
        a_spec=pl.BlockSpec((tm, tk), lambda i, j, k: (i, k)),
        b_spec=pl.BlockSpec((tn, tk), lambda i, j, k: (j, k)),
        o_spec=pl.BlockSpec((tm, tn), lambda i, j, k: (i, j)),
        out_shape=jax.ShapeDtypeStruct((m, n), out_dtype), acc_shape=(tm, tn), name=name)


def _mm_tn(a, b, *, tm, tn, tk, out_dtype, name):
    kd, m = a.shape
    n = b.shape[1]
    return _matmul(
        a, b, dims=TN, grid=(m // tm, n // tn, kd // tk),
        a_spec=pl.BlockSpec((tk, tm), lambda i, j, k: (k, i)),
        b_spec=pl.BlockSpec((tk, tn), lambda i, j, k: (k, j)),
        o_spec=pl.BlockSpec((tm, tn), lambda i, j, k: (i, j)),
        out_shape=jax.ShapeDtypeStruct((m, n), out_dtype), acc_shape=(tm, tn), name=name)


def _mm_up(h2, w_up_b, *, tm, name):
    s = h2.shape[0]
    nb = w_up_b.shape[2]
    per_half = D_FF // nb
    return _matmul(
        h2, w_up_b, dims=NN, grid=(s // tm, N_DEV, 1),
        a_spec=pl.BlockSpec((tm, D_MODEL), lambda i, j, k: (i, 0)),
        b_spec=pl.BlockSpec((None, D_MODEL, nb), lambda i, j, k: (j, 0, 0)),
        o_spec=pl.BlockSpec((None, tm, nb), lambda i, j, k: (j // per_half, i, j % per_half)),
        out_shape=jax.ShapeDtypeStruct((2, s, D_FF), F32), acc_shape=(tm, nb), name=name)


def _mm_up_bwd_x(dp, w_up_b, *, tm, name):
    s = dp.shape[1]
    nb = w_up_b.shape[2]
    per_half = D_FF // nb
    return _matmul(
        dp, w_up_b, dims=NT, grid=(s // tm, 1, N_DEV),
        a_spec=pl.BlockSpec((None, tm, nb), lambda i, j, k: (k // per_half, i, k % per_half)),
        b_spec=pl.BlockSpec((None, D_MODEL, nb), lambda i, j, k: (k, 0, 0)),
        o_spec=pl.BlockSpec((tm, D_MODEL), lambda i, j, k: (i, 0)),
        out_shape=jax.ShapeDtypeStruct((s, D_MODEL), F32), acc_shape=(tm, D_MODEL), name=name)


def _mm_up_bwd_w(h2, dp, *, tm, tk, name):
    s = h2.shape[0]
    nb = D_FF * 2 // N_DEV
    per_half = D_FF // nb
    return _matmul(
        h2, dp, dims=TN, grid=(D_MODEL // tm, N_DEV, s // tk),
        a_spec=pl.BlockSpec((tk, tm), lambda i, j, k: (k, i)),
        b_spec=pl.BlockSpec((None, tk, nb), lambda i, j, k: (j // per_half, k, j % per_half)),
        o_spec=pl.BlockSpec((None, tm, nb), lambda i, j, k: (j, i, 0)),
        out_shape=jax.ShapeDtypeStruct((N_DEV, D_MODEL, nb), BF16), acc_shape=(tm, nb), name=name)


def _norm_fwd(x, g, *, tm, name):
    s, d = x.shape

    def body(x_ref, g_ref, h_ref):
        xv = x_ref[...]
        h_ref[...] = (xv * _rms(xv) * g_ref[...]).astype(BF16)

    return pl.pallas_call(
        body, name=name, grid=(s // tm,),
        in_specs=[pl.BlockSpec((tm, d), lambda i: (i, 0)), pl.BlockSpec((1, d), lambda i: (0, 0))],
        out_specs=pl.BlockSpec((tm, d), lambda i: (i, 0)),
        out_shape=jax.ShapeDtypeStruct((s, d), BF16),
        compiler_params=_params("parallel"),
    )(x, g.reshape(1, d))


def _behind(token):
    return jnp.zeros((8, 128), F32) if token is None else token


def _norm_bwd(dh, x, g, dres, *, tm, name, token=None):
    s, d = x.shape

    def body(dh_ref, x_ref, g_ref, dres_ref, token_ref, dx_ref, dg_ref):
        @pl.when(pl.program_id(0) == 0)
        def _():
            dg_ref[...] = jnp.zeros_like(dg_ref)

        xv = x_ref[...]
        rstd = _rms(xv)
        n = xv * rstd
        dhv = dh_ref[...]
        dn = dhv * g_ref[...]
        dx = rstd * (dn - n * jnp.mean(dn * n, axis=-1, keepdims=True))
        dx_ref[...] = dres_ref[...] + dx
        dg_ref[0:1, :] += jnp.sum(dhv * n, axis=0, keepdims=True)

    return pl.pallas_call(
        body, name=name, grid=(s // tm,),
        in_specs=[pl.BlockSpec((tm, d), lambda i: (i, 0)), pl.BlockSpec((tm, d), lambda i: (i, 0)),
                  pl.BlockSpec((1, d), lambda i: (0, 0)), pl.BlockSpec((tm, d), lambda i: (i, 0)),
                  pl.BlockSpec(memory_space=pl.ANY)],
        out_specs=[pl.BlockSpec((tm, d), lambda i: (i, 0)), pl.BlockSpec((8, d), lambda i: (0, 0))],
        out_shape=[jax.ShapeDtypeStruct((s, d), F32), jax.ShapeDtypeStruct((8, d), F32)],
        compiler_params=_params("arbitrary"),
    )(dh, x, g.reshape(1, d), dres, _behind(token))


def _loss_head(x, g, tgt, *, tm, name):
    s, d = x.shape

    def body(x_ref, g_ref, t_ref, loss_ref, dx_ref, dg_ref):
        @pl.when(pl.program_id(0) == 0)
        def _():
            dg_ref[...] = jnp.zeros_like(dg_ref)
            loss_ref[...] = jnp.zeros_like(loss_ref)

        xv = x_ref[...]
        gv = g_ref[...]
        rstd = _rms(xv)
        n = xv * rstd
        e = n * gv - t_ref[...]
        part = 0.5 * jnp.sum(jnp.mean(e * e, axis=-1, keepdims=True), axis=0, keepdims=True)
        loss_ref[...] += jnp.broadcast_to(part, loss_ref.shape)
        dy = e * (1.0 / d)
        dn = dy * gv
        dx_ref[...] = rstd * (dn - n * jnp.mean(dn * n, axis=-1, keepdims=True))
        dg_ref[0:1, :] += jnp.sum(dy * n, axis=0, keepdims=True)

    return pl.pallas_call(
        body, name=name, grid=(s // tm,),
        in_specs=[pl.BlockSpec((tm, d), lambda i: (i, 0)), pl.BlockSpec((1, d), lambda i: (0, 0)),
                  pl.BlockSpec((tm, d), lambda i: (i, 0))],
        out_specs=[pl.BlockSpec((8, 128), lambda i: (0, 0)), pl.BlockSpec((tm, d), lambda i: (i, 0)),
                   pl.BlockSpec((8, d), lambda i: (0, 0))],
        out_shape=[jax.ShapeDtypeStruct((8, 128), F32), jax.ShapeDtypeStruct((s, d), F32),
                   jax.ShapeDtypeStruct((8, d), F32)],
        compiler_params=_params("arbitrary"),
    )(x, g.reshape(1, d), tgt)


def _scan_rows(a_ref, b_ref, h_ref, carry, *, rows, reverse):
    width = a_ref.shape[1]
    n_chunks = rows // 8
    row = lax.broadcasted_iota(jnp.int32, (8, width), 0)

    def step(ci, carry):
        chunk = (n_chunks - 1 - ci) if reverse else ci
        off = pl.multiple_of(chunk * 8, 8)
        av = a_ref[pl.ds(off, 8), :]
        bv = b_ref[pl.ds(off, 8), :]
        for sh in (1, 2, 4):
            if reverse:
                a_sh = pltpu.roll(av, 8 - sh, 0)
                b_sh = pltpu.roll(bv, 8 - sh, 0)
                m = row < 8 - sh
            else:
                a_sh = pltpu.roll(av, sh, 0)
                b_sh = pltpu.roll(bv, sh, 0)
                m = row >= sh
            bv = jnp.where(m, av * b_sh + bv, bv)
            av = jnp.where(m, av * a_sh, av)
        h = av * carry + bv
        h_ref[pl.ds(off, 8), :] = h
        return h[0:1, :] if reverse else h[7:8, :]

    return lax.fori_loop(0, n_chunks, step, carry)


def _lru_gates(lx, wa_ref, wx_ref, ba, bx, sp):
    lxb = lx.astype(BF16)
    pre_r = jnp.concatenate(
        [_dot(lxb[:, g * LRU_GROUP:(g + 1) * LRU_GROUP], wa_ref[g], NN) for g in range(N_GROUPS)], axis=1)
    pre_i = jnp.concatenate(
        [_dot(lxb[:, g * LRU_GROUP:(g + 1) * LRU_GROUP], wx_ref[g], NN) for g in range(N_GROUPS)], axis=1)
    r = _sigmoid(pre_r + ba)
    ig = _sigmoid(pre_i + bx)
    log_a = (-RG_C * r) * sp
    a = jnp.exp(log_a)
    mult = jnp.sqrt(-jnp.tanh(log_a) * (a * a + 1.0))
    return lxb, r, ig, a, mult


def _mixer_fwd(z, cw, cb, wa_bd, wx_bd, ba, bx, lam, scw, *, tile, name):
    s = z.shape[0]
    n_tiles = s // tile

    def body(z_ref, cw_ref, cb_ref, wa_ref, wx_ref, ba_ref, bx_ref, lam_ref, scw_ref,
             y_ref, hs_ref, ext_lx, ext_q, a_s, b_s, h_car):
        i = pl.program_id(0)

        @pl.when(i == 0)
        def _():
            ext_lx[0:HALO, :] = jnp.zeros((HALO, D_LRU), F32)
            ext_q[0:HALO, :] = jnp.zeros((HALO, D_SC), F32)
            h_car[...] = jnp.zeros_like(h_car)

        ext_lx[HALO:HALO + tile, :] = z_ref[:, 0:D_LRU]
        ext_q[HALO:HALO + tile, :] = z_ref[:, 2 * D_LRU + D_SC:2 * D_LRU + 2 * D_SC] * z_ref[:, 2 * D_LRU + 2 * D_SC:D_IN]
        lx = cb_ref[...] + cw_ref[0:1, :] * ext_lx[pl.ds(HALO - 3, tile), :]
        for k in range(1, 4):
            lx = lx + cw_ref[k:k + 1, :] * ext_lx[pl.ds(HALO - 3 + k, tile), :]
        cq = scw_ref[0:1, :] * ext_q[pl.ds(HALO - 2, tile), :]
        for k in range(1, 3):
            cq = cq + scw_ref[k:k + 1, :] * ext_q[pl.ds(HALO - 2 + k, tile), :]
        ext_lx[0:HALO, :] = ext_lx[tile:tile + HALO, :]
        ext_q[0:HALO, :] = ext_q[tile:tile + HALO, :]

        sp = _softplus(-lam_ref[...])
        _, _, ig, a, mult = _lru_gates(lx, wa_ref, wx_ref, ba_ref[...], bx_ref[...], sp)
        a_s[...] = a
        b_s[...] = mult * (ig * lx)
        h_car[0:1, :] = _scan_rows(a_s, b_s, hs_ref, h_car[0:1, :], rows=tile, reverse=False)

        y_ref[:, 0:D_LRU] = (hs_ref[...] * _gelu(z_ref[:, D_LRU:2 * D_LRU])).astype(BF16)
        y_ref[:, D_LRU:D_MIX] = (z_ref[:, 2 * D_LRU:2 * D_LRU + D_SC] * cq).astype(BF16)

    full = lambda shape: pl.BlockSpec(shape, lambda i: (0,) * len(shape))
    return pl.pallas_call(
        body, name=name, grid=(n_tiles,),
        in_specs=[pl.BlockSpec((tile, D_IN), lambda i: (i, 0)),
                  full((4, D_LRU)), full((1, D_LRU)),
                  full((N_GROUPS, LRU_GROUP, LRU_GROUP)), full((N_GROUPS, LRU_GROUP, LRU_GROUP)),
                  full((1, D_LRU)), full((1, D_LRU)), full((1, D_LRU)), full((3, D_SC))],
        out_specs=[pl.BlockSpec((tile, D_MIX), lambda i: (i, 0)), pl.BlockSpec((tile, D_LRU), lambda i: (i, 0))],
        out_shape=[jax.ShapeDtypeStruct((s, D_MIX), BF16), jax.ShapeDtypeStruct((s, D_LRU), F32)],
        scratch_shapes=[pltpu.VMEM((tile + HALO, D_LRU), F32), pltpu.VMEM((tile + HALO, D_SC), F32),
                        pltpu.VMEM((tile, D_LRU), F32), pltpu.VMEM((tile, D_LRU), F32),
                        pltpu.VMEM((8, D_LRU), F32)],
        compiler_params=_params("arbitrary"),
    )(z, cw, cb.reshape(1, -1), wa_bd, wx_bd, ba.reshape(1, -1), bx.reshape(1, -1), lam.reshape(1, -1), scw)


def _mixer_bwd(z, hs, dy, cw, cb, wa_bd, wx_bd, ba, bx, lam, scw, *, tile, name, token=None):
    s = z.shape[0]
    n_tiles = s // tile
    per8 = tile // 8

    def body(z_ref, zp_ref, hs_ref, hsp_ref, dy_ref, cw_ref, cb_ref, wa_ref, wx_ref, ba_ref, bx_ref, lam_ref, scw_ref,
             token_ref, dz_ref, dcw_ref, dvec_ref, dwa_ref, dwx_ref, dscw_ref,
             ext_lx, ext_q, ext_h, ext_a, ext_dlx, ext_dcq, a_s, b_s, lam_s, l_car):
        i = pl.program_id(0)
        first_tile = i == n_tiles - 1

        @pl.when(i == 0)
        def _():
            for ref in (dcw_ref, dvec_ref, dwa_ref, dwx_ref, dscw_ref, l_car):
                ref[...] = jnp.zeros_like(ref)
            ext_a[tile:tile + HALO, :] = jnp.zeros((HALO, D_LRU), F32)
            ext_dlx[tile:tile + HALO, :] = jnp.zeros((HALO, D_LRU), F32)
            ext_dcq[tile:tile + HALO, :] = jnp.zeros((HALO, D_SC), F32)

        keep = jnp.where(first_tile, 0.0, 1.0)
        sb = z_ref[:, 2 * D_LRU:2 * D_LRU + D_SC]
        sc = z_ref[:, 2 * D_LRU + D_SC:2 * D_LRU + 2 * D_SC]
        sx = z_ref[:, 2 * D_LRU + 2 * D_SC:D_IN]
        ext_lx[0:HALO, :] = zp_ref[:, 0:D_LRU] * keep
        ext_lx[HALO:HALO + tile, :] = z_ref[:, 0:D_LRU]
        ext_q[0:HALO, :] = zp_ref[:, 2 * D_LRU + D_SC:2 * D_LRU + 2 * D_SC] * zp_ref[:, 2 * D_LRU + 2 * D_SC:D_IN] * keep
        ext_q[HALO:HALO + tile, :] = sc * sx
        ext_h[0:HALO, :] = hsp_ref[...] * keep
        ext_h[HALO:HALO + tile, :] = hs_ref[...]

        lx = cb_ref[...] + cw_ref[0:1, :] * ext_lx[pl.ds(HALO - 3, tile), :]
        for k in range(1, 4):
            lx = lx + cw_ref[k:k + 1, :] * ext_lx[pl.ds(HALO - 3 + k, tile), :]
        cq = scw_ref[0:1, :] * ext_q[pl.ds(HALO - 2, tile), :]
        for k in range(1, 3):
            cq = cq + scw_ref[k:k + 1, :] * ext_q[pl.ds(HALO - 2 + k, tile), :]

        sp = _softplus(-lam_ref[...])
        lxb, r, ig, a, mult = _lru_gates(lx, wa_ref, wx_ref, ba_ref[...], bx_ref[...], sp)

        ge, dge = _gelu_parts(z_ref[:, D_LRU:2 * D_LRU])
        dy_lru = dy_ref[:, 0:D_LRU]
        dz_ref[:, D_LRU:2 * D_LRU] = (dy_lru * hs_ref[...] * dge).astype(BF16)

        ext_a[0:tile, :] = a
        a_s[...] = ext_a[pl.ds(1, tile), :]
        b_s[...] = dy_lru * ge
        l_car[0:1, :] = _scan_rows(a_s, b_s, lam_s, l_car[0:1, :], rows=tile, reverse=True)
        ext_a[tile:tile + HALO, :] = ext_a[0:HALO, :]
        lv = lam_s[...]

        da = lv * ext_h[pl.ds(HALO - 1, tile), :]
        d_mult = lv * ig * lx
        d_i = lv * mult * lx
        dlx = lv * mult * ig
        dlog_a = da * a - d_mult * (a * a) / mult
        d_r = dlog_a * (-RG_C * sp)
        dpre_r = d_r * r * (1.0 - r)
        dpre_i = d_i * ig * (1.0 - ig)
        dvec_ref[1:2, :] += jnp.sum(dpre_r, axis=0, keepdims=True)
        dvec_ref[2:3, :] += jnp.sum(dpre_i, axis=0, keepdims=True)
        dvec_ref[3:4, :] += jnp.sum(dlog_a * (-RG_C * r), axis=0, keepdims=True)
        dpr_b = dpre_r.astype(BF16)
        dpi_b = dpre_i.astype(BF16)
        back = []
        for g in range(N_GROUPS):
            cols = slice(g * LRU_GROUP, (g + 1) * LRU_GROUP)
            dwa_ref[g] += _dot(lxb[:, cols], dpr_b[:, cols], TN)
            dwx_ref[g] += _dot(lxb[:, cols], dpi_b[:, cols], TN)
            back.append(_dot(dpr_b[:, cols], wa_ref[g], NT) + _dot(dpi_b[:, cols], wx_ref[g], NT))
        dlx = dlx + jnp.concatenate(back, axis=1)
        dvec_ref[0:1, :] += jnp.sum(dlx, axis=0, keepdims=True)

        ext_dlx[0:tile, :] = dlx
        for k in range(4):
            dcw_ref[k:k + 1, :] += jnp.sum(dlx * ext_lx[pl.ds(HALO - 3 + k, tile), :], axis=0, keepdims=True)
        dlxp = cw_ref[3:4, :] * dlx
        for k in range(3):
            dlxp = dlxp + cw_ref[k:k + 1, :] * ext_dlx[pl.ds(3 - k, tile), :]
        dz_ref[:, 0:D_LRU] = dlxp.astype(BF16)
        ext_dlx[tile:tile + HALO, :] = ext_dlx[0:HALO, :]

        dy_sc = dy_ref[:, D_LRU:D_MIX]
        dz_ref[:, 2 * D_LRU:2 * D_LRU + D_SC] = (dy_sc * cq).astype(BF16)
        dcq = dy_sc * sb
        ext_dcq[0:tile, :] = dcq
        for k in range(3):
            dscw_ref[k:k + 1, :] += jnp.sum(dcq * ext_q[pl.ds(HALO - 2 + k, tile), :], axis=0, keepdims=True)
        dq = scw_ref[2:3, :] * dcq
        for k in range(2):
            dq = dq + scw_ref[k:k + 1, :] * ext_dcq[pl.ds(2 - k, tile), :]
        dz_ref[:, 2 * D_LRU + D_SC:2 * D_LRU + 2 * D_SC] = (dq * sx).astype(BF16)
        dz_ref[:, 2 * D_LRU + 2 * D_SC:D_IN] = (dq * sc).astype(BF16)
        ext_dcq[tile:tile + HALO, :] = ext_dcq[0:HALO, :]

        @pl.when(i == n_tiles - 1)
        def _():
            dvec_ref[3:4, :] = dvec_ref[3:4, :] * (-_sigmoid(-lam_ref[...]))

    rev = lambda i: n_tiles - 1 - i
    prev8 = lambda i: jnp.maximum(rev(i) * per8 - 1, 0)
    full = lambda shape: pl.BlockSpec(shape, lambda i: (0,) * len(shape))
    return pl.pallas_call(
        body, name=name, grid=(n_tiles,),
        in_specs=[pl.BlockSpec((tile, D_IN), lambda i: (rev(i), 0)),
                  pl.BlockSpec((HALO, D_IN), lambda i: (prev8(i), 0)),
                  pl.BlockSpec((tile, D_LRU), lambda i: (rev(i), 0)),
                  pl.BlockSpec((HALO, D_LRU), lambda i: (prev8(i), 0)),
                  pl.BlockSpec((tile, D_MIX), lambda i: (rev(i), 0)),
                  full((4, D_LRU)), full((1, D_LRU)),
                  full((N_GROUPS, LRU_GROUP, LRU_GROUP)), full((N_GROUPS, LRU_GROUP, LRU_GROUP)),
                  full((1, D_LRU)), full((1, D_LRU)), full((1, D_LRU)), full((3, D_SC)),
                  pl.BlockSpec(memory_space=pl.ANY)],
        out_specs=[pl.BlockSpec((tile, D_IN), lambda i: (rev(i), 0)),
                   full((8, D_LRU)), full((8, D_LRU)),
                   full((N_GROUPS, LRU_GROUP, LRU_GROUP)), full((N_GROUPS, LRU_GROUP, LRU_GROUP)),
                   full((8, D_SC))],
        out_shape=[jax.ShapeDtypeStruct((s, D_IN), BF16),
                   jax.ShapeDtypeStruct((8, D_LRU), F32), jax.ShapeDtypeStruct((8, D_LRU), F32),
                   jax.ShapeDtypeStruct((N_GROUPS, LRU_GROUP, LRU_GROUP), F32),
                   jax.ShapeDtypeStruct((N_GROUPS, LRU_GROUP, LRU_GROUP), F32),
                   jax.ShapeDtypeStruct((8, D_SC), F32)],
        scratch_shapes=[pltpu.VMEM((tile + HALO, D_LRU), F32), pltpu.VMEM((tile + HALO, D_SC), F32),
                        pltpu.VMEM((tile + HALO, D_LRU), F32), pltpu.VMEM((tile + HALO, D_LRU), F32),
                        pltpu.VMEM((tile + HALO, D_LRU), F32), pltpu.VMEM((tile + HALO, D_SC), F32),
                        pltpu.VMEM((tile, D_LRU), F32), pltpu.VMEM((tile, D_LRU), F32),
                        pltpu.VMEM((tile, D_LRU), F32), pltpu.VMEM((8, D_LRU), F32)],
        compiler_params=_params("arbitrary"),
    )(z, z, hs, hs, dy, cw, cb.reshape(1, -1), wa_bd, wx_bd, ba.reshape(1, -1), bx.reshape(1, -1),
      lam.reshape(1, -1), scw, _behind(token))


def _ffn_fwd(p, fcw, *, tile, tc, name):
    s = p.shape[1]
    per8 = tile // 8

    def body(p_ref, pp_ref, w_ref, act_ref, ext_p):
        i = pl.program_id(0)
        keep = jnp.where(i == 0, 0.0, 1.0)
        ext_p[:, 0:HALO, :] = pp_ref[...] * keep
        ext_p[:, HALO:HALO + tile, :] = p_ref[...]
        u = []
        for half in range(2):
            acc = w_ref[half, 0:1, :] * ext_p[half, pl.ds(HALO - 2, tile), :]
            for k in range(1, 3):
                acc = acc + w_ref[half, k:k + 1, :] * ext_p[half, pl.ds(HALO - 2 + k, tile), :]
            u.append(acc)
        act_ref[...] = (_gelu(u[0]) * u[1]).astype(BF16)

    return pl.pallas_call(
        body, name=name, grid=(s // tile, D_FF // tc),
        in_specs=[pl.BlockSpec((2, tile, tc), lambda i, j: (0, i, j)),
                  pl.BlockSpec((2, HALO, tc), lambda i, j: (0, jnp.maximum(i * per8 - 1, 0), j)),
                  pl.BlockSpec((2, 3, tc), lambda i, j: (0, 0, j))],
        out_specs=pl.BlockSpec((tile, tc), lambda i, j: (i, j)),
        out_shape=jax.ShapeDtypeStruct((s, D_FF), BF16),
        scratch_shapes=[pltpu.VMEM((2, tile + HALO, tc), F32)],
        compiler_params=_params("parallel", "parallel"),
    )(p, p, fcw)


def _ffn_bwd(p, dact, fcw, *, tile, tc, name, token=None):
    s = p.shape[1]
    n_tiles = s // tile
    per8 = tile // 8
    ext_rows = tile + HALO

    def body(p_ref, pp_ref, pn_ref, da_ref, dan_ref, w_ref, token_ref, dp_ref, dw_ref, ext_p, ext_du):
        i = pl.program_id(1)

        @pl.when(i == 0)
        def _():
            dw_ref[...] = jnp.zeros_like(dw_ref)

        keep_prev = jnp.where(i == 0, 0.0, 1.0)
        keep_next = jnp.where(i == n_tiles - 1, 0.0, 1.0)
        ext_p[:, 0:HALO, :] = pp_ref[...] * keep_prev
        ext_p[:, HALO:HALO + tile, :] = p_ref[...]
        ext_p[:, HALO + tile:2 * HALO + tile, :] = pn_ref[...] * keep_next
        u = []
        for half in range(2):
            acc = w_ref[half, 0:1, :] * ext_p[half, pl.ds(HALO - 2, ext_rows), :]
            for k in range(1, 3):
                acc = acc + w_ref[half, k:k + 1, :] * ext_p[half, pl.ds(HALO - 2 + k, ext_rows), :]
            u.append(acc)
        ge, dge = _gelu_parts(u[0])
        da = jnp.concatenate([da_ref[...], dan_ref[...] * keep_next], axis=0)
        ext_du[0, :, :] = da * u[1] * dge
        ext_du[1, :, :] = da * ge
        for half in range(2):
            du = ext_du[half, 0:tile, :]
            acc = w_ref[half, 2:3, :] * du
            for k in range(2):
                acc = acc + w_ref[half, k:k + 1, :] * ext_du[half, pl.ds(2 - k, tile), :]
            dp_ref[half, :, :] = acc.astype(BF16)
            for k in range(3):
                dw_ref[half, k:k + 1, :] += jnp.sum(du * ext_p[half, pl.ds(HALO - 2 + k, tile), :], axis=0, keepdims=True)

    return pl.pallas_call(
        body, name=name, grid=(D_FF // tc, n_tiles),
        in_specs=[pl.BlockSpec((2, tile, tc), lambda j, i: (0, i, j)),
                  pl.BlockSpec((2, HALO, tc), lambda j, i: (0, jnp.maximum(i * per8 - 1, 0), j)),
                  pl.BlockSpec((2, HALO, tc), lambda j, i: (0, jnp.minimum((i + 1) * per8, n_tiles * per8 - 1), j)),
                  pl.BlockSpec((tile, tc), lambda j, i: (i, j)),
                  pl.BlockSpec((HALO, tc), lambda j, i: (jnp.minimum((i + 1) * per8, n_tiles * per8 - 1), j)),
                  pl.BlockSpec((2, 3, tc), lambda j, i: (0, 0, j)), pl.BlockSpec(memory_space=pl.ANY)],
        out_specs=[pl.BlockSpec((2, tile, tc), lambda j, i: (0, i, j)),
                   pl.BlockSpec((2, 8, tc), lambda j, i: (0, 0, j))],
        out_shape=[jax.ShapeDtypeStruct((2, s, D_FF), BF16), jax.ShapeDtypeStruct((2, 8, D_FF), F32)],
        scratch_shapes=[pltpu.VMEM((2, tile + 2 * HALO, tc), F32), pltpu.VMEM((2, ext_rows, tc), F32)],
        compiler_params=_params("parallel", "arbitrary"),
    )(p, p, p, dact, dact, fcw, _behind(token))


def _adamw_math(w, g, m, v):
    m = ADAM_B1 * m + (1.0 - ADAM_B1) * g
    v = ADAM_B2 * v + (1.0 - ADAM_B2) * (g * g)
    m_hat = m / (1.0 - ADAM_B1 ** ADAM_STEP)
    v_hat = v / (1.0 - ADAM_B2 ** ADAM_STEP)
    delta = -ADAM_LR * (m_hat / (jnp.sqrt(v_hat) + ADAM_EPS) + ADAM_WD * w)
    return delta, m, v


def _adamw(w, g, m, v, *, name):
    rows, cols = w.shape
    tr = rows
    for cand in (512, 256, 128, 64, 32, 16, 8):
        if rows % cand == 0 and rows > cand:
            tr = cand
            break

    def body(w_ref, g_ref, m_ref, v_ref, d_ref, nm_ref, nv_ref):
        d, nm, nv = _adamw_math(w_ref[...], g_ref[...], m_ref[...], v_ref[...])
        d_ref[...] = d
        nm_ref[...] = nm
        nv_ref[...] = nv

    spec = pl.BlockSpec((tr, cols), lambda i: (i, 0))
    return pl.pallas_call(
        body, name=name, grid=(rows // tr,), in_specs=[spec] * 4, out_specs=[spec] * 3,
        out_shape=[jax.ShapeDtypeStruct((rows, cols), F32)] * 3,
        compiler_params=_params("parallel"),
    )(w, g, m, v)


def _sum_parts(parts, *, name):
    _, rows, cols = parts.shape
    tr = rows
    for cand in (256, 128, 64, 32, 16):
        if rows % cand == 0 and rows > cand:
            tr = cand
            break

    def body(p_ref, o_ref):
        acc = p_ref[0].astype(F32)
        for d in range(1, N_DEV):
            acc = acc + p_ref[d].astype(F32)
        o_ref[...] = acc

    return pl.pallas_call(
        body, name=name, grid=(rows // tr,),
        in_specs=[pl.BlockSpec((N_DEV, tr, cols), lambda i: (0, i, 0))],
        out_specs=pl.BlockSpec((tr, cols), lambda i: (i, 0)),
        out_shape=jax.ShapeDtypeStruct((rows, cols), F32),
        compiler_params=_params("parallel"),
    )(parts)


def _place():
    return lax.axis_index("x"), lax.axis_index("y"), lax.axis_index("c")


def _flip(v, bit):
    return 1 - v if bit else v


HBM_SPEC = pl.BlockSpec(memory_space=pltpu.HBM)
SEM_SPEC = pl.BlockSpec(memory_space=pltpu.SEMAPHORE)
ANY_SPEC = pl.BlockSpec(memory_space=pl.ANY)
DATAFLOW = pltpu.SideEffectType.DATAFLOW_SIDE_EFFECTING
N_PEERS = N_DEV - 1


def _peer_copy(k, src_ref, land_ref, send_sem, recv_sem, gather):
    x, y, c = _place()
    my_id = 4 * x + 2 * y + c
    px, py, pc = _flip(x, k & 4), _flip(y, k & 2), _flip(c, k & 1)
    peer_id = 4 * px + 2 * py + pc
    return pltpu.make_async_remote_copy(
        src_ref=src_ref if gather else src_ref.at[peer_id], dst_ref=land_ref.at[my_id],
        send_sem=send_sem.at[k - 1], recv_sem=recv_sem.at[k - 1],
        device_id=(px, py, pc), device_id_type=MESH)


def _copies_start(srcs, *, gather, name):
    n = len(srcs)
    lands = [lax.empty(((N_DEV,) + s.shape) if gather else s.shape, s.dtype) for s in srcs]

    def body(*refs):
        src_refs, land_refs = refs[:n], refs[n:2 * n]
        send_sems, recv_sems = refs[2 * n:3 * n], refs[3 * n:4 * n]
        token, local_sems = refs[6 * n], refs[6 * n + 1]
        x, y, c = _place()
        my_id = 4 * x + 2 * y + c
        own = [pltpu.make_async_copy(src_refs[t] if gather else src_refs[t].at[my_id], land_refs[t].at[my_id],
                                     local_sems.at[t]) for t in range(n)]
        for cp in own:
            cp.start()
        for cp in own:
            cp.wait()
        for t in range(n):
            for k in range(1, N_DEV):
                _peer_copy(k, src_refs[t], land_refs[t], send_sems[t], recv_sems[t], gather).start()
        token[...] = jnp.zeros_like(token)

    sem = pltpu.SemaphoreType.DMA((N_PEERS,))
    outs = pl.pallas_call(
        body, name=name,
        in_specs=[HBM_SPEC] * (2 * n),
        out_specs=[SEM_SPEC] * (2 * n) + [HBM_SPEC] * (2 * n) + [pl.BlockSpec(memory_space=pltpu.VMEM)],
        out_shape=[sem] * (2 * n) + [pltpu.HBM(s.shape, s.dtype) for s in srcs]
        + [pltpu.HBM(l.shape, l.dtype) for l in lands] + [jax.ShapeDtypeStruct((8, 128), F32)],
        input_output_aliases={i: 2 * n + i for i in range(2 * n)},
        scratch_shapes=[pltpu.SemaphoreType.DMA((n,))],
        compiler_params=pltpu.CompilerParams(has_side_effects=DATAFLOW),
    )(*[pltpu.with_memory_space_constraint(s, pltpu.HBM) for s in srcs],
      *[pltpu.with_memory_space_constraint(l, pltpu.HBM) for l in lands])
    handles = [(outs[t], outs[n + t], outs[2 * n + t], outs[3 * n + t]) for t in range(n)]
    return handles, outs[4 * n]


def _copies_wait(handle, after, *, gather, name):
    send_sem, recv_sem, src, land = handle

    def body(src_ref, land_ref, send_ref, recv_ref, after_ref, src_out, land_out):
        for k in range(1, N_DEV):
            cp = _peer_copy(k, src_ref, land_ref, send_ref, recv_ref, gather)
            cp.wait_send()
            cp.wait_recv()

    return pl.pallas_call(
        body, name=name,
        in_specs=[HBM_SPEC, HBM_SPEC, SEM_SPEC, SEM_SPEC, ANY_SPEC],
        out_specs=[HBM_SPEC, HBM_SPEC],
        out_shape=[pltpu.HBM(src.shape, src.dtype), pltpu.HBM(land.shape, land.dtype)],
        input_output_aliases={0: 0, 1: 1},
        compiler_params=pltpu.CompilerParams(has_side_effects=DATAFLOW),
    )(src, land, send_sem, recv_sem, after)[1]


def _sequencer_copies(srcs, *, gather, name, collective_id, after):
    n = len(srcs)
    hbm = pltpu.MemorySpace.HBM
    src_refs = [jax.new_ref(s, memory_space=hbm) for s in srcs]
    land_refs = [jax.empty_ref(jax.ShapeDtypeStruct(((N_DEV,) + s.shape) if gather else s.shape, s.dtype),
                               memory_space=hbm) for s in srcs]
    token_in = jax.new_ref(jnp.zeros((8, 128), F32) if after is None else after, memory_space=hbm)
    token_out = jax.empty_ref(jax.ShapeDtypeStruct((8, 128), F32), memory_space=hbm)

    @pl.kernel(mesh=plsc.ScalarSubcoreMesh(axis_name="seq", num_cores=1), name=name,
               scratch_types=(pltpu.SemaphoreType.DMA((n, N_PEERS)), pltpu.SemaphoreType.DMA((n, N_PEERS)),
                              pltpu.SemaphoreType.DMA((n + 1,))),
               compiler_params=pltpu.CompilerParams(collective_id=collective_id))
    def launch(send_sems, recv_sems, local_sems):
        x, y, c = _place()
        my_id = 4 * x + 2 * y + c
        barrier = pltpu.get_barrier_semaphore()
        for k in range(1, N_DEV):
            peer = (_flip(x, k & 4), _flip(y, k & 2), _flip(c, k & 1))
            pl.semaphore_signal(barrier, inc=1, device_id=peer, device_id_type=MESH)
        pl.semaphore_wait(barrier, N_PEERS)
        own = [pltpu.make_async_copy(src_refs[t] if gather else src_refs[t].at[my_id], land_refs[t].at[my_id],
                                     local_sems.at[t]) for t in range(n)]
        for cp in own:
            cp.start()
        copies = [_peer_copy(k, src_refs[t], land_refs[t], send_sems.at[t], recv_sems.at[t], gather)
                  for t in range(n) for k in range(1, N_DEV)]
        for cp in copies:
            cp.start()
        for cp in own:
            cp.wait()
        for cp in copies:
            cp.wait()
        passed = pltpu.make_async_copy(token_in, token_out, local_sems.at[n])
        passed.start()
        passed.wait()

    launch()
    return [ref[...] for ref in land_refs], token_out[...]


def _all_reduce_small(buf, *, name):
    _, rows, lanes = buf.shape

    def body(in_ref, out_ref, parts, send_sems, recv_sems):
        x, y, c = _place()
        my_id = 4 * x + 2 * y + c
        peers = []
        for k in range(1, N_DEV):
            px, py, pc = _flip(x, k & 4), _flip(y, k & 2), _flip(c, k & 1)
            peers.append(((px, py, pc), 4 * px + 2 * py + pc))
        scatter = [pltpu.make_async_remote_copy(
            src_ref=in_ref.at[pid], dst_ref=parts.at[my_id],
            send_sem=send_sems.at[0, k], recv_sem=recv_sems.at[0, k],
            device_id=peer, device_id_type=MESH) for k, (peer, pid) in enumerate(peers)]
        for cp in scatter:
            cp.start()
        parts[my_id] = in_ref[my_id]
        for cp in scatter:
            cp.wait()
        total = parts[0]
        for d in range(1, N_DEV):
            total = total + parts[d]
        out_ref[my_id] = total
        gather = [pltpu.make_async_remote_copy(
            src_ref=out_ref.at[my_id], dst_ref=out_ref.at[my_id],
            send_sem=send_sems.at[1, k], recv_sem=recv_sems.at[1, k],
            device_id=peer, device_id_type=MESH) for k, (peer, pid) in enumerate(peers)]
        for cp in gather:
            cp.start()
        for k, (peer, pid) in enumerate(peers):
            pltpu.make_async_remote_copy(
                src_ref=out_ref.at[pid], dst_ref=out_ref.at[pid],
                send_sem=send_sems.at[1, k], recv_sem=recv_sems.at[1, k],
                device_id=peer, device_id_type=MESH).wait()

    vmem = pl.BlockSpec(memory_space=pltpu.VMEM)
    return pl.pallas_call(
        body, name=name, in_specs=[vmem], out_specs=vmem,
        out_shape=jax.ShapeDtypeStruct(buf.shape, F32),
        scratch_shapes=[pltpu.VMEM(buf.shape, F32),
                        pltpu.SemaphoreType.DMA((2, 7)), pltpu.SemaphoreType.DMA((2, 7))],
        compiler_params=pltpu.CompilerParams(vmem_limit_bytes=VMEM_LIMIT),
    )(buf)


TM = 512
MIX_TILE = 128
FFN_TILE = 256
FFN_TC = 512


def _block_diag(w):
    wg = w.reshape(N_GROUPS, HEADS_PER_GROUP, LRU_HEAD_DIM, LRU_HEAD_DIM)
    eye = jnp.eye(HEADS_PER_GROUP, dtype=w.dtype)
    bd = wg[:, :, :, None, :] * eye[None, :, None, :, None]
    return bd.reshape(N_GROUPS, LRU_GROUP, LRU_GROUP).astype(BF16)


def _head_blocks(bd):
    b5 = bd.reshape(N_GROUPS, HEADS_PER_GROUP, LRU_HEAD_DIM, HEADS_PER_GROUP, LRU_HEAD_DIM)
    blocks = [b5[:, h, :, h, :] for h in range(HEADS_PER_GROUP)]
    return jnp.stack(blocks, axis=1).reshape(LRU_HEADS, LRU_HEAD_DIM, LRU_HEAD_DIM)


def _w(lw, key, after):
    value = lw[key]
    return value(after) if callable(value) else value


def _layer_fwd(x, lw, tag):
    h1 = _norm_fwd(x, lw["g1"], tm=TM, name=f"norm1_fwd_{tag}")
    z = _mm_nt(h1, _w(lw, "w_in_t", h1), tm=TM, tn=512, tk=D_MODEL, out_dtype=F32, name=f"in_proj_{tag}")
    y_mix, hs = _mixer_fwd(z, _w(lw, "cw", z), lw["cb"], lw["wa_bd"], lw["wx_bd"], lw["ba"], lw["bx"], lw["lam"],
                           _w(lw, "scw", z), tile=MIX_TILE, name=f"mixer_fwd_{tag}")
    x2 = _mm_nn(y_mix, _w(lw, "w_out", y_mix), tm=TM, tn=D_MODEL, tk=D_MIX, out_dtype=F32, name=f"out_proj_{tag}",
                residual=x)
    h2 = _norm_fwd(x2, lw["g2"], tm=TM, name=f"norm2_fwd_{tag}")
    p = _mm_up(h2, _w(lw, "w_up_b", h2), tm=TM, name=f"up_proj_{tag}")
    act = _ffn_fwd(p, _w(lw, "fcw", p), tile=FFN_TILE, tc=FFN_TC, name=f"ffn_fwd_{tag}")
    x3 = _mm_nn(act, _w(lw, "w_down", act), tm=TM, tn=D_MODEL, tk=1024, out_dtype=F32, name=f"down_proj_{tag}",
                residual=x2)
    saved = dict(x=x, h1=h1, z=z, y_mix=y_mix, hs=hs, x2=x2, h2=h2, p=p, act=act)
    return x3, saved


def _layer_bwd(dx3, lw, sv, tag, put):
    w_in_t, w_out, w_up_b, w_down = (_w(lw, k, dx3) for k in ("w_in_t", "w_out", "w_up_b", "w_down"))
    cw, scw, fcw = (_w(lw, k, dx3) for k in ("cw", "scw", "fcw"))
    dact = _mm_nt(dx3, w_down, tm=TM, tn=512, tk=D_MODEL, out_dtype=F32, name=f"down_bwd_x_{tag}")
    g_down = _mm_tn(sv["act"], dx3, tm=512, tn=D_MODEL, tk=512, out_dtype=BF16, name=f"down_bwd_w_{tag}")
    dp, dfcw = _ffn_bwd(sv["p"], dact, fcw, tile=FFN_TILE, tc=FFN_TC, name=f"ffn_bwd_{tag}",
                        token=put("w_down", g_down))
    dh2 = _mm_up_bwd_x(dp, w_up_b, tm=TM, name=f"up_bwd_x_{tag}")
    g_up = _mm_up_bwd_w(sv["h2"], dp, tm=512, tk=512, name=f"up_bwd_w_{tag}")
    dx2, dg2 = _norm_bwd(dh2, sv["x2"], lw["g2"], dx3, tm=TM, name=f"norm2_bwd_{tag}", token=put("w_up_b", g_up))
    dy = _mm_nt(dx2, w_out, tm=TM, tn=512, tk=D_MODEL, out_dtype=F32, name=f"out_bwd_x_{tag}")
    g_out = _mm_tn(sv["y_mix"], dx2, tm=512, tn=D_MODEL, tk=512, out_dtype=BF16, name=f"out_bwd_w_{tag}")
    out_token = put("w_out", g_out)
    dz, dcw, dvec, dwa, dwx, dscw = _mixer_bwd(
        sv["z"], sv["hs"], dy, cw, lw["cb"], lw["wa_bd"], lw["wx_bd"], lw["ba"], lw["bx"], lw["lam"],
        scw, tile=MIX_TILE, name=f"mixer_bwd_{tag}", token=out_token)
    dh1 = _mm_nn(dz, w_in_t, tm=TM, tn=D_MODEL, tk=512, out_dtype=F32, name=f"in_bwd_x_{tag}")
    g_in_t = _mm_tn(dz, sv["h1"], tm=512, tn=D_MODEL, tk=512, out_dtype=BF16, name=f"in_bwd_w_{tag}")
    dx, dg1 = _norm_bwd(dh1, sv["x"], lw["g1"], dx2, tm=TM, name=f"norm1_bwd_{tag}", token=put("w_in_t", g_in_t))
    small = dict(norm1_g=dg1[0], lru_conv_w=dcw[0:4], lru_conv_b=dvec[0], lru_wa=_head_blocks(dwa),
                 lru_ba=dvec[1], lru_wx=_head_blocks(dwx), lru_bx=dvec[2], lru_lambda=dvec[3],
                 sc_conv_w=dscw[0:3], norm2_g=dg2[0], ffn_conv_w=dfcw[:, 0:3, :])
    return dx, small


SMALL_ORDER = ("norm1_g", "lru_conv_w", "lru_conv_b", "lru_wa", "lru_ba", "lru_wx", "lru_bx", "lru_lambda",
               "sc_conv_w", "norm2_g", "ffn_conv_w")


def _local_step(x, tgt, layers, final_g, put):
    saved = []
    h = x
    for l in range(DEPTH):
        h, sv = _layer_fwd(h, layers[l], f"l{l}")
        saved.append(sv)
    loss_blk, dx, dgf = _loss_head(h, final_g, tgt, tm=TM, name="loss_head")
    smalls = [None] * DEPTH
    for l in reversed(range(DEPTH)):
        dx, smalls[l] = _layer_bwd(dx, layers[l], saved[l], f"l{l}", functools.partial(put, l))
    return loss_blk[0, 0], dx, smalls, dgf[0]


def kernel(x, norm1_g, w_in, lru_conv_w, lru_conv_b, lru_wa, lru_ba, lru_wx, lru_bx, lru_lambda, sc_conv_w, w_out, norm2_g, w_up, ffn_conv_w, w_down, final_g, loss_target, m_norm1_g, m_w_in, m_lru_conv_w, m_lru_conv_b, m_lru_wa, m_lru_ba, m_lru_wx, m_lru_bx, m_lru_lambda, m_sc_conv_w, m_w_out, m_norm2_g, m_w_up, m_ffn_conv_w, m_w_down, m_final_g, v_norm1_g, v_w_in, v_lru_conv_w, v_lru_conv_b, v_lru_wa, v_lru_ba, v_lru_wx, v_lru_bx, v_lru_lambda, v_sc_conv_w, v_w_out, v_norm2_g, v_w_up, v_ffn_conv_w, v_w_down, v_final_g):
    names = ["norm1_g", "w_in", "lru_conv_w", "lru_conv_b", "lru_wa", "lru_ba", "lru_wx", "lru_bx", "lru_lambda",
             "sc_conv_w", "w_out", "norm2_g", "w_up", "ffn_conv_w", "w_down", "final_g"]
    w = dict(zip(names, [norm1_g, w_in, lru_conv_w, lru_conv_b, lru_wa, lru_ba, lru_wx, lru_bx, lru_lambda,
                         sc_conv_w, w_out, norm2_g, w_up, ffn_conv_w, w_down, final_g]))
    m = dict(zip(names, [m_norm1_g, m_w_in, m_lru_conv_w, m_lru_conv_b, m_lru_wa, m_lru_ba, m_lru_wx, m_lru_bx,
                         m_lru_lambda, m_sc_conv_w, m_w_out, m_norm2_g, m_w_up, m_ffn_conv_w, m_w_down, m_final_g]))
    v = dict(zip(names, [v_norm1_g, v_w_in, v_lru_conv_w, v_lru_conv_b, v_lru_wa, v_lru_ba, v_lru_wx, v_lru_bx,
                         v_lru_lambda, v_sc_conv_w, v_w_out, v_norm2_g, v_w_up, v_ffn_conv_w, v_w_down, v_final_g]))
    my_id = 4 * lax.axis_index("x") + 2 * lax.axis_index("y") + lax.axis_index("c")

    taps = jnp.zeros((DEPTH, 16, 768), F32)
    taps = taps.at[:, 0:4, 0:128].set(lru_conv_w).at[:, 4:7, 0:64].set(sc_conv_w).at[:, 8:11, :].set(ffn_conv_w)
    shards = {}
    for l in range(DEPTH):
        shards[f"w_in_t{l}"] = jnp.swapaxes(w_in[l], 0, 1).astype(BF16)
        if l == 0:
            shards["taps"] = taps.reshape(DEPTH * 16, 768)
        shards[f"w_out{l}"] = w_out[l].astype(BF16)
        shards[f"w_up_b{l}"] = w_up[l].astype(BF16)
        shards[f"w_down{l}"] = w_down[l].astype(BF16)
    ids = iter(range(16))
    got = {}
    chain = [None]
    for group in (("w_in_t0", "taps"), ("w_out0", "w_up_b0", "w_down0"),
                  ("w_in_t1", "w_out1", "w_up_b1", "w_down1")):
        lands, chain[0] = _sequencer_copies([shards[k] for k in group], gather=True, name=f"gather_{group[0]}",
                                            collective_id=next(ids), after=chain[0])
        got.update(zip(group, lands))

    def fetch(key, after):
        return got[key]

    def tap_rows(l, lo, hi, width, after):
        tl = fetch("taps", after).reshape(N_DEV, DEPTH, 16, 768)[:, l, lo:hi, 0:width]
        return jnp.transpose(tl, (1, 0, 2)).reshape(hi - lo, N_DEV * width)

    layers = []
    for l in range(DEPTH):
        layers.append(dict(
            g1=norm1_g[l], g2=norm2_g[l], cb=lru_conv_b[l], ba=lru_ba[l], bx=lru_bx[l], lam=lru_lambda[l],
            wa_bd=_block_diag(lru_wa[l]), wx_bd=_block_diag(lru_wx[l]),
            cw=functools.partial(tap_rows, l, 0, 4, 128), scw=functools.partial(tap_rows, l, 4, 7, 64),
            fcw=lambda after, l=l: tap_rows(l, 8, 11, 768, after).reshape(3, 2, D_FF).transpose(1, 0, 2),
            w_in_t=lambda after, l=l: fetch(f"w_in_t{l}", after).reshape(D_IN, D_MODEL),
            w_out=lambda after, l=l: fetch(f"w_out{l}", after).reshape(D_MIX, D_MODEL),
            w_up_b=lambda after, l=l: fetch(f"w_up_b{l}", after),
            w_down=lambda after, l=l: fetch(f"w_down{l}", after).reshape(D_FF, D_MODEL)))

    scatter_handles = {}

    def put(l, key, grad):
        blocks = grad if grad.ndim == 3 else grad.reshape(N_DEV, grad.shape[0] // N_DEV, grad.shape[1])
        (scatter_handles[(l, key)],), chain[0] = _sequencer_copies(
            [blocks], gather=False, name=f"scatter_{key}{l}", collective_id=next(ids), after=chain[0])
        return blocks

    loss_local, dx, smalls, dgf = _local_step(x[0], loss_target[0], layers, final_g, put)
    loss = lax.psum(loss_local, ("x", "y", "c"))

    parts = []
    for l in range(DEPTH):
        for key in ("w_in_t", "w_out", "w_up_b", "w_down"):
            parts.append(scatter_handles[(l, key)])

    flat = [smalls[l][k].reshape(-1) for l in range(DEPTH) for k in SMALL_ORDER] + [dgf.reshape(-1)]
    sizes = [f.shape[0] for f in flat]
    total = sum(sizes)
    rows = -(-total // (N_DEV * 128 * 8)) * 8
    flat.append(jnp.zeros((N_DEV * rows * 128 - total,), F32))
    small_sum = _all_reduce_small(jnp.concatenate(flat).reshape(N_DEV, rows, 128), name="reduce_small").reshape(-1)
    small_g, off = [], 0
    for sz in sizes:
        small_g.append(small_sum[off:off + sz])
        off += sz
    gs = {}
    for l in range(DEPTH):
        for i, k in enumerate(SMALL_ORDER):
            gs.setdefault(k, []).append(small_g[l * len(SMALL_ORDER) + i])
    g_final = small_g[-1]

    grads = {}
    per_layer = {k: [] for k in ("w_in", "w_out", "w_up", "w_down")}
    for l in range(DEPTH):
        p_in, p_out, p_up, p_down = parts[4 * l:4 * l + 4]
        per_layer["w_in"].append(jnp.swapaxes(_sum_parts(p_in, name=f"sum_w_in_l{l}"), 0, 1))
        per_layer["w_out"].append(_sum_parts(p_out, name=f"sum_w_out_l{l}"))
        per_layer["w_up"].append(_sum_parts(p_up, name=f"sum_w_up_l{l}"))
        per_layer["w_down"].append(_sum_parts(p_down, name=f"sum_w_down_l{l}"))
    for k, lst in per_layer.items():
        grads[k] = jnp.stack(lst)
    for k in ("norm1_g", "lru_conv_b", "lru_ba", "lru_bx", "lru_lambda", "norm2_g"):
        grads[k] = jnp.stack(gs[k]).reshape(DEPTH, -1)
    for k in ("lru_wa", "lru_wx"):
        grads[k] = jnp.stack(gs[k]).reshape(DEPTH, LRU_HEADS, LRU_HEAD_DIM, LRU_HEAD_DIM)
    grads["final_g"] = g_final
    cw_full = jnp.stack(gs["lru_conv_w"]).reshape(DEPTH, 4, N_DEV, 128)
    grads["lru_conv_w"] = lax.dynamic_index_in_dim(cw_full, my_id, axis=2, keepdims=False)
    scw_full = jnp.stack(gs["sc_conv_w"]).reshape(DEPTH, 3, N_DEV, 64)
    grads["sc_conv_w"] = lax.dynamic_index_in_dim(scw_full, my_id, axis=2, keepdims=False)
    fcw_full = jnp.stack(gs["ffn_conv_w"]).reshape(DEPTH, 2, 3, D_FF).transpose(0, 2, 1, 3).reshape(DEPTH, 3, N_DEV, 768)
    grads["ffn_conv_w"] = lax.dynamic_index_in_dim(fcw_full, my_id, axis=2, keepdims=False)

    deltas, new_m, new_v = {}, {}, {}
    for k in names:
        shape = w[k].shape
        cols = shape[-1]
        as2d = lambda a: a.reshape(-1, cols)
        d, nm, nv = _adamw(as2d(w[k]), as2d(grads[k]), as2d(m[k]), as2d(v[k]), name=f"adamw_{k}")
        deltas[k], new_m[k], new_v[k] = d.reshape(shape), nm.reshape(shape), nv.reshape(shape)

    return (loss, dx[None], *[grads[k] for k in names], *[deltas[k] for k in names],
            *[new_m[k] for k in names], *[new_v[k] for k in names])
```

```python
import functools
import math

import jax
import jax.numpy as jnp
from jax import lax
from jax.experimental import pallas as pl
from jax.experimental.pallas import tpu as pltpu
from jax.experimental.pallas import tpu_sc as plsc

F32 = jnp.float32
BF16 = jnp.bfloat16

N_DEV = 8
DEPTH = 2
D_MODEL = 1024
D_LRU = 1024
D_SC = 512
D_MIX = D_LRU + D_SC
D_IN = 2 * D_LRU + 3 * D_SC
D_FF = 3072
LRU_HEADS = 16
LRU_HEAD_DIM = 64
LRU_GROUP = 256
N_GROUPS = D_LRU // LRU_GROUP
HEADS_PER_GROUP = LRU_GROUP // LRU_HEAD_DIM
RG_C = 8.0
EPS = 1e-6
HALO = 8

ADAM_LR = 0.001
ADAM_B1 = 0.9
ADAM_B2 = 0.999
ADAM_EPS = 1e-08
ADAM_WD = 0.01
ADAM_STEP = 10

GELU_C = math.sqrt(2.0 / math.pi)
GELU_A = 0.044715

VMEM_LIMIT = 56 * 1024 * 1024
MESH = pl.DeviceIdType.MESH


def _params(*sem):
    return pltpu.CompilerParams(dimension_semantics=tuple(sem) if sem else None,
                                vmem_limit_bytes=VMEM_LIMIT)


def _gelu_parts(x):
    x2 = x * x
    t = jnp.tanh(GELU_C * (x + GELU_A * x * x2))
    half = 0.5 * (1.0 + t)
    g = x * half
    dg = half + 0.5 * x * (1.0 - t * t) * (GELU_C * (1.0 + 3.0 * GELU_A * x2))
    return g, dg


def _gelu(x):
    t = jnp.tanh(GELU_C * (x + GELU_A * x * x * x))
    return 0.5 * x * (1.0 + t)


def _sigmoid(x):
    return 1.0 / (1.0 + jnp.exp(-x))


def _softplus(x):
    e = jnp.exp(-jnp.abs(x))
    u = 1.0 + e
    log1p_e = jnp.where(u == 1.0, e, jnp.log(u) * (e / (u - 1.0)))
    return jnp.maximum(x, 0.0) + log1p_e


def _rms(x):
    ms = jnp.mean(x * x, axis=-1, keepdims=True)
    return lax.rsqrt(ms + EPS)


def _dot(a, b, dims):
    return lax.dot_general(a, b, (dims, ((), ())), preferred_element_type=F32)


NN = ((1,), (0,))
NT = ((1,), (1,))
TN = ((0,), (0,))


def _matmul(a, b, *, dims, grid, a_spec, b_spec, o_spec, out_shape, acc_shape, name,
            residual=None, r_spec=None):
    nk = grid[2]

    def body(*refs):
        if residual is None:
            a_ref, b_ref, o_ref, acc_ref = refs
            r_ref = None
        else:
            a_ref, b_ref, r_ref, o_ref, acc_ref = refs
        k = pl.program_id(2)

        @pl.when(k == 0)
        def _():
            acc_ref[...] = jnp.zeros_like(acc_ref)

        acc_ref[...] += _dot(a_ref[...].astype(BF16), b_ref[...].astype(BF16), dims)

        @pl.when(k == nk - 1)
        def _():
            res = acc_ref[...]
            if r_ref is not None:
                res = res + r_ref[...]
            o_ref[...] = res.astype(o_ref.dtype)

    in_specs = [a_spec, b_spec]
    args = [a, b]
    if residual is not None:
        in_specs.append(r_spec)
        args.append(residual)
    return pl.pallas_call(
        body, name=name, grid=grid, in_specs=in_specs, out_specs=o_spec, out_shape=out_shape,
        scratch_shapes=[pltpu.VMEM(acc_shape, F32)],
        compiler_params=_params("parallel", "parallel", "arbitrary"),
    )(*args)


def _mm_nn(a, b, *, tm, tn, tk, out_dtype, name, residual=None):
    m, kd = a.shape
    n = b.shape[1]
    return _matmul(
        a, b, dims=NN, grid=(m // tm, n // tn, kd // tk),
        a_spec=pl.BlockSpec((tm, tk), lambda i, j, k: (i, k)),
        b_spec=pl.BlockSpec((tk, tn), lambda i, j, k: (k, j)),
        o_spec=pl.BlockSpec((tm, tn), lambda i, j, k: (i, j)),
        out_shape=jax.ShapeDtypeStruct((m, n), out_dtype), acc_shape=(tm, tn), name=name,
        residual=residual, r_spec=pl.BlockSpec((tm, tn), lambda i, j, k: (i, j)))


def _mm_nt(a, b, *, tm, tn, tk, out_dtype, name):
    m, kd = a.shape
    n = b.shape[0]
    return _matmul(
        a, b, dims=NT, grid=(m // tm, n // tn, kd // tk),
        a_spec=pl.BlockSpec((tm, tk), lambda i, j, k: (i, k)),
        b_spec=pl.BlockSpec((tn, tk), lambda i, j, k: (j, k)),
        o_spec=pl.BlockSpec((tm, tn), lambda i, j, k: (i, j)),
        out_shape=jax.ShapeDtypeStruct((m, n), out_dtype), acc_shape=(tm, tn), name=name)


def _mm_tn(a, b, *, tm, tn, tk, out_dtype, name):
    kd, m = a.shape
    n = b.shape[1]
    return _matmul(
        a, b, dims=TN, grid=(m // tm, n // tn, kd // tk),
        a_spec=pl.BlockSpec((tk, tm), lambda i, j, k: (k, i)),
        b_spec=pl.BlockSpec((tk, tn), lambda i, j, k: (k, j)),
        o_spec=pl.BlockSpec((tm, tn), lambda i, j, k: (i, j)),
        out_shape=jax.ShapeDtypeStruct((m, n), out_dtype), acc_shape=(tm, tn), name=name)


def _mm_up(h2, w_up_b, *, tm, name):
    s = h2.shape[0]
    nb = w_up_b.shape[2]
    per_half = D_FF // nb
    return _matmul(
        h2, w_up_b, dims=NN, grid=(s // tm, N_DEV, 1),
        a_spec=pl.BlockSpec((tm, D_MODEL), lambda i, j, k: (i, 0)),
        b_spec=pl.BlockSpec((None, D_MODEL, nb), lambda i, j, k: (j, 0, 0)),
        o_spec=pl.BlockSpec((None, tm, nb), lambda i, j, k: (j // per_half, i, j % per_half)),
        out_shape=jax.ShapeDtypeStruct((2, s, D_FF), F32), acc_shape=(tm, nb), name=name)


def _mm_up_bwd_x(dp, w_up_b, *, tm, name):
    s = dp.shape[1]
    nb = w_up_b.shape[2]
    per_half = D_FF // nb
    return _matmul(
        dp, w_up_b, dims=NT, grid=(s // tm, 1, N_DEV),
        a_spec=pl.BlockSpec((None, tm, nb), lambda i, j, k: (k // per_half, i, k % per_half)),
        b_spec=pl.BlockSpec((None, D_MODEL, nb), lambda i, j, k: (k, 0, 0)),
        o_spec=pl.BlockSpec((tm, D_MODEL), lambda i, j, k: (i, 0)),
        out_shape=jax.ShapeDtypeStruct((s, D_MODEL), F32), acc_shape=(tm, D_MODEL), name=name)


def _mm_up_bwd_w(h2, dp, *, tm, tk, name):
    s = h2.shape[0]
    nb = D_FF * 2 // N_DEV
    per_half = D_FF // nb
    return _matmul(
        h2, dp, dims=TN, grid=(D_MODEL // tm, N_DEV, s // tk),
        a_spec=pl.BlockSpec((tk, tm), lambda i, j, k: (k, i)),
        b_spec=pl.BlockSpec((None, tk, nb), lambda i, j, k: (j // per_half, k, j % per_half)),
        o_spec=pl.BlockSpec((None, tm, nb), lambda i, j, k: (j, i, 0)),
        out_shape=jax.ShapeDtypeStruct((N_DEV, D_MODEL, nb), BF16), acc_shape=(tm, nb), name=name)


def _norm_fwd(x, g, *, tm, name):
    s, d = x.shape

    def body(x_ref, g_ref, h_ref):
        xv = x_ref[...]
        h_ref[...] = (xv * _rms(xv) * g_ref[...]).astype(BF16)

    return pl.pallas_call(
        body, name=name, grid=(s // tm,),
        in_specs=[pl.BlockSpec((tm, d), lambda i: (i, 0)), pl.BlockSpec((1, d), lambda i: (0, 0))],
        out_specs=pl.BlockSpec((tm, d), lambda i: (i, 0)),
        out_shape=jax.ShapeDtypeStruct((s, d), BF16),
        compiler_params=_params("parallel"),
    )(x, g.reshape(1, d))


def _behind(token):
    return jnp.zeros((8, 128), F32) if token is None else token


def _norm_bwd(dh, x, g, dres, *, tm, name, token=None):
    s, d = x.shape

    def body(dh_ref, x_ref, g_ref, dres_ref, token_ref, dx_ref, dg_ref):
        @pl.when(pl.program_id(0) == 0)
        def _():
            dg_ref[...] = jnp.zeros_like(dg_ref)

        xv = x_ref[...]
        rstd = _rms(xv)
        n = xv * rstd
        dhv = dh_ref[...]
        dn = dhv * g_ref[...]
        dx = rstd * (dn - n * jnp.mean(dn * n, axis=-1, keepdims=True))
        dx_ref[...] = dres_ref[...] + dx
        dg_ref[0:1, :] += jnp.sum(dhv * n, axis=0, keepdims=True)

    return pl.pallas_call(
        body, name=name, grid=(s // tm,),
        in_specs=[pl.BlockSpec((tm, d), lambda i: (i, 0)), pl.BlockSpec((tm, d), lambda i: (i, 0)),
                  pl.BlockSpec((1, d), lambda i: (0, 0)), pl.BlockSpec((tm, d), lambda i: (i, 0)),
                  pl.BlockSpec(memory_space=pl.ANY)],
        out_specs=[pl.BlockSpec((tm, d), lambda i: (i, 0)), pl.BlockSpec((8, d), lambda i: (0, 0))],
        out_shape=[jax.ShapeDtypeStruct((s, d), F32), jax.ShapeDtypeStruct((8, d), F32)],
        compiler_params=_params("arbitrary"),
    )(dh, x, g.reshape(1, d), dres, _behind(token))


def _loss_head(x, g, tgt, *, tm, name):
    s, d = x.shape

    def body(x_ref, g_ref, t_ref, loss_ref, dx_ref, dg_ref):
        @pl.when(pl.program_id(0) == 0)
        def _():
            dg_ref[...] = jnp.zeros_like(dg_ref)
            loss_ref[...] = jnp.zeros_like(loss_ref)

        xv = x_ref[...]
        gv = g_ref[...]
        rstd = _rms(xv)
        n = xv * rstd
        e = n * gv - t_ref[...]
        part = 0.5 * jnp.sum(jnp.mean(e * e, axis=-1, keepdims=True), axis=0, keepdims=True)
        loss_ref[...] += jnp.broadcast_to(part, loss_ref.shape)
        dy = e * (1.0 / d)
        dn = dy * gv
        dx_ref[...] = rstd * (dn - n * jnp.mean(dn * n, axis=-1, keepdims=True))
        dg_ref[0:1, :] += jnp.sum(dy * n, axis=0, keepdims=True)

    return pl.pallas_call(
        body, name=name, grid=(s // tm,),
        in_specs=[pl.BlockSpec((tm, d), lambda i: (i, 0)), pl.BlockSpec((1, d), lambda i: (0, 0)),
                  pl.BlockSpec((tm, d), lambda i: (i, 0))],
        out_specs=[pl.BlockSpec((8, 128), lambda i: (0, 0)), pl.BlockSpec((tm, d), lambda i: (i, 0)),
                   pl.BlockSpec((8, d), lambda i: (0, 0))],
        out_shape=[jax.ShapeDtypeStruct((8, 128), F32), jax.ShapeDtypeStruct((s, d), F32),
                   jax.ShapeDtypeStruct((8, d), F32)],
        compiler_params=_params("arbitrary"),
    )(x, g.reshape(1, d), tgt)


def _scan_rows(a_ref, b_ref, h_ref, carry, *, rows, reverse):
    width = a_ref.shape[1]
    n_chunks = rows // 8
    row = lax.broadcasted_iota(jnp.int32, (8, width), 0)

    def step(ci, carry):
        chunk = (n_chunks - 1 - ci) if reverse else ci
        off = pl.multiple_of(chunk * 8, 8)
        av = a_ref[pl.ds(off, 8), :]
        bv = b_ref[pl.ds(off, 8), :]
        for sh in (1, 2, 4):
            if reverse:
                a_sh = pltpu.roll(av, 8 - sh, 0)
                b_sh = pltpu.roll(bv, 8 - sh, 0)
                m = row < 8 - sh
            else:
                a_sh = pltpu.roll(av, sh, 0)
                b_sh = pltpu.roll(bv, sh, 0)
                m = row >= sh
            bv = jnp.where(m, av * b_sh + bv, bv)
            av = jnp.where(m, av * a_sh, av)
        h = av * carry + bv
        h_ref[pl.ds(off, 8), :] = h
        return h[0:1, :] if reverse else h[7:8, :]

    return lax.fori_loop(0, n_chunks, step, carry)


def _lru_gates(lx, wa_ref, wx_ref, ba, bx, sp):
    lxb = lx.astype(BF16)
    pre_r = jnp.concatenate(
        [_dot(lxb[:, g * LRU_GROUP:(g + 1) * LRU_GROUP], wa_ref[g], NN) for g in range(N_GROUPS)], axis=1)
    pre_i = jnp.concatenate(
        [_dot(lxb[:, g * LRU_GROUP:(g + 1) * LRU_GROUP], wx_ref[g], NN) for g in range(N_GROUPS)], axis=1)
    r = _sigmoid(pre_r + ba)
    ig = _sigmoid(pre_i + bx)
    log_a = (-RG_C * r) * sp
    a = jnp.exp(log_a)
    mult = jnp.sqrt(-jnp.tanh(log_a) * (a * a + 1.0))
    return lxb, r, ig, a, mult


def _mixer_fwd(z, cw, cb, wa_bd, wx_bd, ba, bx, lam, scw, *, tile, name):
    s = z.shape[0]
    n_tiles = s // tile

    def body(z_ref, cw_ref, cb_ref, wa_ref, wx_ref, ba_ref, bx_ref, lam_ref, scw_ref,
             y_ref, hs_ref, ext_lx, ext_q, a_s, b_s, h_car):
        i = pl.program_id(0)

        @pl.when(i == 0)
        def _():
            ext_lx[0:HALO, :] = jnp.zeros((HALO, D_LRU), F32)
            ext_q[0:HALO, :] = jnp.zeros((HALO, D_SC), F32)
            h_car[...] = jnp.zeros_like(h_car)

        ext_lx[HALO:HALO + tile, :] = z_ref[:, 0:D_LRU]
        ext_q[HALO:HALO + tile, :] = z_ref[:, 2 * D_LRU + D_SC:2 * D_LRU + 2 * D_SC] * z_ref[:, 2 * D_LRU + 2 * D_SC:D_IN]
        lx = cb_ref[...] + cw_ref[0:1, :] * ext_lx[pl.ds(HALO - 3, tile), :]
        for k in range(1, 4):
            lx = lx + cw_ref[k:k + 1, :] * ext_lx[pl.ds(HALO - 3 + k, tile), :]
        cq = scw_ref[0:1, :] * ext_q[pl.ds(HALO - 2, tile), :]
        for k in range(1, 3):
            cq = cq + scw_ref[k:k + 1, :] * ext_q[pl.ds(HALO - 2 + k, tile), :]
        ext_lx[0:HALO, :] = ext_lx[tile:tile + HALO, :]
        ext_q[0:HALO, :] = ext_q[tile:tile + HALO, :]

        sp = _softplus(-lam_ref[...])
        _, _, ig, a, mult = _lru_gates(lx, wa_ref, wx_ref, ba_ref[...], bx_ref[...], sp)
        a_s[...] = a
        b_s[...] = mult * (ig * lx)
        h_car[0:1, :] = _scan_rows(a_s, b_s, hs_ref, h_car[0:1, :], rows=tile, reverse=False)

        y_ref[:, 0:D_LRU] = (hs_ref[...] * _gelu(z_ref[:, D_LRU:2 * D_LRU])).astype(BF16)
        y_ref[:, D_LRU:D_MIX] = (z_ref[:, 2 * D_LRU:2 * D_LRU + D_SC] * cq).astype(BF16)

    full = lambda shape: pl.BlockSpec(shape, lambda i: (0,) * len(shape))
    return pl.pallas_call(
        body, name=name, grid=(n_tiles,),
        in_specs=[pl.BlockSpec((tile, D_IN), lambda i: (i, 0)),
                  full((4, D_LRU)), full((1, D_LRU)),
                  full((N_GROUPS, LRU_GROUP, LRU_GROUP)), full((N_GROUPS, LRU_GROUP, LRU_GROUP)),
                  full((1, D_LRU)), full((1, D_LRU)), full((1, D_LRU)), full((3, D_SC))],
        out_specs=[pl.BlockSpec((tile, D_MIX), lambda i: (i, 0)), pl.BlockSpec((tile, D_LRU), lambda i: (i, 0))],
        out_shape=[jax.ShapeDtypeStruct((s, D_MIX), BF16), jax.ShapeDtypeStruct((s, D_LRU), F32)],
        scratch_shapes=[pltpu.VMEM((tile + HALO, D_LRU), F32), pltpu.VMEM((tile + HALO, D_SC), F32),
                        pltpu.VMEM((tile, D_LRU), F32), pltpu.VMEM((tile, D_LRU), F32),
                        pltpu.VMEM((8, D_LRU), F32)],
        compiler_params=_params("arbitrary"),
    )(z, cw, cb.reshape(1, -1), wa_bd, wx_bd, ba.reshape(1, -1), bx.reshape(1, -1), lam.reshape(1, -1), scw)


def _mixer_bwd(z, hs, dy, cw, cb, wa_bd, wx_bd, ba, bx, lam, scw, *, tile, name, token=None):
    s = z.shape[0]
    n_tiles = s // tile
    per8 = tile // 8

    def body(z_ref, zp_ref, hs_ref, hsp_ref, dy_ref, cw_ref, cb_ref, wa_ref, wx_ref, ba_ref, bx_ref, lam_ref, scw_ref,
             token_ref, dz_ref, dcw_ref, dvec_ref, dwa_ref, dwx_ref, dscw_ref,
             ext_lx, ext_q, ext_h, ext_a, ext_dlx, ext_dcq, a_s, b_s, lam_s, l_car):
        i = pl.program_id(0)
        first_tile = i == n_tiles - 1

        @pl.when(i == 0)
        def _():
            for ref in (dcw_ref, dvec_ref, dwa_ref, dwx_ref, dscw_ref, l_car):
                ref[...] = jnp.zeros_like(ref)
            ext_a[tile:tile + HALO, :] = jnp.zeros((HALO, D_LRU), F32)
            ext_dlx[tile:tile + HALO, :] = jnp.zeros((HALO, D_LRU), F32)
            ext_dcq[tile:tile + HALO, :] = jnp.zeros((HALO, D_SC), F32)

        keep = jnp.where(first_tile, 0.0, 1.0)
        sb = z_ref[:, 2 * D_LRU:2 * D_LRU + D_SC]
        sc = z_ref[:, 2 * D_LRU + D_SC:2 * D_LRU + 2 * D_SC]
        sx = z_ref[:, 2 * D_LRU + 2 * D_SC:D_IN]
        ext_lx[0:HALO, :] = zp_ref[:, 0:D_LRU] * keep
        ext_lx[HALO:HALO + tile, :] = z_ref[:, 0:D_LRU]
        ext_q[0:HALO, :] = zp_ref[:, 2 * D_LRU + D_SC:2 * D_LRU + 2 * D_SC] * zp_ref[:, 2 * D_LRU + 2 * D_SC:D_IN] * keep
        ext_q[HALO:HALO + tile, :] = sc * sx
        ext_h[0:HALO, :] = hsp_ref[...] * keep
        ext_h[HALO:HALO + tile, :] = hs_ref[...]

        lx = cb_ref[...] + cw_ref[0:1, :] * ext_lx[pl.ds(HALO - 3, tile), :]
        for k in range(1, 4):
            lx = lx + cw_ref[k:k + 1, :] * ext_lx[pl.ds(HALO - 3 + k, tile), :]
        cq = scw_ref[0:1, :] * ext_q[pl.ds(HALO - 2, tile), :]
        for k in range(1, 3):
            cq = cq + scw_ref[k:k + 1, :] * ext_q[pl.ds(HALO - 2 + k, tile), :]

        sp = _softplus(-lam_ref[...])
        lxb, r, ig, a, mult = _lru_gates(lx, wa_ref, wx_ref, ba_ref[...], bx_ref[...], sp)

        ge, dge = _gelu_parts(z_ref[:, D_LRU:2 * D_LRU])
        dy_lru = dy_ref[:, 0:D_LRU]
        dz_ref[:, D_LRU:2 * D_LRU] = (dy_lru * hs_ref[...] * dge).astype(BF16)

        ext_a[0:tile, :] = a
        a_s[...] = ext_a[pl.ds(1, tile), :]
        b_s[...] = dy_lru * ge
        l_car[0:1, :] = _scan_rows(a_s, b_s, lam_s, l_car[0:1, :], rows=tile, reverse=True)
        ext_a[tile:tile + HALO, :] = ext_a[0:HALO, :]
        lv = lam_s[...]

        da = lv * ext_h[pl.ds(HALO - 1, tile), :]
        d_mult = lv * ig * lx
        d_i = lv * mult * lx
        dlx = lv * mult * ig
        dlog_a = da * a - d_mult * (a * a) / mult
        d_r = dlog_a * (-RG_C * sp)
        dpre_r = d_r * r * (1.0 - r)
        dpre_i = d_i * ig * (1.0 - ig)
        dvec_ref[1:2, :] += jnp.sum(dpre_r, axis=0, keepdims=True)
        dvec_ref[2:3, :] += jnp.sum(dpre_i, axis=0, keepdims=True)
        dvec_ref[3:4, :] += jnp.sum(dlog_a * (-RG_C * r), axis=0, keepdims=True)
        dpr_b = dpre_r.astype(BF16)
        dpi_b = dpre_i.astype(BF16)
        back = []
        for g in range(N_GROUPS):
            cols = slice(g * LRU_GROUP, (g + 1) * LRU_GROUP)
            dwa_ref[g] += _dot(lxb[:, cols], dpr_b[:, cols], TN)
            dwx_ref[g] += _dot(lxb[:, cols], dpi_b[:, cols], TN)
            back.append(_dot(dpr_b[:, cols], wa_ref[g], NT) + _dot(dpi_b[:, cols], wx_ref[g], NT))
        dlx = dlx + jnp.concatenate(back, axis=1)
        dvec_ref[0:1, :] += jnp.sum(dlx, axis=0, keepdims=True)

        ext_dlx[0:tile, :] = dlx
        for k in range(4):
            dcw_ref[k:k + 1, :] += jnp.sum(dlx * ext_lx[pl.ds(HALO - 3 + k, tile), :], axis=0, keepdims=True)
        dlxp = cw_ref[3:4, :] * dlx
        for k in range(3):
            dlxp = dlxp + cw_ref[k:k + 1, :] * ext_dlx[pl.ds(3 - k, tile), :]
        dz_ref[:, 0:D_LRU] = dlxp.astype(BF16)
        ext_dlx[tile:tile + HALO, :] = ext_dlx[0:HALO, :]

        dy_sc = dy_ref[:, D_LRU:D_MIX]
        dz_ref[:, 2 * D_LRU:2 * D_LRU + D_SC] = (dy_sc * cq).astype(BF16)
        dcq = dy_sc * sb
        ext_dcq[0:tile, :] = dcq
        for k in range(3):
            dscw_ref[k:k + 1, :] += jnp.sum(dcq * ext_q[pl.ds(HALO - 2 + k, tile), :], axis=0, keepdims=True)
        dq = scw_ref[2:3, :] * dcq
        for k in range(2):
            dq = dq + scw_ref[k:k + 1, :] * ext_dcq[pl.ds(2 - k, tile), :]
        dz_ref[:, 2 * D_LRU + D_SC:2 * D_LRU + 2 * D_SC] = (dq * sx).astype(BF16)
        dz_ref[:, 2 * D_LRU + 2 * D_SC:D_IN] = (dq * sc).astype(BF16)
        ext_dcq[tile:tile + HALO, :] = ext_dcq[0:HALO, :]

        @pl.when(i == n_tiles - 1)
        def _():
            dvec_ref[3:4, :] = dvec_ref[3:4, :] * (-_sigmoid(-lam_ref[...]))

    rev = lambda i: n_tiles - 1 - i
    prev8 = lambda i: jnp.maximum(rev(i) * per8 - 1, 0)
    full = lambda shape: pl.BlockSpec(shape, lambda i: (0,) * len(shape))
    return pl.pallas_call(
        body, name=name, grid=(n_tiles,),
        in_specs=[pl.BlockSpec((tile, D_IN), lambda i: (rev(i), 0)),
                  pl.BlockSpec((HALO, D_IN), lambda i: (prev8(i), 0)),
                  pl.BlockSpec((tile, D_LRU), lambda i: (rev(i), 0)),
                  pl.BlockSpec((HALO, D_LRU), lambda i: (prev8(i), 0)),
                  pl.BlockSpec((tile, D_MIX), lambda i: (rev(i), 0)),
                  full((4, D_LRU)), full((1, D_LRU)),
                  full((N_GROUPS, LRU_GROUP, LRU_GROUP)), full((N_GROUPS, LRU_GROUP, LRU_GROUP)),
                  full((1, D_LRU)), full((1, D_LRU)), full((1, D_LRU)), full((3, D_SC)),
                  pl.BlockSpec(memory_space=pl.ANY)],
        out_specs=[pl.BlockSpec((tile, D_IN), lambda i: (rev(i), 0)),
                   full((8, D_LRU)), full((8, D_LRU)),
                   full((N_GROUPS, LRU_GROUP, LRU_GROUP)), full((N_GROUPS, LRU_GROUP, LRU_GROUP)),
                   full((8, D_SC))],
        out_shape=[jax.ShapeDtypeStruct((s, D_IN), BF16),
                   jax.ShapeDtypeStruct((8, D_LRU), F32), jax.ShapeDtypeStruct((8, D_LRU), F32),
                   jax.ShapeDtypeStruct((N_GROUPS, LRU_GROUP, LRU_GROUP), F32),
                   jax.ShapeDtypeStruct((N_GROUPS, LRU_GROUP, LRU_GROUP), F32),
                   jax.ShapeDtypeStruct((8, D_SC), F32)],
        scratch_shapes=[pltpu.VMEM((tile + HALO, D_LRU), F32), pltpu.VMEM((tile + HALO, D_SC), F32),
                        pltpu.VMEM((tile + HALO, D_LRU), F32), pltpu.VMEM((tile + HALO, D_LRU), F32),
                        pltpu.VMEM((tile + HALO, D_LRU), F32), pltpu.VMEM((tile + HALO, D_SC), F32),
                        pltpu.VMEM((tile, D_LRU), F32), pltpu.VMEM((tile, D_LRU), F32),
                        pltpu.VMEM((tile, D_LRU), F32), pltpu.VMEM((8, D_LRU), F32)],
        compiler_params=_params("arbitrary"),
    )(z, z, hs, hs, dy, cw, cb.reshape(1, -1), wa_bd, wx_bd, ba.reshape(1, -1), bx.reshape(1, -1),
      lam.reshape(1, -1), scw, _behind(token))


def _ffn_fwd(p, fcw, *, tile, tc, name):
    s = p.shape[1]
    per8 = tile // 8

    def body(p_ref, pp_ref, w_ref, act_ref, ext_p):
        i = pl.program_id(0)
        keep = jnp.where(i == 0, 0.0, 1.0)
        ext_p[:, 0:HALO, :] = pp_ref[...] * keep
        ext_p[:, HALO:HALO + tile, :] = p_ref[...]
        u = []
        for half in range(2):
            acc = w_ref[half, 0:1, :] * ext_p[half, pl.ds(HALO - 2, tile), :]
            for k in range(1, 3):
                acc = acc + w_ref[half, k:k + 1, :] * ext_p[half, pl.ds(HALO - 2 + k, tile), :]
            u.append(acc)
        act_ref[...] = (_gelu(u[0]) * u[1]).astype(BF16)

    return pl.pallas_call(
        body, name=name, grid=(s // tile, D_FF // tc),
        in_specs=[pl.BlockSpec((2, tile, tc), lambda i, j: (0, i, j)),
                  pl.BlockSpec((2, HALO, tc), lambda i, j: (0, jnp.maximum(i * per8 - 1, 0), j)),
                  pl.BlockSpec((2, 3, tc), lambda i, j: (0, 0, j))],
        out_specs=pl.BlockSpec((tile, tc), lambda i, j: (i, j)),
        out_shape=jax.ShapeDtypeStruct((s, D_FF), BF16),
        scratch_shapes=[pltpu.VMEM((2, tile + HALO, tc), F32)],
        compiler_params=_params("parallel", "parallel"),
    )(p, p, fcw)


def _ffn_bwd(p, dact, fcw, *, tile, tc, name, token=None):
    s = p.shape[1]
    n_tiles = s // tile
    per8 = tile // 8
    ext_rows = tile + HALO

    def body(p_ref, pp_ref, pn_ref, da_ref, dan_ref, w_ref, token_ref, dp_ref, dw_ref, ext_p, ext_du):
        i = pl.program_id(1)

        @pl.when(i == 0)
        def _():
            dw_ref[...] = jnp.zeros_like(dw_ref)

        keep_prev = jnp.where(i == 0, 0.0, 1.0)
        keep_next = jnp.where(i == n_tiles - 1, 0.0, 1.0)
        ext_p[:, 0:HALO, :] = pp_ref[...] * keep_prev
        ext_p[:, HALO:HALO + tile, :] = p_ref[...]
        ext_p[:, HALO + tile:2 * HALO + tile, :] = pn_ref[...] * keep_next
        u = []
        for half in range(2):
            acc = w_ref[half, 0:1, :] * ext_p[half, pl.ds(HALO - 2, ext_rows), :]
            for k in range(1, 3):
                acc = acc + w_ref[half, k:k + 1, :] * ext_p[half, pl.ds(HALO - 2 + k, ext_rows), :]
            u.append(acc)
        ge, dge = _gelu_parts(u[0])
        da = jnp.concatenate([da_ref[...], dan_ref[...] * keep_next], axis=0)
        ext_du[0, :, :] = da * u[1] * dge
        ext_du[1, :, :] = da * ge
        for half in range(2):
            du = ext_du[half, 0:tile, :]
            acc = w_ref[half, 2:3, :] * du
            for k in range(2):
                acc = acc + w_ref[half, k:k + 1, :] * ext_du[half, pl.ds(2 - k, tile), :]
            dp_ref[half, :, :] = acc.astype(BF16)
            for k in range(3):
                dw_ref[half, k:k + 1, :] += jnp.sum(du * ext_p[half, pl.ds(HALO - 2 + k, tile), :], axis=0, keepdims=True)

    return pl.pallas_call(
        body, name=name, grid=(D_FF // tc, n_tiles),
        in_specs=[pl.BlockSpec((2, tile, tc), lambda j, i: (0, i, j)),
                  pl.BlockSpec((2, HALO, tc), lambda j, i: (0, jnp.maximum(i * per8 - 1, 0), j)),
                  pl.BlockSpec((2, HALO, tc), lambda j, i: (0, jnp.minimum((i + 1) * per8, n_tiles * per8 - 1), j)),
                  pl.BlockSpec((tile, tc), lambda j, i: (i, j)),
                  pl.BlockSpec((HALO, tc), lambda j, i: (jnp.minimum((i + 1) * per8, n_tiles * per8 - 1), j)),
                  pl.BlockSpec((2, 3, tc), lambda j, i: (0, 0, j)), pl.BlockSpec(memory_space=pl.ANY)],
        out_specs=[pl.BlockSpec((2, tile, tc), lambda j, i: (0, i, j)),
                   pl.BlockSpec((2, 8, tc), lambda j, i: (0, 0, j))],
        out_shape=[jax.ShapeDtypeStruct((2, s, D_FF), BF16), jax.ShapeDtypeStruct((2, 8, D_FF), F32)],
        scratch_shapes=[pltpu.VMEM((2, tile + 2 * HALO, tc), F32), pltpu.VMEM((2, ext_rows, tc), F32)],
        compiler_params=_params("parallel", "arbitrary"),
    )(p, p, p, dact, dact, fcw, _behind(token))


def _adamw_math(w, g, m, v):
    m = ADAM_B1 * m + (1.0 - ADAM_B1) * g
    v = ADAM_B2 * v + (1.0 - ADAM_B2) * (g * g)
    m_hat = m / (1.0 - ADAM_B1 ** ADAM_STEP)
    v_hat = v / (1.0 - ADAM_B2 ** ADAM_STEP)
    delta = -ADAM_LR * (m_hat / (jnp.sqrt(v_hat) + ADAM_EPS) + ADAM_WD * w)
    return delta, m, v


def _adamw(w, g, m, v, *, name):
    rows, cols = w.shape
    tr = rows
    for cand in (512, 256, 128, 64, 32, 16, 8):
        if rows % cand == 0 and rows > cand:
            tr = cand
            break

    def body(w_ref, g_ref, m_ref, v_ref, d_ref, nm_ref, nv_ref):
        d, nm, nv = _adamw_math(w_ref[...], g_ref[...], m_ref[...], v_ref[...])
        d_ref[...] = d
        nm_ref[...] = nm
        nv_ref[...] = nv

    spec = pl.BlockSpec((tr, cols), lambda i: (i, 0))
    return pl.pallas_call(
        body, name=name, grid=(rows // tr,), in_specs=[spec] * 4, out_specs=[spec] * 3,
        out_shape=[jax.ShapeDtypeStruct((rows, cols), F32)] * 3,
        compiler_params=_params("parallel"),
    )(w, g, m, v)


def _sum_parts(parts, *, name):
    _, rows, cols = parts.shape
    tr = rows
    for cand in (256, 128, 64, 32, 16):
        if rows % cand == 0 and rows > cand:
            tr = cand
            break

    def body(p_ref, o_ref):
        acc = p_ref[0].astype(F32)
        for d in range(1, N_DEV):
            acc = acc + p_ref[d].astype(F32)
        o_ref[...] = acc

    return pl.pallas_call(
        body, name=name, grid=(rows // tr,),
        in_specs=[pl.BlockSpec((N_DEV, tr, cols), lambda i: (0, i, 0))],
        out_specs=pl.BlockSpec((tr, cols), lambda i: (i, 0)),
        out_shape=jax.ShapeDtypeStruct((rows, cols), F32),
        compiler_params=_params("parallel"),
    )(parts)


def _place():
    return lax.axis_index("x"), lax.axis_index("y"), lax.axis_index("c")


def _flip(v, bit):
    return 1 - v if bit else v


N_PEERS = N_DEV - 1


def _peer_copy(k, src_ref, land_ref, send_sem, recv_sem, gather):
    x, y, c = _place()
    my_id = 4 * x + 2 * y + c
    px, py, pc = _flip(x, k & 4), _flip(y, k & 2), _flip(c, k & 1)
    peer_id = 4 * px + 2 * py + pc
    return pltpu.make_async_remote_copy(
        src_ref=src_ref if gather else src_ref.at[peer_id], dst_ref=land_ref.at[my_id],
        send_sem=send_sem.at[k - 1], recv_sem=recv_sem.at[k - 1],
        device_id=(px, py, pc), device_id_type=MESH)


def _sequencer_copies(srcs, *, gather, name, collective_id, after):
    n = len(srcs)
    hbm = pltpu.MemorySpace.HBM
    src_refs = [jax.new_ref(s, memory_space=hbm) for s in srcs]
    land_refs = [jax.empty_ref(jax.ShapeDtypeStruct(((N_DEV,) + s.shape) if gather else s.shape, s.dtype),
                               memory_space=hbm) for s in srcs]
    token_in = jax.new_ref(jnp.zeros((8, 128), F32) if after is None else after, memory_space=hbm)
    token_out = jax.empty_ref(jax.ShapeDtypeStruct((8, 128), F32), memory_space=hbm)

    @pl.kernel(mesh=plsc.ScalarSubcoreMesh(axis_name="seq", num_cores=1), name=name,
               scratch_types=(pltpu.SemaphoreType.DMA((n, N_PEERS)), pltpu.SemaphoreType.DMA((n, N_PEERS)),
                              pltpu.SemaphoreType.DMA((n + 1,))),
               compiler_params=pltpu.CompilerParams(collective_id=collective_id))
    def launch(send_sems, recv_sems, local_sems):
        x, y, c = _place()
        my_id = 4 * x + 2 * y + c
        barrier = pltpu.get_barrier_semaphore()
        for k in range(1, N_DEV):
            peer = (_flip(x, k & 4), _flip(y, k & 2), _flip(c, k & 1))
            pl.semaphore_signal(barrier, inc=1, device_id=peer, device_id_type=MESH)
        pl.semaphore_wait(barrier, N_PEERS)
        own = [pltpu.make_async_copy(src_refs[t] if gather else src_refs[t].at[my_id], land_refs[t].at[my_id],
                                     local_sems.at[t]) for t in range(n)]
        for cp in own:
            cp.start()
        copies = [_peer_copy(k, src_refs[t], land_refs[t], send_sems.at[t], recv_sems.at[t], gather)
                  for t in range(n) for k in range(1, N_DEV)]
        for cp in copies:
            cp.start()
        for cp in own:
            cp.wait()
        for cp in copies:
            cp.wait()
        passed = pltpu.make_async_copy(token_in, token_out, local_sems.at[n])
        passed.start()
        passed.wait()

    launch()
    return [ref[...] for ref in land_refs], token_out[...]


def _all_reduce_small(buf, *, name):
    _, rows, lanes = buf.shape

    def body(in_ref, out_ref, parts, send_sems, recv_sems):
        x, y, c = _place()
        my_id = 4 * x + 2 * y + c
        peers = []
        for k in range(1, N_DEV):
            px, py, pc = _flip(x, k & 4), _flip(y, k & 2), _flip(c, k & 1)
            peers.append(((px, py, pc), 4 * px + 2 * py + pc))
        scatter = [pltpu.make_async_remote_copy(
            src_ref=in_ref.at[pid], dst_ref=parts.at[my_id],
            send_sem=send_sems.at[0, k], recv_sem=recv_sems.at[0, k],
            device_id=peer, device_id_type=MESH) for k, (peer, pid) in enumerate(peers)]
        for cp in scatter:
            cp.start()
        parts[my_id] = in_ref[my_id]
        for cp in scatter:
            cp.wait()
        total = parts[0]
        for d in range(1, N_DEV):
            total = total + parts[d]
        out_ref[my_id] = total
        gather = [pltpu.make_async_remote_copy(
            src_ref=out_ref.at[my_id], dst_ref=out_ref.at[my_id],
            send_sem=send_sems.at[1, k], recv_sem=recv_sems.at[1, k],
            device_id=peer, device_id_type=MESH) for k, (peer, pid) in enumerate(peers)]
        for cp in gather:
            cp.start()
        for k, (peer, pid) in enumerate(peers):
            pltpu.make_async_remote_copy(
                src_ref=out_ref.at[pid], dst_ref=out_ref.at[pid],
                send_sem=send_sems.at[1, k], recv_sem=recv_sems.at[1, k],
                device_id=peer, device_id_type=MESH).wait()

    vmem = pl.BlockSpec(memory_space=pltpu.VMEM)
    return pl.pallas_call(
        body, name=name, in_specs=[vmem], out_specs=vmem,
        out_shape=jax.ShapeDtypeStruct(buf.shape, F32),
        scratch_shapes=[pltpu.VMEM(buf.shape, F32),
                        pltpu.SemaphoreType.DMA((2, 7)), pltpu.SemaphoreType.DMA((2, 7))],
        compiler_params=pltpu.CompilerParams(vmem_limit_bytes=VMEM_LIMIT),
    )(buf)


TM = 512
MIX_TILE = 128
FFN_TILE = 256
FFN_TC = 512


def _block_diag(w):
    wg = w.reshape(N_GROUPS, HEADS_PER_GROUP, LRU_HEAD_DIM, LRU_HEAD_DIM)
    eye = jnp.eye(HEADS_PER_GROUP, dtype=w.dtype)
    bd = wg[:, :, :, None, :] * eye[None, :, None, :, None]
    return bd.reshape(N_GROUPS, LRU_GROUP, LRU_GROUP).astype(BF16)


def _head_blocks(bd):
    b5 = bd.reshape(N_GROUPS, HEADS_PER_GROUP, LRU_HEAD_DIM, HEADS_PER_GROUP, LRU_HEAD_DIM)
    blocks = [b5[:, h, :, h, :] for h in range(HEADS_PER_GROUP)]
    return jnp.stack(blocks, axis=1).reshape(LRU_HEADS, LRU_HEAD_DIM, LRU_HEAD_DIM)


def _w(lw, key, after):
    value = lw[key]
    return value(after) if callable(value) else value


def _layer_fwd(x, lw, tag):
    h1 = _norm_fwd(x, lw["g1"], tm=TM, name=f"norm1_fwd_{tag}")
    z = _mm_nt(h1, _w(lw, "w_in_t", h1), tm=TM, tn=512, tk=D_MODEL, out_dtype=F32, name=f"in_proj_{tag}")
    y_mix, hs = _mixer_fwd(z, _w(lw, "cw", z), lw["cb"], lw["wa_bd"], lw["wx_bd"], lw["ba"], lw["bx"], lw["lam"],
                           _w(lw, "scw", z), tile=MIX_TILE, name=f"mixer_fwd_{tag}")
    x2 = _mm_nn(y_mix, _w(lw, "w_out", y_mix), tm=TM, tn=D_MODEL, tk=D_MIX, out_dtype=F32, name=f"out_proj_{tag}",
                residual=x)
    h2 = _norm_fwd(x2, lw["g2"], tm=TM, name=f"norm2_fwd_{tag}")
    p = _mm_up(h2, _w(lw, "w_up_b", h2), tm=TM, name=f"up_proj_{tag}")
    act = _ffn_fwd(p, _w(lw, "fcw", p), tile=FFN_TILE, tc=FFN_TC, name=f"ffn_fwd_{tag}")
    x3 = _mm_nn(act, _w(lw, "w_down", act), tm=TM, tn=D_MODEL, tk=1024, out_dtype=F32, name=f"down_proj_{tag}",
                residual=x2)
    saved = dict(x=x, h1=h1, z=z, y_mix=y_mix, hs=hs, x2=x2, h2=h2, p=p, act=act)
    return x3, saved


def _layer_bwd(dx3, lw, sv, tag, put):
    w_in_t, w_out, w_up_b, w_down = (_w(lw, k, dx3) for k in ("w_in_t", "w_out", "w_up_b", "w_down"))
    cw, scw, fcw = (_w(lw, k, dx3) for k in ("cw", "scw", "fcw"))
    dact = _mm_nt(dx3, w_down, tm=TM, tn=512, tk=D_MODEL, out_dtype=F32, name=f"down_bwd_x_{tag}")
    g_down = _mm_tn(sv["act"], dx3, tm=512, tn=D_MODEL, tk=512, out_dtype=BF16, name=f"down_bwd_w_{tag}")
    dp, dfcw = _ffn_bwd(sv["p"], dact, fcw, tile=FFN_TILE, tc=FFN_TC, name=f"ffn_bwd_{tag}",
                        token=put("w_down", g_down))
    dh2 = _mm_up_bwd_x(dp, w_up_b, tm=TM, name=f"up_bwd_x_{tag}")
    g_up = _mm_up_bwd_w(sv["h2"], dp, tm=512, tk=512, name=f"up_bwd_w_{tag}")
    dx2, dg2 = _norm_bwd(dh2, sv["x2"], lw["g2"], dx3, tm=TM, name=f"norm2_bwd_{tag}", token=put("w_up_b", g_up))
    dy = _mm_nt(dx2, w_out, tm=TM, tn=512, tk=D_MODEL, out_dtype=F32, name=f"out_bwd_x_{tag}")
    g_out = _mm_tn(sv["y_mix"], dx2, tm=512, tn=D_MODEL, tk=512, out_dtype=BF16, name=f"out_bwd_w_{tag}")
    out_token = put("w_out", g_out)
    dz, dcw, dvec, dwa, dwx, dscw = _mixer_bwd(
        sv["z"], sv["hs"], dy, cw, lw["cb"], lw["wa_bd"], lw["wx_bd"], lw["ba"], lw["bx"], lw["lam"],
        scw, tile=MIX_TILE, name=f"mixer_bwd_{tag}", token=out_token)
    dh1 = _mm_nn(dz, w_in_t, tm=TM, tn=D_MODEL, tk=512, out_dtype=F32, name=f"in_bwd_x_{tag}")
    g_in_t = _mm_tn(dz, sv["h1"], tm=512, tn=D_MODEL, tk=512, out_dtype=BF16, name=f"in_bwd_w_{tag}")
    dx, dg1 = _norm_bwd(dh1, sv["x"], lw["g1"], dx2, tm=TM, name=f"norm1_bwd_{tag}", token=put("w_in_t", g_in_t))
    small = dict(norm1_g=dg1[0], lru_conv_w=dcw[0:4], lru_conv_b=dvec[0], lru_wa=_head_blocks(dwa),
                 lru_ba=dvec[1], lru_wx=_head_blocks(dwx), lru_bx=dvec[2], lru_lambda=dvec[3],
                 sc_conv_w=dscw[0:3], norm2_g=dg2[0], ffn_conv_w=dfcw[:, 0:3, :])
    return dx, small


SMALL_ORDER = ("norm1_g", "lru_conv_w", "lru_conv_b", "lru_wa", "lru_ba", "lru_wx", "lru_bx", "lru_lambda",
               "sc_conv_w", "norm2_g", "ffn_conv_w")


def _local_step(x, tgt, layers, final_g, put):
    saved = []
    h = x
    for l in range(DEPTH):
        h, sv = _layer_fwd(h, layers[l], f"l{l}")
        saved.append(sv)
    loss_blk, dx, dgf = _loss_head(h, final_g, tgt, tm=TM, name="loss_head")
    smalls = [None] * DEPTH
    for l in reversed(range(DEPTH)):
        dx, smalls[l] = _layer_bwd(dx, layers[l], saved[l], f"l{l}", functools.partial(put, l))
    return loss_blk[0, 0], dx, smalls, dgf[0]


def kernel(x, norm1_g, w_in, lru_conv_w, lru_conv_b, lru_wa, lru_ba, lru_wx, lru_bx, lru_lambda, sc_conv_w, w_out, norm2_g, w_up, ffn_conv_w, w_down, final_g, loss_target, m_norm1_g, m_w_in, m_lru_conv_w, m_lru_conv_b, m_lru_wa, m_lru_ba, m_lru_wx, m_lru_bx, m_lru_lambda, m_sc_conv_w, m_w_out, m_norm2_g, m_w_up, m_ffn_conv_w, m_w_down, m_final_g, v_norm1_g, v_w_in, v_lru_conv_w, v_lru_conv_b, v_lru_wa, v_lru_ba, v_lru_wx, v_lru_bx, v_lru_lambda, v_sc_conv_w, v_w_out, v_norm2_g, v_w_up, v_ffn_conv_w, v_w_down, v_final_g):
    names = ["norm1_g", "w_in", "lru_conv_w", "lru_conv_b", "lru_wa", "lru_ba", "lru_wx", "lru_bx", "lru_lambda",
             "sc_conv_w", "w_out", "norm2_g", "w_up", "ffn_conv_w", "w_down", "final_g"]
    w = dict(zip(names, [norm1_g, w_in, lru_conv_w, lru_conv_b, lru_wa, lru_ba, lru_wx, lru_bx, lru_lambda,
                         sc_conv_w, w_out, norm2_g, w_up, ffn_conv_w, w_down, final_g]))
    m = dict(zip(names, [m_norm1_g, m_w_in, m_lru_conv_w, m_lru_conv_b, m_lru_wa, m_lru_ba, m_lru_wx, m_lru_bx,
                         m_lru_lambda, m_sc_conv_w, m_w_out, m_norm2_g, m_w_up, m_ffn_conv_w, m_w_down, m_final_g]))
    v = dict(zip(names, [v_norm1_g, v_w_in, v_lru_conv_w, v_lru_conv_b, v_lru_wa, v_lru_ba, v_lru_wx, v_lru_bx,
                         v_lru_lambda, v_sc_conv_w, v_w_out, v_norm2_g, v_w_up, v_ffn_conv_w, v_w_down, v_final_g]))
    my_id = 4 * lax.axis_index("x") + 2 * lax.axis_index("y") + lax.axis_index("c")

    taps = jnp.zeros((DEPTH, 16, 768), F32)
    taps = taps.at[:, 0:4, 0:128].set(lru_conv_w).at[:, 4:7, 0:64].set(sc_conv_w).at[:, 8:11, :].set(ffn_conv_w)
    shards = {}
    for l in range(DEPTH):
        shards[f"w_in_t{l}"] = jnp.swapaxes(w_in[l], 0, 1).astype(BF16)
        if l == 0:
            shards["taps"] = taps.reshape(DEPTH * 16, 768)
        shards[f"w_out{l}"] = w_out[l].astype(BF16)
        shards[f"w_up_b{l}"] = w_up[l].astype(BF16)
        shards[f"w_down{l}"] = w_down[l].astype(BF16)
    ids = iter(range(16))
    got = {}
    chain = [None]
    for group in (("w_in_t0", "taps"), ("w_out0", "w_up_b0", "w_down0"),
                  ("w_in_t1", "w_out1", "w_up_b1", "w_down1")):
        lands, chain[0] = _sequencer_copies([shards[k] for k in group], gather=True, name=f"gather_{group[0]}",
                                            collective_id=next(ids), after=None)
        got.update(zip(group, lands))

    def fetch(key, after):
        return got[key]

    def tap_rows(l, lo, hi, width, after):
        tl = fetch("taps", after).reshape(N_DEV, DEPTH, 16, 768)[:, l, lo:hi, 0:width]
        return jnp.transpose(tl, (1, 0, 2)).reshape(hi - lo, N_DEV * width)

    layers = []
    for l in range(DEPTH):
        layers.append(dict(
            g1=norm1_g[l], g2=norm2_g[l], cb=lru_conv_b[l], ba=lru_ba[l], bx=lru_bx[l], lam=lru_lambda[l],
            wa_bd=_block_diag(lru_wa[l]), wx_bd=_block_diag(lru_wx[l]),
            cw=functools.partial(tap_rows, l, 0, 4, 128), scw=functools.partial(tap_rows, l, 4, 7, 64),
            fcw=lambda after, l=l: tap_rows(l, 8, 11, 768, after).reshape(3, 2, D_FF).transpose(1, 0, 2),
            w_in_t=lambda after, l=l: fetch(f"w_in_t{l}", after).reshape(D_IN, D_MODEL),
            w_out=lambda after, l=l: fetch(f"w_out{l}", after).reshape(D_MIX, D_MODEL),
            w_up_b=lambda after, l=l: fetch(f"w_up_b{l}", after),
            w_down=lambda after, l=l: fetch(f"w_down{l}", after).reshape(D_FF, D_MODEL)))

    scatter_handles = {}

    def put(l, key, grad):
        blocks = grad if grad.ndim == 3 else grad.reshape(N_DEV, grad.shape[0] // N_DEV, grad.shape[1])
        (scatter_handles[(l, key)],), chain[0] = _sequencer_copies(
            [blocks], gather=False, name=f"scatter_{key}{l}", collective_id=next(ids), after=chain[0])
        return blocks

    loss_local, dx, smalls, dgf = _local_step(x[0], loss_target[0], layers, final_g, put)
    loss = lax.psum(loss_local, ("x", "y", "c"))

    parts = []
    for l in range(DEPTH):
        for key in ("w_in_t", "w_out", "w_up_b", "w_down"):
            parts.append(scatter_handles[(l, key)])

    flat = [smalls[l][k].reshape(-1) for l in range(DEPTH) for k in SMALL_ORDER] + [dgf.reshape(-1)]
    sizes = [f.shape[0] for f in flat]
    total = sum(sizes)
    rows = -(-total // (N_DEV * 128 * 8)) * 8
    flat.append(jnp.zeros((N_DEV * rows * 128 - total,), F32))
    small_sum = _all_reduce_small(jnp.concatenate(flat).reshape(N_DEV, rows, 128), name="reduce_small").reshape(-1)
    small_g, off = [], 0
    for sz in sizes:
        small_g.append(small_sum[off:off + sz])
        off += sz
    gs = {}
    for l in range(DEPTH):
        for i, k in enumerate(SMALL_ORDER):
            gs.setdefault(k, []).append(small_g[l * len(SMALL_ORDER) + i])
    g_final = small_g[-1]

    grads = {}
    per_layer = {k: [] for k in ("w_in", "w_out", "w_up", "w_down")}
    for l in range(DEPTH):
        p_in, p_out, p_up, p_down = parts[4 * l:4 * l + 4]
        per_layer["w_in"].append(jnp.swapaxes(_sum_parts(p_in, name=f"sum_w_in_l{l}"), 0, 1))
        per_layer["w_out"].append(_sum_parts(p_out, name=f"sum_w_out_l{l}"))
        per_layer["w_up"].append(_sum_parts(p_up, name=f"sum_w_up_l{l}"))
        per_layer["w_down"].append(_sum_parts(p_down, name=f"sum_w_down_l{l}"))
    for k, lst in per_layer.items():
        grads[k] = jnp.stack(lst)
    for k in ("norm1_g", "lru_conv_b", "lru_ba", "lru_bx", "lru_lambda", "norm2_g"):
        grads[k] = jnp.stack(gs[k]).reshape(DEPTH, -1)
    for k in ("lru_wa", "lru_wx"):
        grads[k] = jnp.stack(gs[k]).reshape(DEPTH, LRU_HEADS, LRU_HEAD_DIM, LRU_HEAD_DIM)
    grads["final_g"] = g_final
    cw_full = jnp.stack(gs["lru_conv_w"]).reshape(DEPTH, 4, N_DEV, 128)
    grads["lru_conv_w"] = lax.dynamic_index_in_dim(cw_full, my_id, axis=2, keepdims=False)
    scw_full = jnp.stack(gs["sc_conv_w"]).reshape(DEPTH, 3, N_DEV, 64)
    grads["sc_conv_w"] = lax.dynamic_index_in_dim(scw_full, my_id, axis=2, keepdims=False)
    fcw_full = jnp.stack(gs["ffn_conv_w"]).reshape(DEPTH, 2, 3, D_FF).transpose(0, 2, 1, 3).reshape(DEPTH, 3, N_DEV, 768)
    grads["ffn_conv_w"] = lax.dynamic_index_in_dim(fcw_full, my_id, axis=2, keepdims=False)

    deltas, new_m, new_v = {}, {}, {}
    for k in names:
        shape = w[k].shape
        cols = shape[-1]
        as2d = lambda a: a.reshape(-1, cols)
        d, nm, nv = _adamw(as2d(w[k]), as2d(grads[k]), as2d(m[k]), as2d(v[k]), name=f"adamw_{k}")
        deltas[k], new_m[k], new_v[k] = d.reshape(shape), nm.reshape(shape), nv.reshape(shape)

    return (loss, dx[None], *[grads[k] for k in names], *[deltas[k] for k in names],
            *[new_m[k] for k in names], *[new_v[k] for k in names])
```

```python
import functools
import math

import jax
import jax.numpy as jnp
from jax import lax
from jax.experimental import pallas as pl
from jax.experimental.pallas import tpu as pltpu
from jax.experimental.pallas import tpu_sc as plsc

F32 = jnp.float32
BF16 = jnp.bfloat16

N_DEV = 8
DEPTH = 2
D_MODEL = 1024
D_LRU = 1024
D_SC = 512
D_MIX = D_LRU + D_SC
D_IN = 2 * D_LRU + 3 * D_SC
D_FF = 3072
LRU_HEADS = 16
LRU_HEAD_DIM = 64
LRU_GROUP = 256
N_GROUPS = D_LRU // LRU_GROUP
HEADS_PER_GROUP = LRU_GROUP // LRU_HEAD_DIM
RG_C = 8.0
EPS = 1e-6
HALO = 8

ADAM_LR = 0.001
ADAM_B1 = 0.9
ADAM_B2 = 0.999
ADAM_EPS = 1e-08
ADAM_WD = 0.01
ADAM_STEP = 10

GELU_C = math.sqrt(2.0 / math.pi)
GELU_A = 0.044715

VMEM_LIMIT = 56 * 1024 * 1024
MESH = pl.DeviceIdType.MESH


def _params(*sem):
    return pltpu.CompilerParams(dimension_semantics=tuple(sem) if sem else None,
                                vmem_limit_bytes=VMEM_LIMIT)


def _gelu_parts(x):
    x2 = x * x
    t = jnp.tanh(GELU_C * (x + GELU_A * x * x2))
    half = 0.5 * (1.0 + t)
    g = x * half
    dg = half + 0.5 * x * (1.0 - t * t) * (GELU_C * (1.0 + 3.0 * GELU_A * x2))
    return g, dg


def _gelu(x):
    t = jnp.tanh(GELU_C * (x + GELU_A * x * x * x))
    return 0.5 * x * (1.0 + t)


def _sigmoid(x):
    return 1.0 / (1.0 + jnp.exp(-x))


def _softplus(x):
    e = jnp.exp(-jnp.abs(x))
    u = 1.0 + e
    log1p_e = jnp.where(u == 1.0, e, jnp.log(u) * (e / (u - 1.0)))
    return jnp.maximum(x, 0.0) + log1p_e


def _rms(x):
    ms = jnp.mean(x * x, axis=-1, keepdims=True)
    return lax.rsqrt(ms + EPS)


def _dot(a, b, dims):
    return lax.dot_general(a, b, (dims, ((), ())), preferred_element_type=F32)


NN = ((1,), (0,))
NT = ((1,), (1,))
TN = ((0,), (0,))


def _matmul(a, b, *, dims, grid, a_spec, b_spec, o_spec, out_shape, acc_shape, name,
            residual=None, r_spec=None):
    nk = grid[2]

    def body(*refs):
        a_ref, b_ref = refs[0], refs[1]
        r_ref = refs[2] if residual is not None else None
        o_ref = refs[3] if residual is not None else refs[2]
        prod = _dot(a_ref[...].astype(BF16), b_ref[...].astype(BF16), dims)

        def finish(total):
            if r_ref is not None:
                total = total + r_ref[...]
            o_ref[...] = total.astype(o_ref.dtype)

        if nk == 1:
            finish(prod)
            return
        acc_ref = refs[-1]
        k = pl.program_id(2)

        @pl.when(k == 0)
        def _():
            acc_ref[...] = prod

        @pl.when(jnp.logical_and(k > 0, k < nk - 1))
        def _():
            acc_ref[...] += prod

        @pl.when(k == nk - 1)
        def _():
            finish(acc_ref[...] + prod)

    in_specs = [a_spec, b_spec]
    args = [a, b]
    if residual is not None:
        in_specs.append(r_spec)
        args.append(residual)
    return pl.pallas_call(
        body, name=name, grid=grid, in_specs=in_specs, out_specs=o_spec, out_shape=out_shape,
        scratch_shapes=[pltpu.VMEM(acc_shape, F32)] if nk > 1 else [],
        compiler_params=_params("parallel", "parallel", "arbitrary"),
    )(*args)


def _mm_nn(a, b, *, tm, tn, tk, out_dtype, name, residual=None):
    m, kd = a.shape
    n = b.shape[1]
    return _matmul(
        a, b, dims=NN, grid=(m // tm, n // tn, kd // tk),
        a_spec=pl.BlockSpec((tm, tk), lambda i, j, k: (i, k)),
        b_spec=pl.BlockSpec((tk, tn), lambda i, j, k: (k, j)),
        o_spec=pl.BlockSpec((tm, tn), lambda i, j, k: (i, j)),
        out_shape=jax.ShapeDtypeStruct((m, n), out_dtype), acc_shape=(tm, tn), name=name,
        residual=residual, r_spec=pl.BlockSpec((tm, tn), lambda i, j, k: (i, j)))


def _mm_nt(a, b, *, tm, tn, tk, out_dtype, name):
    m, kd = a.shape
    n = b.shape[0]
    return _matmul(
        a, b, dims=NT, grid=(m // tm, n // tn, kd // tk),
        a_spec=pl.BlockSpec((tm, tk), lambda i, j, k: (i, k)),
        b_spec=pl.BlockSpec((tn, tk), lambda i, j, k: (j, k)),
        o_spec=pl.BlockSpec((tm, tn), lambda i, j, k: (i, j)),
        out_shape=jax.ShapeDtypeStruct((m, n), out_dtype), acc_shape=(tm, tn), name=name)


def _mm_tn(a, b, *, tm, tn, tk, out_dtype, name):
    kd, m = a.shape
    n = b.shape[1]
    return _matmul(
        a, b, dims=TN, grid=(m // tm, n // tn, kd // tk),
        a_spec=pl.BlockSpec((tk, tm), lambda i, j, k: (k, i)),
        b_spec=pl.BlockSpec((tk, tn), lambda i, j, k: (k, j)),
        o_spec=pl.BlockSpec((tm, tn), lambda i, j, k: (i, j)),
        out_shape=jax.ShapeDtypeStruct((m, n), out_dtype), acc_shape=(tm, tn), name=name)


def _mm_up(h2, w_up_b, *, tm, name):
    s = h2.shape[0]
    nb = w_up_b.shape[2]
    per_half = D_FF // nb
    return _matmul(
        h2, w_up_b, dims=NN, grid=(s // tm, N_DEV, 1),
        a_spec=pl.BlockSpec((tm, D_MODEL), lambda i, j, k: (i, 0)),
        b_spec=pl.BlockSpec((None, D_MODEL, nb), lambda i, j, k: (j, 0, 0)),
        o_spec=pl.BlockSpec((None, tm, nb), lambda i, j, k: (j // per_half, i, j % per_half)),
        out_shape=jax.ShapeDtypeStruct((2, s, D_FF), F32), acc_shape=(tm, nb), name=name)


def _mm_up_bwd_x(dp, w_up_b, *, tm, name):
    s = dp.shape[1]
    nb = w_up_b.shape[2]
    per_half = D_FF // nb
    return _matmul(
        dp, w_up_b, dims=NT, grid=(s // tm, 1, N_DEV),
        a_spec=pl.BlockSpec((None, tm, nb), lambda i, j, k: (k // per_half, i, k % per_half)),
        b_spec=pl.BlockSpec((None, D_MODEL, nb), lambda i, j, k: (k, 0, 0)),
        o_spec=pl.BlockSpec((tm, D_MODEL), lambda i, j, k: (i, 0)),
        out_shape=jax.ShapeDtypeStruct((s, D_MODEL), F32), acc_shape=(tm, D_MODEL), name=name)


def _mm_up_bwd_w(h2, dp, *, tm, tk, name):
    s = h2.shape[0]
    nb = D_FF * 2 // N_DEV
    per_half = D_FF // nb
    return _matmul(
        h2, dp, dims=TN, grid=(D_MODEL // tm, N_DEV, s // tk),
        a_spec=pl.BlockSpec((tk, tm), lambda i, j, k: (k, i)),
        b_spec=pl.BlockSpec((None, tk, nb), lambda i, j, k: (j // per_half, k, j % per_half)),
        o_spec=pl.BlockSpec((None, tm, nb), lambda i, j, k: (j, i, 0)),
        out_shape=jax.ShapeDtypeStruct((N_DEV, D_MODEL, nb), BF16), acc_shape=(tm, nb), name=name)


def _norm_fwd(x, g, *, tm, name):
    s, d = x.shape

    def body(x_ref, g_ref, h_ref):
        xv = x_ref[...]
        h_ref[...] = (xv * _rms(xv) * g_ref[...]).astype(BF16)

    return pl.pallas_call(
        body, name=name, grid=(s // tm,),
        in_specs=[pl.BlockSpec((tm, d), lambda i: (i, 0)), pl.BlockSpec((1, d), lambda i: (0, 0))],
        out_specs=pl.BlockSpec((tm, d), lambda i: (i, 0)),
        out_shape=jax.ShapeDtypeStruct((s, d), BF16),
        compiler_params=_params("parallel"),
    )(x, g.reshape(1, d))


def _behind(token):
    return jnp.zeros((8, 128), F32) if token is None else token


def _norm_bwd(dh, x, g, dres, *, tm, name, token=None):
    s, d = x.shape

    def body(dh_ref, x_ref, g_ref, dres_ref, token_ref, dx_ref, dxb_ref, dg_ref):
        @pl.when(pl.program_id(0) == 0)
        def _():
            dg_ref[...] = jnp.zeros_like(dg_ref)

        xv = x_ref[...]
        rstd = _rms(xv)
        n = xv * rstd
        dhv = dh_ref[...]
        dn = dhv * g_ref[...]
        dx = rstd * (dn - n * jnp.mean(dn * n, axis=-1, keepdims=True))
        dx = dres_ref[...] + dx
        dx_ref[...] = dx
        dxb_ref[...] = dx.astype(BF16)
        dg_ref[0:1, :] += jnp.sum(dhv * n, axis=0, keepdims=True)

    return pl.pallas_call(
        body, name=name, grid=(s // tm,),
        in_specs=[pl.BlockSpec((tm, d), lambda i: (i, 0)), pl.BlockSpec((tm, d), lambda i: (i, 0)),
                  pl.BlockSpec((1, d), lambda i: (0, 0)), pl.BlockSpec((tm, d), lambda i: (i, 0)),
                  pl.BlockSpec(memory_space=pl.ANY)],
        out_specs=[pl.BlockSpec((tm, d), lambda i: (i, 0)), pl.BlockSpec((tm, d), lambda i: (i, 0)),
                   pl.BlockSpec((8, d), lambda i: (0, 0))],
        out_shape=[jax.ShapeDtypeStruct((s, d), F32), jax.ShapeDtypeStruct((s, d), BF16),
                   jax.ShapeDtypeStruct((8, d), F32)],
        compiler_params=_params("arbitrary"),
    )(dh, x, g.reshape(1, d), dres, _behind(token))


def _loss_head(x, g, tgt, *, tm, name):
    s, d = x.shape

    def body(x_ref, g_ref, t_ref, loss_ref, dx_ref, dxb_ref, dg_ref):
        @pl.when(pl.program_id(0) == 0)
        def _():
            dg_ref[...] = jnp.zeros_like(dg_ref)
            loss_ref[...] = jnp.zeros_like(loss_ref)

        xv = x_ref[...]
        gv = g_ref[...]
        rstd = _rms(xv)
        n = xv * rstd
        e = n * gv - t_ref[...]
        part = 0.5 * jnp.sum(jnp.mean(e * e, axis=-1, keepdims=True), axis=0, keepdims=True)
        loss_ref[...] += jnp.broadcast_to(part, loss_ref.shape)
        dy = e * (1.0 / d)
        dn = dy * gv
        dx = rstd * (dn - n * jnp.mean(dn * n, axis=-1, keepdims=True))
        dx_ref[...] = dx
        dxb_ref[...] = dx.astype(BF16)
        dg_ref[0:1, :] += jnp.sum(dy * n, axis=0, keepdims=True)

    return pl.pallas_call(
        body, name=name, grid=(s // tm,),
        in_specs=[pl.BlockSpec((tm, d), lambda i: (i, 0)), pl.BlockSpec((1, d), lambda i: (0, 0)),
                  pl.BlockSpec((tm, d), lambda i: (i, 0))],
        out_specs=[pl.BlockSpec((8, 128), lambda i: (0, 0)), pl.BlockSpec((tm, d), lambda i: (i, 0)),
                   pl.BlockSpec((tm, d), lambda i: (i, 0)), pl.BlockSpec((8, d), lambda i: (0, 0))],
        out_shape=[jax.ShapeDtypeStruct((8, 128), F32), jax.ShapeDtypeStruct((s, d), F32),
                   jax.ShapeDtypeStruct((s, d), BF16), jax.ShapeDtypeStruct((8, d), F32)],
        compiler_params=_params("arbitrary"),
    )(x, g.reshape(1, d), tgt)


def _scan_rows(a_ref, b_ref, h_ref, carry, *, rows, reverse):
    width = a_ref.shape[1]
    n_chunks = rows // 8
    row = lax.broadcasted_iota(jnp.int32, (8, width), 0)

    def step(ci, carry):
        chunk = (n_chunks - 1 - ci) if reverse else ci
        off = pl.multiple_of(chunk * 8, 8)
        av = a_ref[pl.ds(off, 8), :]
        bv = b_ref[pl.ds(off, 8), :]
        for sh in (1, 2, 4):
            if reverse:
                a_sh = pltpu.roll(av, 8 - sh, 0)
                b_sh = pltpu.roll(bv, 8 - sh, 0)
                m = row < 8 - sh
            else:
                a_sh = pltpu.roll(av, sh, 0)
                b_sh = pltpu.roll(bv, sh, 0)
                m = row >= sh
            bv = jnp.where(m, av * b_sh + bv, bv)
            av = jnp.where(m, av * a_sh, av)
        h = av * carry + bv
        h_ref[pl.ds(off, 8), :] = h
        return h[0:1, :] if reverse else h[7:8, :]

    return lax.fori_loop(0, n_chunks, step, carry)


def _lru_gates(lx, wa_ref, wx_ref, ba, bx, sp):
    lxb = lx.astype(BF16)
    pre_r = jnp.concatenate(
        [_dot(lxb[:, g * LRU_GROUP:(g + 1) * LRU_GROUP], wa_ref[g], NN) for g in range(N_GROUPS)], axis=1)
    pre_i = jnp.concatenate(
        [_dot(lxb[:, g * LRU_GROUP:(g + 1) * LRU_GROUP], wx_ref[g], NN) for g in range(N_GROUPS)], axis=1)
    r = _sigmoid(pre_r + ba)
    ig = _sigmoid(pre_i + bx)
    log_a = (-RG_C * r) * sp
    a = jnp.exp(log_a)
    mult = jnp.sqrt(-jnp.tanh(log_a) * (a * a + 1.0))
    return lxb, r, ig, a, mult


def _mixer_fwd(z, cw, cb, wa_bd, wx_bd, ba, bx, lam, scw, *, tile, name):
    s = z.shape[0]
    n_tiles = s // tile

    def body(z_ref, cw_ref, cb_ref, wa_ref, wx_ref, ba_ref, bx_ref, lam_ref, scw_ref,
             y_ref, hs_ref, ext_lx, ext_q, a_s, b_s, h_car):
        i = pl.program_id(0)

        @pl.when(i == 0)
        def _():
            ext_lx[0:HALO, :] = jnp.zeros((HALO, D_LRU), F32)
            ext_q[0:HALO, :] = jnp.zeros((HALO, D_SC), F32)
            h_car[...] = jnp.zeros_like(h_car)

        ext_lx[HALO:HALO + tile, :] = z_ref[:, 0:D_LRU]
        ext_q[HALO:HALO + tile, :] = z_ref[:, 2 * D_LRU + D_SC:2 * D_LRU + 2 * D_SC] * z_ref[:, 2 * D_LRU + 2 * D_SC:D_IN]
        lx = cb_ref[...] + cw_ref[0:1, :] * ext_lx[pl.ds(HALO - 3, tile), :]
        for k in range(1, 4):
            lx = lx + cw_ref[k:k + 1, :] * ext_lx[pl.ds(HALO - 3 + k, tile), :]
        cq = scw_ref[0:1, :] * ext_q[pl.ds(HALO - 2, tile), :]
        for k in range(1, 3):
            cq = cq + scw_ref[k:k + 1, :] * ext_q[pl.ds(HALO - 2 + k, tile), :]
        ext_lx[0:HALO, :] = ext_lx[tile:tile + HALO, :]
        ext_q[0:HALO, :] = ext_q[tile:tile + HALO, :]

        sp = _softplus(-lam_ref[...])
        _, _, ig, a, mult = _lru_gates(lx, wa_ref, wx_ref, ba_ref[...], bx_ref[...], sp)
        a_s[...] = a
        b_s[...] = mult * (ig * lx)
        h_car[0:1, :] = _scan_rows(a_s, b_s, hs_ref, h_car[0:1, :], rows=tile, reverse=False)

        y_ref[:, 0:D_LRU] = (hs_ref[...] * _gelu(z_ref[:, D_LRU:2 * D_LRU])).astype(BF16)
        y_ref[:, D_LRU:D_MIX] = (z_ref[:, 2 * D_LRU:2 * D_LRU + D_SC] * cq).astype(BF16)

    full = lambda shape: pl.BlockSpec(shape, lambda i: (0,) * len(shape))
    return pl.pallas_call(
        body, name=name, grid=(n_tiles,),
        in_specs=[pl.BlockSpec((tile, D_IN), lambda i: (i, 0)),
                  full((4, D_LRU)), full((1, D_LRU)),
                  full((N_GROUPS, LRU_GROUP, LRU_GROUP)), full((N_GROUPS, LRU_GROUP, LRU_GROUP)),
                  full((1, D_LRU)), full((1, D_LRU)), full((1, D_LRU)), full((3, D_SC))],
        out_specs=[pl.BlockSpec((tile, D_MIX), lambda i: (i, 0)), pl.BlockSpec((tile, D_LRU), lambda i: (i, 0))],
        out_shape=[jax.ShapeDtypeStruct((s, D_MIX), BF16), jax.ShapeDtypeStruct((s, D_LRU), F32)],
        scratch_shapes=[pltpu.VMEM((tile + HALO, D_LRU), F32), pltpu.VMEM((tile + HALO, D_SC), F32),
                        pltpu.VMEM((tile, D_LRU), F32), pltpu.VMEM((tile, D_LRU), F32),
                        pltpu.VMEM((8, D_LRU), F32)],
        compiler_params=_params("arbitrary"),
    )(z, cw, cb.reshape(1, -1), wa_bd, wx_bd, ba.reshape(1, -1), bx.reshape(1, -1), lam.reshape(1, -1), scw)


def _mixer_bwd(z, hs, dy, cw, cb, wa_bd, wx_bd, ba, bx, lam, scw, *, tile, name, token=None):
    s = z.shape[0]
    n_tiles = s // tile
    per8 = tile // 8

    def body(z_ref, zp_ref, hs_ref, hsp_ref, dy_ref, cw_ref, cb_ref, wa_ref, wx_ref, ba_ref, bx_ref, lam_ref, scw_ref,
             token_ref, dz_ref, dcw_ref, dvec_ref, dwa_ref, dwx_ref, dscw_ref,
             ext_lx, ext_q, ext_h, ext_a, ext_dlx, ext_dcq, a_s, b_s, lam_s, l_car):
        i = pl.program_id(0)
        first_tile = i == n_tiles - 1

        @pl.when(i == 0)
        def _():
            for ref in (dcw_ref, dvec_ref, dwa_ref, dwx_ref, dscw_ref, l_car):
                ref[...] = jnp.zeros_like(ref)
            ext_a[tile:tile + HALO, :] = jnp.zeros((HALO, D_LRU), F32)
            ext_dlx[tile:tile + HALO, :] = jnp.zeros((HALO, D_LRU), F32)
            ext_dcq[tile:tile + HALO, :] = jnp.zeros((HALO, D_SC), F32)

        keep = jnp.where(first_tile, 0.0, 1.0)
        sb = z_ref[:, 2 * D_LRU:2 * D_LRU + D_SC]
        sc = z_ref[:, 2 * D_LRU + D_SC:2 * D_LRU + 2 * D_SC]
        sx = z_ref[:, 2 * D_LRU + 2 * D_SC:D_IN]
        ext_lx[0:HALO, :] = zp_ref[:, 0:D_LRU] * keep
        ext_lx[HALO:HALO + tile, :] = z_ref[:, 0:D_LRU]
        ext_q[0:HALO, :] = zp_ref[:, 2 * D_LRU + D_SC:2 * D_LRU + 2 * D_SC] * zp_ref[:, 2 * D_LRU + 2 * D_SC:D_IN] * keep
        ext_q[HALO:HALO + tile, :] = sc * sx
        ext_h[0:HALO, :] = hsp_ref[...] * keep
        ext_h[HALO:HALO + tile, :] = hs_ref[...]

        lx = cb_ref[...] + cw_ref[0:1, :] * ext_lx[pl.ds(HALO - 3, tile), :]
        for k in range(1, 4):
            lx = lx + cw_ref[k:k + 1, :] * ext_lx[pl.ds(HALO - 3 + k, tile), :]
        cq = scw_ref[0:1, :] * ext_q[pl.ds(HALO - 2, tile), :]
        for k in range(1, 3):
            cq = cq + scw_ref[k:k + 1, :] * ext_q[pl.ds(HALO - 2 + k, tile), :]

        sp = _softplus(-lam_ref[...])
        lxb, r, ig, a, mult = _lru_gates(lx, wa_ref, wx_ref, ba_ref[...], bx_ref[...], sp)

        ge, dge = _gelu_parts(z_ref[:, D_LRU:2 * D_LRU])
        dy_lru = dy_ref[:, 0:D_LRU]
        dz_ref[:, D_LRU:2 * D_LRU] = (dy_lru * hs_ref[...] * dge).astype(BF16)

        ext_a[0:tile, :] = a
        a_s[...] = ext_a[pl.ds(1, tile), :]
        b_s[...] = dy_lru * ge
        l_car[0:1, :] = _scan_rows(a_s, b_s, lam_s, l_car[0:1, :], rows=tile, reverse=True)
        ext_a[tile:tile + HALO, :] = ext_a[0:HALO, :]
        lv = lam_s[...]

        da = lv * ext_h[pl.ds(HALO - 1, tile), :]
        d_mult = lv * ig * lx
        d_i = lv * mult * lx
        dlx = lv * mult * ig
        dlog_a = da * a - d_mult * (a * a) / mult
        d_r = dlog_a * (-RG_C * sp)
        dpre_r = d_r * r * (1.0 - r)
        dpre_i = d_i * ig * (1.0 - ig)
        dvec_ref[1:2, :] += jnp.sum(dpre_r, axis=0, keepdims=True)
        dvec_ref[2:3, :] += jnp.sum(dpre_i, axis=0, keepdims=True)
        dvec_ref[3:4, :] += jnp.sum(dlog_a * (-RG_C * r), axis=0, keepdims=True)
        dpr_b = dpre_r.astype(BF16)
        dpi_b = dpre_i.astype(BF16)
        back = []
        for g in range(N_GROUPS):
            cols = slice(g * LRU_GROUP, (g + 1) * LRU_GROUP)
            dwa_ref[g] += _dot(lxb[:, cols], dpr_b[:, cols], TN)
            dwx_ref[g] += _dot(lxb[:, cols], dpi_b[:, cols], TN)
            back.append(_dot(dpr_b[:, cols], wa_ref[g], NT) + _dot(dpi_b[:, cols], wx_ref[g], NT))
        dlx = dlx + jnp.concatenate(back, axis=1)
        dvec_ref[0:1, :] += jnp.sum(dlx, axis=0, keepdims=True)

        ext_dlx[0:tile, :] = dlx
        for k in range(4):
            dcw_ref[k:k + 1, :] += jnp.sum(dlx * ext_lx[pl.ds(HALO - 3 + k, tile), :], axis=0, keepdims=True)
        dlxp = cw_ref[3:4, :] * dlx
        for k in range(3):
            dlxp = dlxp + cw_ref[k:k + 1, :] * ext_dlx[pl.ds(3 - k, tile), :]
        dz_ref[:, 0:D_LRU] = dlxp.astype(BF16)
        ext_dlx[tile:tile + HALO, :] = ext_dlx[0:HALO, :]

        dy_sc = dy_ref[:, D_LRU:D_MIX]
        dz_ref[:, 2 * D_LRU:2 * D_LRU + D_SC] = (dy_sc * cq).astype(BF16)
        dcq = dy_sc * sb
        ext_dcq[0:tile, :] = dcq
        for k in range(3):
            dscw_ref[k:k + 1, :] += jnp.sum(dcq * ext_q[pl.ds(HALO - 2 + k, tile), :], axis=0, keepdims=True)
        dq = scw_ref[2:3, :] * dcq
        for k in range(2):
            dq = dq + scw_ref[k:k + 1, :] * ext_dcq[pl.ds(2 - k, tile), :]
        dz_ref[:, 2 * D_LRU + D_SC:2 * D_LRU + 2 * D_SC] = (dq * sx).astype(BF16)
        dz_ref[:, 2 * D_LRU + 2 * D_SC:D_IN] = (dq * sc).astype(BF16)
        ext_dcq[tile:tile + HALO, :] = ext_dcq[0:HALO, :]

        @pl.when(i == n_tiles - 1)
        def _():
            dvec_ref[3:4, :] = dvec_ref[3:4, :] * (-_sigmoid(-lam_ref[...]))

    rev = lambda i: n_tiles - 1 - i
    prev8 = lambda i: jnp.maximum(rev(i) * per8 - 1, 0)
    full = lambda shape: pl.BlockSpec(shape, lambda i: (0,) * len(shape))
    return pl.pallas_call(
        body, name=name, grid=(n_tiles,),
        in_specs=[pl.BlockSpec((tile, D_IN), lambda i: (rev(i), 0)),
                  pl.BlockSpec((HALO, D_IN), lambda i: (prev8(i), 0)),
                  pl.BlockSpec((tile, D_LRU), lambda i: (rev(i), 0)),
                  pl.BlockSpec((HALO, D_LRU), lambda i: (prev8(i), 0)),
                  pl.BlockSpec((tile, D_MIX), lambda i: (rev(i), 0)),
                  full((4, D_LRU)), full((1, D_LRU)),
                  full((N_GROUPS, LRU_GROUP, LRU_GROUP)), full((N_GROUPS, LRU_GROUP, LRU_GROUP)),
                  full((1, D_LRU)), full((1, D_LRU)), full((1, D_LRU)), full((3, D_SC)),
                  pl.BlockSpec(memory_space=pl.ANY)],
        out_specs=[pl.BlockSpec((tile, D_IN), lambda i: (rev(i), 0)),
                   full((8, D_LRU)), full((8, D_LRU)),
                   full((N_GROUPS, LRU_GROUP, LRU_GROUP)), full((N_GROUPS, LRU_GROUP, LRU_GROUP)),
                   full((8, D_SC))],
        out_shape=[jax.ShapeDtypeStruct((s, D_IN), BF16),
                   jax.ShapeDtypeStruct((8, D_LRU), F32), jax.ShapeDtypeStruct((8, D_LRU), F32),
                   jax.ShapeDtypeStruct((N_GROUPS, LRU_GROUP, LRU_GROUP), F32),
                   jax.ShapeDtypeStruct((N_GROUPS, LRU_GROUP, LRU_GROUP), F32),
                   jax.ShapeDtypeStruct((8, D_SC), F32)],
        scratch_shapes=[pltpu.VMEM((tile + HALO, D_LRU), F32), pltpu.VMEM((tile + HALO, D_SC), F32),
                        pltpu.VMEM((tile + HALO, D_LRU), F32), pltpu.VMEM((tile + HALO, D_LRU), F32),
                        pltpu.VMEM((tile + HALO, D_LRU), F32), pltpu.VMEM((tile + HALO, D_SC), F32),
                        pltpu.VMEM((tile, D_LRU), F32), pltpu.VMEM((tile, D_LRU), F32),
                        pltpu.VMEM((tile, D_LRU), F32), pltpu.VMEM((8, D_LRU), F32)],
        compiler_params=_params("arbitrary"),
    )(z, z, hs, hs, dy, cw, cb.reshape(1, -1), wa_bd, wx_bd, ba.reshape(1, -1), bx.reshape(1, -1),
      lam.reshape(1, -1), scw, _behind(token))


def _ffn_fwd(p, fcw, *, tile, tc, name):
    s = p.shape[1]
    per8 = tile // 8

    def body(p_ref, pp_ref, w_ref, act_ref, ext_p):
        i = pl.program_id(0)
        keep = jnp.where(i == 0, 0.0, 1.0)
        ext_p[:, 0:HALO, :] = pp_ref[...] * keep
        ext_p[:, HALO:HALO + tile, :] = p_ref[...]
        u = []
        for half in range(2):
            acc = w_ref[half, 0:1, :] * ext_p[half, pl.ds(HALO - 2, tile), :]
            for k in range(1, 3):
                acc = acc + w_ref[half, k:k + 1, :] * ext_p[half, pl.ds(HALO - 2 + k, tile), :]
            u.append(acc)
        act_ref[...] = (_gelu(u[0]) * u[1]).astype(BF16)

    return pl.pallas_call(
        body, name=name, grid=(s // tile, D_FF // tc),
        in_specs=[pl.BlockSpec((2, tile, tc), lambda i, j: (0, i, j)),
                  pl.BlockSpec((2, HALO, tc), lambda i, j: (0, jnp.maximum(i * per8 - 1, 0), j)),
                  pl.BlockSpec((2, 3, tc), lambda i, j: (0, 0, j))],
        out_specs=pl.BlockSpec((tile, tc), lambda i, j: (i, j)),
        out_shape=jax.ShapeDtypeStruct((s, D_FF), BF16),
        scratch_shapes=[pltpu.VMEM((2, tile + HALO, tc), F32)],
        compiler_params=_params("parallel", "parallel"),
    )(p, p, fcw)


def _ffn_bwd(p, dact, fcw, *, tile, tc, name, token=None):
    s = p.shape[1]
    n_tiles = s // tile
    per8 = tile // 8
    ext_rows = tile + HALO

    def body(p_ref, pp_ref, pn_ref, da_ref, dan_ref, w_ref, token_ref, dp_ref, dw_ref, ext_p, ext_du):
        i = pl.program_id(1)

        @pl.when(i == 0)
        def _():
            dw_ref[...] = jnp.zeros_like(dw_ref)

        keep_prev = jnp.where(i == 0, 0.0, 1.0)
        keep_next = jnp.where(i == n_tiles - 1, 0.0, 1.0)
        ext_p[:, 0:HALO, :] = pp_ref[...] * keep_prev
        ext_p[:, HALO:HALO + tile, :] = p_ref[...]
        ext_p[:, HALO + tile:2 * HALO + tile, :] = pn_ref[...] * keep_next
        u = []
        for half in range(2):
            acc = w_ref[half, 0:1, :] * ext_p[half, pl.ds(HALO - 2, ext_rows), :]
            for k in range(1, 3):
                acc = acc + w_ref[half, k:k + 1, :] * ext_p[half, pl.ds(HALO - 2 + k, ext_rows), :]
            u.append(acc)
        ge, dge = _gelu_parts(u[0])
        da = jnp.concatenate([da_ref[...], dan_ref[...] * keep_next], axis=0)
        ext_du[0, :, :] = da * u[1] * dge
        ext_du[1, :, :] = da * ge
        for half in range(2):
            du = ext_du[half, 0:tile, :]
            acc = w_ref[half, 2:3, :] * du
            for k in range(2):
                acc = acc + w_ref[half, k:k + 1, :] * ext_du[half, pl.ds(2 - k, tile), :]
            dp_ref[half, :, :] = acc.astype(BF16)
            for k in range(3):
                dw_ref[half, k:k + 1, :] += jnp.sum(du * ext_p[half, pl.ds(HALO - 2 + k, tile), :], axis=0, keepdims=True)

    return pl.pallas_call(
        body, name=name, grid=(D_FF // tc, n_tiles),
        in_specs=[pl.BlockSpec((2, tile, tc), lambda j, i: (0, i, j)),
                  pl.BlockSpec((2, HALO, tc), lambda j, i: (0, jnp.maximum(i * per8 - 1, 0), j)),
                  pl.BlockSpec((2, HALO, tc), lambda j, i: (0, jnp.minimum((i + 1) * per8, n_tiles * per8 - 1), j)),
                  pl.BlockSpec((tile, tc), lambda j, i: (i, j)),
                  pl.BlockSpec((HALO, tc), lambda j, i: (jnp.minimum((i + 1) * per8, n_tiles * per8 - 1), j)),
                  pl.BlockSpec((2, 3, tc), lambda j, i: (0, 0, j)), pl.BlockSpec(memory_space=pl.ANY)],
        out_specs=[pl.BlockSpec((2, tile, tc), lambda j, i: (0, i, j)),
                   pl.BlockSpec((2, 8, tc), lambda j, i: (0, 0, j))],
        out_shape=[jax.ShapeDtypeStruct((2, s, D_FF), BF16), jax.ShapeDtypeStruct((2, 8, D_FF), F32)],
        scratch_shapes=[pltpu.VMEM((2, tile + 2 * HALO, tc), F32), pltpu.VMEM((2, ext_rows, tc), F32)],
        compiler_params=_params("parallel", "arbitrary"),
    )(p, p, p, dact, dact, fcw, _behind(token))


def _adamw_math(w, g, m, v):
    m = ADAM_B1 * m + (1.0 - ADAM_B1) * g
    v = ADAM_B2 * v + (1.0 - ADAM_B2) * (g * g)
    m_hat = m / (1.0 - ADAM_B1 ** ADAM_STEP)
    v_hat = v / (1.0 - ADAM_B2 ** ADAM_STEP)
    delta = -ADAM_LR * (m_hat / (jnp.sqrt(v_hat) + ADAM_EPS) + ADAM_WD * w)
    return delta, m, v


def _adamw(w, g, m, v, *, name):
    rows, cols = w.shape
    tr = rows
    for cand in (512, 256, 128, 64, 32, 16, 8):
        if rows % cand == 0 and rows > cand:
            tr = cand
            break

    def body(w_ref, g_ref, m_ref, v_ref, d_ref, nm_ref, nv_ref):
        d, nm, nv = _adamw_math(w_ref[...], g_ref[...], m_ref[...], v_ref[...])
        d_ref[...] = d
        nm_ref[...] = nm
        nv_ref[...] = nv

    spec = pl.BlockSpec((tr, cols), lambda i: (i, 0))
    return pl.pallas_call(
        body, name=name, grid=(rows // tr,), in_specs=[spec] * 4, out_specs=[spec] * 3,
        out_shape=[jax.ShapeDtypeStruct((rows, cols), F32)] * 3,
        compiler_params=_params("parallel"),
    )(w, g, m, v)


def _sum_parts(parts, *, name):
    _, rows, cols = parts.shape
    tr = rows
    for cand in (256, 128, 64, 32, 16):
        if rows % cand == 0 and rows > cand:
            tr = cand
            break

    def body(p_ref, o_ref):
        acc = p_ref[0].astype(F32)
        for d in range(1, N_DEV):
            acc = acc + p_ref[d].astype(F32)
        o_ref[...] = acc

    return pl.pallas_call(
        body, name=name, grid=(rows // tr,),
        in_specs=[pl.BlockSpec((N_DEV, tr, cols), lambda i: (0, i, 0))],
        out_specs=pl.BlockSpec((tr, cols), lambda i: (i, 0)),
        out_shape=jax.ShapeDtypeStruct((rows, cols), F32),
        compiler_params=_params("parallel"),
    )(parts)


def _place():
    return lax.axis_index("x"), lax.axis_index("y"), lax.axis_index("c")


def _flip(v, bit):
    return 1 - v if bit else v


N_PEERS = N_DEV - 1


def _peer_copy(k, src_ref, land_ref, send_sem, recv_sem, gather):
    x, y, c = _place()
    my_id = 4 * x + 2 * y + c
    px, py, pc = _flip(x, k & 4), _flip(y, k & 2), _flip(c, k & 1)
    peer_id = 4 * px + 2 * py + pc
    return pltpu.make_async_remote_copy(
        src_ref=src_ref if gather else src_ref.at[peer_id], dst_ref=land_ref.at[my_id],
        send_sem=send_sem.at[k - 1], recv_sem=recv_sem.at[k - 1],
        device_id=(px, py, pc), device_id_type=MESH)


def _sequencer_copies(srcs, *, gather, name, collective_id, after):
    n = len(srcs)
    hbm = pltpu.MemorySpace.HBM
    src_refs = [jax.new_ref(s, memory_space=hbm) for s in srcs]
    land_refs = [jax.empty_ref(jax.ShapeDtypeStruct(((N_DEV,) + s.shape) if gather else s.shape, s.dtype),
                               memory_space=hbm) for s in srcs]
    token_in = jax.new_ref(jnp.zeros((8, 128), F32) if after is None else after, memory_space=hbm)
    token_out = jax.empty_ref(jax.ShapeDtypeStruct((8, 128), F32), memory_space=hbm)

    @pl.kernel(mesh=plsc.ScalarSubcoreMesh(axis_name="seq", num_cores=1), name=name,
               scratch_types=(pltpu.SemaphoreType.DMA((n, N_PEERS)), pltpu.SemaphoreType.DMA((n, N_PEERS)),
                              pltpu.SemaphoreType.DMA((n + 1,))),
               compiler_params=pltpu.CompilerParams(collective_id=collective_id))
    def launch(send_sems, recv_sems, local_sems):
        x, y, c = _place()
        my_id = 4 * x + 2 * y + c
        barrier = pltpu.get_barrier_semaphore()
        for k in range(1, N_DEV):
            peer = (_flip(x, k & 4), _flip(y, k & 2), _flip(c, k & 1))
            pl.semaphore_signal(barrier, inc=1, device_id=peer, device_id_type=MESH)
        pl.semaphore_wait(barrier, N_PEERS)
        own = [pltpu.make_async_copy(src_refs[t] if gather else src_refs[t].at[my_id], land_refs[t].at[my_id],
                                     local_sems.at[t]) for t in range(n)]
        for cp in own:
            cp.start()
        copies = [_peer_copy(k, src_refs[t], land_refs[t], send_sems.at[t], recv_sems.at[t], gather)
                  for t in range(n) for k in range(1, N_DEV)]
        for cp in copies:
            cp.start()
        for cp in own:
            cp.wait()
        for cp in copies:
            cp.wait()
        passed = pltpu.make_async_copy(token_in, token_out, local_sems.at[n])
        passed.start()
        passed.wait()

    launch()
    return [ref[...] for ref in land_refs], token_out[...]


def _all_reduce_small(buf, *, name):
    _, rows, lanes = buf.shape

    def body(in_ref, out_ref, parts, send_sems, recv_sems):
        x, y, c = _place()
        my_id = 4 * x + 2 * y + c
        peers = []
        for k in range(1, N_DEV):
            px, py, pc = _flip(x, k & 4), _flip(y, k & 2), _flip(c, k & 1)
            peers.append(((px, py, pc), 4 * px + 2 * py + pc))
        scatter = [pltpu.make_async_remote_copy(
            src_ref=in_ref.at[pid], dst_ref=parts.at[my_id],
            send_sem=send_sems.at[0, k], recv_sem=recv_sems.at[0, k],
            device_id=peer, device_id_type=MESH) for k, (peer, pid) in enumerate(peers)]
        for cp in scatter:
            cp.start()
        parts[my_id] = in_ref[my_id]
        for cp in scatter:
            cp.wait()
        total = parts[0]
        for d in range(1, N_DEV):
            total = total + parts[d]
        out_ref[my_id] = total
        gather = [pltpu.make_async_remote_copy(
            src_ref=out_ref.at[my_id], dst_ref=out_ref.at[my_id],
            send_sem=send_sems.at[1, k], recv_sem=recv_sems.at[1, k],
            device_id=peer, device_id_type=MESH) for k, (peer, pid) in enumerate(peers)]
        for cp in gather:
            cp.start()
        for k, (peer, pid) in enumerate(peers):
            pltpu.make_async_remote_copy(
                src_ref=out_ref.at[pid], dst_ref=out_ref.at[pid],
                send_sem=send_sems.at[1, k], recv_sem=recv_sems.at[1, k],
                device_id=peer, device_id_type=MESH).wait()

    vmem = pl.BlockSpec(memory_space=pltpu.VMEM)
    return pl.pallas_call(
        body, name=name, in_specs=[vmem], out_specs=vmem,
        out_shape=jax.ShapeDtypeStruct(buf.shape, F32),
        scratch_shapes=[pltpu.VMEM(buf.shape, F32),
                        pltpu.SemaphoreType.DMA((2, 7)), pltpu.SemaphoreType.DMA((2, 7))],
        compiler_params=pltpu.CompilerParams(vmem_limit_bytes=VMEM_LIMIT),
    )(buf)


TM = 512
TMM = 1024
TKW = 2048
MIX_TILE = 128
FFN_TILE = 256
FFN_TC = 512


def _block_diag(w):
    wg = w.reshape(N_GROUPS, HEADS_PER_GROUP, LRU_HEAD_DIM, LRU_HEAD_DIM)
    eye = jnp.eye(HEADS_PER_GROUP, dtype=w.dtype)
    bd = wg[:, :, :, None, :] * eye[None, :, None, :, None]
    return bd.reshape(N_GROUPS, LRU_GROUP, LRU_GROUP).astype(BF16)


def _head_blocks(bd):
    b5 = bd.reshape(N_GROUPS, HEADS_PER_GROUP, LRU_HEAD_DIM, HEADS_PER_GROUP, LRU_HEAD_DIM)
    blocks = [b5[:, h, :, h, :] for h in range(HEADS_PER_GROUP)]
    return jnp.stack(blocks, axis=1).reshape(LRU_HEADS, LRU_HEAD_DIM, LRU_HEAD_DIM)


def _w(lw, key, after):
    value = lw[key]
    return value(after) if callable(value) else value


def _layer_fwd(x, lw, tag):
    sv_rows = x.shape[0]
    h1 = _norm_fwd(x, lw["g1"], tm=TM, name=f"norm1_fwd_{tag}")
    z = _mm_nt(h1, _w(lw, "w_in_t", h1), tm=min(TMM, sv_rows), tn=896, tk=D_MODEL, out_dtype=F32, name=f"in_proj_{tag}")
    y_mix, hs = _mixer_fwd(z, _w(lw, "cw", z), lw["cb"], lw["wa_bd"], lw["wx_bd"], lw["ba"], lw["bx"], lw["lam"],
                           _w(lw, "scw", z), tile=MIX_TILE, name=f"mixer_fwd_{tag}")
    x2 = _mm_nn(y_mix, _w(lw, "w_out", y_mix), tm=min(TMM, sv_rows), tn=D_MODEL, tk=D_MIX, out_dtype=F32, name=f"out_proj_{tag}",
                residual=x)
    h2 = _norm_fwd(x2, lw["g2"], tm=TM, name=f"norm2_fwd_{tag}")
    p = _mm_up(h2, _w(lw, "w_up_b", h2), tm=min(TMM, sv_rows), name=f"up_proj_{tag}")
    act = _ffn_fwd(p, _w(lw, "fcw", p), tile=FFN_TILE, tc=FFN_TC, name=f"ffn_fwd_{tag}")
    x3 = _mm_nn(act, _w(lw, "w_down", act), tm=min(TMM, sv_rows), tn=D_MODEL, tk=D_FF // 2, out_dtype=F32, name=f"down_proj_{tag}",
                residual=x2)
    saved = dict(x=x, h1=h1, z=z, y_mix=y_mix, hs=hs, x2=x2, h2=h2, p=p, act=act)
    return x3, saved


def _layer_bwd(dx3, dx3b, lw, sv, tag, put):
    sv_rows = dx3.shape[0]
    w_in_t, w_out, w_up_b, w_down = (_w(lw, k, dx3) for k in ("w_in_t", "w_out", "w_up_b", "w_down"))
    cw, scw, fcw = (_w(lw, k, dx3) for k in ("cw", "scw", "fcw"))
    dact = _mm_nt(dx3b, w_down, tm=min(TMM, sv_rows), tn=1024, tk=D_MODEL, out_dtype=F32, name=f"down_bwd_x_{tag}")
    g_down = _mm_tn(sv["act"], dx3b, tm=1024, tn=D_MODEL, tk=min(TKW, sv_rows), out_dtype=BF16, name=f"down_bwd_w_{tag}")
    dp, dfcw = _ffn_bwd(sv["p"], dact, fcw, tile=FFN_TILE, tc=FFN_TC, name=f"ffn_bwd_{tag}",
                        token=put("w_down", g_down))
    dh2 = _mm_up_bwd_x(dp, w_up_b, tm=min(TMM, sv_rows), name=f"up_bwd_x_{tag}")
    g_up = _mm_up_bwd_w(sv["h2"], dp, tm=D_MODEL, tk=min(TKW, sv_rows), name=f"up_bwd_w_{tag}")
    dx2, dx2b, dg2 = _norm_bwd(dh2, sv["x2"], lw["g2"], dx3, tm=TM, name=f"norm2_bwd_{tag}",
                               token=put("w_up_b", g_up))
    dy = _mm_nt(dx2b, w_out, tm=min(TMM, sv_rows), tn=768, tk=D_MODEL, out_dtype=F32, name=f"out_bwd_x_{tag}")
    g_out = _mm_tn(sv["y_mix"], dx2b, tm=768, tn=D_MODEL, tk=min(TKW, sv_rows), out_dtype=BF16, name=f"out_bwd_w_{tag}")
    out_token = put("w_out", g_out)
    dz, dcw, dvec, dwa, dwx, dscw = _mixer_bwd(
        sv["z"], sv["hs"], dy, cw, lw["cb"], lw["wa_bd"], lw["wx_bd"], lw["ba"], lw["bx"], lw["lam"],
        scw, tile=MIX_TILE, name=f"mixer_bwd_{tag}", token=out_token)
    dh1 = _mm_nn(dz, w_in_t, tm=min(TMM, sv_rows), tn=D_MODEL, tk=896, out_dtype=F32, name=f"in_bwd_x_{tag}")
    g_in_t = _mm_tn(dz, sv["h1"], tm=896, tn=D_MODEL, tk=min(TKW, sv_rows), out_dtype=BF16, name=f"in_bwd_w_{tag}")
    dx, dxb, dg1 = _norm_bwd(dh1, sv["x"], lw["g1"], dx2, tm=TM, name=f"norm1_bwd_{tag}",
                             token=put("w_in_t", g_in_t))
    small = dict(norm1_g=dg1[0], lru_conv_w=dcw[0:4], lru_conv_b=dvec[0], lru_wa=_head_blocks(dwa),
                 lru_ba=dvec[1], lru_wx=_head_blocks(dwx), lru_bx=dvec[2], lru_lambda=dvec[3],
                 sc_conv_w=dscw[0:3], norm2_g=dg2[0], ffn_conv_w=dfcw[:, 0:3, :])
    return dx, dxb, small


SMALL_ORDER = ("norm1_g", "lru_conv_w", "lru_conv_b", "lru_wa", "lru_ba", "lru_wx", "lru_bx", "lru_lambda",
               "sc_conv_w", "norm2_g", "ffn_conv_w")


def _local_step(x, tgt, layers, final_g, put):
    saved = []
    h = x
    for l in range(DEPTH):
        h, sv = _layer_fwd(h, layers[l], f"l{l}")
        saved.append(sv)
    loss_blk, dx, dxb, dgf = _loss_head(h, final_g, tgt, tm=TM, name="loss_head")
    smalls = [None] * DEPTH
    for l in reversed(range(DEPTH)):
        dx, dxb, smalls[l] = _layer_bwd(dx, dxb, layers[l], saved[l], f"l{l}", functools.partial(put, l))
    return loss_blk[0, 0], dx, smalls, dgf[0]


def kernel(x, norm1_g, w_in, lru_conv_w, lru_conv_b, lru_wa, lru_ba, lru_wx, lru_bx, lru_lambda, sc_conv_w, w_out, norm2_g, w_up, ffn_conv_w, w_down, final_g, loss_target, m_norm1_g, m_w_in, m_lru_conv_w, m_lru_conv_b, m_lru_wa, m_lru_ba, m_lru_wx, m_lru_bx, m_lru_lambda, m_sc_conv_w, m_w_out, m_norm2_g, m_w_up, m_ffn_conv_w, m_w_down, m_final_g, v_norm1_g, v_w_in, v_lru_conv_w, v_lru_conv_b, v_lru_wa, v_lru_ba, v_lru_wx, v_lru_bx, v_lru_lambda, v_sc_conv_w, v_w_out, v_norm2_g, v_w_up, v_ffn_conv_w, v_w_down, v_final_g):
    names = ["norm1_g", "w_in", "lru_conv_w", "lru_conv_b", "lru_wa", "lru_ba", "lru_wx", "lru_bx", "lru_lambda",
             "sc_conv_w", "w_out", "norm2_g", "w_up", "ffn_conv_w", "w_down", "final_g"]
    w = dict(zip(names, [norm1_g, w_in, lru_conv_w, lru_conv_b, lru_wa, lru_ba, lru_wx, lru_bx, lru_lambda,
                         sc_conv_w, w_out, norm2_g, w_up, ffn_conv_w, w_down, final_g]))
    m = dict(zip(names, [m_norm1_g, m_w_in, m_lru_conv_w, m_lru_conv_b, m_lru_wa, m_lru_ba, m_lru_wx, m_lru_bx,
                         m_lru_lambda, m_sc_conv_w, m_w_out, m_norm2_g, m_w_up, m_ffn_conv_w, m_w_down, m_final_g]))
    v = dict(zip(names, [v_norm1_g, v_w_in, v_lru_conv_w, v_lru_conv_b, v_lru_wa, v_lru_ba, v_lru_wx, v_lru_bx,
                         v_lru_lambda, v_sc_conv_w, v_w_out, v_norm2_g, v_w_up, v_ffn_conv_w, v_w_down, v_final_g]))
    my_id = 4 * lax.axis_index("x") + 2 * lax.axis_index("y") + lax.axis_index("c")

    taps = jnp.zeros((DEPTH, 16, 768), F32)
    taps = taps.at[:, 0:4, 0:128].set(lru_conv_w).at[:, 4:7, 0:64].set(sc_conv_w).at[:, 8:11, :].set(ffn_conv_w)
    shards = {}
    for l in range(DEPTH):
        shards[f"w_in_t{l}"] = jnp.swapaxes(w_in[l], 0, 1).astype(BF16)
        if l == 0:
            shards["taps"] = taps.reshape(DEPTH * 16, 768)
        shards[f"w_out{l}"] = w_out[l].astype(BF16)
        shards[f"w_up_b{l}"] = w_up[l].astype(BF16)
        shards[f"w_down{l}"] = w_down[l].astype(BF16)
    ids = iter(range(16))
    got = {}
    chain = [None]
    for group in (("w_in_t0", "taps"), ("w_out0",), ("w_up_b0",),
                  ("w_down0", "w_in_t1", "w_out1", "w_up_b1", "w_down1")):
        lands, chain[0] = _sequencer_copies([shards[k] for k in group], gather=True, name=f"gather_{group[0]}",
                                            collective_id=next(ids), after=None)
        got.update(zip(group, lands))

    def fetch(key, after):
        return got[key]

    def tap_rows(l, lo, hi, width, after):
        tl = fetch("taps", after).reshape(N_DEV, DEPTH, 16, 768)[:, l, lo:hi, 0:width]
        return jnp.transpose(tl, (1, 0, 2)).reshape(hi - lo, N_DEV * width)

    layers = []
    for l in range(DEPTH):
        layers.append(dict(
            g1=norm1_g[l], g2=norm2_g[l], cb=lru_conv_b[l], ba=lru_ba[l], bx=lru_bx[l], lam=lru_lambda[l],
            wa_bd=_block_diag(lru_wa[l]), wx_bd=_block_diag(lru_wx[l]),
            cw=functools.partial(tap_rows, l, 0, 4, 128), scw=functools.partial(tap_rows, l, 4, 7, 64),
            fcw=lambda after, l=l: tap_rows(l, 8, 11, 768, after).reshape(3, 2, D_FF).transpose(1, 0, 2),
            w_in_t=lambda after, l=l: fetch(f"w_in_t{l}", after).reshape(D_IN, D_MODEL),
            w_out=lambda after, l=l: fetch(f"w_out{l}", after).reshape(D_MIX, D_MODEL),
            w_up_b=lambda after, l=l: fetch(f"w_up_b{l}", after),
            w_down=lambda after, l=l: fetch(f"w_down{l}", after).reshape(D_FF, D_MODEL)))

    scatter_handles = {}

    def put(l, key, grad):
        blocks = grad if grad.ndim == 3 else grad.reshape(N_DEV, grad.shape[0] // N_DEV, grad.shape[1])
        (scatter_handles[(l, key)],), chain[0] = _sequencer_copies(
            [blocks], gather=False, name=f"scatter_{key}{l}", collective_id=next(ids), after=chain[0])
        return blocks

    loss_local, dx, smalls, dgf = _local_step(x[0], loss_target[0], layers, final_g, put)
    loss = lax.psum(loss_local, ("x", "y", "c"))

    parts = []
    for l in range(DEPTH):
        for key in ("w_in_t", "w_out", "w_up_b", "w_down"):
            parts.append(scatter_handles[(l, key)])

    flat = [smalls[l][k].reshape(-1) for l in range(DEPTH) for k in SMALL_ORDER] + [dgf.reshape(-1)]
    sizes = [f.shape[0] for f in flat]
    total = sum(sizes)
    rows = -(-total // (N_DEV * 128 * 8)) * 8
    flat.append(jnp.zeros((N_DEV * rows * 128 - total,), F32))
    small_sum = _all_reduce_small(jnp.concatenate(flat).reshape(N_DEV, rows, 128), name="reduce_small").reshape(-1)
    small_g, off = [], 0
    for sz in sizes:
        small_g.append(small_sum[off:off + sz])
        off += sz
    gs = {}
    for l in range(DEPTH):
        for i, k in enumerate(SMALL_ORDER):
            gs.setdefault(k, []).append(small_g[l * len(SMALL_ORDER) + i])
    g_final = small_g[-1]

    grads = {}
    per_layer = {k: [] for k in ("w_in", "w_out", "w_up", "w_down")}
    for l in range(DEPTH):
        p_in, p_out, p_up, p_down = parts[4 * l:4 * l + 4]
        per_layer["w_in"].append(jnp.swapaxes(_sum_parts(p_in, name=f"sum_w_in_l{l}"), 0, 1))
        per_layer["w_out"].append(_sum_parts(p_out, name=f"sum_w_out_l{l}"))
        per_layer["w_up"].append(_sum_parts(p_up, name=f"sum_w_up_l{l}"))
        per_layer["w_down"].append(_sum_parts(p_down, name=f"sum_w_down_l{l}"))
    for k, lst in per_layer.items():
        grads[k] = jnp.stack(lst)
    for k in ("norm1_g", "lru_conv_b", "lru_ba", "lru_bx", "lru_lambda", "norm2_g"):
        grads[k] = jnp.stack(gs[k]).reshape(DEPTH, -1)
    for k in ("lru_wa", "lru_wx"):
        grads[k] = jnp.stack(gs[k]).reshape(DEPTH, LRU_HEADS, LRU_HEAD_DIM, LRU_HEAD_DIM)
    grads["final_g"] = g_final
    cw_full = jnp.stack(gs["lru_conv_w"]).reshape(DEPTH, 4, N_DEV, 128)
    grads["lru_conv_w"] = lax.dynamic_index_in_dim(cw_full, my_id, axis=2, keepdims=False)
    scw_full = jnp.stack(gs["sc_conv_w"]).reshape(DEPTH, 3, N_DEV, 64)
    grads["sc_conv_w"] = lax.dynamic_index_in_dim(scw_full, my_id, axis=2, keepdims=False)
    fcw_full = jnp.stack(gs["ffn_conv_w"]).reshape(DEPTH, 2, 3, D_FF).transpose(0, 2, 1, 3).reshape(DEPTH, 3, N_DEV, 768)
    grads["ffn_conv_w"] = lax.dynamic_index_in_dim(fcw_full, my_id, axis=2, keepdims=False)

    deltas, new_m, new_v = {}, {}, {}
    for k in names:
        shape = w[k].shape
        cols = shape[-1]
        as2d = lambda a: a.reshape(-1, cols)
        d, nm, nv = _adamw(as2d(w[k]), as2d(grads[k]), as2d(m[k]), as2d(v[k]), name=f"adamw_{k}")
        deltas[k], new_m[k], new_v[k] = d.reshape(shape), nm.reshape(shape), nv.reshape(shape)

    return (loss, dx[None], *[grads[k] for k in names], *[deltas[k] for k in names],
            *[new_m[k] for k in names], *[new_v[k] for k in names])
```

```python
import functools
import math

import jax
import jax.numpy as jnp
from jax import lax
from jax.experimental import pallas as pl
from jax.experimental.pallas import tpu as pltpu
from jax.experimental.pallas import tpu_sc as plsc

F32 = jnp.float32
BF16 = jnp.bfloat16

N_DEV = 8
DEPTH = 2
D_MODEL = 1024
D_LRU = 1024
D_SC = 512
D_MIX = D_LRU + D_SC
D_IN = 2 * D_LRU + 3 * D_SC
D_FF = 3072
LRU_HEADS = 16
LRU_HEAD_DIM = 64
LRU_GROUP = 256
N_GROUPS = D_LRU // LRU_GROUP
HEADS_PER_GROUP = LRU_GROUP // LRU_HEAD_DIM
RG_C = 8.0
EPS = 1e-6
HALO = 8

ADAM_LR = 0.001
ADAM_B1 = 0.9
ADAM_B2 = 0.999
ADAM_EPS = 1e-08
ADAM_WD = 0.01
ADAM_STEP = 10

GELU_C = math.sqrt(2.0 / math.pi)
GELU_A = 0.044715

VMEM_LIMIT = 56 * 1024 * 1024
MESH = pl.DeviceIdType.MESH


def _params(*sem):
    return pltpu.CompilerParams(dimension_semantics=tuple(sem) if sem else None,
                                vmem_limit_bytes=VMEM_LIMIT)


def _gelu_parts(x):
    x2 = x * x
    t = jnp.tanh(GELU_C * (x + GELU_A * x * x2))
    half = 0.5 * (1.0 + t)
    g = x * half
    dg = half + 0.5 * x * (1.0 - t * t) * (GELU_C * (1.0 + 3.0 * GELU_A * x2))
    return g, dg


def _gelu(x):
    t = jnp.tanh(GELU_C * (x + GELU_A * x * x * x))
    return 0.5 * x * (1.0 + t)


def _sigmoid(x):
    return 1.0 / (1.0 + jnp.exp(-x))


def _softplus(x):
    e = jnp.exp(-jnp.abs(x))
    u = 1.0 + e
    log1p_e = jnp.where(u == 1.0, e, jnp.log(u) * (e / (u - 1.0)))
    return jnp.maximum(x, 0.0) + log1p_e


def _rms(x):
    ms = jnp.mean(x * x, axis=-1, keepdims=True)
    return lax.rsqrt(ms + EPS)


def _dot(a, b, dims):
    return lax.dot_general(a, b, (dims, ((), ())), preferred_element_type=F32)


NN = ((1,), (0,))
NT = ((1,), (1,))
TN = ((0,), (0,))


def _matmul(a, b, *, dims, grid, a_spec, b_spec, o_spec, out_shape, acc_shape, name,
            residual=None, r_spec=None):
    nk = grid[2]

    def body(*refs):
        a_ref, b_ref = refs[0], refs[1]
        r_ref = refs[2] if residual is not None else None
        o_ref = refs[3] if residual is not None else refs[2]
        prod = _dot(a_ref[...].astype(BF16), b_ref[...].astype(BF16), dims)

        def finish(total):
            if r_ref is not None:
                total = total + r_ref[...]
            o_ref[...] = total.astype(o_ref.dtype)

        if nk == 1:
            finish(prod)
            return
        acc_ref = refs[-1]
        k = pl.program_id(2)

        @pl.when(k == 0)
        def _():
            acc_ref[...] = prod

        @pl.when(jnp.logical_and(k > 0, k < nk - 1))
        def _():
            acc_ref[...] += prod

        @pl.when(k == nk - 1)
        def _():
            finish(acc_ref[...] + prod)

    in_specs = [a_spec, b_spec]
    args = [a, b]
    if residual is not None:
        in_specs.append(r_spec)
        args.append(residual)
    return pl.pallas_call(
        body, name=name, grid=grid, in_specs=in_specs, out_specs=o_spec, out_shape=out_shape,
        scratch_shapes=[pltpu.VMEM(acc_shape, F32)] if nk > 1 else [],
        compiler_params=_params("parallel", "parallel", "arbitrary"),
    )(*args)


def _mm_nn(a, b, *, tm, tn, tk, out_dtype, name, residual=None):
    m, kd = a.shape
    n = b.shape[1]
    return _matmul(
        a, b, dims=NN, grid=(m // tm, n // tn, kd // tk),
        a_spec=pl.BlockSpec((tm, tk), lambda i, j, k: (i, k)),
        b_spec=pl.BlockSpec((tk, tn), lambda i, j, k: (k, j)),
        o_spec=pl.BlockSpec((tm, tn), lambda i, j, k: (i, j)),
        out_shape=jax.ShapeDtypeStruct((m, n), out_dtype), acc_shape=(tm, tn), name=name,
        residual=residual, r_spec=pl.BlockSpec((tm, tn), lambda i, j, k: (i, j)))


def _mm_nt(a, b, *, tm, tn, tk, out_dtype, name):
    m, kd = a.shape
    n = b.shape[0]
    return _matmul(
        a, b, dims=NT, grid=(m // tm, n // tn, kd // tk),
        a_spec=pl.BlockSpec((tm, tk), lambda i, j, k: (i, k)),
        b_spec=pl.BlockSpec((tn, tk), lambda i, j, k: (j, k)),
        o_spec=pl.BlockSpec((tm, tn), lambda i, j, k: (i, j)),
        out_shape=jax.ShapeDtypeStruct((m, n), out_dtype), acc_shape=(tm, tn), name=name)


def _mm_tn(a, b, *, tm, tn, tk, out_dtype, name):
    kd, m = a.shape
    n = b.shape[1]
    return _matmul(
        a, b, dims=TN, grid=(m // tm, n // tn, kd // tk),
        a_spec=pl.BlockSpec((tk, tm), lambda i, j, k: (k, i)),
        b_spec=pl.BlockSpec((tk, tn), lambda i, j, k: (k, j)),
        o_spec=pl.BlockSpec((tm, tn), lambda i, j, k: (i, j)),
        out_shape=jax.ShapeDtypeStruct((m, n), out_dtype), acc_shape=(tm, tn), name=name)


def _mm_up(h2, w_up_b, *, tm, name):
    s = h2.shape[0]
    nb = w_up_b.shape[2]
    per_half = D_FF // nb
    return _matmul(
        h2, w_up_b, dims=NN, grid=(s // tm, N_DEV, 1),
        a_spec=pl.BlockSpec((tm, D_MODEL), lambda i, j, k: (i, 0)),
        b_spec=pl.BlockSpec((None, D_MODEL, nb), lambda i, j, k: (j, 0, 0)),
        o_spec=pl.BlockSpec((None, tm, nb), lambda i, j, k: (j // per_half, i, j % per_half)),
        out_shape=jax.ShapeDtypeStruct((2, s, D_FF), F32), acc_shape=(tm, nb), name=name)


def _mm_up_bwd_x(dp, w_up_b, *, tm, name):
    s = dp.shape[1]
    nb = w_up_b.shape[2]
    per_half = D_FF // nb
    return _matmul(
        dp, w_up_b, dims=NT, grid=(s // tm, 1, N_DEV),
        a_spec=pl.BlockSpec((None, tm, nb), lambda i, j, k: (k // per_half, i, k % per_half)),
        b_spec=pl.BlockSpec((None, D_MODEL, nb), lambda i, j, k: (k, 0, 0)),
        o_spec=pl.BlockSpec((tm, D_MODEL), lambda i, j, k: (i, 0)),
        out_shape=jax.ShapeDtypeStruct((s, D_MODEL), F32), acc_shape=(tm, D_MODEL), name=name)


def _mm_up_bwd_w(h2, dp, *, tm, tk, name):
    s = h2.shape[0]
    nb = D_FF * 2 // N_DEV
    per_half = D_FF // nb
    return _matmul(
        h2, dp, dims=TN, grid=(D_MODEL // tm, N_DEV, s // tk),
        a_spec=pl.BlockSpec((tk, tm), lambda i, j, k: (k, i)),
        b_spec=pl.BlockSpec((None, tk, nb), lambda i, j, k: (j // per_half, k, j % per_half)),
        o_spec=pl.BlockSpec((None, tm, nb), lambda i, j, k: (j, i, 0)),
        out_shape=jax.ShapeDtypeStruct((N_DEV, D_MODEL, nb), BF16), acc_shape=(tm, nb), name=name)


def _norm_fwd(x, g, *, tm, name):
    s, d = x.shape

    def body(x_ref, g_ref, h_ref):
        xv = x_ref[...]
        h_ref[...] = (xv * _rms(xv) * g_ref[...]).astype(BF16)

    return pl.pallas_call(
        body, name=name, grid=(s // tm,),
        in_specs=[pl.BlockSpec((tm, d), lambda i: (i, 0)), pl.BlockSpec((1, d), lambda i: (0, 0))],
        out_specs=pl.BlockSpec((tm, d), lambda i: (i, 0)),
        out_shape=jax.ShapeDtypeStruct((s, d), BF16),
        compiler_params=_params("parallel"),
    )(x, g.reshape(1, d))


def _behind(token):
    return jnp.zeros((8, 128), F32) if token is None else token


def _norm_bwd(dh, x, g, dres, *, tm, name, token=None):
    s, d = x.shape

    def body(dh_ref, x_ref, g_ref, dres_ref, token_ref, dx_ref, dxb_ref, dg_ref):
        @pl.when(pl.program_id(0) == 0)
        def _():
            dg_ref[...] = jnp.zeros_like(dg_ref)

        xv = x_ref[...]
        rstd = _rms(xv)
        n = xv * rstd
        dhv = dh_ref[...]
        dn = dhv * g_ref[...]
        dx = rstd * (dn - n * jnp.mean(dn * n, axis=-1, keepdims=True))
        dx = dres_ref[...] + dx
        dx_ref[...] = dx
        dxb_ref[...] = dx.astype(BF16)
        dg_ref[0:1, :] += jnp.sum(dhv * n, axis=0, keepdims=True)

    return pl.pallas_call(
        body, name=name, grid=(s // tm,),
        in_specs=[pl.BlockSpec((tm, d), lambda i: (i, 0)), pl.BlockSpec((tm, d), lambda i: (i, 0)),
                  pl.BlockSpec((1, d), lambda i: (0, 0)), pl.BlockSpec((tm, d), lambda i: (i, 0)),
                  pl.BlockSpec(memory_space=pl.ANY)],
        out_specs=[pl.BlockSpec((tm, d), lambda i: (i, 0)), pl.BlockSpec((tm, d), lambda i: (i, 0)),
                   pl.BlockSpec((8, d), lambda i: (0, 0))],
        out_shape=[jax.ShapeDtypeStruct((s, d), F32), jax.ShapeDtypeStruct((s, d), BF16),
                   jax.ShapeDtypeStruct((8, d), F32)],
        compiler_params=_params("arbitrary"),
    )(dh, x, g.reshape(1, d), dres, _behind(token))


def _loss_head(x, g, tgt, *, tm, name):
    s, d = x.shape

    def body(x_ref, g_ref, t_ref, loss_ref, dx_ref, dxb_ref, dg_ref):
        @pl.when(pl.program_id(0) == 0)
        def _():
            dg_ref[...] = jnp.zeros_like(dg_ref)
            loss_ref[...] = jnp.zeros_like(loss_ref)

        xv = x_ref[...]
        gv = g_ref[...]
        rstd = _rms(xv)
        n = xv * rstd
        e = n * gv - t_ref[...]
        part = 0.5 * jnp.sum(jnp.mean(e * e, axis=-1, keepdims=True), axis=0, keepdims=True)
        loss_ref[...] += jnp.broadcast_to(part, loss_ref.shape)
        dy = e * (1.0 / d)
        dn = dy * gv
        dx = rstd * (dn - n * jnp.mean(dn * n, axis=-1, keepdims=True))
        dx_ref[...] = dx
        dxb_ref[...] = dx.astype(BF16)
        dg_ref[0:1, :] += jnp.sum(dy * n, axis=0, keepdims=True)

    return pl.pallas_call(
        body, name=name, grid=(s // tm,),
        in_specs=[pl.BlockSpec((tm, d), lambda i: (i, 0)), pl.BlockSpec((1, d), lambda i: (0, 0)),
                  pl.BlockSpec((tm, d), lambda i: (i, 0))],
        out_specs=[pl.BlockSpec((8, 128), lambda i: (0, 0)), pl.BlockSpec((tm, d), lambda i: (i, 0)),
                   pl.BlockSpec((tm, d), lambda i: (i, 0)), pl.BlockSpec((8, d), lambda i: (0, 0))],
        out_shape=[jax.ShapeDtypeStruct((8, 128), F32), jax.ShapeDtypeStruct((s, d), F32),
                   jax.ShapeDtypeStruct((s, d), BF16), jax.ShapeDtypeStruct((8, d), F32)],
        compiler_params=_params("arbitrary"),
    )(x, g.reshape(1, d), tgt)


def _scan_rows(a_ref, b_ref, h_ref, carry, *, rows, reverse):
    width = a_ref.shape[1]
    n_chunks = rows // 8
    row = lax.broadcasted_iota(jnp.int32, (8, width), 0)

    def step(ci, carry):
        chunk = (n_chunks - 1 - ci) if reverse else ci
        off = pl.multiple_of(chunk * 8, 8)
        av = a_ref[pl.ds(off, 8), :]
        bv = b_ref[pl.ds(off, 8), :]
        for sh in (1, 2, 4):
            if reverse:
                a_sh = pltpu.roll(av, 8 - sh, 0)
                b_sh = pltpu.roll(bv, 8 - sh, 0)
                m = row < 8 - sh
            else:
                a_sh = pltpu.roll(av, sh, 0)
                b_sh = pltpu.roll(bv, sh, 0)
                m = row >= sh
            bv = jnp.where(m, av * b_sh + bv, bv)
            av = jnp.where(m, av * a_sh, av)
        h = av * carry + bv
        h_ref[pl.ds(off, 8), :] = h
        return h[0:1, :] if reverse else h[7:8, :]

    return lax.fori_loop(0, n_chunks, step, carry)


def _lru_gates(lx, wa_ref, wx_ref, ba, bx, sp):
    lxb = lx.astype(BF16)
    pre_r = jnp.concatenate(
        [_dot(lxb[:, g * LRU_GROUP:(g + 1) * LRU_GROUP], wa_ref[g], NN) for g in range(N_GROUPS)], axis=1)
    pre_i = jnp.concatenate(
        [_dot(lxb[:, g * LRU_GROUP:(g + 1) * LRU_GROUP], wx_ref[g], NN) for g in range(N_GROUPS)], axis=1)
    r = _sigmoid(pre_r + ba)
    ig = _sigmoid(pre_i + bx)
    log_a = (-RG_C * r) * sp
    a = jnp.exp(log_a)
    mult = jnp.sqrt(-jnp.tanh(log_a) * (a * a + 1.0))
    return lxb, r, ig, a, mult


def _mixer_fwd(z, cw, cb, wa_bd, wx_bd, ba, bx, lam, scw, *, tile, name):
    s = z.shape[0]
    n_tiles = s // tile

    def body(z_ref, cw_ref, cb_ref, wa_ref, wx_ref, ba_ref, bx_ref, lam_ref, scw_ref,
             y_ref, hs_ref, ext_lx, ext_q, a_s, b_s, h_car):
        i = pl.program_id(0)

        @pl.when(i == 0)
        def _():
            ext_lx[0:HALO, :] = jnp.zeros((HALO, D_LRU), F32)
            ext_q[0:HALO, :] = jnp.zeros((HALO, D_SC), F32)
            h_car[...] = jnp.zeros_like(h_car)

        ext_lx[HALO:HALO + tile, :] = z_ref[:, 0:D_LRU]
        ext_q[HALO:HALO + tile, :] = z_ref[:, 2 * D_LRU + D_SC:2 * D_LRU + 2 * D_SC] * z_ref[:, 2 * D_LRU + 2 * D_SC:D_IN]
        lx = cb_ref[...] + cw_ref[0:1, :] * ext_lx[pl.ds(HALO - 3, tile), :]
        for k in range(1, 4):
            lx = lx + cw_ref[k:k + 1, :] * ext_lx[pl.ds(HALO - 3 + k, tile), :]
        cq = scw_ref[0:1, :] * ext_q[pl.ds(HALO - 2, tile), :]
        for k in range(1, 3):
            cq = cq + scw_ref[k:k + 1, :] * ext_q[pl.ds(HALO - 2 + k, tile), :]
        ext_lx[0:HALO, :] = ext_lx[tile:tile + HALO, :]
        ext_q[0:HALO, :] = ext_q[tile:tile + HALO, :]

        sp = _softplus(-lam_ref[...])
        _, _, ig, a, mult = _lru_gates(lx, wa_ref, wx_ref, ba_ref[...], bx_ref[...], sp)
        a_s[...] = a
        b_s[...] = mult * (ig * lx)
        h_car[0:1, :] = _scan_rows(a_s, b_s, hs_ref, h_car[0:1, :], rows=tile, reverse=False)

        y_ref[:, 0:D_LRU] = (hs_ref[...] * _gelu(z_ref[:, D_LRU:2 * D_LRU])).astype(BF16)
        y_ref[:, D_LRU:D_MIX] = (z_ref[:, 2 * D_LRU:2 * D_LRU + D_SC] * cq).astype(BF16)

    full = lambda shape: pl.BlockSpec(shape, lambda i: (0,) * len(shape))
    return pl.pallas_call(
        body, name=name, grid=(n_tiles,),
        in_specs=[pl.BlockSpec((tile, D_IN), lambda i: (i, 0)),
                  full((4, D_LRU)), full((1, D_LRU)),
                  full((N_GROUPS, LRU_GROUP, LRU_GROUP)), full((N_GROUPS, LRU_GROUP, LRU_GROUP)),
                  full((1, D_LRU)), full((1, D_LRU)), full((1, D_LRU)), full((3, D_SC))],
        out_specs=[pl.BlockSpec((tile, D_MIX), lambda i: (i, 0)), pl.BlockSpec((tile, D_LRU), lambda i: (i, 0))],
        out_shape=[jax.ShapeDtypeStruct((s, D_MIX), BF16), jax.ShapeDtypeStruct((s, D_LRU), F32)],
        scratch_shapes=[pltpu.VMEM((tile + HALO, D_LRU), F32), pltpu.VMEM((tile + HALO, D_SC), F32),
                        pltpu.VMEM((tile, D_LRU), F32), pltpu.VMEM((tile, D_LRU), F32),
                        pltpu.VMEM((8, D_LRU), F32)],
        compiler_params=_params("arbitrary"),
    )(z, cw, cb.reshape(1, -1), wa_bd, wx_bd, ba.reshape(1, -1), bx.reshape(1, -1), lam.reshape(1, -1), scw)


def _mixer_bwd(z, hs, dy, cw, cb, wa_bd, wx_bd, ba, bx, lam, scw, *, tile, name, token=None):
    s = z.shape[0]
    n_tiles = s // tile
    per8 = tile // 8

    def body(z_ref, zp_ref, hs_ref, hsp_ref, dy_ref, cw_ref, cb_ref, wa_ref, wx_ref, ba_ref, bx_ref, lam_ref, scw_ref,
             token_ref, dz_ref, dcw_ref, dvec_ref, dwa_ref, dwx_ref, dscw_ref,
             ext_lx, ext_q, ext_h, ext_a, ext_dlx, ext_dcq, a_s, b_s, lam_s, l_car):
        i = pl.program_id(0)
        first_tile = i == n_tiles - 1

        @pl.when(i == 0)
        def _():
            for ref in (dcw_ref, dvec_ref, dwa_ref, dwx_ref, dscw_ref, l_car):
                ref[...] = jnp.zeros_like(ref)
            ext_a[tile:tile + HALO, :] = jnp.zeros((HALO, D_LRU), F32)
            ext_dlx[tile:tile + HALO, :] = jnp.zeros((HALO, D_LRU), F32)
            ext_dcq[tile:tile + HALO, :] = jnp.zeros((HALO, D_SC), F32)

        keep = jnp.where(first_tile, 0.0, 1.0)
        sb = z_ref[:, 2 * D_LRU:2 * D_LRU + D_SC]
        sc = z_ref[:, 2 * D_LRU + D_SC:2 * D_LRU + 2 * D_SC]
        sx = z_ref[:, 2 * D_LRU + 2 * D_SC:D_IN]
        ext_lx[0:HALO, :] = zp_ref[:, 0:D_LRU] * keep
        ext_lx[HALO:HALO + tile, :] = z_ref[:, 0:D_LRU]
        ext_q[0:HALO, :] = zp_ref[:, 2 * D_LRU + D_SC:2 * D_LRU + 2 * D_SC] * zp_ref[:, 2 * D_LRU + 2 * D_SC:D_IN] * keep
        ext_q[HALO:HALO + tile, :] = sc * sx
        ext_h[0:HALO, :] = hsp_ref[...] * keep
        ext_h[HALO:HALO + tile, :] = hs_ref[...]

        lx = cb_ref[...] + cw_ref[0:1, :] * ext_lx[pl.ds(HALO - 3, tile), :]
        for k in range(1, 4):
            lx = lx + cw_ref[k:k + 1, :] * ext_lx[pl.ds(HALO - 3 + k, tile), :]
        cq = scw_ref[0:1, :] * ext_q[pl.ds(HALO - 2, tile), :]
        for k in range(1, 3):
            cq = cq + scw_ref[k:k + 1, :] * ext_q[pl.ds(HALO - 2 + k, tile), :]

        sp = _softplus(-lam_ref[...])
        lxb, r, ig, a, mult = _lru_gates(lx, wa_ref, wx_ref, ba_ref[...], bx_ref[...], sp)

        ge, dge = _gelu_parts(z_ref[:, D_LRU:2 * D_LRU])
        dy_lru = dy_ref[:, 0:D_LRU]
        dz_ref[:, D_LRU:2 * D_LRU] = (dy_lru * hs_ref[...] * dge).astype(BF16)

        ext_a[0:tile, :] = a
        a_s[...] = ext_a[pl.ds(1, tile), :]
        b_s[...] = dy_lru * ge
        l_car[0:1, :] = _scan_rows(a_s, b_s, lam_s, l_car[0:1, :], rows=tile, reverse=True)
        ext_a[tile:tile + HALO, :] = ext_a[0:HALO, :]
        lv = lam_s[...]

        da = lv * ext_h[pl.ds(HALO - 1, tile), :]
        d_mult = lv * ig * lx
        d_i = lv * mult * lx
        dlx = lv * mult * ig
        dlog_a = da * a - d_mult * (a * a) / mult
        d_r = dlog_a * (-RG_C * sp)
        dpre_r = d_r * r * (1.0 - r)
        dpre_i = d_i * ig * (1.0 - ig)
        dvec_ref[1:2, :] += jnp.sum(dpre_r, axis=0, keepdims=True)
        dvec_ref[2:3, :] += jnp.sum(dpre_i, axis=0, keepdims=True)
        dvec_ref[3:4, :] += jnp.sum(dlog_a * (-RG_C * r), axis=0, keepdims=True)
        dpr_b = dpre_r.astype(BF16)
        dpi_b = dpre_i.astype(BF16)
        back = []
        for g in range(N_GROUPS):
            cols = slice(g * LRU_GROUP, (g + 1) * LRU_GROUP)
            dwa_ref[g] += _dot(lxb[:, cols], dpr_b[:, cols], TN)
            dwx_ref[g] += _dot(lxb[:, cols], dpi_b[:, cols], TN)
            back.append(_dot(dpr_b[:, cols], wa_ref[g], NT) + _dot(dpi_b[:, cols], wx_ref[g], NT))
        dlx = dlx + jnp.concatenate(back, axis=1)
        dvec_ref[0:1, :] += jnp.sum(dlx, axis=0, keepdims=True)

        ext_dlx[0:tile, :] = dlx
        for k in range(4):
            dcw_ref[k:k + 1, :] += jnp.sum(dlx * ext_lx[pl.ds(HALO - 3 + k, tile), :], axis=0, keepdims=True)
        dlxp = cw_ref[3:4, :] * dlx
        for k in range(3):
            dlxp = dlxp + cw_ref[k:k + 1, :] * ext_dlx[pl.ds(3 - k, tile), :]
        dz_ref[:, 0:D_LRU] = dlxp.astype(BF16)
        ext_dlx[tile:tile + HALO, :] = ext_dlx[0:HALO, :]

        dy_sc = dy_ref[:, D_LRU:D_MIX]
        dz_ref[:, 2 * D_LRU:2 * D_LRU + D_SC] = (dy_sc * cq).astype(BF16)
        dcq = dy_sc * sb
        ext_dcq[0:tile, :] = dcq
        for k in range(3):
            dscw_ref[k:k + 1, :] += jnp.sum(dcq * ext_q[pl.ds(HALO - 2 + k, tile), :], axis=0, keepdims=True)
        dq = scw_ref[2:3, :] * dcq
        for k in range(2):
            dq = dq + scw_ref[k:k + 1, :] * ext_dcq[pl.ds(2 - k, tile), :]
        dz_ref[:, 2 * D_LRU + D_SC:2 * D_LRU + 2 * D_SC] = (dq * sx).astype(BF16)
        dz_ref[:, 2 * D_LRU + 2 * D_SC:D_IN] = (dq * sc).astype(BF16)
        ext_dcq[tile:tile + HALO, :] = ext_dcq[0:HALO, :]

        @pl.when(i == n_tiles - 1)
        def _():
            dvec_ref[3:4, :] = dvec_ref[3:4, :] * (-_sigmoid(-lam_ref[...]))

    rev = lambda i: n_tiles - 1 - i
    prev8 = lambda i: jnp.maximum(rev(i) * per8 - 1, 0)
    full = lambda shape: pl.BlockSpec(shape, lambda i: (0,) * len(shape))
    return pl.pallas_call(
        body, name=name, grid=(n_tiles,),
        in_specs=[pl.BlockSpec((tile, D_IN), lambda i: (rev(i), 0)),
                  pl.BlockSpec((HALO, D_IN), lambda i: (prev8(i), 0)),
                  pl.BlockSpec((tile, D_LRU), lambda i: (rev(i), 0)),
                  pl.BlockSpec((HALO, D_LRU), lambda i: (prev8(i), 0)),
                  pl.BlockSpec((tile, D_MIX), lambda i: (rev(i), 0)),
                  full((4, D_LRU)), full((1, D_LRU)),
                  full((N_GROUPS, LRU_GROUP, LRU_GROUP)), full((N_GROUPS, LRU_GROUP, LRU_GROUP)),
                  full((1, D_LRU)), full((1, D_LRU)), full((1, D_LRU)), full((3, D_SC)),
                  pl.BlockSpec(memory_space=pl.ANY)],
        out_specs=[pl.BlockSpec((tile, D_IN), lambda i: (rev(i), 0)),
                   full((8, D_LRU)), full((8, D_LRU)),
                   full((N_GROUPS, LRU_GROUP, LRU_GROUP)), full((N_GROUPS, LRU_GROUP, LRU_GROUP)),
                   full((8, D_SC))],
        out_shape=[jax.ShapeDtypeStruct((s, D_IN), BF16),
                   jax.ShapeDtypeStruct((8, D_LRU), F32), jax.ShapeDtypeStruct((8, D_LRU), F32),
                   jax.ShapeDtypeStruct((N_GROUPS, LRU_GROUP, LRU_GROUP), F32),
                   jax.ShapeDtypeStruct((N_GROUPS, LRU_GROUP, LRU_GROUP), F32),
                   jax.ShapeDtypeStruct((8, D_SC), F32)],
        scratch_shapes=[pltpu.VMEM((tile + HALO, D_LRU), F32), pltpu.VMEM((tile + HALO, D_SC), F32),
                        pltpu.VMEM((tile + HALO, D_LRU), F32), pltpu.VMEM((tile + HALO, D_LRU), F32),
                        pltpu.VMEM((tile + HALO, D_LRU), F32), pltpu.VMEM((tile + HALO, D_SC), F32),
                        pltpu.VMEM((tile, D_LRU), F32), pltpu.VMEM((tile, D_LRU), F32),
                        pltpu.VMEM((tile, D_LRU), F32), pltpu.VMEM((8, D_LRU), F32)],
        compiler_params=_params("arbitrary"),
    )(z, z, hs, hs, dy, cw, cb.reshape(1, -1), wa_bd, wx_bd, ba.reshape(1, -1), bx.reshape(1, -1),
      lam.reshape(1, -1), scw, _behind(token))


FFN_ROWS = 16


def _conv3_rows(w_ref, ext_ref, half, row, rows):
    acc = w_ref[half, 0:1, :] * ext_ref[half, pl.ds(row - 2, rows), :]
    for k in range(1, 3):
        acc = acc + w_ref[half, k:k + 1, :] * ext_ref[half, pl.ds(row - 2 + k, rows), :]
    return acc


def _ffn_fwd(p, fcw, *, tile, tc, name):
    s = p.shape[1]
    per8 = tile // 8

    def body(p_ref, pp_ref, w_ref, act_ref, ext_p):
        i = pl.program_id(0)
        keep = jnp.where(i == 0, 0.0, 1.0)
        ext_p[:, 0:HALO, :] = pp_ref[...] * keep
        ext_p[:, HALO:HALO + tile, :] = p_ref[...]
        for r0 in range(0, tile, FFN_ROWS):
            u = [_conv3_rows(w_ref, ext_p, half, HALO + r0, FFN_ROWS) for half in range(2)]
            act_ref[r0:r0 + FFN_ROWS, :] = (_gelu(u[0]) * u[1]).astype(BF16)

    return pl.pallas_call(
        body, name=name, grid=(s // tile, D_FF // tc),
        in_specs=[pl.BlockSpec((2, tile, tc), lambda i, j: (0, i, j)),
                  pl.BlockSpec((2, HALO, tc), lambda i, j: (0, jnp.maximum(i * per8 - 1, 0), j)),
                  pl.BlockSpec((2, 3, tc), lambda i, j: (0, 0, j))],
        out_specs=pl.BlockSpec((tile, tc), lambda i, j: (i, j)),
        out_shape=jax.ShapeDtypeStruct((s, D_FF), BF16),
        scratch_shapes=[pltpu.VMEM((2, tile + HALO, tc), F32)],
        compiler_params=_params("parallel", "parallel"),
    )(p, p, fcw)


def _ffn_bwd(p, dact, fcw, *, tile, tc, name, token=None):
    s = p.shape[1]
    n_tiles = s // tile
    per8 = tile // 8
    ext_rows = tile + HALO

    def body(p_ref, pp_ref, pn_ref, da_ref, dan_ref, w_ref, token_ref, dp_ref, dw_ref, ext_p, ext_du, acc_w):
        i = pl.program_id(1)

        @pl.when(i == 0)
        def _():
            acc_w[...] = jnp.zeros_like(acc_w)

        keep_prev = jnp.where(i == 0, 0.0, 1.0)
        keep_next = jnp.where(i == n_tiles - 1, 0.0, 1.0)
        ext_p[:, 0:HALO, :] = pp_ref[...] * keep_prev
        ext_p[:, HALO:HALO + tile, :] = p_ref[...]
        ext_p[:, HALO + tile:2 * HALO + tile, :] = pn_ref[...] * keep_next

        def du_rows(r0, rows, da):
            u = [_conv3_rows(w_ref, ext_p, half, HALO + r0, rows) for half in range(2)]
            ge, dge = _gelu_parts(u[0])
            du = (da * u[1] * dge, da * ge)
            for half in range(2):
                ext_du[half, r0:r0 + rows, :] = du[half]
            return du

        du_rows(tile, HALO, dan_ref[...] * keep_next)
        for r0 in range(tile - FFN_ROWS, -1, -FFN_ROWS):
            du = du_rows(r0, FFN_ROWS, da_ref[r0:r0 + FFN_ROWS, :])
            for half in range(2):
                acc = w_ref[half, 2:3, :] * du[half]
                for k in range(2):
                    acc = acc + w_ref[half, k:k + 1, :] * ext_du[half, pl.ds(r0 + 2 - k, FFN_ROWS), :]
                dp_ref[half, r0:r0 + FFN_ROWS, :] = acc.astype(BF16)
                for k in range(3):
                    prod = du[half] * ext_p[half, pl.ds(HALO + r0 - 2 + k, FFN_ROWS), :]
                    acc_w[half, k] += sum(prod[q:q + 8] for q in range(0, FFN_ROWS, 8))

        @pl.when(i == n_tiles - 1)
        def _():
            dw_ref[...] = jnp.zeros_like(dw_ref)
            for half in range(2):
                for k in range(3):
                    dw_ref[half, k:k + 1, :] = jnp.sum(acc_w[half, k], axis=0, keepdims=True)

    return pl.pallas_call(
        body, name=name, grid=(D_FF // tc, n_tiles),
        in_specs=[pl.BlockSpec((2, tile, tc), lambda j, i: (0, i, j)),
                  pl.BlockSpec((2, HALO, tc), lambda j, i: (0, jnp.maximum(i * per8 - 1, 0), j)),
                  pl.BlockSpec((2, HALO, tc), lambda j, i: (0, jnp.minimum((i + 1) * per8, n_tiles * per8 - 1), j)),
                  pl.BlockSpec((tile, tc), lambda j, i: (i, j)),
                  pl.BlockSpec((HALO, tc), lambda j, i: (jnp.minimum((i + 1) * per8, n_tiles * per8 - 1), j)),
                  pl.BlockSpec((2, 3, tc), lambda j, i: (0, 0, j)), pl.BlockSpec(memory_space=pl.ANY)],
        out_specs=[pl.BlockSpec((2, tile, tc), lambda j, i: (0, i, j)),
                   pl.BlockSpec((2, 8, tc), lambda j, i: (0, 0, j))],
        out_shape=[jax.ShapeDtypeStruct((2, s, D_FF), BF16), jax.ShapeDtypeStruct((2, 8, D_FF), F32)],
        scratch_shapes=[pltpu.VMEM((2, tile + 2 * HALO, tc), F32), pltpu.VMEM((2, ext_rows, tc), F32),
                        pltpu.VMEM((2, 3, 8, tc), F32)],
        compiler_params=_params("parallel", "arbitrary"),
    )(p, p, p, dact, dact, fcw, _behind(token))


def _adamw_math(w, g, m, v):
    m = ADAM_B1 * m + (1.0 - ADAM_B1) * g
    v = ADAM_B2 * v + (1.0 - ADAM_B2) * (g * g)
    m_hat = m / (1.0 - ADAM_B1 ** ADAM_STEP)
    v_hat = v / (1.0 - ADAM_B2 ** ADAM_STEP)
    delta = -ADAM_LR * (m_hat / (jnp.sqrt(v_hat) + ADAM_EPS) + ADAM_WD * w)
    return delta, m, v


def _adamw(w, g, m, v, *, name):
    rows, cols = w.shape
    tr = rows
    for cand in (512, 256, 128, 64, 32, 16, 8):
        if rows % cand == 0 and rows > cand:
            tr = cand
            break

    def body(w_ref, g_ref, m_ref, v_ref, d_ref, nm_ref, nv_ref):
        d, nm, nv = _adamw_math(w_ref[...], g_ref[...], m_ref[...], v_ref[...])
        d_ref[...] = d
        nm_ref[...] = nm
        nv_ref[...] = nv

    spec = pl.BlockSpec((tr, cols), lambda i: (i, 0))
    return pl.pallas_call(
        body, name=name, grid=(rows // tr,), in_specs=[spec] * 4, out_specs=[spec] * 3,
        out_shape=[jax.ShapeDtypeStruct((rows, cols), F32)] * 3,
        compiler_params=_params("parallel"),
    )(w, g, m, v)


def _sum_parts(parts, *, name):
    _, rows, cols = parts.shape
    tr = rows
    for cand in (256, 128, 64, 32, 16):
        if rows % cand == 0 and rows > cand:
            tr = cand
            break

    def body(p_ref, o_ref):
        acc = p_ref[0].astype(F32)
        for d in range(1, N_DEV):
            acc = acc + p_ref[d].astype(F32)
        o_ref[...] = acc

    return pl.pallas_call(
        body, name=name, grid=(rows // tr,),
        in_specs=[pl.BlockSpec((N_DEV, tr, cols), lambda i: (0, i, 0))],
        out_specs=pl.BlockSpec((tr, cols), lambda i: (i, 0)),
        out_shape=jax.ShapeDtypeStruct((rows, cols), F32),
        compiler_params=_params("parallel"),
    )(parts)


def _place():
    return lax.axis_index("x"), lax.axis_index("y"), lax.axis_index("c")


def _flip(v, bit):
    return 1 - v if bit else v


N_PEERS = N_DEV - 1


def _peer_copy(k, src_ref, land_ref, send_sem, recv_sem, gather):
    x, y, c = _place()
    my_id = 4 * x + 2 * y + c
    px, py, pc = _flip(x, k & 4), _flip(y, k & 2), _flip(c, k & 1)
    peer_id = 4 * px + 2 * py + pc
    return pltpu.make_async_remote_copy(
        src_ref=src_ref if gather else src_ref.at[peer_id], dst_ref=land_ref.at[my_id],
        send_sem=send_sem.at[k - 1], recv_sem=recv_sem.at[k - 1],
        device_id=(px, py, pc), device_id_type=MESH)


def _sequencer_copies(srcs, *, gather, name, collective_id, after):
    n = len(srcs)
    hbm = pltpu.MemorySpace.HBM
    src_refs = [jax.new_ref(s, memory_space=hbm) for s in srcs]
    land_refs = [jax.empty_ref(jax.ShapeDtypeStruct(((N_DEV,) + s.shape) if gather else s.shape, s.dtype),
                               memory_space=hbm) for s in srcs]
    token_in = jax.new_ref(jnp.zeros((8, 128), F32) if after is None else after, memory_space=hbm)
    token_out = jax.empty_ref(jax.ShapeDtypeStruct((8, 128), F32), memory_space=hbm)

    @pl.kernel(mesh=plsc.ScalarSubcoreMesh(axis_name="seq", num_cores=1), name=name,
               scratch_types=(pltpu.SemaphoreType.DMA((n, N_PEERS)), pltpu.SemaphoreType.DMA((n, N_PEERS)),
                              pltpu.SemaphoreType.DMA((n + 1,))),
               compiler_params=pltpu.CompilerParams(collective_id=collective_id))
    def launch(send_sems, recv_sems, local_sems):
        x, y, c = _place()
        my_id = 4 * x + 2 * y + c
        barrier = pltpu.get_barrier_semaphore()
        for k in range(1, N_DEV):
            peer = (_flip(x, k & 4), _flip(y, k & 2), _flip(c, k & 1))
            pl.semaphore_signal(barrier, inc=1, device_id=peer, device_id_type=MESH)
        pl.semaphore_wait(barrier, N_PEERS)
        own = [pltpu.make_async_copy(src_refs[t] if gather else src_refs[t].at[my_id], land_refs[t].at[my_id],
                                     local_sems.at[t]) for t in range(n)]
        for cp in own:
            cp.start()
        copies = [_peer_copy(k, src_refs[t], land_refs[t], send_sems.at[t], recv_sems.at[t], gather)
                  for t in range(n) for k in range(1, N_DEV)]
        for cp in copies:
            cp.start()
        for cp in own:
            cp.wait()
        for cp in copies:
            cp.wait()
        passed = pltpu.make_async_copy(token_in, token_out, local_sems.at[n])
        passed.start()
        passed.wait()

    launch()
    return [ref[...] for ref in land_refs], token_out[...]


def _all_reduce_small(buf, *, name):
    _, rows, lanes = buf.shape

    def body(in_ref, out_ref, parts, send_sems, recv_sems):
        x, y, c = _place()
        my_id = 4 * x + 2 * y + c
        peers = []
        for k in range(1, N_DEV):
            px, py, pc = _flip(x, k & 4), _flip(y, k & 2), _flip(c, k & 1)
            peers.append(((px, py, pc), 4 * px + 2 * py + pc))
        scatter = [pltpu.make_async_remote_copy(
            src_ref=in_ref.at[pid], dst_ref=parts.at[my_id],
            send_sem=send_sems.at[0, k], recv_sem=recv_sems.at[0, k],
            device_id=peer, device_id_type=MESH) for k, (peer, pid) in enumerate(peers)]
        for cp in scatter:
            cp.start()
        parts[my_id] = in_ref[my_id]
        for cp in scatter:
            cp.wait()
        total = parts[0]
        for d in range(1, N_DEV):
            total = total + parts[d]
        out_ref[my_id] = total
        gather = [pltpu.make_async_remote_copy(
            src_ref=out_ref.at[my_id], dst_ref=out_ref.at[my_id],
            send_sem=send_sems.at[1, k], recv_sem=recv_sems.at[1, k],
            device_id=peer, device_id_type=MESH) for k, (peer, pid) in enumerate(peers)]
        for cp in gather:
            cp.start()
        for k, (peer, pid) in enumerate(peers):
            pltpu.make_async_remote_copy(
                src_ref=out_ref.at[pid], dst_ref=out_ref.at[pid],
                send_sem=send_sems.at[1, k], recv_sem=recv_sems.at[1, k],
                device_id=peer, device_id_type=MESH).wait()

    vmem = pl.BlockSpec(memory_space=pltpu.VMEM)
    return pl.pallas_call(
        body, name=name, in_specs=[vmem], out_specs=vmem,
        out_shape=jax.ShapeDtypeStruct(buf.shape, F32),
        scratch_shapes=[pltpu.VMEM(buf.shape, F32),
                        pltpu.SemaphoreType.DMA((2, 7)), pltpu.SemaphoreType.DMA((2, 7))],
        compiler_params=pltpu.CompilerParams(vmem_limit_bytes=VMEM_LIMIT),
    )(buf)


TM = 512
TMM = 1024
TKW = 2048
MIX_TILE = 128
FFN_TILE = 256
FFN_TC = 512


def _block_diag(w):
    wg = w.reshape(N_GROUPS, HEADS_PER_GROUP, LRU_HEAD_DIM, LRU_HEAD_DIM)
    eye = jnp.eye(HEADS_PER_GROUP, dtype=w.dtype)
    bd = wg[:, :, :, None, :] * eye[None, :, None, :, None]
    return bd.reshape(N_GROUPS, LRU_GROUP, LRU_GROUP).astype(BF16)


def _head_blocks(bd):
    b5 = bd.reshape(N_GROUPS, HEADS_PER_GROUP, LRU_HEAD_DIM, HEADS_PER_GROUP, LRU_HEAD_DIM)
    blocks = [b5[:, h, :, h, :] for h in range(HEADS_PER_GROUP)]
    return jnp.stack(blocks, axis=1).reshape(LRU_HEADS, LRU_HEAD_DIM, LRU_HEAD_DIM)


def _w(lw, key, after):
    value = lw[key]
    return value(after) if callable(value) else value


def _layer_fwd(x, lw, tag):
    sv_rows = x.shape[0]
    h1 = _norm_fwd(x, lw["g1"], tm=TM, name=f"norm1_fwd_{tag}")
    z = _mm_nt(h1, _w(lw, "w_in_t", h1), tm=min(TMM, sv_rows), tn=896, tk=D_MODEL, out_dtype=F32, name=f"in_proj_{tag}")
    y_mix, hs = _mixer_fwd(z, _w(lw, "cw", z), lw["cb"], lw["wa_bd"], lw["wx_bd"], lw["ba"], lw["bx"], lw["lam"],
                           _w(lw, "scw", z), tile=MIX_TILE, name=f"mixer_fwd_{tag}")
    x2 = _mm_nn(y_mix, _w(lw, "w_out", y_mix), tm=min(TMM, sv_rows), tn=D_MODEL, tk=D_MIX, out_dtype=F32, name=f"out_proj_{tag}",
                residual=x)
    h2 = _norm_fwd(x2, lw["g2"], tm=TM, name=f"norm2_fwd_{tag}")
    p = _mm_up(h2, _w(lw, "w_up_b", h2), tm=min(TMM, sv_rows), name=f"up_proj_{tag}")
    act = _ffn_fwd(p, _w(lw, "fcw", p), tile=FFN_TILE, tc=FFN_TC, name=f"ffn_fwd_{tag}")
    x3 = _mm_nn(act, _w(lw, "w_down", act), tm=min(TMM, sv_rows), tn=D_MODEL, tk=D_FF // 2, out_dtype=F32, name=f"down_proj_{tag}",
                residual=x2)
    saved = dict(x=x, h1=h1, z=z, y_mix=y_mix, hs=hs, x2=x2, h2=h2, p=p, act=act)
    return x3, saved


def _layer_bwd(dx3, dx3b, lw, sv, tag, put):
    sv_rows = dx3.shape[0]
    w_in_t, w_out, w_up_b, w_down = (_w(lw, k, dx3) for k in ("w_in_t", "w_out", "w_up_b", "w_down"))
    cw, scw, fcw = (_w(lw, k, dx3) for k in ("cw", "scw", "fcw"))
    dact = _mm_nt(dx3b, w_down, tm=min(TMM, sv_rows), tn=1024, tk=D_MODEL, out_dtype=F32, name=f"down_bwd_x_{tag}")
    g_down = _mm_tn(sv["act"], dx3b, tm=1024, tn=D_MODEL, tk=min(TKW, sv_rows), out_dtype=BF16, name=f"down_bwd_w_{tag}")
    dp, dfcw = _ffn_bwd(sv["p"], dact, fcw, tile=FFN_TILE, tc=FFN_TC, name=f"ffn_bwd_{tag}",
                        token=put("w_down", g_down))
    dh2 = _mm_up_bwd_x(dp, w_up_b, tm=min(TMM, sv_rows), name=f"up_bwd_x_{tag}")
    g_up = _mm_up_bwd_w(sv["h2"], dp, tm=D_MODEL, tk=min(TKW, sv_rows), name=f"up_bwd_w_{tag}")
    dx2, dx2b, dg2 = _norm_bwd(dh2, sv["x2"], lw["g2"], dx3, tm=TM, name=f"norm2_bwd_{tag}",
                               token=put("w_up_b", g_up))
    dy = _mm_nt(dx2b, w_out, tm=min(TMM, sv_rows), tn=768, tk=D_MODEL, out_dtype=F32, name=f"out_bwd_x_{tag}")
    g_out = _mm_tn(sv["y_mix"], dx2b, tm=768, tn=D_MODEL, tk=min(TKW, sv_rows), out_dtype=BF16, name=f"out_bwd_w_{tag}")
    out_token = put("w_out", g_out)
    dz, dcw, dvec, dwa, dwx, dscw = _mixer_bwd(
        sv["z"], sv["hs"], dy, cw, lw["cb"], lw["wa_bd"], lw["wx_bd"], lw["ba"], lw["bx"], lw["lam"],
        scw, tile=MIX_TILE, name=f"mixer_bwd_{tag}", token=out_token)
    dh1 = _mm_nn(dz, w_in_t, tm=min(TMM, sv_rows), tn=D_MODEL, tk=896, out_dtype=F32, name=f"in_bwd_x_{tag}")
    g_in_t = _mm_tn(dz, sv["h1"], tm=896, tn=D_MODEL, tk=min(TKW, sv_rows), out_dtype=BF16, name=f"in_bwd_w_{tag}")
    dx, dxb, dg1 = _norm_bwd(dh1, sv["x"], lw["g1"], dx2, tm=TM, name=f"norm1_bwd_{tag}",
                             token=put("w_in_t", g_in_t))
    small = dict(norm1_g=dg1[0], lru_conv_w=dcw[0:4], lru_conv_b=dvec[0], lru_wa=_head_blocks(dwa),
                 lru_ba=dvec[1], lru_wx=_head_blocks(dwx), lru_bx=dvec[2], lru_lambda=dvec[3],
                 sc_conv_w=dscw[0:3], norm2_g=dg2[0], ffn_conv_w=dfcw[:, 0:3, :])
    return dx, dxb, small


SMALL_ORDER = ("norm1_g", "lru_conv_w", "lru_conv_b", "lru_wa", "lru_ba", "lru_wx", "lru_bx", "lru_lambda",
               "sc_conv_w", "norm2_g", "ffn_conv_w")


def _local_step(x, tgt, layers, final_g, put):
    saved = []
    h = x
    for l in range(DEPTH):
        h, sv = _layer_fwd(h, layers[l], f"l{l}")
        saved.append(sv)
    loss_blk, dx, dxb, dgf = _loss_head(h, final_g, tgt, tm=TM, name="loss_head")
    smalls = [None] * DEPTH
    for l in reversed(range(DEPTH)):
        dx, dxb, smalls[l] = _layer_bwd(dx, dxb, layers[l], saved[l], f"l{l}", functools.partial(put, l))
    return loss_blk[0, 0], dx, smalls, dgf[0]


def kernel(x, norm1_g, w_in, lru_conv_w, lru_conv_b, lru_wa, lru_ba, lru_wx, lru_bx, lru_lambda, sc_conv_w, w_out, norm2_g, w_up, ffn_conv_w, w_down, final_g, loss_target, m_norm1_g, m_w_in, m_lru_conv_w, m_lru_conv_b, m_lru_wa, m_lru_ba, m_lru_wx, m_lru_bx, m_lru_lambda, m_sc_conv_w, m_w_out, m_norm2_g, m_w_up, m_ffn_conv_w, m_w_down, m_final_g, v_norm1_g, v_w_in, v_lru_conv_w, v_lru_conv_b, v_lru_wa, v_lru_ba, v_lru_wx, v_lru_bx, v_lru_lambda, v_sc_conv_w, v_w_out, v_norm2_g, v_w_up, v_ffn_conv_w, v_w_down, v_final_g):
    names = ["norm1_g", "w_in", "lru_conv_w", "lru_conv_b", "lru_wa", "lru_ba", "lru_wx", "lru_bx", "lru_lambda",
             "sc_conv_w", "w_out", "norm2_g", "w_up", "ffn_conv_w", "w_down", "final_g"]
    w = dict(zip(names, [norm1_g, w_in, lru_conv_w, lru_conv_b, lru_wa, lru_ba, lru_wx, lru_bx, lru_lambda,
                         sc_conv_w, w_out, norm2_g, w_up, ffn_conv_w, w_down, final_g]))
    m = dict(zip(names, [m_norm1_g, m_w_in, m_lru_conv_w, m_lru_conv_b, m_lru_wa, m_lru_ba, m_lru_wx, m_lru_bx,
                         m_lru_lambda, m_sc_conv_w, m_w_out, m_norm2_g, m_w_up, m_ffn_conv_w, m_w_down, m_final_g]))
    v = dict(zip(names, [v_norm1_g, v_w_in, v_lru_conv_w, v_lru_conv_b, v_lru_wa, v_lru_ba, v_lru_wx, v_lru_bx,
                         v_lru_lambda, v_sc_conv_w, v_w_out, v_norm2_g, v_w_up, v_ffn_conv_w, v_w_down, v_final_g]))
    my_id = 4 * lax.axis_index("x") + 2 * lax.axis_index("y") + lax.axis_index("c")

    taps = jnp.zeros((DEPTH, 16, 768), F32)
    taps = taps.at[:, 0:4, 0:128].set(lru_conv_w).at[:, 4:7, 0:64].set(sc_conv_w).at[:, 8:11, :].set(ffn_conv_w)
    shards = {}
    for l in range(DEPTH):
        shards[f"w_in_t{l}"] = jnp.swapaxes(w_in[l], 0, 1).astype(BF16)
        if l == 0:
            shards["taps"] = taps.reshape(DEPTH * 16, 768)
        shards[f"w_out{l}"] = w_out[l].astype(BF16)
        shards[f"w_up_b{l}"] = w_up[l].astype(BF16)
        shards[f"w_down{l}"] = w_down[l].astype(BF16)
    ids = iter(range(16))
    got = {}
    chain = [None]
    for group in (("w_in_t0", "taps"), ("w_out0",), ("w_up_b0",), ("w_down0",),
                  ("w_in_t1", "w_out1", "w_up_b1", "w_down1")):
        lands, chain[0] = _sequencer_copies([shards[k] for k in group], gather=True, name=f"gather_{group[0]}",
                                            collective_id=next(ids), after=None)
        got.update(zip(group, lands))

    def fetch(key, after):
        return got[key]

    def tap_rows(l, lo, hi, width, after):
        tl = fetch("taps", after).reshape(N_DEV, DEPTH, 16, 768)[:, l, lo:hi, 0:width]
        return jnp.transpose(tl, (1, 0, 2)).reshape(hi - lo, N_DEV * width)

    layers = []
    for l in range(DEPTH):
        layers.append(dict(
            g1=norm1_g[l], g2=norm2_g[l], cb=lru_conv_b[l], ba=lru_ba[l], bx=lru_bx[l], lam=lru_lambda[l],
            wa_bd=_block_diag(lru_wa[l]), wx_bd=_block_diag(lru_wx[l]),
            cw=functools.partial(tap_rows, l, 0, 4, 128), scw=functools.partial(tap_rows, l, 4, 7, 64),
            fcw=lambda after, l=l: tap_rows(l, 8, 11, 768, after).reshape(3, 2, D_FF).transpose(1, 0, 2),
            w_in_t=lambda after, l=l: fetch(f"w_in_t{l}", after).reshape(D_IN, D_MODEL),
            w_out=lambda after, l=l: fetch(f"w_out{l}", after).reshape(D_MIX, D_MODEL),
            w_up_b=lambda after, l=l: fetch(f"w_up_b{l}", after),
            w_down=lambda after, l=l: fetch(f"w_down{l}", after).reshape(D_FF, D_MODEL)))

    scatter_handles = {}

    def put(l, key, grad):
        blocks = grad if grad.ndim == 3 else grad.reshape(N_DEV, grad.shape[0] // N_DEV, grad.shape[1])
        (scatter_handles[(l, key)],), chain[0] = _sequencer_copies(
            [blocks], gather=False, name=f"scatter_{key}{l}", collective_id=next(ids), after=chain[0])
        return blocks

    loss_local, dx, smalls, dgf = _local_step(x[0], loss_target[0], layers, final_g, put)
    loss = lax.psum(loss_local, ("x", "y", "c"))

    parts = []
    for l in range(DEPTH):
        for key in ("w_in_t", "w_out", "w_up_b", "w_down"):
            parts.append(scatter_handles[(l, key)])

    flat = [smalls[l][k].reshape(-1) for l in range(DEPTH) for k in SMALL_ORDER] + [dgf.reshape(-1)]
    sizes = [f.shape[0] for f in flat]
    total = sum(sizes)
    rows = -(-total // (N_DEV * 128 * 8)) * 8
    flat.append(jnp.zeros((N_DEV * rows * 128 - total,), F32))
    small_sum = _all_reduce_small(jnp.concatenate(flat).reshape(N_DEV, rows, 128), name="reduce_small").reshape(-1)
    small_g, off = [], 0
    for sz in sizes:
        small_g.append(small_sum[off:off + sz])
        off += sz
    gs = {}
    for l in range(DEPTH):
        for i, k in enumerate(SMALL_ORDER):
            gs.setdefault(k, []).append(small_g[l * len(SMALL_ORDER) + i])
    g_final = small_g[-1]

    grads = {}
    per_layer = {k: [] for k in ("w_in", "w_out", "w_up", "w_down")}
    for l in range(DEPTH):
        p_in, p_out, p_up, p_down = parts[4 * l:4 * l + 4]
        per_layer["w_in"].append(jnp.swapaxes(_sum_parts(p_in, name=f"sum_w_in_l{l}"), 0, 1))
        per_layer["w_out"].append(_sum_parts(p_out, name=f"sum_w_out_l{l}"))
        per_layer["w_up"].append(_sum_parts(p_up, name=f"sum_w_up_l{l}"))
        per_layer["w_down"].append(_sum_parts(p_down, name=f"sum_w_down_l{l}"))
    for k, lst in per_layer.items():
        grads[k] = jnp.stack(lst)
    for k in ("norm1_g", "lru_conv_b", "lru_ba", "lru_bx", "lru_lambda", "norm2_g"):
        grads[k] = jnp.stack(gs[k]).reshape(DEPTH, -1)
    for k in ("lru_wa", "lru_wx"):
        grads[k] = jnp.stack(gs[k]).reshape(DEPTH, LRU_HEADS, LRU_HEAD_DIM, LRU_HEAD_DIM)
    grads["final_g"] = g_final
    cw_full = jnp.stack(gs["lru_conv_w"]).reshape(DEPTH, 4, N_DEV, 128)
    grads["lru_conv_w"] = lax.dynamic_index_in_dim(cw_full, my_id, axis=2, keepdims=False)
    scw_full = jnp.stack(gs["sc_conv_w"]).reshape(DEPTH, 3, N_DEV, 64)
    grads["sc_conv_w"] = lax.dynamic_index_in_dim(scw_full, my_id, axis=2, keepdims=False)
    fcw_full = jnp.stack(gs["ffn_conv_w"]).reshape(DEPTH, 2, 3, D_FF).transpose(0, 2, 1, 3).reshape(DEPTH, 3, N_DEV, 768)
    grads["ffn_conv_w"] = lax.dynamic_index_in_dim(fcw_full, my_id, axis=2, keepdims=False)

    deltas, new_m, new_v = {}, {}, {}
    for k in names:
        shape = w[k].shape
        cols = shape[-1]
        as2d = lambda a: a.reshape(-1, cols)
        d, nm, nv = _adamw(as2d(w[k]), as2d(grads[k]), as2d(m[k]), as2d(v[k]), name=f"adamw_{k}")
        deltas[k], new_m[k], new_v[k] = d.reshape(shape), nm.reshape(shape), nv.reshape(shape)

    return (loss, dx[None], *[grads[k] for k in names], *[deltas[k] for k in names],
            *[new_m[k] for k in names], *[new_v[k] for k in names])
```

```python
import functools
import math

import jax
import jax.numpy as jnp
from jax import lax
from jax.experimental import pallas as pl
from jax.experimental.pallas import tpu as pltpu
from jax.experimental.pallas import tpu_sc as plsc

F32 = jnp.float32
BF16 = jnp.bfloat16

N_DEV = 8
DEPTH = 2
D_MODEL = 1024
D_LRU = 1024
D_SC = 512
D_MIX = D_LRU + D_SC
D_IN = 2 * D_LRU + 3 * D_SC
D_FF = 3072
LRU_HEADS = 16
LRU_HEAD_DIM = 64
LRU_GROUP = 256
N_GROUPS = D_LRU // LRU_GROUP
HEADS_PER_GROUP = LRU_GROUP // LRU_HEAD_DIM
RG_C = 8.0
EPS = 1e-6
HALO = 8

ADAM_LR = 0.001
ADAM_B1 = 0.9
ADAM_B2 = 0.999
ADAM_EPS = 1e-08
ADAM_WD = 0.01
ADAM_STEP = 10

GELU_C = math.sqrt(2.0 / math.pi)
GELU_A = 0.044715

VMEM_LIMIT = 56 * 1024 * 1024
MESH = pl.DeviceIdType.MESH


def _params(*sem):
    return pltpu.CompilerParams(dimension_semantics=tuple(sem) if sem else None,
                                vmem_limit_bytes=VMEM_LIMIT)


def _gelu_parts(x):
    x2 = x * x
    t = jnp.tanh(GELU_C * (x + GELU_A * x * x2))
    half = 0.5 * (1.0 + t)
    g = x * half
    dg = half + 0.5 * x * (1.0 - t * t) * (GELU_C * (1.0 + 3.0 * GELU_A * x2))
    return g, dg


def _gelu(x):
    t = jnp.tanh(GELU_C * (x + GELU_A * x * x * x))
    return 0.5 * x * (1.0 + t)


def _sigmoid(x):
    return 1.0 / (1.0 + jnp.exp(-x))


def _softplus(x):
    e = jnp.exp(-jnp.abs(x))
    u = 1.0 + e
    log1p_e = jnp.where(u == 1.0, e, jnp.log(u) * (e / (u - 1.0)))
    return jnp.maximum(x, 0.0) + log1p_e


def _rms(x):
    ms = jnp.mean(x * x, axis=-1, keepdims=True)
    return lax.rsqrt(ms + EPS)


def _dot(a, b, dims):
    return lax.dot_general(a, b, (dims, ((), ())), preferred_element_type=F32)


NN = ((1,), (0,))
NT = ((1,), (1,))
TN = ((0,), (0,))


def _matmul(a, b, *, dims, grid, a_spec, b_spec, o_spec, out_shape, acc_shape, name,
            residual=None, r_spec=None):
    nk = grid[2]

    def body(*refs):
        a_ref, b_ref = refs[0], refs[1]
        r_ref = refs[2] if residual is not None else None
        o_ref = refs[3] if residual is not None else refs[2]
        prod = _dot(a_ref[...].astype(BF16), b_ref[...].astype(BF16), dims)

        def finish(total):
            if r_ref is not None:
                total = total + r_ref[...]
            o_ref[...] = total.astype(o_ref.dtype)

        if nk == 1:
            finish(prod)
            return
        acc_ref = refs[-1]
        k = pl.program_id(2)

        @pl.when(k == 0)
        def _():
            acc_ref[...] = prod

        @pl.when(jnp.logical_and(k > 0, k < nk - 1))
        def _():
            acc_ref[...] += prod

        @pl.when(k == nk - 1)
        def _():
            finish(acc_ref[...] + prod)

    in_specs = [a_spec, b_spec]
    args = [a, b]
    if residual is not None:
        in_specs.append(r_spec)
        args.append(residual)
    return pl.pallas_call(
        body, name=name, grid=grid, in_specs=in_specs, out_specs=o_spec, out_shape=out_shape,
        scratch_shapes=[pltpu.VMEM(acc_shape, F32)] if nk > 1 else [],
        compiler_params=_params("parallel", "parallel", "arbitrary"),
    )(*args)


def _mm_nn(a, b, *, tm, tn, tk, out_dtype, name, residual=None):
    m, kd = a.shape
    n = b.shape[1]
    return _matmul(
        a, b, dims=NN, grid=(m // tm, n // tn, kd // tk),
        a_spec=pl.BlockSpec((tm, tk), lambda i, j, k: (i, k)),
        b_spec=pl.BlockSpec((tk, tn), lambda i, j, k: (k, j)),
        o_spec=pl.BlockSpec((tm, tn), lambda i, j, k: (i, j)),
        out_shape=jax.ShapeDtypeStruct((m, n), out_dtype), acc_shape=(tm, tn), name=name,
        residual=residual, r_spec=pl.BlockSpec((tm, tn), lambda i, j, k: (i, j)))


def _mm_nt(a, b, *, tm, tn, tk, out_dtype, name):
    m, kd = a.shape
    n = b.shape[0]
    return _matmul(
        a, b, dims=NT, grid=(m // tm, n // tn, kd // tk),
        a_spec=pl.BlockSpec((tm, tk), lambda i, j, k: (i, k)),
        b_spec=pl.BlockSpec((tn, tk), lambda i, j, k: (j, k)),
        o_spec=pl.BlockSpec((tm, tn), lambda i, j, k: (i, j)),
        out_shape=jax.ShapeDtypeStruct((m, n), out_dtype), acc_shape=(tm, tn), name=name)


def _mm_tn(a, b, *, tm, tn, tk, out_dtype, name):
    kd, m = a.shape
    n = b.shape[1]
    return _matmul(
        a, b, dims=TN, grid=(m // tm, n // tn, kd // tk),
        a_spec=pl.BlockSpec((tk, tm), lambda i, j, k: (k, i)),
        b_spec=pl.BlockSpec((tk, tn), lambda i, j, k: (k, j)),
        o_spec=pl.BlockSpec((tm, tn), lambda i, j, k: (i, j)),
        out_shape=jax.ShapeDtypeStruct((m, n), out_dtype), acc_shape=(tm, tn), name=name)


def _mm_up(h2, w_up_b, *, tm, name):
    s = h2.shape[0]
    nb = w_up_b.shape[2]
    per_half = D_FF // nb
    return _matmul(
        h2, w_up_b, dims=NN, grid=(s // tm, N_DEV, 1),
        a_spec=pl.BlockSpec((tm, D_MODEL), lambda i, j, k: (i, 0)),
        b_spec=pl.BlockSpec((None, D_MODEL, nb), lambda i, j, k: (j, 0, 0)),
        o_spec=pl.BlockSpec((None, tm, nb), lambda i, j, k: (j // per_half, i, j % per_half)),
        out_shape=jax.ShapeDtypeStruct((2, s, D_FF), F32), acc_shape=(tm, nb), name=name)


def _mm_up_bwd_x(dp, w_up_b, *, tm, name):
    s = dp.shape[1]
    nb = w_up_b.shape[2]
    per_half = D_FF // nb
    return _matmul(
        dp, w_up_b, dims=NT, grid=(s // tm, 1, N_DEV),
        a_spec=pl.BlockSpec((None, tm, nb), lambda i, j, k: (k // per_half, i, k % per_half)),
        b_spec=pl.BlockSpec((None, D_MODEL, nb), lambda i, j, k: (k, 0, 0)),
        o_spec=pl.BlockSpec((tm, D_MODEL), lambda i, j, k: (i, 0)),
        out_shape=jax.ShapeDtypeStruct((s, D_MODEL), F32), acc_shape=(tm, D_MODEL), name=name)


def _mm_up_bwd_w(h2, dp, *, tm, tk, name):
    s = h2.shape[0]
    nb = D_FF * 2 // N_DEV
    per_half = D_FF // nb
    return _matmul(
        h2, dp, dims=TN, grid=(D_MODEL // tm, N_DEV, s // tk),
        a_spec=pl.BlockSpec((tk, tm), lambda i, j, k: (k, i)),
        b_spec=pl.BlockSpec((None, tk, nb), lambda i, j, k: (j // per_half, k, j % per_half)),
        o_spec=pl.BlockSpec((None, tm, nb), lambda i, j, k: (j, i, 0)),
        out_shape=jax.ShapeDtypeStruct((N_DEV, D_MODEL, nb), BF16), acc_shape=(tm, nb), name=name)


def _norm_fwd(x, g, *, tm, name):
    s, d = x.shape

    def body(x_ref, g_ref, h_ref):
        xv = x_ref[...]
        h_ref[...] = (xv * _rms(xv) * g_ref[...]).astype(BF16)

    return pl.pallas_call(
        body, name=name, grid=(s // tm,),
        in_specs=[pl.BlockSpec((tm, d), lambda i: (i, 0)), pl.BlockSpec((1, d), lambda i: (0, 0))],
        out_specs=pl.BlockSpec((tm, d), lambda i: (i, 0)),
        out_shape=jax.ShapeDtypeStruct((s, d), BF16),
        compiler_params=_params("parallel"),
    )(x, g.reshape(1, d))


def _behind(token):
    return jnp.zeros((8, 128), F32) if token is None else token


def _norm_bwd(dh, x, g, dres, *, tm, name, token=None):
    s, d = x.shape

    def body(dh_ref, x_ref, g_ref, dres_ref, token_ref, dx_ref, dxb_ref, dg_ref):
        @pl.when(pl.program_id(0) == 0)
        def _():
            dg_ref[...] = jnp.zeros_like(dg_ref)

        xv = x_ref[...]
        rstd = _rms(xv)
        n = xv * rstd
        dhv = dh_ref[...]
        dn = dhv * g_ref[...]
        dx = rstd * (dn - n * jnp.mean(dn * n, axis=-1, keepdims=True))
        dx = dres_ref[...] + dx
        dx_ref[...] = dx
        dxb_ref[...] = dx.astype(BF16)
        dg_ref[0:1, :] += jnp.sum(dhv * n, axis=0, keepdims=True)

    return pl.pallas_call(
        body, name=name, grid=(s // tm,),
        in_specs=[pl.BlockSpec((tm, d), lambda i: (i, 0)), pl.BlockSpec((tm, d), lambda i: (i, 0)),
                  pl.BlockSpec((1, d), lambda i: (0, 0)), pl.BlockSpec((tm, d), lambda i: (i, 0)),
                  pl.BlockSpec(memory_space=pl.ANY)],
        out_specs=[pl.BlockSpec((tm, d), lambda i: (i, 0)), pl.BlockSpec((tm, d), lambda i: (i, 0)),
                   pl.BlockSpec((8, d), lambda i: (0, 0))],
        out_shape=[jax.ShapeDtypeStruct((s, d), F32), jax.ShapeDtypeStruct((s, d), BF16),
                   jax.ShapeDtypeStruct((8, d), F32)],
        compiler_params=_params("arbitrary"),
    )(dh, x, g.reshape(1, d), dres, _behind(token))


def _loss_head(x, g, tgt, *, tm, name):
    s, d = x.shape

    def body(x_ref, g_ref, t_ref, loss_ref, dx_ref, dxb_ref, dg_ref):
        @pl.when(pl.program_id(0) == 0)
        def _():
            dg_ref[...] = jnp.zeros_like(dg_ref)
            loss_ref[...] = jnp.zeros_like(loss_ref)

        xv = x_ref[...]
        gv = g_ref[...]
        rstd = _rms(xv)
        n = xv * rstd
        e = n * gv - t_ref[...]
        part = 0.5 * jnp.sum(jnp.mean(e * e, axis=-1, keepdims=True), axis=0, keepdims=True)
        loss_ref[...] += jnp.broadcast_to(part, loss_ref.shape)
        dy = e * (1.0 / d)
        dn = dy * gv
        dx = rstd * (dn - n * jnp.mean(dn * n, axis=-1, keepdims=True))
        dx_ref[...] = dx
        dxb_ref[...] = dx.astype(BF16)
        dg_ref[0:1, :] += jnp.sum(dy * n, axis=0, keepdims=True)

    return pl.pallas_call(
        body, name=name, grid=(s // tm,),
        in_specs=[pl.BlockSpec((tm, d), lambda i: (i, 0)), pl.BlockSpec((1, d), lambda i: (0, 0)),
                  pl.BlockSpec((tm, d), lambda i: (i, 0))],
        out_specs=[pl.BlockSpec((8, 128), lambda i: (0, 0)), pl.BlockSpec((tm, d), lambda i: (i, 0)),
                   pl.BlockSpec((tm, d), lambda i: (i, 0)), pl.BlockSpec((8, d), lambda i: (0, 0))],
        out_shape=[jax.ShapeDtypeStruct((8, 128), F32), jax.ShapeDtypeStruct((s, d), F32),
                   jax.ShapeDtypeStruct((s, d), BF16), jax.ShapeDtypeStruct((8, d), F32)],
        compiler_params=_params("arbitrary"),
    )(x, g.reshape(1, d), tgt)


def _scan_rows(a_ref, b_ref, h_ref, carry, *, rows, reverse):
    width = a_ref.shape[1]
    n_chunks = rows // 8
    row = lax.broadcasted_iota(jnp.int32, (8, width), 0)

    def step(ci, carry):
        chunk = (n_chunks - 1 - ci) if reverse else ci
        off = pl.multiple_of(chunk * 8, 8)
        av = a_ref[pl.ds(off, 8), :]
        bv = b_ref[pl.ds(off, 8), :]
        for sh in (1, 2, 4):
            if reverse:
                a_sh = pltpu.roll(av, 8 - sh, 0)
                b_sh = pltpu.roll(bv, 8 - sh, 0)
                m = row < 8 - sh
            else:
                a_sh = pltpu.roll(av, sh, 0)
                b_sh = pltpu.roll(bv, sh, 0)
                m = row >= sh
            bv = jnp.where(m, av * b_sh + bv, bv)
            av = jnp.where(m, av * a_sh, av)
        h = av * carry + bv
        h_ref[pl.ds(off, 8), :] = h
        return h[0:1, :] if reverse else h[7:8, :]

    return lax.fori_loop(0, n_chunks, step, carry)


def _lru_gates(lx, wa_ref, wx_ref, ba, bx, sp):
    lxb = lx.astype(BF16)
    pre_r = jnp.concatenate(
        [_dot(lxb[:, g * LRU_GROUP:(g + 1) * LRU_GROUP], wa_ref[g], NN) for g in range(N_GROUPS)], axis=1)
    pre_i = jnp.concatenate(
        [_dot(lxb[:, g * LRU_GROUP:(g + 1) * LRU_GROUP], wx_ref[g], NN) for g in range(N_GROUPS)], axis=1)
    r = _sigmoid(pre_r + ba)
    ig = _sigmoid(pre_i + bx)
    log_a = (-RG_C * r) * sp
    a = jnp.exp(log_a)
    mult = jnp.sqrt(-jnp.tanh(log_a) * (a * a + 1.0))
    return lxb, r, ig, a, mult


def _mixer_fwd(z, cw, cb, wa_bd, wx_bd, ba, bx, lam, scw, *, tile, name):
    s = z.shape[0]
    n_tiles = s // tile

    def body(z_ref, cw_ref, cb_ref, wa_ref, wx_ref, ba_ref, bx_ref, lam_ref, scw_ref,
             y_ref, hs_ref, ext_lx, ext_q, a_s, b_s, h_car):
        i = pl.program_id(0)

        @pl.when(i == 0)
        def _():
            ext_lx[0:HALO, :] = jnp.zeros((HALO, D_LRU), F32)
            ext_q[0:HALO, :] = jnp.zeros((HALO, D_SC), F32)
            h_car[...] = jnp.zeros_like(h_car)

        ext_lx[HALO:HALO + tile, :] = z_ref[:, 0:D_LRU]
        ext_q[HALO:HALO + tile, :] = z_ref[:, 2 * D_LRU + D_SC:2 * D_LRU + 2 * D_SC] * z_ref[:, 2 * D_LRU + 2 * D_SC:D_IN]
        lx = cb_ref[...] + cw_ref[0:1, :] * ext_lx[pl.ds(HALO - 3, tile), :]
        for k in range(1, 4):
            lx = lx + cw_ref[k:k + 1, :] * ext_lx[pl.ds(HALO - 3 + k, tile), :]
        cq = scw_ref[0:1, :] * ext_q[pl.ds(HALO - 2, tile), :]
        for k in range(1, 3):
            cq = cq + scw_ref[k:k + 1, :] * ext_q[pl.ds(HALO - 2 + k, tile), :]
        ext_lx[0:HALO, :] = ext_lx[tile:tile + HALO, :]
        ext_q[0:HALO, :] = ext_q[tile:tile + HALO, :]

        sp = _softplus(-lam_ref[...])
        _, _, ig, a, mult = _lru_gates(lx, wa_ref, wx_ref, ba_ref[...], bx_ref[...], sp)
        a_s[...] = a
        b_s[...] = mult * (ig * lx)
        h_car[0:1, :] = _scan_rows(a_s, b_s, hs_ref, h_car[0:1, :], rows=tile, reverse=False)

        y_ref[:, 0:D_LRU] = (hs_ref[...] * _gelu(z_ref[:, D_LRU:2 * D_LRU])).astype(BF16)
        y_ref[:, D_LRU:D_MIX] = (z_ref[:, 2 * D_LRU:2 * D_LRU + D_SC] * cq).astype(BF16)

    full = lambda shape: pl.BlockSpec(shape, lambda i: (0,) * len(shape))
    return pl.pallas_call(
        body, name=name, grid=(n_tiles,),
        in_specs=[pl.BlockSpec((tile, D_IN), lambda i: (i, 0)),
                  full((4, D_LRU)), full((1, D_LRU)),
                  full((N_GROUPS, LRU_GROUP, LRU_GROUP)), full((N_GROUPS, LRU_GROUP, LRU_GROUP)),
                  full((1, D_LRU)), full((1, D_LRU)), full((1, D_LRU)), full((3, D_SC))],
        out_specs=[pl.BlockSpec((tile, D_MIX), lambda i: (i, 0)), pl.BlockSpec((tile, D_LRU), lambda i: (i, 0))],
        out_shape=[jax.ShapeDtypeStruct((s, D_MIX), BF16), jax.ShapeDtypeStruct((s, D_LRU), F32)],
        scratch_shapes=[pltpu.VMEM((tile + HALO, D_LRU), F32), pltpu.VMEM((tile + HALO, D_SC), F32),
                        pltpu.VMEM((tile, D_LRU), F32), pltpu.VMEM((tile, D_LRU), F32),
                        pltpu.VMEM((8, D_LRU), F32)],
        compiler_params=_params("arbitrary"),
    )(z, cw, cb.reshape(1, -1), wa_bd, wx_bd, ba.reshape(1, -1), bx.reshape(1, -1), lam.reshape(1, -1), scw)


def _mixer_bwd(z, hs, dy, cw, cb, wa_bd, wx_bd, ba, bx, lam, scw, *, tile, name, token=None):
    s = z.shape[0]
    n_tiles = s // tile
    per8 = tile // 8

    def body(z_ref, zp_ref, hs_ref, hsp_ref, dy_ref, cw_ref, cb_ref, wa_ref, wx_ref, ba_ref, bx_ref, lam_ref, scw_ref,
             token_ref, dz_ref, dcw_ref, dvec_ref, dwa_ref, dwx_ref, dscw_ref,
             ext_lx, ext_q, ext_h, ext_a, ext_dlx, ext_dcq, a_s, b_s, lam_s, l_car):
        i = pl.program_id(0)
        first_tile = i == n_tiles - 1

        @pl.when(i == 0)
        def _():
            for ref in (dcw_ref, dvec_ref, dwa_ref, dwx_ref, dscw_ref, l_car):
                ref[...] = jnp.zeros_like(ref)
            ext_a[tile:tile + HALO, :] = jnp.zeros((HALO, D_LRU), F32)
            ext_dlx[tile:tile + HALO, :] = jnp.zeros((HALO, D_LRU), F32)
            ext_dcq[tile:tile + HALO, :] = jnp.zeros((HALO, D_SC), F32)

        keep = jnp.where(first_tile, 0.0, 1.0)
        sb = z_ref[:, 2 * D_LRU:2 * D_LRU + D_SC]
        sc = z_ref[:, 2 * D_LRU + D_SC:2 * D_LRU + 2 * D_SC]
        sx = z_ref[:, 2 * D_LRU + 2 * D_SC:D_IN]
        ext_lx[0:HALO, :] = zp_ref[:, 0:D_LRU] * keep
        ext_lx[HALO:HALO + tile, :] = z_ref[:, 0:D_LRU]
        ext_q[0:HALO, :] = zp_ref[:, 2 * D_LRU + D_SC:2 * D_LRU + 2 * D_SC] * zp_ref[:, 2 * D_LRU + 2 * D_SC:D_IN] * keep
        ext_q[HALO:HALO + tile, :] = sc * sx
        ext_h[0:HALO, :] = hsp_ref[...] * keep
        ext_h[HALO:HALO + tile, :] = hs_ref[...]

        lx = cb_ref[...] + cw_ref[0:1, :] * ext_lx[pl.ds(HALO - 3, tile), :]
        for k in range(1, 4):
            lx = lx + cw_ref[k:k + 1, :] * ext_lx[pl.ds(HALO - 3 + k, tile), :]
        cq = scw_ref[0:1, :] * ext_q[pl.ds(HALO - 2, tile), :]
        for k in range(1, 3):
            cq = cq + scw_ref[k:k + 1, :] * ext_q[pl.ds(HALO - 2 + k, tile), :]

        sp = _softplus(-lam_ref[...])
        lxb, r, ig, a, mult = _lru_gates(lx, wa_ref, wx_ref, ba_ref[...], bx_ref[...], sp)

        ge, dge = _gelu_parts(z_ref[:, D_LRU:2 * D_LRU])
        dy_lru = dy_ref[:, 0:D_LRU]
        dz_ref[:, D_LRU:2 * D_LRU] = (dy_lru * hs_ref[...] * dge).astype(BF16)

        ext_a[0:tile, :] = a
        a_s[...] = ext_a[pl.ds(1, tile), :]
        b_s[...] = dy_lru * ge
        l_car[0:1, :] = _scan_rows(a_s, b_s, lam_s, l_car[0:1, :], rows=tile, reverse=True)
        ext_a[tile:tile + HALO, :] = ext_a[0:HALO, :]
        lv = lam_s[...]

        da = lv * ext_h[pl.ds(HALO - 1, tile), :]
        d_mult = lv * ig * lx
        d_i = lv * mult * lx
        dlx = lv * mult * ig
        dlog_a = da * a - d_mult * (a * a) / mult
        d_r = dlog_a * (-RG_C * sp)
        dpre_r = d_r * r * (1.0 - r)
        dpre_i = d_i * ig * (1.0 - ig)
        dvec_ref[1:2, :] += jnp.sum(dpre_r, axis=0, keepdims=True)
        dvec_ref[2:3, :] += jnp.sum(dpre_i, axis=0, keepdims=True)
        dvec_ref[3:4, :] += jnp.sum(dlog_a * (-RG_C * r), axis=0, keepdims=True)
        dpr_b = dpre_r.astype(BF16)
        dpi_b = dpre_i.astype(BF16)
        back = []
        for g in range(N_GROUPS):
            cols = slice(g * LRU_GROUP, (g + 1) * LRU_GROUP)
            dwa_ref[g] += _dot(lxb[:, cols], dpr_b[:, cols], TN)
            dwx_ref[g] += _dot(lxb[:, cols], dpi_b[:, cols], TN)
            back.append(_dot(dpr_b[:, cols], wa_ref[g], NT) + _dot(dpi_b[:, cols], wx_ref[g], NT))
        dlx = dlx + jnp.concatenate(back, axis=1)
        dvec_ref[0:1, :] += jnp.sum(dlx, axis=0, keepdims=True)

        ext_dlx[0:tile, :] = dlx
        for k in range(4):
            dcw_ref[k:k + 1, :] += jnp.sum(dlx * ext_lx[pl.ds(HALO - 3 + k, tile), :], axis=0, keepdims=True)
        dlxp = cw_ref[3:4, :] * dlx
        for k in range(3):
            dlxp = dlxp + cw_ref[k:k + 1, :] * ext_dlx[pl.ds(3 - k, tile), :]
        dz_ref[:, 0:D_LRU] = dlxp.astype(BF16)
        ext_dlx[tile:tile + HALO, :] = ext_dlx[0:HALO, :]

        dy_sc = dy_ref[:, D_LRU:D_MIX]
        dz_ref[:, 2 * D_LRU:2 * D_LRU + D_SC] = (dy_sc * cq).astype(BF16)
        dcq = dy_sc * sb
        ext_dcq[0:tile, :] = dcq
        for k in range(3):
            dscw_ref[k:k + 1, :] += jnp.sum(dcq * ext_q[pl.ds(HALO - 2 + k, tile), :], axis=0, keepdims=True)
        dq = scw_ref[2:3, :] * dcq
        for k in range(2):
            dq = dq + scw_ref[k:k + 1, :] * ext_dcq[pl.ds(2 - k, tile), :]
        dz_ref[:, 2 * D_LRU + D_SC:2 * D_LRU + 2 * D_SC] = (dq * sx).astype(BF16)
        dz_ref[:, 2 * D_LRU + 2 * D_SC:D_IN] = (dq * sc).astype(BF16)
        ext_dcq[tile:tile + HALO, :] = ext_dcq[0:HALO, :]

        @pl.when(i == n_tiles - 1)
        def _():
            dvec_ref[3:4, :] = dvec_ref[3:4, :] * (-_sigmoid(-lam_ref[...]))

    rev = lambda i: n_tiles - 1 - i
    prev8 = lambda i: jnp.maximum(rev(i) * per8 - 1, 0)
    full = lambda shape: pl.BlockSpec(shape, lambda i: (0,) * len(shape))
    return pl.pallas_call(
        body, name=name, grid=(n_tiles,),
        in_specs=[pl.BlockSpec((tile, D_IN), lambda i: (rev(i), 0)),
                  pl.BlockSpec((HALO, D_IN), lambda i: (prev8(i), 0)),
                  pl.BlockSpec((tile, D_LRU), lambda i: (rev(i), 0)),
                  pl.BlockSpec((HALO, D_LRU), lambda i: (prev8(i), 0)),
                  pl.BlockSpec((tile, D_MIX), lambda i: (rev(i), 0)),
                  full((4, D_LRU)), full((1, D_LRU)),
                  full((N_GROUPS, LRU_GROUP, LRU_GROUP)), full((N_GROUPS, LRU_GROUP, LRU_GROUP)),
                  full((1, D_LRU)), full((1, D_LRU)), full((1, D_LRU)), full((3, D_SC)),
                  pl.BlockSpec(memory_space=pl.ANY)],
        out_specs=[pl.BlockSpec((tile, D_IN), lambda i: (rev(i), 0)),
                   full((8, D_LRU)), full((8, D_LRU)),
                   full((N_GROUPS, LRU_GROUP, LRU_GROUP)), full((N_GROUPS, LRU_GROUP, LRU_GROUP)),
                   full((8, D_SC))],
        out_shape=[jax.ShapeDtypeStruct((s, D_IN), BF16),
                   jax.ShapeDtypeStruct((8, D_LRU), F32), jax.ShapeDtypeStruct((8, D_LRU), F32),
                   jax.ShapeDtypeStruct((N_GROUPS, LRU_GROUP, LRU_GROUP), F32),
                   jax.ShapeDtypeStruct((N_GROUPS, LRU_GROUP, LRU_GROUP), F32),
                   jax.ShapeDtypeStruct((8, D_SC), F32)],
        scratch_shapes=[pltpu.VMEM((tile + HALO, D_LRU), F32), pltpu.VMEM((tile + HALO, D_SC), F32),
                        pltpu.VMEM((tile + HALO, D_LRU), F32), pltpu.VMEM((tile + HALO, D_LRU), F32),
                        pltpu.VMEM((tile + HALO, D_LRU), F32), pltpu.VMEM((tile + HALO, D_SC), F32),
                        pltpu.VMEM((tile, D_LRU), F32), pltpu.VMEM((tile, D_LRU), F32),
                        pltpu.VMEM((tile, D_LRU), F32), pltpu.VMEM((8, D_LRU), F32)],
        compiler_params=_params("arbitrary"),
    )(z, z, hs, hs, dy, cw, cb.reshape(1, -1), wa_bd, wx_bd, ba.reshape(1, -1), bx.reshape(1, -1),
      lam.reshape(1, -1), scw, _behind(token))


FFN_ROWS = 16


def _conv3_rows(w_ref, ext_ref, half, row, rows):
    acc = w_ref[half, 0:1, :] * ext_ref[half, pl.ds(row - 2, rows), :]
    for k in range(1, 3):
        acc = acc + w_ref[half, k:k + 1, :] * ext_ref[half, pl.ds(row - 2 + k, rows), :]
    return acc


def _ffn_fwd(p, fcw, *, tile, tc, name):
    s = p.shape[1]
    per8 = tile // 8

    def body(p_ref, pp_ref, w_ref, act_ref, ext_p):
        i = pl.program_id(0)
        keep = jnp.where(i == 0, 0.0, 1.0)
        ext_p[:, 0:HALO, :] = pp_ref[...] * keep
        ext_p[:, HALO:HALO + tile, :] = p_ref[...]
        for r0 in range(0, tile, FFN_ROWS):
            u = [_conv3_rows(w_ref, ext_p, half, HALO + r0, FFN_ROWS) for half in range(2)]
            act_ref[r0:r0 + FFN_ROWS, :] = (_gelu(u[0]) * u[1]).astype(BF16)

    return pl.pallas_call(
        body, name=name, grid=(s // tile, D_FF // tc),
        in_specs=[pl.BlockSpec((2, tile, tc), lambda i, j: (0, i, j)),
                  pl.BlockSpec((2, HALO, tc), lambda i, j: (0, jnp.maximum(i * per8 - 1, 0), j)),
                  pl.BlockSpec((2, 3, tc), lambda i, j: (0, 0, j))],
        out_specs=pl.BlockSpec((tile, tc), lambda i, j: (i, j)),
        out_shape=jax.ShapeDtypeStruct((s, D_FF), BF16),
        scratch_shapes=[pltpu.VMEM((2, tile + HALO, tc), F32)],
        compiler_params=_params("parallel", "parallel"),
    )(p, p, fcw)


def _ffn_bwd(p, dact, fcw, *, tile, tc, name, token=None):
    s = p.shape[1]
    n_tiles = s // tile
    per8 = tile // 8
    ext_rows = tile + HALO

    def body(p_ref, pp_ref, pn_ref, da_ref, dan_ref, w_ref, token_ref, dp_ref, dw_ref, ext_p, ext_du, acc_w):
        i = pl.program_id(1)

        @pl.when(i == 0)
        def _():
            acc_w[...] = jnp.zeros_like(acc_w)

        keep_prev = jnp.where(i == 0, 0.0, 1.0)
        keep_next = jnp.where(i == n_tiles - 1, 0.0, 1.0)
        ext_p[:, 0:HALO, :] = pp_ref[...] * keep_prev
        ext_p[:, HALO:HALO + tile, :] = p_ref[...]
        ext_p[:, HALO + tile:2 * HALO + tile, :] = pn_ref[...] * keep_next

        def du_rows(r0, rows, da):
            u = [_conv3_rows(w_ref, ext_p, half, HALO + r0, rows) for half in range(2)]
            ge, dge = _gelu_parts(u[0])
            du = (da * u[1] * dge, da * ge)
            for half in range(2):
                ext_du[half, r0:r0 + rows, :] = du[half]
            return du

        du_rows(tile, HALO, dan_ref[...] * keep_next)
        for r0 in range(tile - FFN_ROWS, -1, -FFN_ROWS):
            du = du_rows(r0, FFN_ROWS, da_ref[r0:r0 + FFN_ROWS, :])
            for half in range(2):
                acc = w_ref[half, 2:3, :] * du[half]
                for k in range(2):
                    acc = acc + w_ref[half, k:k + 1, :] * ext_du[half, pl.ds(r0 + 2 - k, FFN_ROWS), :]
                dp_ref[half, r0:r0 + FFN_ROWS, :] = acc.astype(BF16)
                for k in range(3):
                    prod = du[half] * ext_p[half, pl.ds(HALO + r0 - 2 + k, FFN_ROWS), :]
                    acc_w[half, k] += sum(prod[q:q + 8] for q in range(0, FFN_ROWS, 8))

        @pl.when(i == n_tiles - 1)
        def _():
            dw_ref[...] = jnp.zeros_like(dw_ref)
            for half in range(2):
                for k in range(3):
                    dw_ref[half, k:k + 1, :] = jnp.sum(acc_w[half, k], axis=0, keepdims=True)

    return pl.pallas_call(
        body, name=name, grid=(D_FF // tc, n_tiles),
        in_specs=[pl.BlockSpec((2, tile, tc), lambda j, i: (0, i, j)),
                  pl.BlockSpec((2, HALO, tc), lambda j, i: (0, jnp.maximum(i * per8 - 1, 0), j)),
                  pl.BlockSpec((2, HALO, tc), lambda j, i: (0, jnp.minimum((i + 1) * per8, n_tiles * per8 - 1), j)),
                  pl.BlockSpec((tile, tc), lambda j, i: (i, j)),
                  pl.BlockSpec((HALO, tc), lambda j, i: (jnp.minimum((i + 1) * per8, n_tiles * per8 - 1), j)),
                  pl.BlockSpec((2, 3, tc), lambda j, i: (0, 0, j)), pl.BlockSpec(memory_space=pl.ANY)],
        out_specs=[pl.BlockSpec((2, tile, tc), lambda j, i: (0, i, j)),
                   pl.BlockSpec((2, 8, tc), lambda j, i: (0, 0, j))],
        out_shape=[jax.ShapeDtypeStruct((2, s, D_FF), BF16), jax.ShapeDtypeStruct((2, 8, D_FF), F32)],
        scratch_shapes=[pltpu.VMEM((2, tile + 2 * HALO, tc), F32), pltpu.VMEM((2, ext_rows, tc), F32),
                        pltpu.VMEM((2, 3, 8, tc), F32)],
        compiler_params=_params("parallel", "arbitrary"),
    )(p, p, p, dact, dact, fcw, _behind(token))


def _adamw_math(w, g, m, v):
    m = ADAM_B1 * m + (1.0 - ADAM_B1) * g
    v = ADAM_B2 * v + (1.0 - ADAM_B2) * (g * g)
    m_hat = m / (1.0 - ADAM_B1 ** ADAM_STEP)
    v_hat = v / (1.0 - ADAM_B2 ** ADAM_STEP)
    delta = -ADAM_LR * (m_hat / (jnp.sqrt(v_hat) + ADAM_EPS) + ADAM_WD * w)
    return delta, m, v


def _adamw(w, g, m, v, *, name):
    rows, cols = w.shape
    tr = rows
    for cand in (512, 256, 128, 64, 32, 16, 8):
        if rows % cand == 0 and rows > cand:
            tr = cand
            break

    def body(w_ref, g_ref, m_ref, v_ref, d_ref, nm_ref, nv_ref):
        d, nm, nv = _adamw_math(w_ref[...], g_ref[...], m_ref[...], v_ref[...])
        d_ref[...] = d
        nm_ref[...] = nm
        nv_ref[...] = nv

    spec = pl.BlockSpec((tr, cols), lambda i: (i, 0))
    return pl.pallas_call(
        body, name=name, grid=(rows // tr,), in_specs=[spec] * 4, out_specs=[spec] * 3,
        out_shape=[jax.ShapeDtypeStruct((rows, cols), F32)] * 3,
        compiler_params=_params("parallel"),
    )(w, g, m, v)


def _sum_parts(parts, *, name):
    _, rows, cols = parts.shape
    tr = rows
    for cand in (256, 128, 64, 32, 16):
        if rows % cand == 0 and rows > cand:
            tr = cand
            break

    def body(p_ref, o_ref):
        acc = p_ref[0].astype(F32)
        for d in range(1, N_DEV):
            acc = acc + p_ref[d].astype(F32)
        o_ref[...] = acc

    return pl.pallas_call(
        body, name=name, grid=(rows // tr,),
        in_specs=[pl.BlockSpec((N_DEV, tr, cols), lambda i: (0, i, 0))],
        out_specs=pl.BlockSpec((tr, cols), lambda i: (i, 0)),
        out_shape=jax.ShapeDtypeStruct((rows, cols), F32),
        compiler_params=_params("parallel"),
    )(parts)


def _place():
    return lax.axis_index("x"), lax.axis_index("y"), lax.axis_index("c")


def _flip(v, bit):
    return 1 - v if bit else v


N_PEERS = N_DEV - 1


def _peer_copy(k, src_ref, land_ref, send_sem, recv_sem, gather):
    x, y, c = _place()
    my_id = 4 * x + 2 * y + c
    px, py, pc = _flip(x, k & 4), _flip(y, k & 2), _flip(c, k & 1)
    peer_id = 4 * px + 2 * py + pc
    return pltpu.make_async_remote_copy(
        src_ref=src_ref if gather else src_ref.at[peer_id], dst_ref=land_ref.at[my_id],
        send_sem=send_sem.at[k - 1], recv_sem=recv_sem.at[k - 1],
        device_id=(px, py, pc), device_id_type=MESH)


def _sequencer_copies(srcs, *, gather, name, collective_id, after):
    n = len(srcs)
    hbm = pltpu.MemorySpace.HBM
    src_refs = [jax.new_ref(s, memory_space=hbm) for s in srcs]
    land_refs = [jax.empty_ref(jax.ShapeDtypeStruct(((N_DEV,) + s.shape) if gather else s.shape, s.dtype),
                               memory_space=hbm) for s in srcs]
    token_in = jax.new_ref(jnp.zeros((8, 128), F32) if after is None else after, memory_space=hbm)
    token_out = jax.empty_ref(jax.ShapeDtypeStruct((8, 128), F32), memory_space=hbm)

    @pl.kernel(mesh=plsc.ScalarSubcoreMesh(axis_name="seq", num_cores=1), name=name,
               scratch_types=(pltpu.SemaphoreType.DMA((n, N_PEERS)), pltpu.SemaphoreType.DMA((n, N_PEERS)),
                              pltpu.SemaphoreType.DMA((n + 1,))),
               compiler_params=pltpu.CompilerParams(collective_id=collective_id))
    def launch(send_sems, recv_sems, local_sems):
        x, y, c = _place()
        my_id = 4 * x + 2 * y + c
        barrier = pltpu.get_barrier_semaphore()
        own = [pltpu.make_async_copy(src_refs[t] if gather else src_refs[t].at[my_id], land_refs[t].at[my_id],
                                     local_sems.at[t]) for t in range(n)]
        if gather:
            sibling = (x, y, 1 - c)
            chips = [(1 - x, y), (x, 1 - y), (1 - x, 1 - y)]
            for peer in [sibling] + [(*chip, c) for chip in chips]:
                pl.semaphore_signal(barrier, inc=1, device_id=peer, device_id_type=MESH)
            pl.semaphore_wait(barrier, 4)

            def copy(t, k, block, to, src=None):
                dst = land_refs[t].at[4 * block[0] + 2 * block[1] + block[2]]
                return pltpu.make_async_remote_copy(
                    src_ref=dst if src is None else src, dst_ref=dst,
                    send_sem=send_sems.at[t, k], recv_sem=recv_sems.at[t, k], device_id=to, device_id_type=MESH)

            for cp in own:
                cp.start()
            sends = []
            for t in range(n):
                sends.append(copy(t, 0, (x, y, c), sibling, src=src_refs[t]))
                sends += [copy(t, 1 + j, (x, y, c), (*chip, c), src=src_refs[t]) for j, chip in enumerate(chips)]
            for cp in sends:
                cp.start()
            for t in range(n):
                for j, chip in enumerate(chips):
                    copy(t, 1 + j, (*chip, c), (x, y, c)).wait_recv()
                    passed_on = copy(t, 4 + j, (*chip, c), sibling)
                    passed_on.start()
                    sends.append(passed_on)
            for t in range(n):
                copy(t, 0, sibling, (x, y, c)).wait_recv()
                for j, chip in enumerate(chips):
                    copy(t, 4 + j, (*chip, 1 - c), (x, y, c)).wait_recv()
            for cp in sends:
                cp.wait_send()
            for cp in own:
                cp.wait()
        else:
            for k in range(1, N_DEV):
                peer = (_flip(x, k & 4), _flip(y, k & 2), _flip(c, k & 1))
                pl.semaphore_signal(barrier, inc=1, device_id=peer, device_id_type=MESH)
            pl.semaphore_wait(barrier, N_PEERS)
            for cp in own:
                cp.start()
            copies = [_peer_copy(k, src_refs[t], land_refs[t], send_sems.at[t], recv_sems.at[t], gather)
                      for t in range(n) for k in range(1, N_DEV)]
            for cp in copies:
                cp.start()
            for cp in own:
                cp.wait()
            for cp in copies:
                cp.wait()
        passed = pltpu.make_async_copy(token_in, token_out, local_sems.at[n])
        passed.start()
        passed.wait()

    launch()
    return [ref[...] for ref in land_refs], token_out[...]


def _all_reduce_small(buf, *, name):
    _, rows, lanes = buf.shape

    def body(in_ref, out_ref, parts, send_sems, recv_sems):
        x, y, c = _place()
        my_id = 4 * x + 2 * y + c
        peers = []
        for k in range(1, N_DEV):
            px, py, pc = _flip(x, k & 4), _flip(y, k & 2), _flip(c, k & 1)
            peers.append(((px, py, pc), 4 * px + 2 * py + pc))
        scatter = [pltpu.make_async_remote_copy(
            src_ref=in_ref.at[pid], dst_ref=parts.at[my_id],
            send_sem=send_sems.at[0, k], recv_sem=recv_sems.at[0, k],
            device_id=peer, device_id_type=MESH) for k, (peer, pid) in enumerate(peers)]
        for cp in scatter:
            cp.start()
        parts[my_id] = in_ref[my_id]
        for cp in scatter:
            cp.wait()
        total = parts[0]
        for d in range(1, N_DEV):
            total = total + parts[d]
        out_ref[my_id] = total
        gather = [pltpu.make_async_remote_copy(
            src_ref=out_ref.at[my_id], dst_ref=out_ref.at[my_id],
            send_sem=send_sems.at[1, k], recv_sem=recv_sems.at[1, k],
            device_id=peer, device_id_type=MESH) for k, (peer, pid) in enumerate(peers)]
        for cp in gather:
            cp.start()
        for k, (peer, pid) in enumerate(peers):
            pltpu.make_async_remote_copy(
                src_ref=out_ref.at[pid], dst_ref=out_ref.at[pid],
                send_sem=send_sems.at[1, k], recv_sem=recv_sems.at[1, k],
                device_id=peer, device_id_type=MESH).wait()

    vmem = pl.BlockSpec(memory_space=pltpu.VMEM)
    return pl.pallas_call(
        body, name=name, in_specs=[vmem], out_specs=vmem,
        out_shape=jax.ShapeDtypeStruct(buf.shape, F32),
        scratch_shapes=[pltpu.VMEM(buf.shape, F32),
                        pltpu.SemaphoreType.DMA((2, 7)), pltpu.SemaphoreType.DMA((2, 7))],
        compiler_params=pltpu.CompilerParams(vmem_limit_bytes=VMEM_LIMIT),
    )(buf)


TM = 512
TMM = 1024
TKW = 2048
MIX_TILE = 128
FFN_TILE = 256
FFN_TC = 512


def _block_diag(w):
    wg = w.reshape(N_GROUPS, HEADS_PER_GROUP, LRU_HEAD_DIM, LRU_HEAD_DIM)
    eye = jnp.eye(HEADS_PER_GROUP, dtype=w.dtype)
    bd = wg[:, :, :, None, :] * eye[None, :, None, :, None]
    return bd.reshape(N_GROUPS, LRU_GROUP, LRU_GROUP).astype(BF16)


def _head_blocks(bd):
    b5 = bd.reshape(N_GROUPS, HEADS_PER_GROUP, LRU_HEAD_DIM, HEADS_PER_GROUP, LRU_HEAD_DIM)
    blocks = [b5[:, h, :, h, :] for h in range(HEADS_PER_GROUP)]
    return jnp.stack(blocks, axis=1).reshape(LRU_HEADS, LRU_HEAD_DIM, LRU_HEAD_DIM)


def _w(lw, key, after):
    value = lw[key]
    return value(after) if callable(value) else value


def _layer_fwd(x, lw, tag):
    sv_rows = x.shape[0]
    h1 = _norm_fwd(x, lw["g1"], tm=TM, name=f"norm1_fwd_{tag}")
    z = _mm_nt(h1, _w(lw, "w_in_t", h1), tm=min(TMM, sv_rows), tn=896, tk=D_MODEL, out_dtype=F32, name=f"in_proj_{tag}")
    y_mix, hs = _mixer_fwd(z, _w(lw, "cw", z), lw["cb"], lw["wa_bd"], lw["wx_bd"], lw["ba"], lw["bx"], lw["lam"],
                           _w(lw, "scw", z), tile=MIX_TILE, name=f"mixer_fwd_{tag}")
    x2 = _mm_nn(y_mix, _w(lw, "w_out", y_mix), tm=min(TMM, sv_rows), tn=D_MODEL, tk=D_MIX, out_dtype=F32, name=f"out_proj_{tag}",
                residual=x)
    h2 = _norm_fwd(x2, lw["g2"], tm=TM, name=f"norm2_fwd_{tag}")
    p = _mm_up(h2, _w(lw, "w_up_b", h2), tm=min(TMM, sv_rows), name=f"up_proj_{tag}")
    act = _ffn_fwd(p, _w(lw, "fcw", p), tile=FFN_TILE, tc=FFN_TC, name=f"ffn_fwd_{tag}")
    x3 = _mm_nn(act, _w(lw, "w_down", act), tm=min(TMM, sv_rows), tn=D_MODEL, tk=D_FF // 2, out_dtype=F32, name=f"down_proj_{tag}",
                residual=x2)
    saved = dict(x=x, h1=h1, z=z, y_mix=y_mix, hs=hs, x2=x2, h2=h2, p=p, act=act)
    return x3, saved


def _layer_bwd(dx3, dx3b, lw, sv, tag, put):
    sv_rows = dx3.shape[0]
    w_in_t, w_out, w_up_b, w_down = (_w(lw, k, dx3) for k in ("w_in_t", "w_out", "w_up_b", "w_down"))
    cw, scw, fcw = (_w(lw, k, dx3) for k in ("cw", "scw", "fcw"))
    dact = _mm_nt(dx3b, w_down, tm=min(TMM, sv_rows), tn=1024, tk=D_MODEL, out_dtype=F32, name=f"down_bwd_x_{tag}")
    g_down = _mm_tn(sv["act"], dx3b, tm=1024, tn=D_MODEL, tk=min(TKW, sv_rows), out_dtype=BF16, name=f"down_bwd_w_{tag}")
    dp, dfcw = _ffn_bwd(sv["p"], dact, fcw, tile=FFN_TILE, tc=FFN_TC, name=f"ffn_bwd_{tag}",
                        token=put("w_down", g_down))
    dh2 = _mm_up_bwd_x(dp, w_up_b, tm=min(TMM, sv_rows), name=f"up_bwd_x_{tag}")
    g_up = _mm_up_bwd_w(sv["h2"], dp, tm=D_MODEL, tk=min(TKW, sv_rows), name=f"up_bwd_w_{tag}")
    dx2, dx2b, dg2 = _norm_bwd(dh2, sv["x2"], lw["g2"], dx3, tm=TM, name=f"norm2_bwd_{tag}",
                               token=put("w_up_b", g_up))
    dy = _mm_nt(dx2b, w_out, tm=min(TMM, sv_rows), tn=768, tk=D_MODEL, out_dtype=F32, name=f"out_bwd_x_{tag}")
    g_out = _mm_tn(sv["y_mix"], dx2b, tm=768, tn=D_MODEL, tk=min(TKW, sv_rows), out_dtype=BF16, name=f"out_bwd_w_{tag}")
    out_token = put("w_out", g_out)
    dz, dcw, dvec, dwa, dwx, dscw = _mixer_bwd(
        sv["z"], sv["hs"], dy, cw, lw["cb"], lw["wa_bd"], lw["wx_bd"], lw["ba"], lw["bx"], lw["lam"],
        scw, tile=MIX_TILE, name=f"mixer_bwd_{tag}", token=out_token)
    dh1 = _mm_nn(dz, w_in_t, tm=min(TMM, sv_rows), tn=D_MODEL, tk=896, out_dtype=F32, name=f"in_bwd_x_{tag}")
    g_in_t = _mm_tn(dz, sv["h1"], tm=896, tn=D_MODEL, tk=min(TKW, sv_rows), out_dtype=BF16, name=f"in_bwd_w_{tag}")
    dx, dxb, dg1 = _norm_bwd(dh1, sv["x"], lw["g1"], dx2, tm=TM, name=f"norm1_bwd_{tag}",
                             token=put("w_in_t", g_in_t))
    small = dict(norm1_g=dg1[0], lru_conv_w=dcw[0:4], lru_conv_b=dvec[0], lru_wa=_head_blocks(dwa),
                 lru_ba=dvec[1], lru_wx=_head_blocks(dwx), lru_bx=dvec[2], lru_lambda=dvec[3],
                 sc_conv_w=dscw[0:3], norm2_g=dg2[0], ffn_conv_w=dfcw[:, 0:3, :])
    return dx, dxb, small


SMALL_ORDER = ("norm1_g", "lru_conv_w", "lru_conv_b", "lru_wa", "lru_ba", "lru_wx", "lru_bx", "lru_lambda",
               "sc_conv_w", "norm2_g", "ffn_conv_w")


def _local_step(x, tgt, layers, final_g, put):
    saved = []
    h = x
    for l in range(DEPTH):
        h, sv = _layer_fwd(h, layers[l], f"l{l}")
        saved.append(sv)
    loss_blk, dx, dxb, dgf = _loss_head(h, final_g, tgt, tm=TM, name="loss_head")
    smalls = [None] * DEPTH
    for l in reversed(range(DEPTH)):
        dx, dxb, smalls[l] = _layer_bwd(dx, dxb, layers[l], saved[l], f"l{l}", functools.partial(put, l))
    return loss_blk[0, 0], dx, smalls, dgf[0]


def kernel(x, norm1_g, w_in, lru_conv_w, lru_conv_b, lru_wa, lru_ba, lru_wx, lru_bx, lru_lambda, sc_conv_w, w_out, norm2_g, w_up, ffn_conv_w, w_down, final_g, loss_target, m_norm1_g, m_w_in, m_lru_conv_w, m_lru_conv_b, m_lru_wa, m_lru_ba, m_lru_wx, m_lru_bx, m_lru_lambda, m_sc_conv_w, m_w_out, m_norm2_g, m_w_up, m_ffn_conv_w, m_w_down, m_final_g, v_norm1_g, v_w_in, v_lru_conv_w, v_lru_conv_b, v_lru_wa, v_lru_ba, v_lru_wx, v_lru_bx, v_lru_lambda, v_sc_conv_w, v_w_out, v_norm2_g, v_w_up, v_ffn_conv_w, v_w_down, v_final_g):
    names = ["norm1_g", "w_in", "lru_conv_w", "lru_conv_b", "lru_wa", "lru_ba", "lru_wx", "lru_bx", "lru_lambda",
             "sc_conv_w", "w_out", "norm2_g", "w_up", "ffn_conv_w", "w_down", "final_g"]
    w = dict(zip(names, [norm1_g, w_in, lru_conv_w, lru_conv_b, lru_wa, lru_ba, lru_wx, lru_bx, lru_lambda,
                         sc_conv_w, w_out, norm2_g, w_up, ffn_conv_w, w_down, final_g]))
    m = dict(zip(names, [m_norm1_g, m_w_in, m_lru_conv_w, m_lru_conv_b, m_lru_wa, m_lru_ba, m_lru_wx, m_lru_bx,
                         m_lru_lambda, m_sc_conv_w, m_w_out, m_norm2_g, m_w_up, m_ffn_conv_w, m_w_down, m_final_g]))
    v = dict(zip(names, [v_norm1_g, v_w_in, v_lru_conv_w, v_lru_conv_b, v_lru_wa, v_lru_ba, v_lru_wx, v_lru_bx,
                         v_lru_lambda, v_sc_conv_w, v_w_out, v_norm2_g, v_w_up, v_ffn_conv_w, v_w_down, v_final_g]))
    my_id = 4 * lax.axis_index("x") + 2 * lax.axis_index("y") + lax.axis_index("c")

    taps = jnp.zeros((DEPTH, 16, 768), F32)
    taps = taps.at[:, 0:4, 0:128].set(lru_conv_w).at[:, 4:7, 0:64].set(sc_conv_w).at[:, 8:11, :].set(ffn_conv_w)
    shards = {}
    for l in range(DEPTH):
        shards[f"w_in_t{l}"] = jnp.swapaxes(w_in[l], 0, 1).astype(BF16)
        if l == 0:
            shards["taps"] = taps.reshape(DEPTH * 16, 768)
        shards[f"w_out{l}"] = w_out[l].astype(BF16)
        shards[f"w_up_b{l}"] = w_up[l].astype(BF16)
        shards[f"w_down{l}"] = w_down[l].astype(BF16)
    ids = iter(range(16))
    got = {}
    chain = [None]
    for group in (("w_in_t0", "taps"), ("w_out0",), ("w_up_b0",), ("w_down0",),
                  ("w_in_t1",), ("w_out1",), ("w_up_b1",), ("w_down1",)):
        lands, chain[0] = _sequencer_copies([shards[k] for k in group], gather=True, name=f"gather_{group[0]}",
                                            collective_id=next(ids), after=None)
        got.update(zip(group, lands))

    def fetch(key, after):
        return got[key]

    def tap_rows(l, lo, hi, width, after):
        tl = fetch("taps", after).reshape(N_DEV, DEPTH, 16, 768)[:, l, lo:hi, 0:width]
        return jnp.transpose(tl, (1, 0, 2)).reshape(hi - lo, N_DEV * width)

    layers = []
    for l in range(DEPTH):
        layers.append(dict(
            g1=norm1_g[l], g2=norm2_g[l], cb=lru_conv_b[l], ba=lru_ba[l], bx=lru_bx[l], lam=lru_lambda[l],
            wa_bd=_block_diag(lru_wa[l]), wx_bd=_block_diag(lru_wx[l]),
            cw=functools.partial(tap_rows, l, 0, 4, 128), scw=functools.partial(tap_rows, l, 4, 7, 64),
            fcw=lambda after, l=l: tap_rows(l, 8, 11, 768, after).reshape(3, 2, D_FF).transpose(1, 0, 2),
            w_in_t=lambda after, l=l: fetch(f"w_in_t{l}", after).reshape(D_IN, D_MODEL),
            w_out=lambda after, l=l: fetch(f"w_out{l}", after).reshape(D_MIX, D_MODEL),
            w_up_b=lambda after, l=l: fetch(f"w_up_b{l}", after),
            w_down=lambda after, l=l: fetch(f"w_down{l}", after).reshape(D_FF, D_MODEL)))

    scatter_handles = {}

    def put(l, key, grad):
        blocks = grad if grad.ndim == 3 else grad.reshape(N_DEV, grad.shape[0] // N_DEV, grad.shape[1])
        (scatter_handles[(l, key)],), chain[0] = _sequencer_copies(
            [blocks], gather=False, name=f"scatter_{key}{l}", collective_id=next(ids), after=chain[0])
        return blocks

    loss_local, dx, smalls, dgf = _local_step(x[0], loss_target[0], layers, final_g, put)
    loss = lax.psum(loss_local, ("x", "y", "c"))

    parts = []
    for l in range(DEPTH):
        for key in ("w_in_t", "w_out", "w_up_b", "w_down"):
            parts.append(scatter_handles[(l, key)])

    flat = [smalls[l][k].reshape(-1) for l in range(DEPTH) for k in SMALL_ORDER] + [dgf.reshape(-1)]
    sizes = [f.shape[0] for f in flat]
    total = sum(sizes)
    rows = -(-total // (N_DEV * 128 * 8)) * 8
    flat.append(jnp.zeros((N_DEV * rows * 128 - total,), F32))
    small_sum = _all_reduce_small(jnp.concatenate(flat).reshape(N_DEV, rows, 128), name="reduce_small").reshape(-1)
    small_g, off = [], 0
    for sz in sizes:
        small_g.append(small_sum[off:off + sz])
        off += sz
    gs = {}
    for l in range(DEPTH):
        for i, k in enumerate(SMALL_ORDER):
            gs.setdefault(k, []).append(small_g[l * len(SMALL_ORDER) + i])
    g_final = small_g[-1]

    grads = {}
    per_layer = {k: [] for k in ("w_in", "w_out", "w_up", "w_down")}
    for l in range(DEPTH):
        p_in, p_out, p_up, p_down = parts[4 * l:4 * l + 4]
        per_layer["w_in"].append(jnp.swapaxes(_sum_parts(p_in, name=f"sum_w_in_l{l}"), 0, 1))
        per_layer["w_out"].append(_sum_parts(p_out, name=f"sum_w_out_l{l}"))
        per_layer["w_up"].append(_sum_parts(p_up, name=f"sum_w_up_l{l}"))
        per_layer["w_down"].append(_sum_parts(p_down, name=f"sum_w_down_l{l}"))
    for k, lst in per_layer.items():
        grads[k] = jnp.stack(lst)
    for k in ("norm1_g", "lru_conv_b", "lru_ba", "lru_bx", "lru_lambda", "norm2_g"):
        grads[k] = jnp.stack(gs[k]).reshape(DEPTH, -1)
    for k in ("lru_wa", "lru_wx"):
        grads[k] = jnp.stack(gs[k]).reshape(DEPTH, LRU_HEADS, LRU_HEAD_DIM, LRU_HEAD_DIM)
    grads["final_g"] = g_final
    cw_full = jnp.stack(gs["lru_conv_w"]).reshape(DEPTH, 4, N_DEV, 128)
    grads["lru_conv_w"] = lax.dynamic_index_in_dim(cw_full, my_id, axis=2, keepdims=False)
    scw_full = jnp.stack(gs["sc_conv_w"]).reshape(DEPTH, 3, N_DEV, 64)
    grads["sc_conv_w"] = lax.dynamic_index_in_dim(scw_full, my_id, axis=2, keepdims=False)
    fcw_full = jnp.stack(gs["ffn_conv_w"]).reshape(DEPTH, 2, 3, D_FF).transpose(0, 2, 1, 3).reshape(DEPTH, 3, N_DEV, 768)
    grads["ffn_conv_w"] = lax.dynamic_index_in_dim(fcw_full, my_id, axis=2, keepdims=False)

    deltas, new_m, new_v = {}, {}, {}
    for k in names:
        shape = w[k].shape
        cols = shape[-1]
        as2d = lambda a: a.reshape(-1, cols)
        d, nm, nv = _adamw(as2d(w[k]), as2d(grads[k]), as2d(m[k]), as2d(v[k]), name=f"adamw_{k}")
        deltas[k], new_m[k], new_v[k] = d.reshape(shape), nm.reshape(shape), nv.reshape(shape)

    return (loss, dx[None], *[grads[k] for k in names], *[deltas[k] for k in names],
            *[new_m[k] for k in names], *[new_v[k] for k in names])
```

```python
import functools
import math

import jax
import jax.numpy as jnp
from jax import lax
from jax.experimental import pallas as pl
from jax.experimental.pallas import tpu as pltpu
from jax.experimental.pallas import tpu_sc as plsc

F32 = jnp.float32
BF16 = jnp.bfloat16

N_DEV = 8
DEPTH = 2
D_MODEL = 1024
D_LRU = 1024
D_SC = 512
D_MIX = D_LRU + D_SC
D_IN = 2 * D_LRU + 3 * D_SC
D_FF = 3072
LRU_HEADS = 16
LRU_HEAD_DIM = 64
LRU_GROUP = 256
N_GROUPS = D_LRU // LRU_GROUP
HEADS_PER_GROUP = LRU_GROUP // LRU_HEAD_DIM
RG_C = 8.0
EPS = 1e-6
HALO = 8

ADAM_LR = 0.001
ADAM_B1 = 0.9
ADAM_B2 = 0.999
ADAM_EPS = 1e-08
ADAM_WD = 0.01
ADAM_STEP = 10

GELU_C = math.sqrt(2.0 / math.pi)
GELU_A = 0.044715

VMEM_LIMIT = 56 * 1024 * 1024
MESH = pl.DeviceIdType.MESH


def _params(*sem):
    return pltpu.CompilerParams(dimension_semantics=tuple(sem) if sem else None,
                                vmem_limit_bytes=VMEM_LIMIT)


def _gelu_parts(x):
    x2 = x * x
    t = jnp.tanh(GELU_C * (x + GELU_A * x * x2))
    half = 0.5 * (1.0 + t)
    g = x * half
    dg = half + 0.5 * x * (1.0 - t * t) * (GELU_C * (1.0 + 3.0 * GELU_A * x2))
    return g, dg


def _gelu(x):
    t = jnp.tanh(GELU_C * (x + GELU_A * x * x * x))
    return 0.5 * x * (1.0 + t)


def _sigmoid(x):
    return 1.0 / (1.0 + jnp.exp(-x))


def _softplus(x):
    e = jnp.exp(-jnp.abs(x))
    u = 1.0 + e
    log1p_e = jnp.where(u == 1.0, e, jnp.log(u) * (e / (u - 1.0)))
    return jnp.maximum(x, 0.0) + log1p_e


def _rms(x):
    ms = jnp.mean(x * x, axis=-1, keepdims=True)
    return lax.rsqrt(ms + EPS)


def _dot(a, b, dims):
    return lax.dot_general(a, b, (dims, ((), ())), preferred_element_type=F32)


NN = ((1,), (0,))
NT = ((1,), (1,))
TN = ((0,), (0,))


def _matmul(a, b, *, dims, grid, a_spec, b_spec, o_spec, out_shape, acc_shape, name,
            residual=None, r_spec=None, token=None):
    nk = grid[2]

    def body(*refs):
        a_ref, b_ref = refs[0], refs[1]
        r_ref = refs[2] if residual is not None else None
        o_ref = refs[2 + (residual is not None) + (token is not None)]
        prod = _dot(a_ref[...].astype(BF16), b_ref[...].astype(BF16), dims)

        def finish(total):
            if r_ref is not None:
                total = total + r_ref[...]
            o_ref[...] = total.astype(o_ref.dtype)

        if nk == 1:
            finish(prod)
            return
        acc_ref = refs[-1]
        k = pl.program_id(2)

        @pl.when(k == 0)
        def _():
            acc_ref[...] = prod

        @pl.when(jnp.logical_and(k > 0, k < nk - 1))
        def _():
            acc_ref[...] += prod

        @pl.when(k == nk - 1)
        def _():
            finish(acc_ref[...] + prod)

    in_specs = [a_spec, b_spec]
    args = [a, b]
    if residual is not None:
        in_specs.append(r_spec)
        args.append(residual)
    if token is not None:
        in_specs.append(pl.BlockSpec(memory_space=pl.ANY))
        args.append(token)
    return pl.pallas_call(
        body, name=name, grid=grid, in_specs=in_specs, out_specs=o_spec, out_shape=out_shape,
        scratch_shapes=[pltpu.VMEM(acc_shape, F32)] if nk > 1 else [],
        compiler_params=_params("parallel", "parallel", "arbitrary"),
    )(*args)


def _mm_nn(a, b, *, tm, tn, tk, out_dtype, name, residual=None):
    m, kd = a.shape
    n = b.shape[1]
    return _matmul(
        a, b, dims=NN, grid=(m // tm, n // tn, kd // tk),
        a_spec=pl.BlockSpec((tm, tk), lambda i, j, k: (i, k)),
        b_spec=pl.BlockSpec((tk, tn), lambda i, j, k: (k, j)),
        o_spec=pl.BlockSpec((tm, tn), lambda i, j, k: (i, j)),
        out_shape=jax.ShapeDtypeStruct((m, n), out_dtype), acc_shape=(tm, tn), name=name,
        residual=residual, r_spec=pl.BlockSpec((tm, tn), lambda i, j, k: (i, j)))


def _mm_nt(a, b, *, tm, tn, tk, out_dtype, name):
    m, kd = a.shape
    n = b.shape[0]
    return _matmul(
        a, b, dims=NT, grid=(m // tm, n // tn, kd // tk),
        a_spec=pl.BlockSpec((tm, tk), lambda i, j, k: (i, k)),
        b_spec=pl.BlockSpec((tn, tk), lambda i, j, k: (j, k)),
        o_spec=pl.BlockSpec((tm, tn), lambda i, j, k: (i, j)),
        out_shape=jax.ShapeDtypeStruct((m, n), out_dtype), acc_shape=(tm, tn), name=name)


def _mm_tn(a, b, *, tm, tn, tk, out_dtype, name, token=None):
    kd, m = a.shape
    n = b.shape[1]
    return _matmul(
        a, b, dims=TN, grid=(m // tm, n // tn, kd // tk),
        a_spec=pl.BlockSpec((tk, tm), lambda i, j, k: (k, i)),
        b_spec=pl.BlockSpec((tk, tn), lambda i, j, k: (k, j)),
        o_spec=pl.BlockSpec((tm, tn), lambda i, j, k: (i, j)),
        out_shape=jax.ShapeDtypeStruct((m, n), out_dtype), acc_shape=(tm, tn), name=name, token=token)


def _mm_up_bwd_w(h2, dp, *, tm, tk, name):
    s = h2.shape[0]
    nb = D_FF * 2 // N_DEV
    per_half = D_FF // nb
    return _matmul(
        h2, dp, dims=TN, grid=(D_MODEL // tm, N_DEV, s // tk),
        a_spec=pl.BlockSpec((tk, tm), lambda i, j, k: (k, i)),
        b_spec=pl.BlockSpec((None, tk, nb), lambda i, j, k: (j // per_half, k, j % per_half)),
        o_spec=pl.BlockSpec((None, tm, nb), lambda i, j, k: (j, i, 0)),
        out_shape=jax.ShapeDtypeStruct((N_DEV, D_MODEL, nb), BF16), acc_shape=(tm, nb), name=name)


def _norm_fwd(x, g, *, tm, name):
    s, d = x.shape

    def body(x_ref, g_ref, h_ref):
        xv = x_ref[...]
        h_ref[...] = (xv * _rms(xv) * g_ref[...]).astype(BF16)

    return pl.pallas_call(
        body, name=name, grid=(s // tm,),
        in_specs=[pl.BlockSpec((tm, d), lambda i: (i, 0)), pl.BlockSpec((1, d), lambda i: (0, 0))],
        out_specs=pl.BlockSpec((tm, d), lambda i: (i, 0)),
        out_shape=jax.ShapeDtypeStruct((s, d), BF16),
        compiler_params=_params("parallel"),
    )(x, g.reshape(1, d))


def _behind(token):
    return jnp.zeros((8, 128), F32) if token is None else token


def _norm_bwd(dh, x, g, dres, *, tm, name, token=None):
    s, d = x.shape

    def body(dh_ref, x_ref, g_ref, dres_ref, token_ref, dx_ref, dxb_ref, dg_ref):
        @pl.when(pl.program_id(0) == 0)
        def _():
            dg_ref[...] = jnp.zeros_like(dg_ref)

        xv = x_ref[...]
        rstd = _rms(xv)
        n = xv * rstd
        dhv = dh_ref[...]
        dn = dhv * g_ref[...]
        dx = rstd * (dn - n * jnp.mean(dn * n, axis=-1, keepdims=True))
        dx = dres_ref[...] + dx
        dx_ref[...] = dx
        dxb_ref[...] = dx.astype(BF16)
        dg_ref[0:1, :] += jnp.sum(dhv * n, axis=0, keepdims=True)

    return pl.pallas_call(
        body, name=name, grid=(s // tm,),
        in_specs=[pl.BlockSpec((tm, d), lambda i: (i, 0)), pl.BlockSpec((tm, d), lambda i: (i, 0)),
                  pl.BlockSpec((1, d), lambda i: (0, 0)), pl.BlockSpec((tm, d), lambda i: (i, 0)),
                  pl.BlockSpec(memory_space=pl.ANY)],
        out_specs=[pl.BlockSpec((tm, d), lambda i: (i, 0)), pl.BlockSpec((tm, d), lambda i: (i, 0)),
                   pl.BlockSpec((8, d), lambda i: (0, 0))],
        out_shape=[jax.ShapeDtypeStruct((s, d), F32), jax.ShapeDtypeStruct((s, d), BF16),
                   jax.ShapeDtypeStruct((8, d), F32)],
        compiler_params=_params("arbitrary"),
    )(dh, x, g.reshape(1, d), dres, _behind(token))


def _loss_head(x, g, tgt, *, tm, name):
    s, d = x.shape

    def body(x_ref, g_ref, t_ref, loss_ref, dx_ref, dxb_ref, dg_ref):
        @pl.when(pl.program_id(0) == 0)
        def _():
            dg_ref[...] = jnp.zeros_like(dg_ref)
            loss_ref[...] = jnp.zeros_like(loss_ref)

        xv = x_ref[...]
        gv = g_ref[...]
        rstd = _rms(xv)
        n = xv * rstd
        e = n * gv - t_ref[...]
        part = 0.5 * jnp.sum(jnp.mean(e * e, axis=-1, keepdims=True), axis=0, keepdims=True)
        loss_ref[...] += jnp.broadcast_to(part, loss_ref.shape)
        dy = e * (1.0 / d)
        dn = dy * gv
        dx = rstd * (dn - n * jnp.mean(dn * n, axis=-1, keepdims=True))
        dx_ref[...] = dx
        dxb_ref[...] = dx.astype(BF16)
        dg_ref[0:1, :] += jnp.sum(dy * n, axis=0, keepdims=True)

    return pl.pallas_call(
        body, name=name, grid=(s // tm,),
        in_specs=[pl.BlockSpec((tm, d), lambda i: (i, 0)), pl.BlockSpec((1, d), lambda i: (0, 0)),
                  pl.BlockSpec((tm, d), lambda i: (i, 0))],
        out_specs=[pl.BlockSpec((8, 128), lambda i: (0, 0)), pl.BlockSpec((tm, d), lambda i: (i, 0)),
                   pl.BlockSpec((tm, d), lambda i: (i, 0)), pl.BlockSpec((8, d), lambda i: (0, 0))],
        out_shape=[jax.ShapeDtypeStruct((8, 128), F32), jax.ShapeDtypeStruct((s, d), F32),
                   jax.ShapeDtypeStruct((s, d), BF16), jax.ShapeDtypeStruct((8, d), F32)],
        compiler_params=_params("arbitrary"),
    )(x, g.reshape(1, d), tgt)


def _scan_rows(a_ref, b_ref, h_ref, carry, *, rows, reverse):
    width = a_ref.shape[1]
    n_chunks = rows // 8
    row = lax.broadcasted_iota(jnp.int32, (8, width), 0)

    def step(ci, carry):
        chunk = (n_chunks - 1 - ci) if reverse else ci
        off = pl.multiple_of(chunk * 8, 8)
        av = a_ref[pl.ds(off, 8), :]
        bv = b_ref[pl.ds(off, 8), :]
        for sh in (1, 2, 4):
            if reverse:
                a_sh = pltpu.roll(av, 8 - sh, 0)
                b_sh = pltpu.roll(bv, 8 - sh, 0)
                m = row < 8 - sh
            else:
                a_sh = pltpu.roll(av, sh, 0)
                b_sh = pltpu.roll(bv, sh, 0)
                m = row >= sh
            bv = jnp.where(m, av * b_sh + bv, bv)
            av = jnp.where(m, av * a_sh, av)
        h = av * carry + bv
        h_ref[pl.ds(off, 8), :] = h
        return h[0:1, :] if reverse else h[7:8, :]

    return lax.fori_loop(0, n_chunks, step, carry)


def _lru_gates(lx, wa_ref, wx_ref, ba, bx, sp):
    lxb = lx.astype(BF16)
    pre_r = jnp.concatenate(
        [_dot(lxb[:, g * LRU_GROUP:(g + 1) * LRU_GROUP], wa_ref[g], NN) for g in range(N_GROUPS)], axis=1)
    pre_i = jnp.concatenate(
        [_dot(lxb[:, g * LRU_GROUP:(g + 1) * LRU_GROUP], wx_ref[g], NN) for g in range(N_GROUPS)], axis=1)
    r = _sigmoid(pre_r + ba)
    ig = _sigmoid(pre_i + bx)
    log_a = (-RG_C * r) * sp
    a = jnp.exp(log_a)
    mult = jnp.sqrt(-jnp.tanh(log_a) * (a * a + 1.0))
    return lxb, r, ig, a, mult


def _mixer_fwd(z, cw, cb, wa_bd, wx_bd, ba, bx, lam, scw, *, tile, name):
    s = z.shape[0]
    n_tiles = s // tile

    def body(z_ref, cw_ref, cb_ref, wa_ref, wx_ref, ba_ref, bx_ref, lam_ref, scw_ref,
             y_ref, hs_ref, ext_lx, ext_q, a_s, b_s, h_car):
        i = pl.program_id(0)

        @pl.when(i == 0)
        def _():
            ext_lx[0:HALO, :] = jnp.zeros((HALO, D_LRU), F32)
            ext_q[0:HALO, :] = jnp.zeros((HALO, D_SC), F32)
            h_car[...] = jnp.zeros_like(h_car)

        ext_lx[HALO:HALO + tile, :] = z_ref[:, 0:D_LRU]
        ext_q[HALO:HALO + tile, :] = z_ref[:, 2 * D_LRU + D_SC:2 * D_LRU + 2 * D_SC] * z_ref[:, 2 * D_LRU + 2 * D_SC:D_IN]
        lx = cb_ref[...] + cw_ref[0:1, :] * ext_lx[pl.ds(HALO - 3, tile), :]
        for k in range(1, 4):
            lx = lx + cw_ref[k:k + 1, :] * ext_lx[pl.ds(HALO - 3 + k, tile), :]
        cq = scw_ref[0:1, :] * ext_q[pl.ds(HALO - 2, tile), :]
        for k in range(1, 3):
            cq = cq + scw_ref[k:k + 1, :] * ext_q[pl.ds(HALO - 2 + k, tile), :]
        ext_lx[0:HALO, :] = ext_lx[tile:tile + HALO, :]
        ext_q[0:HALO, :] = ext_q[tile:tile + HALO, :]

        sp = _softplus(-lam_ref[...])
        _, _, ig, a, mult = _lru_gates(lx, wa_ref, wx_ref, ba_ref[...], bx_ref[...], sp)
        a_s[...] = a
        b_s[...] = mult * (ig * lx)
        h_car[0:1, :] = _scan_rows(a_s, b_s, hs_ref, h_car[0:1, :], rows=tile, reverse=False)

        y_ref[:, 0:D_LRU] = (hs_ref[...] * _gelu(z_ref[:, D_LRU:2 * D_LRU])).astype(BF16)
        y_ref[:, D_LRU:D_MIX] = (z_ref[:, 2 * D_LRU:2 * D_LRU + D_SC] * cq).astype(BF16)

    full = lambda shape: pl.BlockSpec(shape, lambda i: (0,) * len(shape))
    return pl.pallas_call(
        body, name=name, grid=(n_tiles,),
        in_specs=[pl.BlockSpec((tile, D_IN), lambda i: (i, 0)),
                  full((4, D_LRU)), full((1, D_LRU)),
                  full((N_GROUPS, LRU_GROUP, LRU_GROUP)), full((N_GROUPS, LRU_GROUP, LRU_GROUP)),
                  full((1, D_LRU)), full((1, D_LRU)), full((1, D_LRU)), full((3, D_SC))],
        out_specs=[pl.BlockSpec((tile, D_MIX), lambda i: (i, 0)), pl.BlockSpec((tile, D_LRU), lambda i: (i, 0))],
        out_shape=[jax.ShapeDtypeStruct((s, D_MIX), BF16), jax.ShapeDtypeStruct((s, D_LRU), F32)],
        scratch_shapes=[pltpu.VMEM((tile + HALO, D_LRU), F32), pltpu.VMEM((tile + HALO, D_SC), F32),
                        pltpu.VMEM((tile, D_LRU), F32), pltpu.VMEM((tile, D_LRU), F32),
                        pltpu.VMEM((8, D_LRU), F32)],
        compiler_params=_params("arbitrary"),
    )(z, cw, cb.reshape(1, -1), wa_bd, wx_bd, ba.reshape(1, -1), bx.reshape(1, -1), lam.reshape(1, -1), scw)


def _mixer_bwd(z, hs, dy, cw, cb, wa_bd, wx_bd, ba, bx, lam, scw, *, tile, name, token=None):
    s = z.shape[0]
    n_tiles = s // tile
    per8 = tile // 8

    def body(z_ref, zp_ref, hs_ref, hsp_ref, dy_ref, cw_ref, cb_ref, wa_ref, wx_ref, ba_ref, bx_ref, lam_ref, scw_ref,
             token_ref, dz_ref, dcw_ref, dvec_ref, dwa_ref, dwx_ref, dscw_ref,
             ext_lx, ext_q, ext_h, ext_a, ext_dlx, ext_dcq, a_s, b_s, lam_s, l_car):
        i = pl.program_id(0)
        first_tile = i == n_tiles - 1

        @pl.when(i == 0)
        def _():
            for ref in (dcw_ref, dvec_ref, dwa_ref, dwx_ref, dscw_ref, l_car):
                ref[...] = jnp.zeros_like(ref)
            ext_a[tile:tile + HALO, :] = jnp.zeros((HALO, D_LRU), F32)
            ext_dlx[tile:tile + HALO, :] = jnp.zeros((HALO, D_LRU), F32)
            ext_dcq[tile:tile + HALO, :] = jnp.zeros((HALO, D_SC), F32)

        keep = jnp.where(first_tile, 0.0, 1.0)
        sb = z_ref[:, 2 * D_LRU:2 * D_LRU + D_SC]
        sc = z_ref[:, 2 * D_LRU + D_SC:2 * D_LRU + 2 * D_SC]
        sx = z_ref[:, 2 * D_LRU + 2 * D_SC:D_IN]
        ext_lx[0:HALO, :] = zp_ref[:, 0:D_LRU] * keep
        ext_lx[HALO:HALO + tile, :] = z_ref[:, 0:D_LRU]
        ext_q[0:HALO, :] = zp_ref[:, 2 * D_LRU + D_SC:2 * D_LRU + 2 * D_SC] * zp_ref[:, 2 * D_LRU + 2 * D_SC:D_IN] * keep
        ext_q[HALO:HALO + tile, :] = sc * sx
        ext_h[0:HALO, :] = hsp_ref[...] * keep
        ext_h[HALO:HALO + tile, :] = hs_ref[...]

        lx = cb_ref[...] + cw_ref[0:1, :] * ext_lx[pl.ds(HALO - 3, tile), :]
        for k in range(1, 4):
            lx = lx + cw_ref[k:k + 1, :] * ext_lx[pl.ds(HALO - 3 + k, tile), :]
        cq = scw_ref[0:1, :] * ext_q[pl.ds(HALO - 2, tile), :]
        for k in range(1, 3):
            cq = cq + scw_ref[k:k + 1, :] * ext_q[pl.ds(HALO - 2 + k, tile), :]

        sp = _softplus(-lam_ref[...])
        lxb, r, ig, a, mult = _lru_gates(lx, wa_ref, wx_ref, ba_ref[...], bx_ref[...], sp)

        ge, dge = _gelu_parts(z_ref[:, D_LRU:2 * D_LRU])
        dy_lru = dy_ref[:, 0:D_LRU]
        dz_ref[:, D_LRU:2 * D_LRU] = (dy_lru * hs_ref[...] * dge).astype(BF16)

        ext_a[0:tile, :] = a
        a_s[...] = ext_a[pl.ds(1, tile), :]
        b_s[...] = dy_lru * ge
        l_car[0:1, :] = _scan_rows(a_s, b_s, lam_s, l_car[0:1, :], rows=tile, reverse=True)
        ext_a[tile:tile + HALO, :] = ext_a[0:HALO, :]
        lv = lam_s[...]

        da = lv * ext_h[pl.ds(HALO - 1, tile), :]
        d_mult = lv * ig * lx
        d_i = lv * mult * lx
        dlx = lv * mult * ig
        dlog_a = da * a - d_mult * (a * a) / mult
        d_r = dlog_a * (-RG_C * sp)
        dpre_r = d_r * r * (1.0 - r)
        dpre_i = d_i * ig * (1.0 - ig)
        dvec_ref[1:2, :] += jnp.sum(dpre_r, axis=0, keepdims=True)
        dvec_ref[2:3, :] += jnp.sum(dpre_i, axis=0, keepdims=True)
        dvec_ref[3:4, :] += jnp.sum(dlog_a * (-RG_C * r), axis=0, keepdims=True)
        dpr_b = dpre_r.astype(BF16)
        dpi_b = dpre_i.astype(BF16)
        back = []
        for g in range(N_GROUPS):
            cols = slice(g * LRU_GROUP, (g + 1) * LRU_GROUP)
            dwa_ref[g] += _dot(lxb[:, cols], dpr_b[:, cols], TN)
            dwx_ref[g] += _dot(lxb[:, cols], dpi_b[:, cols], TN)
            back.append(_dot(dpr_b[:, cols], wa_ref[g], NT) + _dot(dpi_b[:, cols], wx_ref[g], NT))
        dlx = dlx + jnp.concatenate(back, axis=1)
        dvec_ref[0:1, :] += jnp.sum(dlx, axis=0, keepdims=True)

        ext_dlx[0:tile, :] = dlx
        for k in range(4):
            dcw_ref[k:k + 1, :] += jnp.sum(dlx * ext_lx[pl.ds(HALO - 3 + k, tile), :], axis=0, keepdims=True)
        dlxp = cw_ref[3:4, :] * dlx
        for k in range(3):
            dlxp = dlxp + cw_ref[k:k + 1, :] * ext_dlx[pl.ds(3 - k, tile), :]
        dz_ref[:, 0:D_LRU] = dlxp.astype(BF16)
        ext_dlx[tile:tile + HALO, :] = ext_dlx[0:HALO, :]

        dy_sc = dy_ref[:, D_LRU:D_MIX]
        dz_ref[:, 2 * D_LRU:2 * D_LRU + D_SC] = (dy_sc * cq).astype(BF16)
        dcq = dy_sc * sb
        ext_dcq[0:tile, :] = dcq
        for k in range(3):
            dscw_ref[k:k + 1, :] += jnp.sum(dcq * ext_q[pl.ds(HALO - 2 + k, tile), :], axis=0, keepdims=True)
        dq = scw_ref[2:3, :] * dcq
        for k in range(2):
            dq = dq + scw_ref[k:k + 1, :] * ext_dcq[pl.ds(2 - k, tile), :]
        dz_ref[:, 2 * D_LRU + D_SC:2 * D_LRU + 2 * D_SC] = (dq * sx).astype(BF16)
        dz_ref[:, 2 * D_LRU + 2 * D_SC:D_IN] = (dq * sc).astype(BF16)
        ext_dcq[tile:tile + HALO, :] = ext_dcq[0:HALO, :]

        @pl.when(i == n_tiles - 1)
        def _():
            dvec_ref[3:4, :] = dvec_ref[3:4, :] * (-_sigmoid(-lam_ref[...]))

    rev = lambda i: n_tiles - 1 - i
    prev8 = lambda i: jnp.maximum(rev(i) * per8 - 1, 0)
    full = lambda shape: pl.BlockSpec(shape, lambda i: (0,) * len(shape))
    return pl.pallas_call(
        body, name=name, grid=(n_tiles,),
        in_specs=[pl.BlockSpec((tile, D_IN), lambda i: (rev(i), 0)),
                  pl.BlockSpec((HALO, D_IN), lambda i: (prev8(i), 0)),
                  pl.BlockSpec((tile, D_LRU), lambda i: (rev(i), 0)),
                  pl.BlockSpec((HALO, D_LRU), lambda i: (prev8(i), 0)),
                  pl.BlockSpec((tile, D_MIX), lambda i: (rev(i), 0)),
                  full((4, D_LRU)), full((1, D_LRU)),
                  full((N_GROUPS, LRU_GROUP, LRU_GROUP)), full((N_GROUPS, LRU_GROUP, LRU_GROUP)),
                  full((1, D_LRU)), full((1, D_LRU)), full((1, D_LRU)), full((3, D_SC)),
                  pl.BlockSpec(memory_space=pl.ANY)],
        out_specs=[pl.BlockSpec((tile, D_IN), lambda i: (rev(i), 0)),
                   full((8, D_LRU)), full((8, D_LRU)),
                   full((N_GROUPS, LRU_GROUP, LRU_GROUP)), full((N_GROUPS, LRU_GROUP, LRU_GROUP)),
                   full((8, D_SC))],
        out_shape=[jax.ShapeDtypeStruct((s, D_IN), BF16),
                   jax.ShapeDtypeStruct((8, D_LRU), F32), jax.ShapeDtypeStruct((8, D_LRU), F32),
                   jax.ShapeDtypeStruct((N_GROUPS, LRU_GROUP, LRU_GROUP), F32),
                   jax.ShapeDtypeStruct((N_GROUPS, LRU_GROUP, LRU_GROUP), F32),
                   jax.ShapeDtypeStruct((8, D_SC), F32)],
        scratch_shapes=[pltpu.VMEM((tile + HALO, D_LRU), F32), pltpu.VMEM((tile + HALO, D_SC), F32),
                        pltpu.VMEM((tile + HALO, D_LRU), F32), pltpu.VMEM((tile + HALO, D_LRU), F32),
                        pltpu.VMEM((tile + HALO, D_LRU), F32), pltpu.VMEM((tile + HALO, D_SC), F32),
                        pltpu.VMEM((tile, D_LRU), F32), pltpu.VMEM((tile, D_LRU), F32),
                        pltpu.VMEM((tile, D_LRU), F32), pltpu.VMEM((8, D_LRU), F32)],
        compiler_params=_params("arbitrary"),
    )(z, z, hs, hs, dy, cw, cb.reshape(1, -1), wa_bd, wx_bd, ba.reshape(1, -1), bx.reshape(1, -1),
      lam.reshape(1, -1), scw, _behind(token))


FFN_ROWS = 16


def _conv3_rows(w_ref, ext_ref, half, row, rows):
    acc = w_ref[half, 0:1, :] * ext_ref[half, pl.ds(row - 2, rows), :]
    for k in range(1, 3):
        acc = acc + w_ref[half, k:k + 1, :] * ext_ref[half, pl.ds(row - 2 + k, rows), :]
    return acc


HALO_B = 16


def _ffn_block_fwd(x2, h2, w_up_b, fcw, w_down, *, tile, name):
    s = h2.shape[0]
    nb = w_up_b.shape[2]
    blocks = D_FF // nb
    per16 = tile // HALO_B

    def body(h_ref, hp_ref, wg_ref, wu_ref, fw_ref, wd_ref, x2_ref, x3_ref, act_ref, p_ref, ext_p, acc_ref):
        i = pl.program_id(0)
        j = pl.program_id(1)
        keep = jnp.where(i == 0, 0.0, 1.0)
        lhs = jnp.concatenate([hp_ref[...], h_ref[...]], axis=0)
        for half, w_ref in ((0, wg_ref), (1, wu_ref)):
            pe = _dot(lhs, w_ref[...], NN)
            ext_p[half, 0:HALO_B, :] = pe[0:HALO_B] * keep
            ext_p[half, HALO_B:HALO_B + tile, :] = pe[HALO_B:]
            p_ref[half, :, :] = pe[HALO_B:].astype(BF16)
        for r0 in range(0, tile, FFN_ROWS):
            u = [_conv3_rows(fw_ref, ext_p, half, HALO_B + r0, FFN_ROWS) for half in range(2)]
            act_ref[r0:r0 + FFN_ROWS, :] = (_gelu(u[0]) * u[1]).astype(BF16)
        contrib = _dot(act_ref[...], wd_ref[...], NN)

        @pl.when(j == 0)
        def _():
            acc_ref[...] = contrib

        @pl.when(jnp.logical_and(j > 0, j < blocks - 1))
        def _():
            acc_ref[...] += contrib

        @pl.when(j == blocks - 1)
        def _():
            x3_ref[...] = x2_ref[...] + acc_ref[...] + contrib

    return pl.pallas_call(
        body, name=name, grid=(s // tile, blocks),
        in_specs=[pl.BlockSpec((tile, D_MODEL), lambda i, j: (i, 0)),
                  pl.BlockSpec((HALO_B, D_MODEL), lambda i, j: (jnp.maximum(i * per16 - 1, 0), 0)),
                  pl.BlockSpec((None, D_MODEL, nb), lambda i, j: (j, 0, 0)),
                  pl.BlockSpec((None, D_MODEL, nb), lambda i, j: (j + blocks, 0, 0)),
                  pl.BlockSpec((2, 3, nb), lambda i, j: (0, 0, j)),
                  pl.BlockSpec((nb, D_MODEL), lambda i, j: (j, 0)),
                  pl.BlockSpec((tile, D_MODEL), lambda i, j: (i, 0))],
        out_specs=[pl.BlockSpec((tile, D_MODEL), lambda i, j: (i, 0)),
                   pl.BlockSpec((tile, nb), lambda i, j: (i, j)),
                   pl.BlockSpec((2, tile, nb), lambda i, j: (0, i, j))],
        out_shape=[jax.ShapeDtypeStruct((s, D_MODEL), F32), jax.ShapeDtypeStruct((s, D_FF), BF16),
                   jax.ShapeDtypeStruct((2, s, D_FF), BF16)],
        scratch_shapes=[pltpu.VMEM((2, tile + HALO_B, nb), F32), pltpu.VMEM((tile, D_MODEL), F32)],
        compiler_params=_params("parallel", "arbitrary"),
    )(h2, h2, w_up_b, w_up_b, fcw, w_down, x2)


def _ffn_block_bwd(dx3, dx3b, p, x2, g2, w_up_b, fcw, w_down, *, tile, name, token=None):
    s = x2.shape[0]
    nb = w_up_b.shape[2]
    blocks = D_FF // nb
    n_tiles = s // tile
    per16 = tile // HALO_B
    last16 = s // HALO_B - 1

    def body(dxb_ref, dxbn_ref, wd_ref, p_ref, pp_ref, pn_ref, fw_ref, wg_ref, wu_ref, x2_ref, g_ref, dx3_ref,
             token_ref, dx2_ref, dx2b_ref, dg_ref, dp_ref, dw_ref, ext_p, ext_du, da_s, acc_w, acc_dh):
        i = pl.program_id(0)
        j = pl.program_id(1)

        @pl.when(jnp.logical_and(i == 0, j == 0))
        def _():
            acc_w[...] = jnp.zeros_like(acc_w)
            dg_ref[...] = jnp.zeros_like(dg_ref)

        keep_prev = jnp.where(i == 0, 0.0, 1.0)
        keep_next = jnp.where(i == n_tiles - 1, 0.0, 1.0)
        lhs = jnp.concatenate([dxb_ref[...], dxbn_ref[...]], axis=0)
        da_s[...] = _dot(lhs, wd_ref[...], NT)
        ext_p[:, 0:HALO_B, :] = pp_ref[...].astype(F32) * keep_prev
        ext_p[:, HALO_B:HALO_B + tile, :] = p_ref[...].astype(F32)
        ext_p[:, HALO_B + tile:2 * HALO_B + tile, :] = pn_ref[...].astype(F32) * keep_next

        def du_rows(r0, rows, da):
            u = [_conv3_rows(fw_ref, ext_p, half, HALO_B + r0, rows) for half in range(2)]
            ge, dge = _gelu_parts(u[0])
            du = (da * u[1] * dge, da * ge)
            for half in range(2):
                ext_du[half, r0:r0 + rows, :] = du[half]
            return du

        du_rows(tile, HALO, da_s[tile:tile + HALO, :] * keep_next)
        for r0 in range(tile - FFN_ROWS, -1, -FFN_ROWS):
            du = du_rows(r0, FFN_ROWS, da_s[r0:r0 + FFN_ROWS, :])
            for half in range(2):
                below = [du[half], ext_du[half, pl.ds(r0 + 1, FFN_ROWS), :], ext_du[half, pl.ds(r0 + 2, FFN_ROWS), :]]
                acc = fw_ref[half, 2:3, :] * below[0]
                for k in range(2):
                    acc = acc + fw_ref[half, k:k + 1, :] * below[2 - k]
                dp_ref[half, r0:r0 + FFN_ROWS, :] = acc.astype(BF16)
                p_rows = ext_p[half, HALO_B + r0:HALO_B + r0 + FFN_ROWS, :]
                for k in range(3):
                    prod = below[2 - k] * p_rows
                    acc_w[j, half, k] += sum(prod[q:q + 8] for q in range(0, FFN_ROWS, 8))

        contrib = _dot(dp_ref[0], wg_ref[...], NT) + _dot(dp_ref[1], wu_ref[...], NT)

        @pl.when(j == 0)
        def _():
            acc_dh[...] = contrib

        @pl.when(jnp.logical_and(j > 0, j < blocks - 1))
        def _():
            acc_dh[...] += contrib

        @pl.when(j == blocks - 1)
        def _():
            dh = acc_dh[...] + contrib
            xv = x2_ref[...]
            rstd = _rms(xv)
            n = xv * rstd
            dn = dh * g_ref[...]
            dx = dx3_ref[...] + rstd * (dn - n * jnp.mean(dn * n, axis=-1, keepdims=True))
            dx2_ref[...] = dx
            dx2b_ref[...] = dx.astype(BF16)
            dg_ref[0:1, :] += jnp.sum(dh * n, axis=0, keepdims=True)

        @pl.when(jnp.logical_and(i == n_tiles - 1, j == blocks - 1))
        def _():
            dw_ref[...] = jnp.zeros_like(dw_ref)
            for jj in range(blocks):
                for half in range(2):
                    for k in range(3):
                        dw_ref[half, k:k + 1, jj * nb:(jj + 1) * nb] = jnp.sum(acc_w[jj, half, k], axis=0, keepdims=True)

    next16 = lambda i: jnp.minimum((i + 1) * per16, last16)
    prev16 = lambda i: jnp.maximum(i * per16 - 1, 0)
    return pl.pallas_call(
        body, name=name, grid=(n_tiles, blocks),
        in_specs=[pl.BlockSpec((tile, D_MODEL), lambda i, j: (i, 0)),
                  pl.BlockSpec((HALO_B, D_MODEL), lambda i, j: (next16(i), 0)),
                  pl.BlockSpec((nb, D_MODEL), lambda i, j: (j, 0)),
                  pl.BlockSpec((2, tile, nb), lambda i, j: (0, i, j)),
                  pl.BlockSpec((2, HALO_B, nb), lambda i, j: (0, prev16(i), j)),
                  pl.BlockSpec((2, HALO_B, nb), lambda i, j: (0, next16(i), j)),
                  pl.BlockSpec((2, 3, nb), lambda i, j: (0, 0, j)),
                  pl.BlockSpec((None, D_MODEL, nb), lambda i, j: (j, 0, 0)),
                  pl.BlockSpec((None, D_MODEL, nb), lambda i, j: (j + blocks, 0, 0)),
                  pl.BlockSpec((tile, D_MODEL), lambda i, j: (i, 0)),
                  pl.BlockSpec((1, D_MODEL), lambda i, j: (0, 0)),
                  pl.BlockSpec((tile, D_MODEL), lambda i, j: (i, 0)),
                  pl.BlockSpec(memory_space=pl.ANY)],
        out_specs=[pl.BlockSpec((tile, D_MODEL), lambda i, j: (i, 0)),
                   pl.BlockSpec((tile, D_MODEL), lambda i, j: (i, 0)),
                   pl.BlockSpec((8, D_MODEL), lambda i, j: (0, 0)),
                   pl.BlockSpec((2, tile, nb), lambda i, j: (0, i, j)),
                   pl.BlockSpec((2, 8, D_FF), lambda i, j: (0, 0, 0))],
        out_shape=[jax.ShapeDtypeStruct((s, D_MODEL), F32), jax.ShapeDtypeStruct((s, D_MODEL), BF16),
                   jax.ShapeDtypeStruct((8, D_MODEL), F32), jax.ShapeDtypeStruct((2, s, D_FF), BF16),
                   jax.ShapeDtypeStruct((2, 8, D_FF), F32)],
        scratch_shapes=[pltpu.VMEM((2, tile + 2 * HALO_B, nb), F32), pltpu.VMEM((2, tile + HALO, nb), F32),
                        pltpu.VMEM((tile + HALO_B, nb), F32), pltpu.VMEM((blocks, 2, 3, 8, nb), F32),
                        pltpu.VMEM((tile, D_MODEL), F32)],
        compiler_params=_params("arbitrary", "arbitrary"),
    )(dx3b, dx3b, w_down, p, p, p, fcw, w_up_b, w_up_b, x2, g2.reshape(1, -1), dx3, _behind(token))


def _adamw_math(w, g, m, v):
    m = ADAM_B1 * m + (1.0 - ADAM_B1) * g
    v = ADAM_B2 * v + (1.0 - ADAM_B2) * (g * g)
    m_hat = m / (1.0 - ADAM_B1 ** ADAM_STEP)
    v_hat = v / (1.0 - ADAM_B2 ** ADAM_STEP)
    delta = -ADAM_LR * (m_hat / (jnp.sqrt(v_hat) + ADAM_EPS) + ADAM_WD * w)
    return delta, m, v


def _adamw(w, g, m, v, *, name):
    rows, cols = w.shape
    tr = rows
    for cand in (512, 256, 128, 64, 32, 16, 8):
        if rows % cand == 0 and rows > cand:
            tr = cand
            break

    def body(w_ref, g_ref, m_ref, v_ref, d_ref, nm_ref, nv_ref):
        d, nm, nv = _adamw_math(w_ref[...], g_ref[...], m_ref[...], v_ref[...])
        d_ref[...] = d
        nm_ref[...] = nm
        nv_ref[...] = nv

    spec = pl.BlockSpec((tr, cols), lambda i: (i, 0))
    return pl.pallas_call(
        body, name=name, grid=(rows // tr,), in_specs=[spec] * 4, out_specs=[spec] * 3,
        out_shape=[jax.ShapeDtypeStruct((rows, cols), F32)] * 3,
        compiler_params=_params("parallel"),
    )(w, g, m, v)


def _sum_parts(parts, *, name):
    _, rows, cols = parts.shape
    tr = rows
    for cand in (256, 128, 64, 32, 16):
        if rows % cand == 0 and rows > cand:
            tr = cand
            break

    def body(p_ref, o_ref):
        acc = p_ref[0].astype(F32)
        for d in range(1, N_DEV):
            acc = acc + p_ref[d].astype(F32)
        o_ref[...] = acc

    return pl.pallas_call(
        body, name=name, grid=(rows // tr,),
        in_specs=[pl.BlockSpec((N_DEV, tr, cols), lambda i: (0, i, 0))],
        out_specs=pl.BlockSpec((tr, cols), lambda i: (i, 0)),
        out_shape=jax.ShapeDtypeStruct((rows, cols), F32),
        compiler_params=_params("parallel"),
    )(parts)


def _place():
    return lax.axis_index("x"), lax.axis_index("y"), lax.axis_index("c")


def _flip(v, bit):
    return 1 - v if bit else v


N_PEERS = N_DEV - 1


def _peer_copy(k, src_ref, land_ref, send_sem, recv_sem, gather):
    x, y, c = _place()
    my_id = 4 * x + 2 * y + c
    px, py, pc = _flip(x, k & 4), _flip(y, k & 2), _flip(c, k & 1)
    peer_id = 4 * px + 2 * py + pc
    return pltpu.make_async_remote_copy(
        src_ref=src_ref if gather else src_ref.at[peer_id], dst_ref=land_ref.at[my_id],
        send_sem=send_sem.at[k - 1], recv_sem=recv_sem.at[k - 1],
        device_id=(px, py, pc), device_id_type=MESH)


def _sequencer_copies(srcs, *, gather, name, collective_id, after):
    n = len(srcs)
    hbm = pltpu.MemorySpace.HBM
    src_refs = [jax.new_ref(s, memory_space=hbm) for s in srcs]
    land_refs = [jax.empty_ref(jax.ShapeDtypeStruct(((N_DEV,) + s.shape) if gather else s.shape, s.dtype),
                               memory_space=hbm) for s in srcs]
    token_in = jax.new_ref(jnp.zeros((8, 128), F32) if after is None else after, memory_space=hbm)
    token_out = jax.empty_ref(jax.ShapeDtypeStruct((8, 128), F32), memory_space=hbm)

    @pl.kernel(mesh=plsc.ScalarSubcoreMesh(axis_name="seq", num_cores=1), name=name,
               scratch_types=(pltpu.SemaphoreType.DMA((n, N_PEERS)), pltpu.SemaphoreType.DMA((n, N_PEERS)),
                              pltpu.SemaphoreType.DMA((n + 1,))),
               compiler_params=pltpu.CompilerParams(collective_id=collective_id))
    def launch(send_sems, recv_sems, local_sems):
        x, y, c = _place()
        my_id = 4 * x + 2 * y + c
        barrier = pltpu.get_barrier_semaphore()
        own = [pltpu.make_async_copy(src_refs[t] if gather else src_refs[t].at[my_id], land_refs[t].at[my_id],
                                     local_sems.at[t]) for t in range(n)]
        if gather:
            sibling = (x, y, 1 - c)
            chips = [(1 - x, y), (x, 1 - y), (1 - x, 1 - y)]
            for peer in [sibling] + [(*chip, c) for chip in chips]:
                pl.semaphore_signal(barrier, inc=1, device_id=peer, device_id_type=MESH)
            pl.semaphore_wait(barrier, 4)

            def copy(t, k, block, to, src=None):
                dst = land_refs[t].at[4 * block[0] + 2 * block[1] + block[2]]
                return pltpu.make_async_remote_copy(
                    src_ref=dst if src is None else src, dst_ref=dst,
                    send_sem=send_sems.at[t, k], recv_sem=recv_sems.at[t, k], device_id=to, device_id_type=MESH)

            for cp in own:
                cp.start()
            sends = []
            for t in range(n):
                sends.append(copy(t, 0, (x, y, c), sibling, src=src_refs[t]))
                sends += [copy(t, 1 + j, (x, y, c), (*chip, c), src=src_refs[t]) for j, chip in enumerate(chips)]
            for cp in sends:
                cp.start()
            for t in range(n):
                for j, chip in enumerate(chips):
                    copy(t, 1 + j, (*chip, c), (x, y, c)).wait_recv()
                    passed_on = copy(t, 4 + j, (*chip, c), sibling)
                    passed_on.start()
                    sends.append(passed_on)
            for t in range(n):
                copy(t, 0, sibling, (x, y, c)).wait_recv()
                for j, chip in enumerate(chips):
                    copy(t, 4 + j, (*chip, 1 - c), (x, y, c)).wait_recv()
            for cp in sends:
                cp.wait_send()
            for cp in own:
                cp.wait()
        else:
            for k in range(1, N_DEV):
                peer = (_flip(x, k & 4), _flip(y, k & 2), _flip(c, k & 1))
                pl.semaphore_signal(barrier, inc=1, device_id=peer, device_id_type=MESH)
            pl.semaphore_wait(barrier, N_PEERS)
            for cp in own:
                cp.start()
            copies = [_peer_copy(k, src_refs[t], land_refs[t], send_sems.at[t], recv_sems.at[t], gather)
                      for t in range(n) for k in range(1, N_DEV)]
            for cp in copies:
                cp.start()
            for cp in own:
                cp.wait()
            for cp in copies:
                cp.wait()
        passed = pltpu.make_async_copy(token_in, token_out, local_sems.at[n])
        passed.start()
        passed.wait()

    launch()
    return [ref[...] for ref in land_refs], token_out[...]


def _all_reduce_small(buf, *, name):
    _, rows, lanes = buf.shape

    def body(in_ref, out_ref, parts, send_sems, recv_sems):
        x, y, c = _place()
        my_id = 4 * x + 2 * y + c
        peers = []
        for k in range(1, N_DEV):
            px, py, pc = _flip(x, k & 4), _flip(y, k & 2), _flip(c, k & 1)
            peers.append(((px, py, pc), 4 * px + 2 * py + pc))
        scatter = [pltpu.make_async_remote_copy(
            src_ref=in_ref.at[pid], dst_ref=parts.at[my_id],
            send_sem=send_sems.at[0, k], recv_sem=recv_sems.at[0, k],
            device_id=peer, device_id_type=MESH) for k, (peer, pid) in enumerate(peers)]
        for cp in scatter:
            cp.start()
        parts[my_id] = in_ref[my_id]
        for cp in scatter:
            cp.wait()
        total = parts[0]
        for d in range(1, N_DEV):
            total = total + parts[d]
        out_ref[my_id] = total
        gather = [pltpu.make_async_remote_copy(
            src_ref=out_ref.at[my_id], dst_ref=out_ref.at[my_id],
            send_sem=send_sems.at[1, k], recv_sem=recv_sems.at[1, k],
            device_id=peer, device_id_type=MESH) for k, (peer, pid) in enumerate(peers)]
        for cp in gather:
            cp.start()
        for k, (peer, pid) in enumerate(peers):
            pltpu.make_async_remote_copy(
                src_ref=out_ref.at[pid], dst_ref=out_ref.at[pid],
                send_sem=send_sems.at[1, k], recv_sem=recv_sems.at[1, k],
                device_id=peer, device_id_type=MESH).wait()

    vmem = pl.BlockSpec(memory_space=pltpu.VMEM)
    return pl.pallas_call(
        body, name=name, in_specs=[vmem], out_specs=vmem,
        out_shape=jax.ShapeDtypeStruct(buf.shape, F32),
        scratch_shapes=[pltpu.VMEM(buf.shape, F32),
                        pltpu.SemaphoreType.DMA((2, 7)), pltpu.SemaphoreType.DMA((2, 7))],
        compiler_params=pltpu.CompilerParams(vmem_limit_bytes=VMEM_LIMIT),
    )(buf)


TM = 512
TMM = 1024
TKW = 2048
MIX_TILE = 128
FFN_BLOCK_TILE = 512


def _block_diag(w):
    wg = w.reshape(N_GROUPS, HEADS_PER_GROUP, LRU_HEAD_DIM, LRU_HEAD_DIM)
    eye = jnp.eye(HEADS_PER_GROUP, dtype=w.dtype)
    bd = wg[:, :, :, None, :] * eye[None, :, None, :, None]
    return bd.reshape(N_GROUPS, LRU_GROUP, LRU_GROUP).astype(BF16)


def _head_blocks(bd):
    b5 = bd.reshape(N_GROUPS, HEADS_PER_GROUP, LRU_HEAD_DIM, HEADS_PER_GROUP, LRU_HEAD_DIM)
    blocks = [b5[:, h, :, h, :] for h in range(HEADS_PER_GROUP)]
    return jnp.stack(blocks, axis=1).reshape(LRU_HEADS, LRU_HEAD_DIM, LRU_HEAD_DIM)


def _w(lw, key, after):
    value = lw[key]
    return value(after) if callable(value) else value


def _layer_fwd(x, lw, tag):
    sv_rows = x.shape[0]
    h1 = _norm_fwd(x, lw["g1"], tm=TM, name=f"norm1_fwd_{tag}")
    z = _mm_nt(h1, _w(lw, "w_in_t", h1), tm=min(TMM, sv_rows), tn=896, tk=D_MODEL, out_dtype=F32, name=f"in_proj_{tag}")
    y_mix, hs = _mixer_fwd(z, _w(lw, "cw", z), lw["cb"], lw["wa_bd"], lw["wx_bd"], lw["ba"], lw["bx"], lw["lam"],
                           _w(lw, "scw", z), tile=MIX_TILE, name=f"mixer_fwd_{tag}")
    x2 = _mm_nn(y_mix, _w(lw, "w_out", y_mix), tm=min(TMM, sv_rows), tn=D_MODEL, tk=D_MIX, out_dtype=F32, name=f"out_proj_{tag}",
                residual=x)
    h2 = _norm_fwd(x2, lw["g2"], tm=TM, name=f"norm2_fwd_{tag}")
    x3, act, p = _ffn_block_fwd(x2, h2, _w(lw, "w_up_b", h2), _w(lw, "fcw", h2), _w(lw, "w_down", h2),
                                tile=min(FFN_BLOCK_TILE, sv_rows), name=f"ffn_fwd_{tag}")
    saved = dict(x=x, h1=h1, z=z, y_mix=y_mix, hs=hs, x2=x2, h2=h2, p=p, act=act)
    return x3, saved


def _layer_bwd(dx3, dx3b, lw, sv, tag, put):
    sv_rows = dx3.shape[0]
    w_in_t, w_out, w_up_b, w_down = (_w(lw, k, dx3) for k in ("w_in_t", "w_out", "w_up_b", "w_down"))
    cw, scw, fcw = (_w(lw, k, dx3) for k in ("cw", "scw", "fcw"))
    g_down = _mm_tn(sv["act"], dx3b, tm=1024, tn=D_MODEL, tk=min(TKW, sv_rows), out_dtype=BF16, name=f"down_bwd_w_{tag}")
    dx2, dx2b, dg2, dp, dfcw = _ffn_block_bwd(dx3, dx3b, sv["p"], sv["x2"], lw["g2"], w_up_b, fcw, w_down,
                                              tile=min(FFN_BLOCK_TILE, sv_rows), name=f"ffn_bwd_{tag}",
                                              token=put("w_down", g_down))
    g_up = _mm_up_bwd_w(sv["h2"], dp, tm=D_MODEL, tk=min(TKW, sv_rows), name=f"up_bwd_w_{tag}")
    dy = _mm_nt(dx2b, w_out, tm=min(TMM, sv_rows), tn=768, tk=D_MODEL, out_dtype=F32, name=f"out_bwd_x_{tag}")
    g_out = _mm_tn(sv["y_mix"], dx2b, tm=768, tn=D_MODEL, tk=min(TKW, sv_rows), out_dtype=BF16, name=f"out_bwd_w_{tag}",
                   token=put("w_up_b", g_up))
    out_token = put("w_out", g_out)
    dz, dcw, dvec, dwa, dwx, dscw = _mixer_bwd(
        sv["z"], sv["hs"], dy, cw, lw["cb"], lw["wa_bd"], lw["wx_bd"], lw["ba"], lw["bx"], lw["lam"],
        scw, tile=MIX_TILE, name=f"mixer_bwd_{tag}", token=out_token)
    dh1 = _mm_nn(dz, w_in_t, tm=min(TMM, sv_rows), tn=D_MODEL, tk=896, out_dtype=F32, name=f"in_bwd_x_{tag}")
    g_in_t = _mm_tn(dz, sv["h1"], tm=896, tn=D_MODEL, tk=min(TKW, sv_rows), out_dtype=BF16, name=f"in_bwd_w_{tag}")
    dx, dxb, dg1 = _norm_bwd(dh1, sv["x"], lw["g1"], dx2, tm=TM, name=f"norm1_bwd_{tag}",
                             token=put("w_in_t", g_in_t))
    small = dict(norm1_g=dg1[0], lru_conv_w=dcw[0:4], lru_conv_b=dvec[0], lru_wa=_head_blocks(dwa),
                 lru_ba=dvec[1], lru_wx=_head_blocks(dwx), lru_bx=dvec[2], lru_lambda=dvec[3],
                 sc_conv_w=dscw[0:3], norm2_g=dg2[0], ffn_conv_w=dfcw[:, 0:3, :])
    return dx, dxb, small


SMALL_ORDER = ("norm1_g", "lru_conv_w", "lru_conv_b", "lru_wa", "lru_ba", "lru_wx", "lru_bx", "lru_lambda",
               "sc_conv_w", "norm2_g", "ffn_conv_w")


def _local_step(x, tgt, layers, final_g, put):
    saved = []
    h = x
    for l in range(DEPTH):
        h, sv = _layer_fwd(h, layers[l], f"l{l}")
        saved.append(sv)
    loss_blk, dx, dxb, dgf = _loss_head(h, final_g, tgt, tm=TM, name="loss_head")
    smalls = [None] * DEPTH
    for l in reversed(range(DEPTH)):
        dx, dxb, smalls[l] = _layer_bwd(dx, dxb, layers[l], saved[l], f"l{l}", functools.partial(put, l))
    return loss_blk[0, 0], dx, smalls, dgf[0]


def kernel(x, norm1_g, w_in, lru_conv_w, lru_conv_b, lru_wa, lru_ba, lru_wx, lru_bx, lru_lambda, sc_conv_w, w_out, norm2_g, w_up, ffn_conv_w, w_down, final_g, loss_target, m_norm1_g, m_w_in, m_lru_conv_w, m_lru_conv_b, m_lru_wa, m_lru_ba, m_lru_wx, m_lru_bx, m_lru_lambda, m_sc_conv_w, m_w_out, m_norm2_g, m_w_up, m_ffn_conv_w, m_w_down, m_final_g, v_norm1_g, v_w_in, v_lru_conv_w, v_lru_conv_b, v_lru_wa, v_lru_ba, v_lru_wx, v_lru_bx, v_lru_lambda, v_sc_conv_w, v_w_out, v_norm2_g, v_w_up, v_ffn_conv_w, v_w_down, v_final_g):
    names = ["norm1_g", "w_in", "lru_conv_w", "lru_conv_b", "lru_wa", "lru_ba", "lru_wx", "lru_bx", "lru_lambda",
             "sc_conv_w", "w_out", "norm2_g", "w_up", "ffn_conv_w", "w_down", "final_g"]
    w = dict(zip(names, [norm1_g, w_in, lru_conv_w, lru_conv_b, lru_wa, lru_ba, lru_wx, lru_bx, lru_lambda,
                         sc_conv_w, w_out, norm2_g, w_up, ffn_conv_w, w_down, final_g]))
    m = dict(zip(names, [m_norm1_g, m_w_in, m_lru_conv_w, m_lru_conv_b, m_lru_wa, m_lru_ba, m_lru_wx, m_lru_bx,
                         m_lru_lambda, m_sc_conv_w, m_w_out, m_norm2_g, m_w_up, m_ffn_conv_w, m_w_down, m_final_g]))
    v = dict(zip(names, [v_norm1_g, v_w_in, v_lru_conv_w, v_lru_conv_b, v_lru_wa, v_lru_ba, v_lru_wx, v_lru_bx,
                         v_lru_lambda, v_sc_conv_w, v_w_out, v_norm2_g, v_w_up, v_ffn_conv_w, v_w_down, v_final_g]))
    my_id = 4 * lax.axis_index("x") + 2 * lax.axis_index("y") + lax.axis_index("c")

    taps = jnp.zeros((DEPTH, 16, 768), F32)
    taps = taps.at[:, 0:4, 0:128].set(lru_conv_w).at[:, 4:7, 0:64].set(sc_conv_w).at[:, 8:11, :].set(ffn_conv_w)
    shards = {}
    for l in range(DEPTH):
        shards[f"w_in_t{l}"] = jnp.swapaxes(w_in[l], 0, 1).astype(BF16)
        if l == 0:
            shards["taps"] = taps.reshape(DEPTH * 16, 768)
        shards[f"w_out{l}"] = w_out[l].astype(BF16)
        shards[f"w_up_b{l}"] = w_up[l].astype(BF16)
        shards[f"w_down{l}"] = w_down[l].astype(BF16)
    ids = iter(range(16))
    got = {}
    chain = [None]
    for group in (("w_in_t0", "taps"), ("w_out0",), ("w_up_b0",), ("w_down0",),
                  ("w_in_t1",), ("w_out1",), ("w_up_b1",), ("w_down1",)):
        lands, chain[0] = _sequencer_copies([shards[k] for k in group], gather=True, name=f"gather_{group[0]}",
                                            collective_id=next(ids), after=None)
        got.update(zip(group, lands))

    def fetch(key, after):
        return got[key]

    def tap_rows(l, lo, hi, width, after):
        tl = fetch("taps", after).reshape(N_DEV, DEPTH, 16, 768)[:, l, lo:hi, 0:width]
        return jnp.transpose(tl, (1, 0, 2)).reshape(hi - lo, N_DEV * width)

    layers = []
    for l in range(DEPTH):
        layers.append(dict(
            g1=norm1_g[l], g2=norm2_g[l], cb=lru_conv_b[l], ba=lru_ba[l], bx=lru_bx[l], lam=lru_lambda[l],
            wa_bd=_block_diag(lru_wa[l]), wx_bd=_block_diag(lru_wx[l]),
            cw=functools.partial(tap_rows, l, 0, 4, 128), scw=functools.partial(tap_rows, l, 4, 7, 64),
            fcw=lambda after, l=l: tap_rows(l, 8, 11, 768, after).reshape(3, 2, D_FF).transpose(1, 0, 2),
            w_in_t=lambda after, l=l: fetch(f"w_in_t{l}", after).reshape(D_IN, D_MODEL),
            w_out=lambda after, l=l: fetch(f"w_out{l}", after).reshape(D_MIX, D_MODEL),
            w_up_b=lambda after, l=l: fetch(f"w_up_b{l}", after),
            w_down=lambda after, l=l: fetch(f"w_down{l}", after).reshape(D_FF, D_MODEL)))

    scatter_handles = {}

    def put(l, key, grad):
        blocks = grad if grad.ndim == 3 else grad.reshape(N_DEV, grad.shape[0] // N_DEV, grad.shape[1])
        (scatter_handles[(l, key)],), chain[0] = _sequencer_copies(
            [blocks], gather=False, name=f"scatter_{key}{l}", collective_id=next(ids), after=chain[0])
        return blocks

    loss_local, dx, smalls, dgf = _local_step(x[0], loss_target[0], layers, final_g, put)
    loss = lax.psum(loss_local, ("x", "y", "c"))

    parts = []
    for l in range(DEPTH):
        for key in ("w_in_t", "w_out", "w_up_b", "w_down"):
            parts.append(scatter_handles[(l, key)])

    flat = [smalls[l][k].reshape(-1) for l in range(DEPTH) for k in SMALL_ORDER] + [dgf.reshape(-1)]
    sizes = [f.shape[0] for f in flat]
    total = sum(sizes)
    rows = -(-total // (N_DEV * 128 * 8)) * 8
    flat.append(jnp.zeros((N_DEV * rows * 128 - total,), F32))
    small_sum = _all_reduce_small(jnp.concatenate(flat).reshape(N_DEV, rows, 128), name="reduce_small").reshape(-1)
    small_g, off = [], 0
    for sz in sizes:
        small_g.append(small_sum[off:off + sz])
        off += sz
    gs = {}
    for l in range(DEPTH):
        for i, k in enumerate(SMALL_ORDER):
            gs.setdefault(k, []).append(small_g[l * len(SMALL_ORDER) + i])
    g_final = small_g[-1]

    grads = {}
    per_layer = {k: [] for k in ("w_in", "w_out", "w_up", "w_down")}
    for l in range(DEPTH):
        p_in, p_out, p_up, p_down = parts[4 * l:4 * l + 4]
        per_layer["w_in"].append(jnp.swapaxes(_sum_parts(p_in, name=f"sum_w_in_l{l}"), 0, 1))
        per_layer["w_out"].append(_sum_parts(p_out, name=f"sum_w_out_l{l}"))
        per_layer["w_up"].append(_sum_parts(p_up, name=f"sum_w_up_l{l}"))
        per_layer["w_down"].append(_sum_parts(p_down, name=f"sum_w_down_l{l}"))
    for k, lst in per_layer.items():
        grads[k] = jnp.stack(lst)
    for k in ("norm1_g", "lru_conv_b", "lru_ba", "lru_bx", "lru_lambda", "norm2_g"):
        grads[k] = jnp.stack(gs[k]).reshape(DEPTH, -1)
    for k in ("lru_wa", "lru_wx"):
        grads[k] = jnp.stack(gs[k]).reshape(DEPTH, LRU_HEADS, LRU_HEAD_DIM, LRU_HEAD_DIM)
    grads["final_g"] = g_final
    cw_full = jnp.stack(gs["lru_conv_w"]).reshape(DEPTH, 4, N_DEV, 128)
    grads["lru_conv_w"] = lax.dynamic_index_in_dim(cw_full, my_id, axis=2, keepdims=False)
    scw_full = jnp.stack(gs["sc_conv_w"]).reshape(DEPTH, 3, N_DEV, 64)
    grads["sc_conv_w"] = lax.dynamic_index_in_dim(scw_full, my_id, axis=2, keepdims=False)
    fcw_full = jnp.stack(gs["ffn_conv_w"]).reshape(DEPTH, 2, 3, D_FF).transpose(0, 2, 1, 3).reshape(DEPTH, 3, N_DEV, 768)
    grads["ffn_conv_w"] = lax.dynamic_index_in_dim(fcw_full, my_id, axis=2, keepdims=False)

    deltas, new_m, new_v = {}, {}, {}
    for k in names:
        shape = w[k].shape
        cols = shape[-1]
        as2d = lambda a: a.reshape(-1, cols)
        d, nm, nv = _adamw(as2d(w[k]), as2d(grads[k]), as2d(m[k]), as2d(v[k]), name=f"adamw_{k}")
        deltas[k], new_m[k], new_v[k] = d.reshape(shape), nm.reshape(shape), nv.reshape(shape)

    return (loss, dx[None], *[grads[k] for k in names], *[deltas[k] for k in names],
            *[new_m[k] for k in names], *[new_v[k] for k in names])
```

```python
import functools
import math

import jax
import jax.numpy as jnp
from jax import lax
from jax.experimental import pallas as pl
from jax.experimental.pallas import tpu as pltpu
from jax.experimental.pallas import tpu_sc as plsc

F32 = jnp.float32
BF16 = jnp.bfloat16

N_DEV = 8
DEPTH = 2
D_MODEL = 1024
D_LRU = 1024
D_SC = 512
D_MIX = D_LRU + D_SC
D_IN = 2 * D_LRU + 3 * D_SC
D_FF = 3072
LRU_HEADS = 16
LRU_HEAD_DIM = 64
LRU_GROUP = 256
N_GROUPS = D_LRU // LRU_GROUP
HEADS_PER_GROUP = LRU_GROUP // LRU_HEAD_DIM
RG_C = 8.0
EPS = 1e-6
HALO = 8

ADAM_LR = 0.001
ADAM_B1 = 0.9
ADAM_B2 = 0.999
ADAM_EPS = 1e-08
ADAM_WD = 0.01
ADAM_STEP = 10

GELU_C = math.sqrt(2.0 / math.pi)
GELU_A = 0.044715

VMEM_LIMIT = 56 * 1024 * 1024
MESH = pl.DeviceIdType.MESH


def _params(*sem):
    return pltpu.CompilerParams(dimension_semantics=tuple(sem) if sem else None,
                                vmem_limit_bytes=VMEM_LIMIT)


def _gelu_parts(x):
    x2 = x * x
    t = jnp.tanh(GELU_C * (x + GELU_A * x * x2))
    half = 0.5 * (1.0 + t)
    g = x * half
    dg = half + 0.5 * x * (1.0 - t * t) * (GELU_C * (1.0 + 3.0 * GELU_A * x2))
    return g, dg


def _gelu(x):
    t = jnp.tanh(GELU_C * (x + GELU_A * x * x * x))
    return 0.5 * x * (1.0 + t)


def _sigmoid(x):
    return 1.0 / (1.0 + jnp.exp(-x))


def _softplus(x):
    e = jnp.exp(-jnp.abs(x))
    u = 1.0 + e
    log1p_e = jnp.where(u == 1.0, e, jnp.log(u) * (e / (u - 1.0)))
    return jnp.maximum(x, 0.0) + log1p_e


def _rms(x):
    ms = jnp.mean(x * x, axis=-1, keepdims=True)
    return lax.rsqrt(ms + EPS)


def _dot(a, b, dims):
    return lax.dot_general(a, b, (dims, ((), ())), preferred_element_type=F32)


NN = ((1,), (0,))
NT = ((1,), (1,))
TN = ((0,), (0,))


def _matmul(a, b, *, dims, grid, a_spec, b_spec, o_spec, out_shape, acc_shape, name,
            residual=None, r_spec=None, token=None):
    nk = grid[2]

    def body(*refs):
        a_ref, b_ref = refs[0], refs[1]
        r_ref = refs[2] if residual is not None else None
        o_ref = refs[2 + (residual is not None) + (token is not None)]
        prod = _dot(a_ref[...].astype(BF16), b_ref[...].astype(BF16), dims)

        def finish(total):
            if r_ref is not None:
                total = total + r_ref[...]
            o_ref[...] = total.astype(o_ref.dtype)

        if nk == 1:
            finish(prod)
            return
        acc_ref = refs[-1]
        k = pl.program_id(2)

        @pl.when(k == 0)
        def _():
            acc_ref[...] = prod

        @pl.when(jnp.logical_and(k > 0, k < nk - 1))
        def _():
            acc_ref[...] += prod

        @pl.when(k == nk - 1)
        def _():
            finish(acc_ref[...] + prod)

    in_specs = [a_spec, b_spec]
    args = [a, b]
    if residual is not None:
        in_specs.append(r_spec)
        args.append(residual)
    if token is not None:
        in_specs.append(pl.BlockSpec(memory_space=pl.ANY))
        args.append(token)
    return pl.pallas_call(
        body, name=name, grid=grid, in_specs=in_specs, out_specs=o_spec, out_shape=out_shape,
        scratch_shapes=[pltpu.VMEM(acc_shape, F32)] if nk > 1 else [],
        compiler_params=_params("parallel", "parallel", "arbitrary"),
    )(*args)


def _mm_nn(a, b, *, tm, tn, tk, out_dtype, name, residual=None, token=None):
    m, kd = a.shape
    n = b.shape[1]
    return _matmul(
        a, b, dims=NN, grid=(m // tm, n // tn, kd // tk),
        a_spec=pl.BlockSpec((tm, tk), lambda i, j, k: (i, k)),
        b_spec=pl.BlockSpec((tk, tn), lambda i, j, k: (k, j)),
        o_spec=pl.BlockSpec((tm, tn), lambda i, j, k: (i, j)),
        out_shape=jax.ShapeDtypeStruct((m, n), out_dtype), acc_shape=(tm, tn), name=name,
        residual=residual, r_spec=pl.BlockSpec((tm, tn), lambda i, j, k: (i, j)), token=token)


def _mm_nt(a, b, *, tm, tn, tk, out_dtype, name):
    m, kd = a.shape
    n = b.shape[0]
    return _matmul(
        a, b, dims=NT, grid=(m // tm, n // tn, kd // tk),
        a_spec=pl.BlockSpec((tm, tk), lambda i, j, k: (i, k)),
        b_spec=pl.BlockSpec((tn, tk), lambda i, j, k: (j, k)),
        o_spec=pl.BlockSpec((tm, tn), lambda i, j, k: (i, j)),
        out_shape=jax.ShapeDtypeStruct((m, n), out_dtype), acc_shape=(tm, tn), name=name)


def _mm_tn(a, b, *, tm, tn, tk, out_dtype, name, token=None):
    kd, m = a.shape
    n = b.shape[1]
    return _matmul(
        a, b, dims=TN, grid=(m // tm, n // tn, kd // tk),
        a_spec=pl.BlockSpec((tk, tm), lambda i, j, k: (k, i)),
        b_spec=pl.BlockSpec((tk, tn), lambda i, j, k: (k, j)),
        o_spec=pl.BlockSpec((tm, tn), lambda i, j, k: (i, j)),
        out_shape=jax.ShapeDtypeStruct((m, n), out_dtype), acc_shape=(tm, tn), name=name, token=token)


def _mm_up_bwd_w(h2, dp, *, tm, tk, name):
    s = h2.shape[0]
    nb = D_FF * 2 // N_DEV
    per_half = D_FF // nb
    return _matmul(
        h2, dp, dims=TN, grid=(D_MODEL // tm, N_DEV, s // tk),
        a_spec=pl.BlockSpec((tk, tm), lambda i, j, k: (k, i)),
        b_spec=pl.BlockSpec((None, tk, nb), lambda i, j, k: (j // per_half, k, j % per_half)),
        o_spec=pl.BlockSpec((None, tm, nb), lambda i, j, k: (j, i, 0)),
        out_shape=jax.ShapeDtypeStruct((N_DEV, D_MODEL, nb), BF16), acc_shape=(tm, nb), name=name)


def _norm_fwd(x, g, *, tm, name):
    s, d = x.shape

    def body(x_ref, g_ref, h_ref):
        xv = x_ref[...]
        h_ref[...] = (xv * _rms(xv) * g_ref[...]).astype(BF16)

    return pl.pallas_call(
        body, name=name, grid=(s // tm,),
        in_specs=[pl.BlockSpec((tm, d), lambda i: (i, 0)), pl.BlockSpec((1, d), lambda i: (0, 0))],
        out_specs=pl.BlockSpec((tm, d), lambda i: (i, 0)),
        out_shape=jax.ShapeDtypeStruct((s, d), BF16),
        compiler_params=_params("parallel"),
    )(x, g.reshape(1, d))


def _behind(token):
    return jnp.zeros((8, 128), F32) if token is None else token


def _norm_bwd(dh, x, g, dres, *, tm, name, token=None):
    s, d = x.shape

    def body(dh_ref, x_ref, g_ref, dres_ref, token_ref, dx_ref, dxb_ref, dg_ref):
        @pl.when(pl.program_id(0) == 0)
        def _():
            dg_ref[...] = jnp.zeros_like(dg_ref)

        xv = x_ref[...]
        rstd = _rms(xv)
        n = xv * rstd
        dhv = dh_ref[...]
        dn = dhv * g_ref[...]
        dx = rstd * (dn - n * jnp.mean(dn * n, axis=-1, keepdims=True))
        dx = dres_ref[...] + dx
        dx_ref[...] = dx
        dxb_ref[...] = dx.astype(BF16)
        dg_ref[0:1, :] += jnp.sum(dhv * n, axis=0, keepdims=True)

    return pl.pallas_call(
        body, name=name, grid=(s // tm,),
        in_specs=[pl.BlockSpec((tm, d), lambda i: (i, 0)), pl.BlockSpec((tm, d), lambda i: (i, 0)),
                  pl.BlockSpec((1, d), lambda i: (0, 0)), pl.BlockSpec((tm, d), lambda i: (i, 0)),
                  pl.BlockSpec(memory_space=pl.ANY)],
        out_specs=[pl.BlockSpec((tm, d), lambda i: (i, 0)), pl.BlockSpec((tm, d), lambda i: (i, 0)),
                   pl.BlockSpec((8, d), lambda i: (0, 0))],
        out_shape=[jax.ShapeDtypeStruct((s, d), F32), jax.ShapeDtypeStruct((s, d), BF16),
                   jax.ShapeDtypeStruct((8, d), F32)],
        compiler_params=_params("arbitrary"),
    )(dh, x, g.reshape(1, d), dres, _behind(token))


def _loss_head(x, g, tgt, *, tm, name):
    s, d = x.shape

    def body(x_ref, g_ref, t_ref, loss_ref, dx_ref, dxb_ref, dg_ref):
        @pl.when(pl.program_id(0) == 0)
        def _():
            dg_ref[...] = jnp.zeros_like(dg_ref)
            loss_ref[...] = jnp.zeros_like(loss_ref)

        xv = x_ref[...]
        gv = g_ref[...]
        rstd = _rms(xv)
        n = xv * rstd
        e = n * gv - t_ref[...]
        part = 0.5 * jnp.sum(jnp.mean(e * e, axis=-1, keepdims=True), axis=0, keepdims=True)
        loss_ref[...] += jnp.broadcast_to(part, loss_ref.shape)
        dy = e * (1.0 / d)
        dn = dy * gv
        dx = rstd * (dn - n * jnp.mean(dn * n, axis=-1, keepdims=True))
        dx_ref[...] = dx
        dxb_ref[...] = dx.astype(BF16)
        dg_ref[0:1, :] += jnp.sum(dy * n, axis=0, keepdims=True)

    return pl.pallas_call(
        body, name=name, grid=(s // tm,),
        in_specs=[pl.BlockSpec((tm, d), lambda i: (i, 0)), pl.BlockSpec((1, d), lambda i: (0, 0)),
                  pl.BlockSpec((tm, d), lambda i: (i, 0))],
        out_specs=[pl.BlockSpec((8, 128), lambda i: (0, 0)), pl.BlockSpec((tm, d), lambda i: (i, 0)),
                   pl.BlockSpec((tm, d), lambda i: (i, 0)), pl.BlockSpec((8, d), lambda i: (0, 0))],
        out_shape=[jax.ShapeDtypeStruct((8, 128), F32), jax.ShapeDtypeStruct((s, d), F32),
                   jax.ShapeDtypeStruct((s, d), BF16), jax.ShapeDtypeStruct((8, d), F32)],
        compiler_params=_params("arbitrary"),
    )(x, g.reshape(1, d), tgt)


def _scan_rows(a_ref, b_ref, h_ref, carry, *, rows, reverse):
    width = a_ref.shape[1]
    n_chunks = rows // 8
    row = lax.broadcasted_iota(jnp.int32, (8, width), 0)

    def step(ci, carry):
        chunk = (n_chunks - 1 - ci) if reverse else ci
        off = pl.multiple_of(chunk * 8, 8)
        av = a_ref[pl.ds(off, 8), :]
        bv = b_ref[pl.ds(off, 8), :]
        for sh in (1, 2, 4):
            if reverse:
                a_sh = pltpu.roll(av, 8 - sh, 0)
                b_sh = pltpu.roll(bv, 8 - sh, 0)
                m = row < 8 - sh
            else:
                a_sh = pltpu.roll(av, sh, 0)
                b_sh = pltpu.roll(bv, sh, 0)
                m = row >= sh
            bv = jnp.where(m, av * b_sh + bv, bv)
            av = jnp.where(m, av * a_sh, av)
        h = av * carry + bv
        h_ref[pl.ds(off, 8), :] = h
        return h[0:1, :] if reverse else h[7:8, :]

    return lax.fori_loop(0, n_chunks, step, carry)


def _lru_gates(lx, wa_ref, wx_ref, ba, bx, sp):
    lxb = lx.astype(BF16)
    pre_r = jnp.concatenate(
        [_dot(lxb[:, g * LRU_GROUP:(g + 1) * LRU_GROUP], wa_ref[g], NN) for g in range(N_GROUPS)], axis=1)
    pre_i = jnp.concatenate(
        [_dot(lxb[:, g * LRU_GROUP:(g + 1) * LRU_GROUP], wx_ref[g], NN) for g in range(N_GROUPS)], axis=1)
    r = _sigmoid(pre_r + ba)
    ig = _sigmoid(pre_i + bx)
    log_a = (-RG_C * r) * sp
    a = jnp.exp(log_a)
    mult = jnp.sqrt(-jnp.tanh(log_a) * (a * a + 1.0))
    return lxb, r, ig, a, mult


def _mixer_fwd(z, cw, cb, wa_bd, wx_bd, ba, bx, lam, scw, *, tile, name):
    s = z.shape[0]
    n_tiles = s // tile

    def body(z_ref, cw_ref, cb_ref, wa_ref, wx_ref, ba_ref, bx_ref, lam_ref, scw_ref,
             y_ref, hs_ref, ext_lx, ext_q, a_s, b_s, h_car):
        i = pl.program_id(0)

        @pl.when(i == 0)
        def _():
            ext_lx[0:HALO, :] = jnp.zeros((HALO, D_LRU), F32)
            ext_q[0:HALO, :] = jnp.zeros((HALO, D_SC), F32)
            h_car[...] = jnp.zeros_like(h_car)

        ext_lx[HALO:HALO + tile, :] = z_ref[:, 0:D_LRU]
        ext_q[HALO:HALO + tile, :] = z_ref[:, 2 * D_LRU + D_SC:2 * D_LRU + 2 * D_SC] * z_ref[:, 2 * D_LRU + 2 * D_SC:D_IN]
        lx = cb_ref[...] + cw_ref[0:1, :] * ext_lx[pl.ds(HALO - 3, tile), :]
        for k in range(1, 4):
            lx = lx + cw_ref[k:k + 1, :] * ext_lx[pl.ds(HALO - 3 + k, tile), :]
        cq = scw_ref[0:1, :] * ext_q[pl.ds(HALO - 2, tile), :]
        for k in range(1, 3):
            cq = cq + scw_ref[k:k + 1, :] * ext_q[pl.ds(HALO - 2 + k, tile), :]
        ext_lx[0:HALO, :] = ext_lx[tile:tile + HALO, :]
        ext_q[0:HALO, :] = ext_q[tile:tile + HALO, :]

        sp = _softplus(-lam_ref[...])
        _, _, ig, a, mult = _lru_gates(lx, wa_ref, wx_ref, ba_ref[...], bx_ref[...], sp)
        a_s[...] = a
        b_s[...] = mult * (ig * lx)
        h_car[0:1, :] = _scan_rows(a_s, b_s, hs_ref, h_car[0:1, :], rows=tile, reverse=False)

        y_ref[:, 0:D_LRU] = (hs_ref[...] * _gelu(z_ref[:, D_LRU:2 * D_LRU])).astype(BF16)
        y_ref[:, D_LRU:D_MIX] = (z_ref[:, 2 * D_LRU:2 * D_LRU + D_SC] * cq).astype(BF16)

    full = lambda shape: pl.BlockSpec(shape, lambda i: (0,) * len(shape))
    return pl.pallas_call(
        body, name=name, grid=(n_tiles,),
        in_specs=[pl.BlockSpec((tile, D_IN), lambda i: (i, 0)),
                  full((4, D_LRU)), full((1, D_LRU)),
                  full((N_GROUPS, LRU_GROUP, LRU_GROUP)), full((N_GROUPS, LRU_GROUP, LRU_GROUP)),
                  full((1, D_LRU)), full((1, D_LRU)), full((1, D_LRU)), full((3, D_SC))],
        out_specs=[pl.BlockSpec((tile, D_MIX), lambda i: (i, 0)), pl.BlockSpec((tile, D_LRU), lambda i: (i, 0))],
        out_shape=[jax.ShapeDtypeStruct((s, D_MIX), BF16), jax.ShapeDtypeStruct((s, D_LRU), F32)],
        scratch_shapes=[pltpu.VMEM((tile + HALO, D_LRU), F32), pltpu.VMEM((tile + HALO, D_SC), F32),
                        pltpu.VMEM((tile, D_LRU), F32), pltpu.VMEM((tile, D_LRU), F32),
                        pltpu.VMEM((8, D_LRU), F32)],
        compiler_params=_params("arbitrary"),
    )(z, cw, cb.reshape(1, -1), wa_bd, wx_bd, ba.reshape(1, -1), bx.reshape(1, -1), lam.reshape(1, -1), scw)


def _mixer_bwd(z, hs, dy, cw, cb, wa_bd, wx_bd, ba, bx, lam, scw, *, tile, name, token=None):
    s = z.shape[0]
    n_tiles = s // tile
    per8 = tile // 8

    def body(z_ref, zp_ref, hs_ref, hsp_ref, dy_ref, cw_ref, cb_ref, wa_ref, wx_ref, ba_ref, bx_ref, lam_ref, scw_ref,
             token_ref, dz_ref, dcw_ref, dvec_ref, dwa_ref, dwx_ref, dscw_ref,
             ext_lx, ext_q, ext_h, ext_a, ext_dlx, ext_dcq, a_s, b_s, lam_s, l_car):
        i = pl.program_id(0)
        first_tile = i == n_tiles - 1

        @pl.when(i == 0)
        def _():
            for ref in (dcw_ref, dvec_ref, dwa_ref, dwx_ref, dscw_ref, l_car):
                ref[...] = jnp.zeros_like(ref)
            ext_a[tile:tile + HALO, :] = jnp.zeros((HALO, D_LRU), F32)
            ext_dlx[tile:tile + HALO, :] = jnp.zeros((HALO, D_LRU), F32)
            ext_dcq[tile:tile + HALO, :] = jnp.zeros((HALO, D_SC), F32)

        keep = jnp.where(first_tile, 0.0, 1.0)
        sb = z_ref[:, 2 * D_LRU:2 * D_LRU + D_SC]
        sc = z_ref[:, 2 * D_LRU + D_SC:2 * D_LRU + 2 * D_SC]
        sx = z_ref[:, 2 * D_LRU + 2 * D_SC:D_IN]
        ext_lx[0:HALO, :] = zp_ref[:, 0:D_LRU] * keep
        ext_lx[HALO:HALO + tile, :] = z_ref[:, 0:D_LRU]
        ext_q[0:HALO, :] = zp_ref[:, 2 * D_LRU + D_SC:2 * D_LRU + 2 * D_SC] * zp_ref[:, 2 * D_LRU + 2 * D_SC:D_IN] * keep
        ext_q[HALO:HALO + tile, :] = sc * sx
        ext_h[0:HALO, :] = hsp_ref[...] * keep
        ext_h[HALO:HALO + tile, :] = hs_ref[...]

        lx = cb_ref[...] + cw_ref[0:1, :] * ext_lx[pl.ds(HALO - 3, tile), :]
        for k in range(1, 4):
            lx = lx + cw_ref[k:k + 1, :] * ext_lx[pl.ds(HALO - 3 + k, tile), :]
        cq = scw_ref[0:1, :] * ext_q[pl.ds(HALO - 2, tile), :]
        for k in range(1, 3):
            cq = cq + scw_ref[k:k + 1, :] * ext_q[pl.ds(HALO - 2 + k, tile), :]

        sp = _softplus(-lam_ref[...])
        lxb, r, ig, a, mult = _lru_gates(lx, wa_ref, wx_ref, ba_ref[...], bx_ref[...], sp)

        ge, dge = _gelu_parts(z_ref[:, D_LRU:2 * D_LRU])
        dy_lru = dy_ref[:, 0:D_LRU]
        dz_ref[:, D_LRU:2 * D_LRU] = (dy_lru * hs_ref[...] * dge).astype(BF16)

        ext_a[0:tile, :] = a
        a_s[...] = ext_a[pl.ds(1, tile), :]
        b_s[...] = dy_lru * ge
        l_car[0:1, :] = _scan_rows(a_s, b_s, lam_s, l_car[0:1, :], rows=tile, reverse=True)
        ext_a[tile:tile + HALO, :] = ext_a[0:HALO, :]
        lv = lam_s[...]

        da = lv * ext_h[pl.ds(HALO - 1, tile), :]
        d_mult = lv * ig * lx
        d_i = lv * mult * lx
        dlx = lv * mult * ig
        dlog_a = da * a - d_mult * (a * a) / mult
        d_r = dlog_a * (-RG_C * sp)
        dpre_r = d_r * r * (1.0 - r)
        dpre_i = d_i * ig * (1.0 - ig)
        dvec_ref[1:2, :] += jnp.sum(dpre_r, axis=0, keepdims=True)
        dvec_ref[2:3, :] += jnp.sum(dpre_i, axis=0, keepdims=True)
        dvec_ref[3:4, :] += jnp.sum(dlog_a * (-RG_C * r), axis=0, keepdims=True)
        dpr_b = dpre_r.astype(BF16)
        dpi_b = dpre_i.astype(BF16)
        back = []
        for g in range(N_GROUPS):
            cols = slice(g * LRU_GROUP, (g + 1) * LRU_GROUP)
            dwa_ref[g] += _dot(lxb[:, cols], dpr_b[:, cols], TN)
            dwx_ref[g] += _dot(lxb[:, cols], dpi_b[:, cols], TN)
            back.append(_dot(dpr_b[:, cols], wa_ref[g], NT) + _dot(dpi_b[:, cols], wx_ref[g], NT))
        dlx = dlx + jnp.concatenate(back, axis=1)
        dvec_ref[0:1, :] += jnp.sum(dlx, axis=0, keepdims=True)

        ext_dlx[0:tile, :] = dlx
        for k in range(4):
            dcw_ref[k:k + 1, :] += jnp.sum(dlx * ext_lx[pl.ds(HALO - 3 + k, tile), :], axis=0, keepdims=True)
        dlxp = cw_ref[3:4, :] * dlx
        for k in range(3):
            dlxp = dlxp + cw_ref[k:k + 1, :] * ext_dlx[pl.ds(3 - k, tile), :]
        dz_ref[:, 0:D_LRU] = dlxp.astype(BF16)
        ext_dlx[tile:tile + HALO, :] = ext_dlx[0:HALO, :]

        dy_sc = dy_ref[:, D_LRU:D_MIX]
        dz_ref[:, 2 * D_LRU:2 * D_LRU + D_SC] = (dy_sc * cq).astype(BF16)
        dcq = dy_sc * sb
        ext_dcq[0:tile, :] = dcq
        for k in range(3):
            dscw_ref[k:k + 1, :] += jnp.sum(dcq * ext_q[pl.ds(HALO - 2 + k, tile), :], axis=0, keepdims=True)
        dq = scw_ref[2:3, :] * dcq
        for k in range(2):
            dq = dq + scw_ref[k:k + 1, :] * ext_dcq[pl.ds(2 - k, tile), :]
        dz_ref[:, 2 * D_LRU + D_SC:2 * D_LRU + 2 * D_SC] = (dq * sx).astype(BF16)
        dz_ref[:, 2 * D_LRU + 2 * D_SC:D_IN] = (dq * sc).astype(BF16)
        ext_dcq[tile:tile + HALO, :] = ext_dcq[0:HALO, :]

        @pl.when(i == n_tiles - 1)
        def _():
            dvec_ref[3:4, :] = dvec_ref[3:4, :] * (-_sigmoid(-lam_ref[...]))

    rev = lambda i: n_tiles - 1 - i
    prev8 = lambda i: jnp.maximum(rev(i) * per8 - 1, 0)
    full = lambda shape: pl.BlockSpec(shape, lambda i: (0,) * len(shape))
    return pl.pallas_call(
        body, name=name, grid=(n_tiles,),
        in_specs=[pl.BlockSpec((tile, D_IN), lambda i: (rev(i), 0)),
                  pl.BlockSpec((HALO, D_IN), lambda i: (prev8(i), 0)),
                  pl.BlockSpec((tile, D_LRU), lambda i: (rev(i), 0)),
                  pl.BlockSpec((HALO, D_LRU), lambda i: (prev8(i), 0)),
                  pl.BlockSpec((tile, D_MIX), lambda i: (rev(i), 0)),
                  full((4, D_LRU)), full((1, D_LRU)),
                  full((N_GROUPS, LRU_GROUP, LRU_GROUP)), full((N_GROUPS, LRU_GROUP, LRU_GROUP)),
                  full((1, D_LRU)), full((1, D_LRU)), full((1, D_LRU)), full((3, D_SC)),
                  pl.BlockSpec(memory_space=pl.ANY)],
        out_specs=[pl.BlockSpec((tile, D_IN), lambda i: (rev(i), 0)),
                   full((8, D_LRU)), full((8, D_LRU)),
                   full((N_GROUPS, LRU_GROUP, LRU_GROUP)), full((N_GROUPS, LRU_GROUP, LRU_GROUP)),
                   full((8, D_SC))],
        out_shape=[jax.ShapeDtypeStruct((s, D_IN), BF16),
                   jax.ShapeDtypeStruct((8, D_LRU), F32), jax.ShapeDtypeStruct((8, D_LRU), F32),
                   jax.ShapeDtypeStruct((N_GROUPS, LRU_GROUP, LRU_GROUP), F32),
                   jax.ShapeDtypeStruct((N_GROUPS, LRU_GROUP, LRU_GROUP), F32),
                   jax.ShapeDtypeStruct((8, D_SC), F32)],
        scratch_shapes=[pltpu.VMEM((tile + HALO, D_LRU), F32), pltpu.VMEM((tile + HALO, D_SC), F32),
                        pltpu.VMEM((tile + HALO, D_LRU), F32), pltpu.VMEM((tile + HALO, D_LRU), F32),
                        pltpu.VMEM((tile + HALO, D_LRU), F32), pltpu.VMEM((tile + HALO, D_SC), F32),
                        pltpu.VMEM((tile, D_LRU), F32), pltpu.VMEM((tile, D_LRU), F32),
                        pltpu.VMEM((tile, D_LRU), F32), pltpu.VMEM((8, D_LRU), F32)],
        compiler_params=_params("arbitrary"),
    )(z, z, hs, hs, dy, cw, cb.reshape(1, -1), wa_bd, wx_bd, ba.reshape(1, -1), bx.reshape(1, -1),
      lam.reshape(1, -1), scw, _behind(token))


FFN_ROWS = 16


def _spread_taps(fw_ref, taps):
    for half in range(2):
        for k in range(3):
            taps[half, k] = jnp.broadcast_to(fw_ref[half, k:k + 1, :], taps.shape[2:])


def _rows_from(first, second, start):
    stack = jnp.concatenate([first, second], axis=0)
    return pltpu.roll(stack, 2 * FFN_ROWS - start, 0)[0:FFN_ROWS]


def _conv3_rows(taps, ext_ref, half, row):
    before = ext_ref[half, row - FFN_ROWS:row, :]
    here = ext_ref[half, row:row + FFN_ROWS, :]
    acc = taps[half, 2] * here
    for k in range(2):
        acc = acc + taps[half, k] * _rows_from(before, here, FFN_ROWS - 2 + k)
    return acc


HALO_B = 16


def _ffn_block_fwd(x2, h2, w_up_b, fcw, w_down, *, tile, name):
    s = h2.shape[0]
    nb = w_up_b.shape[2]
    blocks = D_FF // nb
    per16 = tile // HALO_B

    def body(h_ref, hp_ref, wg_ref, wu_ref, fw_ref, wd_ref, x2_ref, x3_ref, act_ref, p_ref, u_ref, ext_p, acc_ref,
             taps):
        i = pl.program_id(0)
        j = pl.program_id(1)
        keep = jnp.where(i == 0, 0.0, 1.0)
        _spread_taps(fw_ref, taps)
        lhs = jnp.concatenate([hp_ref[...], h_ref[...]], axis=0)
        for half, w_ref in ((0, wg_ref), (1, wu_ref)):
            pe = _dot(lhs, w_ref[...], NN)
            ext_p[half, 0:HALO_B, :] = pe[0:HALO_B] * keep
            ext_p[half, HALO_B:HALO_B + tile, :] = pe[HALO_B:]
            p_ref[half, :, :] = pe[HALO_B:].astype(BF16)
        for r0 in range(0, tile, FFN_ROWS):
            u = [_conv3_rows(taps, ext_p, half, HALO_B + r0) for half in range(2)]
            for half in range(2):
                u_ref[half, r0:r0 + FFN_ROWS, :] = u[half].astype(BF16)
            act_ref[r0:r0 + FFN_ROWS, :] = (_gelu(u[0]) * u[1]).astype(BF16)
        contrib = _dot(act_ref[...], wd_ref[...], NN)

        @pl.when(j == 0)
        def _():
            acc_ref[...] = contrib

        @pl.when(jnp.logical_and(j > 0, j < blocks - 1))
        def _():
            acc_ref[...] += contrib

        @pl.when(j == blocks - 1)
        def _():
            x3_ref[...] = x2_ref[...] + acc_ref[...] + contrib

    return pl.pallas_call(
        body, name=name, grid=(s // tile, blocks),
        in_specs=[pl.BlockSpec((tile, D_MODEL), lambda i, j: (i, 0)),
                  pl.BlockSpec((HALO_B, D_MODEL), lambda i, j: (jnp.maximum(i * per16 - 1, 0), 0)),
                  pl.BlockSpec((None, D_MODEL, nb), lambda i, j: (j, 0, 0)),
                  pl.BlockSpec((None, D_MODEL, nb), lambda i, j: (j + blocks, 0, 0)),
                  pl.BlockSpec((2, 3, nb), lambda i, j: (0, 0, j)),
                  pl.BlockSpec((nb, D_MODEL), lambda i, j: (j, 0)),
                  pl.BlockSpec((tile, D_MODEL), lambda i, j: (i, 0))],
        out_specs=[pl.BlockSpec((tile, D_MODEL), lambda i, j: (i, 0)),
                   pl.BlockSpec((tile, nb), lambda i, j: (i, j)),
                   pl.BlockSpec((2, tile, nb), lambda i, j: (0, i, j)),
                   pl.BlockSpec((2, tile, nb), lambda i, j: (0, i, j))],
        out_shape=[jax.ShapeDtypeStruct((s, D_MODEL), F32), jax.ShapeDtypeStruct((s, D_FF), BF16),
                   jax.ShapeDtypeStruct((2, s, D_FF), BF16), jax.ShapeDtypeStruct((2, s, D_FF), BF16)],
        scratch_shapes=[pltpu.VMEM((2, tile + HALO_B, nb), F32), pltpu.VMEM((tile, D_MODEL), F32),
                        pltpu.VMEM((2, 3, FFN_ROWS, nb), F32)],
        compiler_params=_params("parallel", "arbitrary"),
    )(h2, h2, w_up_b, w_up_b, fcw, w_down, x2)


def _ffn_block_bwd(dx3, dx3b, p, u, x2, g2, w_up_b, fcw, w_down, *, tile, name, token=None):
    s = x2.shape[0]
    nb = w_up_b.shape[2]
    blocks = D_FF // nb
    n_tiles = s // tile
    per16 = tile // HALO_B
    last16 = s // HALO_B - 1

    def body(dxb_ref, dxbn_ref, wd_ref, p_ref, u_ref, un_ref, fw_ref, wg_ref, wu_ref, x2_ref, g_ref, dx3_ref,
             token_ref, dx2_ref, dx2b_ref, dg_ref, dp_ref, dw_ref, da_s, acc_w, acc_dh, taps):
        i = pl.program_id(0)
        j = pl.program_id(1)

        @pl.when(jnp.logical_and(i == 0, j == 0))
        def _():
            acc_w[...] = jnp.zeros_like(acc_w)
            dg_ref[...] = jnp.zeros_like(dg_ref)

        keep_next = jnp.where(i == n_tiles - 1, 0.0, 1.0)
        _spread_taps(fw_ref, taps)
        lhs = jnp.concatenate([dxb_ref[...], dxbn_ref[...]], axis=0)
        da_s[...] = _dot(lhs, wd_ref[...], NT)

        def du_rows(da, u_gate, u_up):
            ge, dge = _gelu_parts(u_gate)
            return da * u_up * dge, da * ge

        after = du_rows(da_s[tile:tile + HALO_B, :] * keep_next, un_ref[0].astype(F32), un_ref[1].astype(F32))
        for r0 in range(tile - FFN_ROWS, -1, -FFN_ROWS):
            du = du_rows(da_s[r0:r0 + FFN_ROWS, :], u_ref[0, r0:r0 + FFN_ROWS, :].astype(F32),
                         u_ref[1, r0:r0 + FFN_ROWS, :].astype(F32))
            for half in range(2):
                below = [du[half], _rows_from(du[half], after[half], 1), _rows_from(du[half], after[half], 2)]
                acc = taps[half, 2] * below[0]
                for k in range(2):
                    acc = acc + taps[half, k] * below[2 - k]
                dp_ref[half, r0:r0 + FFN_ROWS, :] = acc.astype(BF16)
                p_rows = p_ref[half, r0:r0 + FFN_ROWS, :].astype(F32)
                for k in range(3):
                    prod = below[2 - k] * p_rows
                    acc_w[j, half, k] += sum(prod[q:q + 8] for q in range(0, FFN_ROWS, 8))
            after = du

        contrib = _dot(dp_ref[0], wg_ref[...], NT) + _dot(dp_ref[1], wu_ref[...], NT)

        @pl.when(j == 0)
        def _():
            acc_dh[...] = contrib

        @pl.when(jnp.logical_and(j > 0, j < blocks - 1))
        def _():
            acc_dh[...] += contrib

        @pl.when(j == blocks - 1)
        def _():
            dh = acc_dh[...] + contrib
            xv = x2_ref[...]
            rstd = _rms(xv)
            n = xv * rstd
            dn = dh * g_ref[...]
            dx = dx3_ref[...] + rstd * (dn - n * jnp.mean(dn * n, axis=-1, keepdims=True))
            dx2_ref[...] = dx
            dx2b_ref[...] = dx.astype(BF16)
            dg_ref[0:1, :] += jnp.sum(dh * n, axis=0, keepdims=True)

        @pl.when(jnp.logical_and(i == n_tiles - 1, j == blocks - 1))
        def _():
            dw_ref[...] = jnp.zeros_like(dw_ref)
            for jj in range(blocks):
                for half in range(2):
                    for k in range(3):
                        dw_ref[half, k:k + 1, jj * nb:(jj + 1) * nb] = jnp.sum(acc_w[jj, half, k], axis=0, keepdims=True)

    next16 = lambda i: jnp.minimum((i + 1) * per16, last16)
    return pl.pallas_call(
        body, name=name, grid=(n_tiles, blocks),
        in_specs=[pl.BlockSpec((tile, D_MODEL), lambda i, j: (i, 0)),
                  pl.BlockSpec((HALO_B, D_MODEL), lambda i, j: (next16(i), 0)),
                  pl.BlockSpec((nb, D_MODEL), lambda i, j: (j, 0)),
                  pl.BlockSpec((2, tile, nb), lambda i, j: (0, i, j)),
                  pl.BlockSpec((2, tile, nb), lambda i, j: (0, i, j)),
                  pl.BlockSpec((2, HALO_B, nb), lambda i, j: (0, next16(i), j)),
                  pl.BlockSpec((2, 3, nb), lambda i, j: (0, 0, j)),
                  pl.BlockSpec((None, D_MODEL, nb), lambda i, j: (j, 0, 0)),
                  pl.BlockSpec((None, D_MODEL, nb), lambda i, j: (j + blocks, 0, 0)),
                  pl.BlockSpec((tile, D_MODEL), lambda i, j: (i, 0)),
                  pl.BlockSpec((1, D_MODEL), lambda i, j: (0, 0)),
                  pl.BlockSpec((tile, D_MODEL), lambda i, j: (i, 0)),
                  pl.BlockSpec(memory_space=pl.ANY)],
        out_specs=[pl.BlockSpec((tile, D_MODEL), lambda i, j: (i, 0)),
                   pl.BlockSpec((tile, D_MODEL), lambda i, j: (i, 0)),
                   pl.BlockSpec((8, D_MODEL), lambda i, j: (0, 0)),
                   pl.BlockSpec((2, tile, nb), lambda i, j: (0, i, j)),
                   pl.BlockSpec((2, 8, D_FF), lambda i, j: (0, 0, 0))],
        out_shape=[jax.ShapeDtypeStruct((s, D_MODEL), F32), jax.ShapeDtypeStruct((s, D_MODEL), BF16),
                   jax.ShapeDtypeStruct((8, D_MODEL), F32), jax.ShapeDtypeStruct((2, s, D_FF), BF16),
                   jax.ShapeDtypeStruct((2, 8, D_FF), F32)],
        scratch_shapes=[pltpu.VMEM((tile + HALO_B, nb), F32), pltpu.VMEM((blocks, 2, 3, 8, nb), F32),
                        pltpu.VMEM((tile, D_MODEL), F32), pltpu.VMEM((2, 3, FFN_ROWS, nb), F32)],
        compiler_params=_params("arbitrary", "arbitrary"),
    )(dx3b, dx3b, w_down, p, u, u, fcw, w_up_b, w_up_b, x2, g2.reshape(1, -1), dx3, _behind(token))


def _adamw_math(w, g, m, v):
    m = ADAM_B1 * m + (1.0 - ADAM_B1) * g
    v = ADAM_B2 * v + (1.0 - ADAM_B2) * (g * g)
    m_hat = m / (1.0 - ADAM_B1 ** ADAM_STEP)
    v_hat = v / (1.0 - ADAM_B2 ** ADAM_STEP)
    delta = -ADAM_LR * (m_hat / (jnp.sqrt(v_hat) + ADAM_EPS) + ADAM_WD * w)
    return delta, m, v


def _adamw(w, g, m, v, *, name):
    rows, cols = w.shape
    tr = rows
    for cand in (512, 256, 128, 64, 32, 16, 8):
        if rows % cand == 0 and rows > cand:
            tr = cand
            break

    def body(w_ref, g_ref, m_ref, v_ref, d_ref, nm_ref, nv_ref):
        d, nm, nv = _adamw_math(w_ref[...], g_ref[...], m_ref[...], v_ref[...])
        d_ref[...] = d
        nm_ref[...] = nm
        nv_ref[...] = nv

    spec = pl.BlockSpec((tr, cols), lambda i: (i, 0))
    return pl.pallas_call(
        body, name=name, grid=(rows // tr,), in_specs=[spec] * 4, out_specs=[spec] * 3,
        out_shape=[jax.ShapeDtypeStruct((rows, cols), F32)] * 3,
        compiler_params=_params("parallel"),
    )(w, g, m, v)


def _sum_parts(parts, *, name):
    _, rows, cols = parts.shape
    tr = rows
    for cand in (256, 128, 64, 32, 16):
        if rows % cand == 0 and rows > cand:
            tr = cand
            break

    def body(p_ref, o_ref):
        acc = p_ref[0].astype(F32)
        for d in range(1, N_DEV):
            acc = acc + p_ref[d].astype(F32)
        o_ref[...] = acc

    return pl.pallas_call(
        body, name=name, grid=(rows // tr,),
        in_specs=[pl.BlockSpec((N_DEV, tr, cols), lambda i: (0, i, 0))],
        out_specs=pl.BlockSpec((tr, cols), lambda i: (i, 0)),
        out_shape=jax.ShapeDtypeStruct((rows, cols), F32),
        compiler_params=_params("parallel"),
    )(parts)


def _place():
    return lax.axis_index("x"), lax.axis_index("y"), lax.axis_index("c")


def _flip(v, bit):
    return 1 - v if bit else v


N_PEERS = N_DEV - 1


def _peer_copy(k, src_ref, land_ref, send_sem, recv_sem, gather):
    x, y, c = _place()
    my_id = 4 * x + 2 * y + c
    px, py, pc = _flip(x, k & 4), _flip(y, k & 2), _flip(c, k & 1)
    peer_id = 4 * px + 2 * py + pc
    return pltpu.make_async_remote_copy(
        src_ref=src_ref if gather else src_ref.at[peer_id], dst_ref=land_ref.at[my_id],
        send_sem=send_sem.at[k - 1], recv_sem=recv_sem.at[k - 1],
        device_id=(px, py, pc), device_id_type=MESH)


def _sequencer_copies(srcs, *, gather, name, collective_id, after):
    n = len(srcs)
    hbm = pltpu.MemorySpace.HBM
    src_refs = [jax.new_ref(s, memory_space=hbm) for s in srcs]
    land_refs = [jax.empty_ref(jax.ShapeDtypeStruct(((N_DEV,) + s.shape) if gather else s.shape, s.dtype),
                               memory_space=hbm) for s in srcs]
    token_in = jax.new_ref(jnp.zeros((8, 128), F32) if after is None else after, memory_space=hbm)
    token_out = jax.empty_ref(jax.ShapeDtypeStruct((8, 128), F32), memory_space=hbm)

    @pl.kernel(mesh=plsc.ScalarSubcoreMesh(axis_name="seq", num_cores=1), name=name,
               scratch_types=(pltpu.SemaphoreType.DMA((n, N_PEERS)), pltpu.SemaphoreType.DMA((n, N_PEERS)),
                              pltpu.SemaphoreType.DMA((n + 1,))),
               compiler_params=pltpu.CompilerParams(collective_id=collective_id))
    def launch(send_sems, recv_sems, local_sems):
        x, y, c = _place()
        my_id = 4 * x + 2 * y + c
        barrier = pltpu.get_barrier_semaphore()
        own = [pltpu.make_async_copy(src_refs[t] if gather else src_refs[t].at[my_id], land_refs[t].at[my_id],
                                     local_sems.at[t]) for t in range(n)]
        if gather:
            sibling = (x, y, 1 - c)
            chips = [(1 - x, y), (x, 1 - y), (1 - x, 1 - y)]
            for peer in [sibling] + [(*chip, c) for chip in chips]:
                pl.semaphore_signal(barrier, inc=1, device_id=peer, device_id_type=MESH)
            pl.semaphore_wait(barrier, 4)

            def copy(t, k, block, to, src=None):
                dst = land_refs[t].at[4 * block[0] + 2 * block[1] + block[2]]
                return pltpu.make_async_remote_copy(
                    src_ref=dst if src is None else src, dst_ref=dst,
                    send_sem=send_sems.at[t, k], recv_sem=recv_sems.at[t, k], device_id=to, device_id_type=MESH)

            for cp in own:
                cp.start()
            sends = []
            for t in range(n):
                sends.append(copy(t, 0, (x, y, c), sibling, src=src_refs[t]))
                sends += [copy(t, 1 + j, (x, y, c), (*chip, c), src=src_refs[t]) for j, chip in enumerate(chips)]
            for cp in sends:
                cp.start()
            for t in range(n):
                for j, chip in enumerate(chips):
                    copy(t, 1 + j, (*chip, c), (x, y, c)).wait_recv()
                    passed_on = copy(t, 4 + j, (*chip, c), sibling)
                    passed_on.start()
                    sends.append(passed_on)
            for t in range(n):
                copy(t, 0, sibling, (x, y, c)).wait_recv()
                for j, chip in enumerate(chips):
                    copy(t, 4 + j, (*chip, 1 - c), (x, y, c)).wait_recv()
            for cp in sends:
                cp.wait_send()
            for cp in own:
                cp.wait()
        else:
            for k in range(1, N_DEV):
                peer = (_flip(x, k & 4), _flip(y, k & 2), _flip(c, k & 1))
                pl.semaphore_signal(barrier, inc=1, device_id=peer, device_id_type=MESH)
            pl.semaphore_wait(barrier, N_PEERS)
            for cp in own:
                cp.start()
            copies = [_peer_copy(k, src_refs[t], land_refs[t], send_sems.at[t], recv_sems.at[t], gather)
                      for t in range(n) for k in range(1, N_DEV)]
            for cp in copies:
                cp.start()
            for cp in own:
                cp.wait()
            for cp in copies:
                cp.wait()
        passed = pltpu.make_async_copy(token_in, token_out, local_sems.at[n])
        passed.start()
        passed.wait()

    launch()
    return [ref[...] for ref in land_refs], token_out[...]


def _all_reduce_small(buf, *, name):
    _, rows, lanes = buf.shape

    def body(in_ref, out_ref, parts, send_sems, recv_sems):
        x, y, c = _place()
        my_id = 4 * x + 2 * y + c
        peers = []
        for k in range(1, N_DEV):
            px, py, pc = _flip(x, k & 4), _flip(y, k & 2), _flip(c, k & 1)
            peers.append(((px, py, pc), 4 * px + 2 * py + pc))
        scatter = [pltpu.make_async_remote_copy(
            src_ref=in_ref.at[pid], dst_ref=parts.at[my_id],
            send_sem=send_sems.at[0, k], recv_sem=recv_sems.at[0, k],
            device_id=peer, device_id_type=MESH) for k, (peer, pid) in enumerate(peers)]
        for cp in scatter:
            cp.start()
        parts[my_id] = in_ref[my_id]
        for cp in scatter:
            cp.wait()
        total = parts[0]
        for d in range(1, N_DEV):
            total = total + parts[d]
        out_ref[my_id] = total
        gather = [pltpu.make_async_remote_copy(
            src_ref=out_ref.at[my_id], dst_ref=out_ref.at[my_id],
            send_sem=send_sems.at[1, k], recv_sem=recv_sems.at[1, k],
            device_id=peer, device_id_type=MESH) for k, (peer, pid) in enumerate(peers)]
        for cp in gather:
            cp.start()
        for k, (peer, pid) in enumerate(peers):
            pltpu.make_async_remote_copy(
                src_ref=out_ref.at[pid], dst_ref=out_ref.at[pid],
                send_sem=send_sems.at[1, k], recv_sem=recv_sems.at[1, k],
                device_id=peer, device_id_type=MESH).wait()

    vmem = pl.BlockSpec(memory_space=pltpu.VMEM)
    return pl.pallas_call(
        body, name=name, in_specs=[vmem], out_specs=vmem,
        out_shape=jax.ShapeDtypeStruct(buf.shape, F32),
        scratch_shapes=[pltpu.VMEM(buf.shape, F32),
                        pltpu.SemaphoreType.DMA((2, 7)), pltpu.SemaphoreType.DMA((2, 7))],
        compiler_params=pltpu.CompilerParams(vmem_limit_bytes=VMEM_LIMIT),
    )(buf)


TM = 512
TMM = 1024
TKW = 2048
MIX_TILE = 128
FFN_BLOCK_TILE = 512


def _block_diag(w):
    wg = w.reshape(N_GROUPS, HEADS_PER_GROUP, LRU_HEAD_DIM, LRU_HEAD_DIM)
    eye = jnp.eye(HEADS_PER_GROUP, dtype=w.dtype)
    bd = wg[:, :, :, None, :] * eye[None, :, None, :, None]
    return bd.reshape(N_GROUPS, LRU_GROUP, LRU_GROUP).astype(BF16)


def _head_blocks(bd):
    b5 = bd.reshape(N_GROUPS, HEADS_PER_GROUP, LRU_HEAD_DIM, HEADS_PER_GROUP, LRU_HEAD_DIM)
    blocks = [b5[:, h, :, h, :] for h in range(HEADS_PER_GROUP)]
    return jnp.stack(blocks, axis=1).reshape(LRU_HEADS, LRU_HEAD_DIM, LRU_HEAD_DIM)


def _w(lw, key, after):
    value = lw[key]
    return value(after) if callable(value) else value


def _layer_fwd(x, lw, tag):
    sv_rows = x.shape[0]
    h1 = _norm_fwd(x, lw["g1"], tm=TM, name=f"norm1_fwd_{tag}")
    z = _mm_nt(h1, _w(lw, "w_in_t", h1), tm=min(TMM, sv_rows), tn=896, tk=D_MODEL, out_dtype=F32, name=f"in_proj_{tag}")
    y_mix, hs = _mixer_fwd(z, _w(lw, "cw", z), lw["cb"], lw["wa_bd"], lw["wx_bd"], lw["ba"], lw["bx"], lw["lam"],
                           _w(lw, "scw", z), tile=MIX_TILE, name=f"mixer_fwd_{tag}")
    x2 = _mm_nn(y_mix, _w(lw, "w_out", y_mix), tm=min(TMM, sv_rows), tn=D_MODEL, tk=D_MIX, out_dtype=F32, name=f"out_proj_{tag}",
                residual=x)
    h2 = _norm_fwd(x2, lw["g2"], tm=TM, name=f"norm2_fwd_{tag}")
    x3, act, p, u = _ffn_block_fwd(x2, h2, _w(lw, "w_up_b", h2), _w(lw, "fcw", h2), _w(lw, "w_down", h2),
                                tile=min(FFN_BLOCK_TILE, sv_rows), name=f"ffn_fwd_{tag}")
    saved = dict(x=x, h1=h1, z=z, y_mix=y_mix, hs=hs, x2=x2, h2=h2, p=p, u=u, act=act)
    return x3, saved


def _layer_bwd(dx3, dx3b, lw, sv, tag, put):
    sv_rows = dx3.shape[0]
    w_in_t, w_out, w_up_b, w_down = (_w(lw, k, dx3) for k in ("w_in_t", "w_out", "w_up_b", "w_down"))
    cw, scw, fcw = (_w(lw, k, dx3) for k in ("cw", "scw", "fcw"))
    g_down = _mm_tn(sv["act"], dx3b, tm=1024, tn=D_MODEL, tk=min(TKW, sv_rows), out_dtype=BF16, name=f"down_bwd_w_{tag}")
    dx2, dx2b, dg2, dp, dfcw = _ffn_block_bwd(dx3, dx3b, sv["p"], sv["u"], sv["x2"], lw["g2"], w_up_b, fcw, w_down,
                                              tile=min(FFN_BLOCK_TILE, sv_rows), name=f"ffn_bwd_{tag}",
                                              token=put("w_down", g_down))
    g_up = _mm_up_bwd_w(sv["h2"], dp, tm=D_MODEL, tk=min(TKW, sv_rows), name=f"up_bwd_w_{tag}")
    dy = _mm_nt(dx2b, w_out, tm=min(TMM, sv_rows), tn=768, tk=D_MODEL, out_dtype=F32, name=f"out_bwd_x_{tag}")
    dz, dcw, dvec, dwa, dwx, dscw = _mixer_bwd(
        sv["z"], sv["hs"], dy, cw, lw["cb"], lw["wa_bd"], lw["wx_bd"], lw["ba"], lw["bx"], lw["lam"],
        scw, tile=MIX_TILE, name=f"mixer_bwd_{tag}", token=put("w_up_b", g_up))
    g_in_t = _mm_tn(dz, sv["h1"], tm=896, tn=D_MODEL, tk=min(TKW, sv_rows), out_dtype=BF16, name=f"in_bwd_w_{tag}")
    dh1 = _mm_nn(dz, w_in_t, tm=min(TMM, sv_rows), tn=D_MODEL, tk=896, out_dtype=F32, name=f"in_bwd_x_{tag}",
                 token=put("w_in_t", g_in_t))
    g_out = _mm_tn(sv["y_mix"], dx2b, tm=768, tn=D_MODEL, tk=min(TKW, sv_rows), out_dtype=BF16, name=f"out_bwd_w_{tag}",
                   token=dh1)
    dx, dxb, dg1 = _norm_bwd(dh1, sv["x"], lw["g1"], dx2, tm=TM, name=f"norm1_bwd_{tag}",
                             token=put("w_out", g_out))
    small = dict(norm1_g=dg1[0], lru_conv_w=dcw[0:4], lru_conv_b=dvec[0], lru_wa=_head_blocks(dwa),
                 lru_ba=dvec[1], lru_wx=_head_blocks(dwx), lru_bx=dvec[2], lru_lambda=dvec[3],
                 sc_conv_w=dscw[0:3], norm2_g=dg2[0], ffn_conv_w=dfcw[:, 0:3, :])
    return dx, dxb, small


SMALL_ORDER = ("norm1_g", "lru_conv_w", "lru_conv_b", "lru_wa", "lru_ba", "lru_wx", "lru_bx", "lru_lambda",
               "sc_conv_w", "norm2_g", "ffn_conv_w")


def _local_step(x, tgt, layers, final_g, put):
    saved = []
    h = x
    for l in range(DEPTH):
        h, sv = _layer_fwd(h, layers[l], f"l{l}")
        saved.append(sv)
    loss_blk, dx, dxb, dgf = _loss_head(h, final_g, tgt, tm=TM, name="loss_head")
    smalls = [None] * DEPTH
    for l in reversed(range(DEPTH)):
        dx, dxb, smalls[l] = _layer_bwd(dx, dxb, layers[l], saved[l], f"l{l}", functools.partial(put, l))
    return loss_blk[0, 0], dx, smalls, dgf[0]


def kernel(x, norm1_g, w_in, lru_conv_w, lru_conv_b, lru_wa, lru_ba, lru_wx, lru_bx, lru_lambda, sc_conv_w, w_out, norm2_g, w_up, ffn_conv_w, w_down, final_g, loss_target, m_norm1_g, m_w_in, m_lru_conv_w, m_lru_conv_b, m_lru_wa, m_lru_ba, m_lru_wx, m_lru_bx, m_lru_lambda, m_sc_conv_w, m_w_out, m_norm2_g, m_w_up, m_ffn_conv_w, m_w_down, m_final_g, v_norm1_g, v_w_in, v_lru_conv_w, v_lru_conv_b, v_lru_wa, v_lru_ba, v_lru_wx, v_lru_bx, v_lru_lambda, v_sc_conv_w, v_w_out, v_norm2_g, v_w_up, v_ffn_conv_w, v_w_down, v_final_g):
    names = ["norm1_g", "w_in", "lru_conv_w", "lru_conv_b", "lru_wa", "lru_ba", "lru_wx", "lru_bx", "lru_lambda",
             "sc_conv_w", "w_out", "norm2_g", "w_up", "ffn_conv_w", "w_down", "final_g"]
    w = dict(zip(names, [norm1_g, w_in, lru_conv_w, lru_conv_b, lru_wa, lru_ba, lru_wx, lru_bx, lru_lambda,
                         sc_conv_w, w_out, norm2_g, w_up, ffn_conv_w, w_down, final_g]))
    m = dict(zip(names, [m_norm1_g, m_w_in, m_lru_conv_w, m_lru_conv_b, m_lru_wa, m_lru_ba, m_lru_wx, m_lru_bx,
                         m_lru_lambda, m_sc_conv_w, m_w_out, m_norm2_g, m_w_up, m_ffn_conv_w, m_w_down, m_final_g]))
    v = dict(zip(names, [v_norm1_g, v_w_in, v_lru_conv_w, v_lru_conv_b, v_lru_wa, v_lru_ba, v_lru_wx, v_lru_bx,
                         v_lru_lambda, v_sc_conv_w, v_w_out, v_norm2_g, v_w_up, v_ffn_conv_w, v_w_down, v_final_g]))
    my_id = 4 * lax.axis_index("x") + 2 * lax.axis_index("y") + lax.axis_index("c")

    taps = jnp.zeros((DEPTH, 16, 768), F32)
    taps = taps.at[:, 0:4, 0:128].set(lru_conv_w).at[:, 4:7, 0:64].set(sc_conv_w).at[:, 8:11, :].set(ffn_conv_w)
    shards = {}
    for l in range(DEPTH):
        shards[f"w_in_t{l}"] = jnp.swapaxes(w_in[l], 0, 1).astype(BF16)
        if l == 0:
            shards["taps"] = taps.reshape(DEPTH * 16, 768)
        shards[f"w_out{l}"] = w_out[l].astype(BF16)
        shards[f"w_up_b{l}"] = w_up[l].astype(BF16)
        shards[f"w_down{l}"] = w_down[l].astype(BF16)
    ids = iter(range(16))
    got = {}
    chain = [None]
    for group in (("w_in_t0", "taps"), ("w_out0",), ("w_up_b0",), ("w_down0",),
                  ("w_in_t1",), ("w_out1",), ("w_up_b1",), ("w_down1",)):
        lands, chain[0] = _sequencer_copies([shards[k] for k in group], gather=True, name=f"gather_{group[0]}",
                                            collective_id=next(ids), after=None)
        got.update(zip(group, lands))

    def fetch(key, after):
        return got[key]

    def tap_rows(l, lo, hi, width, after):
        tl = fetch("taps", after).reshape(N_DEV, DEPTH, 16, 768)[:, l, lo:hi, 0:width]
        return jnp.transpose(tl, (1, 0, 2)).reshape(hi - lo, N_DEV * width)

    layers = []
    for l in range(DEPTH):
        layers.append(dict(
            g1=norm1_g[l], g2=norm2_g[l], cb=lru_conv_b[l], ba=lru_ba[l], bx=lru_bx[l], lam=lru_lambda[l],
            wa_bd=_block_diag(lru_wa[l]), wx_bd=_block_diag(lru_wx[l]),
            cw=functools.partial(tap_rows, l, 0, 4, 128), scw=functools.partial(tap_rows, l, 4, 7, 64),
            fcw=lambda after, l=l: tap_rows(l, 8, 11, 768, after).reshape(3, 2, D_FF).transpose(1, 0, 2),
            w_in_t=lambda after, l=l: fetch(f"w_in_t{l}", after).reshape(D_IN, D_MODEL),
            w_out=lambda after, l=l: fetch(f"w_out{l}", after).reshape(D_MIX, D_MODEL),
            w_up_b=lambda after, l=l: fetch(f"w_up_b{l}", after),
            w_down=lambda after, l=l: fetch(f"w_down{l}", after).reshape(D_FF, D_MODEL)))

    scatter_handles = {}

    def put(l, key, grad):
        blocks = grad if grad.ndim == 3 else grad.reshape(N_DEV, grad.shape[0] // N_DEV, grad.shape[1])
        (scatter_handles[(l, key)],), chain[0] = _sequencer_copies(
            [blocks], gather=False, name=f"scatter_{key}{l}", collective_id=next(ids), after=chain[0])
        return blocks

    loss_local, dx, smalls, dgf = _local_step(x[0], loss_target[0], layers, final_g, put)
    loss = lax.psum(loss_local, ("x", "y", "c"))

    parts = []
    for l in range(DEPTH):
        for key in ("w_in_t", "w_out", "w_up_b", "w_down"):
            parts.append(scatter_handles[(l, key)])

    flat = [smalls[l][k].reshape(-1) for l in range(DEPTH) for k in SMALL_ORDER] + [dgf.reshape(-1)]
    sizes = [f.shape[0] for f in flat]
    total = sum(sizes)
    rows = -(-total // (N_DEV * 128 * 8)) * 8
    flat.append(jnp.zeros((N_DEV * rows * 128 - total,), F32))
    small_sum = _all_reduce_small(jnp.concatenate(flat).reshape(N_DEV, rows, 128), name="reduce_small").reshape(-1)
    small_g, off = [], 0
    for sz in sizes:
        small_g.append(small_sum[off:off + sz])
        off += sz
    gs = {}
    for l in range(DEPTH):
        for i, k in enumerate(SMALL_ORDER):
            gs.setdefault(k, []).append(small_g[l * len(SMALL_ORDER) + i])
    g_final = small_g[-1]

    grads = {}
    per_layer = {k: [] for k in ("w_in", "w_out", "w_up", "w_down")}
    for l in range(DEPTH):
        p_in, p_out, p_up, p_down = parts[4 * l:4 * l + 4]
        per_layer["w_in"].append(jnp.swapaxes(_sum_parts(p_in, name=f"sum_w_in_l{l}"), 0, 1))
        per_layer["w_out"].append(_sum_parts(p_out, name=f"sum_w_out_l{l}"))
        per_layer["w_up"].append(_sum_parts(p_up, name=f"sum_w_up_l{l}"))
        per_layer["w_down"].append(_sum_parts(p_down, name=f"sum_w_down_l{l}"))
    for k, lst in per_layer.items():
        grads[k] = jnp.stack(lst)
    for k in ("norm1_g", "lru_conv_b", "lru_ba", "lru_bx", "lru_lambda", "norm2_g"):
        grads[k] = jnp.stack(gs[k]).reshape(DEPTH, -1)
    for k in ("lru_wa", "lru_wx"):
        grads[k] = jnp.stack(gs[k]).reshape(DEPTH, LRU_HEADS, LRU_HEAD_DIM, LRU_HEAD_DIM)
    grads["final_g"] = g_final
    cw_full = jnp.stack(gs["lru_conv_w"]).reshape(DEPTH, 4, N_DEV, 128)
    grads["lru_conv_w"] = lax.dynamic_index_in_dim(cw_full, my_id, axis=2, keepdims=False)
    scw_full = jnp.stack(gs["sc_conv_w"]).reshape(DEPTH, 3, N_DEV, 64)
    grads["sc_conv_w"] = lax.dynamic_index_in_dim(scw_full, my_id, axis=2, keepdims=False)
    fcw_full = jnp.stack(gs["ffn_conv_w"]).reshape(DEPTH, 2, 3, D_FF).transpose(0, 2, 1, 3).reshape(DEPTH, 3, N_DEV, 768)
    grads["ffn_conv_w"] = lax.dynamic_index_in_dim(fcw_full, my_id, axis=2, keepdims=False)

    deltas, new_m, new_v = {}, {}, {}
    for k in names:
        shape = w[k].shape
        cols = shape[-1]
        as2d = lambda a: a.reshape(-1, cols)
        d, nm, nv = _adamw(as2d(w[k]), as2d(grads[k]), as2d(m[k]), as2d(v[k]), name=f"adamw_{k}")
        deltas[k], new_m[k], new_v[k] = d.reshape(shape), nm.reshape(shape), nv.reshape(shape)

    return (loss, dx[None], *[grads[k] for k in names], *[deltas[k] for k in names],
            *[new_m[k] for k in names], *[new_v[k] for k in names])
```

```python
import functools
import math

import jax
import jax.numpy as jnp
from jax import lax
from jax.experimental import pallas as pl
from jax.experimental.pallas import tpu as pltpu
from jax.experimental.pallas import tpu_sc as plsc

F32 = jnp.float32
BF16 = jnp.bfloat16

N_DEV = 8
DEPTH = 2
D_MODEL = 1024
D_LRU = 1024
D_SC = 512
D_MIX = D_LRU + D_SC
D_IN = 2 * D_LRU + 3 * D_SC
D_FF = 3072
LRU_HEADS = 16
LRU_HEAD_DIM = 64
LRU_GROUP = 256
N_GROUPS = D_LRU // LRU_GROUP
HEADS_PER_GROUP = LRU_GROUP // LRU_HEAD_DIM
RG_C = 8.0
EPS = 1e-6
HALO = 8

ADAM_LR = 0.001
ADAM_B1 = 0.9
ADAM_B2 = 0.999
ADAM_EPS = 1e-08
ADAM_WD = 0.01
ADAM_STEP = 10

GELU_C = math.sqrt(2.0 / math.pi)
GELU_A = 0.044715

VMEM_LIMIT = 56 * 1024 * 1024
MESH = pl.DeviceIdType.MESH


def _params(*sem):
    return pltpu.CompilerParams(dimension_semantics=tuple(sem) if sem else None,
                                vmem_limit_bytes=VMEM_LIMIT)


def _gelu_parts(x):
    x2 = x * x
    t = jnp.tanh(GELU_C * (x + GELU_A * x * x2))
    half = 0.5 * (1.0 + t)
    g = x * half
    dg = half + 0.5 * x * (1.0 - t * t) * (GELU_C * (1.0 + 3.0 * GELU_A * x2))
    return g, dg


def _gelu(x):
    t = jnp.tanh(GELU_C * (x + GELU_A * x * x * x))
    return 0.5 * x * (1.0 + t)


def _sigmoid(x):
    return 1.0 / (1.0 + jnp.exp(-x))


def _softplus(x):
    e = jnp.exp(-jnp.abs(x))
    u = 1.0 + e
    log1p_e = jnp.where(u == 1.0, e, jnp.log(u) * (e / (u - 1.0)))
    return jnp.maximum(x, 0.0) + log1p_e


def _rms(x):
    ms = jnp.mean(x * x, axis=-1, keepdims=True)
    return lax.rsqrt(ms + EPS)


def _dot(a, b, dims):
    return lax.dot_general(a, b, (dims, ((), ())), preferred_element_type=F32)


NN = ((1,), (0,))
NT = ((1,), (1,))
TN = ((0,), (0,))


def _matmul(a, b, *, dims, grid, a_spec, b_spec, o_spec, out_shape, acc_shape, name,
            residual=None, r_spec=None, token=None):
    nk = grid[2]

    def body(*refs):
        a_ref, b_ref = refs[0], refs[1]
        r_ref = refs[2] if residual is not None else None
        o_ref = refs[2 + (residual is not None) + (token is not None)]
        prod = _dot(a_ref[...].astype(BF16), b_ref[...].astype(BF16), dims)

        def finish(total):
            if r_ref is not None:
                total = total + r_ref[...]
            o_ref[...] = total.astype(o_ref.dtype)

        if nk == 1:
            finish(prod)
            return
        acc_ref = refs[-1]
        k = pl.program_id(2)

        @pl.when(k == 0)
        def _():
            acc_ref[...] = prod

        @pl.when(jnp.logical_and(k > 0, k < nk - 1))
        def _():
            acc_ref[...] += prod

        @pl.when(k == nk - 1)
        def _():
            finish(acc_ref[...] + prod)

    in_specs = [a_spec, b_spec]
    args = [a, b]
    if residual is not None:
        in_specs.append(r_spec)
        args.append(residual)
    if token is not None:
        in_specs.append(pl.BlockSpec(memory_space=pl.ANY))
        args.append(token)
    return pl.pallas_call(
        body, name=name, grid=grid, in_specs=in_specs, out_specs=o_spec, out_shape=out_shape,
        scratch_shapes=[pltpu.VMEM(acc_shape, F32)] if nk > 1 else [],
        compiler_params=_params("parallel", "parallel", "arbitrary"),
    )(*args)


def _mm_nn(a, b, *, tm, tn, tk, out_dtype, name, residual=None, token=None):
    m, kd = a.shape
    n = b.shape[1]
    return _matmul(
        a, b, dims=NN, grid=(m // tm, n // tn, kd // tk),
        a_spec=pl.BlockSpec((tm, tk), lambda i, j, k: (i, k)),
        b_spec=pl.BlockSpec((tk, tn), lambda i, j, k: (k, j)),
        o_spec=pl.BlockSpec((tm, tn), lambda i, j, k: (i, j)),
        out_shape=jax.ShapeDtypeStruct((m, n), out_dtype), acc_shape=(tm, tn), name=name,
        residual=residual, r_spec=pl.BlockSpec((tm, tn), lambda i, j, k: (i, j)), token=token)


def _mm_nt(a, b, *, tm, tn, tk, out_dtype, name):
    m, kd = a.shape
    n = b.shape[0]
    return _matmul(
        a, b, dims=NT, grid=(m // tm, n // tn, kd // tk),
        a_spec=pl.BlockSpec((tm, tk), lambda i, j, k: (i, k)),
        b_spec=pl.BlockSpec((tn, tk), lambda i, j, k: (j, k)),
        o_spec=pl.BlockSpec((tm, tn), lambda i, j, k: (i, j)),
        out_shape=jax.ShapeDtypeStruct((m, n), out_dtype), acc_shape=(tm, tn), name=name)


def _mm_tn(a, b, *, tm, tn, tk, out_dtype, name, token=None):
    kd, m = a.shape
    n = b.shape[1]
    return _matmul(
        a, b, dims=TN, grid=(m // tm, n // tn, kd // tk),
        a_spec=pl.BlockSpec((tk, tm), lambda i, j, k: (k, i)),
        b_spec=pl.BlockSpec((tk, tn), lambda i, j, k: (k, j)),
        o_spec=pl.BlockSpec((tm, tn), lambda i, j, k: (i, j)),
        out_shape=jax.ShapeDtypeStruct((m, n), out_dtype), acc_shape=(tm, tn), name=name, token=token)


def _mm_up_bwd_w(h2, dp, *, tm, tk, name):
    s = h2.shape[0]
    nb = D_FF * 2 // N_DEV
    per_half = D_FF // nb
    return _matmul(
        h2, dp, dims=TN, grid=(D_MODEL // tm, N_DEV, s // tk),
        a_spec=pl.BlockSpec((tk, tm), lambda i, j, k: (k, i)),
        b_spec=pl.BlockSpec((None, tk, nb), lambda i, j, k: (j // per_half, k, j % per_half)),
        o_spec=pl.BlockSpec((None, tm, nb), lambda i, j, k: (j, i, 0)),
        out_shape=jax.ShapeDtypeStruct((N_DEV, D_MODEL, nb), BF16), acc_shape=(tm, nb), name=name)


def _norm_fwd(x, g, *, tm, name):
    s, d = x.shape

    def body(x_ref, g_ref, h_ref):
        xv = x_ref[...]
        h_ref[...] = (xv * _rms(xv) * g_ref[...]).astype(BF16)

    return pl.pallas_call(
        body, name=name, grid=(s // tm,),
        in_specs=[pl.BlockSpec((tm, d), lambda i: (i, 0)), pl.BlockSpec((1, d), lambda i: (0, 0))],
        out_specs=pl.BlockSpec((tm, d), lambda i: (i, 0)),
        out_shape=jax.ShapeDtypeStruct((s, d), BF16),
        compiler_params=_params("parallel"),
    )(x, g.reshape(1, d))


def _behind(token):
    return jnp.zeros((8, 128), F32) if token is None else token


def _norm_bwd(dh, x, g, dres, *, tm, name, token=None):
    s, d = x.shape

    def body(dh_ref, x_ref, g_ref, dres_ref, token_ref, dx_ref, dxb_ref, dg_ref):
        @pl.when(pl.program_id(0) == 0)
        def _():
            dg_ref[...] = jnp.zeros_like(dg_ref)

        xv = x_ref[...]
        rstd = _rms(xv)
        n = xv * rstd
        dhv = dh_ref[...]
        dn = dhv * g_ref[...]
        dx = rstd * (dn - n * jnp.mean(dn * n, axis=-1, keepdims=True))
        dx = dres_ref[...] + dx
        dx_ref[...] = dx
        dxb_ref[...] = dx.astype(BF16)
        dg_ref[0:1, :] += jnp.sum(dhv * n, axis=0, keepdims=True)

    return pl.pallas_call(
        body, name=name, grid=(s // tm,),
        in_specs=[pl.BlockSpec((tm, d), lambda i: (i, 0)), pl.BlockSpec((tm, d), lambda i: (i, 0)),
                  pl.BlockSpec((1, d), lambda i: (0, 0)), pl.BlockSpec((tm, d), lambda i: (i, 0)),
                  pl.BlockSpec(memory_space=pl.ANY)],
        out_specs=[pl.BlockSpec((tm, d), lambda i: (i, 0)), pl.BlockSpec((tm, d), lambda i: (i, 0)),
                   pl.BlockSpec((8, d), lambda i: (0, 0))],
        out_shape=[jax.ShapeDtypeStruct((s, d), F32), jax.ShapeDtypeStruct((s, d), BF16),
                   jax.ShapeDtypeStruct((8, d), F32)],
        compiler_params=_params("arbitrary"),
    )(dh, x, g.reshape(1, d), dres, _behind(token))


def _loss_head(x, g, tgt, *, tm, name):
    s, d = x.shape

    def body(x_ref, g_ref, t_ref, loss_ref, dx_ref, dxb_ref, dg_ref):
        @pl.when(pl.program_id(0) == 0)
        def _():
            dg_ref[...] = jnp.zeros_like(dg_ref)
            loss_ref[...] = jnp.zeros_like(loss_ref)

        xv = x_ref[...]
        gv = g_ref[...]
        rstd = _rms(xv)
        n = xv * rstd
        e = n * gv - t_ref[...]
        part = 0.5 * jnp.sum(jnp.mean(e * e, axis=-1, keepdims=True), axis=0, keepdims=True)
        loss_ref[...] += jnp.broadcast_to(part, loss_ref.shape)
        dy = e * (1.0 / d)
        dn = dy * gv
        dx = rstd * (dn - n * jnp.mean(dn * n, axis=-1, keepdims=True))
        dx_ref[...] = dx
        dxb_ref[...] = dx.astype(BF16)
        dg_ref[0:1, :] += jnp.sum(dy * n, axis=0, keepdims=True)

    return pl.pallas_call(
        body, name=name, grid=(s // tm,),
        in_specs=[pl.BlockSpec((tm, d), lambda i: (i, 0)), pl.BlockSpec((1, d), lambda i: (0, 0)),
                  pl.BlockSpec((tm, d), lambda i: (i, 0))],
        out_specs=[pl.BlockSpec((8, 128), lambda i: (0, 0)), pl.BlockSpec((tm, d), lambda i: (i, 0)),
                   pl.BlockSpec((tm, d), lambda i: (i, 0)), pl.BlockSpec((8, d), lambda i: (0, 0))],
        out_shape=[jax.ShapeDtypeStruct((8, 128), F32), jax.ShapeDtypeStruct((s, d), F32),
                   jax.ShapeDtypeStruct((s, d), BF16), jax.ShapeDtypeStruct((8, d), F32)],
        compiler_params=_params("arbitrary"),
    )(x, g.reshape(1, d), tgt)


def _scan_rows(a_ref, b_ref, h_ref, carry, *, rows, reverse):
    width = a_ref.shape[1]
    n_chunks = rows // 8
    row = lax.broadcasted_iota(jnp.int32, (8, width), 0)

    def step(ci, carry):
        chunk = (n_chunks - 1 - ci) if reverse else ci
        off = pl.multiple_of(chunk * 8, 8)
        av = a_ref[pl.ds(off, 8), :]
        bv = b_ref[pl.ds(off, 8), :]
        for sh in (1, 2, 4):
            if reverse:
                a_sh = pltpu.roll(av, 8 - sh, 0)
                b_sh = pltpu.roll(bv, 8 - sh, 0)
                m = row < 8 - sh
            else:
                a_sh = pltpu.roll(av, sh, 0)
                b_sh = pltpu.roll(bv, sh, 0)
                m = row >= sh
            bv = jnp.where(m, av * b_sh + bv, bv)
            av = jnp.where(m, av * a_sh, av)
        h = av * carry + bv
        h_ref[pl.ds(off, 8), :] = h
        return h[0:1, :] if reverse else h[7:8, :]

    return lax.fori_loop(0, n_chunks, step, carry)


def _lru_gates(lx, wa_ref, wx_ref, ba, bx, sp):
    lxb = lx.astype(BF16)
    pre_r = jnp.concatenate(
        [_dot(lxb[:, g * LRU_GROUP:(g + 1) * LRU_GROUP], wa_ref[g], NN) for g in range(N_GROUPS)], axis=1)
    pre_i = jnp.concatenate(
        [_dot(lxb[:, g * LRU_GROUP:(g + 1) * LRU_GROUP], wx_ref[g], NN) for g in range(N_GROUPS)], axis=1)
    r = _sigmoid(pre_r + ba)
    ig = _sigmoid(pre_i + bx)
    log_a = (-RG_C * r) * sp
    a = jnp.exp(log_a)
    mult = jnp.sqrt(-jnp.tanh(log_a) * (a * a + 1.0))
    return lxb, r, ig, a, mult


def _mixer_fwd(z, cw, cb, wa_bd, wx_bd, ba, bx, lam, scw, *, tile, name):
    s = z.shape[0]
    n_tiles = s // tile

    def body(z_ref, cw_ref, cb_ref, wa_ref, wx_ref, ba_ref, bx_ref, lam_ref, scw_ref,
             y_ref, hs_ref, ext_lx, ext_q, a_s, b_s, h_car):
        i = pl.program_id(0)

        @pl.when(i == 0)
        def _():
            ext_lx[0:HALO, :] = jnp.zeros((HALO, D_LRU), F32)
            ext_q[0:HALO, :] = jnp.zeros((HALO, D_SC), F32)
            h_car[...] = jnp.zeros_like(h_car)

        ext_lx[HALO:HALO + tile, :] = z_ref[:, 0:D_LRU]
        ext_q[HALO:HALO + tile, :] = z_ref[:, 2 * D_LRU + D_SC:2 * D_LRU + 2 * D_SC] * z_ref[:, 2 * D_LRU + 2 * D_SC:D_IN]
        lx = cb_ref[...] + cw_ref[0:1, :] * ext_lx[pl.ds(HALO - 3, tile), :]
        for k in range(1, 4):
            lx = lx + cw_ref[k:k + 1, :] * ext_lx[pl.ds(HALO - 3 + k, tile), :]
        cq = scw_ref[0:1, :] * ext_q[pl.ds(HALO - 2, tile), :]
        for k in range(1, 3):
            cq = cq + scw_ref[k:k + 1, :] * ext_q[pl.ds(HALO - 2 + k, tile), :]
        ext_lx[0:HALO, :] = ext_lx[tile:tile + HALO, :]
        ext_q[0:HALO, :] = ext_q[tile:tile + HALO, :]

        sp = _softplus(-lam_ref[...])
        _, _, ig, a, mult = _lru_gates(lx, wa_ref, wx_ref, ba_ref[...], bx_ref[...], sp)
        a_s[...] = a
        b_s[...] = mult * (ig * lx)
        h_car[0:1, :] = _scan_rows(a_s, b_s, hs_ref, h_car[0:1, :], rows=tile, reverse=False)

        y_ref[:, 0:D_LRU] = (hs_ref[...] * _gelu(z_ref[:, D_LRU:2 * D_LRU])).astype(BF16)
        y_ref[:, D_LRU:D_MIX] = (z_ref[:, 2 * D_LRU:2 * D_LRU + D_SC] * cq).astype(BF16)

    full = lambda shape: pl.BlockSpec(shape, lambda i: (0,) * len(shape))
    return pl.pallas_call(
        body, name=name, grid=(n_tiles,),
        in_specs=[pl.BlockSpec((tile, D_IN), lambda i: (i, 0)),
                  full((4, D_LRU)), full((1, D_LRU)),
                  full((N_GROUPS, LRU_GROUP, LRU_GROUP)), full((N_GROUPS, LRU_GROUP, LRU_GROUP)),
                  full((1, D_LRU)), full((1, D_LRU)), full((1, D_LRU)), full((3, D_SC))],
        out_specs=[pl.BlockSpec((tile, D_MIX), lambda i: (i, 0)), pl.BlockSpec((tile, D_LRU), lambda i: (i, 0))],
        out_shape=[jax.ShapeDtypeStruct((s, D_MIX), BF16), jax.ShapeDtypeStruct((s, D_LRU), F32)],
        scratch_shapes=[pltpu.VMEM((tile + HALO, D_LRU), F32), pltpu.VMEM((tile + HALO, D_SC), F32),
                        pltpu.VMEM((tile, D_LRU), F32), pltpu.VMEM((tile, D_LRU), F32),
                        pltpu.VMEM((8, D_LRU), F32)],
        compiler_params=_params("arbitrary"),
    )(z, cw, cb.reshape(1, -1), wa_bd, wx_bd, ba.reshape(1, -1), bx.reshape(1, -1), lam.reshape(1, -1), scw)


def _mixer_bwd(z, hs, dy, cw, cb, wa_bd, wx_bd, ba, bx, lam, scw, *, tile, name, token=None):
    s = z.shape[0]
    n_tiles = s // tile
    per8 = tile // 8

    def body(z_ref, zp_ref, hs_ref, hsp_ref, dy_ref, cw_ref, cb_ref, wa_ref, wx_ref, ba_ref, bx_ref, lam_ref, scw_ref,
             token_ref, dz_ref, dcw_ref, dvec_ref, dwa_ref, dwx_ref, dscw_ref,
             ext_lx, ext_q, ext_h, ext_a, ext_dlx, ext_dcq, a_s, b_s, lam_s, l_car):
        i = pl.program_id(0)
        first_tile = i == n_tiles - 1

        @pl.when(i == 0)
        def _():
            for ref in (dcw_ref, dvec_ref, dwa_ref, dwx_ref, dscw_ref, l_car):
                ref[...] = jnp.zeros_like(ref)
            ext_a[tile:tile + HALO, :] = jnp.zeros((HALO, D_LRU), F32)
            ext_dlx[tile:tile + HALO, :] = jnp.zeros((HALO, D_LRU), F32)
            ext_dcq[tile:tile + HALO, :] = jnp.zeros((HALO, D_SC), F32)

        keep = jnp.where(first_tile, 0.0, 1.0)
        sb = z_ref[:, 2 * D_LRU:2 * D_LRU + D_SC]
        sc = z_ref[:, 2 * D_LRU + D_SC:2 * D_LRU + 2 * D_SC]
        sx = z_ref[:, 2 * D_LRU + 2 * D_SC:D_IN]
        ext_lx[0:HALO, :] = zp_ref[:, 0:D_LRU] * keep
        ext_lx[HALO:HALO + tile, :] = z_ref[:, 0:D_LRU]
        ext_q[0:HALO, :] = zp_ref[:, 2 * D_LRU + D_SC:2 * D_LRU + 2 * D_SC] * zp_ref[:, 2 * D_LRU + 2 * D_SC:D_IN] * keep
        ext_q[HALO:HALO + tile, :] = sc * sx
        ext_h[0:HALO, :] = hsp_ref[...] * keep
        ext_h[HALO:HALO + tile, :] = hs_ref[...]

        lx = cb_ref[...] + cw_ref[0:1, :] * ext_lx[pl.ds(HALO - 3, tile), :]
        for k in range(1, 4):
            lx = lx + cw_ref[k:k + 1, :] * ext_lx[pl.ds(HALO - 3 + k, tile), :]
        cq = scw_ref[0:1, :] * ext_q[pl.ds(HALO - 2, tile), :]
        for k in range(1, 3):
            cq = cq + scw_ref[k:k + 1, :] * ext_q[pl.ds(HALO - 2 + k, tile), :]

        sp = _softplus(-lam_ref[...])
        lxb, r, ig, a, mult = _lru_gates(lx, wa_ref, wx_ref, ba_ref[...], bx_ref[...], sp)

        ge, dge = _gelu_parts(z_ref[:, D_LRU:2 * D_LRU])
        dy_lru = dy_ref[:, 0:D_LRU]
        dz_ref[:, D_LRU:2 * D_LRU] = (dy_lru * hs_ref[...] * dge).astype(BF16)

        ext_a[0:tile, :] = a
        a_s[...] = ext_a[pl.ds(1, tile), :]
        b_s[...] = dy_lru * ge
        l_car[0:1, :] = _scan_rows(a_s, b_s, lam_s, l_car[0:1, :], rows=tile, reverse=True)
        ext_a[tile:tile + HALO, :] = ext_a[0:HALO, :]
        lv = lam_s[...]

        da = lv * ext_h[pl.ds(HALO - 1, tile), :]
        d_mult = lv * ig * lx
        d_i = lv * mult * lx
        dlx = lv * mult * ig
        dlog_a = da * a - d_mult * (a * a) / mult
        d_r = dlog_a * (-RG_C * sp)
        dpre_r = d_r * r * (1.0 - r)
        dpre_i = d_i * ig * (1.0 - ig)
        dvec_ref[1:2, :] += jnp.sum(dpre_r, axis=0, keepdims=True)
        dvec_ref[2:3, :] += jnp.sum(dpre_i, axis=0, keepdims=True)
        dvec_ref[3:4, :] += jnp.sum(dlog_a * (-RG_C * r), axis=0, keepdims=True)
        dpr_b = dpre_r.astype(BF16)
        dpi_b = dpre_i.astype(BF16)
        back = []
        for g in range(N_GROUPS):
            cols = slice(g * LRU_GROUP, (g + 1) * LRU_GROUP)
            dwa_ref[g] += _dot(lxb[:, cols], dpr_b[:, cols], TN)
            dwx_ref[g] += _dot(lxb[:, cols], dpi_b[:, cols], TN)
            back.append(_dot(dpr_b[:, cols], wa_ref[g], NT) + _dot(dpi_b[:, cols], wx_ref[g], NT))
        dlx = dlx + jnp.concatenate(back, axis=1)
        dvec_ref[0:1, :] += jnp.sum(dlx, axis=0, keepdims=True)

        ext_dlx[0:tile, :] = dlx
        for k in range(4):
            dcw_ref[k:k + 1, :] += jnp.sum(dlx * ext_lx[pl.ds(HALO - 3 + k, tile), :], axis=0, keepdims=True)
        dlxp = cw_ref[3:4, :] * dlx
        for k in range(3):
            dlxp = dlxp + cw_ref[k:k + 1, :] * ext_dlx[pl.ds(3 - k, tile), :]
        dz_ref[:, 0:D_LRU] = dlxp.astype(BF16)
        ext_dlx[tile:tile + HALO, :] = ext_dlx[0:HALO, :]

        dy_sc = dy_ref[:, D_LRU:D_MIX]
        dz_ref[:, 2 * D_LRU:2 * D_LRU + D_SC] = (dy_sc * cq).astype(BF16)
        dcq = dy_sc * sb
        ext_dcq[0:tile, :] = dcq
        for k in range(3):
            dscw_ref[k:k + 1, :] += jnp.sum(dcq * ext_q[pl.ds(HALO - 2 + k, tile), :], axis=0, keepdims=True)
        dq = scw_ref[2:3, :] * dcq
        for k in range(2):
            dq = dq + scw_ref[k:k + 1, :] * ext_dcq[pl.ds(2 - k, tile), :]
        dz_ref[:, 2 * D_LRU + D_SC:2 * D_LRU + 2 * D_SC] = (dq * sx).astype(BF16)
        dz_ref[:, 2 * D_LRU + 2 * D_SC:D_IN] = (dq * sc).astype(BF16)
        ext_dcq[tile:tile + HALO, :] = ext_dcq[0:HALO, :]

        @pl.when(i == n_tiles - 1)
        def _():
            dvec_ref[3:4, :] = dvec_ref[3:4, :] * (-_sigmoid(-lam_ref[...]))

    rev = lambda i: n_tiles - 1 - i
    prev8 = lambda i: jnp.maximum(rev(i) * per8 - 1, 0)
    full = lambda shape: pl.BlockSpec(shape, lambda i: (0,) * len(shape))
    return pl.pallas_call(
        body, name=name, grid=(n_tiles,),
        in_specs=[pl.BlockSpec((tile, D_IN), lambda i: (rev(i), 0)),
                  pl.BlockSpec((HALO, D_IN), lambda i: (prev8(i), 0)),
                  pl.BlockSpec((tile, D_LRU), lambda i: (rev(i), 0)),
                  pl.BlockSpec((HALO, D_LRU), lambda i: (prev8(i), 0)),
                  pl.BlockSpec((tile, D_MIX), lambda i: (rev(i), 0)),
                  full((4, D_LRU)), full((1, D_LRU)),
                  full((N_GROUPS, LRU_GROUP, LRU_GROUP)), full((N_GROUPS, LRU_GROUP, LRU_GROUP)),
                  full((1, D_LRU)), full((1, D_LRU)), full((1, D_LRU)), full((3, D_SC)),
                  pl.BlockSpec(memory_space=pl.ANY)],
        out_specs=[pl.BlockSpec((tile, D_IN), lambda i: (rev(i), 0)),
                   full((8, D_LRU)), full((8, D_LRU)),
                   full((N_GROUPS, LRU_GROUP, LRU_GROUP)), full((N_GROUPS, LRU_GROUP, LRU_GROUP)),
                   full((8, D_SC))],
        out_shape=[jax.ShapeDtypeStruct((s, D_IN), BF16),
                   jax.ShapeDtypeStruct((8, D_LRU), F32), jax.ShapeDtypeStruct((8, D_LRU), F32),
                   jax.ShapeDtypeStruct((N_GROUPS, LRU_GROUP, LRU_GROUP), F32),
                   jax.ShapeDtypeStruct((N_GROUPS, LRU_GROUP, LRU_GROUP), F32),
                   jax.ShapeDtypeStruct((8, D_SC), F32)],
        scratch_shapes=[pltpu.VMEM((tile + HALO, D_LRU), F32), pltpu.VMEM((tile + HALO, D_SC), F32),
                        pltpu.VMEM((tile + HALO, D_LRU), F32), pltpu.VMEM((tile + HALO, D_LRU), F32),
                        pltpu.VMEM((tile + HALO, D_LRU), F32), pltpu.VMEM((tile + HALO, D_SC), F32),
                        pltpu.VMEM((tile, D_LRU), F32), pltpu.VMEM((tile, D_LRU), F32),
                        pltpu.VMEM((tile, D_LRU), F32), pltpu.VMEM((8, D_LRU), F32)],
        compiler_params=_params("arbitrary"),
    )(z, z, hs, hs, dy, cw, cb.reshape(1, -1), wa_bd, wx_bd, ba.reshape(1, -1), bx.reshape(1, -1),
      lam.reshape(1, -1), scw, _behind(token))


FFN_ROWS = 16
FFN_GROUPS = 2


def _spread_taps(fw_ref, taps):
    for half in range(2):
        for k in range(3):
            taps[half, k] = jnp.broadcast_to(fw_ref[half, k:k + 1, :], taps.shape[2:])


def _rows_from(first, second, start):
    stack = jnp.concatenate([first, second], axis=0)
    return pltpu.roll(stack, 2 * FFN_ROWS - start, 0)[0:FFN_ROWS]


def _conv3_rows(taps, ext_ref, half, row):
    before = ext_ref[half, row - FFN_ROWS:row, :]
    here = ext_ref[half, row:row + FFN_ROWS, :]
    acc = taps[half, 2] * here
    for k in range(2):
        acc = acc + taps[half, k] * _rows_from(before, here, FFN_ROWS - 2 + k)
    return acc


HALO_B = 16


def _ffn_block_fwd(x2, h2, w_up_b, fcw, w_down, *, tile, name):
    s = h2.shape[0]
    nb = w_up_b.shape[2]
    blocks = D_FF // nb
    per16 = tile // HALO_B

    def body(h_ref, hp_ref, wg_ref, wu_ref, fw_ref, wd_ref, x2_ref, x3_ref, act_ref, p_ref, u_ref, ext_p, acc_ref,
             taps):
        i = pl.program_id(0)
        j = pl.program_id(1)
        keep = jnp.where(i == 0, 0.0, 1.0)
        _spread_taps(fw_ref, taps)
        lhs = jnp.concatenate([hp_ref[...], h_ref[...]], axis=0)
        grp = tile // FFN_GROUPS
        for g in range(FFN_GROUPS):
            new = slice(g * grp + (HALO_B if g else 0), (g + 1) * grp + HALO_B)
            for half, w_ref in ((0, wg_ref), (1, wu_ref)):
                pe = _dot(lhs[new], w_ref[...], NN)
                if g == 0:
                    ext_p[half, 0:HALO_B, :] = pe[0:HALO_B] * keep
                    ext_p[half, HALO_B:grp + HALO_B, :] = pe[HALO_B:]
                    p_ref[half, 0:grp, :] = pe[HALO_B:].astype(BF16)
                else:
                    ext_p[half, new, :] = pe
                    p_ref[half, g * grp:(g + 1) * grp, :] = pe.astype(BF16)
        for g in range(FFN_GROUPS):
            rows = slice(g * grp, (g + 1) * grp)
            acts = []
            for r0 in range(g * grp, (g + 1) * grp, FFN_ROWS):
                u = [_conv3_rows(taps, ext_p, half, HALO_B + r0) for half in range(2)]
                for half in range(2):
                    u_ref[half, r0:r0 + FFN_ROWS, :] = u[half].astype(BF16)
                acts.append((_gelu(u[0]) * u[1]).astype(BF16))
                act_ref[r0:r0 + FFN_ROWS, :] = acts[-1]
            contrib = _dot(jnp.concatenate(acts, axis=0), wd_ref[...], NN)
            acc_ref[rows, :] = contrib + jnp.where(j > 0, acc_ref[rows, :], 0.0)

        @pl.when(j == blocks - 1)
        def _():
            x3_ref[...] = x2_ref[...] + acc_ref[...]

    return pl.pallas_call(
        body, name=name, grid=(s // tile, blocks),
        in_specs=[pl.BlockSpec((tile, D_MODEL), lambda i, j: (i, 0)),
                  pl.BlockSpec((HALO_B, D_MODEL), lambda i, j: (jnp.maximum(i * per16 - 1, 0), 0)),
                  pl.BlockSpec((None, D_MODEL, nb), lambda i, j: (j, 0, 0)),
                  pl.BlockSpec((None, D_MODEL, nb), lambda i, j: (j + blocks, 0, 0)),
                  pl.BlockSpec((2, 3, nb), lambda i, j: (0, 0, j)),
                  pl.BlockSpec((nb, D_MODEL), lambda i, j: (j, 0)),
                  pl.BlockSpec((tile, D_MODEL), lambda i, j: (i, 0))],
        out_specs=[pl.BlockSpec((tile, D_MODEL), lambda i, j: (i, 0)),
                   pl.BlockSpec((tile, nb), lambda i, j: (i, j)),
                   pl.BlockSpec((2, tile, nb), lambda i, j: (0, i, j)),
                   pl.BlockSpec((2, tile, nb), lambda i, j: (0, i, j))],
        out_shape=[jax.ShapeDtypeStruct((s, D_MODEL), F32), jax.ShapeDtypeStruct((s, D_FF), BF16),
                   jax.ShapeDtypeStruct((2, s, D_FF), BF16), jax.ShapeDtypeStruct((2, s, D_FF), BF16)],
        scratch_shapes=[pltpu.VMEM((2, tile + HALO_B, nb), F32), pltpu.VMEM((tile, D_MODEL), F32),
                        pltpu.VMEM((2, 3, FFN_ROWS, nb), F32)],
        compiler_params=_params("parallel", "arbitrary"),
    )(h2, h2, w_up_b, w_up_b, fcw, w_down, x2)


def _ffn_block_bwd(dx3, dx3b, p, u, x2, g2, w_up_b, fcw, w_down, *, tile, name, token=None):
    s = x2.shape[0]
    nb = w_up_b.shape[2]
    blocks = D_FF // nb
    n_tiles = s // tile
    per16 = tile // HALO_B
    last16 = s // HALO_B - 1

    def body(dxb_ref, dxbn_ref, wd_ref, p_ref, u_ref, un_ref, fw_ref, wg_ref, wu_ref, x2_ref, g_ref, dx3_ref,
             token_ref, dx2_ref, dx2b_ref, dg_ref, dp_ref, dw_ref, da_s, acc_w, acc_dh, taps):
        i = pl.program_id(0)
        j = pl.program_id(1)

        @pl.when(jnp.logical_and(i == 0, j == 0))
        def _():
            acc_w[...] = jnp.zeros_like(acc_w)
            dg_ref[...] = jnp.zeros_like(dg_ref)

        keep_next = jnp.where(i == n_tiles - 1, 0.0, 1.0)
        _spread_taps(fw_ref, taps)
        lhs = jnp.concatenate([dxb_ref[...], dxbn_ref[...]], axis=0)
        grp = tile // FFN_GROUPS
        for g in reversed(range(FFN_GROUPS)):
            new = slice(g * grp, (g + 1) * grp + (HALO_B if g == FFN_GROUPS - 1 else 0))
            da_s[new, :] = _dot(lhs[new], wd_ref[...], NT)

        def du_rows(da, u_gate, u_up):
            ge, dge = _gelu_parts(u_gate)
            return da * u_up * dge, da * ge

        after = du_rows(da_s[tile:tile + HALO_B, :] * keep_next, un_ref[0].astype(F32), un_ref[1].astype(F32))
        for g in reversed(range(FFN_GROUPS)):
            rows = slice(g * grp, (g + 1) * grp)
            dps = ([], [])
            for r0 in range((g + 1) * grp - FFN_ROWS, g * grp - 1, -FFN_ROWS):
                du = du_rows(da_s[r0:r0 + FFN_ROWS, :], u_ref[0, r0:r0 + FFN_ROWS, :].astype(F32),
                             u_ref[1, r0:r0 + FFN_ROWS, :].astype(F32))
                for half in range(2):
                    below = [du[half], _rows_from(du[half], after[half], 1), _rows_from(du[half], after[half], 2)]
                    acc = taps[half, 2] * below[0]
                    for k in range(2):
                        acc = acc + taps[half, k] * below[2 - k]
                    dps[half].insert(0, acc.astype(BF16))
                    dp_ref[half, r0:r0 + FFN_ROWS, :] = dps[half][0]
                    p_rows = p_ref[half, r0:r0 + FFN_ROWS, :].astype(F32)
                    for k in range(3):
                        prod = below[2 - k] * p_rows
                        acc_w[j, half, k] += sum(prod[q:q + 8] for q in range(0, FFN_ROWS, 8))
                after = du
            contrib = (_dot(jnp.concatenate(dps[0], axis=0), wg_ref[...], NT)
                       + _dot(jnp.concatenate(dps[1], axis=0), wu_ref[...], NT))
            acc_dh[rows, :] = contrib + jnp.where(j > 0, acc_dh[rows, :], 0.0)

        @pl.when(j == blocks - 1)
        def _():
            dh = acc_dh[...]
            xv = x2_ref[...]
            rstd = _rms(xv)
            n = xv * rstd
            dn = dh * g_ref[...]
            dx = dx3_ref[...] + rstd * (dn - n * jnp.mean(dn * n, axis=-1, keepdims=True))
            dx2_ref[...] = dx
            dx2b_ref[...] = dx.astype(BF16)
            dg_ref[0:1, :] += jnp.sum(dh * n, axis=0, keepdims=True)

        @pl.when(jnp.logical_and(i == n_tiles - 1, j == blocks - 1))
        def _():
            dw_ref[...] = jnp.zeros_like(dw_ref)
            for jj in range(blocks):
                for half in range(2):
                    for k in range(3):
                        dw_ref[half, k:k + 1, jj * nb:(jj + 1) * nb] = jnp.sum(acc_w[jj, half, k], axis=0, keepdims=True)

    next16 = lambda i: jnp.minimum((i + 1) * per16, last16)
    return pl.pallas_call(
        body, name=name, grid=(n_tiles, blocks),
        in_specs=[pl.BlockSpec((tile, D_MODEL), lambda i, j: (i, 0)),
                  pl.BlockSpec((HALO_B, D_MODEL), lambda i, j: (next16(i), 0)),
                  pl.BlockSpec((nb, D_MODEL), lambda i, j: (j, 0)),
                  pl.BlockSpec((2, tile, nb), lambda i, j: (0, i, j)),
                  pl.BlockSpec((2, tile, nb), lambda i, j: (0, i, j)),
                  pl.BlockSpec((2, HALO_B, nb), lambda i, j: (0, next16(i), j)),
                  pl.BlockSpec((2, 3, nb), lambda i, j: (0, 0, j)),
                  pl.BlockSpec((None, D_MODEL, nb), lambda i, j: (j, 0, 0)),
                  pl.BlockSpec((None, D_MODEL, nb), lambda i, j: (j + blocks, 0, 0)),
                  pl.BlockSpec((tile, D_MODEL), lambda i, j: (i, 0)),
                  pl.BlockSpec((1, D_MODEL), lambda i, j: (0, 0)),
                  pl.BlockSpec((tile, D_MODEL), lambda i, j: (i, 0)),
                  pl.BlockSpec(memory_space=pl.ANY)],
        out_specs=[pl.BlockSpec((tile, D_MODEL), lambda i, j: (i, 0)),
                   pl.BlockSpec((tile, D_MODEL), lambda i, j: (i, 0)),
                   pl.BlockSpec((8, D_MODEL), lambda i, j: (0, 0)),
                   pl.BlockSpec((2, tile, nb), lambda i, j: (0, i, j)),
                   pl.BlockSpec((2, 8, D_FF), lambda i, j: (0, 0, 0))],
        out_shape=[jax.ShapeDtypeStruct((s, D_MODEL), F32), jax.ShapeDtypeStruct((s, D_MODEL), BF16),
                   jax.ShapeDtypeStruct((8, D_MODEL), F32), jax.ShapeDtypeStruct((2, s, D_FF), BF16),
                   jax.ShapeDtypeStruct((2, 8, D_FF), F32)],
        scratch_shapes=[pltpu.VMEM((tile + HALO_B, nb), F32), pltpu.VMEM((blocks, 2, 3, 8, nb), F32),
                        pltpu.VMEM((tile, D_MODEL), F32), pltpu.VMEM((2, 3, FFN_ROWS, nb), F32)],
        compiler_params=_params("arbitrary", "arbitrary"),
    )(dx3b, dx3b, w_down, p, u, u, fcw, w_up_b, w_up_b, x2, g2.reshape(1, -1), dx3, _behind(token))


def _adamw_math(w, g, m, v):
    m = ADAM_B1 * m + (1.0 - ADAM_B1) * g
    v = ADAM_B2 * v + (1.0 - ADAM_B2) * (g * g)
    m_hat = m / (1.0 - ADAM_B1 ** ADAM_STEP)
    v_hat = v / (1.0 - ADAM_B2 ** ADAM_STEP)
    delta = -ADAM_LR * (m_hat / (jnp.sqrt(v_hat) + ADAM_EPS) + ADAM_WD * w)
    return delta, m, v


def _adamw(w, g, m, v, *, name):
    rows, cols = w.shape
    tr = rows
    for cand in (512, 256, 128, 64, 32, 16, 8):
        if rows % cand == 0 and rows > cand:
            tr = cand
            break

    def body(w_ref, g_ref, m_ref, v_ref, d_ref, nm_ref, nv_ref):
        d, nm, nv = _adamw_math(w_ref[...], g_ref[...], m_ref[...], v_ref[...])
        d_ref[...] = d
        nm_ref[...] = nm
        nv_ref[...] = nv

    spec = pl.BlockSpec((tr, cols), lambda i: (i, 0))
    return pl.pallas_call(
        body, name=name, grid=(rows // tr,), in_specs=[spec] * 4, out_specs=[spec] * 3,
        out_shape=[jax.ShapeDtypeStruct((rows, cols), F32)] * 3,
        compiler_params=_params("parallel"),
    )(w, g, m, v)


def _sum_parts(parts, *, name):
    _, rows, cols = parts.shape
    tr = rows
    for cand in (256, 128, 64, 32, 16):
        if rows % cand == 0 and rows > cand:
            tr = cand
            break

    def body(p_ref, o_ref):
        acc = p_ref[0].astype(F32)
        for d in range(1, N_DEV):
            acc = acc + p_ref[d].astype(F32)
        o_ref[...] = acc

    return pl.pallas_call(
        body, name=name, grid=(rows // tr,),
        in_specs=[pl.BlockSpec((N_DEV, tr, cols), lambda i: (0, i, 0))],
        out_specs=pl.BlockSpec((tr, cols), lambda i: (i, 0)),
        out_shape=jax.ShapeDtypeStruct((rows, cols), F32),
        compiler_params=_params("parallel"),
    )(parts)


def _place():
    return lax.axis_index("x"), lax.axis_index("y"), lax.axis_index("c")


def _flip(v, bit):
    return 1 - v if bit else v


N_PEERS = N_DEV - 1


def _peer_copy(k, src_ref, land_ref, send_sem, recv_sem, gather):
    x, y, c = _place()
    my_id = 4 * x + 2 * y + c
    px, py, pc = _flip(x, k & 4), _flip(y, k & 2), _flip(c, k & 1)
    peer_id = 4 * px + 2 * py + pc
    return pltpu.make_async_remote_copy(
        src_ref=src_ref if gather else src_ref.at[peer_id], dst_ref=land_ref.at[my_id],
        send_sem=send_sem.at[k - 1], recv_sem=recv_sem.at[k - 1],
        device_id=(px, py, pc), device_id_type=MESH)


def _sequencer_copies(srcs, *, gather, name, collective_id, after):
    n = len(srcs)
    hbm = pltpu.MemorySpace.HBM
    src_refs = [jax.new_ref(s, memory_space=hbm) for s in srcs]
    land_refs = [jax.empty_ref(jax.ShapeDtypeStruct(((N_DEV,) + s.shape) if gather else s.shape, s.dtype),
                               memory_space=hbm) for s in srcs]
    token_in = jax.new_ref(jnp.zeros((8, 128), F32) if after is None else after, memory_space=hbm)
    token_out = jax.empty_ref(jax.ShapeDtypeStruct((8, 128), F32), memory_space=hbm)

    @pl.kernel(mesh=plsc.ScalarSubcoreMesh(axis_name="seq", num_cores=1), name=name,
               scratch_types=(pltpu.SemaphoreType.DMA((n, N_PEERS)), pltpu.SemaphoreType.DMA((n, N_PEERS)),
                              pltpu.SemaphoreType.DMA((n + 1,))),
               compiler_params=pltpu.CompilerParams(collective_id=collective_id))
    def launch(send_sems, recv_sems, local_sems):
        x, y, c = _place()
        my_id = 4 * x + 2 * y + c
        barrier = pltpu.get_barrier_semaphore()
        own = [pltpu.make_async_copy(src_refs[t] if gather else src_refs[t].at[my_id], land_refs[t].at[my_id],
                                     local_sems.at[t]) for t in range(n)]
        if gather:
            sibling = (x, y, 1 - c)
            chips = [(1 - x, y), (x, 1 - y), (1 - x, 1 - y)]
            for peer in [sibling] + [(*chip, c) for chip in chips]:
                pl.semaphore_signal(barrier, inc=1, device_id=peer, device_id_type=MESH)
            pl.semaphore_wait(barrier, 4)

            def copy(t, k, block, to, src=None):
                dst = land_refs[t].at[4 * block[0] + 2 * block[1] + block[2]]
                return pltpu.make_async_remote_copy(
                    src_ref=dst if src is None else src, dst_ref=dst,
                    send_sem=send_sems.at[t, k], recv_sem=recv_sems.at[t, k], device_id=to, device_id_type=MESH)

            for cp in own:
                cp.start()
            sends = []
            for t in range(n):
                sends.append(copy(t, 0, (x, y, c), sibling, src=src_refs[t]))
                sends += [copy(t, 1 + j, (x, y, c), (*chip, c), src=src_refs[t]) for j, chip in enumerate(chips)]
            for cp in sends:
                cp.start()
            for t in range(n):
                for j, chip in enumerate(chips):
                    copy(t, 1 + j, (*chip, c), (x, y, c)).wait_recv()
                    passed_on = copy(t, 4 + j, (*chip, c), sibling)
                    passed_on.start()
                    sends.append(passed_on)
            for t in range(n):
                copy(t, 0, sibling, (x, y, c)).wait_recv()
                for j, chip in enumerate(chips):
                    copy(t, 4 + j, (*chip, 1 - c), (x, y, c)).wait_recv()
            for cp in sends:
                cp.wait_send()
            for cp in own:
                cp.wait()
        else:
            for k in range(1, N_DEV):
                peer = (_flip(x, k & 4), _flip(y, k & 2), _flip(c, k & 1))
                pl.semaphore_signal(barrier, inc=1, device_id=peer, device_id_type=MESH)
            pl.semaphore_wait(barrier, N_PEERS)
            for cp in own:
                cp.start()
            copies = [_peer_copy(k, src_refs[t], land_refs[t], send_sems.at[t], recv_sems.at[t], gather)
                      for t in range(n) for k in range(1, N_DEV)]
            for cp in copies:
                cp.start()
            for cp in own:
                cp.wait()
            for cp in copies:
                cp.wait()
        passed = pltpu.make_async_copy(token_in, token_out, local_sems.at[n])
        passed.start()
        passed.wait()

    launch()
    return [ref[...] for ref in land_refs], token_out[...]


def _all_reduce_small(buf, *, name):
    _, rows, lanes = buf.shape

    def body(in_ref, out_ref, parts, send_sems, recv_sems):
        x, y, c = _place()
        my_id = 4 * x + 2 * y + c
        peers = []
        for k in range(1, N_DEV):
            px, py, pc = _flip(x, k & 4), _flip(y, k & 2), _flip(c, k & 1)
            peers.append(((px, py, pc), 4 * px + 2 * py + pc))
        scatter = [pltpu.make_async_remote_copy(
            src_ref=in_ref.at[pid], dst_ref=parts.at[my_id],
            send_sem=send_sems.at[0, k], recv_sem=recv_sems.at[0, k],
            device_id=peer, device_id_type=MESH) for k, (peer, pid) in enumerate(peers)]
        for cp in scatter:
            cp.start()
        parts[my_id] = in_ref[my_id]
        for cp in scatter:
            cp.wait()
        total = parts[0]
        for d in range(1, N_DEV):
            total = total + parts[d]
        out_ref[my_id] = total
        gather = [pltpu.make_async_remote_copy(
            src_ref=out_ref.at[my_id], dst_ref=out_ref.at[my_id],
            send_sem=send_sems.at[1, k], recv_sem=recv_sems.at[1, k],
            device_id=peer, device_id_type=MESH) for k, (peer, pid) in enumerate(peers)]
        for cp in gather:
            cp.start()
        for k, (peer, pid) in enumerate(peers):
            pltpu.make_async_remote_copy(
                src_ref=out_ref.at[pid], dst_ref=out_ref.at[pid],
                send_sem=send_sems.at[1, k], recv_sem=recv_sems.at[1, k],
                device_id=peer, device_id_type=MESH).wait()

    vmem = pl.BlockSpec(memory_space=pltpu.VMEM)
    return pl.pallas_call(
        body, name=name, in_specs=[vmem], out_specs=vmem,
        out_shape=jax.ShapeDtypeStruct(buf.shape, F32),
        scratch_shapes=[pltpu.VMEM(buf.shape, F32),
                        pltpu.SemaphoreType.DMA((2, 7)), pltpu.SemaphoreType.DMA((2, 7))],
        compiler_params=pltpu.CompilerParams(vmem_limit_bytes=VMEM_LIMIT),
    )(buf)


TM = 512
TMM = 1024
TKW = 2048
MIX_TILE = 128
FFN_BLOCK_TILE = 512


def _block_diag(w):
    wg = w.reshape(N_GROUPS, HEADS_PER_GROUP, LRU_HEAD_DIM, LRU_HEAD_DIM)
    eye = jnp.eye(HEADS_PER_GROUP, dtype=w.dtype)
    bd = wg[:, :, :, None, :] * eye[None, :, None, :, None]
    return bd.reshape(N_GROUPS, LRU_GROUP, LRU_GROUP).astype(BF16)


def _head_blocks(bd):
    b5 = bd.reshape(N_GROUPS, HEADS_PER_GROUP, LRU_HEAD_DIM, HEADS_PER_GROUP, LRU_HEAD_DIM)
    blocks = [b5[:, h, :, h, :] for h in range(HEADS_PER_GROUP)]
    return jnp.stack(blocks, axis=1).reshape(LRU_HEADS, LRU_HEAD_DIM, LRU_HEAD_DIM)


def _w(lw, key, after):
    value = lw[key]
    return value(after) if callable(value) else value


def _layer_fwd(x, lw, tag):
    sv_rows = x.shape[0]
    h1 = _norm_fwd(x, lw["g1"], tm=TM, name=f"norm1_fwd_{tag}")
    z = _mm_nt(h1, _w(lw, "w_in_t", h1), tm=min(TMM, sv_rows), tn=896, tk=D_MODEL, out_dtype=F32, name=f"in_proj_{tag}")
    y_mix, hs = _mixer_fwd(z, _w(lw, "cw", z), lw["cb"], lw["wa_bd"], lw["wx_bd"], lw["ba"], lw["bx"], lw["lam"],
                           _w(lw, "scw", z), tile=MIX_TILE, name=f"mixer_fwd_{tag}")
    x2 = _mm_nn(y_mix, _w(lw, "w_out", y_mix), tm=min(TMM, sv_rows), tn=D_MODEL, tk=D_MIX, out_dtype=F32, name=f"out_proj_{tag}",
                residual=x)
    h2 = _norm_fwd(x2, lw["g2"], tm=TM, name=f"norm2_fwd_{tag}")
    x3, act, p, u = _ffn_block_fwd(x2, h2, _w(lw, "w_up_b", h2), _w(lw, "fcw", h2), _w(lw, "w_down", h2),
                                tile=min(FFN_BLOCK_TILE, sv_rows), name=f"ffn_fwd_{tag}")
    saved = dict(x=x, h1=h1, z=z, y_mix=y_mix, hs=hs, x2=x2, h2=h2, p=p, u=u, act=act)
    return x3, saved


def _layer_bwd(dx3, dx3b, lw, sv, tag, put):
    sv_rows = dx3.shape[0]
    w_in_t, w_out, w_up_b, w_down = (_w(lw, k, dx3) for k in ("w_in_t", "w_out", "w_up_b", "w_down"))
    cw, scw, fcw = (_w(lw, k, dx3) for k in ("cw", "scw", "fcw"))
    g_down = _mm_tn(sv["act"], dx3b, tm=1024, tn=D_MODEL, tk=min(TKW, sv_rows), out_dtype=BF16, name=f"down_bwd_w_{tag}")
    dx2, dx2b, dg2, dp, dfcw = _ffn_block_bwd(dx3, dx3b, sv["p"], sv["u"], sv["x2"], lw["g2"], w_up_b, fcw, w_down,
                                              tile=min(FFN_BLOCK_TILE, sv_rows), name=f"ffn_bwd_{tag}",
                                              token=put("w_down", g_down))
    g_up = _mm_up_bwd_w(sv["h2"], dp, tm=D_MODEL, tk=min(TKW, sv_rows), name=f"up_bwd_w_{tag}")
    dy = _mm_nt(dx2b, w_out, tm=min(TMM, sv_rows), tn=768, tk=D_MODEL, out_dtype=F32, name=f"out_bwd_x_{tag}")
    dz, dcw, dvec, dwa, dwx, dscw = _mixer_bwd(
        sv["z"], sv["hs"], dy, cw, lw["cb"], lw["wa_bd"], lw["wx_bd"], lw["ba"], lw["bx"], lw["lam"],
        scw, tile=MIX_TILE, name=f"mixer_bwd_{tag}", token=put("w_up_b", g_up))
    g_in_t = _mm_tn(dz, sv["h1"], tm=896, tn=D_MODEL, tk=min(TKW, sv_rows), out_dtype=BF16, name=f"in_bwd_w_{tag}")
    dh1 = _mm_nn(dz, w_in_t, tm=min(TMM, sv_rows), tn=D_MODEL, tk=896, out_dtype=F32, name=f"in_bwd_x_{tag}",
                 token=put("w_in_t", g_in_t))
    g_out = _mm_tn(sv["y_mix"], dx2b, tm=768, tn=D_MODEL, tk=min(TKW, sv_rows), out_dtype=BF16, name=f"out_bwd_w_{tag}",
                   token=dh1)
    dx, dxb, dg1 = _norm_bwd(dh1, sv["x"], lw["g1"], dx2, tm=TM, name=f"norm1_bwd_{tag}",
                             token=put("w_out", g_out))
    small = dict(norm1_g=dg1[0], lru_conv_w=dcw[0:4], lru_conv_b=dvec[0], lru_wa=_head_blocks(dwa),
                 lru_ba=dvec[1], lru_wx=_head_blocks(dwx), lru_bx=dvec[2], lru_lambda=dvec[3],
                 sc_conv_w=dscw[0:3], norm2_g=dg2[0], ffn_conv_w=dfcw[:, 0:3, :])
    return dx, dxb, small


SMALL_ORDER = ("norm1_g", "lru_conv_w", "lru_conv_b", "lru_wa", "lru_ba", "lru_wx", "lru_bx", "lru_lambda",
               "sc_conv_w", "norm2_g", "ffn_conv_w")


def _local_step(x, tgt, layers, final_g, put):
    saved = []
    h = x
    for l in range(DEPTH):
        h, sv = _layer_fwd(h, layers[l], f"l{l}")
        saved.append(sv)
    loss_blk, dx, dxb, dgf = _loss_head(h, final_g, tgt, tm=TM, name="loss_head")
    smalls = [None] * DEPTH
    for l in reversed(range(DEPTH)):
        dx, dxb, smalls[l] = _layer_bwd(dx, dxb, layers[l], saved[l], f"l{l}", functools.partial(put, l))
    return loss_blk[0, 0], dx, smalls, dgf[0]


def kernel(x, norm1_g, w_in, lru_conv_w, lru_conv_b, lru_wa, lru_ba, lru_wx, lru_bx, lru_lambda, sc_conv_w, w_out, norm2_g, w_up, ffn_conv_w, w_down, final_g, loss_target, m_norm1_g, m_w_in, m_lru_conv_w, m_lru_conv_b, m_lru_wa, m_lru_ba, m_lru_wx, m_lru_bx, m_lru_lambda, m_sc_conv_w, m_w_out, m_norm2_g, m_w_up, m_ffn_conv_w, m_w_down, m_final_g, v_norm1_g, v_w_in, v_lru_conv_w, v_lru_conv_b, v_lru_wa, v_lru_ba, v_lru_wx, v_lru_bx, v_lru_lambda, v_sc_conv_w, v_w_out, v_norm2_g, v_w_up, v_ffn_conv_w, v_w_down, v_final_g):
    names = ["norm1_g", "w_in", "lru_conv_w", "lru_conv_b", "lru_wa", "lru_ba", "lru_wx", "lru_bx", "lru_lambda",
             "sc_conv_w", "w_out", "norm2_g", "w_up", "ffn_conv_w", "w_down", "final_g"]
    w = dict(zip(names, [norm1_g, w_in, lru_conv_w, lru_conv_b, lru_wa, lru_ba, lru_wx, lru_bx, lru_lambda,
                         sc_conv_w, w_out, norm2_g, w_up, ffn_conv_w, w_down, final_g]))
    m = dict(zip(names, [m_norm1_g, m_w_in, m_lru_conv_w, m_lru_conv_b, m_lru_wa, m_lru_ba, m_lru_wx, m_lru_bx,
                         m_lru_lambda, m_sc_conv_w, m_w_out, m_norm2_g, m_w_up, m_ffn_conv_w, m_w_down, m_final_g]))
    v = dict(zip(names, [v_norm1_g, v_w_in, v_lru_conv_w, v_lru_conv_b, v_lru_wa, v_lru_ba, v_lru_wx, v_lru_bx,
                         v_lru_lambda, v_sc_conv_w, v_w_out, v_norm2_g, v_w_up, v_ffn_conv_w, v_w_down, v_final_g]))
    my_id = 4 * lax.axis_index("x") + 2 * lax.axis_index("y") + lax.axis_index("c")

    taps = jnp.zeros((DEPTH, 16, 768), F32)
    taps = taps.at[:, 0:4, 0:128].set(lru_conv_w).at[:, 4:7, 0:64].set(sc_conv_w).at[:, 8:11, :].set(ffn_conv_w)
    shards = {}
    for l in range(DEPTH):
        shards[f"w_in_t{l}"] = jnp.swapaxes(w_in[l], 0, 1).astype(BF16)
        if l == 0:
            shards["taps"] = taps.reshape(DEPTH * 16, 768)
        shards[f"w_out{l}"] = w_out[l].astype(BF16)
        shards[f"w_up_b{l}"] = w_up[l].astype(BF16)
        shards[f"w_down{l}"] = w_down[l].astype(BF16)
    ids = iter(range(16))
    got = {}
    chain = [None]
    for group in (("w_in_t0", "taps"), ("w_out0",), ("w_up_b0",), ("w_down0",),
                  ("w_in_t1",), ("w_out1",), ("w_up_b1",), ("w_down1",)):
        lands, chain[0] = _sequencer_copies([shards[k] for k in group], gather=True, name=f"gather_{group[0]}",
                                            collective_id=next(ids), after=None)
        got.update(zip(group, lands))

    def fetch(key, after):
        return got[key]

    def tap_rows(l, lo, hi, width, after):
        tl = fetch("taps", after).reshape(N_DEV, DEPTH, 16, 768)[:, l, lo:hi, 0:width]
        return jnp.transpose(tl, (1, 0, 2)).reshape(hi - lo, N_DEV * width)

    layers = []
    for l in range(DEPTH):
        layers.append(dict(
            g1=norm1_g[l], g2=norm2_g[l], cb=lru_conv_b[l], ba=lru_ba[l], bx=lru_bx[l], lam=lru_lambda[l],
            wa_bd=_block_diag(lru_wa[l]), wx_bd=_block_diag(lru_wx[l]),
            cw=functools.partial(tap_rows, l, 0, 4, 128), scw=functools.partial(tap_rows, l, 4, 7, 64),
            fcw=lambda after, l=l: tap_rows(l, 8, 11, 768, after).reshape(3, 2, D_FF).transpose(1, 0, 2),
            w_in_t=lambda after, l=l: fetch(f"w_in_t{l}", after).reshape(D_IN, D_MODEL),
            w_out=lambda after, l=l: fetch(f"w_out{l}", after).reshape(D_MIX, D_MODEL),
            w_up_b=lambda after, l=l: fetch(f"w_up_b{l}", after),
            w_down=lambda after, l=l: fetch(f"w_down{l}", after).reshape(D_FF, D_MODEL)))

    scatter_handles = {}

    def put(l, key, grad):
        blocks = grad if grad.ndim == 3 else grad.reshape(N_DEV, grad.shape[0] // N_DEV, grad.shape[1])
        (scatter_handles[(l, key)],), chain[0] = _sequencer_copies(
            [blocks], gather=False, name=f"scatter_{key}{l}", collective_id=next(ids), after=chain[0])
        return blocks

    loss_local, dx, smalls, dgf = _local_step(x[0], loss_target[0], layers, final_g, put)
    loss = lax.psum(loss_local, ("x", "y", "c"))

    parts = []
    for l in range(DEPTH):
        for key in ("w_in_t", "w_out", "w_up_b", "w_down"):
            parts.append(scatter_handles[(l, key)])

    flat = [smalls[l][k].reshape(-1) for l in range(DEPTH) for k in SMALL_ORDER] + [dgf.reshape(-1)]
    sizes = [f.shape[0] for f in flat]
    total = sum(sizes)
    rows = -(-total // (N_DEV * 128 * 8)) * 8
    flat.append(jnp.zeros((N_DEV * rows * 128 - total,), F32))
    small_sum = _all_reduce_small(jnp.concatenate(flat).reshape(N_DEV, rows, 128), name="reduce_small").reshape(-1)
    small_g, off = [], 0
    for sz in sizes:
        small_g.append(small_sum[off:off + sz])
        off += sz
    gs = {}
    for l in range(DEPTH):
        for i, k in enumerate(SMALL_ORDER):
            gs.setdefault(k, []).append(small_g[l * len(SMALL_ORDER) + i])
    g_final = small_g[-1]

    grads = {}
    per_layer = {k: [] for k in ("w_in", "w_out", "w_up", "w_down")}
    for l in range(DEPTH):
        p_in, p_out, p_up, p_down = parts[4 * l:4 * l + 4]
        per_layer["w_in"].append(jnp.swapaxes(_sum_parts(p_in, name=f"sum_w_in_l{l}"), 0, 1))
        per_layer["w_out"].append(_sum_parts(p_out, name=f"sum_w_out_l{l}"))
        per_layer["w_up"].append(_sum_parts(p_up, name=f"sum_w_up_l{l}"))
        per_layer["w_down"].append(_sum_parts(p_down, name=f"sum_w_down_l{l}"))
    for k, lst in per_layer.items():
        grads[k] = jnp.stack(lst)
    for k in ("norm1_g", "lru_conv_b", "lru_ba", "lru_bx", "lru_lambda", "norm2_g"):
        grads[k] = jnp.stack(gs[k]).reshape(DEPTH, -1)
    for k in ("lru_wa", "lru_wx"):
        grads[k] = jnp.stack(gs[k]).reshape(DEPTH, LRU_HEADS, LRU_HEAD_DIM, LRU_HEAD_DIM)
    grads["final_g"] = g_final
    cw_full = jnp.stack(gs["lru_conv_w"]).reshape(DEPTH, 4, N_DEV, 128)
    grads["lru_conv_w"] = lax.dynamic_index_in_dim(cw_full, my_id, axis=2, keepdims=False)
    scw_full = jnp.stack(gs["sc_conv_w"]).reshape(DEPTH, 3, N_DEV, 64)
    grads["sc_conv_w"] = lax.dynamic_index_in_dim(scw_full, my_id, axis=2, keepdims=False)
    fcw_full = jnp.stack(gs["ffn_conv_w"]).reshape(DEPTH, 2, 3, D_FF).transpose(0, 2, 1, 3).reshape(DEPTH, 3, N_DEV, 768)
    grads["ffn_conv_w"] = lax.dynamic_index_in_dim(fcw_full, my_id, axis=2, keepdims=False)

    deltas, new_m, new_v = {}, {}, {}
    for k in names:
        shape = w[k].shape
        cols = shape[-1]
        as2d = lambda a: a.reshape(-1, cols)
        d, nm, nv = _adamw(as2d(w[k]), as2d(grads[k]), as2d(m[k]), as2d(v[k]), name=f"adamw_{k}")
        deltas[k], new_m[k], new_v[k] = d.reshape(shape), nm.reshape(shape), nv.reshape(shape)

    return (loss, dx[None], *[grads[k] for k in names], *[deltas[k] for k in names],
            *[new_m[k] for k in names], *[new_v[k] for k in names])
```

```python
import functools
import math

import jax
import jax.numpy as jnp
from jax import lax
from jax.experimental import pallas as pl
from jax.experimental.pallas import tpu as pltpu
from jax.experimental.pallas import tpu_sc as plsc

F32 = jnp.float32
BF16 = jnp.bfloat16

N_DEV = 8
DEPTH = 2
D_MODEL = 1024
D_LRU = 1024
D_SC = 512
D_MIX = D_LRU + D_SC
D_IN = 2 * D_LRU + 3 * D_SC
D_FF = 3072
LRU_HEADS = 16
LRU_HEAD_DIM = 64
LRU_GROUP = 256
N_GROUPS = D_LRU // LRU_GROUP
HEADS_PER_GROUP = LRU_GROUP // LRU_HEAD_DIM
RG_C = 8.0
EPS = 1e-6
HALO = 8

ADAM_LR = 0.001
ADAM_B1 = 0.9
ADAM_B2 = 0.999
ADAM_EPS = 1e-08
ADAM_WD = 0.01
ADAM_STEP = 10

GELU_C = math.sqrt(2.0 / math.pi)
GELU_A = 0.044715

VMEM_LIMIT = 56 * 1024 * 1024
MESH = pl.DeviceIdType.MESH


def _params(*sem):
    return pltpu.CompilerParams(dimension_semantics=tuple(sem) if sem else None,
                                vmem_limit_bytes=VMEM_LIMIT)


def _gelu_parts(x):
    x2 = x * x
    t = jnp.tanh(GELU_C * (x + GELU_A * x * x2))
    half = 0.5 * (1.0 + t)
    g = x * half
    dg = half + 0.5 * x * (1.0 - t * t) * (GELU_C * (1.0 + 3.0 * GELU_A * x2))
    return g, dg


def _gelu(x):
    t = jnp.tanh(GELU_C * (x + GELU_A * x * x * x))
    return 0.5 * x * (1.0 + t)


def _sigmoid(x):
    return 1.0 / (1.0 + jnp.exp(-x))


def _softplus(x):
    e = jnp.exp(-jnp.abs(x))
    u = 1.0 + e
    log1p_e = jnp.where(u == 1.0, e, jnp.log(u) * (e / (u - 1.0)))
    return jnp.maximum(x, 0.0) + log1p_e


def _rms(x):
    ms = jnp.mean(x * x, axis=-1, keepdims=True)
    return lax.rsqrt(ms + EPS)


def _dot(a, b, dims):
    return lax.dot_general(a, b, (dims, ((), ())), preferred_element_type=F32)


NN = ((1,), (0,))
NT = ((1,), (1,))
TN = ((0,), (0,))


def _matmul(a, b, *, dims, grid, a_spec, b_spec, o_spec, out_shape, acc_shape, name,
            residual=None, r_spec=None, token=None):
    nk = grid[2]

    def body(*refs):
        a_ref, b_ref = refs[0], refs[1]
        r_ref = refs[2] if residual is not None else None
        o_ref = refs[2 + (residual is not None) + (token is not None)]
        prod = _dot(a_ref[...].astype(BF16), b_ref[...].astype(BF16), dims)

        def finish(total):
            if r_ref is not None:
                total = total + r_ref[...]
            o_ref[...] = total.astype(o_ref.dtype)

        if nk == 1:
            finish(prod)
            return
        acc_ref = refs[-1]
        k = pl.program_id(2)

        @pl.when(k == 0)
        def _():
            acc_ref[...] = prod

        @pl.when(jnp.logical_and(k > 0, k < nk - 1))
        def _():
            acc_ref[...] += prod

        @pl.when(k == nk - 1)
        def _():
            finish(acc_ref[...] + prod)

    in_specs = [a_spec, b_spec]
    args = [a, b]
    if residual is not None:
        in_specs.append(r_spec)
        args.append(residual)
    if token is not None:
        in_specs.append(pl.BlockSpec(memory_space=pl.ANY))
        args.append(token)
    return pl.pallas_call(
        body, name=name, grid=grid, in_specs=in_specs, out_specs=o_spec, out_shape=out_shape,
        scratch_shapes=[pltpu.VMEM(acc_shape, F32)] if nk > 1 else [],
        compiler_params=_params("parallel", "parallel", "arbitrary"),
    )(*args)


def _mm_nn(a, b, *, tm, tn, tk, out_dtype, name, residual=None, token=None):
    m, kd = a.shape
    n = b.shape[1]
    return _matmul(
        a, b, dims=NN, grid=(m // tm, n // tn, kd // tk),
        a_spec=pl.BlockSpec((tm, tk), lambda i, j, k: (i, k)),
        b_spec=pl.BlockSpec((tk, tn), lambda i, j, k: (k, j)),
        o_spec=pl.BlockSpec((tm, tn), lambda i, j, k: (i, j)),
        out_shape=jax.ShapeDtypeStruct((m, n), out_dtype), acc_shape=(tm, tn), name=name,
        residual=residual, r_spec=pl.BlockSpec((tm, tn), lambda i, j, k: (i, j)), token=token)


def _mm_nt(a, b, *, tm, tn, tk, out_dtype, name):
    m, kd = a.shape
    n = b.shape[0]
    return _matmul(
        a, b, dims=NT, grid=(m // tm, n // tn, kd // tk),
        a_spec=pl.BlockSpec((tm, tk), lambda i, j, k: (i, k)),
        b_spec=pl.BlockSpec((tn, tk), lambda i, j, k: (j, k)),
        o_spec=pl.BlockSpec((tm, tn), lambda i, j, k: (i, j)),
        out_shape=jax.ShapeDtypeStruct((m, n), out_dtype), acc_shape=(tm, tn), name=name)


def _mm_tn(a, b, *, tm, tn, tk, out_dtype, name, token=None):
    kd, m = a.shape
    n = b.shape[1]
    return _matmul(
        a, b, dims=TN, grid=(m // tm, n // tn, kd // tk),
        a_spec=pl.BlockSpec((tk, tm), lambda i, j, k: (k, i)),
        b_spec=pl.BlockSpec((tk, tn), lambda i, j, k: (k, j)),
        o_spec=pl.BlockSpec((tm, tn), lambda i, j, k: (i, j)),
        out_shape=jax.ShapeDtypeStruct((m, n), out_dtype), acc_shape=(tm, tn), name=name, token=token)


def _mm_up_bwd_w(h2, dp, *, tm, tk, name):
    s = h2.shape[0]
    nb = D_FF * 2 // N_DEV
    per_half = D_FF // nb
    return _matmul(
        h2, dp, dims=TN, grid=(D_MODEL // tm, N_DEV, s // tk),
        a_spec=pl.BlockSpec((tk, tm), lambda i, j, k: (k, i)),
        b_spec=pl.BlockSpec((None, tk, nb), lambda i, j, k: (j // per_half, k, j % per_half)),
        o_spec=pl.BlockSpec((None, tm, nb), lambda i, j, k: (j, i, 0)),
        out_shape=jax.ShapeDtypeStruct((N_DEV, D_MODEL, nb), BF16), acc_shape=(tm, nb), name=name)


def _behind(token):
    return jnp.zeros((8, 128), F32) if token is None else token


def _norm_in_proj(x, g, w_in_t, *, tm, tn, name):
    s, d = x.shape
    n = w_in_t.shape[0]

    def body(x_ref, g_ref, w_ref, z_ref, h_ref):
        @pl.when(pl.program_id(1) == 0)
        def _():
            xv = x_ref[...]
            h_ref[...] = (xv * _rms(xv) * g_ref[...]).astype(BF16)

        z_ref[...] = _dot(h_ref[...], w_ref[...], NT)

    return pl.pallas_call(
        body, name=name, grid=(s // tm, n // tn),
        in_specs=[pl.BlockSpec((tm, d), lambda i, j: (i, 0)), pl.BlockSpec((1, d), lambda i, j: (0, 0)),
                  pl.BlockSpec((tn, d), lambda i, j: (j, 0))],
        out_specs=[pl.BlockSpec((tm, tn), lambda i, j: (i, j)), pl.BlockSpec((tm, d), lambda i, j: (i, 0))],
        out_shape=[jax.ShapeDtypeStruct((s, n), F32), jax.ShapeDtypeStruct((s, d), BF16)],
        compiler_params=_params("parallel", "arbitrary"),
    )(x, g.reshape(1, d), w_in_t)


def _in_bwd_norm(dz, w_in_t, x, g, dres, *, tm, tk, name, token=None):
    s, kd = dz.shape
    d = w_in_t.shape[1]
    nk = kd // tk

    def body(dz_ref, w_ref, x_ref, g_ref, dres_ref, token_ref, dx_ref, dxb_ref, dg_ref, acc_ref):
        i = pl.program_id(0)
        k = pl.program_id(1)

        @pl.when(jnp.logical_and(i == 0, k == 0))
        def _():
            dg_ref[...] = jnp.zeros_like(dg_ref)

        acc_ref[...] = _dot(dz_ref[...], w_ref[...], NN) + jnp.where(k > 0, acc_ref[...], 0.0)

        @pl.when(k == nk - 1)
        def _():
            dh = acc_ref[...]
            xv = x_ref[...]
            rstd = _rms(xv)
            n = xv * rstd
            dn = dh * g_ref[...]
            dx = dres_ref[...] + rstd * (dn - n * jnp.mean(dn * n, axis=-1, keepdims=True))
            dx_ref[...] = dx
            dxb_ref[...] = dx.astype(BF16)
            dg_ref[0:1, :] += jnp.sum(dh * n, axis=0, keepdims=True)

    row = pl.BlockSpec((tm, d), lambda i, k: (i, 0))
    return pl.pallas_call(
        body, name=name, grid=(s // tm, nk),
        in_specs=[pl.BlockSpec((tm, tk), lambda i, k: (i, k)), pl.BlockSpec((tk, d), lambda i, k: (k, 0)),
                  row, pl.BlockSpec((1, d), lambda i, k: (0, 0)), row, pl.BlockSpec(memory_space=pl.ANY)],
        out_specs=[row, row, pl.BlockSpec((8, d), lambda i, k: (0, 0))],
        out_shape=[jax.ShapeDtypeStruct((s, d), F32), jax.ShapeDtypeStruct((s, d), BF16),
                   jax.ShapeDtypeStruct((8, d), F32)],
        scratch_shapes=[pltpu.VMEM((tm, d), F32)],
        compiler_params=_params("arbitrary", "arbitrary"),
    )(dz, w_in_t, x, g.reshape(1, d), dres, _behind(token))


def _loss_head(x, g, tgt, *, tm, name):
    s, d = x.shape

    def body(x_ref, g_ref, t_ref, loss_ref, dx_ref, dxb_ref, dg_ref):
        @pl.when(pl.program_id(0) == 0)
        def _():
            dg_ref[...] = jnp.zeros_like(dg_ref)
            loss_ref[...] = jnp.zeros_like(loss_ref)

        xv = x_ref[...]
        gv = g_ref[...]
        rstd = _rms(xv)
        n = xv * rstd
        e = n * gv - t_ref[...]
        part = 0.5 * jnp.sum(jnp.mean(e * e, axis=-1, keepdims=True), axis=0, keepdims=True)
        loss_ref[...] += jnp.broadcast_to(part, loss_ref.shape)
        dy = e * (1.0 / d)
        dn = dy * gv
        dx = rstd * (dn - n * jnp.mean(dn * n, axis=-1, keepdims=True))
        dx_ref[...] = dx
        dxb_ref[...] = dx.astype(BF16)
        dg_ref[0:1, :] += jnp.sum(dy * n, axis=0, keepdims=True)

    return pl.pallas_call(
        body, name=name, grid=(s // tm,),
        in_specs=[pl.BlockSpec((tm, d), lambda i: (i, 0)), pl.BlockSpec((1, d), lambda i: (0, 0)),
                  pl.BlockSpec((tm, d), lambda i: (i, 0))],
        out_specs=[pl.BlockSpec((8, 128), lambda i: (0, 0)), pl.BlockSpec((tm, d), lambda i: (i, 0)),
                   pl.BlockSpec((tm, d), lambda i: (i, 0)), pl.BlockSpec((8, d), lambda i: (0, 0))],
        out_shape=[jax.ShapeDtypeStruct((8, 128), F32), jax.ShapeDtypeStruct((s, d), F32),
                   jax.ShapeDtypeStruct((s, d), BF16), jax.ShapeDtypeStruct((8, d), F32)],
        compiler_params=_params("arbitrary"),
    )(x, g.reshape(1, d), tgt)


def _scan_rows(a_ref, b_ref, h_ref, carry, *, rows, reverse):
    width = a_ref.shape[1]
    n_chunks = rows // 8
    row = lax.broadcasted_iota(jnp.int32, (8, width), 0)

    def step(ci, carry):
        chunk = (n_chunks - 1 - ci) if reverse else ci
        off = pl.multiple_of(chunk * 8, 8)
        av = a_ref[pl.ds(off, 8), :]
        bv = b_ref[pl.ds(off, 8), :]
        for sh in (1, 2, 4):
            if reverse:
                a_sh = pltpu.roll(av, 8 - sh, 0)
                b_sh = pltpu.roll(bv, 8 - sh, 0)
                m = row < 8 - sh
            else:
                a_sh = pltpu.roll(av, sh, 0)
                b_sh = pltpu.roll(bv, sh, 0)
                m = row >= sh
            bv = jnp.where(m, av * b_sh + bv, bv)
            av = jnp.where(m, av * a_sh, av)
        h = av * carry + bv
        h_ref[pl.ds(off, 8), :] = h
        return h[0:1, :] if reverse else h[7:8, :]

    return lax.fori_loop(0, n_chunks, step, carry)


def _lru_gates(lx, wa_ref, wx_ref, ba, bx, sp):
    lxb = lx.astype(BF16)
    pre_r = jnp.concatenate(
        [_dot(lxb[:, g * LRU_GROUP:(g + 1) * LRU_GROUP], wa_ref[g], NN) for g in range(N_GROUPS)], axis=1)
    pre_i = jnp.concatenate(
        [_dot(lxb[:, g * LRU_GROUP:(g + 1) * LRU_GROUP], wx_ref[g], NN) for g in range(N_GROUPS)], axis=1)
    r = _sigmoid(pre_r + ba)
    ig = _sigmoid(pre_i + bx)
    log_a = (-RG_C * r) * sp
    a = jnp.exp(log_a)
    mult = jnp.sqrt(-jnp.tanh(log_a) * (a * a + 1.0))
    return lxb, r, ig, a, mult


def _mixer_fwd(z, cw, cb, wa_bd, wx_bd, ba, bx, lam, scw, *, tile, name):
    s = z.shape[0]
    n_tiles = s // tile

    def body(z_ref, cw_ref, cb_ref, wa_ref, wx_ref, ba_ref, bx_ref, lam_ref, scw_ref,
             y_ref, hs_ref, ext_lx, ext_q, a_s, b_s, h_car):
        i = pl.program_id(0)

        @pl.when(i == 0)
        def _():
            ext_lx[0:HALO, :] = jnp.zeros((HALO, D_LRU), F32)
            ext_q[0:HALO, :] = jnp.zeros((HALO, D_SC), F32)
            h_car[...] = jnp.zeros_like(h_car)

        ext_lx[HALO:HALO + tile, :] = z_ref[:, 0:D_LRU]
        ext_q[HALO:HALO + tile, :] = z_ref[:, 2 * D_LRU + D_SC:2 * D_LRU + 2 * D_SC] * z_ref[:, 2 * D_LRU + 2 * D_SC:D_IN]
        lx = cb_ref[...] + cw_ref[0:1, :] * ext_lx[pl.ds(HALO - 3, tile), :]
        for k in range(1, 4):
            lx = lx + cw_ref[k:k + 1, :] * ext_lx[pl.ds(HALO - 3 + k, tile), :]
        cq = scw_ref[0:1, :] * ext_q[pl.ds(HALO - 2, tile), :]
        for k in range(1, 3):
            cq = cq + scw_ref[k:k + 1, :] * ext_q[pl.ds(HALO - 2 + k, tile), :]
        ext_lx[0:HALO, :] = ext_lx[tile:tile + HALO, :]
        ext_q[0:HALO, :] = ext_q[tile:tile + HALO, :]

        sp = _softplus(-lam_ref[...])
        _, _, ig, a, mult = _lru_gates(lx, wa_ref, wx_ref, ba_ref[...], bx_ref[...], sp)
        a_s[...] = a
        b_s[...] = mult * (ig * lx)
        h_car[0:1, :] = _scan_rows(a_s, b_s, hs_ref, h_car[0:1, :], rows=tile, reverse=False)

        y_ref[:, 0:D_LRU] = (hs_ref[...] * _gelu(z_ref[:, D_LRU:2 * D_LRU])).astype(BF16)
        y_ref[:, D_LRU:D_MIX] = (z_ref[:, 2 * D_LRU:2 * D_LRU + D_SC] * cq).astype(BF16)

    full = lambda shape: pl.BlockSpec(shape, lambda i: (0,) * len(shape))
    return pl.pallas_call(
        body, name=name, grid=(n_tiles,),
        in_specs=[pl.BlockSpec((tile, D_IN), lambda i: (i, 0)),
                  full((4, D_LRU)), full((1, D_LRU)),
                  full((N_GROUPS, LRU_GROUP, LRU_GROUP)), full((N_GROUPS, LRU_GROUP, LRU_GROUP)),
                  full((1, D_LRU)), full((1, D_LRU)), full((1, D_LRU)), full((3, D_SC))],
        out_specs=[pl.BlockSpec((tile, D_MIX), lambda i: (i, 0)), pl.BlockSpec((tile, D_LRU), lambda i: (i, 0))],
        out_shape=[jax.ShapeDtypeStruct((s, D_MIX), BF16), jax.ShapeDtypeStruct((s, D_LRU), F32)],
        scratch_shapes=[pltpu.VMEM((tile + HALO, D_LRU), F32), pltpu.VMEM((tile + HALO, D_SC), F32),
                        pltpu.VMEM((tile, D_LRU), F32), pltpu.VMEM((tile, D_LRU), F32),
                        pltpu.VMEM((8, D_LRU), F32)],
        compiler_params=_params("arbitrary"),
    )(z, cw, cb.reshape(1, -1), wa_bd, wx_bd, ba.reshape(1, -1), bx.reshape(1, -1), lam.reshape(1, -1), scw)


def _mixer_bwd(z, hs, dy, cw, cb, wa_bd, wx_bd, ba, bx, lam, scw, *, tile, name, token=None):
    s = z.shape[0]
    n_tiles = s // tile
    per8 = tile // 8

    def body(z_ref, zp_ref, hs_ref, hsp_ref, dy_ref, cw_ref, cb_ref, wa_ref, wx_ref, ba_ref, bx_ref, lam_ref, scw_ref,
             token_ref, dz_ref, dcw_ref, dvec_ref, dwa_ref, dwx_ref, dscw_ref,
             ext_lx, ext_q, ext_h, ext_a, ext_dlx, ext_dcq, a_s, b_s, lam_s, l_car):
        i = pl.program_id(0)
        first_tile = i == n_tiles - 1

        @pl.when(i == 0)
        def _():
            for ref in (dcw_ref, dvec_ref, dwa_ref, dwx_ref, dscw_ref, l_car):
                ref[...] = jnp.zeros_like(ref)
            ext_a[tile:tile + HALO, :] = jnp.zeros((HALO, D_LRU), F32)
            ext_dlx[tile:tile + HALO, :] = jnp.zeros((HALO, D_LRU), F32)
            ext_dcq[tile:tile + HALO, :] = jnp.zeros((HALO, D_SC), F32)

        keep = jnp.where(first_tile, 0.0, 1.0)
        sb = z_ref[:, 2 * D_LRU:2 * D_LRU + D_SC]
        sc = z_ref[:, 2 * D_LRU + D_SC:2 * D_LRU + 2 * D_SC]
        sx = z_ref[:, 2 * D_LRU + 2 * D_SC:D_IN]
        ext_lx[0:HALO, :] = zp_ref[:, 0:D_LRU] * keep
        ext_lx[HALO:HALO + tile, :] = z_ref[:, 0:D_LRU]
        ext_q[0:HALO, :] = zp_ref[:, 2 * D_LRU + D_SC:2 * D_LRU + 2 * D_SC] * zp_ref[:, 2 * D_LRU + 2 * D_SC:D_IN] * keep
        ext_q[HALO:HALO + tile, :] = sc * sx
        ext_h[0:HALO, :] = hsp_ref[...] * keep
        ext_h[HALO:HALO + tile, :] = hs_ref[...]

        lx = cb_ref[...] + cw_ref[0:1, :] * ext_lx[pl.ds(HALO - 3, tile), :]
        for k in range(1, 4):
            lx = lx + cw_ref[k:k + 1, :] * ext_lx[pl.ds(HALO - 3 + k, tile), :]
        cq = scw_ref[0:1, :] * ext_q[pl.ds(HALO - 2, tile), :]
        for k in range(1, 3):
            cq = cq + scw_ref[k:k + 1, :] * ext_q[pl.ds(HALO - 2 + k, tile), :]

        sp = _softplus(-lam_ref[...])
        lxb, r, ig, a, mult = _lru_gates(lx, wa_ref, wx_ref, ba_ref[...], bx_ref[...], sp)

        ge, dge = _gelu_parts(z_ref[:, D_LRU:2 * D_LRU])
        dy_lru = dy_ref[:, 0:D_LRU]
        dz_ref[:, D_LRU:2 * D_LRU] = (dy_lru * hs_ref[...] * dge).astype(BF16)

        ext_a[0:tile, :] = a
        a_s[...] = ext_a[pl.ds(1, tile), :]
        b_s[...] = dy_lru * ge
        l_car[0:1, :] = _scan_rows(a_s, b_s, lam_s, l_car[0:1, :], rows=tile, reverse=True)
        ext_a[tile:tile + HALO, :] = ext_a[0:HALO, :]
        lv = lam_s[...]

        da = lv * ext_h[pl.ds(HALO - 1, tile), :]
        d_mult = lv * ig * lx
        d_i = lv * mult * lx
        dlx = lv * mult * ig
        dlog_a = da * a - d_mult * (a * a) / mult
        d_r = dlog_a * (-RG_C * sp)
        dpre_r = d_r * r * (1.0 - r)
        dpre_i = d_i * ig * (1.0 - ig)
        dvec_ref[1:2, :] += jnp.sum(dpre_r, axis=0, keepdims=True)
        dvec_ref[2:3, :] += jnp.sum(dpre_i, axis=0, keepdims=True)
        dvec_ref[3:4, :] += jnp.sum(dlog_a * (-RG_C * r), axis=0, keepdims=True)
        dpr_b = dpre_r.astype(BF16)
        dpi_b = dpre_i.astype(BF16)
        back = []
        for g in range(N_GROUPS):
            cols = slice(g * LRU_GROUP, (g + 1) * LRU_GROUP)
            dwa_ref[g] += _dot(lxb[:, cols], dpr_b[:, cols], TN)
            dwx_ref[g] += _dot(lxb[:, cols], dpi_b[:, cols], TN)
            back.append(_dot(dpr_b[:, cols], wa_ref[g], NT) + _dot(dpi_b[:, cols], wx_ref[g], NT))
        dlx = dlx + jnp.concatenate(back, axis=1)
        dvec_ref[0:1, :] += jnp.sum(dlx, axis=0, keepdims=True)

        ext_dlx[0:tile, :] = dlx
        for k in range(4):
            dcw_ref[k:k + 1, :] += jnp.sum(dlx * ext_lx[pl.ds(HALO - 3 + k, tile), :], axis=0, keepdims=True)
        dlxp = cw_ref[3:4, :] * dlx
        for k in range(3):
            dlxp = dlxp + cw_ref[k:k + 1, :] * ext_dlx[pl.ds(3 - k, tile), :]
        dz_ref[:, 0:D_LRU] = dlxp.astype(BF16)
        ext_dlx[tile:tile + HALO, :] = ext_dlx[0:HALO, :]

        dy_sc = dy_ref[:, D_LRU:D_MIX]
        dz_ref[:, 2 * D_LRU:2 * D_LRU + D_SC] = (dy_sc * cq).astype(BF16)
        dcq = dy_sc * sb
        ext_dcq[0:tile, :] = dcq
        for k in range(3):
            dscw_ref[k:k + 1, :] += jnp.sum(dcq * ext_q[pl.ds(HALO - 2 + k, tile), :], axis=0, keepdims=True)
        dq = scw_ref[2:3, :] * dcq
        for k in range(2):
            dq = dq + scw_ref[k:k + 1, :] * ext_dcq[pl.ds(2 - k, tile), :]
        dz_ref[:, 2 * D_LRU + D_SC:2 * D_LRU + 2 * D_SC] = (dq * sx).astype(BF16)
        dz_ref[:, 2 * D_LRU + 2 * D_SC:D_IN] = (dq * sc).astype(BF16)
        ext_dcq[tile:tile + HALO, :] = ext_dcq[0:HALO, :]

        @pl.when(i == n_tiles - 1)
        def _():
            dvec_ref[3:4, :] = dvec_ref[3:4, :] * (-_sigmoid(-lam_ref[...]))

    rev = lambda i: n_tiles - 1 - i
    prev8 = lambda i: jnp.maximum(rev(i) * per8 - 1, 0)
    full = lambda shape: pl.BlockSpec(shape, lambda i: (0,) * len(shape))
    return pl.pallas_call(
        body, name=name, grid=(n_tiles,),
        in_specs=[pl.BlockSpec((tile, D_IN), lambda i: (rev(i), 0)),
                  pl.BlockSpec((HALO, D_IN), lambda i: (prev8(i), 0)),
                  pl.BlockSpec((tile, D_LRU), lambda i: (rev(i), 0)),
                  pl.BlockSpec((HALO, D_LRU), lambda i: (prev8(i), 0)),
                  pl.BlockSpec((tile, D_MIX), lambda i: (rev(i), 0)),
                  full((4, D_LRU)), full((1, D_LRU)),
                  full((N_GROUPS, LRU_GROUP, LRU_GROUP)), full((N_GROUPS, LRU_GROUP, LRU_GROUP)),
                  full((1, D_LRU)), full((1, D_LRU)), full((1, D_LRU)), full((3, D_SC)),
                  pl.BlockSpec(memory_space=pl.ANY)],
        out_specs=[pl.BlockSpec((tile, D_IN), lambda i: (rev(i), 0)),
                   full((8, D_LRU)), full((8, D_LRU)),
                   full((N_GROUPS, LRU_GROUP, LRU_GROUP)), full((N_GROUPS, LRU_GROUP, LRU_GROUP)),
                   full((8, D_SC))],
        out_shape=[jax.ShapeDtypeStruct((s, D_IN), BF16),
                   jax.ShapeDtypeStruct((8, D_LRU), F32), jax.ShapeDtypeStruct((8, D_LRU), F32),
                   jax.ShapeDtypeStruct((N_GROUPS, LRU_GROUP, LRU_GROUP), F32),
                   jax.ShapeDtypeStruct((N_GROUPS, LRU_GROUP, LRU_GROUP), F32),
                   jax.ShapeDtypeStruct((8, D_SC), F32)],
        scratch_shapes=[pltpu.VMEM((tile + HALO, D_LRU), F32), pltpu.VMEM((tile + HALO, D_SC), F32),
                        pltpu.VMEM((tile + HALO, D_LRU), F32), pltpu.VMEM((tile + HALO, D_LRU), F32),
                        pltpu.VMEM((tile + HALO, D_LRU), F32), pltpu.VMEM((tile + HALO, D_SC), F32),
                        pltpu.VMEM((tile, D_LRU), F32), pltpu.VMEM((tile, D_LRU), F32),
                        pltpu.VMEM((tile, D_LRU), F32), pltpu.VMEM((8, D_LRU), F32)],
        compiler_params=_params("arbitrary"),
    )(z, z, hs, hs, dy, cw, cb.reshape(1, -1), wa_bd, wx_bd, ba.reshape(1, -1), bx.reshape(1, -1),
      lam.reshape(1, -1), scw, _behind(token))


FFN_ROWS = 16
FFN_GROUPS = 2


def _spread_taps(fw_ref, taps):
    for half in range(2):
        for k in range(3):
            taps[half, k] = jnp.broadcast_to(fw_ref[half, k:k + 1, :], taps.shape[2:])


def _rows_from(first, second, start):
    stack = jnp.concatenate([first, second], axis=0)
    return pltpu.roll(stack, 2 * FFN_ROWS - start, 0)[0:FFN_ROWS]


def _conv3_rows(taps, ext_ref, half, row):
    before = ext_ref[half, row - FFN_ROWS:row, :]
    here = ext_ref[half, row:row + FFN_ROWS, :]
    acc = taps[half, 2] * here
    for k in range(2):
        acc = acc + taps[half, k] * _rows_from(before, here, FFN_ROWS - 2 + k)
    return acc


HALO_B = 16


def _ffn_block_fwd(x2, g2, w_up_b, fcw, w_down, *, tile, name):
    s = x2.shape[0]
    nb = w_up_b.shape[2]
    blocks = D_FF // nb
    per16 = tile // HALO_B

    def body(x2_ref, x2p_ref, g_ref, wg_ref, wu_ref, fw_ref, wd_ref, x3_ref, h_ref, act_ref, p_ref, u_ref,
             ext_p, acc_ref, taps, lhs):
        i = pl.program_id(0)
        j = pl.program_id(1)
        keep = jnp.where(i == 0, 0.0, 1.0)
        _spread_taps(fw_ref, taps)
        @pl.when(j == 0)
        def _():
            for rows_ref, at in ((x2p_ref, 0), (x2_ref, HALO_B)):
                xv = rows_ref[...]
                lhs[at:at + xv.shape[0], :] = (xv * _rms(xv) * g_ref[...]).astype(BF16)
            h_ref[...] = lhs[HALO_B:HALO_B + tile, :]

        grp = tile // FFN_GROUPS
        for g in range(FFN_GROUPS):
            new = slice(g * grp + (HALO_B if g else 0), (g + 1) * grp + HALO_B)
            for half, w_ref in ((0, wg_ref), (1, wu_ref)):
                pe = _dot(lhs[new, :], w_ref[...], NN)
                if g == 0:
                    ext_p[half, 0:HALO_B, :] = pe[0:HALO_B] * keep
                    ext_p[half, HALO_B:grp + HALO_B, :] = pe[HALO_B:]
                    p_ref[half, 0:grp, :] = pe[HALO_B:].astype(BF16)
                else:
                    ext_p[half, new, :] = pe
                    p_ref[half, g * grp:(g + 1) * grp, :] = pe.astype(BF16)
        for g in range(FFN_GROUPS):
            rows = slice(g * grp, (g + 1) * grp)
            acts = []
            for r0 in range(g * grp, (g + 1) * grp, FFN_ROWS):
                u = [_conv3_rows(taps, ext_p, half, HALO_B + r0) for half in range(2)]
                for half in range(2):
                    u_ref[half, r0:r0 + FFN_ROWS, :] = u[half].astype(BF16)
                acts.append((_gelu(u[0]) * u[1]).astype(BF16))
                act_ref[r0:r0 + FFN_ROWS, :] = acts[-1]
            contrib = _dot(jnp.concatenate(acts, axis=0), wd_ref[...], NN)
            acc_ref[rows, :] = contrib + jnp.where(j > 0, acc_ref[rows, :], 0.0)

        @pl.when(j == blocks - 1)
        def _():
            x3_ref[...] = x2_ref[...] + acc_ref[...]

    return pl.pallas_call(
        body, name=name, grid=(s // tile, blocks),
        in_specs=[pl.BlockSpec((tile, D_MODEL), lambda i, j: (i, 0)),
                  pl.BlockSpec((HALO_B, D_MODEL), lambda i, j: (jnp.maximum(i * per16 - 1, 0), 0)),
                  pl.BlockSpec((1, D_MODEL), lambda i, j: (0, 0)),
                  pl.BlockSpec((None, D_MODEL, nb), lambda i, j: (j, 0, 0)),
                  pl.BlockSpec((None, D_MODEL, nb), lambda i, j: (j + blocks, 0, 0)),
                  pl.BlockSpec((2, 3, nb), lambda i, j: (0, 0, j)),
                  pl.BlockSpec((nb, D_MODEL), lambda i, j: (j, 0))],
        out_specs=[pl.BlockSpec((tile, D_MODEL), lambda i, j: (i, 0)),
                   pl.BlockSpec((tile, D_MODEL), lambda i, j: (i, 0)),
                   pl.BlockSpec((tile, nb), lambda i, j: (i, j)),
                   pl.BlockSpec((2, tile, nb), lambda i, j: (0, i, j)),
                   pl.BlockSpec((2, tile, nb), lambda i, j: (0, i, j))],
        out_shape=[jax.ShapeDtypeStruct((s, D_MODEL), F32), jax.ShapeDtypeStruct((s, D_MODEL), BF16),
                   jax.ShapeDtypeStruct((s, D_FF), BF16),
                   jax.ShapeDtypeStruct((2, s, D_FF), BF16), jax.ShapeDtypeStruct((2, s, D_FF), BF16)],
        scratch_shapes=[pltpu.VMEM((2, tile + HALO_B, nb), F32), pltpu.VMEM((tile, D_MODEL), F32),
                        pltpu.VMEM((2, 3, FFN_ROWS, nb), F32), pltpu.VMEM((tile + HALO_B, D_MODEL), BF16)],
        compiler_params=_params("parallel", "arbitrary"),
    )(x2, x2, g2.reshape(1, -1), w_up_b, w_up_b, fcw, w_down)


def _ffn_block_bwd(dx3, dx3b, p, u, x2, g2, w_up_b, fcw, w_down, *, tile, name, token=None):
    s = x2.shape[0]
    nb = w_up_b.shape[2]
    blocks = D_FF // nb
    n_tiles = s // tile
    per16 = tile // HALO_B
    last16 = s // HALO_B - 1

    def body(dxb_ref, dxbn_ref, wd_ref, p_ref, u_ref, un_ref, fw_ref, wg_ref, wu_ref, x2_ref, g_ref, dx3_ref,
             token_ref, dx2_ref, dx2b_ref, dg_ref, dp_ref, dw_ref, da_s, acc_w, acc_dh, taps):
        i = pl.program_id(0)
        j = pl.program_id(1)

        @pl.when(jnp.logical_and(i == 0, j == 0))
        def _():
            acc_w[...] = jnp.zeros_like(acc_w)
            dg_ref[...] = jnp.zeros_like(dg_ref)

        keep_next = jnp.where(i == n_tiles - 1, 0.0, 1.0)
        _spread_taps(fw_ref, taps)
        lhs = jnp.concatenate([dxb_ref[...], dxbn_ref[...]], axis=0)
        grp = tile // FFN_GROUPS
        for g in reversed(range(FFN_GROUPS)):
            new = slice(g * grp, (g + 1) * grp + (HALO_B if g == FFN_GROUPS - 1 else 0))
            da_s[new, :] = _dot(lhs[new], wd_ref[...], NT)

        def du_rows(da, u_gate, u_up):
            ge, dge = _gelu_parts(u_gate)
            return da * u_up * dge, da * ge

        after = du_rows(da_s[tile:tile + HALO_B, :] * keep_next, un_ref[0].astype(F32), un_ref[1].astype(F32))
        for g in reversed(range(FFN_GROUPS)):
            rows = slice(g * grp, (g + 1) * grp)
            dps = ([], [])
            for r0 in range((g + 1) * grp - FFN_ROWS, g * grp - 1, -FFN_ROWS):
                du = du_rows(da_s[r0:r0 + FFN_ROWS, :], u_ref[0, r0:r0 + FFN_ROWS, :].astype(F32),
                             u_ref[1, r0:r0 + FFN_ROWS, :].astype(F32))
                for half in range(2):
                    below = [du[half], _rows_from(du[half], after[half], 1), _rows_from(du[half], after[half], 2)]
                    acc = taps[half, 2] * below[0]
                    for k in range(2):
                        acc = acc + taps[half, k] * below[2 - k]
                    dps[half].insert(0, acc.astype(BF16))
                    dp_ref[half, r0:r0 + FFN_ROWS, :] = dps[half][0]
                    p_rows = p_ref[half, r0:r0 + FFN_ROWS, :].astype(F32)
                    for k in range(3):
                        prod = below[2 - k] * p_rows
                        acc_w[j, half, k] += sum(prod[q:q + 8] for q in range(0, FFN_ROWS, 8))
                after = du
            contrib = (_dot(jnp.concatenate(dps[0], axis=0), wg_ref[...], NT)
                       + _dot(jnp.concatenate(dps[1], axis=0), wu_ref[...], NT))
            acc_dh[rows, :] = contrib + jnp.where(j > 0, acc_dh[rows, :], 0.0)

        @pl.when(j == blocks - 1)
        def _():
            dh = acc_dh[...]
            xv = x2_ref[...]
            rstd = _rms(xv)
            n = xv * rstd
            dn = dh * g_ref[...]
            dx = dx3_ref[...] + rstd * (dn - n * jnp.mean(dn * n, axis=-1, keepdims=True))
            dx2_ref[...] = dx
            dx2b_ref[...] = dx.astype(BF16)
            dg_ref[0:1, :] += jnp.sum(dh * n, axis=0, keepdims=True)

        @pl.when(jnp.logical_and(i == n_tiles - 1, j == blocks - 1))
        def _():
            dw_ref[...] = jnp.zeros_like(dw_ref)
            for jj in range(blocks):
                for half in range(2):
                    for k in range(3):
                        dw_ref[half, k:k + 1, jj * nb:(jj + 1) * nb] = jnp.sum(acc_w[jj, half, k], axis=0, keepdims=True)

    next16 = lambda i: jnp.minimum((i + 1) * per16, last16)
    return pl.pallas_call(
        body, name=name, grid=(n_tiles, blocks),
        in_specs=[pl.BlockSpec((tile, D_MODEL), lambda i, j: (i, 0)),
                  pl.BlockSpec((HALO_B, D_MODEL), lambda i, j: (next16(i), 0)),
                  pl.BlockSpec((nb, D_MODEL), lambda i, j: (j, 0)),
                  pl.BlockSpec((2, tile, nb), lambda i, j: (0, i, j)),
                  pl.BlockSpec((2, tile, nb), lambda i, j: (0, i, j)),
                  pl.BlockSpec((2, HALO_B, nb), lambda i, j: (0, next16(i), j)),
                  pl.BlockSpec((2, 3, nb), lambda i, j: (0, 0, j)),
                  pl.BlockSpec((None, D_MODEL, nb), lambda i, j: (j, 0, 0)),
                  pl.BlockSpec((None, D_MODEL, nb), lambda i, j: (j + blocks, 0, 0)),
                  pl.BlockSpec((tile, D_MODEL), lambda i, j: (i, 0)),
                  pl.BlockSpec((1, D_MODEL), lambda i, j: (0, 0)),
                  pl.BlockSpec((tile, D_MODEL), lambda i, j: (i, 0)),
                  pl.BlockSpec(memory_space=pl.ANY)],
        out_specs=[pl.BlockSpec((tile, D_MODEL), lambda i, j: (i, 0)),
                   pl.BlockSpec((tile, D_MODEL), lambda i, j: (i, 0)),
                   pl.BlockSpec((8, D_MODEL), lambda i, j: (0, 0)),
                   pl.BlockSpec((2, tile, nb), lambda i, j: (0, i, j)),
                   pl.BlockSpec((2, 8, D_FF), lambda i, j: (0, 0, 0))],
        out_shape=[jax.ShapeDtypeStruct((s, D_MODEL), F32), jax.ShapeDtypeStruct((s, D_MODEL), BF16),
                   jax.ShapeDtypeStruct((8, D_MODEL), F32), jax.ShapeDtypeStruct((2, s, D_FF), BF16),
                   jax.ShapeDtypeStruct((2, 8, D_FF), F32)],
        scratch_shapes=[pltpu.VMEM((tile + HALO_B, nb), F32), pltpu.VMEM((blocks, 2, 3, 8, nb), F32),
                        pltpu.VMEM((tile, D_MODEL), F32), pltpu.VMEM((2, 3, FFN_ROWS, nb), F32)],
        compiler_params=_params("arbitrary", "arbitrary"),
    )(dx3b, dx3b, w_down, p, u, u, fcw, w_up_b, w_up_b, x2, g2.reshape(1, -1), dx3, _behind(token))


def _adamw_math(w, g, m, v):
    m = ADAM_B1 * m + (1.0 - ADAM_B1) * g
    v = ADAM_B2 * v + (1.0 - ADAM_B2) * (g * g)
    m_hat = m / (1.0 - ADAM_B1 ** ADAM_STEP)
    v_hat = v / (1.0 - ADAM_B2 ** ADAM_STEP)
    delta = -ADAM_LR * (m_hat / (jnp.sqrt(v_hat) + ADAM_EPS) + ADAM_WD * w)
    return delta, m, v


def _adamw(w, g, m, v, *, name):
    rows, cols = w.shape
    tr = rows
    for cand in (512, 256, 128, 64, 32, 16, 8):
        if rows % cand == 0 and rows > cand:
            tr = cand
            break

    def body(w_ref, g_ref, m_ref, v_ref, d_ref, nm_ref, nv_ref):
        d, nm, nv = _adamw_math(w_ref[...], g_ref[...], m_ref[...], v_ref[...])
        d_ref[...] = d
        nm_ref[...] = nm
        nv_ref[...] = nv

    spec = pl.BlockSpec((tr, cols), lambda i: (i, 0))
    return pl.pallas_call(
        body, name=name, grid=(rows // tr,), in_specs=[spec] * 4, out_specs=[spec] * 3,
        out_shape=[jax.ShapeDtypeStruct((rows, cols), F32)] * 3,
        compiler_params=_params("parallel"),
    )(w, g, m, v)


def _sum_parts(parts, *, name):
    _, rows, cols = parts.shape
    tr = rows
    for cand in (256, 128, 64, 32, 16):
        if rows % cand == 0 and rows > cand:
            tr = cand
            break

    def body(p_ref, o_ref):
        acc = p_ref[0].astype(F32)
        for d in range(1, N_DEV):
            acc = acc + p_ref[d].astype(F32)
        o_ref[...] = acc

    return pl.pallas_call(
        body, name=name, grid=(rows // tr,),
        in_specs=[pl.BlockSpec((N_DEV, tr, cols), lambda i: (0, i, 0))],
        out_specs=pl.BlockSpec((tr, cols), lambda i: (i, 0)),
        out_shape=jax.ShapeDtypeStruct((rows, cols), F32),
        compiler_params=_params("parallel"),
    )(parts)


def _place():
    return lax.axis_index("x"), lax.axis_index("y"), lax.axis_index("c")


def _flip(v, bit):
    return 1 - v if bit else v


N_PEERS = N_DEV - 1


def _peer_copy(k, src_ref, land_ref, send_sem, recv_sem, gather):
    x, y, c = _place()
    my_id = 4 * x + 2 * y + c
    px, py, pc = _flip(x, k & 4), _flip(y, k & 2), _flip(c, k & 1)
    peer_id = 4 * px + 2 * py + pc
    return pltpu.make_async_remote_copy(
        src_ref=src_ref if gather else src_ref.at[peer_id], dst_ref=land_ref.at[my_id],
        send_sem=send_sem.at[k - 1], recv_sem=recv_sem.at[k - 1],
        device_id=(px, py, pc), device_id_type=MESH)


def _sequencer_copies(srcs, *, gather, name, collective_id, after):
    n = len(srcs)
    hbm = pltpu.MemorySpace.HBM
    src_refs = [jax.new_ref(s, memory_space=hbm) for s in srcs]
    land_refs = [jax.empty_ref(jax.ShapeDtypeStruct(((N_DEV,) + s.shape) if gather else s.shape, s.dtype),
                               memory_space=hbm) for s in srcs]
    token_in = jax.new_ref(jnp.zeros((8, 128), F32) if after is None else after, memory_space=hbm)
    token_out = jax.empty_ref(jax.ShapeDtypeStruct((8, 128), F32), memory_space=hbm)

    @pl.kernel(mesh=plsc.ScalarSubcoreMesh(axis_name="seq", num_cores=1), name=name,
               scratch_types=(pltpu.SemaphoreType.DMA((n, N_PEERS)), pltpu.SemaphoreType.DMA((n, N_PEERS)),
                              pltpu.SemaphoreType.DMA((n + 1,))),
               compiler_params=pltpu.CompilerParams(collective_id=collective_id))
    def launch(send_sems, recv_sems, local_sems):
        x, y, c = _place()
        my_id = 4 * x + 2 * y + c
        barrier = pltpu.get_barrier_semaphore()
        own = [pltpu.make_async_copy(src_refs[t] if gather else src_refs[t].at[my_id], land_refs[t].at[my_id],
                                     local_sems.at[t]) for t in range(n)]
        if gather:
            sibling = (x, y, 1 - c)
            chips = [(1 - x, y), (x, 1 - y), (1 - x, 1 - y)]
            for peer in [sibling] + [(*chip, c) for chip in chips]:
                pl.semaphore_signal(barrier, inc=1, device_id=peer, device_id_type=MESH)
            pl.semaphore_wait(barrier, 4)

            def copy(t, k, block, to, src=None):
                dst = land_refs[t].at[4 * block[0] + 2 * block[1] + block[2]]
                return pltpu.make_async_remote_copy(
                    src_ref=dst if src is None else src, dst_ref=dst,
                    send_sem=send_sems.at[t, k], recv_sem=recv_sems.at[t, k], device_id=to, device_id_type=MESH)

            for cp in own:
                cp.start()
            sends = []
            for t in range(n):
                sends.append(copy(t, 0, (x, y, c), sibling, src=src_refs[t]))
                sends += [copy(t, 1 + j, (x, y, c), (*chip, c), src=src_refs[t]) for j, chip in enumerate(chips)]
            for cp in sends:
                cp.start()
            for t in range(n):
                for j, chip in enumerate(chips):
                    copy(t, 1 + j, (*chip, c), (x, y, c)).wait_recv()
                    passed_on = copy(t, 4 + j, (*chip, c), sibling)
                    passed_on.start()
                    sends.append(passed_on)
            for t in range(n):
                copy(t, 0, sibling, (x, y, c)).wait_recv()
                for j, chip in enumerate(chips):
                    copy(t, 4 + j, (*chip, 1 - c), (x, y, c)).wait_recv()
            for cp in sends:
                cp.wait_send()
            for cp in own:
                cp.wait()
        else:
            for k in range(1, N_DEV):
                peer = (_flip(x, k & 4), _flip(y, k & 2), _flip(c, k & 1))
                pl.semaphore_signal(barrier, inc=1, device_id=peer, device_id_type=MESH)
            pl.semaphore_wait(barrier, N_PEERS)
            for cp in own:
                cp.start()
            copies = [_peer_copy(k, src_refs[t], land_refs[t], send_sems.at[t], recv_sems.at[t], gather)
                      for t in range(n) for k in range(1, N_DEV)]
            for cp in copies:
                cp.start()
            for cp in own:
                cp.wait()
            for cp in copies:
                cp.wait()
        passed = pltpu.make_async_copy(token_in, token_out, local_sems.at[n])
        passed.start()
        passed.wait()

    launch()
    return [ref[...] for ref in land_refs], token_out[...]


def _all_reduce_small(buf, *, name):
    _, rows, lanes = buf.shape

    def body(in_ref, out_ref, parts, send_sems, recv_sems):
        x, y, c = _place()
        my_id = 4 * x + 2 * y + c
        peers = []
        for k in range(1, N_DEV):
            px, py, pc = _flip(x, k & 4), _flip(y, k & 2), _flip(c, k & 1)
            peers.append(((px, py, pc), 4 * px + 2 * py + pc))
        scatter = [pltpu.make_async_remote_copy(
            src_ref=in_ref.at[pid], dst_ref=parts.at[my_id],
            send_sem=send_sems.at[0, k], recv_sem=recv_sems.at[0, k],
            device_id=peer, device_id_type=MESH) for k, (peer, pid) in enumerate(peers)]
        for cp in scatter:
            cp.start()
        parts[my_id] = in_ref[my_id]
        for cp in scatter:
            cp.wait()
        total = parts[0]
        for d in range(1, N_DEV):
            total = total + parts[d]
        out_ref[my_id] = total
        gather = [pltpu.make_async_remote_copy(
            src_ref=out_ref.at[my_id], dst_ref=out_ref.at[my_id],
            send_sem=send_sems.at[1, k], recv_sem=recv_sems.at[1, k],
            device_id=peer, device_id_type=MESH) for k, (peer, pid) in enumerate(peers)]
        for cp in gather:
            cp.start()
        for k, (peer, pid) in enumerate(peers):
            pltpu.make_async_remote_copy(
                src_ref=out_ref.at[pid], dst_ref=out_ref.at[pid],
                send_sem=send_sems.at[1, k], recv_sem=recv_sems.at[1, k],
                device_id=peer, device_id_type=MESH).wait()

    vmem = pl.BlockSpec(memory_space=pltpu.VMEM)
    return pl.pallas_call(
        body, name=name, in_specs=[vmem], out_specs=vmem,
        out_shape=jax.ShapeDtypeStruct(buf.shape, F32),
        scratch_shapes=[pltpu.VMEM(buf.shape, F32),
                        pltpu.SemaphoreType.DMA((2, 7)), pltpu.SemaphoreType.DMA((2, 7))],
        compiler_params=pltpu.CompilerParams(vmem_limit_bytes=VMEM_LIMIT),
    )(buf)


TM = 512
TMM = 1024
TKW = 2048
MIX_TILE = 128
FFN_BLOCK_TILE = 512


def _block_diag(w):
    wg = w.reshape(N_GROUPS, HEADS_PER_GROUP, LRU_HEAD_DIM, LRU_HEAD_DIM)
    eye = jnp.eye(HEADS_PER_GROUP, dtype=w.dtype)
    bd = wg[:, :, :, None, :] * eye[None, :, None, :, None]
    return bd.reshape(N_GROUPS, LRU_GROUP, LRU_GROUP).astype(BF16)


def _head_blocks(bd):
    b5 = bd.reshape(N_GROUPS, HEADS_PER_GROUP, LRU_HEAD_DIM, HEADS_PER_GROUP, LRU_HEAD_DIM)
    blocks = [b5[:, h, :, h, :] for h in range(HEADS_PER_GROUP)]
    return jnp.stack(blocks, axis=1).reshape(LRU_HEADS, LRU_HEAD_DIM, LRU_HEAD_DIM)


def _w(lw, key, after):
    value = lw[key]
    return value(after) if callable(value) else value


def _layer_fwd(x, lw, tag):
    sv_rows = x.shape[0]
    z, h1 = _norm_in_proj(x, lw["g1"], _w(lw, "w_in_t", x), tm=min(TMM, sv_rows), tn=896, name=f"in_proj_{tag}")
    y_mix, hs = _mixer_fwd(z, _w(lw, "cw", z), lw["cb"], lw["wa_bd"], lw["wx_bd"], lw["ba"], lw["bx"], lw["lam"],
                           _w(lw, "scw", z), tile=MIX_TILE, name=f"mixer_fwd_{tag}")
    x2 = _mm_nn(y_mix, _w(lw, "w_out", y_mix), tm=min(TMM, sv_rows), tn=D_MODEL, tk=D_MIX, out_dtype=F32, name=f"out_proj_{tag}",
                residual=x)
    x3, h2, act, p, u = _ffn_block_fwd(x2, lw["g2"], _w(lw, "w_up_b", x2), _w(lw, "fcw", x2), _w(lw, "w_down", x2),
                                       tile=min(FFN_BLOCK_TILE, sv_rows), name=f"ffn_fwd_{tag}")
    saved = dict(x=x, h1=h1, z=z, y_mix=y_mix, hs=hs, x2=x2, h2=h2, p=p, u=u, act=act)
    return x3, saved


def _layer_bwd(dx3, dx3b, lw, sv, tag, put):
    sv_rows = dx3.shape[0]
    w_in_t, w_out, w_up_b, w_down = (_w(lw, k, dx3) for k in ("w_in_t", "w_out", "w_up_b", "w_down"))
    cw, scw, fcw = (_w(lw, k, dx3) for k in ("cw", "scw", "fcw"))
    g_down = _mm_tn(sv["act"], dx3b, tm=1024, tn=D_MODEL, tk=min(TKW, sv_rows), out_dtype=BF16, name=f"down_bwd_w_{tag}")
    dx2, dx2b, dg2, dp, dfcw = _ffn_block_bwd(dx3, dx3b, sv["p"], sv["u"], sv["x2"], lw["g2"], w_up_b, fcw, w_down,
                                              tile=min(FFN_BLOCK_TILE, sv_rows), name=f"ffn_bwd_{tag}",
                                              token=put("w_down", g_down))
    g_up = _mm_up_bwd_w(sv["h2"], dp, tm=D_MODEL, tk=min(TKW, sv_rows), name=f"up_bwd_w_{tag}")
    dy = _mm_nt(dx2b, w_out, tm=min(TMM, sv_rows), tn=768, tk=D_MODEL, out_dtype=F32, name=f"out_bwd_x_{tag}")
    dz, dcw, dvec, dwa, dwx, dscw = _mixer_bwd(
        sv["z"], sv["hs"], dy, cw, lw["cb"], lw["wa_bd"], lw["wx_bd"], lw["ba"], lw["bx"], lw["lam"],
        scw, tile=MIX_TILE, name=f"mixer_bwd_{tag}", token=put("w_up_b", g_up))
    g_out = _mm_tn(sv["y_mix"], dx2b, tm=768, tn=D_MODEL, tk=min(TKW, sv_rows), out_dtype=BF16, name=f"out_bwd_w_{tag}",
                   token=dz)
    g_in_t = _mm_tn(dz, sv["h1"], tm=896, tn=D_MODEL, tk=min(TKW, sv_rows), out_dtype=BF16, name=f"in_bwd_w_{tag}",
                    token=put("w_out", g_out))
    dx, dxb, dg1 = _in_bwd_norm(dz, w_in_t, sv["x"], lw["g1"], dx2, tm=TM, tk=896, name=f"in_bwd_x_{tag}",
                                token=put("w_in_t", g_in_t))
    small = dict(norm1_g=dg1[0], lru_conv_w=dcw[0:4], lru_conv_b=dvec[0], lru_wa=_head_blocks(dwa),
                 lru_ba=dvec[1], lru_wx=_head_blocks(dwx), lru_bx=dvec[2], lru_lambda=dvec[3],
                 sc_conv_w=dscw[0:3], norm2_g=dg2[0], ffn_conv_w=dfcw[:, 0:3, :])
    return dx, dxb, small


SMALL_ORDER = ("norm1_g", "lru_conv_w", "lru_conv_b", "lru_wa", "lru_ba", "lru_wx", "lru_bx", "lru_lambda",
               "sc_conv_w", "norm2_g", "ffn_conv_w")


def _local_step(x, tgt, layers, final_g, put):
    saved = []
    h = x
    for l in range(DEPTH):
        h, sv = _layer_fwd(h, layers[l], f"l{l}")
        saved.append(sv)
    loss_blk, dx, dxb, dgf = _loss_head(h, final_g, tgt, tm=TM, name="loss_head")
    smalls = [None] * DEPTH
    for l in reversed(range(DEPTH)):
        dx, dxb, smalls[l] = _layer_bwd(dx, dxb, layers[l], saved[l], f"l{l}", functools.partial(put, l))
    return loss_blk[0, 0], dx, smalls, dgf[0]


def kernel(x, norm1_g, w_in, lru_conv_w, lru_conv_b, lru_wa, lru_ba, lru_wx, lru_bx, lru_lambda, sc_conv_w, w_out, norm2_g, w_up, ffn_conv_w, w_down, final_g, loss_target, m_norm1_g, m_w_in, m_lru_conv_w, m_lru_conv_b, m_lru_wa, m_lru_ba, m_lru_wx, m_lru_bx, m_lru_lambda, m_sc_conv_w, m_w_out, m_norm2_g, m_w_up, m_ffn_conv_w, m_w_down, m_final_g, v_norm1_g, v_w_in, v_lru_conv_w, v_lru_conv_b, v_lru_wa, v_lru_ba, v_lru_wx, v_lru_bx, v_lru_lambda, v_sc_conv_w, v_w_out, v_norm2_g, v_w_up, v_ffn_conv_w, v_w_down, v_final_g):
    names = ["norm1_g", "w_in", "lru_conv_w", "lru_conv_b", "lru_wa", "lru_ba", "lru_wx", "lru_bx", "lru_lambda",
             "sc_conv_w", "w_out", "norm2_g", "w_up", "ffn_conv_w", "w_down", "final_g"]
    w = dict(zip(names, [norm1_g, w_in, lru_conv_w, lru_conv_b, lru_wa, lru_ba, lru_wx, lru_bx, lru_lambda,
                         sc_conv_w, w_out, norm2_g, w_up, ffn_conv_w, w_down, final_g]))
    m = dict(zip(names, [m_norm1_g, m_w_in, m_lru_conv_w, m_lru_conv_b, m_lru_wa, m_lru_ba, m_lru_wx, m_lru_bx,
                         m_lru_lambda, m_sc_conv_w, m_w_out, m_norm2_g, m_w_up, m_ffn_conv_w, m_w_down, m_final_g]))
    v = dict(zip(names, [v_norm1_g, v_w_in, v_lru_conv_w, v_lru_conv_b, v_lru_wa, v_lru_ba, v_lru_wx, v_lru_bx,
                         v_lru_lambda, v_sc_conv_w, v_w_out, v_norm2_g, v_w_up, v_ffn_conv_w, v_w_down, v_final_g]))
    my_id = 4 * lax.axis_index("x") + 2 * lax.axis_index("y") + lax.axis_index("c")

    taps = jnp.zeros((DEPTH, 16, 768), F32)
    taps = taps.at[:, 0:4, 0:128].set(lru_conv_w).at[:, 4:7, 0:64].set(sc_conv_w).at[:, 8:11, :].set(ffn_conv_w)
    shards = {}
    for l in range(DEPTH):
        shards[f"w_in_t{l}"] = jnp.swapaxes(w_in[l], 0, 1).astype(BF16)
        if l == 0:
            shards["taps"] = taps.reshape(DEPTH * 16, 768)
        shards[f"w_out{l}"] = w_out[l].astype(BF16)
        shards[f"w_up_b{l}"] = w_up[l].astype(BF16)
        shards[f"w_down{l}"] = w_down[l].astype(BF16)
    ids = iter(range(16))
    got = {}
    chain = [None]
    for group in (("w_in_t0", "taps"), ("w_out0",), ("w_up_b0",), ("w_down0",),
                  ("w_in_t1",), ("w_out1",), ("w_up_b1",), ("w_down1",)):
        lands, chain[0] = _sequencer_copies([shards[k] for k in group], gather=True, name=f"gather_{group[0]}",
                                            collective_id=next(ids), after=None)
        got.update(zip(group, lands))

    def fetch(key, after):
        return got[key]

    def tap_rows(l, lo, hi, width, after):
        tl = fetch("taps", after).reshape(N_DEV, DEPTH, 16, 768)[:, l, lo:hi, 0:width]
        return jnp.transpose(tl, (1, 0, 2)).reshape(hi - lo, N_DEV * width)

    layers = []
    for l in range(DEPTH):
        layers.append(dict(
            g1=norm1_g[l], g2=norm2_g[l], cb=lru_conv_b[l], ba=lru_ba[l], bx=lru_bx[l], lam=lru_lambda[l],
            wa_bd=_block_diag(lru_wa[l]), wx_bd=_block_diag(lru_wx[l]),
            cw=functools.partial(tap_rows, l, 0, 4, 128), scw=functools.partial(tap_rows, l, 4, 7, 64),
            fcw=lambda after, l=l: tap_rows(l, 8, 11, 768, after).reshape(3, 2, D_FF).transpose(1, 0, 2),
            w_in_t=lambda after, l=l: fetch(f"w_in_t{l}", after).reshape(D_IN, D_MODEL),
            w_out=lambda after, l=l: fetch(f"w_out{l}", after).reshape(D_MIX, D_MODEL),
            w_up_b=lambda after, l=l: fetch(f"w_up_b{l}", after),
            w_down=lambda after, l=l: fetch(f"w_down{l}", after).reshape(D_FF, D_MODEL)))

    scatter_handles = {}

    def put(l, key, grad):
        blocks = grad if grad.ndim == 3 else grad.reshape(N_DEV, grad.shape[0] // N_DEV, grad.shape[1])
        (scatter_handles[(l, key)],), chain[0] = _sequencer_copies(
            [blocks], gather=False, name=f"scatter_{key}{l}", collective_id=next(ids), after=chain[0])
        return blocks

    loss_local, dx, smalls, dgf = _local_step(x[0], loss_target[0], layers, final_g, put)

    parts = []
    for l in range(DEPTH):
        for key in ("w_in_t", "w_out", "w_up_b", "w_down"):
            parts.append(scatter_handles[(l, key)])

    flat = [smalls[l][k].reshape(-1) for l in range(DEPTH) for k in SMALL_ORDER] + [dgf.reshape(-1)]
    flat.append(jnp.broadcast_to(loss_local, (128,)))
    sizes = [f.shape[0] for f in flat]
    total = sum(sizes)
    rows = -(-total // (N_DEV * 128 * 8)) * 8
    flat.append(jnp.zeros((N_DEV * rows * 128 - total,), F32))
    small_sum = _all_reduce_small(jnp.concatenate(flat).reshape(N_DEV, rows, 128), name="reduce_small").reshape(-1)
    small_g, off = [], 0
    for sz in sizes:
        small_g.append(small_sum[off:off + sz])
        off += sz
    gs = {}
    for l in range(DEPTH):
        for i, k in enumerate(SMALL_ORDER):
            gs.setdefault(k, []).append(small_g[l * len(SMALL_ORDER) + i])
    g_final = small_g[-2]
    loss = small_g[-1][0]

    grads = {}
    per_layer = {k: [] for k in ("w_in", "w_out", "w_up", "w_down")}
    for l in range(DEPTH):
        p_in, p_out, p_up, p_down = parts[4 * l:4 * l + 4]
        per_layer["w_in"].append(jnp.swapaxes(_sum_parts(p_in, name=f"sum_w_in_l{l}"), 0, 1))
        per_layer["w_out"].append(_sum_parts(p_out, name=f"sum_w_out_l{l}"))
        per_layer["w_up"].append(_sum_parts(p_up, name=f"sum_w_up_l{l}"))
        per_layer["w_down"].append(_sum_parts(p_down, name=f"sum_w_down_l{l}"))
    for k, lst in per_layer.items():
        grads[k] = jnp.stack(lst)
    for k in ("norm1_g", "lru_conv_b", "lru_ba", "lru_bx", "lru_lambda", "norm2_g"):
        grads[k] = jnp.stack(gs[k]).reshape(DEPTH, -1)
    for k in ("lru_wa", "lru_wx"):
        grads[k] = jnp.stack(gs[k]).reshape(DEPTH, LRU_HEADS, LRU_HEAD_DIM, LRU_HEAD_DIM)
    grads["final_g"] = g_final
    cw_full = jnp.stack(gs["lru_conv_w"]).reshape(DEPTH, 4, N_DEV, 128)
    grads["lru_conv_w"] = lax.dynamic_index_in_dim(cw_full, my_id, axis=2, keepdims=False)
    scw_full = jnp.stack(gs["sc_conv_w"]).reshape(DEPTH, 3, N_DEV, 64)
    grads["sc_conv_w"] = lax.dynamic_index_in_dim(scw_full, my_id, axis=2, keepdims=False)
    fcw_full = jnp.stack(gs["ffn_conv_w"]).reshape(DEPTH, 2, 3, D_FF).transpose(0, 2, 1, 3).reshape(DEPTH, 3, N_DEV, 768)
    grads["ffn_conv_w"] = lax.dynamic_index_in_dim(fcw_full, my_id, axis=2, keepdims=False)

    deltas, new_m, new_v = {}, {}, {}
    for k in names:
        shape = w[k].shape
        cols = shape[-1]
        as2d = lambda a: a.reshape(-1, cols)
        d, nm, nv = _adamw(as2d(w[k]), as2d(grads[k]), as2d(m[k]), as2d(v[k]), name=f"adamw_{k}")
        deltas[k], new_m[k], new_v[k] = d.reshape(shape), nm.reshape(shape), nv.reshape(shape)

    return (loss, dx[None], *[grads[k] for k in names], *[deltas[k] for k in names],
            *[new_m[k] for k in names], *[new_v[k] for k in names])
```

```python
import functools
import math

import jax
import jax.numpy as jnp
from jax import lax
from jax.experimental import pallas as pl
from jax.experimental.pallas import tpu as pltpu
from jax.experimental.pallas import tpu_sc as plsc

F32 = jnp.float32
BF16 = jnp.bfloat16

N_DEV = 8
DEPTH = 2
D_MODEL = 1024
D_LRU = 1024
D_SC = 512
D_MIX = D_LRU + D_SC
D_IN = 2 * D_LRU + 3 * D_SC
D_FF = 3072
LRU_HEADS = 16
LRU_HEAD_DIM = 64
LRU_GROUP = 256
N_GROUPS = D_LRU // LRU_GROUP
HEADS_PER_GROUP = LRU_GROUP // LRU_HEAD_DIM
RG_C = 8.0
EPS = 1e-6
HALO = 8

ADAM_LR = 0.001
ADAM_B1 = 0.9
ADAM_B2 = 0.999
ADAM_EPS = 1e-08
ADAM_WD = 0.01
ADAM_STEP = 10

GELU_C = math.sqrt(2.0 / math.pi)
GELU_A = 0.044715

VMEM_LIMIT = 56 * 1024 * 1024
MESH = pl.DeviceIdType.MESH


def _params(*sem):
    return pltpu.CompilerParams(dimension_semantics=tuple(sem) if sem else None,
                                vmem_limit_bytes=VMEM_LIMIT)


def _gelu_parts(x):
    x2 = x * x
    t = jnp.tanh(GELU_C * (x + GELU_A * x * x2))
    half = 0.5 * (1.0 + t)
    g = x * half
    dg = half + 0.5 * x * (1.0 - t * t) * (GELU_C * (1.0 + 3.0 * GELU_A * x2))
    return g, dg


def _gelu(x):
    t = jnp.tanh(GELU_C * (x + GELU_A * x * x * x))
    return 0.5 * x * (1.0 + t)


def _sigmoid(x):
    return 0.5 * jnp.tanh(0.5 * x) + 0.5


def _softplus(x):
    e = jnp.exp(-jnp.abs(x))
    u = 1.0 + e
    log1p_e = jnp.where(u == 1.0, e, jnp.log(u) * (e / (u - 1.0)))
    return jnp.maximum(x, 0.0) + log1p_e


def _rms(x):
    ms = jnp.mean(x * x, axis=-1, keepdims=True)
    return lax.rsqrt(ms + EPS)


def _dot(a, b, dims):
    return lax.dot_general(a, b, (dims, ((), ())), preferred_element_type=F32)


NN = ((1,), (0,))
NT = ((1,), (1,))
TN = ((0,), (0,))


def _matmul(a, b, *, dims, grid, a_spec, b_spec, o_spec, out_shape, acc_shape, name,
            residual=None, r_spec=None, token=None):
    nk = grid[2]

    def body(*refs):
        a_ref, b_ref = refs[0], refs[1]
        r_ref = refs[2] if residual is not None else None
        o_ref = refs[2 + (residual is not None) + (token is not None)]
        prod = _dot(a_ref[...].astype(BF16), b_ref[...].astype(BF16), dims)

        def finish(total):
            if r_ref is not None:
                total = total + r_ref[...]
            o_ref[...] = total.astype(o_ref.dtype)

        if nk == 1:
            finish(prod)
            return
        acc_ref = refs[-1]
        k = pl.program_id(2)

        @pl.when(k == 0)
        def _():
            acc_ref[...] = prod

        @pl.when(jnp.logical_and(k > 0, k < nk - 1))
        def _():
            acc_ref[...] += prod

        @pl.when(k == nk - 1)
        def _():
            finish(acc_ref[...] + prod)

    in_specs = [a_spec, b_spec]
    args = [a, b]
    if residual is not None:
        in_specs.append(r_spec)
        args.append(residual)
    if token is not None:
        in_specs.append(pl.BlockSpec(memory_space=pl.ANY))
        args.append(token)
    return pl.pallas_call(
        body, name=name, grid=grid, in_specs=in_specs, out_specs=o_spec, out_shape=out_shape,
        scratch_shapes=[pltpu.VMEM(acc_shape, F32)] if nk > 1 else [],
        compiler_params=_params("parallel", "parallel", "arbitrary"),
    )(*args)


def _mm_nn(a, b, *, tm, tn, tk, out_dtype, name, residual=None, token=None):
    m, kd = a.shape
    n = b.shape[1]
    return _matmul(
        a, b, dims=NN, grid=(m // tm, n // tn, kd // tk),
        a_spec=pl.BlockSpec((tm, tk), lambda i, j, k: (i, k)),
        b_spec=pl.BlockSpec((tk, tn), lambda i, j, k: (k, j)),
        o_spec=pl.BlockSpec((tm, tn), lambda i, j, k: (i, j)),
        out_shape=jax.ShapeDtypeStruct((m, n), out_dtype), acc_shape=(tm, tn), name=name,
        residual=residual, r_spec=pl.BlockSpec((tm, tn), lambda i, j, k: (i, j)), token=token)


def _mm_nt(a, b, *, tm, tn, tk, out_dtype, name):
    m, kd = a.shape
    n = b.shape[0]
    return _matmul(
        a, b, dims=NT, grid=(m // tm, n // tn, kd // tk),
        a_spec=pl.BlockSpec((tm, tk), lambda i, j, k: (i, k)),
        b_spec=pl.BlockSpec((tn, tk), lambda i, j, k: (j, k)),
        o_spec=pl.BlockSpec((tm, tn), lambda i, j, k: (i, j)),
        out_shape=jax.ShapeDtypeStruct((m, n), out_dtype), acc_shape=(tm, tn), name=name)


def _mm_tn(a, b, *, tm, tn, tk, out_dtype, name, token=None):
    kd, m = a.shape
    n = b.shape[1]
    return _matmul(
        a, b, dims=TN, grid=(m // tm, n // tn, kd // tk),
        a_spec=pl.BlockSpec((tk, tm), lambda i, j, k: (k, i)),
        b_spec=pl.BlockSpec((tk, tn), lambda i, j, k: (k, j)),
        o_spec=pl.BlockSpec((tm, tn), lambda i, j, k: (i, j)),
        out_shape=jax.ShapeDtypeStruct((m, n), out_dtype), acc_shape=(tm, tn), name=name, token=token)


def _mm_up_bwd_w(h2, dp, *, tm, tk, name):
    s = h2.shape[0]
    nb = D_FF * 2 // N_DEV
    per_half = D_FF // nb
    return _matmul(
        h2, dp, dims=TN, grid=(D_MODEL // tm, N_DEV, s // tk),
        a_spec=pl.BlockSpec((tk, tm), lambda i, j, k: (k, i)),
        b_spec=pl.BlockSpec((None, tk, nb), lambda i, j, k: (j // per_half, k, j % per_half)),
        o_spec=pl.BlockSpec((None, tm, nb), lambda i, j, k: (j, i, 0)),
        out_shape=jax.ShapeDtypeStruct((N_DEV, D_MODEL, nb), BF16), acc_shape=(tm, nb), name=name)


def _behind(token):
    return jnp.zeros((8, 128), F32) if token is None else token


def _norm_in_proj(x, g, w_in_t, *, tm, tn, name):
    s, d = x.shape
    n = w_in_t.shape[0]

    def body(x_ref, g_ref, w_ref, z_ref, h_ref):
        @pl.when(pl.program_id(1) == 0)
        def _():
            xv = x_ref[...]
            h_ref[...] = (xv * _rms(xv) * g_ref[...]).astype(BF16)

        z_ref[...] = _dot(h_ref[...], w_ref[...], NT)

    return pl.pallas_call(
        body, name=name, grid=(s // tm, n // tn),
        in_specs=[pl.BlockSpec((tm, d), lambda i, j: (i, 0)), pl.BlockSpec((1, d), lambda i, j: (0, 0)),
                  pl.BlockSpec((tn, d), lambda i, j: (j, 0))],
        out_specs=[pl.BlockSpec((tm, tn), lambda i, j: (i, j)), pl.BlockSpec((tm, d), lambda i, j: (i, 0))],
        out_shape=[jax.ShapeDtypeStruct((s, n), F32), jax.ShapeDtypeStruct((s, d), BF16)],
        compiler_params=_params("parallel", "arbitrary"),
    )(x, g.reshape(1, d), w_in_t)


def _in_bwd_norm(dz, w_in_t, x, g, dres, *, tm, tk, name, token=None):
    s, kd = dz.shape
    d = w_in_t.shape[1]
    nk = kd // tk

    def body(dz_ref, w_ref, x_ref, g_ref, dres_ref, token_ref, dx_ref, dxb_ref, dg_ref, acc_ref):
        i = pl.program_id(0)
        k = pl.program_id(1)

        @pl.when(jnp.logical_and(i == 0, k == 0))
        def _():
            dg_ref[...] = jnp.zeros_like(dg_ref)

        acc_ref[...] = _dot(dz_ref[...], w_ref[...], NN) + jnp.where(k > 0, acc_ref[...], 0.0)

        @pl.when(k == nk - 1)
        def _():
            dh = acc_ref[...]
            xv = x_ref[...]
            rstd = _rms(xv)
            n = xv * rstd
            dn = dh * g_ref[...]
            dx = dres_ref[...] + rstd * (dn - n * jnp.mean(dn * n, axis=-1, keepdims=True))
            dx_ref[...] = dx
            dxb_ref[...] = dx.astype(BF16)
            dg_ref[0:1, :] += jnp.sum(dh * n, axis=0, keepdims=True)

    row = pl.BlockSpec((tm, d), lambda i, k: (i, 0))
    return pl.pallas_call(
        body, name=name, grid=(s // tm, nk),
        in_specs=[pl.BlockSpec((tm, tk), lambda i, k: (i, k)), pl.BlockSpec((tk, d), lambda i, k: (k, 0)),
                  row, pl.BlockSpec((1, d), lambda i, k: (0, 0)), row, pl.BlockSpec(memory_space=pl.ANY)],
        out_specs=[row, row, pl.BlockSpec((8, d), lambda i, k: (0, 0))],
        out_shape=[jax.ShapeDtypeStruct((s, d), F32), jax.ShapeDtypeStruct((s, d), BF16),
                   jax.ShapeDtypeStruct((8, d), F32)],
        scratch_shapes=[pltpu.VMEM((tm, d), F32)],
        compiler_params=_params("arbitrary", "arbitrary"),
    )(dz, w_in_t, x, g.reshape(1, d), dres, _behind(token))


def _loss_head(x, g, tgt, *, tm, name):
    s, d = x.shape

    def body(x_ref, g_ref, t_ref, loss_ref, dx_ref, dxb_ref, dg_ref):
        @pl.when(pl.program_id(0) == 0)
        def _():
            dg_ref[...] = jnp.zeros_like(dg_ref)
            loss_ref[...] = jnp.zeros_like(loss_ref)

        xv = x_ref[...]
        gv = g_ref[...]
        rstd = _rms(xv)
        n = xv * rstd
        e = n * gv - t_ref[...]
        part = 0.5 * jnp.sum(jnp.mean(e * e, axis=-1, keepdims=True), axis=0, keepdims=True)
        loss_ref[...] += jnp.broadcast_to(part, loss_ref.shape)
        dy = e * (1.0 / d)
        dn = dy * gv
        dx = rstd * (dn - n * jnp.mean(dn * n, axis=-1, keepdims=True))
        dx_ref[...] = dx
        dxb_ref[...] = dx.astype(BF16)
        dg_ref[0:1, :] += jnp.sum(dy * n, axis=0, keepdims=True)

    return pl.pallas_call(
        body, name=name, grid=(s // tm,),
        in_specs=[pl.BlockSpec((tm, d), lambda i: (i, 0)), pl.BlockSpec((1, d), lambda i: (0, 0)),
                  pl.BlockSpec((tm, d), lambda i: (i, 0))],
        out_specs=[pl.BlockSpec((8, 128), lambda i: (0, 0)), pl.BlockSpec((tm, d), lambda i: (i, 0)),
                   pl.BlockSpec((tm, d), lambda i: (i, 0)), pl.BlockSpec((8, d), lambda i: (0, 0))],
        out_shape=[jax.ShapeDtypeStruct((8, 128), F32), jax.ShapeDtypeStruct((s, d), F32),
                   jax.ShapeDtypeStruct((s, d), BF16), jax.ShapeDtypeStruct((8, d), F32)],
        compiler_params=_params("arbitrary"),
    )(x, g.reshape(1, d), tgt)


def _scan_rows(a_ref, b_ref, h_ref, carry, *, rows, reverse):
    width = a_ref.shape[1]
    n_chunks = rows // 8
    row = lax.broadcasted_iota(jnp.int32, (8, width), 0)

    def step(ci, carry):
        chunk = (n_chunks - 1 - ci) if reverse else ci
        off = pl.multiple_of(chunk * 8, 8)
        av = a_ref[pl.ds(off, 8), :]
        bv = b_ref[pl.ds(off, 8), :]
        for sh in (1, 2, 4):
            if reverse:
                a_sh = pltpu.roll(av, 8 - sh, 0)
                b_sh = pltpu.roll(bv, 8 - sh, 0)
                m = row < 8 - sh
            else:
                a_sh = pltpu.roll(av, sh, 0)
                b_sh = pltpu.roll(bv, sh, 0)
                m = row >= sh
            bv = jnp.where(m, av * b_sh + bv, bv)
            av = jnp.where(m, av * a_sh, av)
        h = av * carry + bv
        h_ref[pl.ds(off, 8), :] = h
        return h[0:1, :] if reverse else h[7:8, :]

    return lax.fori_loop(0, n_chunks, step, carry)


P_CB, P_BA, P_BX, P_DECAY, P_DSP, N_PAR = 4, 5, 6, 7, 8, 9


def _spread_mixer_params(par, par_sc, cw_ref, cb_ref, ba_ref, bx_ref, lam_ref, scw_ref):
    rows = par.shape[1:]
    for k in range(4):
        par[k] = jnp.broadcast_to(cw_ref[k:k + 1, :], rows)
    par[P_CB] = jnp.broadcast_to(cb_ref[...], rows)
    par[P_BA] = jnp.broadcast_to(ba_ref[...], rows)
    par[P_BX] = jnp.broadcast_to(bx_ref[...], rows)
    par[P_DECAY] = jnp.broadcast_to(-RG_C * _softplus(-lam_ref[...]), rows)
    par[P_DSP] = jnp.broadcast_to(-_sigmoid(-lam_ref[...]), rows)
    for k in range(3):
        par_sc[k] = jnp.broadcast_to(scw_ref[k:k + 1, :], par_sc.shape[1:])


def _gates_rows(pre_r, pre_i, par):
    r = _sigmoid(pre_r + par[P_BA])
    ig = _sigmoid(pre_i + par[P_BX])
    log_a = r * par[P_DECAY]
    a = jnp.exp(log_a)
    mult = jnp.sqrt(-jnp.tanh(log_a) * (a * a + 1.0))
    return r, ig, a, mult


def _mixer_fwd(z, cw, cb, wa_bd, wx_bd, ba, bx, lam, scw, *, tile, name):
    s = z.shape[0]
    n_tiles = s // tile

    rows_of = lambda r0: slice(r0, r0 + FFN_ROWS)
    col_gate, col_sb, col_sc, col_sx = (slice(D_LRU, 2 * D_LRU), slice(2 * D_LRU, 2 * D_LRU + D_SC),
                                        slice(2 * D_LRU + D_SC, 2 * D_LRU + 2 * D_SC), slice(2 * D_LRU + 2 * D_SC, D_IN))

    def body(z_ref, cw_ref, cb_ref, wa_ref, wx_ref, ba_ref, bx_ref, lam_ref, scw_ref,
             y_ref, hs_ref, par, par_sc, lx_s, lxb_s, a_s, b_s, car_lx, car_q, h_car):
        i = pl.program_id(0)

        @pl.when(i == 0)
        def _():
            car_lx[...] = jnp.zeros_like(car_lx)
            car_q[...] = jnp.zeros_like(car_q)
            h_car[...] = jnp.zeros_like(h_car)
            _spread_mixer_params(par, par_sc, cw_ref, cb_ref, ba_ref, bx_ref, lam_ref, scw_ref)

        before_lx, before_q = car_lx[...], car_q[...]
        for r0 in range(0, tile, FFN_ROWS):
            cur = z_ref[rows_of(r0), 0:D_LRU]
            lx = par[P_CB] + par[3] * cur
            for k in range(3):
                lx = lx + par[k] * _rows_from(before_lx, cur, FFN_ROWS - 3 + k)
            lx_s[rows_of(r0), :] = lx
            lxb_s[rows_of(r0), :] = lx.astype(BF16)
            before_lx = cur
            q = z_ref[rows_of(r0), col_sc] * z_ref[rows_of(r0), col_sx]
            cq = par_sc[2] * q
            for k in range(2):
                cq = cq + par_sc[k] * _rows_from(before_q, q, FFN_ROWS - 2 + k)
            y_ref[rows_of(r0), D_LRU:D_MIX] = (z_ref[rows_of(r0), col_sb] * cq).astype(BF16)
            before_q = q
        car_lx[...] = before_lx
        car_q[...] = before_q

        for g in range(N_GROUPS):
            cols = slice(g * LRU_GROUP, (g + 1) * LRU_GROUP)
            a_s[:, cols] = _dot(lxb_s[:, cols], wa_ref[g], NN)
            b_s[:, cols] = _dot(lxb_s[:, cols], wx_ref[g], NN)

        for r0 in range(0, tile, FFN_ROWS):
            r, ig, a, mult = _gates_rows(a_s[rows_of(r0), :], b_s[rows_of(r0), :], par)
            a_s[rows_of(r0), :] = a
            b_s[rows_of(r0), :] = mult * (ig * lx_s[rows_of(r0), :])
        h_car[0:1, :] = _scan_rows(a_s, b_s, hs_ref, h_car[0:1, :], rows=tile, reverse=False)

        for r0 in range(0, tile, FFN_ROWS):
            y_ref[rows_of(r0), 0:D_LRU] = (hs_ref[rows_of(r0), :] * _gelu(z_ref[rows_of(r0), col_gate])).astype(BF16)

    full = lambda shape: pl.BlockSpec(shape, lambda i: (0,) * len(shape))
    return pl.pallas_call(
        body, name=name, grid=(n_tiles,),
        in_specs=[pl.BlockSpec((tile, D_IN), lambda i: (i, 0)),
                  full((4, D_LRU)), full((1, D_LRU)),
                  full((N_GROUPS, LRU_GROUP, LRU_GROUP)), full((N_GROUPS, LRU_GROUP, LRU_GROUP)),
                  full((1, D_LRU)), full((1, D_LRU)), full((1, D_LRU)), full((3, D_SC))],
        out_specs=[pl.BlockSpec((tile, D_MIX), lambda i: (i, 0)), pl.BlockSpec((tile, D_LRU), lambda i: (i, 0))],
        out_shape=[jax.ShapeDtypeStruct((s, D_MIX), BF16), jax.ShapeDtypeStruct((s, D_LRU), F32)],
        scratch_shapes=[pltpu.VMEM((N_PAR, FFN_ROWS, D_LRU), F32), pltpu.VMEM((3, FFN_ROWS, D_SC), F32),
                        pltpu.VMEM((tile, D_LRU), F32), pltpu.VMEM((tile, D_LRU), BF16),
                        pltpu.VMEM((tile, D_LRU), F32), pltpu.VMEM((tile, D_LRU), F32),
                        pltpu.VMEM((FFN_ROWS, D_LRU), F32), pltpu.VMEM((FFN_ROWS, D_SC), F32),
                        pltpu.VMEM((8, D_LRU), F32)],
        compiler_params=_params("arbitrary"),
    )(z, cw, cb.reshape(1, -1), wa_bd, wx_bd, ba.reshape(1, -1), bx.reshape(1, -1), lam.reshape(1, -1), scw)


def _fold8(x):
    return sum(x[q:q + 8] for q in range(0, x.shape[0], 8))


def _mixer_bwd_rows(z, hs, dy, cw, cb, wa_bd, wx_bd, ba, bx, lam, scw, *, tile, name, token=None):
    s = z.shape[0]
    n_tiles = s // tile
    per8 = tile // 8
    rows_of = lambda r0: slice(r0, r0 + FFN_ROWS)
    col_gate, col_sb, col_sc, col_sx = (slice(D_LRU, 2 * D_LRU), slice(2 * D_LRU, 2 * D_LRU + D_SC),
                                        slice(2 * D_LRU + D_SC, 2 * D_LRU + 2 * D_SC), slice(2 * D_LRU + 2 * D_SC, D_IN))
    up = range(0, tile, FFN_ROWS)
    down = range(tile - FFN_ROWS, -1, -FFN_ROWS)
    A_CB, A_BA, A_BX, A_SP, A_CW = 0, 1, 2, 3, 4

    def body(z_ref, zp_ref, hs_ref, hsp_ref, dy_ref, cw_ref, cb_ref, wa_ref, wx_ref, ba_ref, bx_ref, lam_ref, scw_ref,
             token_ref, dz_ref, dcw_ref, dvec_ref, dwa_ref, dwx_ref, dscw_ref,
             par, par_sc, lx_s, lxb_s, cq_s, pr_s, pi_s, r_s, ig_s, a_s, mult_s, inv_s, ash_s, b_s, lam_s,
             dlx_s, dpr_b, dpi_b, back_s, car_a, car_dlx, car_dcq, l_car, acc, acc_sc):
        i = pl.program_id(0)

        @pl.when(i == 0)
        def _():
            for ref in (dwa_ref, dwx_ref, l_car, car_a, car_dlx, car_dcq, acc, acc_sc):
                ref[...] = jnp.zeros_like(ref)
            _spread_mixer_params(par, par_sc, cw_ref, cb_ref, ba_ref, bx_ref, lam_ref, scw_ref)

        keep = jnp.where(i == n_tiles - 1, 0.0, 1.0)
        zeros8 = lambda n: jnp.zeros((8, n), F32)

        before_lx = jnp.concatenate([zeros8(D_LRU), zp_ref[:, 0:D_LRU] * keep], axis=0)
        before_q = jnp.concatenate([zeros8(D_SC), zp_ref[:, col_sc] * zp_ref[:, col_sx] * keep], axis=0)
        for r0 in up:
            cur = z_ref[rows_of(r0), 0:D_LRU]
            lx = par[P_CB] + par[3] * cur
            for k in range(3):
                lx = lx + par[k] * _rows_from(before_lx, cur, FFN_ROWS - 3 + k)
            lx_s[rows_of(r0), :] = lx
            lxb_s[rows_of(r0), :] = lx.astype(BF16)
            before_lx = cur
            q = z_ref[rows_of(r0), col_sc] * z_ref[rows_of(r0), col_sx]
            cq = par_sc[2] * q
            for k in range(2):
                cq = cq + par_sc[k] * _rows_from(before_q, q, FFN_ROWS - 2 + k)
            cq_s[rows_of(r0), :] = cq
            before_q = q

        for g in range(N_GROUPS):
            cols = slice(g * LRU_GROUP, (g + 1) * LRU_GROUP)
            pr_s[:, cols] = _dot(lxb_s[:, cols], wa_ref[g], NN)
            pi_s[:, cols] = _dot(lxb_s[:, cols], wx_ref[g], NN)

        after_a = car_a[...]
        for r0 in down:
            r, ig, a, mult = _gates_rows(pr_s[rows_of(r0), :], pi_s[rows_of(r0), :], par)
            r_s[rows_of(r0), :] = r
            ig_s[rows_of(r0), :] = ig
            a_s[rows_of(r0), :] = a
            mult_s[rows_of(r0), :] = mult
            inv_s[rows_of(r0), :] = 1.0 / mult
            ash_s[rows_of(r0), :] = _rows_from(a, after_a, 1)
            after_a = a
            ge, dge = _gelu_parts(z_ref[rows_of(r0), col_gate])
            dy_lru = dy_ref[rows_of(r0), 0:D_LRU]
            dz_ref[rows_of(r0), col_gate] = (dy_lru * hs_ref[rows_of(r0), :] * dge).astype(BF16)
            b_s[rows_of(r0), :] = dy_lru * ge
        car_a[...] = after_a
        l_car[0:1, :] = _scan_rows(ash_s, b_s, lam_s, l_car[0:1, :], rows=tile, reverse=True)

        before_h = jnp.concatenate([zeros8(D_LRU), hsp_ref[...] * keep], axis=0)
        for r0 in up:
            lv = lam_s[rows_of(r0), :]
            h_here = hs_ref[rows_of(r0), :]
            lx, r, ig, a = lx_s[rows_of(r0), :], r_s[rows_of(r0), :], ig_s[rows_of(r0), :], a_s[rows_of(r0), :]
            mult = mult_s[rows_of(r0), :]
            da = lv * _rows_from(before_h, h_here, FFN_ROWS - 1)
            before_h = h_here
            d_mult = lv * ig * lx
            d_i = lv * mult * lx
            dlx_s[rows_of(r0), :] = lv * mult * ig
            dlog_a = da * a - d_mult * (a * a) * inv_s[rows_of(r0), :]
            dpre_r = dlog_a * par[P_DECAY] * r * (1.0 - r)
            dpre_i = d_i * ig * (1.0 - ig)
            acc[A_BA] += _fold8(dpre_r)
            acc[A_BX] += _fold8(dpre_i)
            acc[A_SP] += _fold8(dlog_a * r)
            dpr_b[rows_of(r0), :] = dpre_r.astype(BF16)
            dpi_b[rows_of(r0), :] = dpre_i.astype(BF16)

        for g in range(N_GROUPS):
            cols = slice(g * LRU_GROUP, (g + 1) * LRU_GROUP)
            dwa_ref[g] += _dot(lxb_s[:, cols], dpr_b[:, cols], TN)
            dwx_ref[g] += _dot(lxb_s[:, cols], dpi_b[:, cols], TN)
            back_s[:, cols] = _dot(dpr_b[:, cols], wa_ref[g], NT) + _dot(dpi_b[:, cols], wx_ref[g], NT)

        after_dlx, after_dcq = car_dlx[...], car_dcq[...]
        for r0 in down:
            dlx = dlx_s[rows_of(r0), :] + back_s[rows_of(r0), :]
            lxp = z_ref[rows_of(r0), 0:D_LRU]
            acc[A_CB] += _fold8(dlx)
            acc[A_CW + 3] += _fold8(dlx * lxp)
            dlxp = par[3] * dlx
            for sh in range(1, 4):
                below = _rows_from(dlx, after_dlx, sh)
                dlxp = dlxp + par[3 - sh] * below
                acc[A_CW + 3 - sh] += _fold8(below * lxp)
            dz_ref[rows_of(r0), 0:D_LRU] = dlxp.astype(BF16)
            after_dlx = dlx

            dy_sc = dy_ref[rows_of(r0), D_LRU:D_MIX]
            sb, sc, sx = z_ref[rows_of(r0), col_sb], z_ref[rows_of(r0), col_sc], z_ref[rows_of(r0), col_sx]
            dz_ref[rows_of(r0), col_sb] = (dy_sc * cq_s[rows_of(r0), :]).astype(BF16)
            dcq = dy_sc * sb
            q = sc * sx
            acc_sc[2] += _fold8(dcq * q)
            dq = par_sc[2] * dcq
            for sh in range(1, 3):
                below = _rows_from(dcq, after_dcq, sh)
                dq = dq + par_sc[2 - sh] * below
                acc_sc[2 - sh] += _fold8(below * q)
            dz_ref[rows_of(r0), col_sc] = (dq * sx).astype(BF16)
            dz_ref[rows_of(r0), col_sx] = (dq * sc).astype(BF16)
            after_dcq = dcq
        car_dlx[...] = after_dlx
        car_dcq[...] = after_dcq

        @pl.when(i == n_tiles - 1)
        def _():
            total = lambda x: jnp.sum(x, axis=0, keepdims=True)
            dcw_ref[...] = jnp.zeros_like(dcw_ref)
            dvec_ref[...] = jnp.zeros_like(dvec_ref)
            dscw_ref[...] = jnp.zeros_like(dscw_ref)
            for k in range(4):
                dcw_ref[k:k + 1, :] = total(acc[A_CW + k])
            for k in range(3):
                dvec_ref[k:k + 1, :] = total(acc[k])
                dscw_ref[k:k + 1, :] = total(acc_sc[k])
            dvec_ref[3:4, :] = total(acc[A_SP]) * (-RG_C) * par[P_DSP][0:1, :]

    rev = lambda i: n_tiles - 1 - i
    prev8 = lambda i: jnp.maximum(rev(i) * per8 - 1, 0)
    full = lambda shape: pl.BlockSpec(shape, lambda i: (0,) * len(shape))
    wide = lambda rows, dt=F32: pltpu.VMEM((rows, D_LRU), dt)
    return pl.pallas_call(
        body, name=name, grid=(n_tiles,),
        in_specs=[pl.BlockSpec((tile, D_IN), lambda i: (rev(i), 0)),
                  pl.BlockSpec((HALO, D_IN), lambda i: (prev8(i), 0)),
                  pl.BlockSpec((tile, D_LRU), lambda i: (rev(i), 0)),
                  pl.BlockSpec((HALO, D_LRU), lambda i: (prev8(i), 0)),
                  pl.BlockSpec((tile, D_MIX), lambda i: (rev(i), 0)),
                  full((4, D_LRU)), full((1, D_LRU)),
                  full((N_GROUPS, LRU_GROUP, LRU_GROUP)), full((N_GROUPS, LRU_GROUP, LRU_GROUP)),
                  full((1, D_LRU)), full((1, D_LRU)), full((1, D_LRU)), full((3, D_SC)),
                  pl.BlockSpec(memory_space=pl.ANY)],
        out_specs=[pl.BlockSpec((tile, D_IN), lambda i: (rev(i), 0)),
                   full((8, D_LRU)), full((8, D_LRU)),
                   full((N_GROUPS, LRU_GROUP, LRU_GROUP)), full((N_GROUPS, LRU_GROUP, LRU_GROUP)),
                   full((8, D_SC))],
        out_shape=[jax.ShapeDtypeStruct((s, D_IN), BF16),
                   jax.ShapeDtypeStruct((8, D_LRU), F32), jax.ShapeDtypeStruct((8, D_LRU), F32),
                   jax.ShapeDtypeStruct((N_GROUPS, LRU_GROUP, LRU_GROUP), F32),
                   jax.ShapeDtypeStruct((N_GROUPS, LRU_GROUP, LRU_GROUP), F32),
                   jax.ShapeDtypeStruct((8, D_SC), F32)],
        scratch_shapes=[pltpu.VMEM((N_PAR, FFN_ROWS, D_LRU), F32), pltpu.VMEM((3, FFN_ROWS, D_SC), F32),
                        wide(tile), wide(tile, BF16), pltpu.VMEM((tile, D_SC), F32),
                        wide(tile), wide(tile), wide(tile), wide(tile), wide(tile), wide(tile), wide(tile),
                        wide(tile), wide(tile), wide(tile),
                        wide(tile), wide(tile, BF16), wide(tile, BF16), wide(tile),
                        wide(FFN_ROWS), wide(FFN_ROWS), pltpu.VMEM((FFN_ROWS, D_SC), F32), wide(8),
                        pltpu.VMEM((8, 8, D_LRU), F32), pltpu.VMEM((3, 8, D_SC), F32)],
        compiler_params=_params("arbitrary"),
    )(z, z, hs, hs, dy, cw, cb.reshape(1, -1), wa_bd, wx_bd, ba.reshape(1, -1), bx.reshape(1, -1),
      lam.reshape(1, -1), scw, _behind(token))


FFN_ROWS = 16
FFN_GROUPS = 2


def _spread_taps(fw_ref, taps):
    for half in range(2):
        for k in range(3):
            taps[half, k] = jnp.broadcast_to(fw_ref[half, k:k + 1, :], taps.shape[2:])


def _rows_from(first, second, start):
    stack = jnp.concatenate([first, second], axis=0)
    return pltpu.roll(stack, 2 * FFN_ROWS - start, 0)[0:FFN_ROWS]


def _conv3_rows(taps, ext_ref, half, row):
    before = ext_ref[half, row - FFN_ROWS:row, :]
    here = ext_ref[half, row:row + FFN_ROWS, :]
    acc = taps[half, 2] * here
    for k in range(2):
        acc = acc + taps[half, k] * _rows_from(before, here, FFN_ROWS - 2 + k)
    return acc


HALO_B = 16


def _ffn_block_fwd(x2, g2, w_up_b, fcw, w_down, *, tile, name):
    s = x2.shape[0]
    nb = w_up_b.shape[2]
    blocks = D_FF // nb
    per16 = tile // HALO_B

    def body(x2_ref, x2p_ref, g_ref, wg_ref, wu_ref, fw_ref, wd_ref, x3_ref, h_ref, act_ref, p_ref, u_ref,
             ext_p, acc_ref, taps, lhs):
        i = pl.program_id(0)
        j = pl.program_id(1)
        keep = jnp.where(i == 0, 0.0, 1.0)
        _spread_taps(fw_ref, taps)
        @pl.when(j == 0)
        def _():
            for rows_ref, at in ((x2p_ref, 0), (x2_ref, HALO_B)):
                xv = rows_ref[...]
                lhs[at:at + xv.shape[0], :] = (xv * _rms(xv) * g_ref[...]).astype(BF16)
            h_ref[...] = lhs[HALO_B:HALO_B + tile, :]

        grp = tile // FFN_GROUPS
        for g in range(FFN_GROUPS):
            new = slice(g * grp + (HALO_B if g else 0), (g + 1) * grp + HALO_B)
            for half, w_ref in ((0, wg_ref), (1, wu_ref)):
                pe = _dot(lhs[new, :], w_ref[...], NN)
                if g == 0:
                    ext_p[half, 0:HALO_B, :] = pe[0:HALO_B] * keep
                    ext_p[half, HALO_B:grp + HALO_B, :] = pe[HALO_B:]
                    p_ref[half, 0:grp, :] = pe[HALO_B:].astype(BF16)
                else:
                    ext_p[half, new, :] = pe
                    p_ref[half, g * grp:(g + 1) * grp, :] = pe.astype(BF16)
        for g in range(FFN_GROUPS):
            rows = slice(g * grp, (g + 1) * grp)
            acts = []
            for r0 in range(g * grp, (g + 1) * grp, FFN_ROWS):
                u = [_conv3_rows(taps, ext_p, half, HALO_B + r0) for half in range(2)]
                for half in range(2):
                    u_ref[half, r0:r0 + FFN_ROWS, :] = u[half].astype(BF16)
                acts.append((_gelu(u[0]) * u[1]).astype(BF16))
                act_ref[r0:r0 + FFN_ROWS, :] = acts[-1]
            contrib = _dot(jnp.concatenate(acts, axis=0), wd_ref[...], NN)
            acc_ref[rows, :] = contrib + jnp.where(j > 0, acc_ref[rows, :], 0.0)

        @pl.when(j == blocks - 1)
        def _():
            x3_ref[...] = x2_ref[...] + acc_ref[...]

    return pl.pallas_call(
        body, name=name, grid=(s // tile, blocks),
        in_specs=[pl.BlockSpec((tile, D_MODEL), lambda i, j: (i, 0)),
                  pl.BlockSpec((HALO_B, D_MODEL), lambda i, j: (jnp.maximum(i * per16 - 1, 0), 0)),
                  pl.BlockSpec((1, D_MODEL), lambda i, j: (0, 0)),
                  pl.BlockSpec((None, D_MODEL, nb), lambda i, j: (j, 0, 0)),
                  pl.BlockSpec((None, D_MODEL, nb), lambda i, j: (j + blocks, 0, 0)),
                  pl.BlockSpec((2, 3, nb), lambda i, j: (0, 0, j)),
                  pl.BlockSpec((nb, D_MODEL), lambda i, j: (j, 0))],
        out_specs=[pl.BlockSpec((tile, D_MODEL), lambda i, j: (i, 0)),
                   pl.BlockSpec((tile, D_MODEL), lambda i, j: (i, 0)),
                   pl.BlockSpec((tile, nb), lambda i, j: (i, j)),
                   pl.BlockSpec((2, tile, nb), lambda i, j: (0, i, j)),
                   pl.BlockSpec((2, tile, nb), lambda i, j: (0, i, j))],
        out_shape=[jax.ShapeDtypeStruct((s, D_MODEL), F32), jax.ShapeDtypeStruct((s, D_MODEL), BF16),
                   jax.ShapeDtypeStruct((s, D_FF), BF16),
                   jax.ShapeDtypeStruct((2, s, D_FF), BF16), jax.ShapeDtypeStruct((2, s, D_FF), BF16)],
        scratch_shapes=[pltpu.VMEM((2, tile + HALO_B, nb), F32), pltpu.VMEM((tile, D_MODEL), F32),
                        pltpu.VMEM((2, 3, FFN_ROWS, nb), F32), pltpu.VMEM((tile + HALO_B, D_MODEL), BF16)],
        compiler_params=_params("parallel", "arbitrary"),
    )(x2, x2, g2.reshape(1, -1), w_up_b, w_up_b, fcw, w_down)


def _ffn_block_bwd(dx3, dx3b, p, u, x2, g2, w_up_b, fcw, w_down, *, tile, name, token=None):
    s = x2.shape[0]
    nb = w_up_b.shape[2]
    blocks = D_FF // nb
    n_tiles = s // tile
    per16 = tile // HALO_B
    last16 = s // HALO_B - 1

    def body(dxb_ref, dxbn_ref, wd_ref, p_ref, u_ref, un_ref, fw_ref, wg_ref, wu_ref, x2_ref, g_ref, dx3_ref,
             token_ref, dx2_ref, dx2b_ref, dg_ref, dp_ref, dw_ref, da_s, acc_w, acc_dh, taps):
        i = pl.program_id(0)
        j = pl.program_id(1)

        @pl.when(jnp.logical_and(i == 0, j == 0))
        def _():
            acc_w[...] = jnp.zeros_like(acc_w)
            dg_ref[...] = jnp.zeros_like(dg_ref)

        keep_next = jnp.where(i == n_tiles - 1, 0.0, 1.0)
        _spread_taps(fw_ref, taps)
        lhs = jnp.concatenate([dxb_ref[...], dxbn_ref[...]], axis=0)
        grp = tile // FFN_GROUPS
        for g in reversed(range(FFN_GROUPS)):
            new = slice(g * grp, (g + 1) * grp + (HALO_B if g == FFN_GROUPS - 1 else 0))
            da_s[new, :] = _dot(lhs[new], wd_ref[...], NT)

        def du_rows(da, u_gate, u_up):
            ge, dge = _gelu_parts(u_gate)
            return da * u_up * dge, da * ge

        after = du_rows(da_s[tile:tile + HALO_B, :] * keep_next, un_ref[0].astype(F32), un_ref[1].astype(F32))
        for g in reversed(range(FFN_GROUPS)):
            rows = slice(g * grp, (g + 1) * grp)
            dps = ([], [])
            for r0 in range((g + 1) * grp - FFN_ROWS, g * grp - 1, -FFN_ROWS):
                du = du_rows(da_s[r0:r0 + FFN_ROWS, :], u_ref[0, r0:r0 + FFN_ROWS, :].astype(F32),
                             u_ref[1, r0:r0 + FFN_ROWS, :].astype(F32))
                for half in range(2):
                    below = [du[half], _rows_from(du[half], after[half], 1), _rows_from(du[half], after[half], 2)]
                    acc = taps[half, 2] * below[0]
                    for k in range(2):
                        acc = acc + taps[half, k] * below[2 - k]
                    dps[half].insert(0, acc.astype(BF16))
                    dp_ref[half, r0:r0 + FFN_ROWS, :] = dps[half][0]
                    p_rows = p_ref[half, r0:r0 + FFN_ROWS, :].astype(F32)
                    for k in range(3):
                        prod = below[2 - k] * p_rows
                        acc_w[j, half, k] += sum(prod[q:q + 8] for q in range(0, FFN_ROWS, 8))
                after = du
            contrib = (_dot(jnp.concatenate(dps[0], axis=0), wg_ref[...], NT)
                       + _dot(jnp.concatenate(dps[1], axis=0), wu_ref[...], NT))
            acc_dh[rows, :] = contrib + jnp.where(j > 0, acc_dh[rows, :], 0.0)

        @pl.when(j == blocks - 1)
        def _():
            dh = acc_dh[...]
            xv = x2_ref[...]
            rstd = _rms(xv)
            n = xv * rstd
            dn = dh * g_ref[...]
            dx = dx3_ref[...] + rstd * (dn - n * jnp.mean(dn * n, axis=-1, keepdims=True))
            dx2_ref[...] = dx
            dx2b_ref[...] = dx.astype(BF16)
            dg_ref[0:1, :] += jnp.sum(dh * n, axis=0, keepdims=True)

        @pl.when(jnp.logical_and(i == n_tiles - 1, j == blocks - 1))
        def _():
            dw_ref[...] = jnp.zeros_like(dw_ref)
            for jj in range(blocks):
                for half in range(2):
                    for k in range(3):
                        dw_ref[half, k:k + 1, jj * nb:(jj + 1) * nb] = jnp.sum(acc_w[jj, half, k], axis=0, keepdims=True)

    next16 = lambda i: jnp.minimum((i + 1) * per16, last16)
    return pl.pallas_call(
        body, name=name, grid=(n_tiles, blocks),
        in_specs=[pl.BlockSpec((tile, D_MODEL), lambda i, j: (i, 0)),
                  pl.BlockSpec((HALO_B, D_MODEL), lambda i, j: (next16(i), 0)),
                  pl.BlockSpec((nb, D_MODEL), lambda i, j: (j, 0)),
                  pl.BlockSpec((2, tile, nb), lambda i, j: (0, i, j)),
                  pl.BlockSpec((2, tile, nb), lambda i, j: (0, i, j)),
                  pl.BlockSpec((2, HALO_B, nb), lambda i, j: (0, next16(i), j)),
                  pl.BlockSpec((2, 3, nb), lambda i, j: (0, 0, j)),
                  pl.BlockSpec((None, D_MODEL, nb), lambda i, j: (j, 0, 0)),
                  pl.BlockSpec((None, D_MODEL, nb), lambda i, j: (j + blocks, 0, 0)),
                  pl.BlockSpec((tile, D_MODEL), lambda i, j: (i, 0)),
                  pl.BlockSpec((1, D_MODEL), lambda i, j: (0, 0)),
                  pl.BlockSpec((tile, D_MODEL), lambda i, j: (i, 0)),
                  pl.BlockSpec(memory_space=pl.ANY)],
        out_specs=[pl.BlockSpec((tile, D_MODEL), lambda i, j: (i, 0)),
                   pl.BlockSpec((tile, D_MODEL), lambda i, j: (i, 0)),
                   pl.BlockSpec((8, D_MODEL), lambda i, j: (0, 0)),
                   pl.BlockSpec((2, tile, nb), lambda i, j: (0, i, j)),
                   pl.BlockSpec((2, 8, D_FF), lambda i, j: (0, 0, 0))],
        out_shape=[jax.ShapeDtypeStruct((s, D_MODEL), F32), jax.ShapeDtypeStruct((s, D_MODEL), BF16),
                   jax.ShapeDtypeStruct((8, D_MODEL), F32), jax.ShapeDtypeStruct((2, s, D_FF), BF16),
                   jax.ShapeDtypeStruct((2, 8, D_FF), F32)],
        scratch_shapes=[pltpu.VMEM((tile + HALO_B, nb), F32), pltpu.VMEM((blocks, 2, 3, 8, nb), F32),
                        pltpu.VMEM((tile, D_MODEL), F32), pltpu.VMEM((2, 3, FFN_ROWS, nb), F32)],
        compiler_params=_params("arbitrary", "arbitrary"),
    )(dx3b, dx3b, w_down, p, u, u, fcw, w_up_b, w_up_b, x2, g2.reshape(1, -1), dx3, _behind(token))


def _adamw_math(w, g, m, v):
    m = ADAM_B1 * m + (1.0 - ADAM_B1) * g
    v = ADAM_B2 * v + (1.0 - ADAM_B2) * (g * g)
    m_hat = m / (1.0 - ADAM_B1 ** ADAM_STEP)
    v_hat = v / (1.0 - ADAM_B2 ** ADAM_STEP)
    delta = -ADAM_LR * (m_hat / (jnp.sqrt(v_hat) + ADAM_EPS) + ADAM_WD * w)
    return delta, m, v


def _adamw(w, g, m, v, *, name):
    rows, cols = w.shape
    tr = rows
    for cand in (512, 256, 128, 64, 32, 16, 8):
        if rows % cand == 0 and rows > cand:
            tr = cand
            break

    def body(w_ref, g_ref, m_ref, v_ref, d_ref, nm_ref, nv_ref):
        d, nm, nv = _adamw_math(w_ref[...], g_ref[...], m_ref[...], v_ref[...])
        d_ref[...] = d
        nm_ref[...] = nm
        nv_ref[...] = nv

    spec = pl.BlockSpec((tr, cols), lambda i: (i, 0))
    return pl.pallas_call(
        body, name=name, grid=(rows // tr,), in_specs=[spec] * 4, out_specs=[spec] * 3,
        out_shape=[jax.ShapeDtypeStruct((rows, cols), F32)] * 3,
        compiler_params=_params("parallel"),
    )(w, g, m, v)


def _sum_parts(parts, *, name):
    _, rows, cols = parts.shape
    tr = rows
    for cand in (256, 128, 64, 32, 16):
        if rows % cand == 0 and rows > cand:
            tr = cand
            break

    def body(p_ref, o_ref):
        acc = p_ref[0].astype(F32)
        for d in range(1, N_DEV):
            acc = acc + p_ref[d].astype(F32)
        o_ref[...] = acc

    return pl.pallas_call(
        body, name=name, grid=(rows // tr,),
        in_specs=[pl.BlockSpec((N_DEV, tr, cols), lambda i: (0, i, 0))],
        out_specs=pl.BlockSpec((tr, cols), lambda i: (i, 0)),
        out_shape=jax.ShapeDtypeStruct((rows, cols), F32),
        compiler_params=_params("parallel"),
    )(parts)


def _place():
    return lax.axis_index("x"), lax.axis_index("y"), lax.axis_index("c")


def _flip(v, bit):
    return 1 - v if bit else v


N_PEERS = N_DEV - 1


def _peer_copy(k, src_ref, land_ref, send_sem, recv_sem, gather):
    x, y, c = _place()
    my_id = 4 * x + 2 * y + c
    px, py, pc = _flip(x, k & 4), _flip(y, k & 2), _flip(c, k & 1)
    peer_id = 4 * px + 2 * py + pc
    return pltpu.make_async_remote_copy(
        src_ref=src_ref if gather else src_ref.at[peer_id], dst_ref=land_ref.at[my_id],
        send_sem=send_sem.at[k - 1], recv_sem=recv_sem.at[k - 1],
        device_id=(px, py, pc), device_id_type=MESH)


def _sequencer_copies(srcs, *, gather, name, collective_id, after):
    n = len(srcs)
    hbm = pltpu.MemorySpace.HBM
    src_refs = [jax.new_ref(s, memory_space=hbm) for s in srcs]
    land_refs = [jax.empty_ref(jax.ShapeDtypeStruct(((N_DEV,) + s.shape) if gather else s.shape, s.dtype),
                               memory_space=hbm) for s in srcs]
    token_in = jax.new_ref(jnp.zeros((8, 128), F32) if after is None else after, memory_space=hbm)
    token_out = jax.empty_ref(jax.ShapeDtypeStruct((8, 128), F32), memory_space=hbm)

    @pl.kernel(mesh=plsc.ScalarSubcoreMesh(axis_name="seq", num_cores=1), name=name,
               scratch_types=(pltpu.SemaphoreType.DMA((n, N_PEERS)), pltpu.SemaphoreType.DMA((n, N_PEERS)),
                              pltpu.SemaphoreType.DMA((n + 1,))),
               compiler_params=pltpu.CompilerParams(collective_id=collective_id))
    def launch(send_sems, recv_sems, local_sems):
        x, y, c = _place()
        my_id = 4 * x + 2 * y + c
        barrier = pltpu.get_barrier_semaphore()
        own = [pltpu.make_async_copy(src_refs[t] if gather else src_refs[t].at[my_id], land_refs[t].at[my_id],
                                     local_sems.at[t]) for t in range(n)]
        if gather:
            sibling = (x, y, 1 - c)
            chips = [(1 - x, y), (x, 1 - y), (1 - x, 1 - y)]
            for peer in [sibling] + [(*chip, c) for chip in chips]:
                pl.semaphore_signal(barrier, inc=1, device_id=peer, device_id_type=MESH)
            pl.semaphore_wait(barrier, 4)

            def copy(t, k, block, to, src=None):
                dst = land_refs[t].at[4 * block[0] + 2 * block[1] + block[2]]
                return pltpu.make_async_remote_copy(
                    src_ref=dst if src is None else src, dst_ref=dst,
                    send_sem=send_sems.at[t, k], recv_sem=recv_sems.at[t, k], device_id=to, device_id_type=MESH)

            for cp in own:
                cp.start()
            sends = []
            for t in range(n):
                sends.append(copy(t, 0, (x, y, c), sibling, src=src_refs[t]))
                sends += [copy(t, 1 + j, (x, y, c), (*chip, c), src=src_refs[t]) for j, chip in enumerate(chips)]
            for cp in sends:
                cp.start()
            for t in range(n):
                for j, chip in enumerate(chips):
                    copy(t, 1 + j, (*chip, c), (x, y, c)).wait_recv()
                    passed_on = copy(t, 4 + j, (*chip, c), sibling)
                    passed_on.start()
                    sends.append(passed_on)
            for t in range(n):
                copy(t, 0, sibling, (x, y, c)).wait_recv()
                for j, chip in enumerate(chips):
                    copy(t, 4 + j, (*chip, 1 - c), (x, y, c)).wait_recv()
            for cp in sends:
                cp.wait_send()
            for cp in own:
                cp.wait()
        else:
            for k in range(1, N_DEV):
                peer = (_flip(x, k & 4), _flip(y, k & 2), _flip(c, k & 1))
                pl.semaphore_signal(barrier, inc=1, device_id=peer, device_id_type=MESH)
            pl.semaphore_wait(barrier, N_PEERS)
            for cp in own:
                cp.start()
            copies = [_peer_copy(k, src_refs[t], land_refs[t], send_sems.at[t], recv_sems.at[t], gather)
                      for t in range(n) for k in range(1, N_DEV)]
            for cp in copies:
                cp.start()
            for cp in own:
                cp.wait()
            for cp in copies:
                cp.wait()
        passed = pltpu.make_async_copy(token_in, token_out, local_sems.at[n])
        passed.start()
        passed.wait()

    launch()
    return [ref[...] for ref in land_refs], token_out[...]


def _all_reduce_small(buf, *, name):
    _, rows, lanes = buf.shape

    def body(in_ref, out_ref, parts, send_sems, recv_sems):
        x, y, c = _place()
        my_id = 4 * x + 2 * y + c
        peers = []
        for k in range(1, N_DEV):
            px, py, pc = _flip(x, k & 4), _flip(y, k & 2), _flip(c, k & 1)
            peers.append(((px, py, pc), 4 * px + 2 * py + pc))
        scatter = [pltpu.make_async_remote_copy(
            src_ref=in_ref.at[pid], dst_ref=parts.at[my_id],
            send_sem=send_sems.at[0, k], recv_sem=recv_sems.at[0, k],
            device_id=peer, device_id_type=MESH) for k, (peer, pid) in enumerate(peers)]
        for cp in scatter:
            cp.start()
        parts[my_id] = in_ref[my_id]
        for cp in scatter:
            cp.wait()
        total = parts[0]
        for d in range(1, N_DEV):
            total = total + parts[d]
        out_ref[my_id] = total
        gather = [pltpu.make_async_remote_copy(
            src_ref=out_ref.at[my_id], dst_ref=out_ref.at[my_id],
            send_sem=send_sems.at[1, k], recv_sem=recv_sems.at[1, k],
            device_id=peer, device_id_type=MESH) for k, (peer, pid) in enumerate(peers)]
        for cp in gather:
            cp.start()
        for k, (peer, pid) in enumerate(peers):
            pltpu.make_async_remote_copy(
                src_ref=out_ref.at[pid], dst_ref=out_ref.at[pid],
                send_sem=send_sems.at[1, k], recv_sem=recv_sems.at[1, k],
                device_id=peer, device_id_type=MESH).wait()

    vmem = pl.BlockSpec(memory_space=pltpu.VMEM)
    return pl.pallas_call(
        body, name=name, in_specs=[vmem], out_specs=vmem,
        out_shape=jax.ShapeDtypeStruct(buf.shape, F32),
        scratch_shapes=[pltpu.VMEM(buf.shape, F32),
                        pltpu.SemaphoreType.DMA((2, 7)), pltpu.SemaphoreType.DMA((2, 7))],
        compiler_params=pltpu.CompilerParams(vmem_limit_bytes=VMEM_LIMIT),
    )(buf)


TM = 512
TMM = 1024
TKW = 2048
MIX_TILE = 128
FFN_BLOCK_TILE = 512


def _block_diag(w):
    wg = w.reshape(N_GROUPS, HEADS_PER_GROUP, LRU_HEAD_DIM, LRU_HEAD_DIM)
    eye = jnp.eye(HEADS_PER_GROUP, dtype=w.dtype)
    bd = wg[:, :, :, None, :] * eye[None, :, None, :, None]
    return bd.reshape(N_GROUPS, LRU_GROUP, LRU_GROUP).astype(BF16)


def _head_blocks(bd):
    b5 = bd.reshape(N_GROUPS, HEADS_PER_GROUP, LRU_HEAD_DIM, HEADS_PER_GROUP, LRU_HEAD_DIM)
    blocks = [b5[:, h, :, h, :] for h in range(HEADS_PER_GROUP)]
    return jnp.stack(blocks, axis=1).reshape(LRU_HEADS, LRU_HEAD_DIM, LRU_HEAD_DIM)


def _w(lw, key, after):
    value = lw[key]
    return value(after) if callable(value) else value


def _layer_fwd(x, lw, tag):
    sv_rows = x.shape[0]
    z, h1 = _norm_in_proj(x, lw["g1"], _w(lw, "w_in_t", x), tm=min(TMM, sv_rows), tn=896, name=f"in_proj_{tag}")
    y_mix, hs = _mixer_fwd(z, _w(lw, "cw", z), lw["cb"], lw["wa_bd"], lw["wx_bd"], lw["ba"], lw["bx"], lw["lam"],
                           _w(lw, "scw", z), tile=MIX_TILE, name=f"mixer_fwd_{tag}")
    x2 = _mm_nn(y_mix, _w(lw, "w_out", y_mix), tm=min(TMM, sv_rows), tn=D_MODEL, tk=D_MIX, out_dtype=F32, name=f"out_proj_{tag}",
                residual=x)
    x3, h2, act, p, u = _ffn_block_fwd(x2, lw["g2"], _w(lw, "w_up_b", x2), _w(lw, "fcw", x2), _w(lw, "w_down", x2),
                                       tile=min(FFN_BLOCK_TILE, sv_rows), name=f"ffn_fwd_{tag}")
    saved = dict(x=x, h1=h1, z=z, y_mix=y_mix, hs=hs, x2=x2, h2=h2, p=p, u=u, act=act)
    return x3, saved


def _layer_bwd(dx3, dx3b, lw, sv, tag, put):
    sv_rows = dx3.shape[0]
    w_in_t, w_out, w_up_b, w_down = (_w(lw, k, dx3) for k in ("w_in_t", "w_out", "w_up_b", "w_down"))
    cw, scw, fcw = (_w(lw, k, dx3) for k in ("cw", "scw", "fcw"))
    g_down = _mm_tn(sv["act"], dx3b, tm=1024, tn=D_MODEL, tk=min(TKW, sv_rows), out_dtype=BF16, name=f"down_bwd_w_{tag}")
    dx2, dx2b, dg2, dp, dfcw = _ffn_block_bwd(dx3, dx3b, sv["p"], sv["u"], sv["x2"], lw["g2"], w_up_b, fcw, w_down,
                                              tile=min(FFN_BLOCK_TILE, sv_rows), name=f"ffn_bwd_{tag}",
                                              token=put("w_down", g_down))
    g_up = _mm_up_bwd_w(sv["h2"], dp, tm=D_MODEL, tk=min(TKW, sv_rows), name=f"up_bwd_w_{tag}")
    dy = _mm_nt(dx2b, w_out, tm=min(TMM, sv_rows), tn=768, tk=D_MODEL, out_dtype=F32, name=f"out_bwd_x_{tag}")
    dz, dcw, dvec, dwa, dwx, dscw = _mixer_bwd_rows(
        sv["z"], sv["hs"], dy, cw, lw["cb"], lw["wa_bd"], lw["wx_bd"], lw["ba"], lw["bx"], lw["lam"],
        scw, tile=MIX_TILE, name=f"mixer_bwd_{tag}", token=put("w_up_b", g_up))
    g_out = _mm_tn(sv["y_mix"], dx2b, tm=768, tn=D_MODEL, tk=min(TKW, sv_rows), out_dtype=BF16, name=f"out_bwd_w_{tag}",
                   token=dz)
    g_in_t = _mm_tn(dz, sv["h1"], tm=896, tn=D_MODEL, tk=min(TKW, sv_rows), out_dtype=BF16, name=f"in_bwd_w_{tag}",
                    token=put("w_out", g_out))
    dx, dxb, dg1 = _in_bwd_norm(dz, w_in_t, sv["x"], lw["g1"], dx2, tm=TM, tk=896, name=f"in_bwd_x_{tag}",
                                token=put("w_in_t", g_in_t))
    small = dict(norm1_g=dg1[0], lru_conv_w=dcw[0:4], lru_conv_b=dvec[0], lru_wa=_head_blocks(dwa),
                 lru_ba=dvec[1], lru_wx=_head_blocks(dwx), lru_bx=dvec[2], lru_lambda=dvec[3],
                 sc_conv_w=dscw[0:3], norm2_g=dg2[0], ffn_conv_w=dfcw[:, 0:3, :])
    return dx, dxb, small


SMALL_ORDER = ("norm1_g", "lru_conv_w", "lru_conv_b", "lru_wa", "lru_ba", "lru_wx", "lru_bx", "lru_lambda",
               "sc_conv_w", "norm2_g", "ffn_conv_w")


def _local_step(x, tgt, layers, final_g, put):
    saved = []
    h = x
    for l in range(DEPTH):
        h, sv = _layer_fwd(h, layers[l], f"l{l}")
        saved.append(sv)
    loss_blk, dx, dxb, dgf = _loss_head(h, final_g, tgt, tm=TM, name="loss_head")
    smalls = [None] * DEPTH
    for l in reversed(range(DEPTH)):
        dx, dxb, smalls[l] = _layer_bwd(dx, dxb, layers[l], saved[l], f"l{l}", functools.partial(put, l))
    return loss_blk[0, 0], dx, smalls, dgf[0]


def kernel(x, norm1_g, w_in, lru_conv_w, lru_conv_b, lru_wa, lru_ba, lru_wx, lru_bx, lru_lambda, sc_conv_w, w_out, norm2_g, w_up, ffn_conv_w, w_down, final_g, loss_target, m_norm1_g, m_w_in, m_lru_conv_w, m_lru_conv_b, m_lru_wa, m_lru_ba, m_lru_wx, m_lru_bx, m_lru_lambda, m_sc_conv_w, m_w_out, m_norm2_g, m_w_up, m_ffn_conv_w, m_w_down, m_final_g, v_norm1_g, v_w_in, v_lru_conv_w, v_lru_conv_b, v_lru_wa, v_lru_ba, v_lru_wx, v_lru_bx, v_lru_lambda, v_sc_conv_w, v_w_out, v_norm2_g, v_w_up, v_ffn_conv_w, v_w_down, v_final_g):
    names = ["norm1_g", "w_in", "lru_conv_w", "lru_conv_b", "lru_wa", "lru_ba", "lru_wx", "lru_bx", "lru_lambda",
             "sc_conv_w", "w_out", "norm2_g", "w_up", "ffn_conv_w", "w_down", "final_g"]
    w = dict(zip(names, [norm1_g, w_in, lru_conv_w, lru_conv_b, lru_wa, lru_ba, lru_wx, lru_bx, lru_lambda,
                         sc_conv_w, w_out, norm2_g, w_up, ffn_conv_w, w_down, final_g]))
    m = dict(zip(names, [m_norm1_g, m_w_in, m_lru_conv_w, m_lru_conv_b, m_lru_wa, m_lru_ba, m_lru_wx, m_lru_bx,
                         m_lru_lambda, m_sc_conv_w, m_w_out, m_norm2_g, m_w_up, m_ffn_conv_w, m_w_down, m_final_g]))
    v = dict(zip(names, [v_norm1_g, v_w_in, v_lru_conv_w, v_lru_conv_b, v_lru_wa, v_lru_ba, v_lru_wx, v_lru_bx,
                         v_lru_lambda, v_sc_conv_w, v_w_out, v_norm2_g, v_w_up, v_ffn_conv_w, v_w_down, v_final_g]))
    my_id = 4 * lax.axis_index("x") + 2 * lax.axis_index("y") + lax.axis_index("c")

    taps = jnp.zeros((DEPTH, 16, 768), F32)
    taps = taps.at[:, 0:4, 0:128].set(lru_conv_w).at[:, 4:7, 0:64].set(sc_conv_w).at[:, 8:11, :].set(ffn_conv_w)
    shards = {}
    for l in range(DEPTH):
        shards[f"w_in_t{l}"] = jnp.swapaxes(w_in[l], 0, 1).astype(BF16)
        if l == 0:
            shards["taps"] = taps.reshape(DEPTH * 16, 768)
        shards[f"w_out{l}"] = w_out[l].astype(BF16)
        shards[f"w_up_b{l}"] = w_up[l].astype(BF16)
        shards[f"w_down{l}"] = w_down[l].astype(BF16)
    ids = iter(range(16))
    got = {}
    chain = [None]
    for group in (("w_in_t0", "taps"), ("w_out0",), ("w_up_b0",), ("w_down0",),
                  ("w_in_t1",), ("w_out1",), ("w_up_b1",), ("w_down1",)):
        lands, chain[0] = _sequencer_copies([shards[k] for k in group], gather=True, name=f"gather_{group[0]}",
                                            collective_id=next(ids), after=None)
        got.update(zip(group, lands))

    def fetch(key, after):
        return got[key]

    def tap_rows(l, lo, hi, width, after):
        tl = fetch("taps", after).reshape(N_DEV, DEPTH, 16, 768)[:, l, lo:hi, 0:width]
        return jnp.transpose(tl, (1, 0, 2)).reshape(hi - lo, N_DEV * width)

    layers = []
    for l in range(DEPTH):
        layers.append(dict(
            g1=norm1_g[l], g2=norm2_g[l], cb=lru_conv_b[l], ba=lru_ba[l], bx=lru_bx[l], lam=lru_lambda[l],
            wa_bd=_block_diag(lru_wa[l]), wx_bd=_block_diag(lru_wx[l]),
            cw=functools.partial(tap_rows, l, 0, 4, 128), scw=functools.partial(tap_rows, l, 4, 7, 64),
            fcw=lambda after, l=l: tap_rows(l, 8, 11, 768, after).reshape(3, 2, D_FF).transpose(1, 0, 2),
            w_in_t=lambda after, l=l: fetch(f"w_in_t{l}", after).reshape(D_IN, D_MODEL),
            w_out=lambda after, l=l: fetch(f"w_out{l}", after).reshape(D_MIX, D_MODEL),
            w_up_b=lambda after, l=l: fetch(f"w_up_b{l}", after),
            w_down=lambda after, l=l: fetch(f"w_down{l}", after).reshape(D_FF, D_MODEL)))

    scatter_handles = {}

    def put(l, key, grad):
        blocks = grad if grad.ndim == 3 else grad.reshape(N_DEV, grad.shape[0] // N_DEV, grad.shape[1])
        (scatter_handles[(l, key)],), chain[0] = _sequencer_copies(
            [blocks], gather=False, name=f"scatter_{key}{l}", collective_id=next(ids), after=chain[0])
        return blocks

    loss_local, dx, smalls, dgf = _local_step(x[0], loss_target[0], layers, final_g, put)

    parts = []
    for l in range(DEPTH):
        for key in ("w_in_t", "w_out", "w_up_b", "w_down"):
            parts.append(scatter_handles[(l, key)])

    flat = [smalls[l][k].reshape(-1) for l in range(DEPTH) for k in SMALL_ORDER] + [dgf.reshape(-1)]
    flat.append(jnp.broadcast_to(loss_local, (128,)))
    sizes = [f.shape[0] for f in flat]
    total = sum(sizes)
    rows = -(-total // (N_DEV * 128 * 8)) * 8
    flat.append(jnp.zeros((N_DEV * rows * 128 - total,), F32))
    small_sum = _all_reduce_small(jnp.concatenate(flat).reshape(N_DEV, rows, 128), name="reduce_small").reshape(-1)
    small_g, off = [], 0
    for sz in sizes:
        small_g.append(small_sum[off:off + sz])
        off += sz
    gs = {}
    for l in range(DEPTH):
        for i, k in enumerate(SMALL_ORDER):
            gs.setdefault(k, []).append(small_g[l * len(SMALL_ORDER) + i])
    g_final = small_g[-2]
    loss = small_g[-1][0]

    grads = {}
    per_layer = {k: [] for k in ("w_in", "w_out", "w_up", "w_down")}
    for l in range(DEPTH):
        p_in, p_out, p_up, p_down = parts[4 * l:4 * l + 4]
        per_layer["w_in"].append(jnp.swapaxes(_sum_parts(p_in, name=f"sum_w_in_l{l}"), 0, 1))
        per_layer["w_out"].append(_sum_parts(p_out, name=f"sum_w_out_l{l}"))
        per_layer["w_up"].append(_sum_parts(p_up, name=f"sum_w_up_l{l}"))
        per_layer["w_down"].append(_sum_parts(p_down, name=f"sum_w_down_l{l}"))
    for k, lst in per_layer.items():
        grads[k] = jnp.stack(lst)
    for k in ("norm1_g", "lru_conv_b", "lru_ba", "lru_bx", "lru_lambda", "norm2_g"):
        grads[k] = jnp.stack(gs[k]).reshape(DEPTH, -1)
    for k in ("lru_wa", "lru_wx"):
        grads[k] = jnp.stack(gs[k]).reshape(DEPTH, LRU_HEADS, LRU_HEAD_DIM, LRU_HEAD_DIM)
    grads["final_g"] = g_final
    cw_full = jnp.stack(gs["lru_conv_w"]).reshape(DEPTH, 4, N_DEV, 128)
    grads["lru_conv_w"] = lax.dynamic_index_in_dim(cw_full, my_id, axis=2, keepdims=False)
    scw_full = jnp.stack(gs["sc_conv_w"]).reshape(DEPTH, 3, N_DEV, 64)
    grads["sc_conv_w"] = lax.dynamic_index_in_dim(scw_full, my_id, axis=2, keepdims=False)
    fcw_full = jnp.stack(gs["ffn_conv_w"]).reshape(DEPTH, 2, 3, D_FF).transpose(0, 2, 1, 3).reshape(DEPTH, 3, N_DEV, 768)
    grads["ffn_conv_w"] = lax.dynamic_index_in_dim(fcw_full, my_id, axis=2, keepdims=False)

    deltas, new_m, new_v = {}, {}, {}
    for k in names:
        shape = w[k].shape
        cols = shape[-1]
        as2d = lambda a: a.reshape(-1, cols)
        d, nm, nv = _adamw(as2d(w[k]), as2d(grads[k]), as2d(m[k]), as2d(v[k]), name=f"adamw_{k}")
        deltas[k], new_m[k], new_v[k] = d.reshape(shape), nm.reshape(shape), nv.reshape(shape)

    return (loss, dx[None], *[grads[k] for k in names], *[deltas[k] for k in names],
            *[new_m[k] for k in names], *[new_v[k] for k in names])
```

```python
import functools
import math

import jax
import jax.numpy as jnp
from jax import lax
from jax.experimental import pallas as pl
from jax.experimental.pallas import tpu as pltpu
from jax.experimental.pallas import tpu_sc as plsc

F32 = jnp.float32
BF16 = jnp.bfloat16

N_DEV = 8
DEPTH = 2
D_MODEL = 1024
D_LRU = 1024
D_SC = 512
D_MIX = D_LRU + D_SC
D_IN = 2 * D_LRU + 3 * D_SC
D_FF = 3072
LRU_HEADS = 16
LRU_HEAD_DIM = 64
LRU_GROUP = 256
N_GROUPS = D_LRU // LRU_GROUP
HEADS_PER_GROUP = LRU_GROUP // LRU_HEAD_DIM
RG_C = 8.0
EPS = 1e-6
HALO = 8

ADAM_LR = 0.001
ADAM_B1 = 0.9
ADAM_B2 = 0.999
ADAM_EPS = 1e-08
ADAM_WD = 0.01
ADAM_STEP = 10

GELU_C = math.sqrt(2.0 / math.pi)
GELU_A = 0.044715

VMEM_LIMIT = 56 * 1024 * 1024
MESH = pl.DeviceIdType.MESH


def _params(*sem):
    return pltpu.CompilerParams(dimension_semantics=tuple(sem) if sem else None,
                                vmem_limit_bytes=VMEM_LIMIT)


def _gelu_parts(x):
    x2 = x * x
    t = jnp.tanh(GELU_C * (x + GELU_A * x * x2))
    half = 0.5 * (1.0 + t)
    g = x * half
    dg = half + 0.5 * x * (1.0 - t * t) * (GELU_C * (1.0 + 3.0 * GELU_A * x2))
    return g, dg


def _gelu(x):
    t = jnp.tanh(GELU_C * (x + GELU_A * x * x * x))
    return 0.5 * x * (1.0 + t)


def _sigmoid(x):
    return 0.5 * jnp.tanh(0.5 * x) + 0.5


def _softplus(x):
    e = jnp.exp(-jnp.abs(x))
    u = 1.0 + e
    log1p_e = jnp.where(u == 1.0, e, jnp.log(u) * (e / (u - 1.0)))
    return jnp.maximum(x, 0.0) + log1p_e


def _rms(x):
    ms = jnp.mean(x * x, axis=-1, keepdims=True)
    return lax.rsqrt(ms + EPS)


def _dot(a, b, dims):
    return lax.dot_general(a, b, (dims, ((), ())), preferred_element_type=F32)


NN = ((1,), (0,))
NT = ((1,), (1,))
TN = ((0,), (0,))


def _matmul(a, b, *, dims, grid, a_spec, b_spec, o_spec, out_shape, acc_shape, name,
            residual=None, r_spec=None, token=None):
    nk = grid[2]

    def body(*refs):
        a_ref, b_ref = refs[0], refs[1]
        r_ref = refs[2] if residual is not None else None
        o_ref = refs[2 + (residual is not None) + (token is not None)]
        prod = _dot(a_ref[...].astype(BF16), b_ref[...].astype(BF16), dims)

        def finish(total):
            if r_ref is not None:
                total = total + r_ref[...]
            o_ref[...] = total.astype(o_ref.dtype)

        if nk == 1:
            finish(prod)
            return
        acc_ref = refs[-1]
        k = pl.program_id(2)

        @pl.when(k == 0)
        def _():
            acc_ref[...] = prod

        @pl.when(jnp.logical_and(k > 0, k < nk - 1))
        def _():
            acc_ref[...] += prod

        @pl.when(k == nk - 1)
        def _():
            finish(acc_ref[...] + prod)

    in_specs = [a_spec, b_spec]
    args = [a, b]
    if residual is not None:
        in_specs.append(r_spec)
        args.append(residual)
    if token is not None:
        in_specs.append(pl.BlockSpec(memory_space=pl.ANY))
        args.append(token)
    return pl.pallas_call(
        body, name=name, grid=grid, in_specs=in_specs, out_specs=o_spec, out_shape=out_shape,
        scratch_shapes=[pltpu.VMEM(acc_shape, F32)] if nk > 1 else [],
        compiler_params=_params("parallel", "parallel", "arbitrary"),
    )(*args)


def _mm_nn(a, b, *, tm, tn, tk, out_dtype, name, residual=None, token=None):
    m, kd = a.shape
    n = b.shape[1]
    return _matmul(
        a, b, dims=NN, grid=(m // tm, n // tn, kd // tk),
        a_spec=pl.BlockSpec((tm, tk), lambda i, j, k: (i, k)),
        b_spec=pl.BlockSpec((tk, tn), lambda i, j, k: (k, j)),
        o_spec=pl.BlockSpec((tm, tn), lambda i, j, k: (i, j)),
        out_shape=jax.ShapeDtypeStruct((m, n), out_dtype), acc_shape=(tm, tn), name=name,
        residual=residual, r_spec=pl.BlockSpec((tm, tn), lambda i, j, k: (i, j)), token=token)


def _mm_nt(a, b, *, tm, tn, tk, out_dtype, name):
    m, kd = a.shape
    n = b.shape[0]
    return _matmul(
        a, b, dims=NT, grid=(m // tm, n // tn, kd // tk),
        a_spec=pl.BlockSpec((tm, tk), lambda i, j, k: (i, k)),
        b_spec=pl.BlockSpec((tn, tk), lambda i, j, k: (j, k)),
        o_spec=pl.BlockSpec((tm, tn), lambda i, j, k: (i, j)),
        out_shape=jax.ShapeDtypeStruct((m, n), out_dtype), acc_shape=(tm, tn), name=name)


def _mm_tn(a, b, *, tm, tn, tk, out_dtype, name, token=None):
    kd, m = a.shape
    n = b.shape[1]
    return _matmul(
        a, b, dims=TN, grid=(m // tm, n // tn, kd // tk),
        a_spec=pl.BlockSpec((tk, tm), lambda i, j, k: (k, i)),
        b_spec=pl.BlockSpec((tk, tn), lambda i, j, k: (k, j)),
        o_spec=pl.BlockSpec((tm, tn), lambda i, j, k: (i, j)),
        out_shape=jax.ShapeDtypeStruct((m, n), out_dtype), acc_shape=(tm, tn), name=name, token=token)


def _mm_up_bwd_w(h2, dp, *, tm, tk, name):
    s = h2.shape[0]
    nb = D_FF * 2 // N_DEV
    per_half = D_FF // nb
    return _matmul(
        h2, dp, dims=TN, grid=(D_MODEL // tm, N_DEV, s // tk),
        a_spec=pl.BlockSpec((tk, tm), lambda i, j, k: (k, i)),
        b_spec=pl.BlockSpec((None, tk, nb), lambda i, j, k: (j // per_half, k, j % per_half)),
        o_spec=pl.BlockSpec((None, tm, nb), lambda i, j, k: (j, i, 0)),
        out_shape=jax.ShapeDtypeStruct((N_DEV, D_MODEL, nb), BF16), acc_shape=(tm, nb), name=name)


def _behind(token):
    return jnp.zeros((8, 128), F32) if token is None else token


def _norm_in_proj(x, g, w_in_t, *, tm, tn, name):
    s, d = x.shape
    n = w_in_t.shape[0]

    def body(x_ref, g_ref, w_ref, z_ref, h_ref):
        @pl.when(pl.program_id(1) == 0)
        def _():
            xv = x_ref[...]
            h_ref[...] = (xv * _rms(xv) * g_ref[...]).astype(BF16)

        z_ref[...] = _dot(h_ref[...], w_ref[...], NT)

    return pl.pallas_call(
        body, name=name, grid=(s // tm, n // tn),
        in_specs=[pl.BlockSpec((tm, d), lambda i, j: (i, 0)), pl.BlockSpec((1, d), lambda i, j: (0, 0)),
                  pl.BlockSpec((tn, d), lambda i, j: (j, 0))],
        out_specs=[pl.BlockSpec((tm, tn), lambda i, j: (i, j)), pl.BlockSpec((tm, d), lambda i, j: (i, 0))],
        out_shape=[jax.ShapeDtypeStruct((s, n), F32), jax.ShapeDtypeStruct((s, d), BF16)],
        compiler_params=_params("parallel", "arbitrary"),
    )(x, g.reshape(1, d), w_in_t)


def _in_bwd_norm(dz, w_in_t, x, g, dres, *, tm, tk, name, token=None):
    s, kd = dz.shape
    d = w_in_t.shape[1]
    nk = kd // tk

    def body(dz_ref, w_ref, x_ref, g_ref, dres_ref, token_ref, dx_ref, dxb_ref, dg_ref, acc_ref):
        i = pl.program_id(0)
        k = pl.program_id(1)

        @pl.when(jnp.logical_and(i == 0, k == 0))
        def _():
            dg_ref[...] = jnp.zeros_like(dg_ref)

        acc_ref[...] = _dot(dz_ref[...], w_ref[...], NN) + jnp.where(k > 0, acc_ref[...], 0.0)

        @pl.when(k == nk - 1)
        def _():
            dh = acc_ref[...]
            xv = x_ref[...]
            rstd = _rms(xv)
            n = xv * rstd
            dn = dh * g_ref[...]
            dx = dres_ref[...] + rstd * (dn - n * jnp.mean(dn * n, axis=-1, keepdims=True))
            dx_ref[...] = dx
            dxb_ref[...] = dx.astype(BF16)
            dg_ref[0:1, :] += jnp.sum(dh * n, axis=0, keepdims=True)

    row = pl.BlockSpec((tm, d), lambda i, k: (i, 0))
    return pl.pallas_call(
        body, name=name, grid=(s // tm, nk),
        in_specs=[pl.BlockSpec((tm, tk), lambda i, k: (i, k)), pl.BlockSpec((tk, d), lambda i, k: (k, 0)),
                  row, pl.BlockSpec((1, d), lambda i, k: (0, 0)), row, pl.BlockSpec(memory_space=pl.ANY)],
        out_specs=[row, row, pl.BlockSpec((8, d), lambda i, k: (0, 0))],
        out_shape=[jax.ShapeDtypeStruct((s, d), F32), jax.ShapeDtypeStruct((s, d), BF16),
                   jax.ShapeDtypeStruct((8, d), F32)],
        scratch_shapes=[pltpu.VMEM((tm, d), F32)],
        compiler_params=_params("arbitrary", "arbitrary"),
    )(dz, w_in_t, x, g.reshape(1, d), dres, _behind(token))


def _loss_head(x, g, tgt, *, tm, name):
    s, d = x.shape

    def body(x_ref, g_ref, t_ref, loss_ref, dx_ref, dxb_ref, dg_ref):
        @pl.when(pl.program_id(0) == 0)
        def _():
            dg_ref[...] = jnp.zeros_like(dg_ref)
            loss_ref[...] = jnp.zeros_like(loss_ref)

        xv = x_ref[...]
        gv = g_ref[...]
        rstd = _rms(xv)
        n = xv * rstd
        e = n * gv - t_ref[...]
        part = 0.5 * jnp.sum(jnp.mean(e * e, axis=-1, keepdims=True), axis=0, keepdims=True)
        loss_ref[...] += jnp.broadcast_to(part, loss_ref.shape)
        dy = e * (1.0 / d)
        dn = dy * gv
        dx = rstd * (dn - n * jnp.mean(dn * n, axis=-1, keepdims=True))
        dx_ref[...] = dx
        dxb_ref[...] = dx.astype(BF16)
        dg_ref[0:1, :] += jnp.sum(dy * n, axis=0, keepdims=True)

    return pl.pallas_call(
        body, name=name, grid=(s // tm,),
        in_specs=[pl.BlockSpec((tm, d), lambda i: (i, 0)), pl.BlockSpec((1, d), lambda i: (0, 0)),
                  pl.BlockSpec((tm, d), lambda i: (i, 0))],
        out_specs=[pl.BlockSpec((8, 128), lambda i: (0, 0)), pl.BlockSpec((tm, d), lambda i: (i, 0)),
                   pl.BlockSpec((tm, d), lambda i: (i, 0)), pl.BlockSpec((8, d), lambda i: (0, 0))],
        out_shape=[jax.ShapeDtypeStruct((8, 128), F32), jax.ShapeDtypeStruct((s, d), F32),
                   jax.ShapeDtypeStruct((s, d), BF16), jax.ShapeDtypeStruct((8, d), F32)],
        compiler_params=_params("arbitrary"),
    )(x, g.reshape(1, d), tgt)


def _scan_rows(a_ref, b_ref, h_ref, carry, *, rows, reverse):
    width = a_ref.shape[1]
    n_chunks = rows // 8
    row = lax.broadcasted_iota(jnp.int32, (8, width), 0)

    def step(ci, carry):
        chunk = (n_chunks - 1 - ci) if reverse else ci
        off = pl.multiple_of(chunk * 8, 8)
        av = a_ref[pl.ds(off, 8), :]
        bv = b_ref[pl.ds(off, 8), :]
        for sh in (1, 2, 4):
            if reverse:
                a_sh = pltpu.roll(av, 8 - sh, 0)
                b_sh = pltpu.roll(bv, 8 - sh, 0)
                m = row < 8 - sh
            else:
                a_sh = pltpu.roll(av, sh, 0)
                b_sh = pltpu.roll(bv, sh, 0)
                m = row >= sh
            bv = jnp.where(m, av * b_sh + bv, bv)
            av = jnp.where(m, av * a_sh, av)
        h = av * carry + bv
        h_ref[pl.ds(off, 8), :] = h
        return h[0:1, :] if reverse else h[7:8, :]

    return lax.fori_loop(0, n_chunks, step, carry)


P_CB, P_BA, P_BX, P_DECAY, P_DSP, N_PAR = 4, 5, 6, 7, 8, 9


def _spread_mixer_params(par, par_sc, cw_ref, cb_ref, ba_ref, bx_ref, lam_ref, scw_ref):
    rows = par.shape[1:]
    for k in range(4):
        par[k] = jnp.broadcast_to(cw_ref[k:k + 1, :], rows)
    par[P_CB] = jnp.broadcast_to(cb_ref[...], rows)
    par[P_BA] = jnp.broadcast_to(ba_ref[...], rows)
    par[P_BX] = jnp.broadcast_to(bx_ref[...], rows)
    par[P_DECAY] = jnp.broadcast_to(-RG_C * _softplus(-lam_ref[...]), rows)
    par[P_DSP] = jnp.broadcast_to(-_sigmoid(-lam_ref[...]), rows)
    for k in range(3):
        par_sc[k] = jnp.broadcast_to(scw_ref[k:k + 1, :], par_sc.shape[1:])


def _gates_rows(pre_r, pre_i, par):
    r = _sigmoid(pre_r + par[P_BA])
    ig = _sigmoid(pre_i + par[P_BX])
    log_a = r * par[P_DECAY]
    a = jnp.exp(log_a)
    one_minus_a2 = -jnp.tanh(log_a) * (a * a + 1.0)
    return r, ig, a, jnp.sqrt(one_minus_a2), one_minus_a2


def _mixer_fwd(z, cw, cb, wa_bd, wx_bd, ba, bx, lam, scw, *, tile, name):
    s = z.shape[0]
    n_tiles = s // tile

    rows_of = lambda r0: slice(r0, r0 + FFN_ROWS)
    col_gate, col_sb, col_sc, col_sx = (slice(D_LRU, 2 * D_LRU), slice(2 * D_LRU, 2 * D_LRU + D_SC),
                                        slice(2 * D_LRU + D_SC, 2 * D_LRU + 2 * D_SC), slice(2 * D_LRU + 2 * D_SC, D_IN))

    def body(z_ref, cw_ref, cb_ref, wa_ref, wx_ref, ba_ref, bx_ref, lam_ref, scw_ref,
             y_ref, hs_ref, par, par_sc, lx_s, lxb_s, a_s, b_s, car_lx, car_q, h_car):
        i = pl.program_id(0)

        @pl.when(i == 0)
        def _():
            car_lx[...] = jnp.zeros_like(car_lx)
            car_q[...] = jnp.zeros_like(car_q)
            h_car[...] = jnp.zeros_like(h_car)
            _spread_mixer_params(par, par_sc, cw_ref, cb_ref, ba_ref, bx_ref, lam_ref, scw_ref)

        before_lx, before_q = car_lx[...], car_q[...]
        for r0 in range(0, tile, FFN_ROWS):
            cur = z_ref[rows_of(r0), 0:D_LRU]
            lx = par[P_CB] + par[3] * cur
            for k in range(3):
                lx = lx + par[k] * _rows_from(before_lx, cur, FFN_ROWS - 3 + k)
            lx_s[rows_of(r0), :] = lx
            lxb_s[rows_of(r0), :] = lx.astype(BF16)
            before_lx = cur
            q = z_ref[rows_of(r0), col_sc] * z_ref[rows_of(r0), col_sx]
            cq = par_sc[2] * q
            for k in range(2):
                cq = cq + par_sc[k] * _rows_from(before_q, q, FFN_ROWS - 2 + k)
            y_ref[rows_of(r0), D_LRU:D_MIX] = (z_ref[rows_of(r0), col_sb] * cq).astype(BF16)
            before_q = q
        car_lx[...] = before_lx
        car_q[...] = before_q

        for g in range(N_GROUPS):
            cols = slice(g * LRU_GROUP, (g + 1) * LRU_GROUP)
            a_s[:, cols] = _dot(lxb_s[:, cols], wa_ref[g], NN)
            b_s[:, cols] = _dot(lxb_s[:, cols], wx_ref[g], NN)

        for r0 in range(0, tile, FFN_ROWS):
            _, ig, a, mult, _ = _gates_rows(a_s[rows_of(r0), :], b_s[rows_of(r0), :], par)
            a_s[rows_of(r0), :] = a
            b_s[rows_of(r0), :] = mult * (ig * lx_s[rows_of(r0), :])
        h_car[0:1, :] = _scan_rows(a_s, b_s, hs_ref, h_car[0:1, :], rows=tile, reverse=False)

        for r0 in range(0, tile, FFN_ROWS):
            y_ref[rows_of(r0), 0:D_LRU] = (hs_ref[rows_of(r0), :] * _gelu(z_ref[rows_of(r0), col_gate])).astype(BF16)

    full = lambda shape: pl.BlockSpec(shape, lambda i: (0,) * len(shape))
    return pl.pallas_call(
        body, name=name, grid=(n_tiles,),
        in_specs=[pl.BlockSpec((tile, D_IN), lambda i: (i, 0)),
                  full((4, D_LRU)), full((1, D_LRU)),
                  full((N_GROUPS, LRU_GROUP, LRU_GROUP)), full((N_GROUPS, LRU_GROUP, LRU_GROUP)),
                  full((1, D_LRU)), full((1, D_LRU)), full((1, D_LRU)), full((3, D_SC))],
        out_specs=[pl.BlockSpec((tile, D_MIX), lambda i: (i, 0)), pl.BlockSpec((tile, D_LRU), lambda i: (i, 0))],
        out_shape=[jax.ShapeDtypeStruct((s, D_MIX), BF16), jax.ShapeDtypeStruct((s, D_LRU), F32)],
        scratch_shapes=[pltpu.VMEM((N_PAR, FFN_ROWS, D_LRU), F32), pltpu.VMEM((3, FFN_ROWS, D_SC), F32),
                        pltpu.VMEM((tile, D_LRU), F32), pltpu.VMEM((tile, D_LRU), BF16),
                        pltpu.VMEM((tile, D_LRU), F32), pltpu.VMEM((tile, D_LRU), F32),
                        pltpu.VMEM((FFN_ROWS, D_LRU), F32), pltpu.VMEM((FFN_ROWS, D_SC), F32),
                        pltpu.VMEM((8, D_LRU), F32)],
        compiler_params=_params("arbitrary"),
    )(z, cw, cb.reshape(1, -1), wa_bd, wx_bd, ba.reshape(1, -1), bx.reshape(1, -1), lam.reshape(1, -1), scw)


def _fold8(x):
    return sum(x[q:q + 8] for q in range(0, x.shape[0], 8))


def _mixer_bwd_rows(z, hs, dy, cw, cb, wa_bd, wx_bd, ba, bx, lam, scw, *, tile, name, token=None):
    s = z.shape[0]
    n_tiles = s // tile
    per8 = tile // 8
    rows_of = lambda r0: slice(r0, r0 + FFN_ROWS)
    col_gate, col_sb, col_sc, col_sx = (slice(D_LRU, 2 * D_LRU), slice(2 * D_LRU, 2 * D_LRU + D_SC),
                                        slice(2 * D_LRU + D_SC, 2 * D_LRU + 2 * D_SC), slice(2 * D_LRU + 2 * D_SC, D_IN))
    up = range(0, tile, FFN_ROWS)
    down = range(tile - FFN_ROWS, -1, -FFN_ROWS)
    A_CB, A_BA, A_BX, A_SP, A_CW = 0, 1, 2, 3, 4

    def body(z_ref, zp_ref, hs_ref, hsp_ref, dy_ref, cw_ref, cb_ref, wa_ref, wx_ref, ba_ref, bx_ref, lam_ref, scw_ref,
             token_ref, dz_ref, dcw_ref, dvec_ref, dwa_ref, dwx_ref, dscw_ref,
             par, par_sc, lx_s, lxb_s, cq_s, pr_s, pi_s, r_s, ig_s, a_s, mult_s, inv_s, ash_s, b_s, lam_s,
             dlx_s, dpr_b, dpi_b, back_s, car_a, car_dlx, car_dcq, l_car, acc, acc_sc):
        i = pl.program_id(0)

        @pl.when(i == 0)
        def _():
            for ref in (dwa_ref, dwx_ref, l_car, car_a, car_dlx, car_dcq, acc, acc_sc):
                ref[...] = jnp.zeros_like(ref)
            _spread_mixer_params(par, par_sc, cw_ref, cb_ref, ba_ref, bx_ref, lam_ref, scw_ref)

        keep = jnp.where(i == n_tiles - 1, 0.0, 1.0)
        zeros8 = lambda n: jnp.zeros((8, n), F32)

        before_lx = jnp.concatenate([zeros8(D_LRU), zp_ref[:, 0:D_LRU] * keep], axis=0)
        before_q = jnp.concatenate([zeros8(D_SC), zp_ref[:, col_sc] * zp_ref[:, col_sx] * keep], axis=0)
        for r0 in up:
            cur = z_ref[rows_of(r0), 0:D_LRU]
            lx = par[P_CB] + par[3] * cur
            for k in range(3):
                lx = lx + par[k] * _rows_from(before_lx, cur, FFN_ROWS - 3 + k)
            lx_s[rows_of(r0), :] = lx
            lxb_s[rows_of(r0), :] = lx.astype(BF16)
            before_lx = cur
            q = z_ref[rows_of(r0), col_sc] * z_ref[rows_of(r0), col_sx]
            cq = par_sc[2] * q
            for k in range(2):
                cq = cq + par_sc[k] * _rows_from(before_q, q, FFN_ROWS - 2 + k)
            cq_s[rows_of(r0), :] = cq
            before_q = q

        for g in range(N_GROUPS):
            cols = slice(g * LRU_GROUP, (g + 1) * LRU_GROUP)
            pr_s[:, cols] = _dot(lxb_s[:, cols], wa_ref[g], NN)
            pi_s[:, cols] = _dot(lxb_s[:, cols], wx_ref[g], NN)

        after_a = car_a[...]
        for r0 in down:
            r, ig, a, mult, one_minus_a2 = _gates_rows(pr_s[rows_of(r0), :], pi_s[rows_of(r0), :], par)
            r_s[rows_of(r0), :] = r
            ig_s[rows_of(r0), :] = ig
            a_s[rows_of(r0), :] = a
            mult_s[rows_of(r0), :] = mult
            inv_s[rows_of(r0), :] = lax.rsqrt(one_minus_a2)
            ash_s[rows_of(r0), :] = _rows_from(a, after_a, 1)
            after_a = a
            ge, dge = _gelu_parts(z_ref[rows_of(r0), col_gate])
            dy_lru = dy_ref[rows_of(r0), 0:D_LRU]
            dz_ref[rows_of(r0), col_gate] = (dy_lru * hs_ref[rows_of(r0), :] * dge).astype(BF16)
            b_s[rows_of(r0), :] = dy_lru * ge
        car_a[...] = after_a
        l_car[0:1, :] = _scan_rows(ash_s, b_s, lam_s, l_car[0:1, :], rows=tile, reverse=True)

        before_h = jnp.concatenate([zeros8(D_LRU), hsp_ref[...] * keep], axis=0)
        for r0 in up:
            lv = lam_s[rows_of(r0), :]
            h_here = hs_ref[rows_of(r0), :]
            lx, r, ig, a = lx_s[rows_of(r0), :], r_s[rows_of(r0), :], ig_s[rows_of(r0), :], a_s[rows_of(r0), :]
            mult = mult_s[rows_of(r0), :]
            da = lv * _rows_from(before_h, h_here, FFN_ROWS - 1)
            before_h = h_here
            d_mult = lv * ig * lx
            d_i = lv * mult * lx
            dlx_s[rows_of(r0), :] = lv * mult * ig
            dlog_a = da * a - d_mult * (a * a) * inv_s[rows_of(r0), :]
            dpre_r = dlog_a * par[P_DECAY] * r * (1.0 - r)
            dpre_i = d_i * ig * (1.0 - ig)
            acc[A_BA] += _fold8(dpre_r)
            acc[A_BX] += _fold8(dpre_i)
            acc[A_SP] += _fold8(dlog_a * r)
            dpr_b[rows_of(r0), :] = dpre_r.astype(BF16)
            dpi_b[rows_of(r0), :] = dpre_i.astype(BF16)

        for g in range(N_GROUPS):
            cols = slice(g * LRU_GROUP, (g + 1) * LRU_GROUP)
            dwa_ref[g] += _dot(lxb_s[:, cols], dpr_b[:, cols], TN)
            dwx_ref[g] += _dot(lxb_s[:, cols], dpi_b[:, cols], TN)
            back_s[:, cols] = _dot(dpr_b[:, cols], wa_ref[g], NT) + _dot(dpi_b[:, cols], wx_ref[g], NT)

        after_dlx, after_dcq = car_dlx[...], car_dcq[...]
        for r0 in down:
            dlx = dlx_s[rows_of(r0), :] + back_s[rows_of(r0), :]
            lxp = z_ref[rows_of(r0), 0:D_LRU]
            acc[A_CB] += _fold8(dlx)
            acc[A_CW + 3] += _fold8(dlx * lxp)
            dlxp = par[3] * dlx
            for sh in range(1, 4):
                below = _rows_from(dlx, after_dlx, sh)
                dlxp = dlxp + par[3 - sh] * below
                acc[A_CW + 3 - sh] += _fold8(below * lxp)
            dz_ref[rows_of(r0), 0:D_LRU] = dlxp.astype(BF16)
            after_dlx = dlx

            dy_sc = dy_ref[rows_of(r0), D_LRU:D_MIX]
            sb, sc, sx = z_ref[rows_of(r0), col_sb], z_ref[rows_of(r0), col_sc], z_ref[rows_of(r0), col_sx]
            dz_ref[rows_of(r0), col_sb] = (dy_sc * cq_s[rows_of(r0), :]).astype(BF16)
            dcq = dy_sc * sb
            q = sc * sx
            acc_sc[2] += _fold8(dcq * q)
            dq = par_sc[2] * dcq
            for sh in range(1, 3):
                below = _rows_from(dcq, after_dcq, sh)
                dq = dq + par_sc[2 - sh] * below
                acc_sc[2 - sh] += _fold8(below * q)
            dz_ref[rows_of(r0), col_sc] = (dq * sx).astype(BF16)
            dz_ref[rows_of(r0), col_sx] = (dq * sc).astype(BF16)
            after_dcq = dcq
        car_dlx[...] = after_dlx
        car_dcq[...] = after_dcq

        @pl.when(i == n_tiles - 1)
        def _():
            total = lambda x: jnp.sum(x, axis=0, keepdims=True)
            dcw_ref[...] = jnp.zeros_like(dcw_ref)
            dvec_ref[...] = jnp.zeros_like(dvec_ref)
            dscw_ref[...] = jnp.zeros_like(dscw_ref)
            for k in range(4):
                dcw_ref[k:k + 1, :] = total(acc[A_CW + k])
            for k in range(3):
                dvec_ref[k:k + 1, :] = total(acc[k])
                dscw_ref[k:k + 1, :] = total(acc_sc[k])
            dvec_ref[3:4, :] = total(acc[A_SP]) * (-RG_C) * par[P_DSP][0:1, :]

    rev = lambda i: n_tiles - 1 - i
    prev8 = lambda i: jnp.maximum(rev(i) * per8 - 1, 0)
    full = lambda shape: pl.BlockSpec(shape, lambda i: (0,) * len(shape))
    wide = lambda rows, dt=F32: pltpu.VMEM((rows, D_LRU), dt)
    return pl.pallas_call(
        body, name=name, grid=(n_tiles,),
        in_specs=[pl.BlockSpec((tile, D_IN), lambda i: (rev(i), 0)),
                  pl.BlockSpec((HALO, D_IN), lambda i: (prev8(i), 0)),
                  pl.BlockSpec((tile, D_LRU), lambda i: (rev(i), 0)),
                  pl.BlockSpec((HALO, D_LRU), lambda i: (prev8(i), 0)),
                  pl.BlockSpec((tile, D_MIX), lambda i: (rev(i), 0)),
                  full((4, D_LRU)), full((1, D_LRU)),
                  full((N_GROUPS, LRU_GROUP, LRU_GROUP)), full((N_GROUPS, LRU_GROUP, LRU_GROUP)),
                  full((1, D_LRU)), full((1, D_LRU)), full((1, D_LRU)), full((3, D_SC)),
                  pl.BlockSpec(memory_space=pl.ANY)],
        out_specs=[pl.BlockSpec((tile, D_IN), lambda i: (rev(i), 0)),
                   full((8, D_LRU)), full((8, D_LRU)),
                   full((N_GROUPS, LRU_GROUP, LRU_GROUP)), full((N_GROUPS, LRU_GROUP, LRU_GROUP)),
                   full((8, D_SC))],
        out_shape=[jax.ShapeDtypeStruct((s, D_IN), BF16),
                   jax.ShapeDtypeStruct((8, D_LRU), F32), jax.ShapeDtypeStruct((8, D_LRU), F32),
                   jax.ShapeDtypeStruct((N_GROUPS, LRU_GROUP, LRU_GROUP), F32),
                   jax.ShapeDtypeStruct((N_GROUPS, LRU_GROUP, LRU_GROUP), F32),
                   jax.ShapeDtypeStruct((8, D_SC), F32)],
        scratch_shapes=[pltpu.VMEM((N_PAR, FFN_ROWS, D_LRU), F32), pltpu.VMEM((3, FFN_ROWS, D_SC), F32),
                        wide(tile), wide(tile, BF16), pltpu.VMEM((tile, D_SC), F32),
                        wide(tile), wide(tile), wide(tile), wide(tile), wide(tile), wide(tile), wide(tile),
                        wide(tile), wide(tile), wide(tile),
                        wide(tile), wide(tile, BF16), wide(tile, BF16), wide(tile),
                        wide(FFN_ROWS), wide(FFN_ROWS), pltpu.VMEM((FFN_ROWS, D_SC), F32), wide(8),
                        pltpu.VMEM((8, 8, D_LRU), F32), pltpu.VMEM((3, 8, D_SC), F32)],
        compiler_params=_params("arbitrary"),
    )(z, z, hs, hs, dy, cw, cb.reshape(1, -1), wa_bd, wx_bd, ba.reshape(1, -1), bx.reshape(1, -1),
      lam.reshape(1, -1), scw, _behind(token))


FFN_ROWS = 16
FFN_GROUPS = 2


def _spread_taps(fw_ref, taps):
    for half in range(2):
        for k in range(3):
            taps[half, k] = jnp.broadcast_to(fw_ref[half, k:k + 1, :], taps.shape[2:])


def _rows_from(first, second, start):
    stack = jnp.concatenate([first, second], axis=0)
    return pltpu.roll(stack, 2 * FFN_ROWS - start, 0)[0:FFN_ROWS]


def _conv3_rows(taps, ext_ref, half, row):
    before = ext_ref[half, row - FFN_ROWS:row, :]
    here = ext_ref[half, row:row + FFN_ROWS, :]
    acc = taps[half, 2] * here
    for k in range(2):
        acc = acc + taps[half, k] * _rows_from(before, here, FFN_ROWS - 2 + k)
    return acc


HALO_B = 16


def _ffn_block_fwd(x2, g2, w_up_b, fcw, w_down, *, tile, name):
    s = x2.shape[0]
    nb = w_up_b.shape[2]
    blocks = D_FF // nb
    per16 = tile // HALO_B

    def body(x2_ref, x2p_ref, g_ref, wg_ref, wu_ref, fw_ref, wd_ref, x3_ref, h_ref, act_ref, p_ref, u_ref,
             ext_p, acc_ref, taps, lhs):
        i = pl.program_id(0)
        j = pl.program_id(1)
        keep = jnp.where(i == 0, 0.0, 1.0)
        _spread_taps(fw_ref, taps)
        @pl.when(j == 0)
        def _():
            for rows_ref, at in ((x2p_ref, 0), (x2_ref, HALO_B)):
                xv = rows_ref[...]
                lhs[at:at + xv.shape[0], :] = (xv * _rms(xv) * g_ref[...]).astype(BF16)
            h_ref[...] = lhs[HALO_B:HALO_B + tile, :]

        grp = tile // FFN_GROUPS
        for g in range(FFN_GROUPS):
            new = slice(g * grp + (HALO_B if g else 0), (g + 1) * grp + HALO_B)
            for half, w_ref in ((0, wg_ref), (1, wu_ref)):
                pe = _dot(lhs[new, :], w_ref[...], NN)
                if g == 0:
                    ext_p[half, 0:HALO_B, :] = pe[0:HALO_B] * keep
                    ext_p[half, HALO_B:grp + HALO_B, :] = pe[HALO_B:]
                    p_ref[half, 0:grp, :] = pe[HALO_B:].astype(BF16)
                else:
                    ext_p[half, new, :] = pe
                    p_ref[half, g * grp:(g + 1) * grp, :] = pe.astype(BF16)
        for g in range(FFN_GROUPS):
            rows = slice(g * grp, (g + 1) * grp)
            acts = []
            for r0 in range(g * grp, (g + 1) * grp, FFN_ROWS):
                u = [_conv3_rows(taps, ext_p, half, HALO_B + r0) for half in range(2)]
                for half in range(2):
                    u_ref[half, r0:r0 + FFN_ROWS, :] = u[half].astype(BF16)
                acts.append((_gelu(u[0]) * u[1]).astype(BF16))
                act_ref[r0:r0 + FFN_ROWS, :] = acts[-1]
            contrib = _dot(jnp.concatenate(acts, axis=0), wd_ref[...], NN)
            acc_ref[rows, :] = contrib + jnp.where(j > 0, acc_ref[rows, :], 0.0)

        @pl.when(j == blocks - 1)
        def _():
            x3_ref[...] = x2_ref[...] + acc_ref[...]

    return pl.pallas_call(
        body, name=name, grid=(s // tile, blocks),
        in_specs=[pl.BlockSpec((tile, D_MODEL), lambda i, j: (i, 0)),
                  pl.BlockSpec((HALO_B, D_MODEL), lambda i, j: (jnp.maximum(i * per16 - 1, 0), 0)),
                  pl.BlockSpec((1, D_MODEL), lambda i, j: (0, 0)),
                  pl.BlockSpec((None, D_MODEL, nb), lambda i, j: (j, 0, 0)),
                  pl.BlockSpec((None, D_MODEL, nb), lambda i, j: (j + blocks, 0, 0)),
                  pl.BlockSpec((2, 3, nb), lambda i, j: (0, 0, j)),
                  pl.BlockSpec((nb, D_MODEL), lambda i, j: (j, 0))],
        out_specs=[pl.BlockSpec((tile, D_MODEL), lambda i, j: (i, 0)),
                   pl.BlockSpec((tile, D_MODEL), lambda i, j: (i, 0)),
                   pl.BlockSpec((tile, nb), lambda i, j: (i, j)),
                   pl.BlockSpec((2, tile, nb), lambda i, j: (0, i, j)),
                   pl.BlockSpec((2, tile, nb), lambda i, j: (0, i, j))],
        out_shape=[jax.ShapeDtypeStruct((s, D_MODEL), F32), jax.ShapeDtypeStruct((s, D_MODEL), BF16),
                   jax.ShapeDtypeStruct((s, D_FF), BF16),
                   jax.ShapeDtypeStruct((2, s, D_FF), BF16), jax.ShapeDtypeStruct((2, s, D_FF), BF16)],
        scratch_shapes=[pltpu.VMEM((2, tile + HALO_B, nb), F32), pltpu.VMEM((tile, D_MODEL), F32),
                        pltpu.VMEM((2, 3, FFN_ROWS, nb), F32), pltpu.VMEM((tile + HALO_B, D_MODEL), BF16)],
        compiler_params=_params("parallel", "arbitrary"),
    )(x2, x2, g2.reshape(1, -1), w_up_b, w_up_b, fcw, w_down)


def _ffn_block_bwd(dx3, dx3b, p, u, x2, g2, w_up_b, fcw, w_down, *, tile, name, token=None):
    s = x2.shape[0]
    nb = w_up_b.shape[2]
    blocks = D_FF // nb
    n_tiles = s // tile
    per16 = tile // HALO_B
    last16 = s // HALO_B - 1

    def body(dxb_ref, dxbn_ref, wd_ref, p_ref, u_ref, un_ref, fw_ref, wg_ref, wu_ref, x2_ref, g_ref, dx3_ref,
             token_ref, dx2_ref, dx2b_ref, dg_ref, dp_ref, dw_ref, da_s, acc_w, acc_dh, taps):
        i = pl.program_id(0)
        j = pl.program_id(1)

        @pl.when(jnp.logical_and(i == 0, j == 0))
        def _():
            acc_w[...] = jnp.zeros_like(acc_w)
            dg_ref[...] = jnp.zeros_like(dg_ref)

        keep_next = jnp.where(i == n_tiles - 1, 0.0, 1.0)
        _spread_taps(fw_ref, taps)
        lhs = jnp.concatenate([dxb_ref[...], dxbn_ref[...]], axis=0)
        grp = tile // FFN_GROUPS
        for g in reversed(range(FFN_GROUPS)):
            new = slice(g * grp, (g + 1) * grp + (HALO_B if g == FFN_GROUPS - 1 else 0))
            da_s[new, :] = _dot(lhs[new], wd_ref[...], NT)

        def du_rows(da, u_gate, u_up):
            ge, dge = _gelu_parts(u_gate)
            return da * u_up * dge, da * ge

        after = du_rows(da_s[tile:tile + HALO_B, :] * keep_next, un_ref[0].astype(F32), un_ref[1].astype(F32))
        for g in reversed(range(FFN_GROUPS)):
            rows = slice(g * grp, (g + 1) * grp)
            dps = ([], [])
            for r0 in range((g + 1) * grp - FFN_ROWS, g * grp - 1, -FFN_ROWS):
                du = du_rows(da_s[r0:r0 + FFN_ROWS, :], u_ref[0, r0:r0 + FFN_ROWS, :].astype(F32),
                             u_ref[1, r0:r0 + FFN_ROWS, :].astype(F32))
                for half in range(2):
                    below = [du[half], _rows_from(du[half], after[half], 1), _rows_from(du[half], after[half], 2)]
                    acc = taps[half, 2] * below[0]
                    for k in range(2):
                        acc = acc + taps[half, k] * below[2 - k]
                    dps[half].insert(0, acc.astype(BF16))
                    dp_ref[half, r0:r0 + FFN_ROWS, :] = dps[half][0]
                    p_rows = p_ref[half, r0:r0 + FFN_ROWS, :].astype(F32)
                    for k in range(3):
                        prod = below[2 - k] * p_rows
                        acc_w[j, half, k] += sum(prod[q:q + 8] for q in range(0, FFN_ROWS, 8))
                after = du
            contrib = (_dot(jnp.concatenate(dps[0], axis=0), wg_ref[...], NT)
                       + _dot(jnp.concatenate(dps[1], axis=0), wu_ref[...], NT))
            acc_dh[rows, :] = contrib + jnp.where(j > 0, acc_dh[rows, :], 0.0)

        @pl.when(j == blocks - 1)
        def _():
            dh = acc_dh[...]
            xv = x2_ref[...]
            rstd = _rms(xv)
            n = xv * rstd
            dn = dh * g_ref[...]
            dx = dx3_ref[...] + rstd * (dn - n * jnp.mean(dn * n, axis=-1, keepdims=True))
            dx2_ref[...] = dx
            dx2b_ref[...] = dx.astype(BF16)
            dg_ref[0:1, :] += jnp.sum(dh * n, axis=0, keepdims=True)

        @pl.when(jnp.logical_and(i == n_tiles - 1, j == blocks - 1))
        def _():
            dw_ref[...] = jnp.zeros_like(dw_ref)
            for jj in range(blocks):
                for half in range(2):
                    for k in range(3):
                        dw_ref[half, k:k + 1, jj * nb:(jj + 1) * nb] = jnp.sum(acc_w[jj, half, k], axis=0, keepdims=True)

    next16 = lambda i: jnp.minimum((i + 1) * per16, last16)
    return pl.pallas_call(
        body, name=name, grid=(n_tiles, blocks),
        in_specs=[pl.BlockSpec((tile, D_MODEL), lambda i, j: (i, 0)),
                  pl.BlockSpec((HALO_B, D_MODEL), lambda i, j: (next16(i), 0)),
                  pl.BlockSpec((nb, D_MODEL), lambda i, j: (j, 0)),
                  pl.BlockSpec((2, tile, nb), lambda i, j: (0, i, j)),
                  pl.BlockSpec((2, tile, nb), lambda i, j: (0, i, j)),
                  pl.BlockSpec((2, HALO_B, nb), lambda i, j: (0, next16(i), j)),
                  pl.BlockSpec((2, 3, nb), lambda i, j: (0, 0, j)),
                  pl.BlockSpec((None, D_MODEL, nb), lambda i, j: (j, 0, 0)),
                  pl.BlockSpec((None, D_MODEL, nb), lambda i, j: (j + blocks, 0, 0)),
                  pl.BlockSpec((tile, D_MODEL), lambda i, j: (i, 0)),
                  pl.BlockSpec((1, D_MODEL), lambda i, j: (0, 0)),
                  pl.BlockSpec((tile, D_MODEL), lambda i, j: (i, 0)),
                  pl.BlockSpec(memory_space=pl.ANY)],
        out_specs=[pl.BlockSpec((tile, D_MODEL), lambda i, j: (i, 0)),
                   pl.BlockSpec((tile, D_MODEL), lambda i, j: (i, 0)),
                   pl.BlockSpec((8, D_MODEL), lambda i, j: (0, 0)),
                   pl.BlockSpec((2, tile, nb), lambda i, j: (0, i, j)),
                   pl.BlockSpec((2, 8, D_FF), lambda i, j: (0, 0, 0))],
        out_shape=[jax.ShapeDtypeStruct((s, D_MODEL), F32), jax.ShapeDtypeStruct((s, D_MODEL), BF16),
                   jax.ShapeDtypeStruct((8, D_MODEL), F32), jax.ShapeDtypeStruct((2, s, D_FF), BF16),
                   jax.ShapeDtypeStruct((2, 8, D_FF), F32)],
        scratch_shapes=[pltpu.VMEM((tile + HALO_B, nb), F32), pltpu.VMEM((blocks, 2, 3, 8, nb), F32),
                        pltpu.VMEM((tile, D_MODEL), F32), pltpu.VMEM((2, 3, FFN_ROWS, nb), F32)],
        compiler_params=_params("arbitrary", "arbitrary"),
    )(dx3b, dx3b, w_down, p, u, u, fcw, w_up_b, w_up_b, x2, g2.reshape(1, -1), dx3, _behind(token))


def _adamw_math(w, g, m, v):
    m = ADAM_B1 * m + (1.0 - ADAM_B1) * g
    v = ADAM_B2 * v + (1.0 - ADAM_B2) * (g * g)
    m_hat = m / (1.0 - ADAM_B1 ** ADAM_STEP)
    v_hat = v / (1.0 - ADAM_B2 ** ADAM_STEP)
    delta = -ADAM_LR * (m_hat / (jnp.sqrt(v_hat) + ADAM_EPS) + ADAM_WD * w)
    return delta, m, v


def _adamw(w, g, m, v, *, name):
    rows, cols = w.shape
    tr = rows
    for cand in (512, 256, 128, 64, 32, 16, 8):
        if rows % cand == 0 and rows > cand:
            tr = cand
            break

    def body(w_ref, g_ref, m_ref, v_ref, d_ref, nm_ref, nv_ref):
        d, nm, nv = _adamw_math(w_ref[...], g_ref[...], m_ref[...], v_ref[...])
        d_ref[...] = d
        nm_ref[...] = nm
        nv_ref[...] = nv

    spec = pl.BlockSpec((tr, cols), lambda i: (i, 0))
    return pl.pallas_call(
        body, name=name, grid=(rows // tr,), in_specs=[spec] * 4, out_specs=[spec] * 3,
        out_shape=[jax.ShapeDtypeStruct((rows, cols), F32)] * 3,
        compiler_params=_params("parallel"),
    )(w, g, m, v)


def _sum_adamw(parts, w, m, v, *, name):
    depth, rows, cols = w.shape
    tr = rows
    for cand in (256, 128, 64):
        if rows % cand == 0 and rows > cand:
            tr = cand
            break

    def body(*refs):
        part_refs = refs[:depth]
        w_ref, m_ref, v_ref, g_ref, d_ref, nm_ref, nv_ref = refs[depth:]
        layer = pl.program_id(0)
        grad = None
        for k, p_ref in enumerate(part_refs):
            total = p_ref[0].astype(F32)
            for dev in range(1, N_DEV):
                total = total + p_ref[dev].astype(F32)
            grad = total if grad is None else jnp.where(layer == k, total, grad)
        d, nm, nv = _adamw_math(w_ref[...], grad, m_ref[...], v_ref[...])
        g_ref[...] = grad
        d_ref[...] = d
        nm_ref[...] = nm
        nv_ref[...] = nv

    part_spec = lambda k: pl.BlockSpec((N_DEV, tr, cols), lambda l, i: (0, jnp.where(l == k, i, 0), 0))
    spec = pl.BlockSpec((None, tr, cols), lambda l, i: (l, i, 0))
    return pl.pallas_call(
        body, name=name, grid=(depth, rows // tr),
        in_specs=[part_spec(k) for k in range(depth)] + [spec] * 3, out_specs=[spec] * 4,
        out_shape=[jax.ShapeDtypeStruct((depth, rows, cols), F32)] * 4,
        compiler_params=_params("parallel", "parallel"),
    )(*parts, w, m, v)


def _sum_parts(parts, *, name):
    _, rows, cols = parts.shape
    tr = rows
    for cand in (256, 128, 64, 32, 16):
        if rows % cand == 0 and rows > cand:
            tr = cand
            break

    def body(p_ref, o_ref):
        acc = p_ref[0].astype(F32)
        for d in range(1, N_DEV):
            acc = acc + p_ref[d].astype(F32)
        o_ref[...] = acc

    return pl.pallas_call(
        body, name=name, grid=(rows // tr,),
        in_specs=[pl.BlockSpec((N_DEV, tr, cols), lambda i: (0, i, 0))],
        out_specs=pl.BlockSpec((tr, cols), lambda i: (i, 0)),
        out_shape=jax.ShapeDtypeStruct((rows, cols), F32),
        compiler_params=_params("parallel"),
    )(parts)


def _place():
    return lax.axis_index("x"), lax.axis_index("y"), lax.axis_index("c")


def _flip(v, bit):
    return 1 - v if bit else v


N_PEERS = N_DEV - 1


def _peer_copy(k, src_ref, land_ref, send_sem, recv_sem, gather):
    x, y, c = _place()
    my_id = 4 * x + 2 * y + c
    px, py, pc = _flip(x, k & 4), _flip(y, k & 2), _flip(c, k & 1)
    peer_id = 4 * px + 2 * py + pc
    return pltpu.make_async_remote_copy(
        src_ref=src_ref if gather else src_ref.at[peer_id], dst_ref=land_ref.at[my_id],
        send_sem=send_sem.at[k - 1], recv_sem=recv_sem.at[k - 1],
        device_id=(px, py, pc), device_id_type=MESH)


def _sequencer_copies(srcs, *, gather, name, collective_id, after):
    n = len(srcs)
    hbm = pltpu.MemorySpace.HBM
    src_refs = [jax.new_ref(s, memory_space=hbm) for s in srcs]
    land_refs = [jax.empty_ref(jax.ShapeDtypeStruct(((N_DEV,) + s.shape) if gather else s.shape, s.dtype),
                               memory_space=hbm) for s in srcs]
    token_in = jax.new_ref(jnp.zeros((8, 128), F32) if after is None else after, memory_space=hbm)
    token_out = jax.empty_ref(jax.ShapeDtypeStruct((8, 128), F32), memory_space=hbm)

    @pl.kernel(mesh=plsc.ScalarSubcoreMesh(axis_name="seq", num_cores=1), name=name,
               scratch_types=(pltpu.SemaphoreType.DMA((n, N_PEERS)), pltpu.SemaphoreType.DMA((n, N_PEERS)),
                              pltpu.SemaphoreType.DMA((n + 1,))),
               compiler_params=pltpu.CompilerParams(collective_id=collective_id))
    def launch(send_sems, recv_sems, local_sems):
        x, y, c = _place()
        my_id = 4 * x + 2 * y + c
        barrier = pltpu.get_barrier_semaphore()
        own = [pltpu.make_async_copy(src_refs[t] if gather else src_refs[t].at[my_id], land_refs[t].at[my_id],
                                     local_sems.at[t]) for t in range(n)]
        if gather:
            sibling = (x, y, 1 - c)
            chips = [(1 - x, y), (x, 1 - y), (1 - x, 1 - y)]
            for peer in [sibling] + [(*chip, c) for chip in chips]:
                pl.semaphore_signal(barrier, inc=1, device_id=peer, device_id_type=MESH)
            pl.semaphore_wait(barrier, 4)

            def copy(t, k, block, to, src=None):
                dst = land_refs[t].at[4 * block[0] + 2 * block[1] + block[2]]
                return pltpu.make_async_remote_copy(
                    src_ref=dst if src is None else src, dst_ref=dst,
                    send_sem=send_sems.at[t, k], recv_sem=recv_sems.at[t, k], device_id=to, device_id_type=MESH)

            for cp in own:
                cp.start()
            sends = []
            for t in range(n):
                sends.append(copy(t, 0, (x, y, c), sibling, src=src_refs[t]))
                sends += [copy(t, 1 + j, (x, y, c), (*chip, c), src=src_refs[t]) for j, chip in enumerate(chips)]
            for cp in sends:
                cp.start()
            for t in range(n):
                for j, chip in enumerate(chips):
                    copy(t, 1 + j, (*chip, c), (x, y, c)).wait_recv()
                    passed_on = copy(t, 4 + j, (*chip, c), sibling)
                    passed_on.start()
                    sends.append(passed_on)
            for t in range(n):
                copy(t, 0, sibling, (x, y, c)).wait_recv()
                for j, chip in enumerate(chips):
                    copy(t, 4 + j, (*chip, 1 - c), (x, y, c)).wait_recv()
            for cp in sends:
                cp.wait_send()
            for cp in own:
                cp.wait()
        else:
            for k in range(1, N_DEV):
                peer = (_flip(x, k & 4), _flip(y, k & 2), _flip(c, k & 1))
                pl.semaphore_signal(barrier, inc=1, device_id=peer, device_id_type=MESH)
            pl.semaphore_wait(barrier, N_PEERS)
            for cp in own:
                cp.start()
            copies = [_peer_copy(k, src_refs[t], land_refs[t], send_sems.at[t], recv_sems.at[t], gather)
                      for t in range(n) for k in range(1, N_DEV)]
            for cp in copies:
                cp.start()
            for cp in own:
                cp.wait()
            for cp in copies:
                cp.wait()
        passed = pltpu.make_async_copy(token_in, token_out, local_sems.at[n])
        passed.start()
        passed.wait()

    launch()
    return [ref[...] for ref in land_refs], token_out[...]


def _all_reduce_small(buf, *, name):
    _, rows, lanes = buf.shape

    def body(in_ref, out_ref, parts, send_sems, recv_sems):
        x, y, c = _place()
        my_id = 4 * x + 2 * y + c
        peers = []
        for k in range(1, N_DEV):
            px, py, pc = _flip(x, k & 4), _flip(y, k & 2), _flip(c, k & 1)
            peers.append(((px, py, pc), 4 * px + 2 * py + pc))
        scatter = [pltpu.make_async_remote_copy(
            src_ref=in_ref.at[pid], dst_ref=parts.at[my_id],
            send_sem=send_sems.at[0, k], recv_sem=recv_sems.at[0, k],
            device_id=peer, device_id_type=MESH) for k, (peer, pid) in enumerate(peers)]
        for cp in scatter:
            cp.start()
        parts[my_id] = in_ref[my_id]
        for cp in scatter:
            cp.wait()
        total = parts[0]
        for d in range(1, N_DEV):
            total = total + parts[d]
        out_ref[my_id] = total
        gather = [pltpu.make_async_remote_copy(
            src_ref=out_ref.at[my_id], dst_ref=out_ref.at[my_id],
            send_sem=send_sems.at[1, k], recv_sem=recv_sems.at[1, k],
            device_id=peer, device_id_type=MESH) for k, (peer, pid) in enumerate(peers)]
        for cp in gather:
            cp.start()
        for k, (peer, pid) in enumerate(peers):
            pltpu.make_async_remote_copy(
                src_ref=out_ref.at[pid], dst_ref=out_ref.at[pid],
                send_sem=send_sems.at[1, k], recv_sem=recv_sems.at[1, k],
                device_id=peer, device_id_type=MESH).wait()

    vmem = pl.BlockSpec(memory_space=pltpu.VMEM)
    return pl.pallas_call(
        body, name=name, in_specs=[vmem], out_specs=vmem,
        out_shape=jax.ShapeDtypeStruct(buf.shape, F32),
        scratch_shapes=[pltpu.VMEM(buf.shape, F32),
                        pltpu.SemaphoreType.DMA((2, 7)), pltpu.SemaphoreType.DMA((2, 7))],
        compiler_params=pltpu.CompilerParams(vmem_limit_bytes=VMEM_LIMIT),
    )(buf)


TM = 512
TMM = 1024
TKW = 2048
MIX_TILE = 256
FFN_BLOCK_TILE = 512


def _block_diag(w):
    wg = w.reshape(N_GROUPS, HEADS_PER_GROUP, LRU_HEAD_DIM, LRU_HEAD_DIM)
    eye = jnp.eye(HEADS_PER_GROUP, dtype=w.dtype)
    bd = wg[:, :, :, None, :] * eye[None, :, None, :, None]
    return bd.reshape(N_GROUPS, LRU_GROUP, LRU_GROUP).astype(BF16)


def _head_blocks(bd):
    b5 = bd.reshape(N_GROUPS, HEADS_PER_GROUP, LRU_HEAD_DIM, HEADS_PER_GROUP, LRU_HEAD_DIM)
    blocks = [b5[:, h, :, h, :] for h in range(HEADS_PER_GROUP)]
    return jnp.stack(blocks, axis=1).reshape(LRU_HEADS, LRU_HEAD_DIM, LRU_HEAD_DIM)


def _w(lw, key, after):
    value = lw[key]
    return value(after) if callable(value) else value


def _layer_fwd(x, lw, tag):
    sv_rows = x.shape[0]
    z, h1 = _norm_in_proj(x, lw["g1"], _w(lw, "w_in_t", x), tm=min(TMM, sv_rows), tn=896, name=f"in_proj_{tag}")
    y_mix, hs = _mixer_fwd(z, _w(lw, "cw", z), lw["cb"], lw["wa_bd"], lw["wx_bd"], lw["ba"], lw["bx"], lw["lam"],
                           _w(lw, "scw", z), tile=MIX_TILE, name=f"mixer_fwd_{tag}")
    x2 = _mm_nn(y_mix, _w(lw, "w_out", y_mix), tm=min(TMM, sv_rows), tn=D_MODEL, tk=D_MIX, out_dtype=F32, name=f"out_proj_{tag}",
                residual=x)
    x3, h2, act, p, u = _ffn_block_fwd(x2, lw["g2"], _w(lw, "w_up_b", x2), _w(lw, "fcw", x2), _w(lw, "w_down", x2),
                                       tile=min(FFN_BLOCK_TILE, sv_rows), name=f"ffn_fwd_{tag}")
    saved = dict(x=x, h1=h1, z=z, y_mix=y_mix, hs=hs, x2=x2, h2=h2, p=p, u=u, act=act)
    return x3, saved


def _layer_bwd(dx3, dx3b, lw, sv, tag, put):
    sv_rows = dx3.shape[0]
    w_in_t, w_out, w_up_b, w_down = (_w(lw, k, dx3) for k in ("w_in_t", "w_out", "w_up_b", "w_down"))
    cw, scw, fcw = (_w(lw, k, dx3) for k in ("cw", "scw", "fcw"))
    g_down = _mm_tn(sv["act"], dx3b, tm=1024, tn=D_MODEL, tk=min(TKW, sv_rows), out_dtype=BF16, name=f"down_bwd_w_{tag}")
    dx2, dx2b, dg2, dp, dfcw = _ffn_block_bwd(dx3, dx3b, sv["p"], sv["u"], sv["x2"], lw["g2"], w_up_b, fcw, w_down,
                                              tile=min(FFN_BLOCK_TILE, sv_rows), name=f"ffn_bwd_{tag}",
                                              token=put("w_down", g_down))
    g_up = _mm_up_bwd_w(sv["h2"], dp, tm=D_MODEL, tk=min(TKW, sv_rows), name=f"up_bwd_w_{tag}")
    dy = _mm_nt(dx2b, w_out, tm=min(TMM, sv_rows), tn=768, tk=D_MODEL, out_dtype=F32, name=f"out_bwd_x_{tag}")
    dz, dcw, dvec, dwa, dwx, dscw = _mixer_bwd_rows(
        sv["z"], sv["hs"], dy, cw, lw["cb"], lw["wa_bd"], lw["wx_bd"], lw["ba"], lw["bx"], lw["lam"],
        scw, tile=MIX_TILE, name=f"mixer_bwd_{tag}", token=put("w_up_b", g_up))
    g_out = _mm_tn(sv["y_mix"], dx2b, tm=768, tn=D_MODEL, tk=min(TKW, sv_rows), out_dtype=BF16, name=f"out_bwd_w_{tag}",
                   token=dz)
    g_in_t = _mm_tn(dz, sv["h1"], tm=896, tn=D_MODEL, tk=min(TKW, sv_rows), out_dtype=BF16, name=f"in_bwd_w_{tag}",
                    token=put("w_out", g_out))
    dx, dxb, dg1 = _in_bwd_norm(dz, w_in_t, sv["x"], lw["g1"], dx2, tm=TM, tk=896, name=f"in_bwd_x_{tag}",
                                token=put("w_in_t", g_in_t))
    small = dict(norm1_g=dg1[0], lru_conv_w=dcw[0:4], lru_conv_b=dvec[0], lru_wa=_head_blocks(dwa),
                 lru_ba=dvec[1], lru_wx=_head_blocks(dwx), lru_bx=dvec[2], lru_lambda=dvec[3],
                 sc_conv_w=dscw[0:3], norm2_g=dg2[0], ffn_conv_w=dfcw[:, 0:3, :])
    return dx, dxb, small


SMALL_ORDER = ("norm1_g", "lru_conv_w", "lru_conv_b", "lru_wa", "lru_ba", "lru_wx", "lru_bx", "lru_lambda",
               "sc_conv_w", "norm2_g", "ffn_conv_w")


def _local_step(x, tgt, layers, final_g, put):
    saved = []
    h = x
    for l in range(DEPTH):
        h, sv = _layer_fwd(h, layers[l], f"l{l}")
        saved.append(sv)
    loss_blk, dx, dxb, dgf = _loss_head(h, final_g, tgt, tm=TM, name="loss_head")
    smalls = [None] * DEPTH
    for l in reversed(range(DEPTH)):
        dx, dxb, smalls[l] = _layer_bwd(dx, dxb, layers[l], saved[l], f"l{l}", functools.partial(put, l))
    return loss_blk[0, 0], dx, smalls, dgf[0]


def kernel(x, norm1_g, w_in, lru_conv_w, lru_conv_b, lru_wa, lru_ba, lru_wx, lru_bx, lru_lambda, sc_conv_w, w_out, norm2_g, w_up, ffn_conv_w, w_down, final_g, loss_target, m_norm1_g, m_w_in, m_lru_conv_w, m_lru_conv_b, m_lru_wa, m_lru_ba, m_lru_wx, m_lru_bx, m_lru_lambda, m_sc_conv_w, m_w_out, m_norm2_g, m_w_up, m_ffn_conv_w, m_w_down, m_final_g, v_norm1_g, v_w_in, v_lru_conv_w, v_lru_conv_b, v_lru_wa, v_lru_ba, v_lru_wx, v_lru_bx, v_lru_lambda, v_sc_conv_w, v_w_out, v_norm2_g, v_w_up, v_ffn_conv_w, v_w_down, v_final_g):
    names = ["norm1_g", "w_in", "lru_conv_w", "lru_conv_b", "lru_wa", "lru_ba", "lru_wx", "lru_bx", "lru_lambda",
             "sc_conv_w", "w_out", "norm2_g", "w_up", "ffn_conv_w", "w_down", "final_g"]
    w = dict(zip(names, [norm1_g, w_in, lru_conv_w, lru_conv_b, lru_wa, lru_ba, lru_wx, lru_bx, lru_lambda,
                         sc_conv_w, w_out, norm2_g, w_up, ffn_conv_w, w_down, final_g]))
    m = dict(zip(names, [m_norm1_g, m_w_in, m_lru_conv_w, m_lru_conv_b, m_lru_wa, m_lru_ba, m_lru_wx, m_lru_bx,
                         m_lru_lambda, m_sc_conv_w, m_w_out, m_norm2_g, m_w_up, m_ffn_conv_w, m_w_down, m_final_g]))
    v = dict(zip(names, [v_norm1_g, v_w_in, v_lru_conv_w, v_lru_conv_b, v_lru_wa, v_lru_ba, v_lru_wx, v_lru_bx,
                         v_lru_lambda, v_sc_conv_w, v_w_out, v_norm2_g, v_w_up, v_ffn_conv_w, v_w_down, v_final_g]))
    my_id = 4 * lax.axis_index("x") + 2 * lax.axis_index("y") + lax.axis_index("c")

    taps = jnp.zeros((DEPTH, 16, 768), F32)
    taps = taps.at[:, 0:4, 0:128].set(lru_conv_w).at[:, 4:7, 0:64].set(sc_conv_w).at[:, 8:11, :].set(ffn_conv_w)
    shards = {}
    for l in range(DEPTH):
        shards[f"w_in_t{l}"] = jnp.swapaxes(w_in[l], 0, 1).astype(BF16)
        if l == 0:
            shards["taps"] = taps.reshape(DEPTH * 16, 768)
        shards[f"w_out{l}"] = w_out[l].astype(BF16)
        shards[f"w_up_b{l}"] = w_up[l].astype(BF16)
        shards[f"w_down{l}"] = w_down[l].astype(BF16)
    ids = iter(range(16))
    got = {}
    chain = [None]
    for group in (("w_in_t0", "taps"), ("w_out0",), ("w_up_b0",), ("w_down0",),
                  ("w_in_t1",), ("w_out1",), ("w_up_b1",), ("w_down1",)):
        lands, chain[0] = _sequencer_copies([shards[k] for k in group], gather=True, name=f"gather_{group[0]}",
                                            collective_id=next(ids), after=None)
        got.update(zip(group, lands))

    def fetch(key, after):
        return got[key]

    def tap_rows(l, lo, hi, width, after):
        tl = fetch("taps", after).reshape(N_DEV, DEPTH, 16, 768)[:, l, lo:hi, 0:width]
        return jnp.transpose(tl, (1, 0, 2)).reshape(hi - lo, N_DEV * width)

    layers = []
    for l in range(DEPTH):
        layers.append(dict(
            g1=norm1_g[l], g2=norm2_g[l], cb=lru_conv_b[l], ba=lru_ba[l], bx=lru_bx[l], lam=lru_lambda[l],
            wa_bd=_block_diag(lru_wa[l]), wx_bd=_block_diag(lru_wx[l]),
            cw=functools.partial(tap_rows, l, 0, 4, 128), scw=functools.partial(tap_rows, l, 4, 7, 64),
            fcw=lambda after, l=l: tap_rows(l, 8, 11, 768, after).reshape(3, 2, D_FF).transpose(1, 0, 2),
            w_in_t=lambda after, l=l: fetch(f"w_in_t{l}", after).reshape(D_IN, D_MODEL),
            w_out=lambda after, l=l: fetch(f"w_out{l}", after).reshape(D_MIX, D_MODEL),
            w_up_b=lambda after, l=l: fetch(f"w_up_b{l}", after),
            w_down=lambda after, l=l: fetch(f"w_down{l}", after).reshape(D_FF, D_MODEL)))

    scatter_handles = {}

    def put(l, key, grad):
        blocks = grad if grad.ndim == 3 else grad.reshape(N_DEV, grad.shape[0] // N_DEV, grad.shape[1])
        (scatter_handles[(l, key)],), chain[0] = _sequencer_copies(
            [blocks], gather=False, name=f"scatter_{key}{l}", collective_id=next(ids), after=chain[0])
        return blocks

    loss_local, dx, smalls, dgf = _local_step(x[0], loss_target[0], layers, final_g, put)

    parts = []
    for l in range(DEPTH):
        for key in ("w_in_t", "w_out", "w_up_b", "w_down"):
            parts.append(scatter_handles[(l, key)])

    flat = [smalls[l][k].reshape(-1) for l in range(DEPTH) for k in SMALL_ORDER] + [dgf.reshape(-1)]
    flat.append(jnp.broadcast_to(loss_local, (128,)))
    sizes = [f.shape[0] for f in flat]
    total = sum(sizes)
    rows = -(-total // (N_DEV * 128 * 8)) * 8
    flat.append(jnp.zeros((N_DEV * rows * 128 - total,), F32))
    small_sum = _all_reduce_small(jnp.concatenate(flat).reshape(N_DEV, rows, 128), name="reduce_small").reshape(-1)
    small_g, off = [], 0
    for sz in sizes:
        small_g.append(small_sum[off:off + sz])
        off += sz
    gs = {}
    for l in range(DEPTH):
        for i, k in enumerate(SMALL_ORDER):
            gs.setdefault(k, []).append(small_g[l * len(SMALL_ORDER) + i])
    g_final = small_g[-2]
    loss = small_g[-1][0]

    grads, deltas, new_m, new_v = {}, {}, {}, {}
    grads["w_in"] = jnp.stack([jnp.swapaxes(_sum_parts(parts[4 * l], name=f"sum_w_in_l{l}"), 0, 1)
                               for l in range(DEPTH)])
    for slot, k in ((1, "w_out"), (2, "w_up"), (3, "w_down")):
        grads[k], deltas[k], new_m[k], new_v[k] = _sum_adamw(
            [parts[4 * l + slot] for l in range(DEPTH)], w[k], m[k], v[k], name=f"adamw_{k}")
    for k in ("norm1_g", "lru_conv_b", "lru_ba", "lru_bx", "lru_lambda", "norm2_g"):
        grads[k] = jnp.stack(gs[k]).reshape(DEPTH, -1)
    for k in ("lru_wa", "lru_wx"):
        grads[k] = jnp.stack(gs[k]).reshape(DEPTH, LRU_HEADS, LRU_HEAD_DIM, LRU_HEAD_DIM)
    grads["final_g"] = g_final
    cw_full = jnp.stack(gs["lru_conv_w"]).reshape(DEPTH, 4, N_DEV, 128)
    grads["lru_conv_w"] = lax.dynamic_index_in_dim(cw_full, my_id, axis=2, keepdims=False)
    scw_full = jnp.stack(gs["sc_conv_w"]).reshape(DEPTH, 3, N_DEV, 64)
    grads["sc_conv_w"] = lax.dynamic_index_in_dim(scw_full, my_id, axis=2, keepdims=False)
    fcw_full = jnp.stack(gs["ffn_conv_w"]).reshape(DEPTH, 2, 3, D_FF).transpose(0, 2, 1, 3).reshape(DEPTH, 3, N_DEV, 768)
    grads["ffn_conv_w"] = lax.dynamic_index_in_dim(fcw_full, my_id, axis=2, keepdims=False)

    for k in names:
        if k in deltas:
            continue
        shape = w[k].shape
        cols = shape[-1]
        as2d = lambda a: a.reshape(-1, cols)
        d, nm, nv = _adamw(as2d(w[k]), as2d(grads[k]), as2d(m[k]), as2d(v[k]), name=f"adamw_{k}")
        deltas[k], new_m[k], new_v[k] = d.reshape(shape), nm.reshape(shape), nv.reshape(shape)

    return (loss, dx[None], *[grads[k] for k in names], *[deltas[k] for k in names],
            *[new_m[k] for k in names], *[new_v[k] for k in names])
```

```python
import functools
import math

import jax
import jax.numpy as jnp
from jax import lax
from jax.experimental import pallas as pl
from jax.experimental.pallas import tpu as pltpu
from jax.experimental.pallas import tpu_sc as plsc

F32 = jnp.float32
BF16 = jnp.bfloat16

N_DEV = 8
DEPTH = 2
D_MODEL = 1024
D_LRU = 1024
D_SC = 512
D_MIX = D_LRU + D_SC
D_IN = 2 * D_LRU + 3 * D_SC
D_FF = 3072
LRU_HEADS = 16
LRU_HEAD_DIM = 64
LRU_GROUP = 256
N_GROUPS = D_LRU // LRU_GROUP
HEADS_PER_GROUP = LRU_GROUP // LRU_HEAD_DIM
RG_C = 8.0
EPS = 1e-6
HALO = 8

ADAM_LR = 0.001
ADAM_B1 = 0.9
ADAM_B2 = 0.999
ADAM_EPS = 1e-08
ADAM_WD = 0.01
ADAM_STEP = 10

GELU_C = math.sqrt(2.0 / math.pi)
GELU_A = 0.044715

VMEM_LIMIT = 56 * 1024 * 1024
MESH = pl.DeviceIdType.MESH


def _params(*sem):
    return pltpu.CompilerParams(dimension_semantics=tuple(sem) if sem else None,
                                vmem_limit_bytes=VMEM_LIMIT)


def _gelu_parts(x):
    x2 = x * x
    t = jnp.tanh(GELU_C * (x + GELU_A * x * x2))
    half = 0.5 * (1.0 + t)
    g = x * half
    dg = half + 0.5 * x * (1.0 - t * t) * (GELU_C * (1.0 + 3.0 * GELU_A * x2))
    return g, dg


def _gelu(x):
    t = jnp.tanh(GELU_C * (x + GELU_A * x * x * x))
    return 0.5 * x * (1.0 + t)


def _sigmoid(x):
    return 0.5 * jnp.tanh(0.5 * x) + 0.5


def _softplus(x):
    e = jnp.exp(-jnp.abs(x))
    u = 1.0 + e
    log1p_e = jnp.where(u == 1.0, e, jnp.log(u) * (e / (u - 1.0)))
    return jnp.maximum(x, 0.0) + log1p_e


def _rms(x):
    ms = jnp.mean(x * x, axis=-1, keepdims=True)
    return lax.rsqrt(ms + EPS)


def _dot(a, b, dims):
    return lax.dot_general(a, b, (dims, ((), ())), preferred_element_type=F32)


NN = ((1,), (0,))
NT = ((1,), (1,))
TN = ((0,), (0,))


def _matmul(a, b, *, dims, grid, a_spec, b_spec, o_spec, out_shape, acc_shape, name,
            residual=None, r_spec=None, token=None):
    nk = grid[2]

    def body(*refs):
        a_ref, b_ref = refs[0], refs[1]
        r_ref = refs[2] if residual is not None else None
        o_ref = refs[2 + (residual is not None) + (token is not None)]
        prod = _dot(a_ref[...].astype(BF16), b_ref[...].astype(BF16), dims)

        def finish(total):
            if r_ref is not None:
                total = total + r_ref[...]
            o_ref[...] = total.astype(o_ref.dtype)

        if nk == 1:
            finish(prod)
            return
        acc_ref = refs[-1]
        k = pl.program_id(2)

        @pl.when(k == 0)
        def _():
            acc_ref[...] = prod

        @pl.when(jnp.logical_and(k > 0, k < nk - 1))
        def _():
            acc_ref[...] += prod

        @pl.when(k == nk - 1)
        def _():
            finish(acc_ref[...] + prod)

    in_specs = [a_spec, b_spec]
    args = [a, b]
    if residual is not None:
        in_specs.append(r_spec)
        args.append(residual)
    if token is not None:
        in_specs.append(pl.BlockSpec(memory_space=pl.ANY))
        args.append(token)
    return pl.pallas_call(
        body, name=name, grid=grid, in_specs=in_specs, out_specs=o_spec, out_shape=out_shape,
        scratch_shapes=[pltpu.VMEM(acc_shape, F32)] if nk > 1 else [],
        compiler_params=_params("parallel", "parallel", "arbitrary"),
    )(*args)


def _mm_nn(a, b, *, tm, tn, tk, out_dtype, name, residual=None, token=None):
    m, kd = a.shape
    n = b.shape[1]
    return _matmul(
        a, b, dims=NN, grid=(m // tm, n // tn, kd // tk),
        a_spec=pl.BlockSpec((tm, tk), lambda i, j, k: (i, k)),
        b_spec=pl.BlockSpec((tk, tn), lambda i, j, k: (k, j)),
        o_spec=pl.BlockSpec((tm, tn), lambda i, j, k: (i, j)),
        out_shape=jax.ShapeDtypeStruct((m, n), out_dtype), acc_shape=(tm, tn), name=name,
        residual=residual, r_spec=pl.BlockSpec((tm, tn), lambda i, j, k: (i, j)), token=token)


def _mm_nt(a, b, *, tm, tn, tk, out_dtype, name):
    m, kd = a.shape
    n = b.shape[0]
    return _matmul(
        a, b, dims=NT, grid=(m // tm, n // tn, kd // tk),
        a_spec=pl.BlockSpec((tm, tk), lambda i, j, k: (i, k)),
        b_spec=pl.BlockSpec((tn, tk), lambda i, j, k: (j, k)),
        o_spec=pl.BlockSpec((tm, tn), lambda i, j, k: (i, j)),
        out_shape=jax.ShapeDtypeStruct((m, n), out_dtype), acc_shape=(tm, tn), name=name)


def _mm_tn(a, b, *, tm, tn, tk, out_dtype, name, token=None):
    kd, m = a.shape
    n = b.shape[1]
    return _matmul(
        a, b, dims=TN, grid=(m // tm, n // tn, kd // tk),
        a_spec=pl.BlockSpec((tk, tm), lambda i, j, k: (k, i)),
        b_spec=pl.BlockSpec((tk, tn), lambda i, j, k: (k, j)),
        o_spec=pl.BlockSpec((tm, tn), lambda i, j, k: (i, j)),
        out_shape=jax.ShapeDtypeStruct((m, n), out_dtype), acc_shape=(tm, tn), name=name, token=token)


def _mm_up_bwd_w(h2, dp, *, tm, tk, name):
    s = h2.shape[0]
    nb = D_FF * 2 // N_DEV
    per_half = D_FF // nb
    return _matmul(
        h2, dp, dims=TN, grid=(D_MODEL // tm, N_DEV, s // tk),
        a_spec=pl.BlockSpec((tk, tm), lambda i, j, k: (k, i)),
        b_spec=pl.BlockSpec((None, tk, nb), lambda i, j, k: (j // per_half, k, j % per_half)),
        o_spec=pl.BlockSpec((None, tm, nb), lambda i, j, k: (j, i, 0)),
        out_shape=jax.ShapeDtypeStruct((N_DEV, D_MODEL, nb), BF16), acc_shape=(tm, nb), name=name)


def _behind(token):
    return jnp.zeros((8, 128), F32) if token is None else token


def _norm_in_proj(x, g, w_in_t, *, tm, tn, name):
    s, d = x.shape
    n = w_in_t.shape[0]

    def body(x_ref, g_ref, w_ref, z_ref, h_ref):
        @pl.when(pl.program_id(1) == 0)
        def _():
            xv = x_ref[...]
            h_ref[...] = (xv * _rms(xv) * g_ref[...]).astype(BF16)

        z_ref[...] = _dot(h_ref[...], w_ref[...], NT)

    return pl.pallas_call(
        body, name=name, grid=(s // tm, n // tn),
        in_specs=[pl.BlockSpec((tm, d), lambda i, j: (i, 0)), pl.BlockSpec((1, d), lambda i, j: (0, 0)),
                  pl.BlockSpec((tn, d), lambda i, j: (j, 0))],
        out_specs=[pl.BlockSpec((tm, tn), lambda i, j: (i, j)), pl.BlockSpec((tm, d), lambda i, j: (i, 0))],
        out_shape=[jax.ShapeDtypeStruct((s, n), F32), jax.ShapeDtypeStruct((s, d), BF16)],
        compiler_params=_params("parallel", "arbitrary"),
    )(x, g.reshape(1, d), w_in_t)


def _in_bwd_norm(dz, w_in_t, x, g, dres, *, tm, tk, name, token=None):
    s, kd = dz.shape
    d = w_in_t.shape[1]
    nk = kd // tk

    def body(dz_ref, w_ref, x_ref, g_ref, dres_ref, token_ref, dx_ref, dxb_ref, dg_ref, acc_ref):
        i = pl.program_id(0)
        k = pl.program_id(1)

        @pl.when(jnp.logical_and(i == 0, k == 0))
        def _():
            dg_ref[...] = jnp.zeros_like(dg_ref)

        acc_ref[...] = _dot(dz_ref[...], w_ref[...], NN) + jnp.where(k > 0, acc_ref[...], 0.0)

        @pl.when(k == nk - 1)
        def _():
            dh = acc_ref[...]
            xv = x_ref[...]
            rstd = _rms(xv)
            n = xv * rstd
            dn = dh * g_ref[...]
            dx = dres_ref[...] + rstd * (dn - n * jnp.mean(dn * n, axis=-1, keepdims=True))
            dx_ref[...] = dx
            dxb_ref[...] = dx.astype(BF16)
            dg_ref[0:1, :] += jnp.sum(dh * n, axis=0, keepdims=True)

    row = pl.BlockSpec((tm, d), lambda i, k: (i, 0))
    return pl.pallas_call(
        body, name=name, grid=(s // tm, nk),
        in_specs=[pl.BlockSpec((tm, tk), lambda i, k: (i, k)), pl.BlockSpec((tk, d), lambda i, k: (k, 0)),
                  row, pl.BlockSpec((1, d), lambda i, k: (0, 0)), row, pl.BlockSpec(memory_space=pl.ANY)],
        out_specs=[row, row, pl.BlockSpec((8, d), lambda i, k: (0, 0))],
        out_shape=[jax.ShapeDtypeStruct((s, d), F32), jax.ShapeDtypeStruct((s, d), BF16),
                   jax.ShapeDtypeStruct((8, d), F32)],
        scratch_shapes=[pltpu.VMEM((tm, d), F32)],
        compiler_params=_params("arbitrary", "arbitrary"),
    )(dz, w_in_t, x, g.reshape(1, d), dres, _behind(token))


def _loss_head(x, g, tgt, *, tm, name):
    s, d = x.shape

    def body(x_ref, g_ref, t_ref, loss_ref, dx_ref, dxb_ref, dg_ref):
        @pl.when(pl.program_id(0) == 0)
        def _():
            dg_ref[...] = jnp.zeros_like(dg_ref)
            loss_ref[...] = jnp.zeros_like(loss_ref)

        xv = x_ref[...]
        gv = g_ref[...]
        rstd = _rms(xv)
        n = xv * rstd
        e = n * gv - t_ref[...]
        part = 0.5 * jnp.sum(jnp.mean(e * e, axis=-1, keepdims=True), axis=0, keepdims=True)
        loss_ref[...] += jnp.broadcast_to(part, loss_ref.shape)
        dy = e * (1.0 / d)
        dn = dy * gv
        dx = rstd * (dn - n * jnp.mean(dn * n, axis=-1, keepdims=True))
        dx_ref[...] = dx
        dxb_ref[...] = dx.astype(BF16)
        dg_ref[0:1, :] += jnp.sum(dy * n, axis=0, keepdims=True)

    return pl.pallas_call(
        body, name=name, grid=(s // tm,),
        in_specs=[pl.BlockSpec((tm, d), lambda i: (i, 0)), pl.BlockSpec((1, d), lambda i: (0, 0)),
                  pl.BlockSpec((tm, d), lambda i: (i, 0))],
        out_specs=[pl.BlockSpec((8, 128), lambda i: (0, 0)), pl.BlockSpec((tm, d), lambda i: (i, 0)),
                   pl.BlockSpec((tm, d), lambda i: (i, 0)), pl.BlockSpec((8, d), lambda i: (0, 0))],
        out_shape=[jax.ShapeDtypeStruct((8, 128), F32), jax.ShapeDtypeStruct((s, d), F32),
                   jax.ShapeDtypeStruct((s, d), BF16), jax.ShapeDtypeStruct((8, d), F32)],
        compiler_params=_params("arbitrary"),
    )(x, g.reshape(1, d), tgt)


def _scan_rows(a_ref, b_ref, h_ref, carry, *, rows, reverse):
    width = a_ref.shape[1]
    n_chunks = rows // 8
    row = lax.broadcasted_iota(jnp.int32, (8, width), 0)

    def step(ci, carry):
        chunk = (n_chunks - 1 - ci) if reverse else ci
        off = pl.multiple_of(chunk * 8, 8)
        av = a_ref[pl.ds(off, 8), :]
        bv = b_ref[pl.ds(off, 8), :]
        for sh in (1, 2, 4):
            if reverse:
                a_sh = pltpu.roll(av, 8 - sh, 0)
                b_sh = pltpu.roll(bv, 8 - sh, 0)
                m = row < 8 - sh
            else:
                a_sh = pltpu.roll(av, sh, 0)
                b_sh = pltpu.roll(bv, sh, 0)
                m = row >= sh
            bv = jnp.where(m, av * b_sh + bv, bv)
            av = jnp.where(m, av * a_sh, av)
        h = av * carry + bv
        h_ref[pl.ds(off, 8), :] = h
        return h[0:1, :] if reverse else h[7:8, :]

    return lax.fori_loop(0, n_chunks, step, carry)


P_CB, P_BA, P_BX, P_DECAY, P_DSP, N_PAR = 4, 5, 6, 7, 8, 9


def _spread_mixer_params(par, par_sc, cw_ref, cb_ref, ba_ref, bx_ref, lam_ref, scw_ref):
    rows = par.shape[1:]
    for k in range(4):
        par[k] = jnp.broadcast_to(cw_ref[k:k + 1, :], rows)
    par[P_CB] = jnp.broadcast_to(cb_ref[...], rows)
    par[P_BA] = jnp.broadcast_to(ba_ref[...], rows)
    par[P_BX] = jnp.broadcast_to(bx_ref[...], rows)
    par[P_DECAY] = jnp.broadcast_to(-RG_C * _softplus(-lam_ref[...]), rows)
    par[P_DSP] = jnp.broadcast_to(-_sigmoid(-lam_ref[...]), rows)
    for k in range(3):
        par_sc[k] = jnp.broadcast_to(scw_ref[k:k + 1, :], par_sc.shape[1:])


def _gates_rows(pre_r, pre_i, par):
    r = _sigmoid(pre_r + par[P_BA])
    ig = _sigmoid(pre_i + par[P_BX])
    log_a = r * par[P_DECAY]
    a = jnp.exp(log_a)
    one_minus_a2 = -jnp.tanh(log_a) * (a * a + 1.0)
    return r, ig, a, jnp.sqrt(one_minus_a2), one_minus_a2


def _mixer_fwd(z, cw, cb, wa_bd, wx_bd, ba, bx, lam, scw, *, tile, name):
    s = z.shape[0]
    n_tiles = s // tile

    rows_of = lambda r0: slice(r0, r0 + FFN_ROWS)
    col_gate, col_sb, col_sc, col_sx = (slice(D_LRU, 2 * D_LRU), slice(2 * D_LRU, 2 * D_LRU + D_SC),
                                        slice(2 * D_LRU + D_SC, 2 * D_LRU + 2 * D_SC), slice(2 * D_LRU + 2 * D_SC, D_IN))

    def body(z_ref, cw_ref, cb_ref, wa_ref, wx_ref, ba_ref, bx_ref, lam_ref, scw_ref,
             y_ref, hs_ref, par, par_sc, lx_s, lxb_s, a_s, b_s, car_lx, car_q, h_car):
        i = pl.program_id(0)

        @pl.when(i == 0)
        def _():
            car_lx[...] = jnp.zeros_like(car_lx)
            car_q[...] = jnp.zeros_like(car_q)
            h_car[...] = jnp.zeros_like(h_car)
            _spread_mixer_params(par, par_sc, cw_ref, cb_ref, ba_ref, bx_ref, lam_ref, scw_ref)

        before_lx, before_q = car_lx[...], car_q[...]
        for r0 in range(0, tile, FFN_ROWS):
            cur = z_ref[rows_of(r0), 0:D_LRU]
            lx = par[P_CB] + par[3] * cur
            for k in range(3):
                lx = lx + par[k] * _rows_from(before_lx, cur, FFN_ROWS - 3 + k)
            lx_s[rows_of(r0), :] = lx
            lxb_s[rows_of(r0), :] = lx.astype(BF16)
            before_lx = cur
            q = z_ref[rows_of(r0), col_sc] * z_ref[rows_of(r0), col_sx]
            cq = par_sc[2] * q
            for k in range(2):
                cq = cq + par_sc[k] * _rows_from(before_q, q, FFN_ROWS - 2 + k)
            y_ref[rows_of(r0), D_LRU:D_MIX] = (z_ref[rows_of(r0), col_sb] * cq).astype(BF16)
            before_q = q
        car_lx[...] = before_lx
        car_q[...] = before_q

        for g in range(N_GROUPS):
            cols = slice(g * LRU_GROUP, (g + 1) * LRU_GROUP)
            a_s[:, cols] = _dot(lxb_s[:, cols], wa_ref[g], NN)
            b_s[:, cols] = _dot(lxb_s[:, cols], wx_ref[g], NN)

        for r0 in range(0, tile, FFN_ROWS):
            _, ig, a, mult, _ = _gates_rows(a_s[rows_of(r0), :], b_s[rows_of(r0), :], par)
            a_s[rows_of(r0), :] = a
            b_s[rows_of(r0), :] = mult * (ig * lx_s[rows_of(r0), :])
        h_car[0:1, :] = _scan_rows(a_s, b_s, hs_ref, h_car[0:1, :], rows=tile, reverse=False)

        for r0 in range(0, tile, FFN_ROWS):
            y_ref[rows_of(r0), 0:D_LRU] = (hs_ref[rows_of(r0), :] * _gelu(z_ref[rows_of(r0), col_gate])).astype(BF16)

    full = lambda shape: pl.BlockSpec(shape, lambda i: (0,) * len(shape))
    return pl.pallas_call(
        body, name=name, grid=(n_tiles,),
        in_specs=[pl.BlockSpec((tile, D_IN), lambda i: (i, 0)),
                  full((4, D_LRU)), full((1, D_LRU)),
                  full((N_GROUPS, LRU_GROUP, LRU_GROUP)), full((N_GROUPS, LRU_GROUP, LRU_GROUP)),
                  full((1, D_LRU)), full((1, D_LRU)), full((1, D_LRU)), full((3, D_SC))],
        out_specs=[pl.BlockSpec((tile, D_MIX), lambda i: (i, 0)), pl.BlockSpec((tile, D_LRU), lambda i: (i, 0))],
        out_shape=[jax.ShapeDtypeStruct((s, D_MIX), BF16), jax.ShapeDtypeStruct((s, D_LRU), F32)],
        scratch_shapes=[pltpu.VMEM((N_PAR, FFN_ROWS, D_LRU), F32), pltpu.VMEM((3, FFN_ROWS, D_SC), F32),
                        pltpu.VMEM((tile, D_LRU), F32), pltpu.VMEM((tile, D_LRU), BF16),
                        pltpu.VMEM((tile, D_LRU), F32), pltpu.VMEM((tile, D_LRU), F32),
                        pltpu.VMEM((FFN_ROWS, D_LRU), F32), pltpu.VMEM((FFN_ROWS, D_SC), F32),
                        pltpu.VMEM((8, D_LRU), F32)],
        compiler_params=_params("arbitrary"),
    )(z, cw, cb.reshape(1, -1), wa_bd, wx_bd, ba.reshape(1, -1), bx.reshape(1, -1), lam.reshape(1, -1), scw)


def _fold8(x):
    return sum(x[q:q + 8] for q in range(0, x.shape[0], 8))


def _mixer_bwd_rows(z, hs, dy, cw, cb, wa_bd, wx_bd, ba, bx, lam, scw, *, tile, name, token=None):
    s = z.shape[0]
    n_tiles = s // tile
    per8 = tile // 8
    rows_of = lambda r0: slice(r0, r0 + FFN_ROWS)
    col_gate, col_sb, col_sc, col_sx = (slice(D_LRU, 2 * D_LRU), slice(2 * D_LRU, 2 * D_LRU + D_SC),
                                        slice(2 * D_LRU + D_SC, 2 * D_LRU + 2 * D_SC), slice(2 * D_LRU + 2 * D_SC, D_IN))
    up = range(0, tile, FFN_ROWS)
    down = range(tile - FFN_ROWS, -1, -FFN_ROWS)
    A_CB, A_BA, A_BX, A_SP, A_CW = 0, 1, 2, 3, 4

    def body(z_ref, zp_ref, hs_ref, hsp_ref, dy_ref, cw_ref, cb_ref, wa_ref, wx_ref, ba_ref, bx_ref, lam_ref, scw_ref,
             token_ref, dz_ref, dcw_ref, dvec_ref, dwa_ref, dwx_ref, dscw_ref,
             par, par_sc, lx_s, lxb_s, cq_s, pr_s, pi_s, r_s, ig_s, a_s, mult_s, inv_s, ash_s, b_s, lam_s,
             dlx_s, dpr_b, dpi_b, back_s, car_a, car_dlx, car_dcq, l_car, acc, acc_sc):
        i = pl.program_id(0)

        @pl.when(i == 0)
        def _():
            for ref in (dwa_ref, dwx_ref, l_car, car_a, car_dlx, car_dcq, acc, acc_sc):
                ref[...] = jnp.zeros_like(ref)
            _spread_mixer_params(par, par_sc, cw_ref, cb_ref, ba_ref, bx_ref, lam_ref, scw_ref)

        keep = jnp.where(i == n_tiles - 1, 0.0, 1.0)
        zeros8 = lambda n: jnp.zeros((8, n), F32)

        before_lx = jnp.concatenate([zeros8(D_LRU), zp_ref[:, 0:D_LRU] * keep], axis=0)
        before_q = jnp.concatenate([zeros8(D_SC), zp_ref[:, col_sc] * zp_ref[:, col_sx] * keep], axis=0)
        for r0 in up:
            cur = z_ref[rows_of(r0), 0:D_LRU]
            lx = par[P_CB] + par[3] * cur
            for k in range(3):
                lx = lx + par[k] * _rows_from(before_lx, cur, FFN_ROWS - 3 + k)
            lx_s[rows_of(r0), :] = lx
            lxb_s[rows_of(r0), :] = lx.astype(BF16)
            before_lx = cur
            q = z_ref[rows_of(r0), col_sc] * z_ref[rows_of(r0), col_sx]
            cq = par_sc[2] * q
            for k in range(2):
                cq = cq + par_sc[k] * _rows_from(before_q, q, FFN_ROWS - 2 + k)
            cq_s[rows_of(r0), :] = cq
            before_q = q

        for g in range(N_GROUPS):
            cols = slice(g * LRU_GROUP, (g + 1) * LRU_GROUP)
            pr_s[:, cols] = _dot(lxb_s[:, cols], wa_ref[g], NN)
            pi_s[:, cols] = _dot(lxb_s[:, cols], wx_ref[g], NN)

        after_a = car_a[...]
        for r0 in down:
            r, ig, a, mult, one_minus_a2 = _gates_rows(pr_s[rows_of(r0), :], pi_s[rows_of(r0), :], par)
            r_s[rows_of(r0), :] = r
            ig_s[rows_of(r0), :] = ig
            a_s[rows_of(r0), :] = a
            mult_s[rows_of(r0), :] = mult
            inv_s[rows_of(r0), :] = lax.rsqrt(one_minus_a2)
            ash_s[rows_of(r0), :] = _rows_from(a, after_a, 1)
            after_a = a
            ge, dge = _gelu_parts(z_ref[rows_of(r0), col_gate])
            dy_lru = dy_ref[rows_of(r0), 0:D_LRU]
            dz_ref[rows_of(r0), col_gate] = (dy_lru * hs_ref[rows_of(r0), :] * dge).astype(BF16)
            b_s[rows_of(r0), :] = dy_lru * ge
        car_a[...] = after_a
        l_car[0:1, :] = _scan_rows(ash_s, b_s, lam_s, l_car[0:1, :], rows=tile, reverse=True)

        before_h = jnp.concatenate([zeros8(D_LRU), hsp_ref[...] * keep], axis=0)
        for r0 in up:
            lv = lam_s[rows_of(r0), :]
            h_here = hs_ref[rows_of(r0), :]
            lx, r, ig, a = lx_s[rows_of(r0), :], r_s[rows_of(r0), :], ig_s[rows_of(r0), :], a_s[rows_of(r0), :]
            mult = mult_s[rows_of(r0), :]
            da = lv * _rows_from(before_h, h_here, FFN_ROWS - 1)
            before_h = h_here
            d_mult = lv * ig * lx
            d_i = lv * mult * lx
            dlx_s[rows_of(r0), :] = lv * mult * ig
            dlog_a = da * a - d_mult * (a * a) * inv_s[rows_of(r0), :]
            dpre_r = dlog_a * par[P_DECAY] * r * (1.0 - r)
            dpre_i = d_i * ig * (1.0 - ig)
            acc[A_BA] += _fold8(dpre_r)
            acc[A_BX] += _fold8(dpre_i)
            acc[A_SP] += _fold8(dlog_a * r)
            dpr_b[rows_of(r0), :] = dpre_r.astype(BF16)
            dpi_b[rows_of(r0), :] = dpre_i.astype(BF16)

        for g in range(N_GROUPS):
            cols = slice(g * LRU_GROUP, (g + 1) * LRU_GROUP)
            dwa_ref[g] += _dot(lxb_s[:, cols], dpr_b[:, cols], TN)
            dwx_ref[g] += _dot(lxb_s[:, cols], dpi_b[:, cols], TN)
            back_s[:, cols] = _dot(dpr_b[:, cols], wa_ref[g], NT) + _dot(dpi_b[:, cols], wx_ref[g], NT)

        after_dlx, after_dcq = car_dlx[...], car_dcq[...]
        for r0 in down:
            dlx = dlx_s[rows_of(r0), :] + back_s[rows_of(r0), :]
            lxp = z_ref[rows_of(r0), 0:D_LRU]
            acc[A_CB] += _fold8(dlx)
            acc[A_CW + 3] += _fold8(dlx * lxp)
            dlxp = par[3] * dlx
            for sh in range(1, 4):
                below = _rows_from(dlx, after_dlx, sh)
                dlxp = dlxp + par[3 - sh] * below
                acc[A_CW + 3 - sh] += _fold8(below * lxp)
            dz_ref[rows_of(r0), 0:D_LRU] = dlxp.astype(BF16)
            after_dlx = dlx

            dy_sc = dy_ref[rows_of(r0), D_LRU:D_MIX]
            sb, sc, sx = z_ref[rows_of(r0), col_sb], z_ref[rows_of(r0), col_sc], z_ref[rows_of(r0), col_sx]
            dz_ref[rows_of(r0), col_sb] = (dy_sc * cq_s[rows_of(r0), :]).astype(BF16)
            dcq = dy_sc * sb
            q = sc * sx
            acc_sc[2] += _fold8(dcq * q)
            dq = par_sc[2] * dcq
            for sh in range(1, 3):
                below = _rows_from(dcq, after_dcq, sh)
                dq = dq + par_sc[2 - sh] * below
                acc_sc[2 - sh] += _fold8(below * q)
            dz_ref[rows_of(r0), col_sc] = (dq * sx).astype(BF16)
            dz_ref[rows_of(r0), col_sx] = (dq * sc).astype(BF16)
            after_dcq = dcq
        car_dlx[...] = after_dlx
        car_dcq[...] = after_dcq

        @pl.when(i == n_tiles - 1)
        def _():
            total = lambda x: jnp.sum(x, axis=0, keepdims=True)
            dcw_ref[...] = jnp.zeros_like(dcw_ref)
            dvec_ref[...] = jnp.zeros_like(dvec_ref)
            dscw_ref[...] = jnp.zeros_like(dscw_ref)
            for k in range(4):
                dcw_ref[k:k + 1, :] = total(acc[A_CW + k])
            for k in range(3):
                dvec_ref[k:k + 1, :] = total(acc[k])
                dscw_ref[k:k + 1, :] = total(acc_sc[k])
            dvec_ref[3:4, :] = total(acc[A_SP]) * (-RG_C) * par[P_DSP][0:1, :]

    rev = lambda i: n_tiles - 1 - i
    prev8 = lambda i: jnp.maximum(rev(i) * per8 - 1, 0)
    full = lambda shape: pl.BlockSpec(shape, lambda i: (0,) * len(shape))
    wide = lambda rows, dt=F32: pltpu.VMEM((rows, D_LRU), dt)
    return pl.pallas_call(
        body, name=name, grid=(n_tiles,),
        in_specs=[pl.BlockSpec((tile, D_IN), lambda i: (rev(i), 0)),
                  pl.BlockSpec((HALO, D_IN), lambda i: (prev8(i), 0)),
                  pl.BlockSpec((tile, D_LRU), lambda i: (rev(i), 0)),
                  pl.BlockSpec((HALO, D_LRU), lambda i: (prev8(i), 0)),
                  pl.BlockSpec((tile, D_MIX), lambda i: (rev(i), 0)),
                  full((4, D_LRU)), full((1, D_LRU)),
                  full((N_GROUPS, LRU_GROUP, LRU_GROUP)), full((N_GROUPS, LRU_GROUP, LRU_GROUP)),
                  full((1, D_LRU)), full((1, D_LRU)), full((1, D_LRU)), full((3, D_SC)),
                  pl.BlockSpec(memory_space=pl.ANY)],
        out_specs=[pl.BlockSpec((tile, D_IN), lambda i: (rev(i), 0)),
                   full((8, D_LRU)), full((8, D_LRU)),
                   full((N_GROUPS, LRU_GROUP, LRU_GROUP)), full((N_GROUPS, LRU_GROUP, LRU_GROUP)),
                   full((8, D_SC))],
        out_shape=[jax.ShapeDtypeStruct((s, D_IN), BF16),
                   jax.ShapeDtypeStruct((8, D_LRU), F32), jax.ShapeDtypeStruct((8, D_LRU), F32),
                   jax.ShapeDtypeStruct((N_GROUPS, LRU_GROUP, LRU_GROUP), F32),
                   jax.ShapeDtypeStruct((N_GROUPS, LRU_GROUP, LRU_GROUP), F32),
                   jax.ShapeDtypeStruct((8, D_SC), F32)],
        scratch_shapes=[pltpu.VMEM((N_PAR, FFN_ROWS, D_LRU), F32), pltpu.VMEM((3, FFN_ROWS, D_SC), F32),
                        wide(tile), wide(tile, BF16), pltpu.VMEM((tile, D_SC), F32),
                        wide(tile), wide(tile), wide(tile), wide(tile), wide(tile), wide(tile), wide(tile),
                        wide(tile), wide(tile), wide(tile),
                        wide(tile), wide(tile, BF16), wide(tile, BF16), wide(tile),
                        wide(FFN_ROWS), wide(FFN_ROWS), pltpu.VMEM((FFN_ROWS, D_SC), F32), wide(8),
                        pltpu.VMEM((8, 8, D_LRU), F32), pltpu.VMEM((3, 8, D_SC), F32)],
        compiler_params=_params("arbitrary"),
    )(z, z, hs, hs, dy, cw, cb.reshape(1, -1), wa_bd, wx_bd, ba.reshape(1, -1), bx.reshape(1, -1),
      lam.reshape(1, -1), scw, _behind(token))


FFN_ROWS = 16
FFN_GROUPS = 2


def _spread_taps(fw_ref, taps):
    for half in range(2):
        for k in range(3):
            taps[half, k] = jnp.broadcast_to(fw_ref[half, k:k + 1, :], taps.shape[2:])


def _rows_from(first, second, start):
    stack = jnp.concatenate([first, second], axis=0)
    return pltpu.roll(stack, 2 * FFN_ROWS - start, 0)[0:FFN_ROWS]


def _conv3_rows(taps, ext_ref, half, row):
    before = ext_ref[half, row - FFN_ROWS:row, :]
    here = ext_ref[half, row:row + FFN_ROWS, :]
    acc = taps[half, 2] * here
    for k in range(2):
        acc = acc + taps[half, k] * _rows_from(before, here, FFN_ROWS - 2 + k)
    return acc


HALO_B = 16


def _ffn_block_fwd(x2, g2, w_up_b, fcw, w_down, *, tile, name):
    s = x2.shape[0]
    nb = w_up_b.shape[2]
    blocks = D_FF // nb
    per16 = tile // HALO_B

    def body(x2_ref, x2p_ref, g_ref, wg_ref, wu_ref, fw_ref, wd_ref, x3_ref, h_ref, act_ref, p_ref, u_ref,
             ext_p, acc_ref, taps, lhs):
        i = pl.program_id(0)
        j = pl.program_id(1)
        keep = jnp.where(i == 0, 0.0, 1.0)
        _spread_taps(fw_ref, taps)
        @pl.when(j == 0)
        def _():
            for rows_ref, at in ((x2p_ref, 0), (x2_ref, HALO_B)):
                xv = rows_ref[...]
                lhs[at:at + xv.shape[0], :] = (xv * _rms(xv) * g_ref[...]).astype(BF16)
            h_ref[...] = lhs[HALO_B:HALO_B + tile, :]

        grp = tile // FFN_GROUPS
        for g in range(FFN_GROUPS):
            new = slice(g * grp + (HALO_B if g else 0), (g + 1) * grp + HALO_B)
            for half, w_ref in ((0, wg_ref), (1, wu_ref)):
                pe = _dot(lhs[new, :], w_ref[...], NN)
                if g == 0:
                    ext_p[half, 0:HALO_B, :] = pe[0:HALO_B] * keep
                    ext_p[half, HALO_B:grp + HALO_B, :] = pe[HALO_B:]
                    p_ref[half, 0:grp, :] = pe[HALO_B:].astype(BF16)
                else:
                    ext_p[half, new, :] = pe
                    p_ref[half, g * grp:(g + 1) * grp, :] = pe.astype(BF16)
        for g in range(FFN_GROUPS):
            rows = slice(g * grp, (g + 1) * grp)
            acts = []
            for r0 in range(g * grp, (g + 1) * grp, FFN_ROWS):
                u = [_conv3_rows(taps, ext_p, half, HALO_B + r0) for half in range(2)]
                for half in range(2):
                    u_ref[half, r0:r0 + FFN_ROWS, :] = u[half].astype(BF16)
                acts.append((_gelu(u[0]) * u[1]).astype(BF16))
                act_ref[r0:r0 + FFN_ROWS, :] = acts[-1]
            contrib = _dot(jnp.concatenate(acts, axis=0), wd_ref[...], NN)
            acc_ref[rows, :] = contrib + jnp.where(j > 0, acc_ref[rows, :], 0.0)

        @pl.when(j == blocks - 1)
        def _():
            x3_ref[...] = x2_ref[...] + acc_ref[...]

    return pl.pallas_call(
        body, name=name, grid=(s // tile, blocks),
        in_specs=[pl.BlockSpec((tile, D_MODEL), lambda i, j: (i, 0)),
                  pl.BlockSpec((HALO_B, D_MODEL), lambda i, j: (jnp.maximum(i * per16 - 1, 0), 0)),
                  pl.BlockSpec((1, D_MODEL), lambda i, j: (0, 0)),
                  pl.BlockSpec((None, D_MODEL, nb), lambda i, j: (j, 0, 0)),
                  pl.BlockSpec((None, D_MODEL, nb), lambda i, j: (j + blocks, 0, 0)),
                  pl.BlockSpec((2, 3, nb), lambda i, j: (0, 0, j)),
                  pl.BlockSpec((nb, D_MODEL), lambda i, j: (j, 0))],
        out_specs=[pl.BlockSpec((tile, D_MODEL), lambda i, j: (i, 0)),
                   pl.BlockSpec((tile, D_MODEL), lambda i, j: (i, 0)),
                   pl.BlockSpec((tile, nb), lambda i, j: (i, j)),
                   pl.BlockSpec((2, tile, nb), lambda i, j: (0, i, j)),
                   pl.BlockSpec((2, tile, nb), lambda i, j: (0, i, j))],
        out_shape=[jax.ShapeDtypeStruct((s, D_MODEL), F32), jax.ShapeDtypeStruct((s, D_MODEL), BF16),
                   jax.ShapeDtypeStruct((s, D_FF), BF16),
                   jax.ShapeDtypeStruct((2, s, D_FF), BF16), jax.ShapeDtypeStruct((2, s, D_FF), BF16)],
        scratch_shapes=[pltpu.VMEM((2, tile + HALO_B, nb), F32), pltpu.VMEM((tile, D_MODEL), F32),
                        pltpu.VMEM((2, 3, FFN_ROWS, nb), F32), pltpu.VMEM((tile + HALO_B, D_MODEL), BF16)],
        compiler_params=_params("parallel", "arbitrary"),
    )(x2, x2, g2.reshape(1, -1), w_up_b, w_up_b, fcw, w_down)


def _ffn_block_bwd(dx3, dx3b, p, u, x2, g2, w_up_b, fcw, w_down, *, tile, name, token=None):
    s = x2.shape[0]
    nb = w_up_b.shape[2]
    blocks = D_FF // nb
    n_tiles = s // tile
    per16 = tile // HALO_B
    last16 = s // HALO_B - 1

    def body(dxb_ref, dxbn_ref, wd_ref, p_ref, u_ref, un_ref, fw_ref, wg_ref, wu_ref, x2_ref, g_ref, dx3_ref,
             token_ref, dx2_ref, dx2b_ref, dg_ref, dp_ref, dw_ref, da_s, acc_w, acc_dh, taps):
        i = pl.program_id(0)
        j = pl.program_id(1)

        @pl.when(jnp.logical_and(i == 0, j == 0))
        def _():
            acc_w[...] = jnp.zeros_like(acc_w)
            dg_ref[...] = jnp.zeros_like(dg_ref)

        keep_next = jnp.where(i == n_tiles - 1, 0.0, 1.0)
        _spread_taps(fw_ref, taps)
        lhs = jnp.concatenate([dxb_ref[...], dxbn_ref[...]], axis=0)
        grp = tile // FFN_GROUPS
        for g in reversed(range(FFN_GROUPS)):
            new = slice(g * grp, (g + 1) * grp + (HALO_B if g == FFN_GROUPS - 1 else 0))
            da_s[new, :] = _dot(lhs[new], wd_ref[...], NT)

        def du_rows(da, u_gate, u_up):
            ge, dge = _gelu_parts(u_gate)
            return da * u_up * dge, da * ge

        after = du_rows(da_s[tile:tile + HALO_B, :] * keep_next, un_ref[0].astype(F32), un_ref[1].astype(F32))
        for g in reversed(range(FFN_GROUPS)):
            rows = slice(g * grp, (g + 1) * grp)
            dps = ([], [])
            for r0 in range((g + 1) * grp - FFN_ROWS, g * grp - 1, -FFN_ROWS):
                du = du_rows(da_s[r0:r0 + FFN_ROWS, :], u_ref[0, r0:r0 + FFN_ROWS, :].astype(F32),
                             u_ref[1, r0:r0 + FFN_ROWS, :].astype(F32))
                for half in range(2):
                    below = [du[half], _rows_from(du[half], after[half], 1), _rows_from(du[half], after[half], 2)]
                    acc = taps[half, 2] * below[0]
                    for k in range(2):
                        acc = acc + taps[half, k] * below[2 - k]
                    dps[half].insert(0, acc.astype(BF16))
                    dp_ref[half, r0:r0 + FFN_ROWS, :] = dps[half][0]
                    p_rows = p_ref[half, r0:r0 + FFN_ROWS, :].astype(F32)
                    for k in range(3):
                        prod = below[2 - k] * p_rows
                        acc_w[j, half, k] += sum(prod[q:q + 8] for q in range(0, FFN_ROWS, 8))
                after = du
            contrib = (_dot(jnp.concatenate(dps[0], axis=0), wg_ref[...], NT)
                       + _dot(jnp.concatenate(dps[1], axis=0), wu_ref[...], NT))
            acc_dh[rows, :] = contrib + jnp.where(j > 0, acc_dh[rows, :], 0.0)

        @pl.when(j == blocks - 1)
        def _():
            dh = acc_dh[...]
            xv = x2_ref[...]
            rstd = _rms(xv)
            n = xv * rstd
            dn = dh * g_ref[...]
            dx = dx3_ref[...] + rstd * (dn - n * jnp.mean(dn * n, axis=-1, keepdims=True))
            dx2_ref[...] = dx
            dx2b_ref[...] = dx.astype(BF16)
            dg_ref[0:1, :] += jnp.sum(dh * n, axis=0, keepdims=True)

        @pl.when(jnp.logical_and(i == n_tiles - 1, j == blocks - 1))
        def _():
            dw_ref[...] = jnp.zeros_like(dw_ref)
            for jj in range(blocks):
                for half in range(2):
                    for k in range(3):
                        dw_ref[half, k:k + 1, jj * nb:(jj + 1) * nb] = jnp.sum(acc_w[jj, half, k], axis=0, keepdims=True)

    next16 = lambda i: jnp.minimum((i + 1) * per16, last16)
    return pl.pallas_call(
        body, name=name, grid=(n_tiles, blocks),
        in_specs=[pl.BlockSpec((tile, D_MODEL), lambda i, j: (i, 0)),
                  pl.BlockSpec((HALO_B, D_MODEL), lambda i, j: (next16(i), 0)),
                  pl.BlockSpec((nb, D_MODEL), lambda i, j: (j, 0)),
                  pl.BlockSpec((2, tile, nb), lambda i, j: (0, i, j)),
                  pl.BlockSpec((2, tile, nb), lambda i, j: (0, i, j)),
                  pl.BlockSpec((2, HALO_B, nb), lambda i, j: (0, next16(i), j)),
                  pl.BlockSpec((2, 3, nb), lambda i, j: (0, 0, j)),
                  pl.BlockSpec((None, D_MODEL, nb), lambda i, j: (j, 0, 0)),
                  pl.BlockSpec((None, D_MODEL, nb), lambda i, j: (j + blocks, 0, 0)),
                  pl.BlockSpec((tile, D_MODEL), lambda i, j: (i, 0)),
                  pl.BlockSpec((1, D_MODEL), lambda i, j: (0, 0)),
                  pl.BlockSpec((tile, D_MODEL), lambda i, j: (i, 0)),
                  pl.BlockSpec(memory_space=pl.ANY)],
        out_specs=[pl.BlockSpec((tile, D_MODEL), lambda i, j: (i, 0)),
                   pl.BlockSpec((tile, D_MODEL), lambda i, j: (i, 0)),
                   pl.BlockSpec((8, D_MODEL), lambda i, j: (0, 0)),
                   pl.BlockSpec((2, tile, nb), lambda i, j: (0, i, j)),
                   pl.BlockSpec((2, 8, D_FF), lambda i, j: (0, 0, 0))],
        out_shape=[jax.ShapeDtypeStruct((s, D_MODEL), F32), jax.ShapeDtypeStruct((s, D_MODEL), BF16),
                   jax.ShapeDtypeStruct((8, D_MODEL), F32), jax.ShapeDtypeStruct((2, s, D_FF), BF16),
                   jax.ShapeDtypeStruct((2, 8, D_FF), F32)],
        scratch_shapes=[pltpu.VMEM((tile + HALO_B, nb), F32), pltpu.VMEM((blocks, 2, 3, 8, nb), F32),
                        pltpu.VMEM((tile, D_MODEL), F32), pltpu.VMEM((2, 3, FFN_ROWS, nb), F32)],
        compiler_params=_params("arbitrary", "arbitrary"),
    )(dx3b, dx3b, w_down, p, u, u, fcw, w_up_b, w_up_b, x2, g2.reshape(1, -1), dx3, _behind(token))


def _adamw_math(w, g, m, v):
    m = ADAM_B1 * m + (1.0 - ADAM_B1) * g
    v = ADAM_B2 * v + (1.0 - ADAM_B2) * (g * g)
    m_hat = m / (1.0 - ADAM_B1 ** ADAM_STEP)
    v_hat = v / (1.0 - ADAM_B2 ** ADAM_STEP)
    delta = -ADAM_LR * (m_hat / (jnp.sqrt(v_hat) + ADAM_EPS) + ADAM_WD * w)
    return delta, m, v


def _adamw(w, g, m, v, *, name):
    rows, cols = w.shape
    tr = rows
    for cand in (512, 256, 128, 64, 32, 16, 8):
        if rows % cand == 0 and rows > cand:
            tr = cand
            break

    def body(w_ref, g_ref, m_ref, v_ref, d_ref, nm_ref, nv_ref):
        d, nm, nv = _adamw_math(w_ref[...], g_ref[...], m_ref[...], v_ref[...])
        d_ref[...] = d
        nm_ref[...] = nm
        nv_ref[...] = nv

    spec = pl.BlockSpec((tr, cols), lambda i: (i, 0))
    return pl.pallas_call(
        body, name=name, grid=(rows // tr,), in_specs=[spec] * 4, out_specs=[spec] * 3,
        out_shape=[jax.ShapeDtypeStruct((rows, cols), F32)] * 3,
        compiler_params=_params("parallel"),
    )(w, g, m, v)


def _sum_adamw(parts, w, m, v, *, name):
    depth, rows, cols = w.shape
    tr = rows
    for cand in (256, 128, 64):
        if rows % cand == 0 and rows > cand:
            tr = cand
            break

    def body(*refs):
        part_refs = refs[:depth]
        w_ref, m_ref, v_ref, g_ref, d_ref, nm_ref, nv_ref = refs[depth:]
        layer = pl.program_id(0)
        grad = None
        for k, p_ref in enumerate(part_refs):
            total = p_ref[0].astype(F32)
            for dev in range(1, N_DEV):
                total = total + p_ref[dev].astype(F32)
            grad = total if grad is None else jnp.where(layer == k, total, grad)
        d, nm, nv = _adamw_math(w_ref[...], grad, m_ref[...], v_ref[...])
        g_ref[...] = grad
        d_ref[...] = d
        nm_ref[...] = nm
        nv_ref[...] = nv

    part_spec = lambda k: pl.BlockSpec((N_DEV, tr, cols), lambda l, i: (0, jnp.where(l == k, i, 0), 0))
    spec = pl.BlockSpec((None, tr, cols), lambda l, i: (l, i, 0))
    return pl.pallas_call(
        body, name=name, grid=(depth, rows // tr),
        in_specs=[part_spec(k) for k in range(depth)] + [spec] * 3, out_specs=[spec] * 4,
        out_shape=[jax.ShapeDtypeStruct((depth, rows, cols), F32)] * 4,
        compiler_params=_params("parallel", "parallel"),
    )(*parts, w, m, v)


def _sum_parts(parts, *, name):
    _, rows, cols = parts.shape
    tr = rows
    for cand in (256, 128, 64, 32, 16):
        if rows % cand == 0 and rows > cand:
            tr = cand
            break

    def body(p_ref, o_ref):
        acc = p_ref[0].astype(F32)
        for d in range(1, N_DEV):
            acc = acc + p_ref[d].astype(F32)
        o_ref[...] = acc

    return pl.pallas_call(
        body, name=name, grid=(rows // tr,),
        in_specs=[pl.BlockSpec((N_DEV, tr, cols), lambda i: (0, i, 0))],
        out_specs=pl.BlockSpec((tr, cols), lambda i: (i, 0)),
        out_shape=jax.ShapeDtypeStruct((rows, cols), F32),
        compiler_params=_params("parallel"),
    )(parts)


def _place():
    return lax.axis_index("x"), lax.axis_index("y"), lax.axis_index("c")


def _flip(v, bit):
    return 1 - v if bit else v


N_PEERS = N_DEV - 1


def _peer_copy(k, src_ref, land_ref, send_sem, recv_sem, gather):
    x, y, c = _place()
    my_id = 4 * x + 2 * y + c
    px, py, pc = _flip(x, k & 4), _flip(y, k & 2), _flip(c, k & 1)
    peer_id = 4 * px + 2 * py + pc
    return pltpu.make_async_remote_copy(
        src_ref=src_ref if gather else src_ref.at[peer_id], dst_ref=land_ref.at[my_id],
        send_sem=send_sem.at[k - 1], recv_sem=recv_sem.at[k - 1],
        device_id=(px, py, pc), device_id_type=MESH)


def _sequencer_copies(srcs, *, gather, name, collective_id, after):
    n = len(srcs)
    hbm = pltpu.MemorySpace.HBM
    src_refs = [jax.new_ref(s, memory_space=hbm) for s in srcs]
    land_refs = [jax.empty_ref(jax.ShapeDtypeStruct(((N_DEV,) + s.shape) if gather else s.shape, s.dtype),
                               memory_space=hbm) for s in srcs]
    token_in = jax.new_ref(jnp.zeros((8, 128), F32) if after is None else after, memory_space=hbm)
    token_out = jax.empty_ref(jax.ShapeDtypeStruct((8, 128), F32), memory_space=hbm)

    @pl.kernel(mesh=plsc.ScalarSubcoreMesh(axis_name="seq", num_cores=1), name=name,
               scratch_types=(pltpu.SemaphoreType.DMA((n, N_PEERS)), pltpu.SemaphoreType.DMA((n, N_PEERS)),
                              pltpu.SemaphoreType.DMA((n + 1,))),
               compiler_params=pltpu.CompilerParams(collective_id=collective_id))
    def launch(send_sems, recv_sems, local_sems):
        x, y, c = _place()
        my_id = 4 * x + 2 * y + c
        barrier = pltpu.get_barrier_semaphore()
        own = [pltpu.make_async_copy(src_refs[t] if gather else src_refs[t].at[my_id], land_refs[t].at[my_id],
                                     local_sems.at[t]) for t in range(n)]
        if gather:
            sibling = (x, y, 1 - c)
            chips = [(1 - x, y), (x, 1 - y), (1 - x, 1 - y)]
            for peer in [sibling] + [(*chip, c) for chip in chips]:
                pl.semaphore_signal(barrier, inc=1, device_id=peer, device_id_type=MESH)
            pl.semaphore_wait(barrier, 4)

            def copy(t, k, block, to, src=None):
                dst = land_refs[t].at[4 * block[0] + 2 * block[1] + block[2]]
                return pltpu.make_async_remote_copy(
                    src_ref=dst if src is None else src, dst_ref=dst,
                    send_sem=send_sems.at[t, k], recv_sem=recv_sems.at[t, k], device_id=to, device_id_type=MESH)

            for cp in own:
                cp.start()
            sends = []
            for t in range(n):
                sends.append(copy(t, 0, (x, y, c), sibling, src=src_refs[t]))
                sends += [copy(t, 1 + j, (x, y, c), (*chip, c), src=src_refs[t]) for j, chip in enumerate(chips)]
            for cp in sends:
                cp.start()
            for t in range(n):
                for j, chip in enumerate(chips):
                    copy(t, 1 + j, (*chip, c), (x, y, c)).wait_recv()
                    passed_on = copy(t, 4 + j, (*chip, c), sibling)
                    passed_on.start()
                    sends.append(passed_on)
            for t in range(n):
                copy(t, 0, sibling, (x, y, c)).wait_recv()
                for j, chip in enumerate(chips):
                    copy(t, 4 + j, (*chip, 1 - c), (x, y, c)).wait_recv()
            for cp in sends:
                cp.wait_send()
            for cp in own:
                cp.wait()
        else:
            for k in range(1, N_DEV):
                peer = (_flip(x, k & 4), _flip(y, k & 2), _flip(c, k & 1))
                pl.semaphore_signal(barrier, inc=1, device_id=peer, device_id_type=MESH)
            pl.semaphore_wait(barrier, N_PEERS)
            for cp in own:
                cp.start()
            copies = [_peer_copy(k, src_refs[t], land_refs[t], send_sems.at[t], recv_sems.at[t], gather)
                      for t in range(n) for k in range(1, N_DEV)]
            for cp in copies:
                cp.start()
            for cp in own:
                cp.wait()
            for cp in copies:
                cp.wait()
        passed = pltpu.make_async_copy(token_in, token_out, local_sems.at[n])
        passed.start()
        passed.wait()

    launch()
    return [ref[...] for ref in land_refs], token_out[...]


def _all_reduce_small(buf, *, name):
    _, rows, lanes = buf.shape

    def body(in_ref, out_ref, parts, send_sems, recv_sems):
        x, y, c = _place()
        my_id = 4 * x + 2 * y + c
        peers = []
        for k in range(1, N_DEV):
            px, py, pc = _flip(x, k & 4), _flip(y, k & 2), _flip(c, k & 1)
            peers.append(((px, py, pc), 4 * px + 2 * py + pc))
        scatter = [pltpu.make_async_remote_copy(
            src_ref=in_ref.at[pid], dst_ref=parts.at[my_id],
            send_sem=send_sems.at[0, k], recv_sem=recv_sems.at[0, k],
            device_id=peer, device_id_type=MESH) for k, (peer, pid) in enumerate(peers)]
        for cp in scatter:
            cp.start()
        parts[my_id] = in_ref[my_id]
        for cp in scatter:
            cp.wait()
        total = parts[0]
        for d in range(1, N_DEV):
            total = total + parts[d]
        out_ref[my_id] = total
        gather = [pltpu.make_async_remote_copy(
            src_ref=out_ref.at[my_id], dst_ref=out_ref.at[my_id],
            send_sem=send_sems.at[1, k], recv_sem=recv_sems.at[1, k],
            device_id=peer, device_id_type=MESH) for k, (peer, pid) in enumerate(peers)]
        for cp in gather:
            cp.start()
        for k, (peer, pid) in enumerate(peers):
            pltpu.make_async_remote_copy(
                src_ref=out_ref.at[pid], dst_ref=out_ref.at[pid],
                send_sem=send_sems.at[1, k], recv_sem=recv_sems.at[1, k],
                device_id=peer, device_id_type=MESH).wait()

    vmem = pl.BlockSpec(memory_space=pltpu.VMEM)
    return pl.pallas_call(
        body, name=name, in_specs=[vmem], out_specs=vmem,
        out_shape=jax.ShapeDtypeStruct(buf.shape, F32),
        scratch_shapes=[pltpu.VMEM(buf.shape, F32),
                        pltpu.SemaphoreType.DMA((2, 7)), pltpu.SemaphoreType.DMA((2, 7))],
        compiler_params=pltpu.CompilerParams(vmem_limit_bytes=VMEM_LIMIT),
    )(buf)


TM = 512
TMM = 1024
TKW = 2048
MIX_TILE = 256
FFN_FWD_TILE = 512
FFN_BWD_TILE = 512


def _block_diag(w):
    wg = w.reshape(N_GROUPS, HEADS_PER_GROUP, LRU_HEAD_DIM, LRU_HEAD_DIM)
    eye = jnp.eye(HEADS_PER_GROUP, dtype=w.dtype)
    bd = wg[:, :, :, None, :] * eye[None, :, None, :, None]
    return bd.reshape(N_GROUPS, LRU_GROUP, LRU_GROUP).astype(BF16)


def _head_blocks(bd):
    b5 = bd.reshape(N_GROUPS, HEADS_PER_GROUP, LRU_HEAD_DIM, HEADS_PER_GROUP, LRU_HEAD_DIM)
    blocks = [b5[:, h, :, h, :] for h in range(HEADS_PER_GROUP)]
    return jnp.stack(blocks, axis=1).reshape(LRU_HEADS, LRU_HEAD_DIM, LRU_HEAD_DIM)


def _w(lw, key, after):
    value = lw[key]
    return value(after) if callable(value) else value


def _layer_fwd(x, lw, tag):
    sv_rows = x.shape[0]
    z, h1 = _norm_in_proj(x, lw["g1"], _w(lw, "w_in_t", x), tm=min(TMM, sv_rows), tn=896, name=f"in_proj_{tag}")
    y_mix, hs = _mixer_fwd(z, _w(lw, "cw", z), lw["cb"], lw["wa_bd"], lw["wx_bd"], lw["ba"], lw["bx"], lw["lam"],
                           _w(lw, "scw", z), tile=MIX_TILE, name=f"mixer_fwd_{tag}")
    x2 = _mm_nn(y_mix, _w(lw, "w_out", y_mix), tm=min(TMM, sv_rows), tn=D_MODEL, tk=D_MIX, out_dtype=F32, name=f"out_proj_{tag}",
                residual=x)
    x3, h2, act, p, u = _ffn_block_fwd(x2, lw["g2"], _w(lw, "w_up_b", x2), _w(lw, "fcw", x2), _w(lw, "w_down", x2),
                                       tile=min(FFN_FWD_TILE, sv_rows), name=f"ffn_fwd_{tag}")
    saved = dict(x=x, h1=h1, z=z, y_mix=y_mix, hs=hs, x2=x2, h2=h2, p=p, u=u, act=act)
    return x3, saved


def _layer_bwd(dx3, dx3b, lw, sv, tag, put):
    sv_rows = dx3.shape[0]
    w_in_t, w_out, w_up_b, w_down = (_w(lw, k, dx3) for k in ("w_in_t", "w_out", "w_up_b", "w_down"))
    cw, scw, fcw = (_w(lw, k, dx3) for k in ("cw", "scw", "fcw"))
    g_down = _mm_tn(sv["act"], dx3b, tm=1024, tn=D_MODEL, tk=min(TKW, sv_rows), out_dtype=BF16, name=f"down_bwd_w_{tag}")
    dx2, dx2b, dg2, dp, dfcw = _ffn_block_bwd(dx3, dx3b, sv["p"], sv["u"], sv["x2"], lw["g2"], w_up_b, fcw, w_down,
                                              tile=min(FFN_BWD_TILE, sv_rows), name=f"ffn_bwd_{tag}",
                                              token=put("w_down", g_down))
    g_up = _mm_up_bwd_w(sv["h2"], dp, tm=D_MODEL, tk=min(TKW, sv_rows), name=f"up_bwd_w_{tag}")
    dy = _mm_nt(dx2b, w_out, tm=min(TMM, sv_rows), tn=768, tk=D_MODEL, out_dtype=F32, name=f"out_bwd_x_{tag}")
    dz, dcw, dvec, dwa, dwx, dscw = _mixer_bwd_rows(
        sv["z"], sv["hs"], dy, cw, lw["cb"], lw["wa_bd"], lw["wx_bd"], lw["ba"], lw["bx"], lw["lam"],
        scw, tile=MIX_TILE, name=f"mixer_bwd_{tag}", token=put("w_up_b", g_up))
    g_out = _mm_tn(sv["y_mix"], dx2b, tm=768, tn=D_MODEL, tk=min(TKW, sv_rows), out_dtype=BF16, name=f"out_bwd_w_{tag}",
                   token=dz)
    g_in_t = _mm_tn(dz, sv["h1"], tm=896, tn=D_MODEL, tk=min(TKW, sv_rows), out_dtype=BF16, name=f"in_bwd_w_{tag}",
                    token=put("w_out", g_out))
    dx, dxb, dg1 = _in_bwd_norm(dz, w_in_t, sv["x"], lw["g1"], dx2, tm=min(TMM, sv_rows), tk=896, name=f"in_bwd_x_{tag}",
                                token=put("w_in_t", g_in_t))
    small = dict(norm1_g=dg1[0], lru_conv_w=dcw[0:4], lru_conv_b=dvec[0], lru_wa=_head_blocks(dwa),
                 lru_ba=dvec[1], lru_wx=_head_blocks(dwx), lru_bx=dvec[2], lru_lambda=dvec[3],
                 sc_conv_w=dscw[0:3], norm2_g=dg2[0], ffn_conv_w=dfcw[:, 0:3, :])
    return dx, dxb, small


SMALL_ORDER = ("norm1_g", "lru_conv_w", "lru_conv_b", "lru_wa", "lru_ba", "lru_wx", "lru_bx", "lru_lambda",
               "sc_conv_w", "norm2_g", "ffn_conv_w")


def _local_step(x, tgt, layers, final_g, put):
    saved = []
    h = x
    for l in range(DEPTH):
        h, sv = _layer_fwd(h, layers[l], f"l{l}")
        saved.append(sv)
    loss_blk, dx, dxb, dgf = _loss_head(h, final_g, tgt, tm=TM, name="loss_head")
    smalls = [None] * DEPTH
    for l in reversed(range(DEPTH)):
        dx, dxb, smalls[l] = _layer_bwd(dx, dxb, layers[l], saved[l], f"l{l}", functools.partial(put, l))
    return loss_blk[0, 0], dx, smalls, dgf[0]


def kernel(x, norm1_g, w_in, lru_conv_w, lru_conv_b, lru_wa, lru_ba, lru_wx, lru_bx, lru_lambda, sc_conv_w, w_out, norm2_g, w_up, ffn_conv_w, w_down, final_g, loss_target, m_norm1_g, m_w_in, m_lru_conv_w, m_lru_conv_b, m_lru_wa, m_lru_ba, m_lru_wx, m_lru_bx, m_lru_lambda, m_sc_conv_w, m_w_out, m_norm2_g, m_w_up, m_ffn_conv_w, m_w_down, m_final_g, v_norm1_g, v_w_in, v_lru_conv_w, v_lru_conv_b, v_lru_wa, v_lru_ba, v_lru_wx, v_lru_bx, v_lru_lambda, v_sc_conv_w, v_w_out, v_norm2_g, v_w_up, v_ffn_conv_w, v_w_down, v_final_g):
    names = ["norm1_g", "w_in", "lru_conv_w", "lru_conv_b", "lru_wa", "lru_ba", "lru_wx", "lru_bx", "lru_lambda",
             "sc_conv_w", "w_out", "norm2_g", "w_up", "ffn_conv_w", "w_down", "final_g"]
    w = dict(zip(names, [norm1_g, w_in, lru_conv_w, lru_conv_b, lru_wa, lru_ba, lru_wx, lru_bx, lru_lambda,
                         sc_conv_w, w_out, norm2_g, w_up, ffn_conv_w, w_down, final_g]))
    m = dict(zip(names, [m_norm1_g, m_w_in, m_lru_conv_w, m_lru_conv_b, m_lru_wa, m_lru_ba, m_lru_wx, m_lru_bx,
                         m_lru_lambda, m_sc_conv_w, m_w_out, m_norm2_g, m_w_up, m_ffn_conv_w, m_w_down, m_final_g]))
    v = dict(zip(names, [v_norm1_g, v_w_in, v_lru_conv_w, v_lru_conv_b, v_lru_wa, v_lru_ba, v_lru_wx, v_lru_bx,
                         v_lru_lambda, v_sc_conv_w, v_w_out, v_norm2_g, v_w_up, v_ffn_conv_w, v_w_down, v_final_g]))
    my_id = 4 * lax.axis_index("x") + 2 * lax.axis_index("y") + lax.axis_index("c")

    taps = jnp.zeros((DEPTH, 16, 768), F32)
    taps = taps.at[:, 0:4, 0:128].set(lru_conv_w).at[:, 4:7, 0:64].set(sc_conv_w).at[:, 8:11, :].set(ffn_conv_w)
    shards = {}
    for l in range(DEPTH):
        shards[f"w_in_t{l}"] = jnp.swapaxes(w_in[l], 0, 1).astype(BF16)
        if l == 0:
            shards["taps"] = taps.reshape(DEPTH * 16, 768)
        shards[f"w_out{l}"] = w_out[l].astype(BF16)
        shards[f"w_up_b{l}"] = w_up[l].astype(BF16)
        shards[f"w_down{l}"] = w_down[l].astype(BF16)
    ids = iter(range(16))
    got = {}
    chain = [None]
    for group in (("w_in_t0", "taps"), ("w_out0",), ("w_up_b0",), ("w_down0",),
                  ("w_in_t1",), ("w_out1",), ("w_up_b1",), ("w_down1",)):
        lands, chain[0] = _sequencer_copies([shards[k] for k in group], gather=True, name=f"gather_{group[0]}",
                                            collective_id=next(ids), after=None)
        got.update(zip(group, lands))

    def fetch(key, after):
        return got[key]

    def tap_rows(l, lo, hi, width, after):
        tl = fetch("taps", after).reshape(N_DEV, DEPTH, 16, 768)[:, l, lo:hi, 0:width]
        return jnp.transpose(tl, (1, 0, 2)).reshape(hi - lo, N_DEV * width)

    layers = []
    for l in range(DEPTH):
        layers.append(dict(
            g1=norm1_g[l], g2=norm2_g[l], cb=lru_conv_b[l], ba=lru_ba[l], bx=lru_bx[l], lam=lru_lambda[l],
            wa_bd=_block_diag(lru_wa[l]), wx_bd=_block_diag(lru_wx[l]),
            cw=functools.partial(tap_rows, l, 0, 4, 128), scw=functools.partial(tap_rows, l, 4, 7, 64),
            fcw=lambda after, l=l: tap_rows(l, 8, 11, 768, after).reshape(3, 2, D_FF).transpose(1, 0, 2),
            w_in_t=lambda after, l=l: fetch(f"w_in_t{l}", after).reshape(D_IN, D_MODEL),
            w_out=lambda after, l=l: fetch(f"w_out{l}", after).reshape(D_MIX, D_MODEL),
            w_up_b=lambda after, l=l: fetch(f"w_up_b{l}", after),
            w_down=lambda after, l=l: fetch(f"w_down{l}", after).reshape(D_FF, D_MODEL)))

    scatter_handles = {}

    def put(l, key, grad):
        blocks = grad if grad.ndim == 3 else grad.reshape(N_DEV, grad.shape[0] // N_DEV, grad.shape[1])
        (scatter_handles[(l, key)],), chain[0] = _sequencer_copies(
            [blocks], gather=False, name=f"scatter_{key}{l}", collective_id=next(ids), after=chain[0])
        return blocks

    loss_local, dx, smalls, dgf = _local_step(x[0], loss_target[0], layers, final_g, put)

    parts = []
    for l in range(DEPTH):
        for key in ("w_in_t", "w_out", "w_up_b", "w_down"):
            parts.append(scatter_handles[(l, key)])

    flat = [smalls[l][k].reshape(-1) for l in range(DEPTH) for k in SMALL_ORDER] + [dgf.reshape(-1)]
    flat.append(jnp.broadcast_to(loss_local, (128,)))
    sizes = [f.shape[0] for f in flat]
    total = sum(sizes)
    rows = -(-total // (N_DEV * 128 * 8)) * 8
    flat.append(jnp.zeros((N_DEV * rows * 128 - total,), F32))
    small_sum = _all_reduce_small(jnp.concatenate(flat).reshape(N_DEV, rows, 128), name="reduce_small").reshape(-1)
    small_g, off = [], 0
    for sz in sizes:
        small_g.append(small_sum[off:off + sz])
        off += sz
    gs = {}
    for l in range(DEPTH):
        for i, k in enumerate(SMALL_ORDER):
            gs.setdefault(k, []).append(small_g[l * len(SMALL_ORDER) + i])
    g_final = small_g[-2]
    loss = small_g[-1][0]

    grads, deltas, new_m, new_v = {}, {}, {}, {}
    grads["w_in"] = jnp.stack([jnp.swapaxes(_sum_parts(parts[4 * l], name=f"sum_w_in_l{l}"), 0, 1)
                               for l in range(DEPTH)])
    for slot, k in ((1, "w_out"), (2, "w_up"), (3, "w_down")):
        grads[k], deltas[k], new_m[k], new_v[k] = _sum_adamw(
            [parts[4 * l + slot] for l in range(DEPTH)], w[k], m[k], v[k], name=f"adamw_{k}")
    for k in ("norm1_g", "lru_conv_b", "lru_ba", "lru_bx", "lru_lambda", "norm2_g"):
        grads[k] = jnp.stack(gs[k]).reshape(DEPTH, -1)
    for k in ("lru_wa", "lru_wx"):
        grads[k] = jnp.stack(gs[k]).reshape(DEPTH, LRU_HEADS, LRU_HEAD_DIM, LRU_HEAD_DIM)
    grads["final_g"] = g_final
    cw_full = jnp.stack(gs["lru_conv_w"]).reshape(DEPTH, 4, N_DEV, 128)
    grads["lru_conv_w"] = lax.dynamic_index_in_dim(cw_full, my_id, axis=2, keepdims=False)
    scw_full = jnp.stack(gs["sc_conv_w"]).reshape(DEPTH, 3, N_DEV, 64)
    grads["sc_conv_w"] = lax.dynamic_index_in_dim(scw_full, my_id, axis=2, keepdims=False)
    fcw_full = jnp.stack(gs["ffn_conv_w"]).reshape(DEPTH, 2, 3, D_FF).transpose(0, 2, 1, 3).reshape(DEPTH, 3, N_DEV, 768)
    grads["ffn_conv_w"] = lax.dynamic_index_in_dim(fcw_full, my_id, axis=2, keepdims=False)

    for k in names:
        if k in deltas:
            continue
        shape = w[k].shape
        cols = shape[-1]
        as2d = lambda a: a.reshape(-1, cols)
        d, nm, nv = _adamw(as2d(w[k]), as2d(grads[k]), as2d(m[k]), as2d(v[k]), name=f"adamw_{k}")
        deltas[k], new_m[k], new_v[k] = d.reshape(shape), nm.reshape(shape), nv.reshape(shape)

    return (loss, dx[None], *[grads[k] for k in names], *[deltas[k] for k in names],
            *[new_m[k] for k in names], *[new_v[k] for k in names])
```

```python
import functools
import math

import jax
import jax.numpy as jnp
from jax import lax
from jax.experimental import pallas as pl
from jax.experimental.pallas import tpu as pltpu
from jax.experimental.pallas import tpu_sc as plsc

F32 = jnp.float32
BF16 = jnp.bfloat16

N_DEV = 8
DEPTH = 2
D_MODEL = 1024
D_LRU = 1024
D_SC = 512
D_MIX = D_LRU + D_SC
D_IN = 2 * D_LRU + 3 * D_SC
D_FF = 3072
LRU_HEADS = 16
LRU_HEAD_DIM = 64
LRU_GROUP = 256
N_GROUPS = D_LRU // LRU_GROUP
HEADS_PER_GROUP = LRU_GROUP // LRU_HEAD_DIM
RG_C = 8.0
EPS = 1e-6
HALO = 8

ADAM_LR = 0.001
ADAM_B1 = 0.9
ADAM_B2 = 0.999
ADAM_EPS = 1e-08
ADAM_WD = 0.01
ADAM_STEP = 10

GELU_C = math.sqrt(2.0 / math.pi)
GELU_A = 0.044715

VMEM_LIMIT = 56 * 1024 * 1024
MESH = pl.DeviceIdType.MESH


def _params(*sem):
    return pltpu.CompilerParams(dimension_semantics=tuple(sem) if sem else None,
                                vmem_limit_bytes=VMEM_LIMIT)


def _gelu_parts(x):
    x2 = x * x
    t = jnp.tanh(GELU_C * (x + GELU_A * x * x2))
    half = 0.5 * (1.0 + t)
    g = x * half
    dg = half + 0.5 * x * (1.0 - t * t) * (GELU_C * (1.0 + 3.0 * GELU_A * x2))
    return g, dg


def _gelu(x):
    t = jnp.tanh(GELU_C * (x + GELU_A * x * x * x))
    return 0.5 * x * (1.0 + t)


def _sigmoid(x):
    return 0.5 * jnp.tanh(0.5 * x) + 0.5


def _softplus(x):
    e = jnp.exp(-jnp.abs(x))
    u = 1.0 + e
    log1p_e = jnp.where(u == 1.0, e, jnp.log(u) * (e / (u - 1.0)))
    return jnp.maximum(x, 0.0) + log1p_e


def _rms(x):
    ms = jnp.mean(x * x, axis=-1, keepdims=True)
    return lax.rsqrt(ms + EPS)


def _dot(a, b, dims):
    return lax.dot_general(a, b, (dims, ((), ())), preferred_element_type=F32)


NN = ((1,), (0,))
NT = ((1,), (1,))
TN = ((0,), (0,))


def _matmul(a, b, *, dims, grid, a_spec, b_spec, o_spec, out_shape, acc_shape, name,
            residual=None, r_spec=None, token=None):
    nk = grid[2]

    def body(*refs):
        a_ref, b_ref = refs[0], refs[1]
        r_ref = refs[2] if residual is not None else None
        o_ref = refs[2 + (residual is not None) + (token is not None)]
        prod = _dot(a_ref[...].astype(BF16), b_ref[...].astype(BF16), dims)

        def finish(total):
            if r_ref is not None:
                total = total + r_ref[...]
            o_ref[...] = total.astype(o_ref.dtype)

        if nk == 1:
            finish(prod)
            return
        acc_ref = refs[-1]
        k = pl.program_id(2)

        @pl.when(k == 0)
        def _():
            acc_ref[...] = prod

        @pl.when(jnp.logical_and(k > 0, k < nk - 1))
        def _():
            acc_ref[...] += prod

        @pl.when(k == nk - 1)
        def _():
            finish(acc_ref[...] + prod)

    in_specs = [a_spec, b_spec]
    args = [a, b]
    if residual is not None:
        in_specs.append(r_spec)
        args.append(residual)
    if token is not None:
        in_specs.append(pl.BlockSpec(memory_space=pl.ANY))
        args.append(token)
    return pl.pallas_call(
        body, name=name, grid=grid, in_specs=in_specs, out_specs=o_spec, out_shape=out_shape,
        scratch_shapes=[pltpu.VMEM(acc_shape, F32)] if nk > 1 else [],
        compiler_params=_params("parallel", "parallel", "arbitrary"),
    )(*args)


def _mm_nn(a, b, *, tm, tn, tk, out_dtype, name, residual=None, token=None):
    m, kd = a.shape
    n = b.shape[1]
    return _matmul(
        a, b, dims=NN, grid=(m // tm, n // tn, kd // tk),
        a_spec=pl.BlockSpec((tm, tk), lambda i, j, k: (i, k)),
        b_spec=pl.BlockSpec((tk, tn), lambda i, j, k: (k, j)),
        o_spec=pl.BlockSpec((tm, tn), lambda i, j, k: (i, j)),
        out_shape=jax.ShapeDtypeStruct((m, n), out_dtype), acc_shape=(tm, tn), name=name,
        residual=residual, r_spec=pl.BlockSpec((tm, tn), lambda i, j, k: (i, j)), token=token)


def _mm_nt(a, b, *, tm, tn, tk, out_dtype, name):
    m, kd = a.shape
    n = b.shape[0]
    return _matmul(
        a, b, dims=NT, grid=(m // tm, n // tn, kd // tk),
        a_spec=pl.BlockSpec((tm, tk), lambda i, j, k: (i, k)),
        b_spec=pl.BlockSpec((tn, tk), lambda i, j, k: (j, k)),
        o_spec=pl.BlockSpec((tm, tn), lambda i, j, k: (i, j)),
        out_shape=jax.ShapeDtypeStruct((m, n), out_dtype), acc_shape=(tm, tn), name=name)


def _mm_tn(a, b, *, tm, tn, tk, out_dtype, name, token=None):
    kd, m = a.shape
    n = b.shape[1]
    return _matmul(
        a, b, dims=TN, grid=(m // tm, n // tn, kd // tk),
        a_spec=pl.BlockSpec((tk, tm), lambda i, j, k: (k, i)),
        b_spec=pl.BlockSpec((tk, tn), lambda i, j, k: (k, j)),
        o_spec=pl.BlockSpec((tm, tn), lambda i, j, k: (i, j)),
        out_shape=jax.ShapeDtypeStruct((m, n), out_dtype), acc_shape=(tm, tn), name=name, token=token)


def _mm_up_bwd_w(h2, dp, *, tm, tk, name):
    s = h2.shape[0]
    nb = D_FF * 2 // N_DEV
    per_half = D_FF // nb
    return _matmul(
        h2, dp, dims=TN, grid=(D_MODEL // tm, N_DEV, s // tk),
        a_spec=pl.BlockSpec((tk, tm), lambda i, j, k: (k, i)),
        b_spec=pl.BlockSpec((None, tk, nb), lambda i, j, k: (j // per_half, k, j % per_half)),
        o_spec=pl.BlockSpec((None, tm, nb), lambda i, j, k: (j, i, 0)),
        out_shape=jax.ShapeDtypeStruct((N_DEV, D_MODEL, nb), BF16), acc_shape=(tm, nb), name=name)


def _behind(token):
    return jnp.zeros((8, 128), F32) if token is None else token


def _norm_in_proj(x, g, w_in_t, *, tm, tn, name):
    s, d = x.shape
    n = w_in_t.shape[0]

    def body(x_ref, g_ref, w_ref, z_ref, h_ref):
        @pl.when(pl.program_id(1) == 0)
        def _():
            xv = x_ref[...]
            h_ref[...] = (xv * _rms(xv) * g_ref[...]).astype(BF16)

        z_ref[...] = _dot(h_ref[...], w_ref[...], NT)

    return pl.pallas_call(
        body, name=name, grid=(s // tm, n // tn),
        in_specs=[pl.BlockSpec((tm, d), lambda i, j: (i, 0)), pl.BlockSpec((1, d), lambda i, j: (0, 0)),
                  pl.BlockSpec((tn, d), lambda i, j: (j, 0))],
        out_specs=[pl.BlockSpec((tm, tn), lambda i, j: (i, j)), pl.BlockSpec((tm, d), lambda i, j: (i, 0))],
        out_shape=[jax.ShapeDtypeStruct((s, n), F32), jax.ShapeDtypeStruct((s, d), BF16)],
        compiler_params=_params("parallel", "arbitrary"),
    )(x, g.reshape(1, d), w_in_t)


def _in_bwd_norm(dz, w_in_t, x, g, dres, *, tm, tk, name, token=None):
    s, kd = dz.shape
    d = w_in_t.shape[1]
    nk = kd // tk

    def body(dz_ref, w_ref, x_ref, g_ref, dres_ref, token_ref, dx_ref, dxb_ref, dg_ref, acc_ref):
        i = pl.program_id(0)
        k = pl.program_id(1)

        @pl.when(jnp.logical_and(i == 0, k == 0))
        def _():
            dg_ref[...] = jnp.zeros_like(dg_ref)

        acc_ref[...] = _dot(dz_ref[...], w_ref[...], NN) + jnp.where(k > 0, acc_ref[...], 0.0)

        @pl.when(k == nk - 1)
        def _():
            dh = acc_ref[...]
            xv = x_ref[...]
            rstd = _rms(xv)
            n = xv * rstd
            dn = dh * g_ref[...]
            dx = dres_ref[...] + rstd * (dn - n * jnp.mean(dn * n, axis=-1, keepdims=True))
            dx_ref[...] = dx
            dxb_ref[...] = dx.astype(BF16)
            dg_ref[0:1, :] += jnp.sum(dh * n, axis=0, keepdims=True)

    row = pl.BlockSpec((tm, d), lambda i, k: (i, 0))
    return pl.pallas_call(
        body, name=name, grid=(s // tm, nk),
        in_specs=[pl.BlockSpec((tm, tk), lambda i, k: (i, k)), pl.BlockSpec((tk, d), lambda i, k: (k, 0)),
                  row, pl.BlockSpec((1, d), lambda i, k: (0, 0)), row, pl.BlockSpec(memory_space=pl.ANY)],
        out_specs=[row, row, pl.BlockSpec((8, d), lambda i, k: (0, 0))],
        out_shape=[jax.ShapeDtypeStruct((s, d), F32), jax.ShapeDtypeStruct((s, d), BF16),
                   jax.ShapeDtypeStruct((8, d), F32)],
        scratch_shapes=[pltpu.VMEM((tm, d), F32)],
        compiler_params=_params("arbitrary", "arbitrary"),
    )(dz, w_in_t, x, g.reshape(1, d), dres, _behind(token))


def _loss_head(x, g, tgt, *, tm, name):
    s, d = x.shape

    def body(x_ref, g_ref, t_ref, loss_ref, dx_ref, dxb_ref, dg_ref):
        @pl.when(pl.program_id(0) == 0)
        def _():
            dg_ref[...] = jnp.zeros_like(dg_ref)
            loss_ref[...] = jnp.zeros_like(loss_ref)

        xv = x_ref[...]
        gv = g_ref[...]
        rstd = _rms(xv)
        n = xv * rstd
        e = n * gv - t_ref[...]
        part = 0.5 * jnp.sum(jnp.mean(e * e, axis=-1, keepdims=True), axis=0, keepdims=True)
        loss_ref[...] += jnp.broadcast_to(part, loss_ref.shape)
        dy = e * (1.0 / d)
        dn = dy * gv
        dx = rstd * (dn - n * jnp.mean(dn * n, axis=-1, keepdims=True))
        dx_ref[...] = dx
        dxb_ref[...] = dx.astype(BF16)
        dg_ref[0:1, :] += jnp.sum(dy * n, axis=0, keepdims=True)

    return pl.pallas_call(
        body, name=name, grid=(s // tm,),
        in_specs=[pl.BlockSpec((tm, d), lambda i: (i, 0)), pl.BlockSpec((1, d), lambda i: (0, 0)),
                  pl.BlockSpec((tm, d), lambda i: (i, 0))],
        out_specs=[pl.BlockSpec((8, 128), lambda i: (0, 0)), pl.BlockSpec((tm, d), lambda i: (i, 0)),
                   pl.BlockSpec((tm, d), lambda i: (i, 0)), pl.BlockSpec((8, d), lambda i: (0, 0))],
        out_shape=[jax.ShapeDtypeStruct((8, 128), F32), jax.ShapeDtypeStruct((s, d), F32),
                   jax.ShapeDtypeStruct((s, d), BF16), jax.ShapeDtypeStruct((8, d), F32)],
        compiler_params=_params("arbitrary"),
    )(x, g.reshape(1, d), tgt)


def _scan_rows(a_ref, b_ref, h_ref, carry, *, rows, reverse):
    width = a_ref.shape[1]
    n_chunks = rows // 8
    row = lax.broadcasted_iota(jnp.int32, (8, width), 0)

    def step(ci, carry):
        chunk = (n_chunks - 1 - ci) if reverse else ci
        off = pl.multiple_of(chunk * 8, 8)
        av = a_ref[pl.ds(off, 8), :]
        bv = b_ref[pl.ds(off, 8), :]
        for sh in (1, 2, 4):
            if reverse:
                a_sh = pltpu.roll(av, 8 - sh, 0)
                b_sh = pltpu.roll(bv, 8 - sh, 0)
                m = row < 8 - sh
            else:
                a_sh = pltpu.roll(av, sh, 0)
                b_sh = pltpu.roll(bv, sh, 0)
                m = row >= sh
            bv = jnp.where(m, av * b_sh + bv, bv)
            av = jnp.where(m, av * a_sh, av)
        h = av * carry + bv
        h_ref[pl.ds(off, 8), :] = h
        return h[0:1, :] if reverse else h[7:8, :]

    return lax.fori_loop(0, n_chunks, step, carry)


P_CB, P_BA, P_BX, P_DECAY, P_DSP, N_PAR = 4, 5, 6, 7, 8, 9


def _spread_mixer_params(par, par_sc, cw_ref, cb_ref, ba_ref, bx_ref, lam_ref, scw_ref):
    rows = par.shape[1:]
    for k in range(4):
        par[k] = jnp.broadcast_to(cw_ref[k:k + 1, :], rows)
    par[P_CB] = jnp.broadcast_to(cb_ref[...], rows)
    par[P_BA] = jnp.broadcast_to(ba_ref[...], rows)
    par[P_BX] = jnp.broadcast_to(bx_ref[...], rows)
    par[P_DECAY] = jnp.broadcast_to(-RG_C * _softplus(-lam_ref[...]), rows)
    par[P_DSP] = jnp.broadcast_to(-_sigmoid(-lam_ref[...]), rows)
    for k in range(3):
        par_sc[k] = jnp.broadcast_to(scw_ref[k:k + 1, :], par_sc.shape[1:])


def _gates_rows(pre_r, pre_i, par):
    r = _sigmoid(pre_r + par[P_BA])
    ig = _sigmoid(pre_i + par[P_BX])
    log_a = r * par[P_DECAY]
    a = jnp.exp(log_a)
    one_minus_a2 = -jnp.tanh(log_a) * (a * a + 1.0)
    return r, ig, a, jnp.sqrt(one_minus_a2), one_minus_a2


def _mixer_fwd(z, cw, cb, wa_bd, wx_bd, ba, bx, lam, scw, *, tile, name):
    s = z.shape[0]
    n_tiles = s // tile

    rows_of = lambda r0: slice(r0, r0 + FFN_ROWS)
    col_gate, col_sb, col_sc, col_sx = (slice(D_LRU, 2 * D_LRU), slice(2 * D_LRU, 2 * D_LRU + D_SC),
                                        slice(2 * D_LRU + D_SC, 2 * D_LRU + 2 * D_SC), slice(2 * D_LRU + 2 * D_SC, D_IN))

    def body(z_ref, cw_ref, cb_ref, wa_ref, wx_ref, ba_ref, bx_ref, lam_ref, scw_ref,
             y_ref, hs_ref, par, par_sc, lx_s, lxb_s, a_s, b_s, car_lx, car_q, h_car):
        i = pl.program_id(0)

        @pl.when(i == 0)
        def _():
            car_lx[...] = jnp.zeros_like(car_lx)
            car_q[...] = jnp.zeros_like(car_q)
            h_car[...] = jnp.zeros_like(h_car)
            _spread_mixer_params(par, par_sc, cw_ref, cb_ref, ba_ref, bx_ref, lam_ref, scw_ref)

        before_lx, before_q = car_lx[...], car_q[...]
        for r0 in range(0, tile, FFN_ROWS):
            cur = z_ref[rows_of(r0), 0:D_LRU]
            lx = par[P_CB] + par[3] * cur
            for k in range(3):
                lx = lx + par[k] * _rows_from(before_lx, cur, FFN_ROWS - 3 + k)
            lx_s[rows_of(r0), :] = lx
            lxb_s[rows_of(r0), :] = lx.astype(BF16)
            before_lx = cur
            q = z_ref[rows_of(r0), col_sc] * z_ref[rows_of(r0), col_sx]
            cq = par_sc[2] * q
            for k in range(2):
                cq = cq + par_sc[k] * _rows_from(before_q, q, FFN_ROWS - 2 + k)
            y_ref[rows_of(r0), D_LRU:D_MIX] = (z_ref[rows_of(r0), col_sb] * cq).astype(BF16)
            before_q = q
        car_lx[...] = before_lx
        car_q[...] = before_q

        for g in range(N_GROUPS):
            cols = slice(g * LRU_GROUP, (g + 1) * LRU_GROUP)
            a_s[:, cols] = _dot(lxb_s[:, cols], wa_ref[g], NN)
            b_s[:, cols] = _dot(lxb_s[:, cols], wx_ref[g], NN)

        for r0 in range(0, tile, FFN_ROWS):
            _, ig, a, mult, _ = _gates_rows(a_s[rows_of(r0), :], b_s[rows_of(r0), :], par)
            a_s[rows_of(r0), :] = a
            b_s[rows_of(r0), :] = mult * (ig * lx_s[rows_of(r0), :])
        h_car[0:1, :] = _scan_rows(a_s, b_s, hs_ref, h_car[0:1, :], rows=tile, reverse=False)

        for r0 in range(0, tile, FFN_ROWS):
            y_ref[rows_of(r0), 0:D_LRU] = (hs_ref[rows_of(r0), :] * _gelu(z_ref[rows_of(r0), col_gate])).astype(BF16)

    full = lambda shape: pl.BlockSpec(shape, lambda i: (0,) * len(shape))
    return pl.pallas_call(
        body, name=name, grid=(n_tiles,),
        in_specs=[pl.BlockSpec((tile, D_IN), lambda i: (i, 0)),
                  full((4, D_LRU)), full((1, D_LRU)),
                  full((N_GROUPS, LRU_GROUP, LRU_GROUP)), full((N_GROUPS, LRU_GROUP, LRU_GROUP)),
                  full((1, D_LRU)), full((1, D_LRU)), full((1, D_LRU)), full((3, D_SC))],
        out_specs=[pl.BlockSpec((tile, D_MIX), lambda i: (i, 0)), pl.BlockSpec((tile, D_LRU), lambda i: (i, 0))],
        out_shape=[jax.ShapeDtypeStruct((s, D_MIX), BF16), jax.ShapeDtypeStruct((s, D_LRU), F32)],
        scratch_shapes=[pltpu.VMEM((N_PAR, FFN_ROWS, D_LRU), F32), pltpu.VMEM((3, FFN_ROWS, D_SC), F32),
                        pltpu.VMEM((tile, D_LRU), F32), pltpu.VMEM((tile, D_LRU), BF16),
                        pltpu.VMEM((tile, D_LRU), F32), pltpu.VMEM((tile, D_LRU), F32),
                        pltpu.VMEM((FFN_ROWS, D_LRU), F32), pltpu.VMEM((FFN_ROWS, D_SC), F32),
                        pltpu.VMEM((8, D_LRU), F32)],
        compiler_params=_params("arbitrary"),
    )(z, cw, cb.reshape(1, -1), wa_bd, wx_bd, ba.reshape(1, -1), bx.reshape(1, -1), lam.reshape(1, -1), scw)


def _fold8(x):
    return sum(x[q:q + 8] for q in range(0, x.shape[0], 8))


def _mixer_bwd_rows(z, hs, dy, cw, cb, wa_bd, wx_bd, ba, bx, lam, scw, *, tile, name, token=None):
    s = z.shape[0]
    n_tiles = s // tile
    per8 = tile // 8
    rows_of = lambda r0: slice(r0, r0 + FFN_ROWS)
    col_gate, col_sb, col_sc, col_sx = (slice(D_LRU, 2 * D_LRU), slice(2 * D_LRU, 2 * D_LRU + D_SC),
                                        slice(2 * D_LRU + D_SC, 2 * D_LRU + 2 * D_SC), slice(2 * D_LRU + 2 * D_SC, D_IN))
    up = range(0, tile, FFN_ROWS)
    down = range(tile - FFN_ROWS, -1, -FFN_ROWS)
    A_CB, A_BA, A_BX, A_SP, A_CW = 0, 1, 2, 3, 4

    def body(z_ref, zp_ref, hs_ref, hsp_ref, dy_ref, cw_ref, cb_ref, wa_ref, wx_ref, ba_ref, bx_ref, lam_ref, scw_ref,
             token_ref, dz_ref, dcw_ref, dvec_ref, dwa_ref, dwx_ref, dscw_ref,
             par, par_sc, lx_s, lxb_s, cq_s, pr_s, pi_s, r_s, ig_s, a_s, mult_s, inv_s, ash_s, b_s, lam_s,
             dlx_s, dpr_b, dpi_b, back_s, car_a, car_dlx, car_dcq, l_car, acc, acc_sc):
        i = pl.program_id(0)

        @pl.when(i == 0)
        def _():
            for ref in (dwa_ref, dwx_ref, l_car, car_a, car_dlx, car_dcq, acc, acc_sc):
                ref[...] = jnp.zeros_like(ref)
            _spread_mixer_params(par, par_sc, cw_ref, cb_ref, ba_ref, bx_ref, lam_ref, scw_ref)

        keep = jnp.where(i == n_tiles - 1, 0.0, 1.0)
        zeros8 = lambda n: jnp.zeros((8, n), F32)

        before_lx = jnp.concatenate([zeros8(D_LRU), zp_ref[:, 0:D_LRU] * keep], axis=0)
        before_q = jnp.concatenate([zeros8(D_SC), zp_ref[:, col_sc] * zp_ref[:, col_sx] * keep], axis=0)
        for r0 in up:
            cur = z_ref[rows_of(r0), 0:D_LRU]
            lx = par[P_CB] + par[3] * cur
            for k in range(3):
                lx = lx + par[k] * _rows_from(before_lx, cur, FFN_ROWS - 3 + k)
            lx_s[rows_of(r0), :] = lx
            lxb_s[rows_of(r0), :] = lx.astype(BF16)
            before_lx = cur
            q = z_ref[rows_of(r0), col_sc] * z_ref[rows_of(r0), col_sx]
            cq = par_sc[2] * q
            for k in range(2):
                cq = cq + par_sc[k] * _rows_from(before_q, q, FFN_ROWS - 2 + k)
            cq_s[rows_of(r0), :] = cq
            before_q = q

        for g in range(N_GROUPS):
            cols = slice(g * LRU_GROUP, (g + 1) * LRU_GROUP)
            pr_s[:, cols] = _dot(lxb_s[:, cols], wa_ref[g], NN)
            pi_s[:, cols] = _dot(lxb_s[:, cols], wx_ref[g], NN)

        after_a = car_a[...]
        for r0 in down:
            r, ig, a, mult, one_minus_a2 = _gates_rows(pr_s[rows_of(r0), :], pi_s[rows_of(r0), :], par)
            r_s[rows_of(r0), :] = r
            ig_s[rows_of(r0), :] = ig
            a_s[rows_of(r0), :] = a
            mult_s[rows_of(r0), :] = mult
            inv_s[rows_of(r0), :] = lax.rsqrt(one_minus_a2)
            ash_s[rows_of(r0), :] = _rows_from(a, after_a, 1)
            after_a = a
            ge, dge = _gelu_parts(z_ref[rows_of(r0), col_gate])
            dy_lru = dy_ref[rows_of(r0), 0:D_LRU]
            dz_ref[rows_of(r0), col_gate] = (dy_lru * hs_ref[rows_of(r0), :] * dge).astype(BF16)
            b_s[rows_of(r0), :] = dy_lru * ge
        car_a[...] = after_a
        l_car[0:1, :] = _scan_rows(ash_s, b_s, lam_s, l_car[0:1, :], rows=tile, reverse=True)

        before_h = jnp.concatenate([zeros8(D_LRU), hsp_ref[...] * keep], axis=0)
        for r0 in up:
            lv = lam_s[rows_of(r0), :]
            h_here = hs_ref[rows_of(r0), :]
            lx, r, ig, a = lx_s[rows_of(r0), :], r_s[rows_of(r0), :], ig_s[rows_of(r0), :], a_s[rows_of(r0), :]
            mult = mult_s[rows_of(r0), :]
            da = lv * _rows_from(before_h, h_here, FFN_ROWS - 1)
            before_h = h_here
            d_mult = lv * ig * lx
            d_i = lv * mult * lx
            dlx_s[rows_of(r0), :] = lv * mult * ig
            dlog_a = da * a - d_mult * (a * a) * inv_s[rows_of(r0), :]
            dpre_r = dlog_a * par[P_DECAY] * r * (1.0 - r)
            dpre_i = d_i * ig * (1.0 - ig)
            acc[A_BA] += _fold8(dpre_r)
            acc[A_BX] += _fold8(dpre_i)
            acc[A_SP] += _fold8(dlog_a * r)
            dpr_b[rows_of(r0), :] = dpre_r.astype(BF16)
            dpi_b[rows_of(r0), :] = dpre_i.astype(BF16)

        for g in range(N_GROUPS):
            cols = slice(g * LRU_GROUP, (g + 1) * LRU_GROUP)
            dwa_ref[g] += _dot(lxb_s[:, cols], dpr_b[:, cols], TN)
            dwx_ref[g] += _dot(lxb_s[:, cols], dpi_b[:, cols], TN)
            back_s[:, cols] = _dot(dpr_b[:, cols], wa_ref[g], NT) + _dot(dpi_b[:, cols], wx_ref[g], NT)

        after_dlx, after_dcq = car_dlx[...], car_dcq[...]
        for r0 in down:
            dlx = dlx_s[rows_of(r0), :] + back_s[rows_of(r0), :]
            lxp = z_ref[rows_of(r0), 0:D_LRU]
            acc[A_CB] += _fold8(dlx)
            acc[A_CW + 3] += _fold8(dlx * lxp)
            dlxp = par[3] * dlx
            for sh in range(1, 4):
                below = _rows_from(dlx, after_dlx, sh)
                dlxp = dlxp + par[3 - sh] * below
                acc[A_CW + 3 - sh] += _fold8(below * lxp)
            dz_ref[rows_of(r0), 0:D_LRU] = dlxp.astype(BF16)
            after_dlx = dlx

            dy_sc = dy_ref[rows_of(r0), D_LRU:D_MIX]
            sb, sc, sx = z_ref[rows_of(r0), col_sb], z_ref[rows_of(r0), col_sc], z_ref[rows_of(r0), col_sx]
            dz_ref[rows_of(r0), col_sb] = (dy_sc * cq_s[rows_of(r0), :]).astype(BF16)
            dcq = dy_sc * sb
            q = sc * sx
            acc_sc[2] += _fold8(dcq * q)
            dq = par_sc[2] * dcq
            for sh in range(1, 3):
                below = _rows_from(dcq, after_dcq, sh)
                dq = dq + par_sc[2 - sh] * below
                acc_sc[2 - sh] += _fold8(below * q)
            dz_ref[rows_of(r0), col_sc] = (dq * sx).astype(BF16)
            dz_ref[rows_of(r0), col_sx] = (dq * sc).astype(BF16)
            after_dcq = dcq
        car_dlx[...] = after_dlx
        car_dcq[...] = after_dcq

        @pl.when(i == n_tiles - 1)
        def _():
            total = lambda x: jnp.sum(x, axis=0, keepdims=True)
            dcw_ref[...] = jnp.zeros_like(dcw_ref)
            dvec_ref[...] = jnp.zeros_like(dvec_ref)
            dscw_ref[...] = jnp.zeros_like(dscw_ref)
            for k in range(4):
                dcw_ref[k:k + 1, :] = total(acc[A_CW + k])
            for k in range(3):
                dvec_ref[k:k + 1, :] = total(acc[k])
                dscw_ref[k:k + 1, :] = total(acc_sc[k])
            dvec_ref[3:4, :] = total(acc[A_SP]) * (-RG_C) * par[P_DSP][0:1, :]

    rev = lambda i: n_tiles - 1 - i
    prev8 = lambda i: jnp.maximum(rev(i) * per8 - 1, 0)
    full = lambda shape: pl.BlockSpec(shape, lambda i: (0,) * len(shape))
    wide = lambda rows, dt=F32: pltpu.VMEM((rows, D_LRU), dt)
    return pl.pallas_call(
        body, name=name, grid=(n_tiles,),
        in_specs=[pl.BlockSpec((tile, D_IN), lambda i: (rev(i), 0)),
                  pl.BlockSpec((HALO, D_IN), lambda i: (prev8(i), 0)),
                  pl.BlockSpec((tile, D_LRU), lambda i: (rev(i), 0)),
                  pl.BlockSpec((HALO, D_LRU), lambda i: (prev8(i), 0)),
                  pl.BlockSpec((tile, D_MIX), lambda i: (rev(i), 0)),
                  full((4, D_LRU)), full((1, D_LRU)),
                  full((N_GROUPS, LRU_GROUP, LRU_GROUP)), full((N_GROUPS, LRU_GROUP, LRU_GROUP)),
                  full((1, D_LRU)), full((1, D_LRU)), full((1, D_LRU)), full((3, D_SC)),
                  pl.BlockSpec(memory_space=pl.ANY)],
        out_specs=[pl.BlockSpec((tile, D_IN), lambda i: (rev(i), 0)),
                   full((8, D_LRU)), full((8, D_LRU)),
                   full((N_GROUPS, LRU_GROUP, LRU_GROUP)), full((N_GROUPS, LRU_GROUP, LRU_GROUP)),
                   full((8, D_SC))],
        out_shape=[jax.ShapeDtypeStruct((s, D_IN), BF16),
                   jax.ShapeDtypeStruct((8, D_LRU), F32), jax.ShapeDtypeStruct((8, D_LRU), F32),
                   jax.ShapeDtypeStruct((N_GROUPS, LRU_GROUP, LRU_GROUP), F32),
                   jax.ShapeDtypeStruct((N_GROUPS, LRU_GROUP, LRU_GROUP), F32),
                   jax.ShapeDtypeStruct((8, D_SC), F32)],
        scratch_shapes=[pltpu.VMEM((N_PAR, FFN_ROWS, D_LRU), F32), pltpu.VMEM((3, FFN_ROWS, D_SC), F32),
                        wide(tile), wide(tile, BF16), pltpu.VMEM((tile, D_SC), F32),
                        wide(tile), wide(tile), wide(tile), wide(tile), wide(tile), wide(tile), wide(tile),
                        wide(tile), wide(tile), wide(tile),
                        wide(tile), wide(tile, BF16), wide(tile, BF16), wide(tile),
                        wide(FFN_ROWS), wide(FFN_ROWS), pltpu.VMEM((FFN_ROWS, D_SC), F32), wide(8),
                        pltpu.VMEM((8, 8, D_LRU), F32), pltpu.VMEM((3, 8, D_SC), F32)],
        compiler_params=_params("arbitrary"),
    )(z, z, hs, hs, dy, cw, cb.reshape(1, -1), wa_bd, wx_bd, ba.reshape(1, -1), bx.reshape(1, -1),
      lam.reshape(1, -1), scw, _behind(token))


FFN_ROWS = 16
FFN_GROUPS = 2


def _spread_taps(fw_ref, taps):
    for half in range(2):
        for k in range(3):
            taps[half, k] = jnp.broadcast_to(fw_ref[half, k:k + 1, :], taps.shape[2:])


def _rows_from(first, second, start):
    stack = jnp.concatenate([first, second], axis=0)
    return pltpu.roll(stack, 2 * FFN_ROWS - start, 0)[0:FFN_ROWS]


def _conv3_rows(taps, ext_ref, half, row):
    before = ext_ref[half, row - FFN_ROWS:row, :]
    here = ext_ref[half, row:row + FFN_ROWS, :]
    acc = taps[half, 2] * here
    for k in range(2):
        acc = acc + taps[half, k] * _rows_from(before, here, FFN_ROWS - 2 + k)
    return acc


HALO_B = 16


def _ffn_block_fwd(x2, g2, w_up_b, fcw, w_down, *, tile, name):
    s = x2.shape[0]
    nb = w_up_b.shape[2]
    blocks = D_FF // nb
    per16 = tile // HALO_B

    def body(x2_ref, x2p_ref, g_ref, wg_ref, wu_ref, fw_ref, wd_ref, x3_ref, h_ref, act_ref, p_ref, u_ref,
             ext_p, acc_ref, taps, lhs):
        i = pl.program_id(0)
        j = pl.program_id(1)
        keep = jnp.where(i == 0, 0.0, 1.0)
        _spread_taps(fw_ref, taps)
        @pl.when(j == 0)
        def _():
            for rows_ref, at in ((x2p_ref, 0), (x2_ref, HALO_B)):
                xv = rows_ref[...]
                lhs[at:at + xv.shape[0], :] = (xv * _rms(xv) * g_ref[...]).astype(BF16)
            h_ref[...] = lhs[HALO_B:HALO_B + tile, :]

        grp = tile // FFN_GROUPS
        for g in range(FFN_GROUPS):
            new = slice(g * grp + (HALO_B if g else 0), (g + 1) * grp + HALO_B)
            for half, w_ref in ((0, wg_ref), (1, wu_ref)):
                pe = _dot(lhs[new, :], w_ref[...], NN)
                if g == 0:
                    ext_p[half, 0:HALO_B, :] = pe[0:HALO_B] * keep
                    ext_p[half, HALO_B:grp + HALO_B, :] = pe[HALO_B:]
                    p_ref[half, 0:grp, :] = pe[HALO_B:].astype(BF16)
                else:
                    ext_p[half, new, :] = pe
                    p_ref[half, g * grp:(g + 1) * grp, :] = pe.astype(BF16)
        for g in range(FFN_GROUPS):
            rows = slice(g * grp, (g + 1) * grp)
            acts = []
            for r0 in range(g * grp, (g + 1) * grp, FFN_ROWS):
                u = [_conv3_rows(taps, ext_p, half, HALO_B + r0) for half in range(2)]
                for half in range(2):
                    u_ref[half, r0:r0 + FFN_ROWS, :] = u[half].astype(BF16)
                acts.append((_gelu(u[0]) * u[1]).astype(BF16))
                act_ref[r0:r0 + FFN_ROWS, :] = acts[-1]
            contrib = _dot(jnp.concatenate(acts, axis=0), wd_ref[...], NN)
            acc_ref[rows, :] = contrib + jnp.where(j > 0, acc_ref[rows, :], 0.0)

        @pl.when(j == blocks - 1)
        def _():
            x3_ref[...] = x2_ref[...] + acc_ref[...]

    return pl.pallas_call(
        body, name=name, grid=(s // tile, blocks),
        in_specs=[pl.BlockSpec((tile, D_MODEL), lambda i, j: (i, 0)),
                  pl.BlockSpec((HALO_B, D_MODEL), lambda i, j: (jnp.maximum(i * per16 - 1, 0), 0)),
                  pl.BlockSpec((1, D_MODEL), lambda i, j: (0, 0)),
                  pl.BlockSpec((None, D_MODEL, nb), lambda i, j: (j, 0, 0)),
                  pl.BlockSpec((None, D_MODEL, nb), lambda i, j: (j + blocks, 0, 0)),
                  pl.BlockSpec((2, 3, nb), lambda i, j: (0, 0, j)),
                  pl.BlockSpec((nb, D_MODEL), lambda i, j: (j, 0))],
        out_specs=[pl.BlockSpec((tile, D_MODEL), lambda i, j: (i, 0)),
                   pl.BlockSpec((tile, D_MODEL), lambda i, j: (i, 0)),
                   pl.BlockSpec((tile, nb), lambda i, j: (i, j)),
                   pl.BlockSpec((2, tile, nb), lambda i, j: (0, i, j)),
                   pl.BlockSpec((2, tile, nb), lambda i, j: (0, i, j))],
        out_shape=[jax.ShapeDtypeStruct((s, D_MODEL), F32), jax.ShapeDtypeStruct((s, D_MODEL), BF16),
                   jax.ShapeDtypeStruct((s, D_FF), BF16),
                   jax.ShapeDtypeStruct((2, s, D_FF), BF16), jax.ShapeDtypeStruct((2, s, D_FF), BF16)],
        scratch_shapes=[pltpu.VMEM((2, tile + HALO_B, nb), F32), pltpu.VMEM((tile, D_MODEL), F32),
                        pltpu.VMEM((2, 3, FFN_ROWS, nb), F32), pltpu.VMEM((tile + HALO_B, D_MODEL), BF16)],
        compiler_params=_params("parallel", "arbitrary"),
    )(x2, x2, g2.reshape(1, -1), w_up_b, w_up_b, fcw, w_down)


def _ffn_block_bwd(dx3, dx3b, p, u, x2, g2, w_up_b, fcw, w_down, *, tile, name, token=None):
    s = x2.shape[0]
    nb = w_up_b.shape[2]
    blocks = D_FF // nb
    n_tiles = s // tile
    per16 = tile // HALO_B
    last16 = s // HALO_B - 1

    def body(dxb_ref, dxbn_ref, wd_ref, p_ref, u_ref, un_ref, fw_ref, wg_ref, wu_ref, x2_ref, g_ref, dx3_ref,
             token_ref, dx2_ref, dx2b_ref, dg_ref, dp_ref, dw_ref, da_s, acc_w, acc_dh, taps):
        i = pl.program_id(0)
        j = pl.program_id(1)

        @pl.when(jnp.logical_and(i == 0, j == 0))
        def _():
            acc_w[...] = jnp.zeros_like(acc_w)
            dg_ref[...] = jnp.zeros_like(dg_ref)

        keep_next = jnp.where(i == n_tiles - 1, 0.0, 1.0)
        _spread_taps(fw_ref, taps)
        lhs = jnp.concatenate([dxb_ref[...], dxbn_ref[...]], axis=0)
        grp = tile // FFN_GROUPS
        for g in reversed(range(FFN_GROUPS)):
            new = slice(g * grp, (g + 1) * grp + (HALO_B if g == FFN_GROUPS - 1 else 0))
            da_s[new, :] = _dot(lhs[new], wd_ref[...], NT)

        def du_rows(da, u_gate, u_up):
            ge, dge = _gelu_parts(u_gate)
            return da * u_up * dge, da * ge

        after = du_rows(da_s[tile:tile + HALO_B, :] * keep_next, un_ref[0].astype(F32), un_ref[1].astype(F32))
        for g in reversed(range(FFN_GROUPS)):
            rows = slice(g * grp, (g + 1) * grp)
            dps = ([], [])
            for r0 in range((g + 1) * grp - FFN_ROWS, g * grp - 1, -FFN_ROWS):
                du = du_rows(da_s[r0:r0 + FFN_ROWS, :], u_ref[0, r0:r0 + FFN_ROWS, :].astype(F32),
                             u_ref[1, r0:r0 + FFN_ROWS, :].astype(F32))
                for half in range(2):
                    below = [du[half], _rows_from(du[half], after[half], 1), _rows_from(du[half], after[half], 2)]
                    acc = taps[half, 2] * below[0]
                    for k in range(2):
                        acc = acc + taps[half, k] * below[2 - k]
                    dps[half].insert(0, acc.astype(BF16))
                    dp_ref[half, r0:r0 + FFN_ROWS, :] = dps[half][0]
                    p_rows = p_ref[half, r0:r0 + FFN_ROWS, :].astype(F32)
                    for k in range(3):
                        prod = below[2 - k] * p_rows
                        acc_w[j, half, k] += sum(prod[q:q + 8] for q in range(0, FFN_ROWS, 8))
                after = du
            contrib = (_dot(jnp.concatenate(dps[0], axis=0), wg_ref[...], NT)
                       + _dot(jnp.concatenate(dps[1], axis=0), wu_ref[...], NT))
            acc_dh[rows, :] = contrib + jnp.where(j > 0, acc_dh[rows, :], 0.0)

        @pl.when(j == blocks - 1)
        def _():
            dh = acc_dh[...]
            xv = x2_ref[...]
            rstd = _rms(xv)
            n = xv * rstd
            dn = dh * g_ref[...]
            dx = dx3_ref[...] + rstd * (dn - n * jnp.mean(dn * n, axis=-1, keepdims=True))
            dx2_ref[...] = dx
            dx2b_ref[...] = dx.astype(BF16)
            dg_ref[0:1, :] += jnp.sum(dh * n, axis=0, keepdims=True)

        @pl.when(jnp.logical_and(i == n_tiles - 1, j == blocks - 1))
        def _():
            dw_ref[...] = jnp.zeros_like(dw_ref)
            for jj in range(blocks):
                for half in range(2):
                    for k in range(3):
                        dw_ref[half, k:k + 1, jj * nb:(jj + 1) * nb] = jnp.sum(acc_w[jj, half, k], axis=0, keepdims=True)

    next16 = lambda i: jnp.minimum((i + 1) * per16, last16)
    return pl.pallas_call(
        body, name=name, grid=(n_tiles, blocks),
        in_specs=[pl.BlockSpec((tile, D_MODEL), lambda i, j: (i, 0)),
                  pl.BlockSpec((HALO_B, D_MODEL), lambda i, j: (next16(i), 0)),
                  pl.BlockSpec((nb, D_MODEL), lambda i, j: (j, 0)),
                  pl.BlockSpec((2, tile, nb), lambda i, j: (0, i, j)),
                  pl.BlockSpec((2, tile, nb), lambda i, j: (0, i, j)),
                  pl.BlockSpec((2, HALO_B, nb), lambda i, j: (0, next16(i), j)),
                  pl.BlockSpec((2, 3, nb), lambda i, j: (0, 0, j)),
                  pl.BlockSpec((None, D_MODEL, nb), lambda i, j: (j, 0, 0)),
                  pl.BlockSpec((None, D_MODEL, nb), lambda i, j: (j + blocks, 0, 0)),
                  pl.BlockSpec((tile, D_MODEL), lambda i, j: (i, 0)),
                  pl.BlockSpec((1, D_MODEL), lambda i, j: (0, 0)),
                  pl.BlockSpec((tile, D_MODEL), lambda i, j: (i, 0)),
                  pl.BlockSpec(memory_space=pl.ANY)],
        out_specs=[pl.BlockSpec((tile, D_MODEL), lambda i, j: (i, 0)),
                   pl.BlockSpec((tile, D_MODEL), lambda i, j: (i, 0)),
                   pl.BlockSpec((8, D_MODEL), lambda i, j: (0, 0)),
                   pl.BlockSpec((2, tile, nb), lambda i, j: (0, i, j)),
                   pl.BlockSpec((2, 8, D_FF), lambda i, j: (0, 0, 0))],
        out_shape=[jax.ShapeDtypeStruct((s, D_MODEL), F32), jax.ShapeDtypeStruct((s, D_MODEL), BF16),
                   jax.ShapeDtypeStruct((8, D_MODEL), F32), jax.ShapeDtypeStruct((2, s, D_FF), BF16),
                   jax.ShapeDtypeStruct((2, 8, D_FF), F32)],
        scratch_shapes=[pltpu.VMEM((tile + HALO_B, nb), F32), pltpu.VMEM((blocks, 2, 3, 8, nb), F32),
                        pltpu.VMEM((tile, D_MODEL), F32), pltpu.VMEM((2, 3, FFN_ROWS, nb), F32)],
        compiler_params=_params("arbitrary", "arbitrary"),
    )(dx3b, dx3b, w_down, p, u, u, fcw, w_up_b, w_up_b, x2, g2.reshape(1, -1), dx3, _behind(token))


def _adamw_math(w, g, m, v):
    m = ADAM_B1 * m + (1.0 - ADAM_B1) * g
    v = ADAM_B2 * v + (1.0 - ADAM_B2) * (g * g)
    m_hat = m / (1.0 - ADAM_B1 ** ADAM_STEP)
    v_hat = v / (1.0 - ADAM_B2 ** ADAM_STEP)
    delta = -ADAM_LR * (m_hat / (jnp.sqrt(v_hat) + ADAM_EPS) + ADAM_WD * w)
    return delta, m, v


def _adamw(w, g, m, v, *, name):
    rows, cols = w.shape
    tr = rows
    for cand in (512, 256, 128, 64, 32, 16, 8):
        if rows % cand == 0 and rows > cand:
            tr = cand
            break

    def body(w_ref, g_ref, m_ref, v_ref, d_ref, nm_ref, nv_ref):
        d, nm, nv = _adamw_math(w_ref[...], g_ref[...], m_ref[...], v_ref[...])
        d_ref[...] = d
        nm_ref[...] = nm
        nv_ref[...] = nv

    spec = pl.BlockSpec((tr, cols), lambda i: (i, 0))
    return pl.pallas_call(
        body, name=name, grid=(rows // tr,), in_specs=[spec] * 4, out_specs=[spec] * 3,
        out_shape=[jax.ShapeDtypeStruct((rows, cols), F32)] * 3,
        compiler_params=_params("parallel"),
    )(w, g, m, v)


def _sum_adamw(parts, w, m, v, *, name):
    depth, rows, cols = w.shape
    tr = rows
    for cand in (256, 128, 64):
        if rows % cand == 0 and rows > cand:
            tr = cand
            break

    def body(*refs):
        part_refs = refs[:depth]
        w_ref, m_ref, v_ref, g_ref, d_ref, nm_ref, nv_ref = refs[depth:]
        layer = pl.program_id(0)
        grad = None
        for k, p_ref in enumerate(part_refs):
            total = p_ref[0].astype(F32)
            for dev in range(1, N_DEV):
                total = total + p_ref[dev].astype(F32)
            grad = total if grad is None else jnp.where(layer == k, total, grad)
        d, nm, nv = _adamw_math(w_ref[...], grad, m_ref[...], v_ref[...])
        g_ref[...] = grad
        d_ref[...] = d
        nm_ref[...] = nm
        nv_ref[...] = nv

    part_spec = lambda k: pl.BlockSpec((N_DEV, tr, cols), lambda l, i: (0, jnp.where(l == k, i, 0), 0))
    spec = pl.BlockSpec((None, tr, cols), lambda l, i: (l, i, 0))
    return pl.pallas_call(
        body, name=name, grid=(depth, rows // tr),
        in_specs=[part_spec(k) for k in range(depth)] + [spec] * 3, out_specs=[spec] * 4,
        out_shape=[jax.ShapeDtypeStruct((depth, rows, cols), F32)] * 4,
        compiler_params=_params("parallel", "parallel"),
    )(*parts, w, m, v)


def _sum_parts(parts, *, name):
    _, rows, cols = parts.shape
    tr = rows
    for cand in (256, 128, 64, 32, 16):
        if rows % cand == 0 and rows > cand:
            tr = cand
            break

    def body(p_ref, o_ref):
        acc = p_ref[0].astype(F32)
        for d in range(1, N_DEV):
            acc = acc + p_ref[d].astype(F32)
        o_ref[...] = acc

    return pl.pallas_call(
        body, name=name, grid=(rows // tr,),
        in_specs=[pl.BlockSpec((N_DEV, tr, cols), lambda i: (0, i, 0))],
        out_specs=pl.BlockSpec((tr, cols), lambda i: (i, 0)),
        out_shape=jax.ShapeDtypeStruct((rows, cols), F32),
        compiler_params=_params("parallel"),
    )(parts)


def _place():
    return lax.axis_index("x"), lax.axis_index("y"), lax.axis_index("c")


def _flip(v, bit):
    return 1 - v if bit else v


N_PEERS = N_DEV - 1


def _peer_copy(k, src_ref, land_ref, send_sem, recv_sem, gather):
    x, y, c = _place()
    my_id = 4 * x + 2 * y + c
    px, py, pc = _flip(x, k & 4), _flip(y, k & 2), _flip(c, k & 1)
    peer_id = 4 * px + 2 * py + pc
    return pltpu.make_async_remote_copy(
        src_ref=src_ref if gather else src_ref.at[peer_id], dst_ref=land_ref.at[my_id],
        send_sem=send_sem.at[k - 1], recv_sem=recv_sem.at[k - 1],
        device_id=(px, py, pc), device_id_type=MESH)


def _sequencer_copies(srcs, *, gather, name, collective_id, after):
    n = len(srcs)
    hbm = pltpu.MemorySpace.HBM
    src_refs = [jax.new_ref(s, memory_space=hbm) for s in srcs]
    land_refs = [jax.empty_ref(jax.ShapeDtypeStruct(((N_DEV,) + s.shape) if gather else s.shape, s.dtype),
                               memory_space=hbm) for s in srcs]
    token_in = jax.new_ref(jnp.zeros((8, 128), F32) if after is None else after, memory_space=hbm)
    token_out = jax.empty_ref(jax.ShapeDtypeStruct((8, 128), F32), memory_space=hbm)

    @pl.kernel(mesh=plsc.ScalarSubcoreMesh(axis_name="seq", num_cores=1), name=name,
               scratch_types=(pltpu.SemaphoreType.DMA((n, N_PEERS)), pltpu.SemaphoreType.DMA((n, N_PEERS)),
                              pltpu.SemaphoreType.DMA((n + 1,))),
               compiler_params=pltpu.CompilerParams(collective_id=collective_id))
    def launch(send_sems, recv_sems, local_sems):
        x, y, c = _place()
        my_id = 4 * x + 2 * y + c
        barrier = pltpu.get_barrier_semaphore()
        own = [pltpu.make_async_copy(src_refs[t] if gather else src_refs[t].at[my_id], land_refs[t].at[my_id],
                                     local_sems.at[t]) for t in range(n)]
        if gather:
            sibling = (x, y, 1 - c)
            chips = [(1 - x, y), (x, 1 - y), (1 - x, 1 - y)]
            for peer in [sibling] + [(*chip, c) for chip in chips]:
                pl.semaphore_signal(barrier, inc=1, device_id=peer, device_id_type=MESH)
            pl.semaphore_wait(barrier, 4)

            def copy(t, k, block, to, src=None):
                dst = land_refs[t].at[4 * block[0] + 2 * block[1] + block[2]]
                return pltpu.make_async_remote_copy(
                    src_ref=dst if src is None else src, dst_ref=dst,
                    send_sem=send_sems.at[t, k], recv_sem=recv_sems.at[t, k], device_id=to, device_id_type=MESH)

            for cp in own:
                cp.start()
            sends = []
            for t in range(n):
                sends.append(copy(t, 0, (x, y, c), sibling, src=src_refs[t]))
                sends += [copy(t, 1 + j, (x, y, c), (*chip, c), src=src_refs[t]) for j, chip in enumerate(chips)]
            for cp in sends:
                cp.start()
            for t in range(n):
                for j, chip in enumerate(chips):
                    copy(t, 1 + j, (*chip, c), (x, y, c)).wait_recv()
                    passed_on = copy(t, 4 + j, (*chip, c), sibling)
                    passed_on.start()
                    sends.append(passed_on)
            for t in range(n):
                copy(t, 0, sibling, (x, y, c)).wait_recv()
                for j, chip in enumerate(chips):
                    copy(t, 4 + j, (*chip, 1 - c), (x, y, c)).wait_recv()
            for cp in sends:
                cp.wait_send()
            for cp in own:
                cp.wait()
        else:
            for k in range(1, N_DEV):
                peer = (_flip(x, k & 4), _flip(y, k & 2), _flip(c, k & 1))
                pl.semaphore_signal(barrier, inc=1, device_id=peer, device_id_type=MESH)
            pl.semaphore_wait(barrier, N_PEERS)
            for cp in own:
                cp.start()
            copies = [_peer_copy(k, src_refs[t], land_refs[t], send_sems.at[t], recv_sems.at[t], gather)
                      for t in range(n) for k in range(1, N_DEV)]
            for cp in copies:
                cp.start()
            for cp in own:
                cp.wait()
            for cp in copies:
                cp.wait()
        passed = pltpu.make_async_copy(token_in, token_out, local_sems.at[n])
        passed.start()
        passed.wait()

    launch()
    return [ref[...] for ref in land_refs], token_out[...]


TM = 512
TMM = 1024
TKW = 2048
MIX_TILE = 256
FFN_FWD_TILE = 512
FFN_BWD_TILE = 512


def _block_diag(w):
    wg = w.reshape(N_GROUPS, HEADS_PER_GROUP, LRU_HEAD_DIM, LRU_HEAD_DIM)
    eye = jnp.eye(HEADS_PER_GROUP, dtype=w.dtype)
    bd = wg[:, :, :, None, :] * eye[None, :, None, :, None]
    return bd.reshape(N_GROUPS, LRU_GROUP, LRU_GROUP).astype(BF16)


def _head_blocks(bd):
    b5 = bd.reshape(N_GROUPS, HEADS_PER_GROUP, LRU_HEAD_DIM, HEADS_PER_GROUP, LRU_HEAD_DIM)
    blocks = [b5[:, h, :, h, :] for h in range(HEADS_PER_GROUP)]
    return jnp.stack(blocks, axis=1).reshape(LRU_HEADS, LRU_HEAD_DIM, LRU_HEAD_DIM)


def _w(lw, key, after):
    value = lw[key]
    return value(after) if callable(value) else value


def _layer_fwd(x, lw, tag):
    sv_rows = x.shape[0]
    z, h1 = _norm_in_proj(x, lw["g1"], _w(lw, "w_in_t", x), tm=min(TMM, sv_rows), tn=896, name=f"in_proj_{tag}")
    y_mix, hs = _mixer_fwd(z, _w(lw, "cw", z), lw["cb"], lw["wa_bd"], lw["wx_bd"], lw["ba"], lw["bx"], lw["lam"],
                           _w(lw, "scw", z), tile=MIX_TILE, name=f"mixer_fwd_{tag}")
    x2 = _mm_nn(y_mix, _w(lw, "w_out", y_mix), tm=min(TMM, sv_rows), tn=D_MODEL, tk=D_MIX, out_dtype=F32, name=f"out_proj_{tag}",
                residual=x)
    x3, h2, act, p, u = _ffn_block_fwd(x2, lw["g2"], _w(lw, "w_up_b", x2), _w(lw, "fcw", x2), _w(lw, "w_down", x2),
                                       tile=min(FFN_FWD_TILE, sv_rows), name=f"ffn_fwd_{tag}")
    saved = dict(x=x, h1=h1, z=z, y_mix=y_mix, hs=hs, x2=x2, h2=h2, p=p, u=u, act=act)
    return x3, saved


def _layer_bwd(dx3, dx3b, lw, sv, tag, put):
    sv_rows = dx3.shape[0]
    w_in_t, w_out, w_up_b, w_down = (_w(lw, k, dx3) for k in ("w_in_t", "w_out", "w_up_b", "w_down"))
    cw, scw, fcw = (_w(lw, k, dx3) for k in ("cw", "scw", "fcw"))
    g_down = _mm_tn(sv["act"], dx3b, tm=1024, tn=D_MODEL, tk=min(TKW, sv_rows), out_dtype=BF16, name=f"down_bwd_w_{tag}")
    dx2, dx2b, dg2, dp, dfcw = _ffn_block_bwd(dx3, dx3b, sv["p"], sv["u"], sv["x2"], lw["g2"], w_up_b, fcw, w_down,
                                              tile=min(FFN_BWD_TILE, sv_rows), name=f"ffn_bwd_{tag}",
                                              token=put("w_down", g_down))
    g_up = _mm_up_bwd_w(sv["h2"], dp, tm=D_MODEL, tk=min(TKW, sv_rows), name=f"up_bwd_w_{tag}")
    dy = _mm_nt(dx2b, w_out, tm=min(TMM, sv_rows), tn=768, tk=D_MODEL, out_dtype=F32, name=f"out_bwd_x_{tag}")
    dz, dcw, dvec, dwa, dwx, dscw = _mixer_bwd_rows(
        sv["z"], sv["hs"], dy, cw, lw["cb"], lw["wa_bd"], lw["wx_bd"], lw["ba"], lw["bx"], lw["lam"],
        scw, tile=MIX_TILE, name=f"mixer_bwd_{tag}", token=put("w_up_b", g_up))
    g_out = _mm_tn(sv["y_mix"], dx2b, tm=768, tn=D_MODEL, tk=min(TKW, sv_rows), out_dtype=BF16, name=f"out_bwd_w_{tag}",
                   token=dz)
    g_in_t = _mm_tn(dz, sv["h1"], tm=896, tn=D_MODEL, tk=min(TKW, sv_rows), out_dtype=BF16, name=f"in_bwd_w_{tag}",
                    token=put("w_out", g_out))
    dx, dxb, dg1 = _in_bwd_norm(dz, w_in_t, sv["x"], lw["g1"], dx2, tm=min(TMM, sv_rows), tk=896, name=f"in_bwd_x_{tag}",
                                token=put("w_in_t", g_in_t))
    small = dict(norm1_g=dg1[0], lru_conv_w=dcw[0:4], lru_conv_b=dvec[0], lru_wa=_head_blocks(dwa),
                 lru_ba=dvec[1], lru_wx=_head_blocks(dwx), lru_bx=dvec[2], lru_lambda=dvec[3],
                 sc_conv_w=dscw[0:3], norm2_g=dg2[0], ffn_conv_w=dfcw[:, 0:3, :])
    return dx, dxb, small


SMALL_ORDER = ("norm1_g", "lru_conv_w", "lru_conv_b", "lru_wa", "lru_ba", "lru_wx", "lru_bx", "lru_lambda",
               "sc_conv_w", "norm2_g", "ffn_conv_w")


def _local_step(x, tgt, layers, final_g, put):
    saved = []
    h = x
    for l in range(DEPTH):
        h, sv = _layer_fwd(h, layers[l], f"l{l}")
        saved.append(sv)
    loss_blk, dx, dxb, dgf = _loss_head(h, final_g, tgt, tm=TM, name="loss_head")
    smalls = [None] * DEPTH
    for l in reversed(range(DEPTH)):
        dx, dxb, smalls[l] = _layer_bwd(dx, dxb, layers[l], saved[l], f"l{l}", functools.partial(put, l))
    return loss_blk[0, 0], dx, smalls, dgf[0]


def kernel(x, norm1_g, w_in, lru_conv_w, lru_conv_b, lru_wa, lru_ba, lru_wx, lru_bx, lru_lambda, sc_conv_w, w_out, norm2_g, w_up, ffn_conv_w, w_down, final_g, loss_target, m_norm1_g, m_w_in, m_lru_conv_w, m_lru_conv_b, m_lru_wa, m_lru_ba, m_lru_wx, m_lru_bx, m_lru_lambda, m_sc_conv_w, m_w_out, m_norm2_g, m_w_up, m_ffn_conv_w, m_w_down, m_final_g, v_norm1_g, v_w_in, v_lru_conv_w, v_lru_conv_b, v_lru_wa, v_lru_ba, v_lru_wx, v_lru_bx, v_lru_lambda, v_sc_conv_w, v_w_out, v_norm2_g, v_w_up, v_ffn_conv_w, v_w_down, v_final_g):
    names = ["norm1_g", "w_in", "lru_conv_w", "lru_conv_b", "lru_wa", "lru_ba", "lru_wx", "lru_bx", "lru_lambda",
             "sc_conv_w", "w_out", "norm2_g", "w_up", "ffn_conv_w", "w_down", "final_g"]
    w = dict(zip(names, [norm1_g, w_in, lru_conv_w, lru_conv_b, lru_wa, lru_ba, lru_wx, lru_bx, lru_lambda,
                         sc_conv_w, w_out, norm2_g, w_up, ffn_conv_w, w_down, final_g]))
    m = dict(zip(names, [m_norm1_g, m_w_in, m_lru_conv_w, m_lru_conv_b, m_lru_wa, m_lru_ba, m_lru_wx, m_lru_bx,
                         m_lru_lambda, m_sc_conv_w, m_w_out, m_norm2_g, m_w_up, m_ffn_conv_w, m_w_down, m_final_g]))
    v = dict(zip(names, [v_norm1_g, v_w_in, v_lru_conv_w, v_lru_conv_b, v_lru_wa, v_lru_ba, v_lru_wx, v_lru_bx,
                         v_lru_lambda, v_sc_conv_w, v_w_out, v_norm2_g, v_w_up, v_ffn_conv_w, v_w_down, v_final_g]))
    my_id = 4 * lax.axis_index("x") + 2 * lax.axis_index("y") + lax.axis_index("c")

    taps = jnp.zeros((DEPTH, 16, 768), F32)
    taps = taps.at[:, 0:4, 0:128].set(lru_conv_w).at[:, 4:7, 0:64].set(sc_conv_w).at[:, 8:11, :].set(ffn_conv_w)
    shards = {}
    for l in range(DEPTH):
        shards[f"w_in_t{l}"] = jnp.swapaxes(w_in[l], 0, 1).astype(BF16)
        if l == 0:
            shards["taps"] = taps.reshape(DEPTH * 16, 768)
        shards[f"w_out{l}"] = w_out[l].astype(BF16)
        shards[f"w_up_b{l}"] = w_up[l].astype(BF16)
        shards[f"w_down{l}"] = w_down[l].astype(BF16)
    ids = iter(range(18))
    got = {}
    chain = [None]
    for group in (("w_in_t0", "taps"), ("w_out0",), ("w_up_b0",), ("w_down0",),
                  ("w_in_t1",), ("w_out1",), ("w_up_b1",), ("w_down1",)):
        lands, chain[0] = _sequencer_copies([shards[k] for k in group], gather=True, name=f"gather_{group[0]}",
                                            collective_id=next(ids), after=None)
        got.update(zip(group, lands))

    def fetch(key, after):
        return got[key]

    def tap_rows(l, lo, hi, width, after):
        tl = fetch("taps", after).reshape(N_DEV, DEPTH, 16, 768)[:, l, lo:hi, 0:width]
        return jnp.transpose(tl, (1, 0, 2)).reshape(hi - lo, N_DEV * width)

    layers = []
    for l in range(DEPTH):
        layers.append(dict(
            g1=norm1_g[l], g2=norm2_g[l], cb=lru_conv_b[l], ba=lru_ba[l], bx=lru_bx[l], lam=lru_lambda[l],
            wa_bd=_block_diag(lru_wa[l]), wx_bd=_block_diag(lru_wx[l]),
            cw=functools.partial(tap_rows, l, 0, 4, 128), scw=functools.partial(tap_rows, l, 4, 7, 64),
            fcw=lambda after, l=l: tap_rows(l, 8, 11, 768, after).reshape(3, 2, D_FF).transpose(1, 0, 2),
            w_in_t=lambda after, l=l: fetch(f"w_in_t{l}", after).reshape(D_IN, D_MODEL),
            w_out=lambda after, l=l: fetch(f"w_out{l}", after).reshape(D_MIX, D_MODEL),
            w_up_b=lambda after, l=l: fetch(f"w_up_b{l}", after),
            w_down=lambda after, l=l: fetch(f"w_down{l}", after).reshape(D_FF, D_MODEL)))

    scatter_handles = {}

    def put(l, key, grad):
        blocks = grad if grad.ndim == 3 else grad.reshape(N_DEV, grad.shape[0] // N_DEV, grad.shape[1])
        (scatter_handles[(l, key)],), chain[0] = _sequencer_copies(
            [blocks], gather=False, name=f"scatter_{key}{l}", collective_id=next(ids), after=chain[0])
        return blocks

    loss_local, dx, smalls, dgf = _local_step(x[0], loss_target[0], layers, final_g, put)

    parts = []
    for l in range(DEPTH):
        for key in ("w_in_t", "w_out", "w_up_b", "w_down"):
            parts.append(scatter_handles[(l, key)])

    flat = [smalls[l][k].reshape(-1) for l in range(DEPTH) for k in SMALL_ORDER] + [dgf.reshape(-1)]
    flat.append(jnp.broadcast_to(loss_local, (128,)))
    sizes = [f.shape[0] for f in flat]
    total = sum(sizes)
    rows = -(-total // (N_DEV * 128 * 8)) * 8
    flat.append(jnp.zeros((N_DEV * rows * 128 - total,), F32))
    (small_parts,), chain[0] = _sequencer_copies([jnp.concatenate(flat).reshape(N_DEV, rows, 128)], gather=False,
                                                 name="scatter_small", collective_id=next(ids), after=chain[0])
    (small_all,), _ = _sequencer_copies([_sum_parts(small_parts, name="sum_small")], gather=True,
                                        name="gather_small", collective_id=next(ids), after=chain[0])
    small_sum = small_all.reshape(-1)
    small_g, off = [], 0
    for sz in sizes:
        small_g.append(small_sum[off:off + sz])
        off += sz
    gs = {}
    for l in range(DEPTH):
        for i, k in enumerate(SMALL_ORDER):
            gs.setdefault(k, []).append(small_g[l * len(SMALL_ORDER) + i])
    g_final = small_g[-2]
    loss = small_g[-1][0]

    grads, deltas, new_m, new_v = {}, {}, {}, {}
    grads["w_in"] = jnp.stack([jnp.swapaxes(_sum_parts(parts[4 * l], name=f"sum_w_in_l{l}"), 0, 1)
                               for l in range(DEPTH)])
    for slot, k in ((1, "w_out"), (2, "w_up"), (3, "w_down")):
        grads[k], deltas[k], new_m[k], new_v[k] = _sum_adamw(
            [parts[4 * l + slot] for l in range(DEPTH)], w[k], m[k], v[k], name=f"adamw_{k}")
    for k in ("norm1_g", "lru_conv_b", "lru_ba", "lru_bx", "lru_lambda", "norm2_g"):
        grads[k] = jnp.stack(gs[k]).reshape(DEPTH, -1)
    for k in ("lru_wa", "lru_wx"):
        grads[k] = jnp.stack(gs[k]).reshape(DEPTH, LRU_HEADS, LRU_HEAD_DIM, LRU_HEAD_DIM)
    grads["final_g"] = g_final
    cw_full = jnp.stack(gs["lru_conv_w"]).reshape(DEPTH, 4, N_DEV, 128)
    grads["lru_conv_w"] = lax.dynamic_index_in_dim(cw_full, my_id, axis=2, keepdims=False)
    scw_full = jnp.stack(gs["sc_conv_w"]).reshape(DEPTH, 3, N_DEV, 64)
    grads["sc_conv_w"] = lax.dynamic_index_in_dim(scw_full, my_id, axis=2, keepdims=False)
    fcw_full = jnp.stack(gs["ffn_conv_w"]).reshape(DEPTH, 2, 3, D_FF).transpose(0, 2, 1, 3).reshape(DEPTH, 3, N_DEV, 768)
    grads["ffn_conv_w"] = lax.dynamic_index_in_dim(fcw_full, my_id, axis=2, keepdims=False)

    for k in names:
        if k in deltas:
            continue
        shape = w[k].shape
        cols = shape[-1]
        as2d = lambda a: a.reshape(-1, cols)
        d, nm, nv = _adamw(as2d(w[k]), as2d(grads[k]), as2d(m[k]), as2d(v[k]), name=f"adamw_{k}")
        deltas[k], new_m[k], new_v[k] = d.reshape(shape), nm.reshape(shape), nv.reshape(shape)

    return (loss, dx[None], *[grads[k] for k in names], *[deltas[k] for k in names],
            *[new_m[k] for k in names], *[new_v[k] for k in names])
```

```python
import functools
import math

import jax
import jax.numpy as jnp
from jax import lax
from jax.experimental import pallas as pl
from jax.experimental.pallas import tpu as pltpu
from jax.experimental.pallas import tpu_sc as plsc

F32 = jnp.float32
BF16 = jnp.bfloat16

N_DEV = 8
DEPTH = 2
D_MODEL = 1024
D_LRU = 1024
D_SC = 512
D_MIX = D_LRU + D_SC
D_IN = 2 * D_LRU + 3 * D_SC
D_FF = 3072
LRU_HEADS = 16
LRU_HEAD_DIM = 64
LRU_GROUP = 256
N_GROUPS = D_LRU // LRU_GROUP
HEADS_PER_GROUP = LRU_GROUP // LRU_HEAD_DIM
RG_C = 8.0
EPS = 1e-6
HALO = 8

ADAM_LR = 0.001
ADAM_B1 = 0.9
ADAM_B2 = 0.999
ADAM_EPS = 1e-08
ADAM_WD = 0.01
ADAM_STEP = 10

GELU_C = math.sqrt(2.0 / math.pi)
GELU_A = 0.044715

VMEM_LIMIT = 56 * 1024 * 1024
MESH = pl.DeviceIdType.MESH


def _params(*sem):
    return pltpu.CompilerParams(dimension_semantics=tuple(sem) if sem else None,
                                vmem_limit_bytes=VMEM_LIMIT)


def _gelu_parts(x):
    x2 = x * x
    t = jnp.tanh(GELU_C * (x + GELU_A * x * x2))
    half = 0.5 * (1.0 + t)
    g = x * half
    dg = half + 0.5 * x * (1.0 - t * t) * (GELU_C * (1.0 + 3.0 * GELU_A * x2))
    return g, dg


def _gelu(x):
    t = jnp.tanh(GELU_C * (x + GELU_A * x * x * x))
    return 0.5 * x * (1.0 + t)


def _sigmoid(x):
    return 0.5 * jnp.tanh(0.5 * x) + 0.5


def _softplus(x):
    e = jnp.exp(-jnp.abs(x))
    u = 1.0 + e
    log1p_e = jnp.where(u == 1.0, e, jnp.log(u) * (e / (u - 1.0)))
    return jnp.maximum(x, 0.0) + log1p_e


def _rms(x):
    ms = jnp.mean(x * x, axis=-1, keepdims=True)
    return lax.rsqrt(ms + EPS)


def _dot(a, b, dims):
    return lax.dot_general(a, b, (dims, ((), ())), preferred_element_type=F32)


NN = ((1,), (0,))
NT = ((1,), (1,))
TN = ((0,), (0,))


def _matmul(a, b, *, dims, grid, a_spec, b_spec, o_spec, out_shape, acc_shape, name,
            residual=None, r_spec=None, token=None):
    nk = grid[2]

    def body(*refs):
        a_ref, b_ref = refs[0], refs[1]
        r_ref = refs[2] if residual is not None else None
        o_ref = refs[2 + (residual is not None) + (token is not None)]
        prod = _dot(a_ref[...].astype(BF16), b_ref[...].astype(BF16), dims)

        def finish(total):
            if r_ref is not None:
                total = total + r_ref[...]
            o_ref[...] = total.astype(o_ref.dtype)

        if nk == 1:
            finish(prod)
            return
        acc_ref = refs[-1]
        k = pl.program_id(2)

        @pl.when(k == 0)
        def _():
            acc_ref[...] = prod

        @pl.when(jnp.logical_and(k > 0, k < nk - 1))
        def _():
            acc_ref[...] += prod

        @pl.when(k == nk - 1)
        def _():
            finish(acc_ref[...] + prod)

    in_specs = [a_spec, b_spec]
    args = [a, b]
    if residual is not None:
        in_specs.append(r_spec)
        args.append(residual)
    if token is not None:
        in_specs.append(pl.BlockSpec(memory_space=pl.ANY))
        args.append(token)
    return pl.pallas_call(
        body, name=name, grid=grid, in_specs=in_specs, out_specs=o_spec, out_shape=out_shape,
        scratch_shapes=[pltpu.VMEM(acc_shape, F32)] if nk > 1 else [],
        compiler_params=_params("parallel", "parallel", "arbitrary"),
    )(*args)


def _mm_nn(a, b, *, tm, tn, tk, out_dtype, name, residual=None, token=None):
    m, kd = a.shape
    n = b.shape[1]
    return _matmul(
        a, b, dims=NN, grid=(m // tm, n // tn, kd // tk),
        a_spec=pl.BlockSpec((tm, tk), lambda i, j, k: (i, k)),
        b_spec=pl.BlockSpec((tk, tn), lambda i, j, k: (k, j)),
        o_spec=pl.BlockSpec((tm, tn), lambda i, j, k: (i, j)),
        out_shape=jax.ShapeDtypeStruct((m, n), out_dtype), acc_shape=(tm, tn), name=name,
        residual=residual, r_spec=pl.BlockSpec((tm, tn), lambda i, j, k: (i, j)), token=token)


def _mm_nt(a, b, *, tm, tn, tk, out_dtype, name):
    m, kd = a.shape
    n = b.shape[0]
    return _matmul(
        a, b, dims=NT, grid=(m // tm, n // tn, kd // tk),
        a_spec=pl.BlockSpec((tm, tk), lambda i, j, k: (i, k)),
        b_spec=pl.BlockSpec((tn, tk), lambda i, j, k: (j, k)),
        o_spec=pl.BlockSpec((tm, tn), lambda i, j, k: (i, j)),
        out_shape=jax.ShapeDtypeStruct((m, n), out_dtype), acc_shape=(tm, tn), name=name)


def _mm_tn(a, b, *, tm, tn, tk, out_dtype, name, token=None):
    kd, m = a.shape
    n = b.shape[1]
    return _matmul(
        a, b, dims=TN, grid=(m // tm, n // tn, kd // tk),
        a_spec=pl.BlockSpec((tk, tm), lambda i, j, k: (k, i)),
        b_spec=pl.BlockSpec((tk, tn), lambda i, j, k: (k, j)),
        o_spec=pl.BlockSpec((tm, tn), lambda i, j, k: (i, j)),
        out_shape=jax.ShapeDtypeStruct((m, n), out_dtype), acc_shape=(tm, tn), name=name, token=token)


def _mm_up_bwd_w(h2, dp, *, tm, tk, name):
    s = h2.shape[0]
    nb = D_FF * 2 // N_DEV
    per_half = D_FF // nb
    return _matmul(
        h2, dp, dims=TN, grid=(D_MODEL // tm, N_DEV, s // tk),
        a_spec=pl.BlockSpec((tk, tm), lambda i, j, k: (k, i)),
        b_spec=pl.BlockSpec((None, tk, nb), lambda i, j, k: (j // per_half, k, j % per_half)),
        o_spec=pl.BlockSpec((None, tm, nb), lambda i, j, k: (j, i, 0)),
        out_shape=jax.ShapeDtypeStruct((N_DEV, D_MODEL, nb), BF16), acc_shape=(tm, nb), name=name)


def _behind(token):
    return jnp.zeros((8, 128), F32) if token is None else token


def _norm_in_proj(x, g, w_in_t, *, tm, tn, name):
    s, d = x.shape
    n = w_in_t.shape[0]

    def body(x_ref, g_ref, w_ref, z_ref, h_ref):
        @pl.when(pl.program_id(1) == 0)
        def _():
            xv = x_ref[...]
            h_ref[...] = (xv * _rms(xv) * g_ref[...]).astype(BF16)

        z_ref[...] = _dot(h_ref[...], w_ref[...], NT)

    return pl.pallas_call(
        body, name=name, grid=(s // tm, n // tn),
        in_specs=[pl.BlockSpec((tm, d), lambda i, j: (i, 0)), pl.BlockSpec((1, d), lambda i, j: (0, 0)),
                  pl.BlockSpec((tn, d), lambda i, j: (j, 0))],
        out_specs=[pl.BlockSpec((tm, tn), lambda i, j: (i, j)), pl.BlockSpec((tm, d), lambda i, j: (i, 0))],
        out_shape=[jax.ShapeDtypeStruct((s, n), F32), jax.ShapeDtypeStruct((s, d), BF16)],
        compiler_params=_params("parallel", "arbitrary"),
    )(x, g.reshape(1, d), w_in_t)


def _in_bwd_norm(dz, w_in_t, x, g, dres, *, tm, tk, name, token=None):
    s, kd = dz.shape
    d = w_in_t.shape[1]
    nk = kd // tk

    def body(dz_ref, w_ref, x_ref, g_ref, dres_ref, token_ref, dx_ref, dxb_ref, dg_ref, acc_ref):
        i = pl.program_id(0)
        k = pl.program_id(1)

        @pl.when(jnp.logical_and(i == 0, k == 0))
        def _():
            dg_ref[...] = jnp.zeros_like(dg_ref)

        acc_ref[...] = _dot(dz_ref[...], w_ref[...], NN) + jnp.where(k > 0, acc_ref[...], 0.0)

        @pl.when(k == nk - 1)
        def _():
            dh = acc_ref[...]
            xv = x_ref[...]
            rstd = _rms(xv)
            n = xv * rstd
            dn = dh * g_ref[...]
            dx = dres_ref[...] + rstd * (dn - n * jnp.mean(dn * n, axis=-1, keepdims=True))
            dx_ref[...] = dx
            dxb_ref[...] = dx.astype(BF16)
            dg_ref[0:1, :] += jnp.sum(dh * n, axis=0, keepdims=True)

    row = pl.BlockSpec((tm, d), lambda i, k: (i, 0))
    return pl.pallas_call(
        body, name=name, grid=(s // tm, nk),
        in_specs=[pl.BlockSpec((tm, tk), lambda i, k: (i, k)), pl.BlockSpec((tk, d), lambda i, k: (k, 0)),
                  row, pl.BlockSpec((1, d), lambda i, k: (0, 0)), row, pl.BlockSpec(memory_space=pl.ANY)],
        out_specs=[row, row, pl.BlockSpec((8, d), lambda i, k: (0, 0))],
        out_shape=[jax.ShapeDtypeStruct((s, d), F32), jax.ShapeDtypeStruct((s, d), BF16),
                   jax.ShapeDtypeStruct((8, d), F32)],
        scratch_shapes=[pltpu.VMEM((tm, d), F32)],
        compiler_params=_params("arbitrary", "arbitrary"),
    )(dz, w_in_t, x, g.reshape(1, d), dres, _behind(token))


def _loss_head(x, g, tgt, *, tm, name):
    s, d = x.shape

    def body(x_ref, g_ref, t_ref, loss_ref, dx_ref, dxb_ref, dg_ref):
        @pl.when(pl.program_id(0) == 0)
        def _():
            dg_ref[...] = jnp.zeros_like(dg_ref)
            loss_ref[...] = jnp.zeros_like(loss_ref)

        xv = x_ref[...]
        gv = g_ref[...]
        rstd = _rms(xv)
        n = xv * rstd
        e = n * gv - t_ref[...]
        part = 0.5 * jnp.sum(jnp.mean(e * e, axis=-1, keepdims=True), axis=0, keepdims=True)
        loss_ref[...] += jnp.broadcast_to(part, loss_ref.shape)
        dy = e * (1.0 / d)
        dn = dy * gv
        dx = rstd * (dn - n * jnp.mean(dn * n, axis=-1, keepdims=True))
        dx_ref[...] = dx
        dxb_ref[...] = dx.astype(BF16)
        dg_ref[0:1, :] += jnp.sum(dy * n, axis=0, keepdims=True)

    return pl.pallas_call(
        body, name=name, grid=(s // tm,),
        in_specs=[pl.BlockSpec((tm, d), lambda i: (i, 0)), pl.BlockSpec((1, d), lambda i: (0, 0)),
                  pl.BlockSpec((tm, d), lambda i: (i, 0))],
        out_specs=[pl.BlockSpec((8, 128), lambda i: (0, 0)), pl.BlockSpec((tm, d), lambda i: (i, 0)),
                   pl.BlockSpec((tm, d), lambda i: (i, 0)), pl.BlockSpec((8, d), lambda i: (0, 0))],
        out_shape=[jax.ShapeDtypeStruct((8, 128), F32), jax.ShapeDtypeStruct((s, d), F32),
                   jax.ShapeDtypeStruct((s, d), BF16), jax.ShapeDtypeStruct((8, d), F32)],
        compiler_params=_params("arbitrary"),
    )(x, g.reshape(1, d), tgt)


def _scan_rows(a_ref, b_ref, h_ref, carry, *, rows, reverse):
    width = a_ref.shape[1]
    n_chunks = rows // 8
    row = lax.broadcasted_iota(jnp.int32, (8, width), 0)

    def step(ci, carry):
        chunk = (n_chunks - 1 - ci) if reverse else ci
        off = pl.multiple_of(chunk * 8, 8)
        av = a_ref[pl.ds(off, 8), :]
        bv = b_ref[pl.ds(off, 8), :]
        for sh in (1, 2, 4):
            if reverse:
                a_sh = pltpu.roll(av, 8 - sh, 0)
                b_sh = pltpu.roll(bv, 8 - sh, 0)
                m = row < 8 - sh
            else:
                a_sh = pltpu.roll(av, sh, 0)
                b_sh = pltpu.roll(bv, sh, 0)
                m = row >= sh
            bv = jnp.where(m, av * b_sh + bv, bv)
            av = jnp.where(m, av * a_sh, av)
        h = av * carry + bv
        h_ref[pl.ds(off, 8), :] = h
        return h[0:1, :] if reverse else h[7:8, :]

    return lax.fori_loop(0, n_chunks, step, carry)


P_CB, P_BA, P_BX, P_DECAY, P_DSP, N_PAR = 4, 5, 6, 7, 8, 9


def _spread_mixer_params(par, par_sc, cw_ref, cb_ref, ba_ref, bx_ref, lam_ref, scw_ref):
    rows = par.shape[1:]
    for k in range(4):
        par[k] = jnp.broadcast_to(cw_ref[k:k + 1, :], rows)
    par[P_CB] = jnp.broadcast_to(cb_ref[...], rows)
    par[P_BA] = jnp.broadcast_to(ba_ref[...], rows)
    par[P_BX] = jnp.broadcast_to(bx_ref[...], rows)
    par[P_DECAY] = jnp.broadcast_to(-RG_C * _softplus(-lam_ref[...]), rows)
    par[P_DSP] = jnp.broadcast_to(-_sigmoid(-lam_ref[...]), rows)
    for k in range(3):
        par_sc[k] = jnp.broadcast_to(scw_ref[k:k + 1, :], par_sc.shape[1:])


def _gates_rows(pre_r, pre_i, par):
    r = _sigmoid(pre_r + par[P_BA])
    ig = _sigmoid(pre_i + par[P_BX])
    log_a = r * par[P_DECAY]
    a = jnp.exp(log_a)
    one_minus_a2 = -jnp.tanh(log_a) * (a * a + 1.0)
    return r, ig, a, jnp.sqrt(one_minus_a2), one_minus_a2


def _mixer_fwd(z, cw, cb, wa_bd, wx_bd, ba, bx, lam, scw, *, tile, name):
    s = z.shape[0]
    n_tiles = s // tile

    rows_of = lambda r0: slice(r0, r0 + FFN_ROWS)
    col_gate, col_sb, col_sc, col_sx = (slice(D_LRU, 2 * D_LRU), slice(2 * D_LRU, 2 * D_LRU + D_SC),
                                        slice(2 * D_LRU + D_SC, 2 * D_LRU + 2 * D_SC), slice(2 * D_LRU + 2 * D_SC, D_IN))

    def body(z_ref, cw_ref, cb_ref, wa_ref, wx_ref, ba_ref, bx_ref, lam_ref, scw_ref,
             y_ref, hs_ref, par, par_sc, lx_s, lxb_s, a_s, b_s, car_lx, car_q, h_car):
        i = pl.program_id(0)

        @pl.when(i == 0)
        def _():
            car_lx[...] = jnp.zeros_like(car_lx)
            car_q[...] = jnp.zeros_like(car_q)
            h_car[...] = jnp.zeros_like(h_car)
            _spread_mixer_params(par, par_sc, cw_ref, cb_ref, ba_ref, bx_ref, lam_ref, scw_ref)

        before_lx, before_q = car_lx[...], car_q[...]
        for r0 in range(0, tile, FFN_ROWS):
            cur = z_ref[rows_of(r0), 0:D_LRU]
            lx = par[P_CB] + par[3] * cur
            for k in range(3):
                lx = lx + par[k] * _rows_from(before_lx, cur, FFN_ROWS - 3 + k)
            lx_s[rows_of(r0), :] = lx
            lxb_s[rows_of(r0), :] = lx.astype(BF16)
            before_lx = cur
            q = z_ref[rows_of(r0), col_sc] * z_ref[rows_of(r0), col_sx]
            cq = par_sc[2] * q
            for k in range(2):
                cq = cq + par_sc[k] * _rows_from(before_q, q, FFN_ROWS - 2 + k)
            y_ref[rows_of(r0), D_LRU:D_MIX] = (z_ref[rows_of(r0), col_sb] * cq).astype(BF16)
            before_q = q
        car_lx[...] = before_lx
        car_q[...] = before_q

        for g in range(N_GROUPS):
            cols = slice(g * LRU_GROUP, (g + 1) * LRU_GROUP)
            a_s[:, cols] = _dot(lxb_s[:, cols], wa_ref[g], NN)
            b_s[:, cols] = _dot(lxb_s[:, cols], wx_ref[g], NN)

        for r0 in range(0, tile, FFN_ROWS):
            _, ig, a, mult, _ = _gates_rows(a_s[rows_of(r0), :], b_s[rows_of(r0), :], par)
            a_s[rows_of(r0), :] = a
            b_s[rows_of(r0), :] = mult * (ig * lx_s[rows_of(r0), :])
        h_car[0:1, :] = _scan_rows(a_s, b_s, hs_ref, h_car[0:1, :], rows=tile, reverse=False)

        for r0 in range(0, tile, FFN_ROWS):
            y_ref[rows_of(r0), 0:D_LRU] = (hs_ref[rows_of(r0), :] * _gelu(z_ref[rows_of(r0), col_gate])).astype(BF16)

    full = lambda shape: pl.BlockSpec(shape, lambda i: (0,) * len(shape))
    return pl.pallas_call(
        body, name=name, grid=(n_tiles,),
        in_specs=[pl.BlockSpec((tile, D_IN), lambda i: (i, 0)),
                  full((4, D_LRU)), full((1, D_LRU)),
                  full((N_GROUPS, LRU_GROUP, LRU_GROUP)), full((N_GROUPS, LRU_GROUP, LRU_GROUP)),
                  full((1, D_LRU)), full((1, D_LRU)), full((1, D_LRU)), full((3, D_SC))],
        out_specs=[pl.BlockSpec((tile, D_MIX), lambda i: (i, 0)), pl.BlockSpec((tile, D_LRU), lambda i: (i, 0))],
        out_shape=[jax.ShapeDtypeStruct((s, D_MIX), BF16), jax.ShapeDtypeStruct((s, D_LRU), F32)],
        scratch_shapes=[pltpu.VMEM((N_PAR, FFN_ROWS, D_LRU), F32), pltpu.VMEM((3, FFN_ROWS, D_SC), F32),
                        pltpu.VMEM((tile, D_LRU), F32), pltpu.VMEM((tile, D_LRU), BF16),
                        pltpu.VMEM((tile, D_LRU), F32), pltpu.VMEM((tile, D_LRU), F32),
                        pltpu.VMEM((FFN_ROWS, D_LRU), F32), pltpu.VMEM((FFN_ROWS, D_SC), F32),
                        pltpu.VMEM((8, D_LRU), F32)],
        compiler_params=_params("arbitrary"),
    )(z, cw, cb.reshape(1, -1), wa_bd, wx_bd, ba.reshape(1, -1), bx.reshape(1, -1), lam.reshape(1, -1), scw)


def _fold8(x):
    return sum(x[q:q + 8] for q in range(0, x.shape[0], 8))


def _mixer_bwd_rows(z, hs, dy, cw, cb, wa_bd, wx_bd, ba, bx, lam, scw, *, tile, name, token=None):
    s = z.shape[0]
    n_tiles = s // tile
    per8 = tile // 8
    rows_of = lambda r0: slice(r0, r0 + FFN_ROWS)
    col_gate, col_sb, col_sc, col_sx = (slice(D_LRU, 2 * D_LRU), slice(2 * D_LRU, 2 * D_LRU + D_SC),
                                        slice(2 * D_LRU + D_SC, 2 * D_LRU + 2 * D_SC), slice(2 * D_LRU + 2 * D_SC, D_IN))
    up = range(0, tile, FFN_ROWS)
    down = range(tile - FFN_ROWS, -1, -FFN_ROWS)
    A_CB, A_BA, A_BX, A_SP, A_CW = 0, 1, 2, 3, 4

    def body(z_ref, zp_ref, hs_ref, hsp_ref, dy_ref, cw_ref, cb_ref, wa_ref, wx_ref, ba_ref, bx_ref, lam_ref, scw_ref,
             token_ref, dz_ref, dcw_ref, dvec_ref, dwa_ref, dwx_ref, dscw_ref,
             par, par_sc, lx_s, lxb_s, cq_s, pr_s, pi_s, r_s, ig_s, a_s, mult_s, inv_s, ash_s, b_s, lam_s,
             dlx_s, dpr_b, dpi_b, back_s, car_a, car_dlx, car_dcq, l_car, acc, acc_sc):
        i = pl.program_id(0)

        @pl.when(i == 0)
        def _():
            for ref in (dwa_ref, dwx_ref, l_car, car_a, car_dlx, car_dcq, acc, acc_sc):
                ref[...] = jnp.zeros_like(ref)
            _spread_mixer_params(par, par_sc, cw_ref, cb_ref, ba_ref, bx_ref, lam_ref, scw_ref)

        keep = jnp.where(i == n_tiles - 1, 0.0, 1.0)
        zeros8 = lambda n: jnp.zeros((8, n), F32)

        before_lx = jnp.concatenate([zeros8(D_LRU), zp_ref[:, 0:D_LRU] * keep], axis=0)
        before_q = jnp.concatenate([zeros8(D_SC), zp_ref[:, col_sc] * zp_ref[:, col_sx] * keep], axis=0)
        for r0 in up:
            cur = z_ref[rows_of(r0), 0:D_LRU]
            lx = par[P_CB] + par[3] * cur
            for k in range(3):
                lx = lx + par[k] * _rows_from(before_lx, cur, FFN_ROWS - 3 + k)
            lx_s[rows_of(r0), :] = lx
            lxb_s[rows_of(r0), :] = lx.astype(BF16)
            before_lx = cur
            q = z_ref[rows_of(r0), col_sc] * z_ref[rows_of(r0), col_sx]
            cq = par_sc[2] * q
            for k in range(2):
                cq = cq + par_sc[k] * _rows_from(before_q, q, FFN_ROWS - 2 + k)
            cq_s[rows_of(r0), :] = cq
            before_q = q

        for g in range(N_GROUPS):
            cols = slice(g * LRU_GROUP, (g + 1) * LRU_GROUP)
            pr_s[:, cols] = _dot(lxb_s[:, cols], wa_ref[g], NN)
            pi_s[:, cols] = _dot(lxb_s[:, cols], wx_ref[g], NN)

        after_a = car_a[...]
        for r0 in down:
            r, ig, a, mult, one_minus_a2 = _gates_rows(pr_s[rows_of(r0), :], pi_s[rows_of(r0), :], par)
            r_s[rows_of(r0), :] = r
            ig_s[rows_of(r0), :] = ig
            a_s[rows_of(r0), :] = a
            mult_s[rows_of(r0), :] = mult
            inv_s[rows_of(r0), :] = lax.rsqrt(one_minus_a2)
            ash_s[rows_of(r0), :] = _rows_from(a, after_a, 1)
            after_a = a
            ge, dge = _gelu_parts(z_ref[rows_of(r0), col_gate])
            dy_lru = dy_ref[rows_of(r0), 0:D_LRU]
            dz_ref[rows_of(r0), col_gate] = (dy_lru * hs_ref[rows_of(r0), :] * dge).astype(BF16)
            b_s[rows_of(r0), :] = dy_lru * ge
        car_a[...] = after_a
        l_car[0:1, :] = _scan_rows(ash_s, b_s, lam_s, l_car[0:1, :], rows=tile, reverse=True)

        before_h = jnp.concatenate([zeros8(D_LRU), hsp_ref[...] * keep], axis=0)
        for r0 in up:
            lv = lam_s[rows_of(r0), :]
            h_here = hs_ref[rows_of(r0), :]
            lx, r, ig, a = lx_s[rows_of(r0), :], r_s[rows_of(r0), :], ig_s[rows_of(r0), :], a_s[rows_of(r0), :]
            mult = mult_s[rows_of(r0), :]
            da = lv * _rows_from(before_h, h_here, FFN_ROWS - 1)
            before_h = h_here
            d_mult = lv * ig * lx
            d_i = lv * mult * lx
            dlx_s[rows_of(r0), :] = lv * mult * ig
            dlog_a = da * a - d_mult * (a * a) * inv_s[rows_of(r0), :]
            dpre_r = dlog_a * par[P_DECAY] * r * (1.0 - r)
            dpre_i = d_i * ig * (1.0 - ig)
            acc[A_BA] += _fold8(dpre_r)
            acc[A_BX] += _fold8(dpre_i)
            acc[A_SP] += _fold8(dlog_a * r)
            dpr_b[rows_of(r0), :] = dpre_r.astype(BF16)
            dpi_b[rows_of(r0), :] = dpre_i.astype(BF16)

        for g in range(N_GROUPS):
            cols = slice(g * LRU_GROUP, (g + 1) * LRU_GROUP)
            dwa_ref[g] += _dot(lxb_s[:, cols], dpr_b[:, cols], TN)
            dwx_ref[g] += _dot(lxb_s[:, cols], dpi_b[:, cols], TN)
            back_s[:, cols] = _dot(dpr_b[:, cols], wa_ref[g], NT) + _dot(dpi_b[:, cols], wx_ref[g], NT)

        after_dlx, after_dcq = car_dlx[...], car_dcq[...]
        for r0 in down:
            dlx = dlx_s[rows_of(r0), :] + back_s[rows_of(r0), :]
            lxp = z_ref[rows_of(r0), 0:D_LRU]
            acc[A_CB] += _fold8(dlx)
            acc[A_CW + 3] += _fold8(dlx * lxp)
            dlxp = par[3] * dlx
            for sh in range(1, 4):
                below = _rows_from(dlx, after_dlx, sh)
                dlxp = dlxp + par[3 - sh] * below
                acc[A_CW + 3 - sh] += _fold8(below * lxp)
            dz_ref[rows_of(r0), 0:D_LRU] = dlxp.astype(BF16)
            after_dlx = dlx

            dy_sc = dy_ref[rows_of(r0), D_LRU:D_MIX]
            sb, sc, sx = z_ref[rows_of(r0), col_sb], z_ref[rows_of(r0), col_sc], z_ref[rows_of(r0), col_sx]
            dz_ref[rows_of(r0), col_sb] = (dy_sc * cq_s[rows_of(r0), :]).astype(BF16)
            dcq = dy_sc * sb
            q = sc * sx
            acc_sc[2] += _fold8(dcq * q)
            dq = par_sc[2] * dcq
            for sh in range(1, 3):
                below = _rows_from(dcq, after_dcq, sh)
                dq = dq + par_sc[2 - sh] * below
                acc_sc[2 - sh] += _fold8(below * q)
            dz_ref[rows_of(r0), col_sc] = (dq * sx).astype(BF16)
            dz_ref[rows_of(r0), col_sx] = (dq * sc).astype(BF16)
            after_dcq = dcq
        car_dlx[...] = after_dlx
        car_dcq[...] = after_dcq

        @pl.when(i == n_tiles - 1)
        def _():
            total = lambda x: jnp.sum(x, axis=0, keepdims=True)
            dcw_ref[...] = jnp.zeros_like(dcw_ref)
            dvec_ref[...] = jnp.zeros_like(dvec_ref)
            dscw_ref[...] = jnp.zeros_like(dscw_ref)
            for k in range(4):
                dcw_ref[k:k + 1, :] = total(acc[A_CW + k])
            for k in range(3):
                dvec_ref[k:k + 1, :] = total(acc[k])
                dscw_ref[k:k + 1, :] = total(acc_sc[k])
            dvec_ref[3:4, :] = total(acc[A_SP]) * (-RG_C) * par[P_DSP][0:1, :]

    rev = lambda i: n_tiles - 1 - i
    prev8 = lambda i: jnp.maximum(rev(i) * per8 - 1, 0)
    full = lambda shape: pl.BlockSpec(shape, lambda i: (0,) * len(shape))
    wide = lambda rows, dt=F32: pltpu.VMEM((rows, D_LRU), dt)
    return pl.pallas_call(
        body, name=name, grid=(n_tiles,),
        in_specs=[pl.BlockSpec((tile, D_IN), lambda i: (rev(i), 0)),
                  pl.BlockSpec((HALO, D_IN), lambda i: (prev8(i), 0)),
                  pl.BlockSpec((tile, D_LRU), lambda i: (rev(i), 0)),
                  pl.BlockSpec((HALO, D_LRU), lambda i: (prev8(i), 0)),
                  pl.BlockSpec((tile, D_MIX), lambda i: (rev(i), 0)),
                  full((4, D_LRU)), full((1, D_LRU)),
                  full((N_GROUPS, LRU_GROUP, LRU_GROUP)), full((N_GROUPS, LRU_GROUP, LRU_GROUP)),
                  full((1, D_LRU)), full((1, D_LRU)), full((1, D_LRU)), full((3, D_SC)),
                  pl.BlockSpec(memory_space=pl.ANY)],
        out_specs=[pl.BlockSpec((tile, D_IN), lambda i: (rev(i), 0)),
                   full((8, D_LRU)), full((8, D_LRU)),
                   full((N_GROUPS, LRU_GROUP, LRU_GROUP)), full((N_GROUPS, LRU_GROUP, LRU_GROUP)),
                   full((8, D_SC))],
        out_shape=[jax.ShapeDtypeStruct((s, D_IN), BF16),
                   jax.ShapeDtypeStruct((8, D_LRU), F32), jax.ShapeDtypeStruct((8, D_LRU), F32),
                   jax.ShapeDtypeStruct((N_GROUPS, LRU_GROUP, LRU_GROUP), F32),
                   jax.ShapeDtypeStruct((N_GROUPS, LRU_GROUP, LRU_GROUP), F32),
                   jax.ShapeDtypeStruct((8, D_SC), F32)],
        scratch_shapes=[pltpu.VMEM((N_PAR, FFN_ROWS, D_LRU), F32), pltpu.VMEM((3, FFN_ROWS, D_SC), F32),
                        wide(tile), wide(tile, BF16), pltpu.VMEM((tile, D_SC), F32),
                        wide(tile), wide(tile), wide(tile), wide(tile), wide(tile), wide(tile), wide(tile),
                        wide(tile), wide(tile), wide(tile),
                        wide(tile), wide(tile, BF16), wide(tile, BF16), wide(tile),
                        wide(FFN_ROWS), wide(FFN_ROWS), pltpu.VMEM((FFN_ROWS, D_SC), F32), wide(8),
                        pltpu.VMEM((8, 8, D_LRU), F32), pltpu.VMEM((3, 8, D_SC), F32)],
        compiler_params=_params("arbitrary"),
    )(z, z, hs, hs, dy, cw, cb.reshape(1, -1), wa_bd, wx_bd, ba.reshape(1, -1), bx.reshape(1, -1),
      lam.reshape(1, -1), scw, _behind(token))


FFN_ROWS = 16
FFN_GROUPS = 2


def _spread_taps(fw_ref, taps):
    for half in range(2):
        for k in range(3):
            taps[half, k] = jnp.broadcast_to(fw_ref[half, k:k + 1, :], taps.shape[2:])


def _rows_from(first, second, start):
    stack = jnp.concatenate([first, second], axis=0)
    return pltpu.roll(stack, 2 * FFN_ROWS - start, 0)[0:FFN_ROWS]


def _conv3_rows(taps, ext_ref, half, row):
    before = ext_ref[half, row - FFN_ROWS:row, :]
    here = ext_ref[half, row:row + FFN_ROWS, :]
    acc = taps[half, 2] * here
    for k in range(2):
        acc = acc + taps[half, k] * _rows_from(before, here, FFN_ROWS - 2 + k)
    return acc


HALO_B = 16


def _ffn_block_fwd(x2, g2, w_up_b, fcw, w_down, *, tile, name):
    s = x2.shape[0]
    nb = w_up_b.shape[2]
    blocks = D_FF // nb
    per16 = tile // HALO_B

    def body(x2_ref, x2p_ref, g_ref, wg_ref, wu_ref, fw_ref, wd_ref, x3_ref, h_ref, act_ref, p_ref, u_ref,
             ext_p, acc_ref, taps, lhs):
        i = pl.program_id(0)
        j = pl.program_id(1)
        keep = jnp.where(i == 0, 0.0, 1.0)
        _spread_taps(fw_ref, taps)
        @pl.when(j == 0)
        def _():
            for rows_ref, at in ((x2p_ref, 0), (x2_ref, HALO_B)):
                xv = rows_ref[...]
                lhs[at:at + xv.shape[0], :] = (xv * _rms(xv) * g_ref[...]).astype(BF16)
            h_ref[...] = lhs[HALO_B:HALO_B + tile, :]

        grp = tile // FFN_GROUPS
        for g in range(FFN_GROUPS):
            new = slice(g * grp + (HALO_B if g else 0), (g + 1) * grp + HALO_B)
            for half, w_ref in ((0, wg_ref), (1, wu_ref)):
                pe = _dot(lhs[new, :], w_ref[...], NN)
                if g == 0:
                    ext_p[half, 0:HALO_B, :] = pe[0:HALO_B] * keep
                    ext_p[half, HALO_B:grp + HALO_B, :] = pe[HALO_B:]
                    p_ref[half, 0:grp, :] = pe[HALO_B:].astype(BF16)
                else:
                    ext_p[half, new, :] = pe
                    p_ref[half, g * grp:(g + 1) * grp, :] = pe.astype(BF16)
        for g in range(FFN_GROUPS):
            rows = slice(g * grp, (g + 1) * grp)
            acts = []
            for r0 in range(g * grp, (g + 1) * grp, FFN_ROWS):
                u = [_conv3_rows(taps, ext_p, half, HALO_B + r0) for half in range(2)]
                for half in range(2):
                    u_ref[half, r0:r0 + FFN_ROWS, :] = u[half].astype(BF16)
                acts.append((_gelu(u[0]) * u[1]).astype(BF16))
                act_ref[r0:r0 + FFN_ROWS, :] = acts[-1]
            contrib = _dot(jnp.concatenate(acts, axis=0), wd_ref[...], NN)
            acc_ref[rows, :] = contrib + jnp.where(j > 0, acc_ref[rows, :], 0.0)

        @pl.when(j == blocks - 1)
        def _():
            x3_ref[...] = x2_ref[...] + acc_ref[...]

    return pl.pallas_call(
        body, name=name, grid=(s // tile, blocks),
        in_specs=[pl.BlockSpec((tile, D_MODEL), lambda i, j: (i, 0)),
                  pl.BlockSpec((HALO_B, D_MODEL), lambda i, j: (jnp.maximum(i * per16 - 1, 0), 0)),
                  pl.BlockSpec((1, D_MODEL), lambda i, j: (0, 0)),
                  pl.BlockSpec((None, D_MODEL, nb), lambda i, j: (j, 0, 0)),
                  pl.BlockSpec((None, D_MODEL, nb), lambda i, j: (j + blocks, 0, 0)),
                  pl.BlockSpec((2, 3, nb), lambda i, j: (0, 0, j)),
                  pl.BlockSpec((nb, D_MODEL), lambda i, j: (j, 0))],
        out_specs=[pl.BlockSpec((tile, D_MODEL), lambda i, j: (i, 0)),
                   pl.BlockSpec((tile, D_MODEL), lambda i, j: (i, 0)),
                   pl.BlockSpec((tile, nb), lambda i, j: (i, j)),
                   pl.BlockSpec((2, tile, nb), lambda i, j: (0, i, j)),
                   pl.BlockSpec((2, tile, nb), lambda i, j: (0, i, j))],
        out_shape=[jax.ShapeDtypeStruct((s, D_MODEL), F32), jax.ShapeDtypeStruct((s, D_MODEL), BF16),
                   jax.ShapeDtypeStruct((s, D_FF), BF16),
                   jax.ShapeDtypeStruct((2, s, D_FF), BF16), jax.ShapeDtypeStruct((2, s, D_FF), BF16)],
        scratch_shapes=[pltpu.VMEM((2, tile + HALO_B, nb), F32), pltpu.VMEM((tile, D_MODEL), F32),
                        pltpu.VMEM((2, 3, FFN_ROWS, nb), F32), pltpu.VMEM((tile + HALO_B, D_MODEL), BF16)],
        compiler_params=_params("parallel", "arbitrary"),
    )(x2, x2, g2.reshape(1, -1), w_up_b, w_up_b, fcw, w_down)


def _ffn_block_bwd(dx3, dx3b, p, u, x2, g2, w_up_b, fcw, w_down, *, tile, name, token=None):
    s = x2.shape[0]
    nb = w_up_b.shape[2]
    blocks = D_FF // nb
    n_tiles = s // tile
    per16 = tile // HALO_B
    last16 = s // HALO_B - 1

    def body(dxb_ref, dxbn_ref, wd_ref, p_ref, u_ref, un_ref, fw_ref, wg_ref, wu_ref, x2_ref, g_ref, dx3_ref,
             token_ref, dx2_ref, dx2b_ref, dg_ref, dp_ref, dw_ref, da_s, acc_w, acc_dh, taps):
        i = pl.program_id(0)
        j = pl.program_id(1)

        @pl.when(jnp.logical_and(i == 0, j == 0))
        def _():
            acc_w[...] = jnp.zeros_like(acc_w)
            dg_ref[...] = jnp.zeros_like(dg_ref)

        keep_next = jnp.where(i == n_tiles - 1, 0.0, 1.0)
        _spread_taps(fw_ref, taps)
        lhs = jnp.concatenate([dxb_ref[...], dxbn_ref[...]], axis=0)
        grp = tile // FFN_GROUPS
        for g in reversed(range(FFN_GROUPS)):
            new = slice(g * grp, (g + 1) * grp + (HALO_B if g == FFN_GROUPS - 1 else 0))
            da_s[new, :] = _dot(lhs[new], wd_ref[...], NT)

        def du_rows(da, u_gate, u_up):
            ge, dge = _gelu_parts(u_gate)
            return da * u_up * dge, da * ge

        after = du_rows(da_s[tile:tile + HALO_B, :] * keep_next, un_ref[0].astype(F32), un_ref[1].astype(F32))
        for g in reversed(range(FFN_GROUPS)):
            rows = slice(g * grp, (g + 1) * grp)
            dps = ([], [])
            for r0 in range((g + 1) * grp - FFN_ROWS, g * grp - 1, -FFN_ROWS):
                du = du_rows(da_s[r0:r0 + FFN_ROWS, :], u_ref[0, r0:r0 + FFN_ROWS, :].astype(F32),
                             u_ref[1, r0:r0 + FFN_ROWS, :].astype(F32))
                for half in range(2):
                    below = [du[half], _rows_from(du[half], after[half], 1), _rows_from(du[half], after[half], 2)]
                    acc = taps[half, 2] * below[0]
                    for k in range(2):
                        acc = acc + taps[half, k] * below[2 - k]
                    dps[half].insert(0, acc.astype(BF16))
                    dp_ref[half, r0:r0 + FFN_ROWS, :] = dps[half][0]
                    p_rows = p_ref[half, r0:r0 + FFN_ROWS, :].astype(F32)
                    for k in range(3):
                        prod = below[2 - k] * p_rows
                        acc_w[j, half, k] += sum(prod[q:q + 8] for q in range(0, FFN_ROWS, 8))
                after = du
            contrib = (_dot(jnp.concatenate(dps[0], axis=0), wg_ref[...], NT)
                       + _dot(jnp.concatenate(dps[1], axis=0), wu_ref[...], NT))
            acc_dh[rows, :] = contrib + jnp.where(j > 0, acc_dh[rows, :], 0.0)

        @pl.when(j == blocks - 1)
        def _():
            dh = acc_dh[...]
            xv = x2_ref[...]
            rstd = _rms(xv)
            n = xv * rstd
            dn = dh * g_ref[...]
            dx = dx3_ref[...] + rstd * (dn - n * jnp.mean(dn * n, axis=-1, keepdims=True))
            dx2_ref[...] = dx
            dx2b_ref[...] = dx.astype(BF16)
            dg_ref[0:1, :] += jnp.sum(dh * n, axis=0, keepdims=True)

        @pl.when(jnp.logical_and(i == n_tiles - 1, j == blocks - 1))
        def _():
            dw_ref[...] = jnp.zeros_like(dw_ref)
            for jj in range(blocks):
                for half in range(2):
                    for k in range(3):
                        dw_ref[half, k:k + 1, jj * nb:(jj + 1) * nb] = jnp.sum(acc_w[jj, half, k], axis=0, keepdims=True)

    next16 = lambda i: jnp.minimum((i + 1) * per16, last16)
    return pl.pallas_call(
        body, name=name, grid=(n_tiles, blocks),
        in_specs=[pl.BlockSpec((tile, D_MODEL), lambda i, j: (i, 0)),
                  pl.BlockSpec((HALO_B, D_MODEL), lambda i, j: (next16(i), 0)),
                  pl.BlockSpec((nb, D_MODEL), lambda i, j: (j, 0)),
                  pl.BlockSpec((2, tile, nb), lambda i, j: (0, i, j)),
                  pl.BlockSpec((2, tile, nb), lambda i, j: (0, i, j)),
                  pl.BlockSpec((2, HALO_B, nb), lambda i, j: (0, next16(i), j)),
                  pl.BlockSpec((2, 3, nb), lambda i, j: (0, 0, j)),
                  pl.BlockSpec((None, D_MODEL, nb), lambda i, j: (j, 0, 0)),
                  pl.BlockSpec((None, D_MODEL, nb), lambda i, j: (j + blocks, 0, 0)),
                  pl.BlockSpec((tile, D_MODEL), lambda i, j: (i, 0)),
                  pl.BlockSpec((1, D_MODEL), lambda i, j: (0, 0)),
                  pl.BlockSpec((tile, D_MODEL), lambda i, j: (i, 0)),
                  pl.BlockSpec(memory_space=pl.ANY)],
        out_specs=[pl.BlockSpec((tile, D_MODEL), lambda i, j: (i, 0)),
                   pl.BlockSpec((tile, D_MODEL), lambda i, j: (i, 0)),
                   pl.BlockSpec((8, D_MODEL), lambda i, j: (0, 0)),
                   pl.BlockSpec((2, tile, nb), lambda i, j: (0, i, j)),
                   pl.BlockSpec((2, 8, D_FF), lambda i, j: (0, 0, 0))],
        out_shape=[jax.ShapeDtypeStruct((s, D_MODEL), F32), jax.ShapeDtypeStruct((s, D_MODEL), BF16),
                   jax.ShapeDtypeStruct((8, D_MODEL), F32), jax.ShapeDtypeStruct((2, s, D_FF), BF16),
                   jax.ShapeDtypeStruct((2, 8, D_FF), F32)],
        scratch_shapes=[pltpu.VMEM((tile + HALO_B, nb), F32), pltpu.VMEM((blocks, 2, 3, 8, nb), F32),
                        pltpu.VMEM((tile, D_MODEL), F32), pltpu.VMEM((2, 3, FFN_ROWS, nb), F32)],
        compiler_params=_params("arbitrary", "arbitrary"),
    )(dx3b, dx3b, w_down, p, u, u, fcw, w_up_b, w_up_b, x2, g2.reshape(1, -1), dx3, _behind(token))


def _adamw_math(w, g, m, v):
    m = ADAM_B1 * m + (1.0 - ADAM_B1) * g
    v = ADAM_B2 * v + (1.0 - ADAM_B2) * (g * g)
    m_hat = m / (1.0 - ADAM_B1 ** ADAM_STEP)
    v_hat = v / (1.0 - ADAM_B2 ** ADAM_STEP)
    delta = -ADAM_LR * (m_hat / (jnp.sqrt(v_hat) + ADAM_EPS) + ADAM_WD * w)
    return delta, m, v


def _adamw(w, g, m, v, *, name):
    rows, cols = w.shape
    tr = rows
    for cand in (512, 256, 128, 64, 32, 16, 8):
        if rows % cand == 0 and rows > cand:
            tr = cand
            break

    def body(w_ref, g_ref, m_ref, v_ref, d_ref, nm_ref, nv_ref):
        d, nm, nv = _adamw_math(w_ref[...], g_ref[...], m_ref[...], v_ref[...])
        d_ref[...] = d
        nm_ref[...] = nm
        nv_ref[...] = nv

    spec = pl.BlockSpec((tr, cols), lambda i: (i, 0))
    return pl.pallas_call(
        body, name=name, grid=(rows // tr,), in_specs=[spec] * 4, out_specs=[spec] * 3,
        out_shape=[jax.ShapeDtypeStruct((rows, cols), F32)] * 3,
        compiler_params=_params("parallel"),
    )(w, g, m, v)


def _sum_adamw(parts, w, m, v, *, name):
    depth, rows, cols = w.shape
    tr = rows
    for cand in (256, 128, 64):
        if rows % cand == 0 and rows > cand:
            tr = cand
            break

    def body(*refs):
        part_refs = refs[:depth]
        w_ref, m_ref, v_ref, g_ref, d_ref, nm_ref, nv_ref = refs[depth:]
        layer = pl.program_id(0)
        grad = None
        for k, p_ref in enumerate(part_refs):
            total = p_ref[0].astype(F32)
            for dev in range(1, N_DEV):
                total = total + p_ref[dev].astype(F32)
            grad = total if grad is None else jnp.where(layer == k, total, grad)
        d, nm, nv = _adamw_math(w_ref[...], grad, m_ref[...], v_ref[...])
        g_ref[...] = grad
        d_ref[...] = d
        nm_ref[...] = nm
        nv_ref[...] = nv

    part_spec = lambda k: pl.BlockSpec((N_DEV, tr, cols), lambda l, i: (0, jnp.where(l == k, i, 0), 0))
    spec = pl.BlockSpec((None, tr, cols), lambda l, i: (l, i, 0))
    return pl.pallas_call(
        body, name=name, grid=(depth, rows // tr),
        in_specs=[part_spec(k) for k in range(depth)] + [spec] * 3, out_specs=[spec] * 4,
        out_shape=[jax.ShapeDtypeStruct((depth, rows, cols), F32)] * 4,
        compiler_params=_params("parallel", "parallel"),
    )(*parts, w, m, v)


def _sum_parts(parts, *, name, tokens=()):
    _, rows, cols = parts.shape
    tr = rows
    for cand in (256, 128, 64, 32, 16):
        if rows % cand == 0 and rows > cand:
            tr = cand
            break

    def body(p_ref, *rest):
        acc = p_ref[0].astype(F32)
        for d in range(1, N_DEV):
            acc = acc + p_ref[d].astype(F32)
        rest[-1][...] = acc

    return pl.pallas_call(
        body, name=name, grid=(rows // tr,),
        in_specs=[pl.BlockSpec((N_DEV, tr, cols), lambda i: (0, i, 0))]
        + [pl.BlockSpec(memory_space=pl.ANY)] * len(tokens),
        out_specs=pl.BlockSpec((tr, cols), lambda i: (i, 0)),
        out_shape=jax.ShapeDtypeStruct((rows, cols), F32),
        compiler_params=_params("parallel"),
    )(parts, *tokens)


def _place():
    return lax.axis_index("x"), lax.axis_index("y"), lax.axis_index("c")


def _flip(v, bit):
    return 1 - v if bit else v


N_PEERS = N_DEV - 1


def _peer_copy(k, src_ref, land_ref, send_sem, recv_sem, gather):
    x, y, c = _place()
    my_id = 4 * x + 2 * y + c
    px, py, pc = _flip(x, k & 4), _flip(y, k & 2), _flip(c, k & 1)
    peer_id = 4 * px + 2 * py + pc
    return pltpu.make_async_remote_copy(
        src_ref=src_ref if gather else src_ref.at[peer_id], dst_ref=land_ref.at[my_id],
        send_sem=send_sem.at[k - 1], recv_sem=recv_sem.at[k - 1],
        device_id=(px, py, pc), device_id_type=MESH)


def _sequencer_copies(srcs, *, gather, name, collective_id, after):
    n = len(srcs)
    hbm = pltpu.MemorySpace.HBM
    src_refs = [jax.new_ref(s, memory_space=hbm) for s in srcs]
    land_refs = [jax.empty_ref(jax.ShapeDtypeStruct(((N_DEV,) + s.shape) if gather else s.shape, s.dtype),
                               memory_space=hbm) for s in srcs]
    token_in = jax.new_ref(jnp.zeros((8, 128), F32) if after is None else after, memory_space=hbm)
    token_out = jax.empty_ref(jax.ShapeDtypeStruct((8, 128), F32), memory_space=hbm)

    @pl.kernel(mesh=plsc.ScalarSubcoreMesh(axis_name="seq", num_cores=1), name=name,
               scratch_types=(pltpu.SemaphoreType.DMA((n, N_PEERS)), pltpu.SemaphoreType.DMA((n, N_PEERS)),
                              pltpu.SemaphoreType.DMA((n + 1,))),
               compiler_params=pltpu.CompilerParams(collective_id=collective_id))
    def launch(send_sems, recv_sems, local_sems):
        x, y, c = _place()
        my_id = 4 * x + 2 * y + c
        barrier = pltpu.get_barrier_semaphore()
        own = [pltpu.make_async_copy(src_refs[t] if gather else src_refs[t].at[my_id], land_refs[t].at[my_id],
                                     local_sems.at[t]) for t in range(n)]
        if gather:
            sibling = (x, y, 1 - c)
            chips = [(1 - x, y), (x, 1 - y), (1 - x, 1 - y)]
            for peer in [sibling] + [(*chip, c) for chip in chips]:
                pl.semaphore_signal(barrier, inc=1, device_id=peer, device_id_type=MESH)
            pl.semaphore_wait(barrier, 4)

            def copy(t, k, block, to, src=None):
                dst = land_refs[t].at[4 * block[0] + 2 * block[1] + block[2]]
                return pltpu.make_async_remote_copy(
                    src_ref=dst if src is None else src, dst_ref=dst,
                    send_sem=send_sems.at[t, k], recv_sem=recv_sems.at[t, k], device_id=to, device_id_type=MESH)

            for cp in own:
                cp.start()
            sends = []
            for t in range(n):
                sends.append(copy(t, 0, (x, y, c), sibling, src=src_refs[t]))
                sends += [copy(t, 1 + j, (x, y, c), (*chip, c), src=src_refs[t]) for j, chip in enumerate(chips)]
            for cp in sends:
                cp.start()
            for t in range(n):
                for j, chip in enumerate(chips):
                    copy(t, 1 + j, (*chip, c), (x, y, c)).wait_recv()
                    passed_on = copy(t, 4 + j, (*chip, c), sibling)
                    passed_on.start()
                    sends.append(passed_on)
            for t in range(n):
                copy(t, 0, sibling, (x, y, c)).wait_recv()
                for j, chip in enumerate(chips):
                    copy(t, 4 + j, (*chip, 1 - c), (x, y, c)).wait_recv()
            for cp in sends:
                cp.wait_send()
            for cp in own:
                cp.wait()
        else:
            for k in range(1, N_DEV):
                peer = (_flip(x, k & 4), _flip(y, k & 2), _flip(c, k & 1))
                pl.semaphore_signal(barrier, inc=1, device_id=peer, device_id_type=MESH)
            pl.semaphore_wait(barrier, N_PEERS)
            for cp in own:
                cp.start()
            copies = [_peer_copy(k, src_refs[t], land_refs[t], send_sems.at[t], recv_sems.at[t], gather)
                      for t in range(n) for k in range(1, N_DEV)]
            for cp in copies:
                cp.start()
            for cp in own:
                cp.wait()
            for cp in copies:
                cp.wait()
        passed = pltpu.make_async_copy(token_in, token_out, local_sems.at[n])
        passed.start()
        passed.wait()

    launch()
    return [ref[...] for ref in land_refs], token_out[...]


TM = 512
TMM = 1024
TKW = 2048
MIX_TILE = 256
FFN_FWD_TILE = 512
FFN_BWD_TILE = 512


def _block_diag(w):
    wg = w.reshape(N_GROUPS, HEADS_PER_GROUP, LRU_HEAD_DIM, LRU_HEAD_DIM)
    eye = jnp.eye(HEADS_PER_GROUP, dtype=w.dtype)
    bd = wg[:, :, :, None, :] * eye[None, :, None, :, None]
    return bd.reshape(N_GROUPS, LRU_GROUP, LRU_GROUP).astype(BF16)


def _head_blocks(bd):
    b5 = bd.reshape(N_GROUPS, HEADS_PER_GROUP, LRU_HEAD_DIM, HEADS_PER_GROUP, LRU_HEAD_DIM)
    blocks = [b5[:, h, :, h, :] for h in range(HEADS_PER_GROUP)]
    return jnp.stack(blocks, axis=1).reshape(LRU_HEADS, LRU_HEAD_DIM, LRU_HEAD_DIM)


def _w(lw, key, after):
    value = lw[key]
    return value(after) if callable(value) else value


def _layer_fwd(x, lw, tag):
    sv_rows = x.shape[0]
    z, h1 = _norm_in_proj(x, lw["g1"], _w(lw, "w_in_t", x), tm=min(TMM, sv_rows), tn=896, name=f"in_proj_{tag}")
    y_mix, hs = _mixer_fwd(z, _w(lw, "cw", z), lw["cb"], lw["wa_bd"], lw["wx_bd"], lw["ba"], lw["bx"], lw["lam"],
                           _w(lw, "scw", z), tile=MIX_TILE, name=f"mixer_fwd_{tag}")
    x2 = _mm_nn(y_mix, _w(lw, "w_out", y_mix), tm=min(TMM, sv_rows), tn=D_MODEL, tk=D_MIX, out_dtype=F32, name=f"out_proj_{tag}",
                residual=x)
    x3, h2, act, p, u = _ffn_block_fwd(x2, lw["g2"], _w(lw, "w_up_b", x2), _w(lw, "fcw", x2), _w(lw, "w_down", x2),
                                       tile=min(FFN_FWD_TILE, sv_rows), name=f"ffn_fwd_{tag}")
    saved = dict(x=x, h1=h1, z=z, y_mix=y_mix, hs=hs, x2=x2, h2=h2, p=p, u=u, act=act)
    return x3, saved


def _layer_bwd(dx3, dx3b, lw, sv, tag, put):
    sv_rows = dx3.shape[0]
    w_in_t, w_out, w_up_b, w_down = (_w(lw, k, dx3) for k in ("w_in_t", "w_out", "w_up_b", "w_down"))
    cw, scw, fcw = (_w(lw, k, dx3) for k in ("cw", "scw", "fcw"))
    g_down = _mm_tn(sv["act"], dx3b, tm=1024, tn=D_MODEL, tk=min(TKW, sv_rows), out_dtype=BF16, name=f"down_bwd_w_{tag}")
    dx2, dx2b, dg2, dp, dfcw = _ffn_block_bwd(dx3, dx3b, sv["p"], sv["u"], sv["x2"], lw["g2"], w_up_b, fcw, w_down,
                                              tile=min(FFN_BWD_TILE, sv_rows), name=f"ffn_bwd_{tag}",
                                              token=put("w_down", g_down))
    g_up = _mm_up_bwd_w(sv["h2"], dp, tm=D_MODEL, tk=min(TKW, sv_rows), name=f"up_bwd_w_{tag}")
    dy = _mm_nt(dx2b, w_out, tm=min(TMM, sv_rows), tn=768, tk=D_MODEL, out_dtype=F32, name=f"out_bwd_x_{tag}")
    dz, dcw, dvec, dwa, dwx, dscw = _mixer_bwd_rows(
        sv["z"], sv["hs"], dy, cw, lw["cb"], lw["wa_bd"], lw["wx_bd"], lw["ba"], lw["bx"], lw["lam"],
        scw, tile=MIX_TILE, name=f"mixer_bwd_{tag}", token=put("w_up_b", g_up))
    g_out = _mm_tn(sv["y_mix"], dx2b, tm=768, tn=D_MODEL, tk=min(TKW, sv_rows), out_dtype=BF16, name=f"out_bwd_w_{tag}",
                   token=dz)
    g_in_t = _mm_tn(dz, sv["h1"], tm=896, tn=D_MODEL, tk=min(TKW, sv_rows), out_dtype=BF16, name=f"in_bwd_w_{tag}",
                    token=put("w_out", g_out))
    dx, dxb, dg1 = _in_bwd_norm(dz, w_in_t, sv["x"], lw["g1"], dx2, tm=min(TMM, sv_rows), tk=896, name=f"in_bwd_x_{tag}",
                                token=put("w_in_t", g_in_t))
    small = dict(norm1_g=dg1[0], lru_conv_w=dcw[0:4], lru_conv_b=dvec[0], lru_wa=_head_blocks(dwa),
                 lru_ba=dvec[1], lru_wx=_head_blocks(dwx), lru_bx=dvec[2], lru_lambda=dvec[3],
                 sc_conv_w=dscw[0:3], norm2_g=dg2[0], ffn_conv_w=dfcw[:, 0:3, :])
    return dx, dxb, small


SMALL_ORDER = ("norm1_g", "lru_conv_w", "lru_conv_b", "lru_wa", "lru_ba", "lru_wx", "lru_bx", "lru_lambda",
               "sc_conv_w", "norm2_g", "ffn_conv_w")


def _local_step(x, tgt, layers, final_g, put):
    saved = []
    h = x
    for l in range(DEPTH):
        h, sv = _layer_fwd(h, layers[l], f"l{l}")
        saved.append(sv)
    loss_blk, dx, dxb, dgf = _loss_head(h, final_g, tgt, tm=TM, name="loss_head")
    smalls = [None] * DEPTH
    for l in reversed(range(DEPTH)):
        dx, dxb, smalls[l] = _layer_bwd(dx, dxb, layers[l], saved[l], f"l{l}", functools.partial(put, l))
    return loss_blk[0, 0], dx, smalls, dgf[0]


def kernel(x, norm1_g, w_in, lru_conv_w, lru_conv_b, lru_wa, lru_ba, lru_wx, lru_bx, lru_lambda, sc_conv_w, w_out, norm2_g, w_up, ffn_conv_w, w_down, final_g, loss_target, m_norm1_g, m_w_in, m_lru_conv_w, m_lru_conv_b, m_lru_wa, m_lru_ba, m_lru_wx, m_lru_bx, m_lru_lambda, m_sc_conv_w, m_w_out, m_norm2_g, m_w_up, m_ffn_conv_w, m_w_down, m_final_g, v_norm1_g, v_w_in, v_lru_conv_w, v_lru_conv_b, v_lru_wa, v_lru_ba, v_lru_wx, v_lru_bx, v_lru_lambda, v_sc_conv_w, v_w_out, v_norm2_g, v_w_up, v_ffn_conv_w, v_w_down, v_final_g):
    names = ["norm1_g", "w_in", "lru_conv_w", "lru_conv_b", "lru_wa", "lru_ba", "lru_wx", "lru_bx", "lru_lambda",
             "sc_conv_w", "w_out", "norm2_g", "w_up", "ffn_conv_w", "w_down", "final_g"]
    w = dict(zip(names, [norm1_g, w_in, lru_conv_w, lru_conv_b, lru_wa, lru_ba, lru_wx, lru_bx, lru_lambda,
                         sc_conv_w, w_out, norm2_g, w_up, ffn_conv_w, w_down, final_g]))
    m = dict(zip(names, [m_norm1_g, m_w_in, m_lru_conv_w, m_lru_conv_b, m_lru_wa, m_lru_ba, m_lru_wx, m_lru_bx,
                         m_lru_lambda, m_sc_conv_w, m_w_out, m_norm2_g, m_w_up, m_ffn_conv_w, m_w_down, m_final_g]))
    v = dict(zip(names, [v_norm1_g, v_w_in, v_lru_conv_w, v_lru_conv_b, v_lru_wa, v_lru_ba, v_lru_wx, v_lru_bx,
                         v_lru_lambda, v_sc_conv_w, v_w_out, v_norm2_g, v_w_up, v_ffn_conv_w, v_w_down, v_final_g]))
    my_id = 4 * lax.axis_index("x") + 2 * lax.axis_index("y") + lax.axis_index("c")

    taps = jnp.zeros((DEPTH, 16, 768), F32)
    taps = taps.at[:, 0:4, 0:128].set(lru_conv_w).at[:, 4:7, 0:64].set(sc_conv_w).at[:, 8:11, :].set(ffn_conv_w)
    shards = {}
    for l in range(DEPTH):
        shards[f"w_in_t{l}"] = jnp.swapaxes(w_in[l], 0, 1).astype(BF16)
        if l == 0:
            shards["taps"] = taps.reshape(DEPTH * 16, 768)
        shards[f"w_out{l}"] = w_out[l].astype(BF16)
        shards[f"w_up_b{l}"] = w_up[l].astype(BF16)
        shards[f"w_down{l}"] = w_down[l].astype(BF16)
    ids = iter(range(18))
    got = {}
    chain = [None]
    for group in (("w_in_t0", "taps"), ("w_out0",), ("w_up_b0",), ("w_down0",),
                  ("w_in_t1",), ("w_out1",), ("w_up_b1",), ("w_down1",)):
        lands, chain[0] = _sequencer_copies([shards[k] for k in group], gather=True, name=f"gather_{group[0]}",
                                            collective_id=next(ids), after=None)
        got.update(zip(group, lands))

    def fetch(key, after):
        return got[key]

    def tap_rows(l, lo, hi, width, after):
        tl = fetch("taps", after).reshape(N_DEV, DEPTH, 16, 768)[:, l, lo:hi, 0:width]
        return jnp.transpose(tl, (1, 0, 2)).reshape(hi - lo, N_DEV * width)

    layers = []
    for l in range(DEPTH):
        layers.append(dict(
            g1=norm1_g[l], g2=norm2_g[l], cb=lru_conv_b[l], ba=lru_ba[l], bx=lru_bx[l], lam=lru_lambda[l],
            wa_bd=_block_diag(lru_wa[l]), wx_bd=_block_diag(lru_wx[l]),
            cw=functools.partial(tap_rows, l, 0, 4, 128), scw=functools.partial(tap_rows, l, 4, 7, 64),
            fcw=lambda after, l=l: tap_rows(l, 8, 11, 768, after).reshape(3, 2, D_FF).transpose(1, 0, 2),
            w_in_t=lambda after, l=l: fetch(f"w_in_t{l}", after).reshape(D_IN, D_MODEL),
            w_out=lambda after, l=l: fetch(f"w_out{l}", after).reshape(D_MIX, D_MODEL),
            w_up_b=lambda after, l=l: fetch(f"w_up_b{l}", after),
            w_down=lambda after, l=l: fetch(f"w_down{l}", after).reshape(D_FF, D_MODEL)))

    scatter_handles = {}

    def put(l, key, grad):
        blocks = grad if grad.ndim == 3 else grad.reshape(N_DEV, grad.shape[0] // N_DEV, grad.shape[1])
        (scatter_handles[(l, key)],), chain[0] = _sequencer_copies(
            [blocks], gather=False, name=f"scatter_{key}{l}", collective_id=next(ids), after=chain[0])
        return blocks

    loss_local, dx, smalls, dgf = _local_step(x[0], loss_target[0], layers, final_g, put)

    parts = []
    for l in range(DEPTH):
        for key in ("w_in_t", "w_out", "w_up_b", "w_down"):
            parts.append(scatter_handles[(l, key)])

    flat = [smalls[l][k].reshape(-1) for l in range(DEPTH) for k in SMALL_ORDER] + [dgf.reshape(-1)]
    flat.append(jnp.broadcast_to(loss_local, (128,)))
    sizes = [f.shape[0] for f in flat]
    total = sum(sizes)
    rows = -(-total // (N_DEV * 128 * 8)) * 8
    flat.append(jnp.zeros((N_DEV * rows * 128 - total,), F32))
    (small_parts,), chain[0] = _sequencer_copies([jnp.concatenate(flat).reshape(N_DEV, rows, 128)], gather=False,
                                                 name="scatter_small", collective_id=next(ids), after=chain[0])
    grads, deltas, new_m, new_v = {}, {}, {}, {}
    for slot, k in ((2, "w_up"), (3, "w_down"), (1, "w_out")):
        grads[k], deltas[k], new_m[k], new_v[k] = _sum_adamw(
            [parts[4 * l + slot] for l in range(DEPTH)], w[k], m[k], v[k], name=f"adamw_{k}")
    small_mine = _sum_parts(small_parts, name="sum_small", tokens=(deltas["w_up"], deltas["w_down"]))
    (small_all,), _ = _sequencer_copies([small_mine], gather=True, name="gather_small",
                                        collective_id=next(ids), after=chain[0])
    grads["w_in"] = jnp.stack([jnp.swapaxes(_sum_parts(parts[4 * l], name=f"sum_w_in_l{l}"), 0, 1)
                               for l in range(DEPTH)])
    small_sum = small_all.reshape(-1)
    small_g, off = [], 0
    for sz in sizes:
        small_g.append(small_sum[off:off + sz])
        off += sz
    gs = {}
    for l in range(DEPTH):
        for i, k in enumerate(SMALL_ORDER):
            gs.setdefault(k, []).append(small_g[l * len(SMALL_ORDER) + i])
    g_final = small_g[-2]
    loss = small_g[-1][0]

    for k in ("norm1_g", "lru_conv_b", "lru_ba", "lru_bx", "lru_lambda", "norm2_g"):
        grads[k] = jnp.stack(gs[k]).reshape(DEPTH, -1)
    for k in ("lru_wa", "lru_wx"):
        grads[k] = jnp.stack(gs[k]).reshape(DEPTH, LRU_HEADS, LRU_HEAD_DIM, LRU_HEAD_DIM)
    grads["final_g"] = g_final
    cw_full = jnp.stack(gs["lru_conv_w"]).reshape(DEPTH, 4, N_DEV, 128)
    grads["lru_conv_w"] = lax.dynamic_index_in_dim(cw_full, my_id, axis=2, keepdims=False)
    scw_full = jnp.stack(gs["sc_conv_w"]).reshape(DEPTH, 3, N_DEV, 64)
    grads["sc_conv_w"] = lax.dynamic_index_in_dim(scw_full, my_id, axis=2, keepdims=False)
    fcw_full = jnp.stack(gs["ffn_conv_w"]).reshape(DEPTH, 2, 3, D_FF).transpose(0, 2, 1, 3).reshape(DEPTH, 3, N_DEV, 768)
    grads["ffn_conv_w"] = lax.dynamic_index_in_dim(fcw_full, my_id, axis=2, keepdims=False)

    for k in names:
        if k in deltas:
            continue
        shape = w[k].shape
        cols = shape[-1]
        as2d = lambda a: a.reshape(-1, cols)
        d, nm, nv = _adamw(as2d(w[k]), as2d(grads[k]), as2d(m[k]), as2d(v[k]), name=f"adamw_{k}")
        deltas[k], new_m[k], new_v[k] = d.reshape(shape), nm.reshape(shape), nv.reshape(shape)

    return (loss, dx[None], *[grads[k] for k in names], *[deltas[k] for k in names],
            *[new_m[k] for k in names], *[new_v[k] for k in names])
```

```python
import functools
import math

import jax
import jax.numpy as jnp
from jax import lax
from jax.experimental import pallas as pl
from jax.experimental.pallas import tpu as pltpu
from jax.experimental.pallas import tpu_sc as plsc

F32 = jnp.float32
BF16 = jnp.bfloat16

N_DEV = 8
DEPTH = 2
D_MODEL = 1024
D_LRU = 1024
D_SC = 512
D_MIX = D_LRU + D_SC
D_IN = 2 * D_LRU + 3 * D_SC
D_FF = 3072
LRU_HEADS = 16
LRU_HEAD_DIM = 64
LRU_GROUP = 256
N_GROUPS = D_LRU // LRU_GROUP
HEADS_PER_GROUP = LRU_GROUP // LRU_HEAD_DIM
RG_C = 8.0
EPS = 1e-6
HALO = 8

ADAM_LR = 0.001
ADAM_B1 = 0.9
ADAM_B2 = 0.999
ADAM_EPS = 1e-08
ADAM_WD = 0.01
ADAM_STEP = 10

GELU_C = math.sqrt(2.0 / math.pi)
GELU_A = 0.044715

VMEM_LIMIT = 56 * 1024 * 1024
MESH = pl.DeviceIdType.MESH


def _params(*sem):
    return pltpu.CompilerParams(dimension_semantics=tuple(sem) if sem else None,
                                vmem_limit_bytes=VMEM_LIMIT)


def _gelu_parts(x):
    x2 = x * x
    t = jnp.tanh(x * (GELU_C + (GELU_C * GELU_A) * x2))
    half = 0.5 * t + 0.5
    g = x * half
    dg = half + (0.5 * x) * (1.0 - t * t) * (GELU_C + (3.0 * GELU_C * GELU_A) * x2)
    return g, dg


def _gelu(x):
    t = jnp.tanh(x * (GELU_C + (GELU_C * GELU_A) * (x * x)))
    return x * (0.5 * t + 0.5)


def _sigmoid(x):
    return 0.5 * jnp.tanh(0.5 * x) + 0.5


def _softplus(x):
    e = jnp.exp(-jnp.abs(x))
    u = 1.0 + e
    log1p_e = jnp.where(u == 1.0, e, jnp.log(u) * (e / (u - 1.0)))
    return jnp.maximum(x, 0.0) + log1p_e


def _rms(x):
    ms = jnp.mean(x * x, axis=-1, keepdims=True)
    return lax.rsqrt(ms + EPS)


def _dot(a, b, dims):
    return lax.dot_general(a, b, (dims, ((), ())), preferred_element_type=F32)


NN = ((1,), (0,))
NT = ((1,), (1,))
TN = ((0,), (0,))


def _matmul(a, b, *, dims, grid, a_spec, b_spec, o_spec, out_shape, acc_shape, name,
            residual=None, r_spec=None, token=None):
    nk = grid[2]

    def body(*refs):
        a_ref, b_ref = refs[0], refs[1]
        r_ref = refs[2] if residual is not None else None
        o_ref = refs[2 + (residual is not None) + (token is not None)]
        prod = _dot(a_ref[...].astype(BF16), b_ref[...].astype(BF16), dims)

        def finish(total):
            if r_ref is not None:
                total = total + r_ref[...]
            o_ref[...] = total.astype(o_ref.dtype)

        if nk == 1:
            finish(prod)
            return
        acc_ref = refs[-1]
        k = pl.program_id(2)

        @pl.when(k == 0)
        def _():
            acc_ref[...] = prod

        @pl.when(jnp.logical_and(k > 0, k < nk - 1))
        def _():
            acc_ref[...] += prod

        @pl.when(k == nk - 1)
        def _():
            finish(acc_ref[...] + prod)

    in_specs = [a_spec, b_spec]
    args = [a, b]
    if residual is not None:
        in_specs.append(r_spec)
        args.append(residual)
    if token is not None:
        in_specs.append(pl.BlockSpec(memory_space=pl.ANY))
        args.append(token)
    return pl.pallas_call(
        body, name=name, grid=grid, in_specs=in_specs, out_specs=o_spec, out_shape=out_shape,
        scratch_shapes=[pltpu.VMEM(acc_shape, F32)] if nk > 1 else [],
        compiler_params=_params("parallel", "parallel", "arbitrary"),
    )(*args)


def _mm_nn(a, b, *, tm, tn, tk, out_dtype, name, residual=None, token=None):
    m, kd = a.shape
    n = b.shape[1]
    return _matmul(
        a, b, dims=NN, grid=(m // tm, n // tn, kd // tk),
        a_spec=pl.BlockSpec((tm, tk), lambda i, j, k: (i, k)),
        b_spec=pl.BlockSpec((tk, tn), lambda i, j, k: (k, j)),
        o_spec=pl.BlockSpec((tm, tn), lambda i, j, k: (i, j)),
        out_shape=jax.ShapeDtypeStruct((m, n), out_dtype), acc_shape=(tm, tn), name=name,
        residual=residual, r_spec=pl.BlockSpec((tm, tn), lambda i, j, k: (i, j)), token=token)


def _mm_nt(a, b, *, tm, tn, tk, out_dtype, name):
    m, kd = a.shape
    n = b.shape[0]
    return _matmul(
        a, b, dims=NT, grid=(m // tm, n // tn, kd // tk),
        a_spec=pl.BlockSpec((tm, tk), lambda i, j, k: (i, k)),
        b_spec=pl.BlockSpec((tn, tk), lambda i, j, k: (j, k)),
        o_spec=pl.BlockSpec((tm, tn), lambda i, j, k: (i, j)),
        out_shape=jax.ShapeDtypeStruct((m, n), out_dtype), acc_shape=(tm, tn), name=name)


def _mm_tn(a, b, *, tm, tn, tk, out_dtype, name, token=None):
    kd, m = a.shape
    n = b.shape[1]
    return _matmul(
        a, b, dims=TN, grid=(m // tm, n // tn, kd // tk),
        a_spec=pl.BlockSpec((tk, tm), lambda i, j, k: (k, i)),
        b_spec=pl.BlockSpec((tk, tn), lambda i, j, k: (k, j)),
        o_spec=pl.BlockSpec((tm, tn), lambda i, j, k: (i, j)),
        out_shape=jax.ShapeDtypeStruct((m, n), out_dtype), acc_shape=(tm, tn), name=name, token=token)


def _mm_up_bwd_w(h2, dp, *, tm, tk, name):
    s = h2.shape[0]
    nb = D_FF * 2 // N_DEV
    per_half = D_FF // nb
    return _matmul(
        h2, dp, dims=TN, grid=(D_MODEL // tm, N_DEV, s // tk),
        a_spec=pl.BlockSpec((tk, tm), lambda i, j, k: (k, i)),
        b_spec=pl.BlockSpec((None, tk, nb), lambda i, j, k: (j // per_half, k, j % per_half)),
        o_spec=pl.BlockSpec((None, tm, nb), lambda i, j, k: (j, i, 0)),
        out_shape=jax.ShapeDtypeStruct((N_DEV, D_MODEL, nb), BF16), acc_shape=(tm, nb), name=name)


def _behind(token):
    return jnp.zeros((8, 128), F32) if token is None else token


def _norm_in_proj(x, g, w_in_t, *, tm, tn, name):
    s, d = x.shape
    n = w_in_t.shape[0]

    def body(x_ref, g_ref, w_ref, z_ref, h_ref):
        @pl.when(pl.program_id(1) == 0)
        def _():
            xv = x_ref[...]
            h_ref[...] = (xv * _rms(xv) * g_ref[...]).astype(BF16)

        z_ref[...] = _dot(h_ref[...], w_ref[...], NT)

    return pl.pallas_call(
        body, name=name, grid=(s // tm, n // tn),
        in_specs=[pl.BlockSpec((tm, d), lambda i, j: (i, 0)), pl.BlockSpec((1, d), lambda i, j: (0, 0)),
                  pl.BlockSpec((tn, d), lambda i, j: (j, 0))],
        out_specs=[pl.BlockSpec((tm, tn), lambda i, j: (i, j)), pl.BlockSpec((tm, d), lambda i, j: (i, 0))],
        out_shape=[jax.ShapeDtypeStruct((s, n), F32), jax.ShapeDtypeStruct((s, d), BF16)],
        compiler_params=_params("parallel", "arbitrary"),
    )(x, g.reshape(1, d), w_in_t)


def _in_bwd_norm(dz, w_in_t, x, g, dres, *, tm, tk, name, token=None):
    s, kd = dz.shape
    d = w_in_t.shape[1]
    nk = kd // tk

    def body(dz_ref, w_ref, x_ref, g_ref, dres_ref, token_ref, dx_ref, dxb_ref, dg_ref, acc_ref):
        i = pl.program_id(0)
        k = pl.program_id(1)

        @pl.when(jnp.logical_and(i == 0, k == 0))
        def _():
            dg_ref[...] = jnp.zeros_like(dg_ref)

        acc_ref[...] = _dot(dz_ref[...], w_ref[...], NN) + jnp.where(k > 0, acc_ref[...], 0.0)

        @pl.when(k == nk - 1)
        def _():
            dh = acc_ref[...]
            xv = x_ref[...]
            rstd = _rms(xv)
            n = xv * rstd
            dn = dh * g_ref[...]
            dx = dres_ref[...] + rstd * (dn - n * jnp.mean(dn * n, axis=-1, keepdims=True))
            dx_ref[...] = dx
            dxb_ref[...] = dx.astype(BF16)
            dg_ref[0:1, :] += jnp.sum(dh * n, axis=0, keepdims=True)

    row = pl.BlockSpec((tm, d), lambda i, k: (i, 0))
    return pl.pallas_call(
        body, name=name, grid=(s // tm, nk),
        in_specs=[pl.BlockSpec((tm, tk), lambda i, k: (i, k)), pl.BlockSpec((tk, d), lambda i, k: (k, 0)),
                  row, pl.BlockSpec((1, d), lambda i, k: (0, 0)), row, pl.BlockSpec(memory_space=pl.ANY)],
        out_specs=[row, row, pl.BlockSpec((8, d), lambda i, k: (0, 0))],
        out_shape=[jax.ShapeDtypeStruct((s, d), F32), jax.ShapeDtypeStruct((s, d), BF16),
                   jax.ShapeDtypeStruct((8, d), F32)],
        scratch_shapes=[pltpu.VMEM((tm, d), F32)],
        compiler_params=_params("arbitrary", "arbitrary"),
    )(dz, w_in_t, x, g.reshape(1, d), dres, _behind(token))


def _loss_head(x, g, tgt, *, tm, name):
    s, d = x.shape

    def body(x_ref, g_ref, t_ref, loss_ref, dx_ref, dxb_ref, dg_ref):
        @pl.when(pl.program_id(0) == 0)
        def _():
            dg_ref[...] = jnp.zeros_like(dg_ref)
            loss_ref[...] = jnp.zeros_like(loss_ref)

        xv = x_ref[...]
        gv = g_ref[...]
        rstd = _rms(xv)
        n = xv * rstd
        e = n * gv - t_ref[...]
        part = 0.5 * jnp.sum(jnp.mean(e * e, axis=-1, keepdims=True), axis=0, keepdims=True)
        loss_ref[...] += jnp.broadcast_to(part, loss_ref.shape)
        dy = e * (1.0 / d)
        dn = dy * gv
        dx = rstd * (dn - n * jnp.mean(dn * n, axis=-1, keepdims=True))
        dx_ref[...] = dx
        dxb_ref[...] = dx.astype(BF16)
        dg_ref[0:1, :] += jnp.sum(dy * n, axis=0, keepdims=True)

    return pl.pallas_call(
        body, name=name, grid=(s // tm,),
        in_specs=[pl.BlockSpec((tm, d), lambda i: (i, 0)), pl.BlockSpec((1, d), lambda i: (0, 0)),
                  pl.BlockSpec((tm, d), lambda i: (i, 0))],
        out_specs=[pl.BlockSpec((8, 128), lambda i: (0, 0)), pl.BlockSpec((tm, d), lambda i: (i, 0)),
                   pl.BlockSpec((tm, d), lambda i: (i, 0)), pl.BlockSpec((8, d), lambda i: (0, 0))],
        out_shape=[jax.ShapeDtypeStruct((8, 128), F32), jax.ShapeDtypeStruct((s, d), F32),
                   jax.ShapeDtypeStruct((s, d), BF16), jax.ShapeDtypeStruct((8, d), F32)],
        compiler_params=_params("arbitrary"),
    )(x, g.reshape(1, d), tgt)


def _scan_rows(a_ref, b_ref, h_ref, carry, *, rows, reverse):
    width = a_ref.shape[1]
    n_chunks = rows // 8
    row = lax.broadcasted_iota(jnp.int32, (8, width), 0)

    def step(ci, carry):
        chunk = (n_chunks - 1 - ci) if reverse else ci
        off = pl.multiple_of(chunk * 8, 8)
        av = a_ref[pl.ds(off, 8), :]
        bv = b_ref[pl.ds(off, 8), :]
        for sh in (1, 2, 4):
            if reverse:
                a_sh = pltpu.roll(av, 8 - sh, 0)
                b_sh = pltpu.roll(bv, 8 - sh, 0)
                m = row < 8 - sh
            else:
                a_sh = pltpu.roll(av, sh, 0)
                b_sh = pltpu.roll(bv, sh, 0)
                m = row >= sh
            bv = jnp.where(m, av * b_sh + bv, bv)
            av = jnp.where(m, av * a_sh, av)
        h = av * carry + bv
        h_ref[pl.ds(off, 8), :] = h
        return h[0:1, :] if reverse else h[7:8, :]

    return lax.fori_loop(0, n_chunks, step, carry)


P_CB, P_BA, P_BX, P_DECAY, P_DSP, N_PAR = 4, 5, 6, 7, 8, 9


def _spread_mixer_params(par, par_sc, cw_ref, cb_ref, ba_ref, bx_ref, lam_ref, scw_ref):
    rows = par.shape[1:]
    for k in range(4):
        par[k] = jnp.broadcast_to(cw_ref[k:k + 1, :], rows)
    par[P_CB] = jnp.broadcast_to(cb_ref[...], rows)
    par[P_BA] = jnp.broadcast_to(ba_ref[...], rows)
    par[P_BX] = jnp.broadcast_to(bx_ref[...], rows)
    par[P_DECAY] = jnp.broadcast_to(-RG_C * _softplus(-lam_ref[...]), rows)
    par[P_DSP] = jnp.broadcast_to(-_sigmoid(-lam_ref[...]), rows)
    for k in range(3):
        par_sc[k] = jnp.broadcast_to(scw_ref[k:k + 1, :], par_sc.shape[1:])


def _gates_rows(pre_r, pre_i, par):
    r = _sigmoid(pre_r + par[P_BA])
    ig = _sigmoid(pre_i + par[P_BX])
    log_a = r * par[P_DECAY]
    a = jnp.exp(log_a)
    one_minus_a2 = -jnp.tanh(log_a) * (a * a + 1.0)
    return r, ig, a, jnp.sqrt(one_minus_a2), one_minus_a2


def _mixer_fwd(z, cw, cb, wa_bd, wx_bd, ba, bx, lam, scw, *, tile, name):
    s = z.shape[0]
    n_tiles = s // tile

    rows_of = lambda r0: slice(r0, r0 + FFN_ROWS)
    col_gate, col_sb, col_sc, col_sx = (slice(D_LRU, 2 * D_LRU), slice(2 * D_LRU, 2 * D_LRU + D_SC),
                                        slice(2 * D_LRU + D_SC, 2 * D_LRU + 2 * D_SC), slice(2 * D_LRU + 2 * D_SC, D_IN))

    def body(z_ref, cw_ref, cb_ref, wa_ref, wx_ref, ba_ref, bx_ref, lam_ref, scw_ref,
             y_ref, hs_ref, par, par_sc, lx_s, lxb_s, a_s, b_s, car_lx, car_q, h_car):
        i = pl.program_id(0)

        @pl.when(i == 0)
        def _():
            car_lx[...] = jnp.zeros_like(car_lx)
            car_q[...] = jnp.zeros_like(car_q)
            h_car[...] = jnp.zeros_like(h_car)
            _spread_mixer_params(par, par_sc, cw_ref, cb_ref, ba_ref, bx_ref, lam_ref, scw_ref)

        before_lx, before_q = car_lx[...], car_q[...]
        for r0 in range(0, tile, FFN_ROWS):
            cur = z_ref[rows_of(r0), 0:D_LRU]
            lx = par[P_CB] + par[3] * cur
            for k in range(3):
                lx = lx + par[k] * _rows_from(before_lx, cur, FFN_ROWS - 3 + k)
            lx_s[rows_of(r0), :] = lx
            lxb_s[rows_of(r0), :] = lx.astype(BF16)
            before_lx = cur
            q = z_ref[rows_of(r0), col_sc] * z_ref[rows_of(r0), col_sx]
            cq = par_sc[2] * q
            for k in range(2):
                cq = cq + par_sc[k] * _rows_from(before_q, q, FFN_ROWS - 2 + k)
            y_ref[rows_of(r0), D_LRU:D_MIX] = (z_ref[rows_of(r0), col_sb] * cq).astype(BF16)
            before_q = q
        car_lx[...] = before_lx
        car_q[...] = before_q

        for g in range(N_GROUPS):
            cols = slice(g * LRU_GROUP, (g + 1) * LRU_GROUP)
            a_s[:, cols] = _dot(lxb_s[:, cols], wa_ref[g], NN)
            b_s[:, cols] = _dot(lxb_s[:, cols], wx_ref[g], NN)

        for r0 in range(0, tile, FFN_ROWS):
            _, ig, a, mult, _ = _gates_rows(a_s[rows_of(r0), :], b_s[rows_of(r0), :], par)
            a_s[rows_of(r0), :] = a
            b_s[rows_of(r0), :] = mult * (ig * lx_s[rows_of(r0), :])
        h_car[0:1, :] = _scan_rows(a_s, b_s, hs_ref, h_car[0:1, :], rows=tile, reverse=False)

        for r0 in range(0, tile, FFN_ROWS):
            y_ref[rows_of(r0), 0:D_LRU] = (hs_ref[rows_of(r0), :] * _gelu(z_ref[rows_of(r0), col_gate])).astype(BF16)

    full = lambda shape: pl.BlockSpec(shape, lambda i: (0,) * len(shape))
    return pl.pallas_call(
        body, name=name, grid=(n_tiles,),
        in_specs=[pl.BlockSpec((tile, D_IN), lambda i: (i, 0)),
                  full((4, D_LRU)), full((1, D_LRU)),
                  full((N_GROUPS, LRU_GROUP, LRU_GROUP)), full((N_GROUPS, LRU_GROUP, LRU_GROUP)),
                  full((1, D_LRU)), full((1, D_LRU)), full((1, D_LRU)), full((3, D_SC))],
        out_specs=[pl.BlockSpec((tile, D_MIX), lambda i: (i, 0)), pl.BlockSpec((tile, D_LRU), lambda i: (i, 0))],
        out_shape=[jax.ShapeDtypeStruct((s, D_MIX), BF16), jax.ShapeDtypeStruct((s, D_LRU), F32)],
        scratch_shapes=[pltpu.VMEM((N_PAR, FFN_ROWS, D_LRU), F32), pltpu.VMEM((3, FFN_ROWS, D_SC), F32),
                        pltpu.VMEM((tile, D_LRU), F32), pltpu.VMEM((tile, D_LRU), BF16),
                        pltpu.VMEM((tile, D_LRU), F32), pltpu.VMEM((tile, D_LRU), F32),
                        pltpu.VMEM((FFN_ROWS, D_LRU), F32), pltpu.VMEM((FFN_ROWS, D_SC), F32),
                        pltpu.VMEM((8, D_LRU), F32)],
        compiler_params=_params("arbitrary"),
    )(z, cw, cb.reshape(1, -1), wa_bd, wx_bd, ba.reshape(1, -1), bx.reshape(1, -1), lam.reshape(1, -1), scw)


def _fold8(x):
    return sum(x[q:q + 8] for q in range(0, x.shape[0], 8))


def _mixer_bwd_rows(z, hs, dy, cw, cb, wa_bd, wx_bd, ba, bx, lam, scw, *, tile, name, token=None):
    s = z.shape[0]
    n_tiles = s // tile
    per8 = tile // 8
    rows_of = lambda r0: slice(r0, r0 + FFN_ROWS)
    col_gate, col_sb, col_sc, col_sx = (slice(D_LRU, 2 * D_LRU), slice(2 * D_LRU, 2 * D_LRU + D_SC),
                                        slice(2 * D_LRU + D_SC, 2 * D_LRU + 2 * D_SC), slice(2 * D_LRU + 2 * D_SC, D_IN))
    up = range(0, tile, FFN_ROWS)
    down = range(tile - FFN_ROWS, -1, -FFN_ROWS)
    A_CB, A_BA, A_BX, A_SP, A_CW = 0, 1, 2, 3, 4

    def body(z_ref, zp_ref, hs_ref, hsp_ref, dy_ref, cw_ref, cb_ref, wa_ref, wx_ref, ba_ref, bx_ref, lam_ref, scw_ref,
             token_ref, dz_ref, dcw_ref, dvec_ref, dwa_ref, dwx_ref, dscw_ref,
             par, par_sc, lx_s, lxb_s, cq_s, pr_s, pi_s, r_s, ig_s, a_s, mult_s, inv_s, ash_s, b_s, lam_s,
             dlx_s, dpr_b, dpi_b, back_s, car_a, car_dlx, car_dcq, l_car, acc, acc_sc):
        i = pl.program_id(0)

        @pl.when(i == 0)
        def _():
            for ref in (dwa_ref, dwx_ref, l_car, car_a, car_dlx, car_dcq, acc, acc_sc):
                ref[...] = jnp.zeros_like(ref)
            _spread_mixer_params(par, par_sc, cw_ref, cb_ref, ba_ref, bx_ref, lam_ref, scw_ref)

        keep = jnp.where(i == n_tiles - 1, 0.0, 1.0)
        zeros8 = lambda n: jnp.zeros((8, n), F32)

        before_lx = jnp.concatenate([zeros8(D_LRU), zp_ref[:, 0:D_LRU] * keep], axis=0)
        before_q = jnp.concatenate([zeros8(D_SC), zp_ref[:, col_sc] * zp_ref[:, col_sx] * keep], axis=0)
        for r0 in up:
            cur = z_ref[rows_of(r0), 0:D_LRU]
            lx = par[P_CB] + par[3] * cur
            for k in range(3):
                lx = lx + par[k] * _rows_from(before_lx, cur, FFN_ROWS - 3 + k)
            lx_s[rows_of(r0), :] = lx
            lxb_s[rows_of(r0), :] = lx.astype(BF16)
            before_lx = cur
            q = z_ref[rows_of(r0), col_sc] * z_ref[rows_of(r0), col_sx]
            cq = par_sc[2] * q
            for k in range(2):
                cq = cq + par_sc[k] * _rows_from(before_q, q, FFN_ROWS - 2 + k)
            cq_s[rows_of(r0), :] = cq
            before_q = q

        for g in range(N_GROUPS):
            cols = slice(g * LRU_GROUP, (g + 1) * LRU_GROUP)
            pr_s[:, cols] = _dot(lxb_s[:, cols], wa_ref[g], NN)
            pi_s[:, cols] = _dot(lxb_s[:, cols], wx_ref[g], NN)

        after_a = car_a[...]
        for r0 in down:
            r, ig, a, mult, one_minus_a2 = _gates_rows(pr_s[rows_of(r0), :], pi_s[rows_of(r0), :], par)
            r_s[rows_of(r0), :] = r
            ig_s[rows_of(r0), :] = ig
            a_s[rows_of(r0), :] = a
            mult_s[rows_of(r0), :] = mult
            inv_s[rows_of(r0), :] = lax.rsqrt(one_minus_a2)
            ash_s[rows_of(r0), :] = _rows_from(a, after_a, 1)
            after_a = a
            ge, dge = _gelu_parts(z_ref[rows_of(r0), col_gate])
            dy_lru = dy_ref[rows_of(r0), 0:D_LRU]
            dz_ref[rows_of(r0), col_gate] = (dy_lru * hs_ref[rows_of(r0), :] * dge).astype(BF16)
            b_s[rows_of(r0), :] = dy_lru * ge
        car_a[...] = after_a
        l_car[0:1, :] = _scan_rows(ash_s, b_s, lam_s, l_car[0:1, :], rows=tile, reverse=True)

        before_h = jnp.concatenate([zeros8(D_LRU), hsp_ref[...] * keep], axis=0)
        for r0 in up:
            lv = lam_s[rows_of(r0), :]
            h_here = hs_ref[rows_of(r0), :]
            lx, r, ig, a = lx_s[rows_of(r0), :], r_s[rows_of(r0), :], ig_s[rows_of(r0), :], a_s[rows_of(r0), :]
            mult = mult_s[rows_of(r0), :]
            da = lv * _rows_from(before_h, h_here, FFN_ROWS - 1)
            before_h = h_here
            d_mult = lv * ig * lx
            d_i = lv * mult * lx
            dlx_s[rows_of(r0), :] = lv * mult * ig
            dlog_a = da * a - d_mult * (a * a) * inv_s[rows_of(r0), :]
            dpre_r = dlog_a * par[P_DECAY] * r * (1.0 - r)
            dpre_i = d_i * ig * (1.0 - ig)
            acc[A_BA] += _fold8(dpre_r)
            acc[A_BX] += _fold8(dpre_i)
            acc[A_SP] += _fold8(dlog_a * r)
            dpr_b[rows_of(r0), :] = dpre_r.astype(BF16)
            dpi_b[rows_of(r0), :] = dpre_i.astype(BF16)

        for g in range(N_GROUPS):
            cols = slice(g * LRU_GROUP, (g + 1) * LRU_GROUP)
            dwa_ref[g] += _dot(lxb_s[:, cols], dpr_b[:, cols], TN)
            dwx_ref[g] += _dot(lxb_s[:, cols], dpi_b[:, cols], TN)
            back_s[:, cols] = _dot(dpr_b[:, cols], wa_ref[g], NT) + _dot(dpi_b[:, cols], wx_ref[g], NT)

        after_dlx, after_dcq = car_dlx[...], car_dcq[...]
        for r0 in down:
            dlx = dlx_s[rows_of(r0), :] + back_s[rows_of(r0), :]
            lxp = z_ref[rows_of(r0), 0:D_LRU]
            acc[A_CB] += _fold8(dlx)
            acc[A_CW + 3] += _fold8(dlx * lxp)
            dlxp = par[3] * dlx
            for sh in range(1, 4):
                below = _rows_from(dlx, after_dlx, sh)
                dlxp = dlxp + par[3 - sh] * below
                acc[A_CW + 3 - sh] += _fold8(below * lxp)
            dz_ref[rows_of(r0), 0:D_LRU] = dlxp.astype(BF16)
            after_dlx = dlx

            dy_sc = dy_ref[rows_of(r0), D_LRU:D_MIX]
            sb, sc, sx = z_ref[rows_of(r0), col_sb], z_ref[rows_of(r0), col_sc], z_ref[rows_of(r0), col_sx]
            dz_ref[rows_of(r0), col_sb] = (dy_sc * cq_s[rows_of(r0), :]).astype(BF16)
            dcq = dy_sc * sb
            q = sc * sx
            acc_sc[2] += _fold8(dcq * q)
            dq = par_sc[2] * dcq
            for sh in range(1, 3):
                below = _rows_from(dcq, after_dcq, sh)
                dq = dq + par_sc[2 - sh] * below
                acc_sc[2 - sh] += _fold8(below * q)
            dz_ref[rows_of(r0), col_sc] = (dq * sx).astype(BF16)
            dz_ref[rows_of(r0), col_sx] = (dq * sc).astype(BF16)
            after_dcq = dcq
        car_dlx[...] = after_dlx
        car_dcq[...] = after_dcq

        @pl.when(i == n_tiles - 1)
        def _():
            total = lambda x: jnp.sum(x, axis=0, keepdims=True)
            dcw_ref[...] = jnp.zeros_like(dcw_ref)
            dvec_ref[...] = jnp.zeros_like(dvec_ref)
            dscw_ref[...] = jnp.zeros_like(dscw_ref)
            for k in range(4):
                dcw_ref[k:k + 1, :] = total(acc[A_CW + k])
            for k in range(3):
                dvec_ref[k:k + 1, :] = total(acc[k])
                dscw_ref[k:k + 1, :] = total(acc_sc[k])
            dvec_ref[3:4, :] = total(acc[A_SP]) * (-RG_C) * par[P_DSP][0:1, :]

    rev = lambda i: n_tiles - 1 - i
    prev8 = lambda i: jnp.maximum(rev(i) * per8 - 1, 0)
    full = lambda shape: pl.BlockSpec(shape, lambda i: (0,) * len(shape))
    wide = lambda rows, dt=F32: pltpu.VMEM((rows, D_LRU), dt)
    return pl.pallas_call(
        body, name=name, grid=(n_tiles,),
        in_specs=[pl.BlockSpec((tile, D_IN), lambda i: (rev(i), 0)),
                  pl.BlockSpec((HALO, D_IN), lambda i: (prev8(i), 0)),
                  pl.BlockSpec((tile, D_LRU), lambda i: (rev(i), 0)),
                  pl.BlockSpec((HALO, D_LRU), lambda i: (prev8(i), 0)),
                  pl.BlockSpec((tile, D_MIX), lambda i: (rev(i), 0)),
                  full((4, D_LRU)), full((1, D_LRU)),
                  full((N_GROUPS, LRU_GROUP, LRU_GROUP)), full((N_GROUPS, LRU_GROUP, LRU_GROUP)),
                  full((1, D_LRU)), full((1, D_LRU)), full((1, D_LRU)), full((3, D_SC)),
                  pl.BlockSpec(memory_space=pl.ANY)],
        out_specs=[pl.BlockSpec((tile, D_IN), lambda i: (rev(i), 0)),
                   full((8, D_LRU)), full((8, D_LRU)),
                   full((N_GROUPS, LRU_GROUP, LRU_GROUP)), full((N_GROUPS, LRU_GROUP, LRU_GROUP)),
                   full((8, D_SC))],
        out_shape=[jax.ShapeDtypeStruct((s, D_IN), BF16),
                   jax.ShapeDtypeStruct((8, D_LRU), F32), jax.ShapeDtypeStruct((8, D_LRU), F32),
                   jax.ShapeDtypeStruct((N_GROUPS, LRU_GROUP, LRU_GROUP), F32),
                   jax.ShapeDtypeStruct((N_GROUPS, LRU_GROUP, LRU_GROUP), F32),
                   jax.ShapeDtypeStruct((8, D_SC), F32)],
        scratch_shapes=[pltpu.VMEM((N_PAR, FFN_ROWS, D_LRU), F32), pltpu.VMEM((3, FFN_ROWS, D_SC), F32),
                        wide(tile), wide(tile, BF16), pltpu.VMEM((tile, D_SC), F32),
                        wide(tile), wide(tile), wide(tile), wide(tile), wide(tile), wide(tile), wide(tile),
                        wide(tile), wide(tile), wide(tile),
                        wide(tile), wide(tile, BF16), wide(tile, BF16), wide(tile),
                        wide(FFN_ROWS), wide(FFN_ROWS), pltpu.VMEM((FFN_ROWS, D_SC), F32), wide(8),
                        pltpu.VMEM((8, 8, D_LRU), F32), pltpu.VMEM((3, 8, D_SC), F32)],
        compiler_params=_params("arbitrary"),
    )(z, z, hs, hs, dy, cw, cb.reshape(1, -1), wa_bd, wx_bd, ba.reshape(1, -1), bx.reshape(1, -1),
      lam.reshape(1, -1), scw, _behind(token))


FFN_ROWS = 16
FFN_GROUPS = 2


def _spread_taps(fw_ref, taps):
    for half in range(2):
        for k in range(3):
            taps[half, k] = jnp.broadcast_to(fw_ref[half, k:k + 1, :], taps.shape[2:])


def _rows_from(first, second, start):
    stack = jnp.concatenate([first, second], axis=0)
    return pltpu.roll(stack, 2 * FFN_ROWS - start, 0)[0:FFN_ROWS]


def _conv3_rows(taps, ext_ref, half, row):
    before = ext_ref[half, row - FFN_ROWS:row, :]
    here = ext_ref[half, row:row + FFN_ROWS, :]
    acc = taps[half, 2] * here
    for k in range(2):
        acc = acc + taps[half, k] * _rows_from(before, here, FFN_ROWS - 2 + k)
    return acc


HALO_B = 16


def _ffn_block_fwd(x2, g2, w_up_b, fcw, w_down, *, tile, name):
    s = x2.shape[0]
    nb = w_up_b.shape[2]
    blocks = D_FF // nb
    per16 = tile // HALO_B

    def body(x2_ref, x2p_ref, g_ref, wg_ref, wu_ref, fw_ref, wd_ref, x3_ref, h_ref, act_ref, p_ref, u_ref,
             ext_p, acc_ref, taps, lhs):
        i = pl.program_id(0)
        j = pl.program_id(1)
        keep = jnp.where(i == 0, 0.0, 1.0)
        _spread_taps(fw_ref, taps)
        @pl.when(j == 0)
        def _():
            for rows_ref, at in ((x2p_ref, 0), (x2_ref, HALO_B)):
                xv = rows_ref[...]
                lhs[at:at + xv.shape[0], :] = (xv * _rms(xv) * g_ref[...]).astype(BF16)
            h_ref[...] = lhs[HALO_B:HALO_B + tile, :]

        grp = tile // FFN_GROUPS
        for g in range(FFN_GROUPS):
            new = slice(g * grp + (HALO_B if g else 0), (g + 1) * grp + HALO_B)
            for half, w_ref in ((0, wg_ref), (1, wu_ref)):
                pe = _dot(lhs[new, :], w_ref[...], NN)
                if g == 0:
                    ext_p[half, 0:HALO_B, :] = pe[0:HALO_B] * keep
                    ext_p[half, HALO_B:grp + HALO_B, :] = pe[HALO_B:]
                    p_ref[half, 0:grp, :] = pe[HALO_B:].astype(BF16)
                else:
                    ext_p[half, new, :] = pe
                    p_ref[half, g * grp:(g + 1) * grp, :] = pe.astype(BF16)
        for g in range(FFN_GROUPS):
            rows = slice(g * grp, (g + 1) * grp)
            acts = []
            for r0 in range(g * grp, (g + 1) * grp, FFN_ROWS):
                u = [_conv3_rows(taps, ext_p, half, HALO_B + r0) for half in range(2)]
                for half in range(2):
                    u_ref[half, r0:r0 + FFN_ROWS, :] = u[half].astype(BF16)
                acts.append((_gelu(u[0]) * u[1]).astype(BF16))
                act_ref[r0:r0 + FFN_ROWS, :] = acts[-1]
            contrib = _dot(jnp.concatenate(acts, axis=0), wd_ref[...], NN)
            acc_ref[rows, :] = contrib + jnp.where(j > 0, acc_ref[rows, :], 0.0)

        @pl.when(j == blocks - 1)
        def _():
            x3_ref[...] = x2_ref[...] + acc_ref[...]

    return pl.pallas_call(
        body, name=name, grid=(s // tile, blocks),
        in_specs=[pl.BlockSpec((tile, D_MODEL), lambda i, j: (i, 0)),
                  pl.BlockSpec((HALO_B, D_MODEL), lambda i, j: (jnp.maximum(i * per16 - 1, 0), 0)),
                  pl.BlockSpec((1, D_MODEL), lambda i, j: (0, 0)),
                  pl.BlockSpec((None, D_MODEL, nb), lambda i, j: (j, 0, 0)),
                  pl.BlockSpec((None, D_MODEL, nb), lambda i, j: (j + blocks, 0, 0)),
                  pl.BlockSpec((2, 3, nb), lambda i, j: (0, 0, j)),
                  pl.BlockSpec((nb, D_MODEL), lambda i, j: (j, 0))],
        out_specs=[pl.BlockSpec((tile, D_MODEL), lambda i, j: (i, 0)),
                   pl.BlockSpec((tile, D_MODEL), lambda i, j: (i, 0)),
                   pl.BlockSpec((tile, nb), lambda i, j: (i, j)),
                   pl.BlockSpec((2, tile, nb), lambda i, j: (0, i, j)),
                   pl.BlockSpec((2, tile, nb), lambda i, j: (0, i, j))],
        out_shape=[jax.ShapeDtypeStruct((s, D_MODEL), F32), jax.ShapeDtypeStruct((s, D_MODEL), BF16),
                   jax.ShapeDtypeStruct((s, D_FF), BF16),
                   jax.ShapeDtypeStruct((2, s, D_FF), BF16), jax.ShapeDtypeStruct((2, s, D_FF), BF16)],
        scratch_shapes=[pltpu.VMEM((2, tile + HALO_B, nb), F32), pltpu.VMEM((tile, D_MODEL), F32),
                        pltpu.VMEM((2, 3, FFN_ROWS, nb), F32), pltpu.VMEM((tile + HALO_B, D_MODEL), BF16)],
        compiler_params=_params("parallel", "arbitrary"),
    )(x2, x2, g2.reshape(1, -1), w_up_b, w_up_b, fcw, w_down)


def _ffn_block_bwd(dx3, dx3b, p, u, x2, g2, w_up_b, fcw, w_down, *, tile, name, token=None):
    s = x2.shape[0]
    nb = w_up_b.shape[2]
    blocks = D_FF // nb
    n_tiles = s // tile
    per16 = tile // HALO_B
    last16 = s // HALO_B - 1

    def body(dxb_ref, dxbn_ref, wd_ref, p_ref, u_ref, un_ref, fw_ref, wg_ref, wu_ref, x2_ref, g_ref, dx3_ref,
             token_ref, dx2_ref, dx2b_ref, dg_ref, dp_ref, dw_ref, da_s, acc_w, acc_dh, taps):
        i = pl.program_id(0)
        j = pl.program_id(1)

        @pl.when(jnp.logical_and(i == 0, j == 0))
        def _():
            acc_w[...] = jnp.zeros_like(acc_w)
            dg_ref[...] = jnp.zeros_like(dg_ref)

        keep_next = jnp.where(i == n_tiles - 1, 0.0, 1.0)
        _spread_taps(fw_ref, taps)
        lhs = jnp.concatenate([dxb_ref[...], dxbn_ref[...]], axis=0)
        grp = tile // FFN_GROUPS
        for g in reversed(range(FFN_GROUPS)):
            new = slice(g * grp, (g + 1) * grp + (HALO_B if g == FFN_GROUPS - 1 else 0))
            da_s[new, :] = _dot(lhs[new], wd_ref[...], NT)

        def du_rows(da, u_gate, u_up):
            ge, dge = _gelu_parts(u_gate)
            return da * u_up * dge, da * ge

        after = du_rows(da_s[tile:tile + HALO_B, :] * keep_next, un_ref[0].astype(F32), un_ref[1].astype(F32))
        for g in reversed(range(FFN_GROUPS)):
            rows = slice(g * grp, (g + 1) * grp)
            dps = ([], [])
            for r0 in range((g + 1) * grp - FFN_ROWS, g * grp - 1, -FFN_ROWS):
                du = du_rows(da_s[r0:r0 + FFN_ROWS, :], u_ref[0, r0:r0 + FFN_ROWS, :].astype(F32),
                             u_ref[1, r0:r0 + FFN_ROWS, :].astype(F32))
                for half in range(2):
                    below = [du[half], _rows_from(du[half], after[half], 1), _rows_from(du[half], after[half], 2)]
                    acc = taps[half, 2] * below[0]
                    for k in range(2):
                        acc = acc + taps[half, k] * below[2 - k]
                    dps[half].insert(0, acc.astype(BF16))
                    dp_ref[half, r0:r0 + FFN_ROWS, :] = dps[half][0]
                    p_rows = p_ref[half, r0:r0 + FFN_ROWS, :].astype(F32)
                    for k in range(3):
                        prod = below[2 - k] * p_rows
                        acc_w[j, half, k] += sum(prod[q:q + 8] for q in range(0, FFN_ROWS, 8))
                after = du
            contrib = (_dot(jnp.concatenate(dps[0], axis=0), wg_ref[...], NT)
                       + _dot(jnp.concatenate(dps[1], axis=0), wu_ref[...], NT))
            acc_dh[rows, :] = contrib + jnp.where(j > 0, acc_dh[rows, :], 0.0)

        @pl.when(j == blocks - 1)
        def _():
            dh = acc_dh[...]
            xv = x2_ref[...]
            rstd = _rms(xv)
            n = xv * rstd
            dn = dh * g_ref[...]
            dx = dx3_ref[...] + rstd * (dn - n * jnp.mean(dn * n, axis=-1, keepdims=True))
            dx2_ref[...] = dx
            dx2b_ref[...] = dx.astype(BF16)
            dg_ref[0:1, :] += jnp.sum(dh * n, axis=0, keepdims=True)

        @pl.when(jnp.logical_and(i == n_tiles - 1, j == blocks - 1))
        def _():
            dw_ref[...] = jnp.zeros_like(dw_ref)
            for jj in range(blocks):
                for half in range(2):
                    for k in range(3):
                        dw_ref[half, k:k + 1, jj * nb:(jj + 1) * nb] = jnp.sum(acc_w[jj, half, k], axis=0, keepdims=True)

    next16 = lambda i: jnp.minimum((i + 1) * per16, last16)
    return pl.pallas_call(
        body, name=name, grid=(n_tiles, blocks),
        in_specs=[pl.BlockSpec((tile, D_MODEL), lambda i, j: (i, 0)),
                  pl.BlockSpec((HALO_B, D_MODEL), lambda i, j: (next16(i), 0)),
                  pl.BlockSpec((nb, D_MODEL), lambda i, j: (j, 0)),
                  pl.BlockSpec((2, tile, nb), lambda i, j: (0, i, j)),
                  pl.BlockSpec((2, tile, nb), lambda i, j: (0, i, j)),
                  pl.BlockSpec((2, HALO_B, nb), lambda i, j: (0, next16(i), j)),
                  pl.BlockSpec((2, 3, nb), lambda i, j: (0, 0, j)),
                  pl.BlockSpec((None, D_MODEL, nb), lambda i, j: (j, 0, 0)),
                  pl.BlockSpec((None, D_MODEL, nb), lambda i, j: (j + blocks, 0, 0)),
                  pl.BlockSpec((tile, D_MODEL), lambda i, j: (i, 0)),
                  pl.BlockSpec((1, D_MODEL), lambda i, j: (0, 0)),
                  pl.BlockSpec((tile, D_MODEL), lambda i, j: (i, 0)),
                  pl.BlockSpec(memory_space=pl.ANY)],
        out_specs=[pl.BlockSpec((tile, D_MODEL), lambda i, j: (i, 0)),
                   pl.BlockSpec((tile, D_MODEL), lambda i, j: (i, 0)),
                   pl.BlockSpec((8, D_MODEL), lambda i, j: (0, 0)),
                   pl.BlockSpec((2, tile, nb), lambda i, j: (0, i, j)),
                   pl.BlockSpec((2, 8, D_FF), lambda i, j: (0, 0, 0))],
        out_shape=[jax.ShapeDtypeStruct((s, D_MODEL), F32), jax.ShapeDtypeStruct((s, D_MODEL), BF16),
                   jax.ShapeDtypeStruct((8, D_MODEL), F32), jax.ShapeDtypeStruct((2, s, D_FF), BF16),
                   jax.ShapeDtypeStruct((2, 8, D_FF), F32)],
        scratch_shapes=[pltpu.VMEM((tile + HALO_B, nb), F32), pltpu.VMEM((blocks, 2, 3, 8, nb), F32),
                        pltpu.VMEM((tile, D_MODEL), F32), pltpu.VMEM((2, 3, FFN_ROWS, nb), F32)],
        compiler_params=_params("arbitrary", "arbitrary"),
    )(dx3b, dx3b, w_down, p, u, u, fcw, w_up_b, w_up_b, x2, g2.reshape(1, -1), dx3, _behind(token))


def _adamw_math(w, g, m, v):
    m = ADAM_B1 * m + (1.0 - ADAM_B1) * g
    v = ADAM_B2 * v + (1.0 - ADAM_B2) * (g * g)
    m_hat = m / (1.0 - ADAM_B1 ** ADAM_STEP)
    v_hat = v / (1.0 - ADAM_B2 ** ADAM_STEP)
    delta = -ADAM_LR * (m_hat / (jnp.sqrt(v_hat) + ADAM_EPS) + ADAM_WD * w)
    return delta, m, v


def _adamw(w, g, m, v, *, name):
    rows, cols = w.shape
    tr = rows
    for cand in (512, 256, 128, 64, 32, 16, 8):
        if rows % cand == 0 and rows > cand:
            tr = cand
            break

    def body(w_ref, g_ref, m_ref, v_ref, d_ref, nm_ref, nv_ref):
        d, nm, nv = _adamw_math(w_ref[...], g_ref[...], m_ref[...], v_ref[...])
        d_ref[...] = d
        nm_ref[...] = nm
        nv_ref[...] = nv

    spec = pl.BlockSpec((tr, cols), lambda i: (i, 0))
    return pl.pallas_call(
        body, name=name, grid=(rows // tr,), in_specs=[spec] * 4, out_specs=[spec] * 3,
        out_shape=[jax.ShapeDtypeStruct((rows, cols), F32)] * 3,
        compiler_params=_params("parallel"),
    )(w, g, m, v)


def _adamw_many(ws, gs, ms, vs, *, name):
    n = len(ws)

    def body(*refs):
        for i in range(n):
            d, nm, nv = _adamw_math(refs[i][...], refs[n + i][...], refs[2 * n + i][...], refs[3 * n + i][...])
            refs[4 * n + 3 * i][...] = d
            refs[4 * n + 3 * i + 1][...] = nm
            refs[4 * n + 3 * i + 2][...] = nv

    vmem = pl.BlockSpec(memory_space=pltpu.VMEM)
    outs = pl.pallas_call(
        body, name=name, in_specs=[vmem] * (4 * n), out_specs=[vmem] * (3 * n),
        out_shape=[jax.ShapeDtypeStruct(a.shape, F32) for a in ws for _ in range(3)],
        compiler_params=pltpu.CompilerParams(vmem_limit_bytes=VMEM_LIMIT),
    )(*ws, *gs, *ms, *vs)
    return [tuple(outs[3 * i:3 * i + 3]) for i in range(n)]


def _sum_adamw(parts, w, m, v, *, name):
    depth, rows, cols = w.shape
    tr = rows
    for cand in (256, 128, 64):
        if rows % cand == 0 and rows > cand:
            tr = cand
            break

    def body(*refs):
        part_refs = refs[:depth]
        w_ref, m_ref, v_ref, g_ref, d_ref, nm_ref, nv_ref = refs[depth:]
        layer = pl.program_id(0)
        grad = None
        for k, p_ref in enumerate(part_refs):
            total = p_ref[0].astype(F32)
            for dev in range(1, N_DEV):
                total = total + p_ref[dev].astype(F32)
            grad = total if grad is None else jnp.where(layer == k, total, grad)
        d, nm, nv = _adamw_math(w_ref[...], grad, m_ref[...], v_ref[...])
        g_ref[...] = grad
        d_ref[...] = d
        nm_ref[...] = nm
        nv_ref[...] = nv

    part_spec = lambda k: pl.BlockSpec((N_DEV, tr, cols), lambda l, i: (0, jnp.where(l == k, i, 0), 0))
    spec = pl.BlockSpec((None, tr, cols), lambda l, i: (l, i, 0))
    return pl.pallas_call(
        body, name=name, grid=(depth, rows // tr),
        in_specs=[part_spec(k) for k in range(depth)] + [spec] * 3, out_specs=[spec] * 4,
        out_shape=[jax.ShapeDtypeStruct((depth, rows, cols), F32)] * 4,
        compiler_params=_params("parallel", "parallel"),
    )(*parts, w, m, v)


def _sum_parts(parts, *, name, tokens=()):
    _, rows, cols = parts.shape
    tr = rows
    for cand in (256, 128, 64, 32, 16):
        if rows % cand == 0 and rows > cand:
            tr = cand
            break

    def body(p_ref, *rest):
        acc = p_ref[0].astype(F32)
        for d in range(1, N_DEV):
            acc = acc + p_ref[d].astype(F32)
        rest[-1][...] = acc

    return pl.pallas_call(
        body, name=name, grid=(rows // tr,),
        in_specs=[pl.BlockSpec((N_DEV, tr, cols), lambda i: (0, i, 0))]
        + [pl.BlockSpec(memory_space=pl.ANY)] * len(tokens),
        out_specs=pl.BlockSpec((tr, cols), lambda i: (i, 0)),
        out_shape=jax.ShapeDtypeStruct((rows, cols), F32),
        compiler_params=_params("parallel"),
    )(parts, *tokens)


def _place():
    return lax.axis_index("x"), lax.axis_index("y"), lax.axis_index("c")


def _flip(v, bit):
    return 1 - v if bit else v


N_PEERS = N_DEV - 1


def _peer_copy(k, src_ref, land_ref, send_sem, recv_sem, gather):
    x, y, c = _place()
    my_id = 4 * x + 2 * y + c
    px, py, pc = _flip(x, k & 4), _flip(y, k & 2), _flip(c, k & 1)
    peer_id = 4 * px + 2 * py + pc
    return pltpu.make_async_remote_copy(
        src_ref=src_ref if gather else src_ref.at[peer_id], dst_ref=land_ref.at[my_id],
        send_sem=send_sem.at[k - 1], recv_sem=recv_sem.at[k - 1],
        device_id=(px, py, pc), device_id_type=MESH)


def _sequencer_copies(srcs, *, gather, name, collective_id, after):
    n = len(srcs)
    hbm = pltpu.MemorySpace.HBM
    src_refs = [jax.new_ref(s, memory_space=hbm) for s in srcs]
    land_refs = [jax.empty_ref(jax.ShapeDtypeStruct(((N_DEV,) + s.shape) if gather else s.shape, s.dtype),
                               memory_space=hbm) for s in srcs]
    token_in = jax.new_ref(jnp.zeros((8, 128), F32) if after is None else after, memory_space=hbm)
    token_out = jax.empty_ref(jax.ShapeDtypeStruct((8, 128), F32), memory_space=hbm)

    @pl.kernel(mesh=plsc.ScalarSubcoreMesh(axis_name="seq", num_cores=1), name=name,
               scratch_types=(pltpu.SemaphoreType.DMA((n, N_PEERS)), pltpu.SemaphoreType.DMA((n, N_PEERS)),
                              pltpu.SemaphoreType.DMA((n + 1,))),
               compiler_params=pltpu.CompilerParams(collective_id=collective_id))
    def launch(send_sems, recv_sems, local_sems):
        x, y, c = _place()
        my_id = 4 * x + 2 * y + c
        barrier = pltpu.get_barrier_semaphore()
        own = [pltpu.make_async_copy(src_refs[t] if gather else src_refs[t].at[my_id], land_refs[t].at[my_id],
                                     local_sems.at[t]) for t in range(n)]
        if gather:
            sibling = (x, y, 1 - c)
            chips = [(1 - x, y), (x, 1 - y), (1 - x, 1 - y)]
            for peer in [sibling] + [(*chip, c) for chip in chips]:
                pl.semaphore_signal(barrier, inc=1, device_id=peer, device_id_type=MESH)
            pl.semaphore_wait(barrier, 4)

            def copy(t, k, block, to, src=None):
                dst = land_refs[t].at[4 * block[0] + 2 * block[1] + block[2]]
                return pltpu.make_async_remote_copy(
                    src_ref=dst if src is None else src, dst_ref=dst,
                    send_sem=send_sems.at[t, k], recv_sem=recv_sems.at[t, k], device_id=to, device_id_type=MESH)

            for cp in own:
                cp.start()
            sends = []
            for t in range(n):
                sends.append(copy(t, 0, (x, y, c), sibling, src=src_refs[t]))
                sends += [copy(t, 1 + j, (x, y, c), (*chip, c), src=src_refs[t]) for j, chip in enumerate(chips)]
            for cp in sends:
                cp.start()
            for t in range(n):
                for j, chip in enumerate(chips):
                    copy(t, 1 + j, (*chip, c), (x, y, c)).wait_recv()
                    passed_on = copy(t, 4 + j, (*chip, c), sibling)
                    passed_on.start()
                    sends.append(passed_on)
            for t in range(n):
                copy(t, 0, sibling, (x, y, c)).wait_recv()
                for j, chip in enumerate(chips):
                    copy(t, 4 + j, (*chip, 1 - c), (x, y, c)).wait_recv()
            for cp in sends:
                cp.wait_send()
            for cp in own:
                cp.wait()
        else:
            for k in range(1, N_DEV):
                peer = (_flip(x, k & 4), _flip(y, k & 2), _flip(c, k & 1))
                pl.semaphore_signal(barrier, inc=1, device_id=peer, device_id_type=MESH)
            pl.semaphore_wait(barrier, N_PEERS)
            for cp in own:
                cp.start()
            copies = [_peer_copy(k, src_refs[t], land_refs[t], send_sems.at[t], recv_sems.at[t], gather)
                      for t in range(n) for k in range(1, N_DEV)]
            for cp in copies:
                cp.start()
            for cp in own:
                cp.wait()
            for cp in copies:
                cp.wait()
        passed = pltpu.make_async_copy(token_in, token_out, local_sems.at[n])
        passed.start()
        passed.wait()

    launch()
    return [ref[...] for ref in land_refs], token_out[...]


TM = 512
TMM = 1024
TKW = 2048
MIX_TILE = 256
FFN_FWD_TILE = 512
FFN_BWD_TILE = 512


def _block_diag(w):
    wg = w.reshape(N_GROUPS, HEADS_PER_GROUP, LRU_HEAD_DIM, LRU_HEAD_DIM)
    eye = jnp.eye(HEADS_PER_GROUP, dtype=w.dtype)
    bd = wg[:, :, :, None, :] * eye[None, :, None, :, None]
    return bd.reshape(N_GROUPS, LRU_GROUP, LRU_GROUP).astype(BF16)


def _head_blocks(bd):
    b5 = bd.reshape(N_GROUPS, HEADS_PER_GROUP, LRU_HEAD_DIM, HEADS_PER_GROUP, LRU_HEAD_DIM)
    blocks = [b5[:, h, :, h, :] for h in range(HEADS_PER_GROUP)]
    return jnp.stack(blocks, axis=1).reshape(LRU_HEADS, LRU_HEAD_DIM, LRU_HEAD_DIM)


def _w(lw, key, after):
    value = lw[key]
    return value(after) if callable(value) else value


def _layer_fwd(x, lw, tag):
    sv_rows = x.shape[0]
    z, h1 = _norm_in_proj(x, lw["g1"], _w(lw, "w_in_t", x), tm=min(TMM, sv_rows), tn=896, name=f"in_proj_{tag}")
    y_mix, hs = _mixer_fwd(z, _w(lw, "cw", z), lw["cb"], lw["wa_bd"], lw["wx_bd"], lw["ba"], lw["bx"], lw["lam"],
                           _w(lw, "scw", z), tile=MIX_TILE, name=f"mixer_fwd_{tag}")
    x2 = _mm_nn(y_mix, _w(lw, "w_out", y_mix), tm=min(TMM, sv_rows), tn=D_MODEL, tk=D_MIX, out_dtype=F32, name=f"out_proj_{tag}",
                residual=x)
    x3, h2, act, p, u = _ffn_block_fwd(x2, lw["g2"], _w(lw, "w_up_b", x2), _w(lw, "fcw", x2), _w(lw, "w_down", x2),
                                       tile=min(FFN_FWD_TILE, sv_rows), name=f"ffn_fwd_{tag}")
    saved = dict(x=x, h1=h1, z=z, y_mix=y_mix, hs=hs, x2=x2, h2=h2, p=p, u=u, act=act)
    return x3, saved


def _layer_bwd(dx3, dx3b, lw, sv, tag, put):
    sv_rows = dx3.shape[0]
    w_in_t, w_out, w_up_b, w_down = (_w(lw, k, dx3) for k in ("w_in_t", "w_out", "w_up_b", "w_down"))
    cw, scw, fcw = (_w(lw, k, dx3) for k in ("cw", "scw", "fcw"))
    g_down = _mm_tn(sv["act"], dx3b, tm=1024, tn=D_MODEL, tk=min(TKW, sv_rows), out_dtype=BF16, name=f"down_bwd_w_{tag}")
    dx2, dx2b, dg2, dp, dfcw = _ffn_block_bwd(dx3, dx3b, sv["p"], sv["u"], sv["x2"], lw["g2"], w_up_b, fcw, w_down,
                                              tile=min(FFN_BWD_TILE, sv_rows), name=f"ffn_bwd_{tag}",
                                              token=put("w_down", g_down))
    g_up = _mm_up_bwd_w(sv["h2"], dp, tm=D_MODEL, tk=min(TKW, sv_rows), name=f"up_bwd_w_{tag}")
    dy = _mm_nt(dx2b, w_out, tm=min(TMM, sv_rows), tn=768, tk=D_MODEL, out_dtype=F32, name=f"out_bwd_x_{tag}")
    dz, dcw, dvec, dwa, dwx, dscw = _mixer_bwd_rows(
        sv["z"], sv["hs"], dy, cw, lw["cb"], lw["wa_bd"], lw["wx_bd"], lw["ba"], lw["bx"], lw["lam"],
        scw, tile=MIX_TILE, name=f"mixer_bwd_{tag}", token=put("w_up_b", g_up))
    g_out = _mm_tn(sv["y_mix"], dx2b, tm=768, tn=D_MODEL, tk=min(TKW, sv_rows), out_dtype=BF16, name=f"out_bwd_w_{tag}",
                   token=dz)
    g_in_t = _mm_tn(dz, sv["h1"], tm=896, tn=D_MODEL, tk=min(TKW, sv_rows), out_dtype=BF16, name=f"in_bwd_w_{tag}",
                    token=put("w_out", g_out))
    dx, dxb, dg1 = _in_bwd_norm(dz, w_in_t, sv["x"], lw["g1"], dx2, tm=min(TMM, sv_rows), tk=896, name=f"in_bwd_x_{tag}",
                                token=put("w_in_t", g_in_t))
    small = dict(norm1_g=dg1[0], lru_conv_w=dcw[0:4], lru_conv_b=dvec[0], lru_wa=_head_blocks(dwa),
                 lru_ba=dvec[1], lru_wx=_head_blocks(dwx), lru_bx=dvec[2], lru_lambda=dvec[3],
                 sc_conv_w=dscw[0:3], norm2_g=dg2[0], ffn_conv_w=dfcw[:, 0:3, :])
    return dx, dxb, small


SMALL_ORDER = ("norm1_g", "lru_conv_w", "lru_conv_b", "lru_wa", "lru_ba", "lru_wx", "lru_bx", "lru_lambda",
               "sc_conv_w", "norm2_g", "ffn_conv_w")


def _local_step(x, tgt, layers, final_g, put):
    saved = []
    h = x
    for l in range(DEPTH):
        h, sv = _layer_fwd(h, layers[l], f"l{l}")
        saved.append(sv)
    loss_blk, dx, dxb, dgf = _loss_head(h, final_g, tgt, tm=TM, name="loss_head")
    smalls = [None] * DEPTH
    for l in reversed(range(DEPTH)):
        dx, dxb, smalls[l] = _layer_bwd(dx, dxb, layers[l], saved[l], f"l{l}", functools.partial(put, l))
    return loss_blk[0, 0], dx, smalls, dgf[0]


def kernel(x, norm1_g, w_in, lru_conv_w, lru_conv_b, lru_wa, lru_ba, lru_wx, lru_bx, lru_lambda, sc_conv_w, w_out, norm2_g, w_up, ffn_conv_w, w_down, final_g, loss_target, m_norm1_g, m_w_in, m_lru_conv_w, m_lru_conv_b, m_lru_wa, m_lru_ba, m_lru_wx, m_lru_bx, m_lru_lambda, m_sc_conv_w, m_w_out, m_norm2_g, m_w_up, m_ffn_conv_w, m_w_down, m_final_g, v_norm1_g, v_w_in, v_lru_conv_w, v_lru_conv_b, v_lru_wa, v_lru_ba, v_lru_wx, v_lru_bx, v_lru_lambda, v_sc_conv_w, v_w_out, v_norm2_g, v_w_up, v_ffn_conv_w, v_w_down, v_final_g):
    names = ["norm1_g", "w_in", "lru_conv_w", "lru_conv_b", "lru_wa", "lru_ba", "lru_wx", "lru_bx", "lru_lambda",
             "sc_conv_w", "w_out", "norm2_g", "w_up", "ffn_conv_w", "w_down", "final_g"]
    w = dict(zip(names, [norm1_g, w_in, lru_conv_w, lru_conv_b, lru_wa, lru_ba, lru_wx, lru_bx, lru_lambda,
                         sc_conv_w, w_out, norm2_g, w_up, ffn_conv_w, w_down, final_g]))
    m = dict(zip(names, [m_norm1_g, m_w_in, m_lru_conv_w, m_lru_conv_b, m_lru_wa, m_lru_ba, m_lru_wx, m_lru_bx,
                         m_lru_lambda, m_sc_conv_w, m_w_out, m_norm2_g, m_w_up, m_ffn_conv_w, m_w_down, m_final_g]))
    v = dict(zip(names, [v_norm1_g, v_w_in, v_lru_conv_w, v_lru_conv_b, v_lru_wa, v_lru_ba, v_lru_wx, v_lru_bx,
                         v_lru_lambda, v_sc_conv_w, v_w_out, v_norm2_g, v_w_up, v_ffn_conv_w, v_w_down, v_final_g]))
    my_id = 4 * lax.axis_index("x") + 2 * lax.axis_index("y") + lax.axis_index("c")

    taps = jnp.zeros((DEPTH, 16, 768), F32)
    taps = taps.at[:, 0:4, 0:128].set(lru_conv_w).at[:, 4:7, 0:64].set(sc_conv_w).at[:, 8:11, :].set(ffn_conv_w)
    shards = {}
    for l in range(DEPTH):
        shards[f"w_in_t{l}"] = jnp.swapaxes(w_in[l], 0, 1).astype(BF16)
        if l == 0:
            shards["taps"] = taps.reshape(DEPTH * 16, 768)
        shards[f"w_out{l}"] = w_out[l].astype(BF16)
        shards[f"w_up_b{l}"] = w_up[l].astype(BF16)
        shards[f"w_down{l}"] = w_down[l].astype(BF16)
    ids = iter(range(18))
    got = {}
    chain = [None]
    for group in (("w_in_t0", "taps"), ("w_out0",), ("w_up_b0",), ("w_down0",),
                  ("w_in_t1",), ("w_out1",), ("w_up_b1",), ("w_down1",)):
        lands, chain[0] = _sequencer_copies([shards[k] for k in group], gather=True, name=f"gather_{group[0]}",
                                            collective_id=next(ids), after=None)
        got.update(zip(group, lands))

    def fetch(key, after):
        return got[key]

    def tap_rows(l, lo, hi, width, after):
        tl = fetch("taps", after).reshape(N_DEV, DEPTH, 16, 768)[:, l, lo:hi, 0:width]
        return jnp.transpose(tl, (1, 0, 2)).reshape(hi - lo, N_DEV * width)

    layers = []
    for l in range(DEPTH):
        layers.append(dict(
            g1=norm1_g[l], g2=norm2_g[l], cb=lru_conv_b[l], ba=lru_ba[l], bx=lru_bx[l], lam=lru_lambda[l],
            wa_bd=_block_diag(lru_wa[l]), wx_bd=_block_diag(lru_wx[l]),
            cw=functools.partial(tap_rows, l, 0, 4, 128), scw=functools.partial(tap_rows, l, 4, 7, 64),
            fcw=lambda after, l=l: tap_rows(l, 8, 11, 768, after).reshape(3, 2, D_FF).transpose(1, 0, 2),
            w_in_t=lambda after, l=l: fetch(f"w_in_t{l}", after).reshape(D_IN, D_MODEL),
            w_out=lambda after, l=l: fetch(f"w_out{l}", after).reshape(D_MIX, D_MODEL),
            w_up_b=lambda after, l=l: fetch(f"w_up_b{l}", after),
            w_down=lambda after, l=l: fetch(f"w_down{l}", after).reshape(D_FF, D_MODEL)))

    scatter_handles = {}

    def put(l, key, grad):
        blocks = grad if grad.ndim == 3 else grad.reshape(N_DEV, grad.shape[0] // N_DEV, grad.shape[1])
        (scatter_handles[(l, key)],), chain[0] = _sequencer_copies(
            [blocks], gather=False, name=f"scatter_{key}{l}", collective_id=next(ids), after=chain[0])
        return blocks

    loss_local, dx, smalls, dgf = _local_step(x[0], loss_target[0], layers, final_g, put)

    parts = []
    for l in range(DEPTH):
        for key in ("w_in_t", "w_out", "w_up_b", "w_down"):
            parts.append(scatter_handles[(l, key)])

    flat = [smalls[l][k].reshape(-1) for l in range(DEPTH) for k in SMALL_ORDER] + [dgf.reshape(-1)]
    flat.append(jnp.broadcast_to(loss_local, (128,)))
    sizes = [f.shape[0] for f in flat]
    total = sum(sizes)
    rows = -(-total // (N_DEV * 128 * 8)) * 8
    flat.append(jnp.zeros((N_DEV * rows * 128 - total,), F32))
    (small_parts,), chain[0] = _sequencer_copies([jnp.concatenate(flat).reshape(N_DEV, rows, 128)], gather=False,
                                                 name="scatter_small", collective_id=next(ids), after=chain[0])
    grads, deltas, new_m, new_v = {}, {}, {}, {}
    for slot, k in ((2, "w_up"), (3, "w_down"), (1, "w_out")):
        grads[k], deltas[k], new_m[k], new_v[k] = _sum_adamw(
            [parts[4 * l + slot] for l in range(DEPTH)], w[k], m[k], v[k], name=f"adamw_{k}")
    small_mine = _sum_parts(small_parts, name="sum_small", tokens=(deltas["w_up"], deltas["w_down"]))
    (small_all,), _ = _sequencer_copies([small_mine], gather=True, name="gather_small",
                                        collective_id=next(ids), after=chain[0])
    grads["w_in"] = jnp.stack([jnp.swapaxes(_sum_parts(parts[4 * l], name=f"sum_w_in_l{l}"), 0, 1)
                               for l in range(DEPTH)])
    small_sum = small_all.reshape(-1)
    small_g, off = [], 0
    for sz in sizes:
        small_g.append(small_sum[off:off + sz])
        off += sz
    gs = {}
    for l in range(DEPTH):
        for i, k in enumerate(SMALL_ORDER):
            gs.setdefault(k, []).append(small_g[l * len(SMALL_ORDER) + i])
    g_final = small_g[-2]
    loss = small_g[-1][0]

    for k in ("norm1_g", "lru_conv_b", "lru_ba", "lru_bx", "lru_lambda", "norm2_g"):
        grads[k] = jnp.stack(gs[k]).reshape(DEPTH, -1)
    for k in ("lru_wa", "lru_wx"):
        grads[k] = jnp.stack(gs[k]).reshape(DEPTH, LRU_HEADS, LRU_HEAD_DIM, LRU_HEAD_DIM)
    grads["final_g"] = g_final
    cw_full = jnp.stack(gs["lru_conv_w"]).reshape(DEPTH, 4, N_DEV, 128)
    grads["lru_conv_w"] = lax.dynamic_index_in_dim(cw_full, my_id, axis=2, keepdims=False)
    scw_full = jnp.stack(gs["sc_conv_w"]).reshape(DEPTH, 3, N_DEV, 64)
    grads["sc_conv_w"] = lax.dynamic_index_in_dim(scw_full, my_id, axis=2, keepdims=False)
    fcw_full = jnp.stack(gs["ffn_conv_w"]).reshape(DEPTH, 2, 3, D_FF).transpose(0, 2, 1, 3).reshape(DEPTH, 3, N_DEV, 768)
    grads["ffn_conv_w"] = lax.dynamic_index_in_dim(fcw_full, my_id, axis=2, keepdims=False)

    d, nm, nv = _adamw(w["w_in"].reshape(-1, D_IN // N_DEV), grads["w_in"].reshape(-1, D_IN // N_DEV),
                       m["w_in"].reshape(-1, D_IN // N_DEV), v["w_in"].reshape(-1, D_IN // N_DEV), name="adamw_w_in")
    deltas["w_in"], new_m["w_in"], new_v["w_in"] = (a.reshape(w["w_in"].shape) for a in (d, nm, nv))
    small_names = [k for k in names if k not in deltas]
    lanes = lambda a: a.reshape(-1, 128)
    updated = _adamw_many([lanes(w[k]) for k in small_names], [lanes(grads[k]) for k in small_names],
                          [lanes(m[k]) for k in small_names], [lanes(v[k]) for k in small_names], name="adamw_small")
    for k, (d, nm, nv) in zip(small_names, updated):
        deltas[k], new_m[k], new_v[k] = (a.reshape(w[k].shape) for a in (d, nm, nv))

    return (loss, dx[None], *[grads[k] for k in names], *[deltas[k] for k in names],
            *[new_m[k] for k in names], *[new_v[k] for k in names])
```

```python
import functools
import math

import jax
import jax.numpy as jnp
from jax import lax
from jax.experimental import pallas as pl
from jax.experimental.pallas import tpu as pltpu
from jax.experimental.pallas import tpu_sc as plsc

F32 = jnp.float32
BF16 = jnp.bfloat16

N_DEV = 8
DEPTH = 2
D_MODEL = 1024
D_LRU = 1024
D_SC = 512
D_MIX = D_LRU + D_SC
D_IN = 2 * D_LRU + 3 * D_SC
D_FF = 3072
LRU_HEADS = 16
LRU_HEAD_DIM = 64
LRU_GROUP = 256
N_GROUPS = D_LRU // LRU_GROUP
HEADS_PER_GROUP = LRU_GROUP // LRU_HEAD_DIM
RG_C = 8.0
EPS = 1e-6
HALO = 8

ADAM_LR = 0.001
ADAM_B1 = 0.9
ADAM_B2 = 0.999
ADAM_EPS = 1e-08
ADAM_WD = 0.01
ADAM_STEP = 10

GELU_C = math.sqrt(2.0 / math.pi)
GELU_A = 0.044715

VMEM_LIMIT = 56 * 1024 * 1024
MESH = pl.DeviceIdType.MESH


def _params(*sem):
    return pltpu.CompilerParams(dimension_semantics=tuple(sem) if sem else None,
                                vmem_limit_bytes=VMEM_LIMIT)


def _gelu_parts(x):
    x2 = x * x
    t = jnp.tanh(GELU_C * (x + GELU_A * x * x2))
    half = 0.5 * (1.0 + t)
    g = x * half
    dg = half + 0.5 * x * (1.0 - t * t) * (GELU_C * (1.0 + 3.0 * GELU_A * x2))
    return g, dg


def _gelu(x):
    t = jnp.tanh(GELU_C * (x + GELU_A * x * x * x))
    return 0.5 * x * (1.0 + t)


def _sigmoid(x):
    return 0.5 * jnp.tanh(0.5 * x) + 0.5


def _softplus(x):
    e = jnp.exp(-jnp.abs(x))
    u = 1.0 + e
    log1p_e = jnp.where(u == 1.0, e, jnp.log(u) * (e / (u - 1.0)))
    return jnp.maximum(x, 0.0) + log1p_e


def _rms(x):
    ms = jnp.mean(x * x, axis=-1, keepdims=True)
    return lax.rsqrt(ms + EPS)


def _dot(a, b, dims):
    return lax.dot_general(a, b, (dims, ((), ())), preferred_element_type=F32)


NN = ((1,), (0,))
NT = ((1,), (1,))
TN = ((0,), (0,))


def _matmul(a, b, *, dims, grid, a_spec, b_spec, o_spec, out_shape, acc_shape, name,
            residual=None, r_spec=None, token=None):
    nk = grid[2]

    def body(*refs):
        a_ref, b_ref = refs[0], refs[1]
        r_ref = refs[2] if residual is not None else None
        o_ref = refs[2 + (residual is not None) + (token is not None)]
        prod = _dot(a_ref[...].astype(BF16), b_ref[...].astype(BF16), dims)

        def finish(total):
            if r_ref is not None:
                total = total + r_ref[...]
            o_ref[...] = total.astype(o_ref.dtype)

        if nk == 1:
            finish(prod)
            return
        acc_ref = refs[-1]
        k = pl.program_id(2)

        @pl.when(k == 0)
        def _():
            acc_ref[...] = prod

        @pl.when(jnp.logical_and(k > 0, k < nk - 1))
        def _():
            acc_ref[...] += prod

        @pl.when(k == nk - 1)
        def _():
            finish(acc_ref[...] + prod)

    in_specs = [a_spec, b_spec]
    args = [a, b]
    if residual is not None:
        in_specs.append(r_spec)
        args.append(residual)
    if token is not None:
        in_specs.append(pl.BlockSpec(memory_space=pl.ANY))
        args.append(token)
    return pl.pallas_call(
        body, name=name, grid=grid, in_specs=in_specs, out_specs=o_spec, out_shape=out_shape,
        scratch_shapes=[pltpu.VMEM(acc_shape, F32)] if nk > 1 else [],
        compiler_params=_params("parallel", "parallel", "arbitrary"),
    )(*args)


def _mm_nn(a, b, *, tm, tn, tk, out_dtype, name, residual=None, token=None):
    m, kd = a.shape
    n = b.shape[1]
    return _matmul(
        a, b, dims=NN, grid=(m // tm, n // tn, kd // tk),
        a_spec=pl.BlockSpec((tm, tk), lambda i, j, k: (i, k)),
        b_spec=pl.BlockSpec((tk, tn), lambda i, j, k: (k, j)),
        o_spec=pl.BlockSpec((tm, tn), lambda i, j, k: (i, j)),
        out_shape=jax.ShapeDtypeStruct((m, n), out_dtype), acc_shape=(tm, tn), name=name,
        residual=residual, r_spec=pl.BlockSpec((tm, tn), lambda i, j, k: (i, j)), token=token)


def _mm_nt(a, b, *, tm, tn, tk, out_dtype, name):
    m, kd = a.shape
    n = b.shape[0]
    return _matmul(
        a, b, dims=NT, grid=(m // tm, n // tn, kd // tk),
        a_spec=pl.BlockSpec((tm, tk), lambda i, j, k: (i, k)),
        b_spec=pl.BlockSpec((tn, tk), lambda i, j, k: (j, k)),
        o_spec=pl.BlockSpec((tm, tn), lambda i, j, k: (i, j)),
        out_shape=jax.ShapeDtypeStruct((m, n), out_dtype), acc_shape=(tm, tn), name=name)


def _mm_tn(a, b, *, tm, tn, tk, out_dtype, name, token=None):
    kd, m = a.shape
    n = b.shape[1]
    return _matmul(
        a, b, dims=TN, grid=(m // tm, n // tn, kd // tk),
        a_spec=pl.BlockSpec((tk, tm), lambda i, j, k: (k, i)),
        b_spec=pl.BlockSpec((tk, tn), lambda i, j, k: (k, j)),
        o_spec=pl.BlockSpec((tm, tn), lambda i, j, k: (i, j)),
        out_shape=jax.ShapeDtypeStruct((m, n), out_dtype), acc_shape=(tm, tn), name=name, token=token)


def _mm_up_bwd_w(h2, dp, *, tm, tk, name):
    s = h2.shape[0]
    nb = D_FF * 2 // N_DEV
    per_half = D_FF // nb
    return _matmul(
        h2, dp, dims=TN, grid=(D_MODEL // tm, N_DEV, s // tk),
        a_spec=pl.BlockSpec((tk, tm), lambda i, j, k: (k, i)),
        b_spec=pl.BlockSpec((None, tk, nb), lambda i, j, k: (j // per_half, k, j % per_half)),
        o_spec=pl.BlockSpec((None, tm, nb), lambda i, j, k: (j, i, 0)),
        out_shape=jax.ShapeDtypeStruct((N_DEV, D_MODEL, nb), BF16), acc_shape=(tm, nb), name=name)


def _behind(token):
    return jnp.zeros((8, 128), F32) if token is None else token


def _norm_in_proj(x, g, w_in_t, *, tm, tn, name):
    s, d = x.shape
    n = w_in_t.shape[0]

    def body(x_ref, g_ref, w_ref, z_ref, h_ref):
        @pl.when(pl.program_id(1) == 0)
        def _():
            xv = x_ref[...]
            h_ref[...] = (xv * _rms(xv) * g_ref[...]).astype(BF16)

        z_ref[...] = _dot(h_ref[...], w_ref[...], NT)

    return pl.pallas_call(
        body, name=name, grid=(s // tm, n // tn),
        in_specs=[pl.BlockSpec((tm, d), lambda i, j: (i, 0)), pl.BlockSpec((1, d), lambda i, j: (0, 0)),
                  pl.BlockSpec((tn, d), lambda i, j: (j, 0))],
        out_specs=[pl.BlockSpec((tm, tn), lambda i, j: (i, j)), pl.BlockSpec((tm, d), lambda i, j: (i, 0))],
        out_shape=[jax.ShapeDtypeStruct((s, n), F32), jax.ShapeDtypeStruct((s, d), BF16)],
        compiler_params=_params("parallel", "arbitrary"),
    )(x, g.reshape(1, d), w_in_t)


def _in_bwd_norm(dz, w_in_t, x, g, dres, *, tm, tk, name, token=None):
    s, kd = dz.shape
    d = w_in_t.shape[1]
    nk = kd // tk

    def body(dz_ref, w_ref, x_ref, g_ref, dres_ref, token_ref, dx_ref, dxb_ref, dg_ref, acc_ref):
        i = pl.program_id(0)
        k = pl.program_id(1)

        @pl.when(jnp.logical_and(i == 0, k == 0))
        def _():
            dg_ref[...] = jnp.zeros_like(dg_ref)

        acc_ref[...] = _dot(dz_ref[...], w_ref[...], NN) + jnp.where(k > 0, acc_ref[...], 0.0)

        @pl.when(k == nk - 1)
        def _():
            dh = acc_ref[...]
            xv = x_ref[...]
            rstd = _rms(xv)
            n = xv * rstd
            dn = dh * g_ref[...]
            dx = dres_ref[...] + rstd * (dn - n * jnp.mean(dn * n, axis=-1, keepdims=True))
            dx_ref[...] = dx
            dxb_ref[...] = dx.astype(BF16)
            dg_ref[0:1, :] += jnp.sum(dh * n, axis=0, keepdims=True)

    row = pl.BlockSpec((tm, d), lambda i, k: (i, 0))
    return pl.pallas_call(
        body, name=name, grid=(s // tm, nk),
        in_specs=[pl.BlockSpec((tm, tk), lambda i, k: (i, k)), pl.BlockSpec((tk, d), lambda i, k: (k, 0)),
                  row, pl.BlockSpec((1, d), lambda i, k: (0, 0)), row, pl.BlockSpec(memory_space=pl.ANY)],
        out_specs=[row, row, pl.BlockSpec((8, d), lambda i, k: (0, 0))],
        out_shape=[jax.ShapeDtypeStruct((s, d), F32), jax.ShapeDtypeStruct((s, d), BF16),
                   jax.ShapeDtypeStruct((8, d), F32)],
        scratch_shapes=[pltpu.VMEM((tm, d), F32)],
        compiler_params=_params("arbitrary", "arbitrary"),
    )(dz, w_in_t, x, g.reshape(1, d), dres, _behind(token))


def _loss_head(x, g, tgt, *, tm, name):
    s, d = x.shape

    def body(x_ref, g_ref, t_ref, loss_ref, dx_ref, dxb_ref, dg_ref):
        @pl.when(pl.program_id(0) == 0)
        def _():
            dg_ref[...] = jnp.zeros_like(dg_ref)
            loss_ref[...] = jnp.zeros_like(loss_ref)

        xv = x_ref[...]
        gv = g_ref[...]
        rstd = _rms(xv)
        n = xv * rstd
        e = n * gv - t_ref[...]
        part = 0.5 * jnp.sum(jnp.mean(e * e, axis=-1, keepdims=True), axis=0, keepdims=True)
        loss_ref[...] += jnp.broadcast_to(part, loss_ref.shape)
        dy = e * (1.0 / d)
        dn = dy * gv
        dx = rstd * (dn - n * jnp.mean(dn * n, axis=-1, keepdims=True))
        dx_ref[...] = dx
        dxb_ref[...] = dx.astype(BF16)
        dg_ref[0:1, :] += jnp.sum(dy * n, axis=0, keepdims=True)

    return pl.pallas_call(
        body, name=name, grid=(s // tm,),
        in_specs=[pl.BlockSpec((tm, d), lambda i: (i, 0)), pl.BlockSpec((1, d), lambda i: (0, 0)),
                  pl.BlockSpec((tm, d), lambda i: (i, 0))],
        out_specs=[pl.BlockSpec((8, 128), lambda i: (0, 0)), pl.BlockSpec((tm, d), lambda i: (i, 0)),
                   pl.BlockSpec((tm, d), lambda i: (i, 0)), pl.BlockSpec((8, d), lambda i: (0, 0))],
        out_shape=[jax.ShapeDtypeStruct((8, 128), F32), jax.ShapeDtypeStruct((s, d), F32),
                   jax.ShapeDtypeStruct((s, d), BF16), jax.ShapeDtypeStruct((8, d), F32)],
        compiler_params=_params("arbitrary"),
    )(x, g.reshape(1, d), tgt)


def _scan_rows(a_ref, b_ref, h_ref, carry, *, rows, reverse):
    width = a_ref.shape[1]
    n_chunks = rows // 8
    row = lax.broadcasted_iota(jnp.int32, (8, width), 0)

    def step(ci, carry):
        chunk = (n_chunks - 1 - ci) if reverse else ci
        off = pl.multiple_of(chunk * 8, 8)
        av = a_ref[pl.ds(off, 8), :]
        bv = b_ref[pl.ds(off, 8), :]
        for sh in (1, 2, 4):
            if reverse:
                a_sh = pltpu.roll(av, 8 - sh, 0)
                b_sh = pltpu.roll(bv, 8 - sh, 0)
                m = row < 8 - sh
            else:
                a_sh = pltpu.roll(av, sh, 0)
                b_sh = pltpu.roll(bv, sh, 0)
                m = row >= sh
            bv = jnp.where(m, av * b_sh + bv, bv)
            av = jnp.where(m, av * a_sh, av)
        h = av * carry + bv
        h_ref[pl.ds(off, 8), :] = h
        return h[0:1, :] if reverse else h[7:8, :]

    return lax.fori_loop(0, n_chunks, step, carry, unroll=4)


P_CB, P_BA, P_BX, P_DECAY, P_DSP, N_PAR = 4, 5, 6, 7, 8, 9


def _spread_mixer_params(par, par_sc, cw_ref, cb_ref, ba_ref, bx_ref, lam_ref, scw_ref):
    rows = par.shape[1:]
    for k in range(4):
        par[k] = jnp.broadcast_to(cw_ref[k:k + 1, :], rows)
    par[P_CB] = jnp.broadcast_to(cb_ref[...], rows)
    par[P_BA] = jnp.broadcast_to(ba_ref[...], rows)
    par[P_BX] = jnp.broadcast_to(bx_ref[...], rows)
    par[P_DECAY] = jnp.broadcast_to(-RG_C * _softplus(-lam_ref[...]), rows)
    par[P_DSP] = jnp.broadcast_to(-_sigmoid(-lam_ref[...]), rows)
    for k in range(3):
        par_sc[k] = jnp.broadcast_to(scw_ref[k:k + 1, :], par_sc.shape[1:])


def _gates_rows(pre_r, pre_i, par):
    r = _sigmoid(pre_r + par[P_BA])
    ig = _sigmoid(pre_i + par[P_BX])
    log_a = r * par[P_DECAY]
    a = jnp.exp(log_a)
    one_minus_a2 = -jnp.tanh(log_a) * (a * a + 1.0)
    return r, ig, a, jnp.sqrt(one_minus_a2), one_minus_a2


def _mixer_fwd(z, cw, cb, wa_bd, wx_bd, ba, bx, lam, scw, *, tile, name):
    s = z.shape[0]
    n_tiles = s // tile

    rows_of = lambda r0: slice(r0, r0 + FFN_ROWS)
    col_gate, col_sb, col_sc, col_sx = (slice(D_LRU, 2 * D_LRU), slice(2 * D_LRU, 2 * D_LRU + D_SC),
                                        slice(2 * D_LRU + D_SC, 2 * D_LRU + 2 * D_SC), slice(2 * D_LRU + 2 * D_SC, D_IN))

    def body(z_ref, cw_ref, cb_ref, wa_ref, wx_ref, ba_ref, bx_ref, lam_ref, scw_ref,
             y_ref, hs_ref, par, par_sc, lx_s, lxb_s, a_s, b_s, car_lx, car_q, h_car):
        i = pl.program_id(0)

        @pl.when(i == 0)
        def _():
            car_lx[...] = jnp.zeros_like(car_lx)
            car_q[...] = jnp.zeros_like(car_q)
            h_car[...] = jnp.zeros_like(h_car)
            _spread_mixer_params(par, par_sc, cw_ref, cb_ref, ba_ref, bx_ref, lam_ref, scw_ref)

        before_lx, before_q = car_lx[...], car_q[...]
        for r0 in range(0, tile, FFN_ROWS):
            cur = z_ref[rows_of(r0), 0:D_LRU]
            lx = par[P_CB] + par[3] * cur
            for k in range(3):
                lx = lx + par[k] * _rows_from(before_lx, cur, FFN_ROWS - 3 + k)
            lx_s[rows_of(r0), :] = lx
            lxb_s[rows_of(r0), :] = lx.astype(BF16)
            before_lx = cur
            q = z_ref[rows_of(r0), col_sc] * z_ref[rows_of(r0), col_sx]
            cq = par_sc[2] * q
            for k in range(2):
                cq = cq + par_sc[k] * _rows_from(before_q, q, FFN_ROWS - 2 + k)
            y_ref[rows_of(r0), D_LRU:D_MIX] = (z_ref[rows_of(r0), col_sb] * cq).astype(BF16)
            before_q = q
        car_lx[...] = before_lx
        car_q[...] = before_q

        for g in range(N_GROUPS):
            cols = slice(g * LRU_GROUP, (g + 1) * LRU_GROUP)
            a_s[:, cols] = _dot(lxb_s[:, cols], wa_ref[g], NN)
            b_s[:, cols] = _dot(lxb_s[:, cols], wx_ref[g], NN)

        for r0 in range(0, tile, FFN_ROWS):
            _, ig, a, mult, _ = _gates_rows(a_s[rows_of(r0), :], b_s[rows_of(r0), :], par)
            a_s[rows_of(r0), :] = a
            b_s[rows_of(r0), :] = mult * (ig * lx_s[rows_of(r0), :])
        h_car[0:1, :] = _scan_rows(a_s, b_s, hs_ref, h_car[0:1, :], rows=tile, reverse=False)

        for r0 in range(0, tile, FFN_ROWS):
            y_ref[rows_of(r0), 0:D_LRU] = (hs_ref[rows_of(r0), :] * _gelu(z_ref[rows_of(r0), col_gate])).astype(BF16)

    full = lambda shape: pl.BlockSpec(shape, lambda i: (0,) * len(shape))
    return pl.pallas_call(
        body, name=name, grid=(n_tiles,),
        in_specs=[pl.BlockSpec((tile, D_IN), lambda i: (i, 0)),
                  full((4, D_LRU)), full((1, D_LRU)),
                  full((N_GROUPS, LRU_GROUP, LRU_GROUP)), full((N_GROUPS, LRU_GROUP, LRU_GROUP)),
                  full((1, D_LRU)), full((1, D_LRU)), full((1, D_LRU)), full((3, D_SC))],
        out_specs=[pl.BlockSpec((tile, D_MIX), lambda i: (i, 0)), pl.BlockSpec((tile, D_LRU), lambda i: (i, 0))],
        out_shape=[jax.ShapeDtypeStruct((s, D_MIX), BF16), jax.ShapeDtypeStruct((s, D_LRU), F32)],
        scratch_shapes=[pltpu.VMEM((N_PAR, FFN_ROWS, D_LRU), F32), pltpu.VMEM((3, FFN_ROWS, D_SC), F32),
                        pltpu.VMEM((tile, D_LRU), F32), pltpu.VMEM((tile, D_LRU), BF16),
                        pltpu.VMEM((tile, D_LRU), F32), pltpu.VMEM((tile, D_LRU), F32),
                        pltpu.VMEM((FFN_ROWS, D_LRU), F32), pltpu.VMEM((FFN_ROWS, D_SC), F32),
                        pltpu.VMEM((8, D_LRU), F32)],
        compiler_params=_params("arbitrary"),
    )(z, cw, cb.reshape(1, -1), wa_bd, wx_bd, ba.reshape(1, -1), bx.reshape(1, -1), lam.reshape(1, -1), scw)


def _fold8(x):
    return sum(x[q:q + 8] for q in range(0, x.shape[0], 8))


def _mixer_bwd_rows(z, hs, dy, cw, cb, wa_bd, wx_bd, ba, bx, lam, scw, *, tile, name, token=None):
    s = z.shape[0]
    n_tiles = s // tile
    per8 = tile // 8
    rows_of = lambda r0: slice(r0, r0 + FFN_ROWS)
    col_gate, col_sb, col_sc, col_sx = (slice(D_LRU, 2 * D_LRU), slice(2 * D_LRU, 2 * D_LRU + D_SC),
                                        slice(2 * D_LRU + D_SC, 2 * D_LRU + 2 * D_SC), slice(2 * D_LRU + 2 * D_SC, D_IN))
    up = range(0, tile, FFN_ROWS)
    down = range(tile - FFN_ROWS, -1, -FFN_ROWS)
    A_CB, A_BA, A_BX, A_SP, A_CW = 0, 1, 2, 3, 4

    def body(z_ref, zp_ref, hs_ref, hsp_ref, dy_ref, cw_ref, cb_ref, wa_ref, wx_ref, ba_ref, bx_ref, lam_ref, scw_ref,
             token_ref, dz_ref, dcw_ref, dvec_ref, dwa_ref, dwx_ref, dscw_ref,
             par, par_sc, lx_s, lxb_s, cq_s, pr_s, pi_s, r_s, ig_s, a_s, mult_s, inv_s, ash_s, b_s, lam_s,
             dlx_s, dpr_b, dpi_b, back_s, car_a, car_dlx, car_dcq, l_car, acc, acc_sc):
        i = pl.program_id(0)

        @pl.when(i == 0)
        def _():
            for ref in (dwa_ref, dwx_ref, l_car, car_a, car_dlx, car_dcq, acc, acc_sc):
                ref[...] = jnp.zeros_like(ref)
            _spread_mixer_params(par, par_sc, cw_ref, cb_ref, ba_ref, bx_ref, lam_ref, scw_ref)

        keep = jnp.where(i == n_tiles - 1, 0.0, 1.0)
        zeros8 = lambda n: jnp.zeros((8, n), F32)

        before_lx = jnp.concatenate([zeros8(D_LRU), zp_ref[:, 0:D_LRU] * keep], axis=0)
        before_q = jnp.concatenate([zeros8(D_SC), zp_ref[:, col_sc] * zp_ref[:, col_sx] * keep], axis=0)
        for r0 in up:
            cur = z_ref[rows_of(r0), 0:D_LRU]
            lx = par[P_CB] + par[3] * cur
            for k in range(3):
                lx = lx + par[k] * _rows_from(before_lx, cur, FFN_ROWS - 3 + k)
            lx_s[rows_of(r0), :] = lx
            lxb_s[rows_of(r0), :] = lx.astype(BF16)
            before_lx = cur
            q = z_ref[rows_of(r0), col_sc] * z_ref[rows_of(r0), col_sx]
            cq = par_sc[2] * q
            for k in range(2):
                cq = cq + par_sc[k] * _rows_from(before_q, q, FFN_ROWS - 2 + k)
            cq_s[rows_of(r0), :] = cq
            before_q = q

        for g in range(N_GROUPS):
            cols = slice(g * LRU_GROUP, (g + 1) * LRU_GROUP)
            pr_s[:, cols] = _dot(lxb_s[:, cols], wa_ref[g], NN)
            pi_s[:, cols] = _dot(lxb_s[:, cols], wx_ref[g], NN)

        after_a = car_a[...]
        for r0 in down:
            r, ig, a, mult, one_minus_a2 = _gates_rows(pr_s[rows_of(r0), :], pi_s[rows_of(r0), :], par)
            r_s[rows_of(r0), :] = r
            ig_s[rows_of(r0), :] = ig
            a_s[rows_of(r0), :] = a
            mult_s[rows_of(r0), :] = mult
            inv_s[rows_of(r0), :] = lax.rsqrt(one_minus_a2)
            ash_s[rows_of(r0), :] = _rows_from(a, after_a, 1)
            after_a = a
            ge, dge = _gelu_parts(z_ref[rows_of(r0), col_gate])
            dy_lru = dy_ref[rows_of(r0), 0:D_LRU]
            dz_ref[rows_of(r0), col_gate] = (dy_lru * hs_ref[rows_of(r0), :] * dge).astype(BF16)
            b_s[rows_of(r0), :] = dy_lru * ge
        car_a[...] = after_a
        l_car[0:1, :] = _scan_rows(ash_s, b_s, lam_s, l_car[0:1, :], rows=tile, reverse=True)

        before_h = jnp.concatenate([zeros8(D_LRU), hsp_ref[...] * keep], axis=0)
        for r0 in up:
            lv = lam_s[rows_of(r0), :]
            h_here = hs_ref[rows_of(r0), :]
            lx, r, ig, a = lx_s[rows_of(r0), :], r_s[rows_of(r0), :], ig_s[rows_of(r0), :], a_s[rows_of(r0), :]
            mult = mult_s[rows_of(r0), :]
            da = lv * _rows_from(before_h, h_here, FFN_ROWS - 1)
            before_h = h_here
            d_mult = lv * ig * lx
            d_i = lv * mult * lx
            dlx_s[rows_of(r0), :] = lv * mult * ig
            dlog_a = da * a - d_mult * (a * a) * inv_s[rows_of(r0), :]
            dpre_r = dlog_a * par[P_DECAY] * r * (1.0 - r)
            dpre_i = d_i * ig * (1.0 - ig)
            acc[A_BA] += _fold8(dpre_r)
            acc[A_BX] += _fold8(dpre_i)
            acc[A_SP] += _fold8(dlog_a * r)
            dpr_b[rows_of(r0), :] = dpre_r.astype(BF16)
            dpi_b[rows_of(r0), :] = dpre_i.astype(BF16)

        for g in range(N_GROUPS):
            cols = slice(g * LRU_GROUP, (g + 1) * LRU_GROUP)
            dwa_ref[g] += _dot(lxb_s[:, cols], dpr_b[:, cols], TN)
            dwx_ref[g] += _dot(lxb_s[:, cols], dpi_b[:, cols], TN)
            back_s[:, cols] = _dot(dpr_b[:, cols], wa_ref[g], NT) + _dot(dpi_b[:, cols], wx_ref[g], NT)

        after_dlx, after_dcq = car_dlx[...], car_dcq[...]
        for r0 in down:
            dlx = dlx_s[rows_of(r0), :] + back_s[rows_of(r0), :]
            lxp = z_ref[rows_of(r0), 0:D_LRU]
            acc[A_CB] += _fold8(dlx)
            acc[A_CW + 3] += _fold8(dlx * lxp)
            dlxp = par[3] * dlx
            for sh in range(1, 4):
                below = _rows_from(dlx, after_dlx, sh)
                dlxp = dlxp + par[3 - sh] * below
                acc[A_CW + 3 - sh] += _fold8(below * lxp)
            dz_ref[rows_of(r0), 0:D_LRU] = dlxp.astype(BF16)
            after_dlx = dlx

            dy_sc = dy_ref[rows_of(r0), D_LRU:D_MIX]
            sb, sc, sx = z_ref[rows_of(r0), col_sb], z_ref[rows_of(r0), col_sc], z_ref[rows_of(r0), col_sx]
            dz_ref[rows_of(r0), col_sb] = (dy_sc * cq_s[rows_of(r0), :]).astype(BF16)
            dcq = dy_sc * sb
            q = sc * sx
            acc_sc[2] += _fold8(dcq * q)
            dq = par_sc[2] * dcq
            for sh in range(1, 3):
                below = _rows_from(dcq, after_dcq, sh)
                dq = dq + par_sc[2 - sh] * below
                acc_sc[2 - sh] += _fold8(below * q)
            dz_ref[rows_of(r0), col_sc] = (dq * sx).astype(BF16)
            dz_ref[rows_of(r0), col_sx] = (dq * sc).astype(BF16)
            after_dcq = dcq
        car_dlx[...] = after_dlx
        car_dcq[...] = after_dcq

        @pl.when(i == n_tiles - 1)
        def _():
            total = lambda x: jnp.sum(x, axis=0, keepdims=True)
            dcw_ref[...] = jnp.zeros_like(dcw_ref)
            dvec_ref[...] = jnp.zeros_like(dvec_ref)
            dscw_ref[...] = jnp.zeros_like(dscw_ref)
            for k in range(4):
                dcw_ref[k:k + 1, :] = total(acc[A_CW + k])
            for k in range(3):
                dvec_ref[k:k + 1, :] = total(acc[k])
                dscw_ref[k:k + 1, :] = total(acc_sc[k])
            dvec_ref[3:4, :] = total(acc[A_SP]) * (-RG_C) * par[P_DSP][0:1, :]

    rev = lambda i: n_tiles - 1 - i
    prev8 = lambda i: jnp.maximum(rev(i) * per8 - 1, 0)
    full = lambda shape: pl.BlockSpec(shape, lambda i: (0,) * len(shape))
    wide = lambda rows, dt=F32: pltpu.VMEM((rows, D_LRU), dt)
    return pl.pallas_call(
        body, name=name, grid=(n_tiles,),
        in_specs=[pl.BlockSpec((tile, D_IN), lambda i: (rev(i), 0)),
                  pl.BlockSpec((HALO, D_IN), lambda i: (prev8(i), 0)),
                  pl.BlockSpec((tile, D_LRU), lambda i: (rev(i), 0)),
                  pl.BlockSpec((HALO, D_LRU), lambda i: (prev8(i), 0)),
                  pl.BlockSpec((tile, D_MIX), lambda i: (rev(i), 0)),
                  full((4, D_LRU)), full((1, D_LRU)),
                  full((N_GROUPS, LRU_GROUP, LRU_GROUP)), full((N_GROUPS, LRU_GROUP, LRU_GROUP)),
                  full((1, D_LRU)), full((1, D_LRU)), full((1, D_LRU)), full((3, D_SC)),
                  pl.BlockSpec(memory_space=pl.ANY)],
        out_specs=[pl.BlockSpec((tile, D_IN), lambda i: (rev(i), 0)),
                   full((8, D_LRU)), full((8, D_LRU)),
                   full((N_GROUPS, LRU_GROUP, LRU_GROUP)), full((N_GROUPS, LRU_GROUP, LRU_GROUP)),
                   full((8, D_SC))],
        out_shape=[jax.ShapeDtypeStruct((s, D_IN), BF16),
                   jax.ShapeDtypeStruct((8, D_LRU), F32), jax.ShapeDtypeStruct((8, D_LRU), F32),
                   jax.ShapeDtypeStruct((N_GROUPS, LRU_GROUP, LRU_GROUP), F32),
                   jax.ShapeDtypeStruct((N_GROUPS, LRU_GROUP, LRU_GROUP), F32),
                   jax.ShapeDtypeStruct((8, D_SC), F32)],
        scratch_shapes=[pltpu.VMEM((N_PAR, FFN_ROWS, D_LRU), F32), pltpu.VMEM((3, FFN_ROWS, D_SC), F32),
                        wide(tile), wide(tile, BF16), pltpu.VMEM((tile, D_SC), F32),
                        wide(tile), wide(tile), wide(tile), wide(tile), wide(tile), wide(tile), wide(tile),
                        wide(tile), wide(tile), wide(tile),
                        wide(tile), wide(tile, BF16), wide(tile, BF16), wide(tile),
                        wide(FFN_ROWS), wide(FFN_ROWS), pltpu.VMEM((FFN_ROWS, D_SC), F32), wide(8),
                        pltpu.VMEM((8, 8, D_LRU), F32), pltpu.VMEM((3, 8, D_SC), F32)],
        compiler_params=_params("arbitrary"),
    )(z, z, hs, hs, dy, cw, cb.reshape(1, -1), wa_bd, wx_bd, ba.reshape(1, -1), bx.reshape(1, -1),
      lam.reshape(1, -1), scw, _behind(token))


FFN_ROWS = 16
FFN_GROUPS = 2


def _spread_taps(fw_ref, taps):
    for half in range(2):
        for k in range(3):
            taps[half, k] = jnp.broadcast_to(fw_ref[half, k:k + 1, :], taps.shape[2:])


def _rows_from(first, second, start):
    stack = jnp.concatenate([first, second], axis=0)
    return pltpu.roll(stack, 2 * FFN_ROWS - start, 0)[0:FFN_ROWS]


def _conv3_rows(taps, ext_ref, half, row):
    before = ext_ref[half, row - FFN_ROWS:row, :]
    here = ext_ref[half, row:row + FFN_ROWS, :]
    acc = taps[half, 2] * here
    for k in range(2):
        acc = acc + taps[half, k] * _rows_from(before, here, FFN_ROWS - 2 + k)
    return acc


HALO_B = 16


def _ffn_block_fwd(x2, g2, w_up_b, fcw, w_down, *, tile, name):
    s = x2.shape[0]
    nb = w_up_b.shape[2]
    blocks = D_FF // nb
    per16 = tile // HALO_B

    def body(x2_ref, x2p_ref, g_ref, wg_ref, wu_ref, fw_ref, wd_ref, x3_ref, h_ref, act_ref, p_ref, u_ref,
             ext_p, acc_ref, taps, lhs):
        i = pl.program_id(0)
        j = pl.program_id(1)
        keep = jnp.where(i == 0, 0.0, 1.0)
        _spread_taps(fw_ref, taps)
        @pl.when(j == 0)
        def _():
            for rows_ref, at in ((x2p_ref, 0), (x2_ref, HALO_B)):
                xv = rows_ref[...]
                lhs[at:at + xv.shape[0], :] = (xv * _rms(xv) * g_ref[...]).astype(BF16)
            h_ref[...] = lhs[HALO_B:HALO_B + tile, :]

        grp = tile // FFN_GROUPS
        for g in range(FFN_GROUPS):
            new = slice(g * grp + (HALO_B if g else 0), (g + 1) * grp + HALO_B)
            for half, w_ref in ((0, wg_ref), (1, wu_ref)):
                pe = _dot(lhs[new, :], w_ref[...], NN)
                if g == 0:
                    ext_p[half, 0:HALO_B, :] = pe[0:HALO_B] * keep
                    ext_p[half, HALO_B:grp + HALO_B, :] = pe[HALO_B:]
                    p_ref[half, 0:grp, :] = pe[HALO_B:].astype(BF16)
                else:
                    ext_p[half, new, :] = pe
                    p_ref[half, g * grp:(g + 1) * grp, :] = pe.astype(BF16)
        for g in range(FFN_GROUPS):
            rows = slice(g * grp, (g + 1) * grp)
            acts = []
            for r0 in range(g * grp, (g + 1) * grp, FFN_ROWS):
                u = [_conv3_rows(taps, ext_p, half, HALO_B + r0) for half in range(2)]
                for half in range(2):
                    u_ref[half, r0:r0 + FFN_ROWS, :] = u[half].astype(BF16)
                acts.append((_gelu(u[0]) * u[1]).astype(BF16))
                act_ref[r0:r0 + FFN_ROWS, :] = acts[-1]
            contrib = _dot(jnp.concatenate(acts, axis=0), wd_ref[...], NN)
            acc_ref[rows, :] = contrib + jnp.where(j > 0, acc_ref[rows, :], 0.0)

        @pl.when(j == blocks - 1)
        def _():
            x3_ref[...] = x2_ref[...] + acc_ref[...]

    return pl.pallas_call(
        body, name=name, grid=(s // tile, blocks),
        in_specs=[pl.BlockSpec((tile, D_MODEL), lambda i, j: (i, 0)),
                  pl.BlockSpec((HALO_B, D_MODEL), lambda i, j: (jnp.maximum(i * per16 - 1, 0), 0)),
                  pl.BlockSpec((1, D_MODEL), lambda i, j: (0, 0)),
                  pl.BlockSpec((None, D_MODEL, nb), lambda i, j: (j, 0, 0)),
                  pl.BlockSpec((None, D_MODEL, nb), lambda i, j: (j + blocks, 0, 0)),
                  pl.BlockSpec((2, 3, nb), lambda i, j: (0, 0, j)),
                  pl.BlockSpec((nb, D_MODEL), lambda i, j: (j, 0))],
        out_specs=[pl.BlockSpec((tile, D_MODEL), lambda i, j: (i, 0)),
                   pl.BlockSpec((tile, D_MODEL), lambda i, j: (i, 0)),
                   pl.BlockSpec((tile, nb), lambda i, j: (i, j)),
                   pl.BlockSpec((2, tile, nb), lambda i, j: (0, i, j)),
                   pl.BlockSpec((2, tile, nb), lambda i, j: (0, i, j))],
        out_shape=[jax.ShapeDtypeStruct((s, D_MODEL), F32), jax.ShapeDtypeStruct((s, D_MODEL), BF16),
                   jax.ShapeDtypeStruct((s, D_FF), BF16),
                   jax.ShapeDtypeStruct((2, s, D_FF), BF16), jax.ShapeDtypeStruct((2, s, D_FF), BF16)],
        scratch_shapes=[pltpu.VMEM((2, tile + HALO_B, nb), F32), pltpu.VMEM((tile, D_MODEL), F32),
                        pltpu.VMEM((2, 3, FFN_ROWS, nb), F32), pltpu.VMEM((tile + HALO_B, D_MODEL), BF16)],
        compiler_params=_params("parallel", "arbitrary"),
    )(x2, x2, g2.reshape(1, -1), w_up_b, w_up_b, fcw, w_down)


def _ffn_block_bwd(dx3, dx3b, p, u, x2, g2, w_up_b, fcw, w_down, *, tile, name, token=None):
    s = x2.shape[0]
    nb = w_up_b.shape[2]
    blocks = D_FF // nb
    n_tiles = s // tile
    per16 = tile // HALO_B
    last16 = s // HALO_B - 1

    def body(dxb_ref, dxbn_ref, wd_ref, p_ref, u_ref, un_ref, fw_ref, wg_ref, wu_ref, x2_ref, g_ref, dx3_ref,
             token_ref, dx2_ref, dx2b_ref, dg_ref, dp_ref, dw_ref, da_s, acc_w, acc_dh, taps):
        i = pl.program_id(0)
        j = pl.program_id(1)

        @pl.when(jnp.logical_and(i == 0, j == 0))
        def _():
            acc_w[...] = jnp.zeros_like(acc_w)
            dg_ref[...] = jnp.zeros_like(dg_ref)

        keep_next = jnp.where(i == n_tiles - 1, 0.0, 1.0)
        _spread_taps(fw_ref, taps)
        lhs = jnp.concatenate([dxb_ref[...], dxbn_ref[...]], axis=0)
        grp = tile // FFN_GROUPS
        for g in reversed(range(FFN_GROUPS)):
            new = slice(g * grp, (g + 1) * grp + (HALO_B if g == FFN_GROUPS - 1 else 0))
            da_s[new, :] = _dot(lhs[new], wd_ref[...], NT)

        def du_rows(da, u_gate, u_up):
            ge, dge = _gelu_parts(u_gate)
            return da * u_up * dge, da * ge

        after = du_rows(da_s[tile:tile + HALO_B, :] * keep_next, un_ref[0].astype(F32), un_ref[1].astype(F32))
        for g in reversed(range(FFN_GROUPS)):
            rows = slice(g * grp, (g + 1) * grp)
            dps = ([], [])
            for r0 in range((g + 1) * grp - FFN_ROWS, g * grp - 1, -FFN_ROWS):
                du = du_rows(da_s[r0:r0 + FFN_ROWS, :], u_ref[0, r0:r0 + FFN_ROWS, :].astype(F32),
                             u_ref[1, r0:r0 + FFN_ROWS, :].astype(F32))
                for half in range(2):
                    below = [du[half], _rows_from(du[half], after[half], 1), _rows_from(du[half], after[half], 2)]
                    acc = taps[half, 2] * below[0]
                    for k in range(2):
                        acc = acc + taps[half, k] * below[2 - k]
                    dps[half].insert(0, acc.astype(BF16))
                    dp_ref[half, r0:r0 + FFN_ROWS, :] = dps[half][0]
                    p_rows = p_ref[half, r0:r0 + FFN_ROWS, :].astype(F32)
                    for k in range(3):
                        prod = below[2 - k] * p_rows
                        acc_w[j, half, k] += sum(prod[q:q + 8] for q in range(0, FFN_ROWS, 8))
                after = du
            contrib = (_dot(jnp.concatenate(dps[0], axis=0), wg_ref[...], NT)
                       + _dot(jnp.concatenate(dps[1], axis=0), wu_ref[...], NT))
            acc_dh[rows, :] = contrib + jnp.where(j > 0, acc_dh[rows, :], 0.0)

        @pl.when(j == blocks - 1)
        def _():
            dh = acc_dh[...]
            xv = x2_ref[...]
            rstd = _rms(xv)
            n = xv * rstd
            dn = dh * g_ref[...]
            dx = dx3_ref[...] + rstd * (dn - n * jnp.mean(dn * n, axis=-1, keepdims=True))
            dx2_ref[...] = dx
            dx2b_ref[...] = dx.astype(BF16)
            dg_ref[0:1, :] += jnp.sum(dh * n, axis=0, keepdims=True)

        @pl.when(jnp.logical_and(i == n_tiles - 1, j == blocks - 1))
        def _():
            dw_ref[...] = jnp.zeros_like(dw_ref)
            for jj in range(blocks):
                for half in range(2):
                    for k in range(3):
                        dw_ref[half, k:k + 1, jj * nb:(jj + 1) * nb] = jnp.sum(acc_w[jj, half, k], axis=0, keepdims=True)

    next16 = lambda i: jnp.minimum((i + 1) * per16, last16)
    return pl.pallas_call(
        body, name=name, grid=(n_tiles, blocks),
        in_specs=[pl.BlockSpec((tile, D_MODEL), lambda i, j: (i, 0)),
                  pl.BlockSpec((HALO_B, D_MODEL), lambda i, j: (next16(i), 0)),
                  pl.BlockSpec((nb, D_MODEL), lambda i, j: (j, 0)),
                  pl.BlockSpec((2, tile, nb), lambda i, j: (0, i, j)),
                  pl.BlockSpec((2, tile, nb), lambda i, j: (0, i, j)),
                  pl.BlockSpec((2, HALO_B, nb), lambda i, j: (0, next16(i), j)),
                  pl.BlockSpec((2, 3, nb), lambda i, j: (0, 0, j)),
                  pl.BlockSpec((None, D_MODEL, nb), lambda i, j: (j, 0, 0)),
                  pl.BlockSpec((None, D_MODEL, nb), lambda i, j: (j + blocks, 0, 0)),
                  pl.BlockSpec((tile, D_MODEL), lambda i, j: (i, 0)),
                  pl.BlockSpec((1, D_MODEL), lambda i, j: (0, 0)),
                  pl.BlockSpec((tile, D_MODEL), lambda i, j: (i, 0)),
                  pl.BlockSpec(memory_space=pl.ANY)],
        out_specs=[pl.BlockSpec((tile, D_MODEL), lambda i, j: (i, 0)),
                   pl.BlockSpec((tile, D_MODEL), lambda i, j: (i, 0)),
                   pl.BlockSpec((8, D_MODEL), lambda i, j: (0, 0)),
                   pl.BlockSpec((2, tile, nb), lambda i, j: (0, i, j)),
                   pl.BlockSpec((2, 8, D_FF), lambda i, j: (0, 0, 0))],
        out_shape=[jax.ShapeDtypeStruct((s, D_MODEL), F32), jax.ShapeDtypeStruct((s, D_MODEL), BF16),
                   jax.ShapeDtypeStruct((8, D_MODEL), F32), jax.ShapeDtypeStruct((2, s, D_FF), BF16),
                   jax.ShapeDtypeStruct((2, 8, D_FF), F32)],
        scratch_shapes=[pltpu.VMEM((tile + HALO_B, nb), F32), pltpu.VMEM((blocks, 2, 3, 8, nb), F32),
                        pltpu.VMEM((tile, D_MODEL), F32), pltpu.VMEM((2, 3, FFN_ROWS, nb), F32)],
        compiler_params=_params("arbitrary", "arbitrary"),
    )(dx3b, dx3b, w_down, p, u, u, fcw, w_up_b, w_up_b, x2, g2.reshape(1, -1), dx3, _behind(token))


def _adamw_math(w, g, m, v):
    m = ADAM_B1 * m + (1.0 - ADAM_B1) * g
    v = ADAM_B2 * v + (1.0 - ADAM_B2) * (g * g)
    m_hat = m / (1.0 - ADAM_B1 ** ADAM_STEP)
    v_hat = v / (1.0 - ADAM_B2 ** ADAM_STEP)
    delta = -ADAM_LR * (m_hat / (jnp.sqrt(v_hat) + ADAM_EPS) + ADAM_WD * w)
    return delta, m, v


def _adamw(w, g, m, v, *, name):
    rows, cols = w.shape
    tr = rows
    for cand in (512, 256, 128, 64, 32, 16, 8):
        if rows % cand == 0 and rows > cand:
            tr = cand
            break

    def body(w_ref, g_ref, m_ref, v_ref, d_ref, nm_ref, nv_ref):
        d, nm, nv = _adamw_math(w_ref[...], g_ref[...], m_ref[...], v_ref[...])
        d_ref[...] = d
        nm_ref[...] = nm
        nv_ref[...] = nv

    spec = pl.BlockSpec((tr, cols), lambda i: (i, 0))
    return pl.pallas_call(
        body, name=name, grid=(rows // tr,), in_specs=[spec] * 4, out_specs=[spec] * 3,
        out_shape=[jax.ShapeDtypeStruct((rows, cols), F32)] * 3,
        compiler_params=_params("parallel"),
    )(w, g, m, v)


def _sum_adamw(parts, w, m, v, *, name):
    depth, rows, cols = w.shape
    tr = rows
    for cand in (256, 128, 64):
        if rows % cand == 0 and rows > cand:
            tr = cand
            break

    def body(*refs):
        part_refs = refs[:depth]
        w_ref, m_ref, v_ref, g_ref, d_ref, nm_ref, nv_ref = refs[depth:]
        layer = pl.program_id(0)
        grad = None
        for k, p_ref in enumerate(part_refs):
            total = p_ref[0].astype(F32)
            for dev in range(1, N_DEV):
                total = total + p_ref[dev].astype(F32)
            grad = total if grad is None else jnp.where(layer == k, total, grad)
        d, nm, nv = _adamw_math(w_ref[...], grad, m_ref[...], v_ref[...])
        g_ref[...] = grad
        d_ref[...] = d
        nm_ref[...] = nm
        nv_ref[...] = nv

    part_spec = lambda k: pl.BlockSpec((N_DEV, tr, cols), lambda l, i: (0, jnp.where(l == k, i, 0), 0))
    spec = pl.BlockSpec((None, tr, cols), lambda l, i: (l, i, 0))
    return pl.pallas_call(
        body, name=name, grid=(depth, rows // tr),
        in_specs=[part_spec(k) for k in range(depth)] + [spec] * 3, out_specs=[spec] * 4,
        out_shape=[jax.ShapeDtypeStruct((depth, rows, cols), F32)] * 4,
        compiler_params=_params("parallel", "parallel"),
    )(*parts, w, m, v)


def _sum_parts(parts, *, name, tokens=()):
    _, rows, cols = parts.shape
    tr = rows
    for cand in (256, 128, 64, 32, 16):
        if rows % cand == 0 and rows > cand:
            tr = cand
            break

    def body(p_ref, *rest):
        acc = p_ref[0].astype(F32)
        for d in range(1, N_DEV):
            acc = acc + p_ref[d].astype(F32)
        rest[-1][...] = acc

    return pl.pallas_call(
        body, name=name, grid=(rows // tr,),
        in_specs=[pl.BlockSpec((N_DEV, tr, cols), lambda i: (0, i, 0))]
        + [pl.BlockSpec(memory_space=pl.ANY)] * len(tokens),
        out_specs=pl.BlockSpec((tr, cols), lambda i: (i, 0)),
        out_shape=jax.ShapeDtypeStruct((rows, cols), F32),
        compiler_params=_params("parallel"),
    )(parts, *tokens)


def _place():
    return lax.axis_index("x"), lax.axis_index("y"), lax.axis_index("c")


def _flip(v, bit):
    return 1 - v if bit else v


N_PEERS = N_DEV - 1


def _peer_copy(k, src_ref, land_ref, send_sem, recv_sem, gather):
    x, y, c = _place()
    my_id = 4 * x + 2 * y + c
    px, py, pc = _flip(x, k & 4), _flip(y, k & 2), _flip(c, k & 1)
    peer_id = 4 * px + 2 * py + pc
    return pltpu.make_async_remote_copy(
        src_ref=src_ref if gather else src_ref.at[peer_id], dst_ref=land_ref.at[my_id],
        send_sem=send_sem.at[k - 1], recv_sem=recv_sem.at[k - 1],
        device_id=(px, py, pc), device_id_type=MESH)


def _sequencer_copies(srcs, *, gather, name, collective_id, after):
    n = len(srcs)
    hbm = pltpu.MemorySpace.HBM
    src_refs = [jax.new_ref(s, memory_space=hbm) for s in srcs]
    land_refs = [jax.empty_ref(jax.ShapeDtypeStruct(((N_DEV,) + s.shape) if gather else s.shape, s.dtype),
                               memory_space=hbm) for s in srcs]
    token_in = jax.new_ref(jnp.zeros((8, 128), F32) if after is None else after, memory_space=hbm)
    token_out = jax.empty_ref(jax.ShapeDtypeStruct((8, 128), F32), memory_space=hbm)

    @pl.kernel(mesh=plsc.ScalarSubcoreMesh(axis_name="seq", num_cores=1), name=name,
               scratch_types=(pltpu.SemaphoreType.DMA((n, N_PEERS)), pltpu.SemaphoreType.DMA((n, N_PEERS)),
                              pltpu.SemaphoreType.DMA((n + 1,))),
               compiler_params=pltpu.CompilerParams(collective_id=collective_id))
    def launch(send_sems, recv_sems, local_sems):
        x, y, c = _place()
        my_id = 4 * x + 2 * y + c
        barrier = pltpu.get_barrier_semaphore()
        own = [pltpu.make_async_copy(src_refs[t] if gather else src_refs[t].at[my_id], land_refs[t].at[my_id],
                                     local_sems.at[t]) for t in range(n)]
        if gather:
            sibling = (x, y, 1 - c)
            chips = [(1 - x, y), (x, 1 - y), (1 - x, 1 - y)]
            for peer in [sibling] + [(*chip, c) for chip in chips]:
                pl.semaphore_signal(barrier, inc=1, device_id=peer, device_id_type=MESH)
            pl.semaphore_wait(barrier, 4)

            def copy(t, k, block, to, src=None):
                dst = land_refs[t].at[4 * block[0] + 2 * block[1] + block[2]]
                return pltpu.make_async_remote_copy(
                    src_ref=dst if src is None else src, dst_ref=dst,
                    send_sem=send_sems.at[t, k], recv_sem=recv_sems.at[t, k], device_id=to, device_id_type=MESH)

            for cp in own:
                cp.start()
            sends = []
            for t in range(n):
                sends.append(copy(t, 0, (x, y, c), sibling, src=src_refs[t]))
                sends += [copy(t, 1 + j, (x, y, c), (*chip, c), src=src_refs[t]) for j, chip in enumerate(chips)]
            for cp in sends:
                cp.start()
            for t in range(n):
                for j, chip in enumerate(chips):
                    copy(t, 1 + j, (*chip, c), (x, y, c)).wait_recv()
                    passed_on = copy(t, 4 + j, (*chip, c), sibling)
                    passed_on.start()
                    sends.append(passed_on)
            for t in range(n):
                copy(t, 0, sibling, (x, y, c)).wait_recv()
                for j, chip in enumerate(chips):
                    copy(t, 4 + j, (*chip, 1 - c), (x, y, c)).wait_recv()
            for cp in sends:
                cp.wait_send()
            for cp in own:
                cp.wait()
        else:
            for k in range(1, N_DEV):
                peer = (_flip(x, k & 4), _flip(y, k & 2), _flip(c, k & 1))
                pl.semaphore_signal(barrier, inc=1, device_id=peer, device_id_type=MESH)
            pl.semaphore_wait(barrier, N_PEERS)
            for cp in own:
                cp.start()
            copies = [_peer_copy(k, src_refs[t], land_refs[t], send_sems.at[t], recv_sems.at[t], gather)
                      for t in range(n) for k in range(1, N_DEV)]
            for cp in copies:
                cp.start()
            for cp in own:
                cp.wait()
            for cp in copies:
                cp.wait()
        passed = pltpu.make_async_copy(token_in, token_out, local_sems.at[n])
        passed.start()
        passed.wait()

    launch()
    return [ref[...] for ref in land_refs], token_out[...]


TM = 512
TMM = 1024
TKW = 2048
MIX_TILE = 256
FFN_FWD_TILE = 512
FFN_BWD_TILE = 512


def _block_diag(w):
    wg = w.reshape(N_GROUPS, HEADS_PER_GROUP, LRU_HEAD_DIM, LRU_HEAD_DIM)
    eye = jnp.eye(HEADS_PER_GROUP, dtype=w.dtype)
    bd = wg[:, :, :, None, :] * eye[None, :, None, :, None]
    return bd.reshape(N_GROUPS, LRU_GROUP, LRU_GROUP).astype(BF16)


def _head_blocks(bd):
    b5 = bd.reshape(N_GROUPS, HEADS_PER_GROUP, LRU_HEAD_DIM, HEADS_PER_GROUP, LRU_HEAD_DIM)
    blocks = [b5[:, h, :, h, :] for h in range(HEADS_PER_GROUP)]
    return jnp.stack(blocks, axis=1).reshape(LRU_HEADS, LRU_HEAD_DIM, LRU_HEAD_DIM)


def _w(lw, key, after):
    value = lw[key]
    return value(after) if callable(value) else value


def _layer_fwd(x, lw, tag):
    sv_rows = x.shape[0]
    z, h1 = _norm_in_proj(x, lw["g1"], _w(lw, "w_in_t", x), tm=min(TMM, sv_rows), tn=896, name=f"in_proj_{tag}")
    y_mix, hs = _mixer_fwd(z, _w(lw, "cw", z), lw["cb"], lw["wa_bd"], lw["wx_bd"], lw["ba"], lw["bx"], lw["lam"],
                           _w(lw, "scw", z), tile=MIX_TILE, name=f"mixer_fwd_{tag}")
    x2 = _mm_nn(y_mix, _w(lw, "w_out", y_mix), tm=min(TMM, sv_rows), tn=D_MODEL, tk=D_MIX, out_dtype=F32, name=f"out_proj_{tag}",
                residual=x)
    x3, h2, act, p, u = _ffn_block_fwd(x2, lw["g2"], _w(lw, "w_up_b", x2), _w(lw, "fcw", x2), _w(lw, "w_down", x2),
                                       tile=min(FFN_FWD_TILE, sv_rows), name=f"ffn_fwd_{tag}")
    saved = dict(x=x, h1=h1, z=z, y_mix=y_mix, hs=hs, x2=x2, h2=h2, p=p, u=u, act=act)
    return x3, saved


def _layer_bwd(dx3, dx3b, lw, sv, tag, put):
    sv_rows = dx3.shape[0]
    w_in_t, w_out, w_up_b, w_down = (_w(lw, k, dx3) for k in ("w_in_t", "w_out", "w_up_b", "w_down"))
    cw, scw, fcw = (_w(lw, k, dx3) for k in ("cw", "scw", "fcw"))
    g_down = _mm_tn(sv["act"], dx3b, tm=1024, tn=D_MODEL, tk=min(TKW, sv_rows), out_dtype=BF16, name=f"down_bwd_w_{tag}")
    dx2, dx2b, dg2, dp, dfcw = _ffn_block_bwd(dx3, dx3b, sv["p"], sv["u"], sv["x2"], lw["g2"], w_up_b, fcw, w_down,
                                              tile=min(FFN_BWD_TILE, sv_rows), name=f"ffn_bwd_{tag}",
                                              token=put("w_down", g_down))
    g_up = _mm_up_bwd_w(sv["h2"], dp, tm=D_MODEL, tk=min(TKW, sv_rows), name=f"up_bwd_w_{tag}")
    dy = _mm_nt(dx2b, w_out, tm=min(TMM, sv_rows), tn=768, tk=D_MODEL, out_dtype=F32, name=f"out_bwd_x_{tag}")
    dz, dcw, dvec, dwa, dwx, dscw = _mixer_bwd_rows(
        sv["z"], sv["hs"], dy, cw, lw["cb"], lw["wa_bd"], lw["wx_bd"], lw["ba"], lw["bx"], lw["lam"],
        scw, tile=MIX_TILE, name=f"mixer_bwd_{tag}", token=put("w_up_b", g_up))
    g_out = _mm_tn(sv["y_mix"], dx2b, tm=768, tn=D_MODEL, tk=min(TKW, sv_rows), out_dtype=BF16, name=f"out_bwd_w_{tag}",
                   token=dz)
    g_in_t = _mm_tn(dz, sv["h1"], tm=896, tn=D_MODEL, tk=min(TKW, sv_rows), out_dtype=BF16, name=f"in_bwd_w_{tag}",
                    token=put("w_out", g_out))
    dx, dxb, dg1 = _in_bwd_norm(dz, w_in_t, sv["x"], lw["g1"], dx2, tm=min(TMM, sv_rows), tk=896, name=f"in_bwd_x_{tag}",
                                token=put("w_in_t", g_in_t))
    small = dict(norm1_g=dg1[0], lru_conv_w=dcw[0:4], lru_conv_b=dvec[0], lru_wa=_head_blocks(dwa),
                 lru_ba=dvec[1], lru_wx=_head_blocks(dwx), lru_bx=dvec[2], lru_lambda=dvec[3],
                 sc_conv_w=dscw[0:3], norm2_g=dg2[0], ffn_conv_w=dfcw[:, 0:3, :])
    return dx, dxb, small


SMALL_ORDER = ("norm1_g", "lru_conv_w", "lru_conv_b", "lru_wa", "lru_ba", "lru_wx", "lru_bx", "lru_lambda",
               "sc_conv_w", "norm2_g", "ffn_conv_w")


def _local_step(x, tgt, layers, final_g, put):
    saved = []
    h = x
    for l in range(DEPTH):
        h, sv = _layer_fwd(h, layers[l], f"l{l}")
        saved.append(sv)
    loss_blk, dx, dxb, dgf = _loss_head(h, final_g, tgt, tm=TM, name="loss_head")
    smalls = [None] * DEPTH
    for l in reversed(range(DEPTH)):
        dx, dxb, smalls[l] = _layer_bwd(dx, dxb, layers[l], saved[l], f"l{l}", functools.partial(put, l))
    return loss_blk[0, 0], dx, smalls, dgf[0]


def kernel(x, norm1_g, w_in, lru_conv_w, lru_conv_b, lru_wa, lru_ba, lru_wx, lru_bx, lru_lambda, sc_conv_w, w_out, norm2_g, w_up, ffn_conv_w, w_down, final_g, loss_target, m_norm1_g, m_w_in, m_lru_conv_w, m_lru_conv_b, m_lru_wa, m_lru_ba, m_lru_wx, m_lru_bx, m_lru_lambda, m_sc_conv_w, m_w_out, m_norm2_g, m_w_up, m_ffn_conv_w, m_w_down, m_final_g, v_norm1_g, v_w_in, v_lru_conv_w, v_lru_conv_b, v_lru_wa, v_lru_ba, v_lru_wx, v_lru_bx, v_lru_lambda, v_sc_conv_w, v_w_out, v_norm2_g, v_w_up, v_ffn_conv_w, v_w_down, v_final_g):
    names = ["norm1_g", "w_in", "lru_conv_w", "lru_conv_b", "lru_wa", "lru_ba", "lru_wx", "lru_bx", "lru_lambda",
             "sc_conv_w", "w_out", "norm2_g", "w_up", "ffn_conv_w", "w_down", "final_g"]
    w = dict(zip(names, [norm1_g, w_in, lru_conv_w, lru_conv_b, lru_wa, lru_ba, lru_wx, lru_bx, lru_lambda,
                         sc_conv_w, w_out, norm2_g, w_up, ffn_conv_w, w_down, final_g]))
    m = dict(zip(names, [m_norm1_g, m_w_in, m_lru_conv_w, m_lru_conv_b, m_lru_wa, m_lru_ba, m_lru_wx, m_lru_bx,
                         m_lru_lambda, m_sc_conv_w, m_w_out, m_norm2_g, m_w_up, m_ffn_conv_w, m_w_down, m_final_g]))
    v = dict(zip(names, [v_norm1_g, v_w_in, v_lru_conv_w, v_lru_conv_b, v_lru_wa, v_lru_ba, v_lru_wx, v_lru_bx,
                         v_lru_lambda, v_sc_conv_w, v_w_out, v_norm2_g, v_w_up, v_ffn_conv_w, v_w_down, v_final_g]))
    my_id = 4 * lax.axis_index("x") + 2 * lax.axis_index("y") + lax.axis_index("c")

    taps = jnp.zeros((DEPTH, 16, 768), F32)
    taps = taps.at[:, 0:4, 0:128].set(lru_conv_w).at[:, 4:7, 0:64].set(sc_conv_w).at[:, 8:11, :].set(ffn_conv_w)
    shards = {}
    for l in range(DEPTH):
        shards[f"w_in_t{l}"] = jnp.swapaxes(w_in[l], 0, 1).astype(BF16)
        if l == 0:
            shards["taps"] = taps.reshape(DEPTH * 16, 768)
        shards[f"w_out{l}"] = w_out[l].astype(BF16)
        shards[f"w_up_b{l}"] = w_up[l].astype(BF16)
        shards[f"w_down{l}"] = w_down[l].astype(BF16)
    ids = iter(range(18))
    got = {}
    chain = [None]
    for group in (("w_in_t0", "taps"), ("w_out0",), ("w_up_b0",), ("w_down0",),
                  ("w_in_t1",), ("w_out1",), ("w_up_b1",), ("w_down1",)):
        lands, chain[0] = _sequencer_copies([shards[k] for k in group], gather=True, name=f"gather_{group[0]}",
                                            collective_id=next(ids), after=None)
        got.update(zip(group, lands))

    def fetch(key, after):
        return got[key]

    def tap_rows(l, lo, hi, width, after):
        tl = fetch("taps", after).reshape(N_DEV, DEPTH, 16, 768)[:, l, lo:hi, 0:width]
        return jnp.transpose(tl, (1, 0, 2)).reshape(hi - lo, N_DEV * width)

    layers = []
    for l in range(DEPTH):
        layers.append(dict(
            g1=norm1_g[l], g2=norm2_g[l], cb=lru_conv_b[l], ba=lru_ba[l], bx=lru_bx[l], lam=lru_lambda[l],
            wa_bd=_block_diag(lru_wa[l]), wx_bd=_block_diag(lru_wx[l]),
            cw=functools.partial(tap_rows, l, 0, 4, 128), scw=functools.partial(tap_rows, l, 4, 7, 64),
            fcw=lambda after, l=l: tap_rows(l, 8, 11, 768, after).reshape(3, 2, D_FF).transpose(1, 0, 2),
            w_in_t=lambda after, l=l: fetch(f"w_in_t{l}", after).reshape(D_IN, D_MODEL),
            w_out=lambda after, l=l: fetch(f"w_out{l}", after).reshape(D_MIX, D_MODEL),
            w_up_b=lambda after, l=l: fetch(f"w_up_b{l}", after),
            w_down=lambda after, l=l: fetch(f"w_down{l}", after).reshape(D_FF, D_MODEL)))

    scatter_handles = {}

    def put(l, key, grad):
        blocks = grad if grad.ndim == 3 else grad.reshape(N_DEV, grad.shape[0] // N_DEV, grad.shape[1])
        (scatter_handles[(l, key)],), chain[0] = _sequencer_copies(
            [blocks], gather=False, name=f"scatter_{key}{l}", collective_id=next(ids), after=chain[0])
        return blocks

    loss_local, dx, smalls, dgf = _local_step(x[0], loss_target[0], layers, final_g, put)

    parts = []
    for l in range(DEPTH):
        for key in ("w_in_t", "w_out", "w_up_b", "w_down"):
            parts.append(scatter_handles[(l, key)])

    flat = [smalls[l][k].reshape(-1) for l in range(DEPTH) for k in SMALL_ORDER] + [dgf.reshape(-1)]
    flat.append(jnp.broadcast_to(loss_local, (128,)))
    sizes = [f.shape[0] for f in flat]
    total = sum(sizes)
    rows = -(-total // (N_DEV * 128 * 8)) * 8
    flat.append(jnp.zeros((N_DEV * rows * 128 - total,), F32))
    (small_parts,), chain[0] = _sequencer_copies([jnp.concatenate(flat).reshape(N_DEV, rows, 128)], gather=False,
                                                 name="scatter_small", collective_id=next(ids), after=chain[0])
    grads, deltas, new_m, new_v = {}, {}, {}, {}
    for slot, k in ((2, "w_up"), (3, "w_down"), (1, "w_out")):
        grads[k], deltas[k], new_m[k], new_v[k] = _sum_adamw(
            [parts[4 * l + slot] for l in range(DEPTH)], w[k], m[k], v[k], name=f"adamw_{k}")
    small_mine = _sum_parts(small_parts, name="sum_small", tokens=(deltas["w_up"], deltas["w_down"]))
    (small_all,), _ = _sequencer_copies([small_mine], gather=True, name="gather_small",
                                        collective_id=next(ids), after=chain[0])
    grads["w_in"] = jnp.stack([jnp.swapaxes(_sum_parts(parts[4 * l], name=f"sum_w_in_l{l}"), 0, 1)
                               for l in range(DEPTH)])
    small_sum = small_all.reshape(-1)
    small_g, off = [], 0
    for sz in sizes:
        small_g.append(small_sum[off:off + sz])
        off += sz
    gs = {}
    for l in range(DEPTH):
        for i, k in enumerate(SMALL_ORDER):
            gs.setdefault(k, []).append(small_g[l * len(SMALL_ORDER) + i])
    g_final = small_g[-2]
    loss = small_g[-1][0]

    for k in ("norm1_g", "lru_conv_b", "lru_ba", "lru_bx", "lru_lambda", "norm2_g"):
        grads[k] = jnp.stack(gs[k]).reshape(DEPTH, -1)
    for k in ("lru_wa", "lru_wx"):
        grads[k] = jnp.stack(gs[k]).reshape(DEPTH, LRU_HEADS, LRU_HEAD_DIM, LRU_HEAD_DIM)
    grads["final_g"] = g_final
    cw_full = jnp.stack(gs["lru_conv_w"]).reshape(DEPTH, 4, N_DEV, 128)
    grads["lru_conv_w"] = lax.dynamic_index_in_dim(cw_full, my_id, axis=2, keepdims=False)
    scw_full = jnp.stack(gs["sc_conv_w"]).reshape(DEPTH, 3, N_DEV, 64)
    grads["sc_conv_w"] = lax.dynamic_index_in_dim(scw_full, my_id, axis=2, keepdims=False)
    fcw_full = jnp.stack(gs["ffn_conv_w"]).reshape(DEPTH, 2, 3, D_FF).transpose(0, 2, 1, 3).reshape(DEPTH, 3, N_DEV, 768)
    grads["ffn_conv_w"] = lax.dynamic_index_in_dim(fcw_full, my_id, axis=2, keepdims=False)

    for k in names:
        if k in deltas:
            continue
        shape = w[k].shape
        cols = shape[-1]
        as2d = lambda a: a.reshape(-1, cols)
        d, nm, nv = _adamw(as2d(w[k]), as2d(grads[k]), as2d(m[k]), as2d(v[k]), name=f"adamw_{k}")
        deltas[k], new_m[k], new_v[k] = d.reshape(shape), nm.reshape(shape), nv.reshape(shape)

    return (loss, dx[None], *[grads[k] for k in names], *[deltas[k] for k in names],
            *[new_m[k] for k in names], *[new_v[k] for k in names])
```

```python
import functools
import math

import jax
import jax.numpy as jnp
from jax import lax
from jax.experimental import pallas as pl
from jax.experimental.pallas import tpu as pltpu
from jax.experimental.pallas import tpu_sc as plsc

F32 = jnp.float32
BF16 = jnp.bfloat16

N_DEV = 8
DEPTH = 2
D_MODEL = 1024
D_LRU = 1024
D_SC = 512
D_MIX = D_LRU + D_SC
D_IN = 2 * D_LRU + 3 * D_SC
D_FF = 3072
LRU_HEADS = 16
LRU_HEAD_DIM = 64
LRU_GROUP = 256
N_GROUPS = D_LRU // LRU_GROUP
HEADS_PER_GROUP = LRU_GROUP // LRU_HEAD_DIM
RG_C = 8.0
EPS = 1e-6
HALO = 8

ADAM_LR = 0.001
ADAM_B1 = 0.9
ADAM_B2 = 0.999
ADAM_EPS = 1e-08
ADAM_WD = 0.01
ADAM_STEP = 10

GELU_C = math.sqrt(2.0 / math.pi)
GELU_A = 0.044715

VMEM_LIMIT = 56 * 1024 * 1024
MESH = pl.DeviceIdType.MESH


def _params(*sem):
    return pltpu.CompilerParams(dimension_semantics=tuple(sem) if sem else None,
                                vmem_limit_bytes=VMEM_LIMIT)


def _gelu_parts(x):
    x2 = x * x
    t = jnp.tanh(GELU_C * (x + GELU_A * x * x2))
    half = 0.5 * (1.0 + t)
    g = x * half
    dg = half + 0.5 * x * (1.0 - t * t) * (GELU_C * (1.0 + 3.0 * GELU_A * x2))
    return g, dg


def _gelu(x):
    t = jnp.tanh(GELU_C * (x + GELU_A * x * x * x))
    return 0.5 * x * (1.0 + t)


def _sigmoid(x):
    return 0.5 * jnp.tanh(0.5 * x) + 0.5


def _softplus(x):
    e = jnp.exp(-jnp.abs(x))
    u = 1.0 + e
    log1p_e = jnp.where(u == 1.0, e, jnp.log(u) * (e / (u - 1.0)))
    return jnp.maximum(x, 0.0) + log1p_e


def _rms(x):
    ms = jnp.mean(x * x, axis=-1, keepdims=True)
    return lax.rsqrt(ms + EPS)


def _dot(a, b, dims):
    return lax.dot_general(a, b, (dims, ((), ())), preferred_element_type=F32)


NN = ((1,), (0,))
NT = ((1,), (1,))
TN = ((0,), (0,))


def _matmul(a, b, *, dims, grid, a_spec, b_spec, o_spec, out_shape, acc_shape, name,
            residual=None, r_spec=None, token=None):
    nk = grid[2]

    def body(*refs):
        a_ref, b_ref = refs[0], refs[1]
        r_ref = refs[2] if residual is not None else None
        o_ref = refs[2 + (residual is not None) + (token is not None)]
        prod = _dot(a_ref[...].astype(BF16), b_ref[...].astype(BF16), dims)

        def finish(total):
            if r_ref is not None:
                total = total + r_ref[...]
            o_ref[...] = total.astype(o_ref.dtype)

        if nk == 1:
            finish(prod)
            return
        acc_ref = refs[-1]
        k = pl.program_id(2)

        @pl.when(k == 0)
        def _():
            acc_ref[...] = prod

        @pl.when(jnp.logical_and(k > 0, k < nk - 1))
        def _():
            acc_ref[...] += prod

        @pl.when(k == nk - 1)
        def _():
            finish(acc_ref[...] + prod)

    in_specs = [a_spec, b_spec]
    args = [a, b]
    if residual is not None:
        in_specs.append(r_spec)
        args.append(residual)
    if token is not None:
        in_specs.append(pl.BlockSpec(memory_space=pl.ANY))
        args.append(token)
    return pl.pallas_call(
        body, name=name, grid=grid, in_specs=in_specs, out_specs=o_spec, out_shape=out_shape,
        scratch_shapes=[pltpu.VMEM(acc_shape, F32)] if nk > 1 else [],
        compiler_params=_params("parallel", "parallel", "arbitrary"),
    )(*args)


def _mm_nn(a, b, *, tm, tn, tk, out_dtype, name, residual=None, token=None):
    m, kd = a.shape
    n = b.shape[1]
    return _matmul(
        a, b, dims=NN, grid=(m // tm, n // tn, kd // tk),
        a_spec=pl.BlockSpec((tm, tk), lambda i, j, k: (i, k)),
        b_spec=pl.BlockSpec((tk, tn), lambda i, j, k: (k, j)),
        o_spec=pl.BlockSpec((tm, tn), lambda i, j, k: (i, j)),
        out_shape=jax.ShapeDtypeStruct((m, n), out_dtype), acc_shape=(tm, tn), name=name,
        residual=residual, r_spec=pl.BlockSpec((tm, tn), lambda i, j, k: (i, j)), token=token)


def _mm_nt(a, b, *, tm, tn, tk, out_dtype, name):
    m, kd = a.shape
    n = b.shape[0]
    return _matmul(
        a, b, dims=NT, grid=(m // tm, n // tn, kd // tk),
        a_spec=pl.BlockSpec((tm, tk), lambda i, j, k: (i, k)),
        b_spec=pl.BlockSpec((tn, tk), lambda i, j, k: (j, k)),
        o_spec=pl.BlockSpec((tm, tn), lambda i, j, k: (i, j)),
        out_shape=jax.ShapeDtypeStruct((m, n), out_dtype), acc_shape=(tm, tn), name=name)


def _mm_tn(a, b, *, tm, tn, tk, out_dtype, name, token=None):
    kd, m = a.shape
    n = b.shape[1]
    return _matmul(
        a, b, dims=TN, grid=(m // tm, n // tn, kd // tk),
        a_spec=pl.BlockSpec((tk, tm), lambda i, j, k: (k, i)),
        b_spec=pl.BlockSpec((tk, tn), lambda i, j, k: (k, j)),
        o_spec=pl.BlockSpec((tm, tn), lambda i, j, k: (i, j)),
        out_shape=jax.ShapeDtypeStruct((m, n), out_dtype), acc_shape=(tm, tn), name=name, token=token)


def _mm_up_bwd_w(h2, dp, *, tm, tk, name):
    s = h2.shape[0]
    nb = D_FF * 2 // N_DEV
    per_half = D_FF // nb
    return _matmul(
        h2, dp, dims=TN, grid=(D_MODEL // tm, N_DEV, s // tk),
        a_spec=pl.BlockSpec((tk, tm), lambda i, j, k: (k, i)),
        b_spec=pl.BlockSpec((None, tk, nb), lambda i, j, k: (j // per_half, k, j % per_half)),
        o_spec=pl.BlockSpec((None, tm, nb), lambda i, j, k: (j, i, 0)),
        out_shape=jax.ShapeDtypeStruct((N_DEV, D_MODEL, nb), BF16), acc_shape=(tm, nb), name=name)


def _behind(token):
    return jnp.zeros((8, 128), F32) if token is None else token


def _norm_in_proj(x, g, w_in_t, *, tm, tn, name):
    s, d = x.shape
    n = w_in_t.shape[0]

    def body(x_ref, g_ref, w_ref, z_ref, h_ref):
        @pl.when(pl.program_id(1) == 0)
        def _():
            xv = x_ref[...]
            h_ref[...] = (xv * _rms(xv) * g_ref[...]).astype(BF16)

        z_ref[...] = _dot(h_ref[...], w_ref[...], NT)

    return pl.pallas_call(
        body, name=name, grid=(s // tm, n // tn),
        in_specs=[pl.BlockSpec((tm, d), lambda i, j: (i, 0)), pl.BlockSpec((1, d), lambda i, j: (0, 0)),
                  pl.BlockSpec((tn, d), lambda i, j: (j, 0))],
        out_specs=[pl.BlockSpec((tm, tn), lambda i, j: (i, j)), pl.BlockSpec((tm, d), lambda i, j: (i, 0))],
        out_shape=[jax.ShapeDtypeStruct((s, n), F32), jax.ShapeDtypeStruct((s, d), BF16)],
        compiler_params=_params("parallel", "arbitrary"),
    )(x, g.reshape(1, d), w_in_t)


def _in_bwd_norm(dz, w_in_t, x, g, dres, *, tm, tk, name, token=None):
    s, kd = dz.shape
    d = w_in_t.shape[1]
    nk = kd // tk

    def body(dz_ref, w_ref, x_ref, g_ref, dres_ref, token_ref, dx_ref, dxb_ref, dg_ref, acc_ref):
        i = pl.program_id(0)
        k = pl.program_id(1)

        @pl.when(jnp.logical_and(i == 0, k == 0))
        def _():
            dg_ref[...] = jnp.zeros_like(dg_ref)

        acc_ref[...] = _dot(dz_ref[...], w_ref[...], NN) + jnp.where(k > 0, acc_ref[...], 0.0)

        @pl.when(k == nk - 1)
        def _():
            dh = acc_ref[...]
            xv = x_ref[...]
            rstd = _rms(xv)
            n = xv * rstd
            dn = dh * g_ref[...]
            dx = dres_ref[...] + rstd * (dn - n * jnp.mean(dn * n, axis=-1, keepdims=True))
            dx_ref[...] = dx
            dxb_ref[...] = dx.astype(BF16)
            dg_ref[0:1, :] += jnp.sum(dh * n, axis=0, keepdims=True)

    row = pl.BlockSpec((tm, d), lambda i, k: (i, 0))
    return pl.pallas_call(
        body, name=name, grid=(s // tm, nk),
        in_specs=[pl.BlockSpec((tm, tk), lambda i, k: (i, k)), pl.BlockSpec((tk, d), lambda i, k: (k, 0)),
                  row, pl.BlockSpec((1, d), lambda i, k: (0, 0)), row, pl.BlockSpec(memory_space=pl.ANY)],
        out_specs=[row, row, pl.BlockSpec((8, d), lambda i, k: (0, 0))],
        out_shape=[jax.ShapeDtypeStruct((s, d), F32), jax.ShapeDtypeStruct((s, d), BF16),
                   jax.ShapeDtypeStruct((8, d), F32)],
        scratch_shapes=[pltpu.VMEM((tm, d), F32)],
        compiler_params=_params("arbitrary", "arbitrary"),
    )(dz, w_in_t, x, g.reshape(1, d), dres, _behind(token))


def _loss_head(x, g, tgt, *, tm, name):
    s, d = x.shape

    def body(x_ref, g_ref, t_ref, loss_ref, dx_ref, dxb_ref, dg_ref):
        @pl.when(pl.program_id(0) == 0)
        def _():
            dg_ref[...] = jnp.zeros_like(dg_ref)
            loss_ref[...] = jnp.zeros_like(loss_ref)

        xv = x_ref[...]
        gv = g_ref[...]
        rstd = _rms(xv)
        n = xv * rstd
        e = n * gv - t_ref[...]
        part = 0.5 * jnp.sum(jnp.mean(e * e, axis=-1, keepdims=True), axis=0, keepdims=True)
        loss_ref[...] += jnp.broadcast_to(part, loss_ref.shape)
        dy = e * (1.0 / d)
        dn = dy * gv
        dx = rstd * (dn - n * jnp.mean(dn * n, axis=-1, keepdims=True))
        dx_ref[...] = dx
        dxb_ref[...] = dx.astype(BF16)
        dg_ref[0:1, :] += jnp.sum(dy * n, axis=0, keepdims=True)

    return pl.pallas_call(
        body, name=name, grid=(s // tm,),
        in_specs=[pl.BlockSpec((tm, d), lambda i: (i, 0)), pl.BlockSpec((1, d), lambda i: (0, 0)),
                  pl.BlockSpec((tm, d), lambda i: (i, 0))],
        out_specs=[pl.BlockSpec((8, 128), lambda i: (0, 0)), pl.BlockSpec((tm, d), lambda i: (i, 0)),
                   pl.BlockSpec((tm, d), lambda i: (i, 0)), pl.BlockSpec((8, d), lambda i: (0, 0))],
        out_shape=[jax.ShapeDtypeStruct((8, 128), F32), jax.ShapeDtypeStruct((s, d), F32),
                   jax.ShapeDtypeStruct((s, d), BF16), jax.ShapeDtypeStruct((8, d), F32)],
        compiler_params=_params("arbitrary"),
    )(x, g.reshape(1, d), tgt)


def _scan_rows(a_ref, b_ref, h_ref, carry, *, rows, reverse):
    width = a_ref.shape[1]
    n_chunks = rows // 8
    row = lax.broadcasted_iota(jnp.int32, (8, width), 0)

    def step(ci, carry):
        chunk = (n_chunks - 1 - ci) if reverse else ci
        off = pl.multiple_of(chunk * 8, 8)
        av = a_ref[pl.ds(off, 8), :]
        bv = b_ref[pl.ds(off, 8), :]
        for sh in (1, 2, 4):
            if reverse:
                a_sh = pltpu.roll(av, 8 - sh, 0)
                b_sh = pltpu.roll(bv, 8 - sh, 0)
                m = row < 8 - sh
            else:
                a_sh = pltpu.roll(av, sh, 0)
                b_sh = pltpu.roll(bv, sh, 0)
                m = row >= sh
            bv = jnp.where(m, av * b_sh + bv, bv)
            av = jnp.where(m, av * a_sh, av)
        h = av * carry + bv
        h_ref[pl.ds(off, 8), :] = h
        return h[0:1, :] if reverse else h[7:8, :]

    return lax.fori_loop(0, n_chunks, step, carry)


P_CB, P_BA, P_BX, P_DECAY, P_DSP, N_PAR = 4, 5, 6, 7, 8, 9


def _spread_mixer_params(par, par_sc, cw_ref, cb_ref, ba_ref, bx_ref, lam_ref, scw_ref):
    rows = par.shape[1:]
    for k in range(4):
        par[k] = jnp.broadcast_to(cw_ref[k:k + 1, :], rows)
    par[P_CB] = jnp.broadcast_to(cb_ref[...], rows)
    par[P_BA] = jnp.broadcast_to(ba_ref[...], rows)
    par[P_BX] = jnp.broadcast_to(bx_ref[...], rows)
    par[P_DECAY] = jnp.broadcast_to(-RG_C * _softplus(-lam_ref[...]), rows)
    par[P_DSP] = jnp.broadcast_to(-_sigmoid(-lam_ref[...]), rows)
    for k in range(3):
        par_sc[k] = jnp.broadcast_to(scw_ref[k:k + 1, :], par_sc.shape[1:])


def _gates_rows(pre_r, pre_i, par):
    r = _sigmoid(pre_r + par[P_BA])
    ig = _sigmoid(pre_i + par[P_BX])
    log_a = r * par[P_DECAY]
    a = jnp.exp(log_a)
    one_minus_a2 = -jnp.tanh(log_a) * (a * a + 1.0)
    return r, ig, a, jnp.sqrt(one_minus_a2), one_minus_a2


def _mixer_fwd(z, cw, cb, wa_bd, wx_bd, ba, bx, lam, scw, *, tile, name):
    s = z.shape[0]
    n_tiles = s // tile

    rows_of = lambda r0: slice(r0, r0 + FFN_ROWS)
    col_gate, col_sb, col_sc, col_sx = (slice(D_LRU, 2 * D_LRU), slice(2 * D_LRU, 2 * D_LRU + D_SC),
                                        slice(2 * D_LRU + D_SC, 2 * D_LRU + 2 * D_SC), slice(2 * D_LRU + 2 * D_SC, D_IN))

    def body(z_ref, cw_ref, cb_ref, wa_ref, wx_ref, ba_ref, bx_ref, lam_ref, scw_ref,
             y_ref, hs_ref, par, par_sc, lx_s, lxb_s, a_s, b_s, car_lx, car_q, h_car):
        i = pl.program_id(0)

        @pl.when(i == 0)
        def _():
            car_lx[...] = jnp.zeros_like(car_lx)
            car_q[...] = jnp.zeros_like(car_q)
            h_car[...] = jnp.zeros_like(h_car)
            _spread_mixer_params(par, par_sc, cw_ref, cb_ref, ba_ref, bx_ref, lam_ref, scw_ref)

        before_lx, before_q = car_lx[...], car_q[...]
        for r0 in range(0, tile, FFN_ROWS):
            cur = z_ref[rows_of(r0), 0:D_LRU]
            lx = par[P_CB] + par[3] * cur
            for k in range(3):
                lx = lx + par[k] * _rows_from(before_lx, cur, FFN_ROWS - 3 + k)
            lx_s[rows_of(r0), :] = lx
            lxb_s[rows_of(r0), :] = lx.astype(BF16)
            before_lx = cur
            q = z_ref[rows_of(r0), col_sc] * z_ref[rows_of(r0), col_sx]
            cq = par_sc[2] * q
            for k in range(2):
                cq = cq + par_sc[k] * _rows_from(before_q, q, FFN_ROWS - 2 + k)
            y_ref[rows_of(r0), D_LRU:D_MIX] = (z_ref[rows_of(r0), col_sb] * cq).astype(BF16)
            before_q = q
        car_lx[...] = before_lx
        car_q[...] = before_q

        for g in range(N_GROUPS):
            cols = slice(g * LRU_GROUP, (g + 1) * LRU_GROUP)
            a_s[:, cols] = _dot(lxb_s[:, cols], wa_ref[g], NN)
            b_s[:, cols] = _dot(lxb_s[:, cols], wx_ref[g], NN)

        for r0 in range(0, tile, FFN_ROWS):
            _, ig, a, mult, _ = _gates_rows(a_s[rows_of(r0), :], b_s[rows_of(r0), :], par)
            a_s[rows_of(r0), :] = a
            b_s[rows_of(r0), :] = mult * (ig * lx_s[rows_of(r0), :])
        h_car[0:1, :] = _scan_rows(a_s, b_s, hs_ref, h_car[0:1, :], rows=tile, reverse=False)

        for r0 in range(0, tile, FFN_ROWS):
            y_ref[rows_of(r0), 0:D_LRU] = (hs_ref[rows_of(r0), :] * _gelu(z_ref[rows_of(r0), col_gate])).astype(BF16)

    full = lambda shape: pl.BlockSpec(shape, lambda i: (0,) * len(shape))
    return pl.pallas_call(
        body, name=name, grid=(n_tiles,),
        in_specs=[pl.BlockSpec((tile, D_IN), lambda i: (i, 0)),
                  full((4, D_LRU)), full((1, D_LRU)),
                  full((N_GROUPS, LRU_GROUP, LRU_GROUP)), full((N_GROUPS, LRU_GROUP, LRU_GROUP)),
                  full((1, D_LRU)), full((1, D_LRU)), full((1, D_LRU)), full((3, D_SC))],
        out_specs=[pl.BlockSpec((tile, D_MIX), lambda i: (i, 0)), pl.BlockSpec((tile, D_LRU), lambda i: (i, 0))],
        out_shape=[jax.ShapeDtypeStruct((s, D_MIX), BF16), jax.ShapeDtypeStruct((s, D_LRU), F32)],
        scratch_shapes=[pltpu.VMEM((N_PAR, FFN_ROWS, D_LRU), F32), pltpu.VMEM((3, FFN_ROWS, D_SC), F32),
                        pltpu.VMEM((tile, D_LRU), F32), pltpu.VMEM((tile, D_LRU), BF16),
                        pltpu.VMEM((tile, D_LRU), F32), pltpu.VMEM((tile, D_LRU), F32),
                        pltpu.VMEM((FFN_ROWS, D_LRU), F32), pltpu.VMEM((FFN_ROWS, D_SC), F32),
                        pltpu.VMEM((8, D_LRU), F32)],
        compiler_params=_params("arbitrary"),
    )(z, cw, cb.reshape(1, -1), wa_bd, wx_bd, ba.reshape(1, -1), bx.reshape(1, -1), lam.reshape(1, -1), scw)


def _fold8(x):
    return sum(x[q:q + 8] for q in range(0, x.shape[0], 8))


def _mixer_bwd_rows(z, hs, dy, cw, cb, wa_bd, wx_bd, ba, bx, lam, scw, *, tile, name, token=None):
    s = z.shape[0]
    n_tiles = s // tile
    per8 = tile // 8
    rows_of = lambda r0: slice(r0, r0 + FFN_ROWS)
    col_gate, col_sb, col_sc, col_sx = (slice(D_LRU, 2 * D_LRU), slice(2 * D_LRU, 2 * D_LRU + D_SC),
                                        slice(2 * D_LRU + D_SC, 2 * D_LRU + 2 * D_SC), slice(2 * D_LRU + 2 * D_SC, D_IN))
    up = range(0, tile, FFN_ROWS)
    down = range(tile - FFN_ROWS, -1, -FFN_ROWS)
    A_CB, A_BA, A_BX, A_SP, A_CW = 0, 1, 2, 3, 4

    def body(z_ref, zp_ref, hs_ref, hsp_ref, dy_ref, cw_ref, cb_ref, wa_ref, wx_ref, ba_ref, bx_ref, lam_ref, scw_ref,
             token_ref, dz_ref, dcw_ref, dvec_ref, dwa_ref, dwx_ref, dscw_ref,
             par, par_sc, lx_s, lxb_s, cq_s, pr_s, pi_s, r_s, ig_s, a_s, mult_s, inv_s, ash_s, b_s, lam_s,
             dlx_s, dpr_b, dpi_b, back_s, car_a, car_dlx, car_dcq, l_car, acc, acc_sc):
        i = pl.program_id(0)

        @pl.when(i == 0)
        def _():
            for ref in (dwa_ref, dwx_ref, l_car, car_a, car_dlx, car_dcq, acc, acc_sc):
                ref[...] = jnp.zeros_like(ref)
            _spread_mixer_params(par, par_sc, cw_ref, cb_ref, ba_ref, bx_ref, lam_ref, scw_ref)

        keep = jnp.where(i == n_tiles - 1, 0.0, 1.0)
        zeros8 = lambda n: jnp.zeros((8, n), F32)

        before_lx = jnp.concatenate([zeros8(D_LRU), zp_ref[:, 0:D_LRU] * keep], axis=0)
        before_q = jnp.concatenate([zeros8(D_SC), zp_ref[:, col_sc] * zp_ref[:, col_sx] * keep], axis=0)
        for r0 in up:
            cur = z_ref[rows_of(r0), 0:D_LRU]
            lx = par[P_CB] + par[3] * cur
            for k in range(3):
                lx = lx + par[k] * _rows_from(before_lx, cur, FFN_ROWS - 3 + k)
            lx_s[rows_of(r0), :] = lx
            lxb_s[rows_of(r0), :] = lx.astype(BF16)
            before_lx = cur
            q = z_ref[rows_of(r0), col_sc] * z_ref[rows_of(r0), col_sx]
            cq = par_sc[2] * q
            for k in range(2):
                cq = cq + par_sc[k] * _rows_from(before_q, q, FFN_ROWS - 2 + k)
            cq_s[rows_of(r0), :] = cq
            before_q = q

        for g in range(N_GROUPS):
            cols = slice(g * LRU_GROUP, (g + 1) * LRU_GROUP)
            pr_s[:, cols] = _dot(lxb_s[:, cols], wa_ref[g], NN)
            pi_s[:, cols] = _dot(lxb_s[:, cols], wx_ref[g], NN)

        after_a = car_a[...]
        for r0 in down:
            r, ig, a, mult, one_minus_a2 = _gates_rows(pr_s[rows_of(r0), :], pi_s[rows_of(r0), :], par)
            r_s[rows_of(r0), :] = r
            ig_s[rows_of(r0), :] = ig
            a_s[rows_of(r0), :] = a
            mult_s[rows_of(r0), :] = mult
            inv_s[rows_of(r0), :] = lax.rsqrt(one_minus_a2)
            ash_s[rows_of(r0), :] = _rows_from(a, after_a, 1)
            after_a = a
            ge, dge = _gelu_parts(z_ref[rows_of(r0), col_gate])
            dy_lru = dy_ref[rows_of(r0), 0:D_LRU]
            dz_ref[rows_of(r0), col_gate] = (dy_lru * hs_ref[rows_of(r0), :] * dge).astype(BF16)
            b_s[rows_of(r0), :] = dy_lru * ge
        car_a[...] = after_a
        l_car[0:1, :] = _scan_rows(ash_s, b_s, lam_s, l_car[0:1, :], rows=tile, reverse=True)

        before_h = jnp.concatenate([zeros8(D_LRU), hsp_ref[...] * keep], axis=0)
        for r0 in up:
            lv = lam_s[rows_of(r0), :]
            h_here = hs_ref[rows_of(r0), :]
            lx, r, ig, a = lx_s[rows_of(r0), :], r_s[rows_of(r0), :], ig_s[rows_of(r0), :], a_s[rows_of(r0), :]
            mult = mult_s[rows_of(r0), :]
            da = lv * _rows_from(before_h, h_here, FFN_ROWS - 1)
            before_h = h_here
            d_mult = lv * ig * lx
            d_i = lv * mult * lx
            dlx_s[rows_of(r0), :] = lv * mult * ig
            dlog_a = da * a - d_mult * (a * a) * inv_s[rows_of(r0), :]
            dpre_r = dlog_a * par[P_DECAY] * r * (1.0 - r)
            dpre_i = d_i * ig * (1.0 - ig)
            acc[A_BA] += _fold8(dpre_r)
            acc[A_BX] += _fold8(dpre_i)
            acc[A_SP] += _fold8(dlog_a * r)
            dpr_b[rows_of(r0), :] = dpre_r.astype(BF16)
            dpi_b[rows_of(r0), :] = dpre_i.astype(BF16)

        for g in range(N_GROUPS):
            cols = slice(g * LRU_GROUP, (g + 1) * LRU_GROUP)
            dwa_ref[g] += _dot(lxb_s[:, cols], dpr_b[:, cols], TN)
            dwx_ref[g] += _dot(lxb_s[:, cols], dpi_b[:, cols], TN)
            back_s[:, cols] = _dot(dpr_b[:, cols], wa_ref[g], NT) + _dot(dpi_b[:, cols], wx_ref[g], NT)

        after_dlx, after_dcq = car_dlx[...], car_dcq[...]
        for r0 in down:
            dlx = dlx_s[rows_of(r0), :] + back_s[rows_of(r0), :]
            lxp = z_ref[rows_of(r0), 0:D_LRU]
            acc[A_CB] += _fold8(dlx)
            acc[A_CW + 3] += _fold8(dlx * lxp)
            dlxp = par[3] * dlx
            for sh in range(1, 4):
                below = _rows_from(dlx, after_dlx, sh)
                dlxp = dlxp + par[3 - sh] * below
                acc[A_CW + 3 - sh] += _fold8(below * lxp)
            dz_ref[rows_of(r0), 0:D_LRU] = dlxp.astype(BF16)
            after_dlx = dlx

            dy_sc = dy_ref[rows_of(r0), D_LRU:D_MIX]
            sb, sc, sx = z_ref[rows_of(r0), col_sb], z_ref[rows_of(r0), col_sc], z_ref[rows_of(r0), col_sx]
            dz_ref[rows_of(r0), col_sb] = (dy_sc * cq_s[rows_of(r0), :]).astype(BF16)
            dcq = dy_sc * sb
            q = sc * sx
            acc_sc[2] += _fold8(dcq * q)
            dq = par_sc[2] * dcq
            for sh in range(1, 3):
                below = _rows_from(dcq, after_dcq, sh)
                dq = dq + par_sc[2 - sh] * below
                acc_sc[2 - sh] += _fold8(below * q)
            dz_ref[rows_of(r0), col_sc] = (dq * sx).astype(BF16)
            dz_ref[rows_of(r0), col_sx] = (dq * sc).astype(BF16)
            after_dcq = dcq
        car_dlx[...] = after_dlx
        car_dcq[...] = after_dcq

        @pl.when(i == n_tiles - 1)
        def _():
            total = lambda x: jnp.sum(x, axis=0, keepdims=True)
            dcw_ref[...] = jnp.zeros_like(dcw_ref)
            dvec_ref[...] = jnp.zeros_like(dvec_ref)
            dscw_ref[...] = jnp.zeros_like(dscw_ref)
            for k in range(4):
                dcw_ref[k:k + 1, :] = total(acc[A_CW + k])
            for k in range(3):
                dvec_ref[k:k + 1, :] = total(acc[k])
                dscw_ref[k:k + 1, :] = total(acc_sc[k])
            dvec_ref[3:4, :] = total(acc[A_SP]) * (-RG_C) * par[P_DSP][0:1, :]

    rev = lambda i: n_tiles - 1 - i
    prev8 = lambda i: jnp.maximum(rev(i) * per8 - 1, 0)
    full = lambda shape: pl.BlockSpec(shape, lambda i: (0,) * len(shape))
    wide = lambda rows, dt=F32: pltpu.VMEM((rows, D_LRU), dt)
    return pl.pallas_call(
        body, name=name, grid=(n_tiles,),
        in_specs=[pl.BlockSpec((tile, D_IN), lambda i: (rev(i), 0)),
                  pl.BlockSpec((HALO, D_IN), lambda i: (prev8(i), 0)),
                  pl.BlockSpec((tile, D_LRU), lambda i: (rev(i), 0)),
                  pl.BlockSpec((HALO, D_LRU), lambda i: (prev8(i), 0)),
                  pl.BlockSpec((tile, D_MIX), lambda i: (rev(i), 0)),
                  full((4, D_LRU)), full((1, D_LRU)),
                  full((N_GROUPS, LRU_GROUP, LRU_GROUP)), full((N_GROUPS, LRU_GROUP, LRU_GROUP)),
                  full((1, D_LRU)), full((1, D_LRU)), full((1, D_LRU)), full((3, D_SC)),
                  pl.BlockSpec(memory_space=pl.ANY)],
        out_specs=[pl.BlockSpec((tile, D_IN), lambda i: (rev(i), 0)),
                   full((8, D_LRU)), full((8, D_LRU)),
                   full((N_GROUPS, LRU_GROUP, LRU_GROUP)), full((N_GROUPS, LRU_GROUP, LRU_GROUP)),
                   full((8, D_SC))],
        out_shape=[jax.ShapeDtypeStruct((s, D_IN), BF16),
                   jax.ShapeDtypeStruct((8, D_LRU), F32), jax.ShapeDtypeStruct((8, D_LRU), F32),
                   jax.ShapeDtypeStruct((N_GROUPS, LRU_GROUP, LRU_GROUP), F32),
                   jax.ShapeDtypeStruct((N_GROUPS, LRU_GROUP, LRU_GROUP), F32),
                   jax.ShapeDtypeStruct((8, D_SC), F32)],
        scratch_shapes=[pltpu.VMEM((N_PAR, FFN_ROWS, D_LRU), F32), pltpu.VMEM((3, FFN_ROWS, D_SC), F32),
                        wide(tile), wide(tile, BF16), pltpu.VMEM((tile, D_SC), F32),
                        wide(tile), wide(tile), wide(tile), wide(tile), wide(tile), wide(tile), wide(tile),
                        wide(tile), wide(tile), wide(tile),
                        wide(tile), wide(tile, BF16), wide(tile, BF16), wide(tile),
                        wide(FFN_ROWS), wide(FFN_ROWS), pltpu.VMEM((FFN_ROWS, D_SC), F32), wide(8),
                        pltpu.VMEM((8, 8, D_LRU), F32), pltpu.VMEM((3, 8, D_SC), F32)],
        compiler_params=_params("arbitrary"),
    )(z, z, hs, hs, dy, cw, cb.reshape(1, -1), wa_bd, wx_bd, ba.reshape(1, -1), bx.reshape(1, -1),
      lam.reshape(1, -1), scw, _behind(token))


FFN_ROWS = 16
FFN_GROUPS = 2


def _spread_taps(fw_ref, taps):
    for half in range(2):
        for k in range(3):
            taps[half, k] = jnp.broadcast_to(fw_ref[half, k:k + 1, :], taps.shape[2:])


def _rows_from(first, second, start):
    stack = jnp.concatenate([first, second], axis=0)
    return pltpu.roll(stack, 2 * FFN_ROWS - start, 0)[0:FFN_ROWS]


def _conv3_rows(taps, ext_ref, half, row):
    before = ext_ref[half, row - FFN_ROWS:row, :]
    here = ext_ref[half, row:row + FFN_ROWS, :]
    acc = taps[half, 2] * here
    for k in range(2):
        acc = acc + taps[half, k] * _rows_from(before, here, FFN_ROWS - 2 + k)
    return acc


HALO_B = 16


def _ffn_block_fwd(x2, g2, w_up_b, fcw, w_down, *, tile, name):
    s = x2.shape[0]
    nb = w_up_b.shape[2]
    blocks = D_FF // nb
    per16 = tile // HALO_B

    def body(x2_ref, x2p_ref, g_ref, wg_ref, wu_ref, fw_ref, wd_ref, x3_ref, h_ref, act_ref, p_ref, u_ref,
             ext_p, acc_ref, taps, lhs):
        i = pl.program_id(0)
        j = pl.program_id(1)
        keep = jnp.where(i == 0, 0.0, 1.0)
        _spread_taps(fw_ref, taps)
        @pl.when(j == 0)
        def _():
            for rows_ref, at in ((x2p_ref, 0), (x2_ref, HALO_B)):
                xv = rows_ref[...]
                lhs[at:at + xv.shape[0], :] = (xv * _rms(xv) * g_ref[...]).astype(BF16)
            h_ref[...] = lhs[HALO_B:HALO_B + tile, :]

        grp = tile // FFN_GROUPS
        for g in range(FFN_GROUPS):
            new = slice(g * grp + (HALO_B if g else 0), (g + 1) * grp + HALO_B)
            for half, w_ref in ((0, wg_ref), (1, wu_ref)):
                pe = _dot(lhs[new, :], w_ref[...], NN)
                if g == 0:
                    ext_p[half, 0:HALO_B, :] = pe[0:HALO_B] * keep
                    ext_p[half, HALO_B:grp + HALO_B, :] = pe[HALO_B:]
                    p_ref[half, 0:grp, :] = pe[HALO_B:].astype(BF16)
                else:
                    ext_p[half, new, :] = pe
                    p_ref[half, g * grp:(g + 1) * grp, :] = pe.astype(BF16)
        for g in range(FFN_GROUPS):
            rows = slice(g * grp, (g + 1) * grp)
            acts = []
            for r0 in range(g * grp, (g + 1) * grp, FFN_ROWS):
                u = [_conv3_rows(taps, ext_p, half, HALO_B + r0) for half in range(2)]
                for half in range(2):
                    u_ref[half, r0:r0 + FFN_ROWS, :] = u[half].astype(BF16)
                acts.append((_gelu(u[0]) * u[1]).astype(BF16))
                act_ref[r0:r0 + FFN_ROWS, :] = acts[-1]
            contrib = _dot(jnp.concatenate(acts, axis=0), wd_ref[...], NN)
            acc_ref[rows, :] = contrib + jnp.where(j > 0, acc_ref[rows, :], 0.0)

        @pl.when(j == blocks - 1)
        def _():
            x3_ref[...] = x2_ref[...] + acc_ref[...]

    return pl.pallas_call(
        body, name=name, grid=(s // tile, blocks),
        in_specs=[pl.BlockSpec((tile, D_MODEL), lambda i, j: (i, 0)),
                  pl.BlockSpec((HALO_B, D_MODEL), lambda i, j: (jnp.maximum(i * per16 - 1, 0), 0)),
                  pl.BlockSpec((1, D_MODEL), lambda i, j: (0, 0)),
                  pl.BlockSpec((None, D_MODEL, nb), lambda i, j: (j, 0, 0)),
                  pl.BlockSpec((None, D_MODEL, nb), lambda i, j: (j + blocks, 0, 0)),
                  pl.BlockSpec((2, 3, nb), lambda i, j: (0, 0, j)),
                  pl.BlockSpec((nb, D_MODEL), lambda i, j: (j, 0))],
        out_specs=[pl.BlockSpec((tile, D_MODEL), lambda i, j: (i, 0)),
                   pl.BlockSpec((tile, D_MODEL), lambda i, j: (i, 0)),
                   pl.BlockSpec((tile, nb), lambda i, j: (i, j)),
                   pl.BlockSpec((2, tile, nb), lambda i, j: (0, i, j)),
                   pl.BlockSpec((2, tile, nb), lambda i, j: (0, i, j))],
        out_shape=[jax.ShapeDtypeStruct((s, D_MODEL), F32), jax.ShapeDtypeStruct((s, D_MODEL), BF16),
                   jax.ShapeDtypeStruct((s, D_FF), BF16),
                   jax.ShapeDtypeStruct((2, s, D_FF), BF16), jax.ShapeDtypeStruct((2, s, D_FF), BF16)],
        scratch_shapes=[pltpu.VMEM((2, tile + HALO_B, nb), F32), pltpu.VMEM((tile, D_MODEL), F32),
                        pltpu.VMEM((2, 3, FFN_ROWS, nb), F32), pltpu.VMEM((tile + HALO_B, D_MODEL), BF16)],
        compiler_params=_params("parallel", "arbitrary"),
    )(x2, x2, g2.reshape(1, -1), w_up_b, w_up_b, fcw, w_down)


def _ffn_block_bwd(dx3, dx3b, p, u, x2, g2, w_up_b, fcw, w_down, *, tile, name, token=None):
    s = x2.shape[0]
    nb = w_up_b.shape[2]
    blocks = D_FF // nb
    n_tiles = s // tile
    per16 = tile // HALO_B
    last16 = s // HALO_B - 1

    def body(dxb_ref, dxbn_ref, wd_ref, p_ref, u_ref, un_ref, fw_ref, wg_ref, wu_ref, x2_ref, g_ref, dx3_ref,
             token_ref, dx2_ref, dx2b_ref, dg_ref, dp_ref, dw_ref, da_s, acc_w, acc_dh, taps):
        i = pl.program_id(0)
        j = pl.program_id(1)

        @pl.when(jnp.logical_and(i == 0, j == 0))
        def _():
            acc_w[...] = jnp.zeros_like(acc_w)
            dg_ref[...] = jnp.zeros_like(dg_ref)

        keep_next = jnp.where(i == n_tiles - 1, 0.0, 1.0)
        _spread_taps(fw_ref, taps)
        lhs = jnp.concatenate([dxb_ref[...], dxbn_ref[...]], axis=0)
        grp = tile // FFN_GROUPS
        for g in reversed(range(FFN_GROUPS)):
            new = slice(g * grp, (g + 1) * grp + (HALO_B if g == FFN_GROUPS - 1 else 0))
            da_s[new, :] = _dot(lhs[new], wd_ref[...], NT)

        def du_rows(da, u_gate, u_up):
            ge, dge = _gelu_parts(u_gate)
            return da * u_up * dge, da * ge

        after = du_rows(da_s[tile:tile + HALO_B, :] * keep_next, un_ref[0].astype(F32), un_ref[1].astype(F32))
        for g in reversed(range(FFN_GROUPS)):
            rows = slice(g * grp, (g + 1) * grp)
            dps = ([], [])
            for r0 in range((g + 1) * grp - FFN_ROWS, g * grp - 1, -FFN_ROWS):
                du = du_rows(da_s[r0:r0 + FFN_ROWS, :], u_ref[0, r0:r0 + FFN_ROWS, :].astype(F32),
                             u_ref[1, r0:r0 + FFN_ROWS, :].astype(F32))
                for half in range(2):
                    below = [du[half], _rows_from(du[half], after[half], 1), _rows_from(du[half], after[half], 2)]
                    acc = taps[half, 2] * below[0]
                    for k in range(2):
                        acc = acc + taps[half, k] * below[2 - k]
                    dps[half].insert(0, acc.astype(BF16))
                    dp_ref[half, r0:r0 + FFN_ROWS, :] = dps[half][0]
                    p_rows = p_ref[half, r0:r0 + FFN_ROWS, :].astype(F32)
                    for k in range(3):
                        prod = below[2 - k] * p_rows
                        acc_w[j, half, k] += sum(prod[q:q + 8] for q in range(0, FFN_ROWS, 8))
                after = du
            contrib = (_dot(jnp.concatenate(dps[0], axis=0), wg_ref[...], NT)
                       + _dot(jnp.concatenate(dps[1], axis=0), wu_ref[...], NT))
            acc_dh[rows, :] = contrib + jnp.where(j > 0, acc_dh[rows, :], 0.0)

        @pl.when(j == blocks - 1)
        def _():
            dh = acc_dh[...]
            xv = x2_ref[...]
            rstd = _rms(xv)
            n = xv * rstd
            dn = dh * g_ref[...]
            dx = dx3_ref[...] + rstd * (dn - n * jnp.mean(dn * n, axis=-1, keepdims=True))
            dx2_ref[...] = dx
            dx2b_ref[...] = dx.astype(BF16)
            dg_ref[0:1, :] += jnp.sum(dh * n, axis=0, keepdims=True)

        @pl.when(jnp.logical_and(i == n_tiles - 1, j == blocks - 1))
        def _():
            dw_ref[...] = jnp.zeros_like(dw_ref)
            for jj in range(blocks):
                for half in range(2):
                    for k in range(3):
                        dw_ref[half, k:k + 1, jj * nb:(jj + 1) * nb] = jnp.sum(acc_w[jj, half, k], axis=0, keepdims=True)

    next16 = lambda i: jnp.minimum((i + 1) * per16, last16)
    return pl.pallas_call(
        body, name=name, grid=(n_tiles, blocks),
        in_specs=[pl.BlockSpec((tile, D_MODEL), lambda i, j: (i, 0)),
                  pl.BlockSpec((HALO_B, D_MODEL), lambda i, j: (next16(i), 0)),
                  pl.BlockSpec((nb, D_MODEL), lambda i, j: (j, 0)),
                  pl.BlockSpec((2, tile, nb), lambda i, j: (0, i, j)),
                  pl.BlockSpec((2, tile, nb), lambda i, j: (0, i, j)),
                  pl.BlockSpec((2, HALO_B, nb), lambda i, j: (0, next16(i), j)),
                  pl.BlockSpec((2, 3, nb), lambda i, j: (0, 0, j)),
                  pl.BlockSpec((None, D_MODEL, nb), lambda i, j: (j, 0, 0)),
                  pl.BlockSpec((None, D_MODEL, nb), lambda i, j: (j + blocks, 0, 0)),
                  pl.BlockSpec((tile, D_MODEL), lambda i, j: (i, 0)),
                  pl.BlockSpec((1, D_MODEL), lambda i, j: (0, 0)),
                  pl.BlockSpec((tile, D_MODEL), lambda i, j: (i, 0)),
                  pl.BlockSpec(memory_space=pl.ANY)],
        out_specs=[pl.BlockSpec((tile, D_MODEL), lambda i, j: (i, 0)),
                   pl.BlockSpec((tile, D_MODEL), lambda i, j: (i, 0)),
                   pl.BlockSpec((8, D_MODEL), lambda i, j: (0, 0)),
                   pl.BlockSpec((2, tile, nb), lambda i, j: (0, i, j)),
                   pl.BlockSpec((2, 8, D_FF), lambda i, j: (0, 0, 0))],
        out_shape=[jax.ShapeDtypeStruct((s, D_MODEL), F32), jax.ShapeDtypeStruct((s, D_MODEL), BF16),
                   jax.ShapeDtypeStruct((8, D_MODEL), F32), jax.ShapeDtypeStruct((2, s, D_FF), BF16),
                   jax.ShapeDtypeStruct((2, 8, D_FF), F32)],
        scratch_shapes=[pltpu.VMEM((tile + HALO_B, nb), F32), pltpu.VMEM((blocks, 2, 3, 8, nb), F32),
                        pltpu.VMEM((tile, D_MODEL), F32), pltpu.VMEM((2, 3, FFN_ROWS, nb), F32)],
        compiler_params=_params("arbitrary", "arbitrary"),
    )(dx3b, dx3b, w_down, p, u, u, fcw, w_up_b, w_up_b, x2, g2.reshape(1, -1), dx3, _behind(token))


def _adamw_math(w, g, m, v):
    m = ADAM_B1 * m + (1.0 - ADAM_B1) * g
    v = ADAM_B2 * v + (1.0 - ADAM_B2) * (g * g)
    m_hat = m / (1.0 - ADAM_B1 ** ADAM_STEP)
    v_hat = v / (1.0 - ADAM_B2 ** ADAM_STEP)
    delta = -ADAM_LR * (m_hat / (jnp.sqrt(v_hat) + ADAM_EPS) + ADAM_WD * w)
    return delta, m, v


def _adamw(w, g, m, v, *, name):
    rows, cols = w.shape
    tr = rows
    for cand in (512, 256, 128, 64, 32, 16, 8):
        if rows % cand == 0 and rows > cand:
            tr = cand
            break

    def body(w_ref, g_ref, m_ref, v_ref, d_ref, nm_ref, nv_ref):
        d, nm, nv = _adamw_math(w_ref[...], g_ref[...], m_ref[...], v_ref[...])
        d_ref[...] = d
        nm_ref[...] = nm
        nv_ref[...] = nv

    spec = pl.BlockSpec((tr, cols), lambda i: (i, 0))
    return pl.pallas_call(
        body, name=name, grid=(rows // tr,), in_specs=[spec] * 4, out_specs=[spec] * 3,
        out_shape=[jax.ShapeDtypeStruct((rows, cols), F32)] * 3,
        compiler_params=_params("parallel"),
    )(w, g, m, v)


def _sum_adamw(parts, w, m, v, *, name):
    depth, rows, cols = w.shape
    tr = rows
    for cand in (256, 128, 64):
        if rows % cand == 0 and rows > cand:
            tr = cand
            break

    def body(*refs):
        part_refs = refs[:depth]
        w_ref, m_ref, v_ref, g_ref, d_ref, nm_ref, nv_ref = refs[depth:]
        layer = pl.program_id(0)
        grad = None
        for k, p_ref in enumerate(part_refs):
            total = p_ref[0].astype(F32)
            for dev in range(1, N_DEV):
                total = total + p_ref[dev].astype(F32)
            grad = total if grad is None else jnp.where(layer == k, total, grad)
        d, nm, nv = _adamw_math(w_ref[...], grad, m_ref[...], v_ref[...])
        g_ref[...] = grad
        d_ref[...] = d
        nm_ref[...] = nm
        nv_ref[...] = nv

    part_spec = lambda k: pl.BlockSpec((N_DEV, tr, cols), lambda l, i: (0, jnp.where(l == k, i, 0), 0))
    spec = pl.BlockSpec((None, tr, cols), lambda l, i: (l, i, 0))
    return pl.pallas_call(
        body, name=name, grid=(depth, rows // tr),
        in_specs=[part_spec(k) for k in range(depth)] + [spec] * 3, out_specs=[spec] * 4,
        out_shape=[jax.ShapeDtypeStruct((depth, rows, cols), F32)] * 4,
        compiler_params=_params("parallel", "parallel"),
    )(*parts, w, m, v)


def _sum_parts(parts, *, name, tokens=()):
    _, rows, cols = parts.shape
    tr = rows
    for cand in (256, 128, 64, 32, 16):
        if rows % cand == 0 and rows > cand:
            tr = cand
            break

    def body(p_ref, *rest):
        acc = p_ref[0].astype(F32)
        for d in range(1, N_DEV):
            acc = acc + p_ref[d].astype(F32)
        rest[-1][...] = acc

    return pl.pallas_call(
        body, name=name, grid=(rows // tr,),
        in_specs=[pl.BlockSpec((N_DEV, tr, cols), lambda i: (0, i, 0))]
        + [pl.BlockSpec(memory_space=pl.ANY)] * len(tokens),
        out_specs=pl.BlockSpec((tr, cols), lambda i: (i, 0)),
        out_shape=jax.ShapeDtypeStruct((rows, cols), F32),
        compiler_params=_params("parallel"),
    )(parts, *tokens)


def _place():
    return lax.axis_index("x"), lax.axis_index("y"), lax.axis_index("c")


def _flip(v, bit):
    return 1 - v if bit else v


N_PEERS = N_DEV - 1


def _peer_copy(k, src_ref, land_ref, send_sem, recv_sem, gather):
    x, y, c = _place()
    my_id = 4 * x + 2 * y + c
    px, py, pc = _flip(x, k & 4), _flip(y, k & 2), _flip(c, k & 1)
    peer_id = 4 * px + 2 * py + pc
    return pltpu.make_async_remote_copy(
        src_ref=src_ref if gather else src_ref.at[peer_id], dst_ref=land_ref.at[my_id],
        send_sem=send_sem.at[k - 1], recv_sem=recv_sem.at[k - 1],
        device_id=(px, py, pc), device_id_type=MESH)


def _sequencer_copies(srcs, *, gather, name, collective_id, after):
    n = len(srcs)
    hbm = pltpu.MemorySpace.HBM
    src_refs = [jax.new_ref(s, memory_space=hbm) for s in srcs]
    land_refs = [jax.empty_ref(jax.ShapeDtypeStruct(((N_DEV,) + s.shape) if gather else s.shape, s.dtype),
                               memory_space=hbm) for s in srcs]
    token_in = jax.new_ref(jnp.zeros((8, 128), F32) if after is None else after, memory_space=hbm)
    token_out = jax.empty_ref(jax.ShapeDtypeStruct((8, 128), F32), memory_space=hbm)

    @pl.kernel(mesh=plsc.ScalarSubcoreMesh(axis_name="seq", num_cores=1), name=name,
               scratch_types=(pltpu.SemaphoreType.DMA((n, N_PEERS)), pltpu.SemaphoreType.DMA((n, N_PEERS)),
                              pltpu.SemaphoreType.DMA((n + 1,))),
               compiler_params=pltpu.CompilerParams(collective_id=collective_id))
    def launch(send_sems, recv_sems, local_sems):
        x, y, c = _place()
        my_id = 4 * x + 2 * y + c
        barrier = pltpu.get_barrier_semaphore()
        own = [pltpu.make_async_copy(src_refs[t] if gather else src_refs[t].at[my_id], land_refs[t].at[my_id],
                                     local_sems.at[t]) for t in range(n)]
        if gather:
            sibling = (x, y, 1 - c)
            chips = [(1 - x, y), (x, 1 - y), (1 - x, 1 - y)]
            for peer in [sibling] + [(*chip, c) for chip in chips]:
                pl.semaphore_signal(barrier, inc=1, device_id=peer, device_id_type=MESH)
            pl.semaphore_wait(barrier, 4)

            def copy(t, k, block, to, src=None):
                dst = land_refs[t].at[4 * block[0] + 2 * block[1] + block[2]]
                return pltpu.make_async_remote_copy(
                    src_ref=dst if src is None else src, dst_ref=dst,
                    send_sem=send_sems.at[t, k], recv_sem=recv_sems.at[t, k], device_id=to, device_id_type=MESH)

            for cp in own:
                cp.start()
            sends = []
            for t in range(n):
                sends.append(copy(t, 0, (x, y, c), sibling, src=src_refs[t]))
                sends += [copy(t, 1 + j, (x, y, c), (*chip, c), src=src_refs[t]) for j, chip in enumerate(chips)]
            for cp in sends:
                cp.start()
            for t in range(n):
                for j, chip in enumerate(chips):
                    copy(t, 1 + j, (*chip, c), (x, y, c)).wait_recv()
                    passed_on = copy(t, 4 + j, (*chip, c), sibling)
                    passed_on.start()
                    sends.append(passed_on)
            for t in range(n):
                copy(t, 0, sibling, (x, y, c)).wait_recv()
                for j, chip in enumerate(chips):
                    copy(t, 4 + j, (*chip, 1 - c), (x, y, c)).wait_recv()
            for cp in sends:
                cp.wait_send()
            for cp in own:
                cp.wait()
        else:
            for k in range(1, N_DEV):
                peer = (_flip(x, k & 4), _flip(y, k & 2), _flip(c, k & 1))
                pl.semaphore_signal(barrier, inc=1, device_id=peer, device_id_type=MESH)
            pl.semaphore_wait(barrier, N_PEERS)
            for cp in own:
                cp.start()
            copies = [_peer_copy(k, src_refs[t], land_refs[t], send_sems.at[t], recv_sems.at[t], gather)
                      for t in range(n) for k in range(1, N_DEV)]
            for cp in copies:
                cp.start()
            for cp in own:
                cp.wait()
            for cp in copies:
                cp.wait()
        passed = pltpu.make_async_copy(token_in, token_out, local_sems.at[n])
        passed.start()
        passed.wait()

    launch()
    return [ref[...] for ref in land_refs], token_out[...]


TM = 512
TMM = 1024
TKW = 2048
MIX_TILE = 256
FFN_FWD_TILE = 1024
FFN_BWD_TILE = 512


def _block_diag(w):
    wg = w.reshape(N_GROUPS, HEADS_PER_GROUP, LRU_HEAD_DIM, LRU_HEAD_DIM)
    eye = jnp.eye(HEADS_PER_GROUP, dtype=w.dtype)
    bd = wg[:, :, :, None, :] * eye[None, :, None, :, None]
    return bd.reshape(N_GROUPS, LRU_GROUP, LRU_GROUP).astype(BF16)


def _head_blocks(bd):
    b5 = bd.reshape(N_GROUPS, HEADS_PER_GROUP, LRU_HEAD_DIM, HEADS_PER_GROUP, LRU_HEAD_DIM)
    blocks = [b5[:, h, :, h, :] for h in range(HEADS_PER_GROUP)]
    return jnp.stack(blocks, axis=1).reshape(LRU_HEADS, LRU_HEAD_DIM, LRU_HEAD_DIM)


def _w(lw, key, after):
    value = lw[key]
    return value(after) if callable(value) else value


def _layer_fwd(x, lw, tag):
    sv_rows = x.shape[0]
    z, h1 = _norm_in_proj(x, lw["g1"], _w(lw, "w_in_t", x), tm=min(TMM, sv_rows), tn=896, name=f"in_proj_{tag}")
    y_mix, hs = _mixer_fwd(z, _w(lw, "cw", z), lw["cb"], lw["wa_bd"], lw["wx_bd"], lw["ba"], lw["bx"], lw["lam"],
                           _w(lw, "scw", z), tile=MIX_TILE, name=f"mixer_fwd_{tag}")
    x2 = _mm_nn(y_mix, _w(lw, "w_out", y_mix), tm=min(TMM, sv_rows), tn=D_MODEL, tk=D_MIX, out_dtype=F32, name=f"out_proj_{tag}",
                residual=x)
    x3, h2, act, p, u = _ffn_block_fwd(x2, lw["g2"], _w(lw, "w_up_b", x2), _w(lw, "fcw", x2), _w(lw, "w_down", x2),
                                       tile=min(FFN_FWD_TILE, sv_rows), name=f"ffn_fwd_{tag}")
    saved = dict(x=x, h1=h1, z=z, y_mix=y_mix, hs=hs, x2=x2, h2=h2, p=p, u=u, act=act)
    return x3, saved


def _layer_bwd(dx3, dx3b, lw, sv, tag, put):
    sv_rows = dx3.shape[0]
    w_in_t, w_out, w_up_b, w_down = (_w(lw, k, dx3) for k in ("w_in_t", "w_out", "w_up_b", "w_down"))
    cw, scw, fcw = (_w(lw, k, dx3) for k in ("cw", "scw", "fcw"))
    g_down = _mm_tn(sv["act"], dx3b, tm=1024, tn=D_MODEL, tk=min(TKW, sv_rows), out_dtype=BF16, name=f"down_bwd_w_{tag}")
    dx2, dx2b, dg2, dp, dfcw = _ffn_block_bwd(dx3, dx3b, sv["p"], sv["u"], sv["x2"], lw["g2"], w_up_b, fcw, w_down,
                                              tile=min(FFN_BWD_TILE, sv_rows), name=f"ffn_bwd_{tag}",
                                              token=put("w_down", g_down))
    g_up = _mm_up_bwd_w(sv["h2"], dp, tm=D_MODEL, tk=min(TKW, sv_rows), name=f"up_bwd_w_{tag}")
    dy = _mm_nt(dx2b, w_out, tm=min(TMM, sv_rows), tn=768, tk=D_MODEL, out_dtype=F32, name=f"out_bwd_x_{tag}")
    dz, dcw, dvec, dwa, dwx, dscw = _mixer_bwd_rows(
        sv["z"], sv["hs"], dy, cw, lw["cb"], lw["wa_bd"], lw["wx_bd"], lw["ba"], lw["bx"], lw["lam"],
        scw, tile=MIX_TILE, name=f"mixer_bwd_{tag}", token=put("w_up_b", g_up))
    g_out = _mm_tn(sv["y_mix"], dx2b, tm=768, tn=D_MODEL, tk=min(TKW, sv_rows), out_dtype=BF16, name=f"out_bwd_w_{tag}",
                   token=dz)
    g_in_t = _mm_tn(dz, sv["h1"], tm=896, tn=D_MODEL, tk=min(TKW, sv_rows), out_dtype=BF16, name=f"in_bwd_w_{tag}",
                    token=put("w_out", g_out))
    dx, dxb, dg1 = _in_bwd_norm(dz, w_in_t, sv["x"], lw["g1"], dx2, tm=min(TMM, sv_rows), tk=896, name=f"in_bwd_x_{tag}",
                                token=put("w_in_t", g_in_t))
    small = dict(norm1_g=dg1[0], lru_conv_w=dcw[0:4], lru_conv_b=dvec[0], lru_wa=_head_blocks(dwa),
                 lru_ba=dvec[1], lru_wx=_head_blocks(dwx), lru_bx=dvec[2], lru_lambda=dvec[3],
                 sc_conv_w=dscw[0:3], norm2_g=dg2[0], ffn_conv_w=dfcw[:, 0:3, :])
    return dx, dxb, small


SMALL_ORDER = ("norm1_g", "lru_conv_w", "lru_conv_b", "lru_wa", "lru_ba", "lru_wx", "lru_bx", "lru_lambda",
               "sc_conv_w", "norm2_g", "ffn_conv_w")


def _local_step(x, tgt, layers, final_g, put):
    saved = []
    h = x
    for l in range(DEPTH):
        h, sv = _layer_fwd(h, layers[l], f"l{l}")
        saved.append(sv)
    loss_blk, dx, dxb, dgf = _loss_head(h, final_g, tgt, tm=TM, name="loss_head")
    smalls = [None] * DEPTH
    for l in reversed(range(DEPTH)):
        dx, dxb, smalls[l] = _layer_bwd(dx, dxb, layers[l], saved[l], f"l{l}", functools.partial(put, l))
    return loss_blk[0, 0], dx, smalls, dgf[0]


def kernel(x, norm1_g, w_in, lru_conv_w, lru_conv_b, lru_wa, lru_ba, lru_wx, lru_bx, lru_lambda, sc_conv_w, w_out, norm2_g, w_up, ffn_conv_w, w_down, final_g, loss_target, m_norm1_g, m_w_in, m_lru_conv_w, m_lru_conv_b, m_lru_wa, m_lru_ba, m_lru_wx, m_lru_bx, m_lru_lambda, m_sc_conv_w, m_w_out, m_norm2_g, m_w_up, m_ffn_conv_w, m_w_down, m_final_g, v_norm1_g, v_w_in, v_lru_conv_w, v_lru_conv_b, v_lru_wa, v_lru_ba, v_lru_wx, v_lru_bx, v_lru_lambda, v_sc_conv_w, v_w_out, v_norm2_g, v_w_up, v_ffn_conv_w, v_w_down, v_final_g):
    names = ["norm1_g", "w_in", "lru_conv_w", "lru_conv_b", "lru_wa", "lru_ba", "lru_wx", "lru_bx", "lru_lambda",
             "sc_conv_w", "w_out", "norm2_g", "w_up", "ffn_conv_w", "w_down", "final_g"]
    w = dict(zip(names, [norm1_g, w_in, lru_conv_w, lru_conv_b, lru_wa, lru_ba, lru_wx, lru_bx, lru_lambda,
                         sc_conv_w, w_out, norm2_g, w_up, ffn_conv_w, w_down, final_g]))
    m = dict(zip(names, [m_norm1_g, m_w_in, m_lru_conv_w, m_lru_conv_b, m_lru_wa, m_lru_ba, m_lru_wx, m_lru_bx,
                         m_lru_lambda, m_sc_conv_w, m_w_out, m_norm2_g, m_w_up, m_ffn_conv_w, m_w_down, m_final_g]))
    v = dict(zip(names, [v_norm1_g, v_w_in, v_lru_conv_w, v_lru_conv_b, v_lru_wa, v_lru_ba, v_lru_wx, v_lru_bx,
                         v_lru_lambda, v_sc_conv_w, v_w_out, v_norm2_g, v_w_up, v_ffn_conv_w, v_w_down, v_final_g]))
    my_id = 4 * lax.axis_index("x") + 2 * lax.axis_index("y") + lax.axis_index("c")

    taps = jnp.zeros((DEPTH, 16, 768), F32)
    taps = taps.at[:, 0:4, 0:128].set(lru_conv_w).at[:, 4:7, 0:64].set(sc_conv_w).at[:, 8:11, :].set(ffn_conv_w)
    shards = {}
    for l in range(DEPTH):
        shards[f"w_in_t{l}"] = jnp.swapaxes(w_in[l], 0, 1).astype(BF16)
        if l == 0:
            shards["taps"] = taps.reshape(DEPTH * 16, 768)
        shards[f"w_out{l}"] = w_out[l].astype(BF16)
        shards[f"w_up_b{l}"] = w_up[l].astype(BF16)
        shards[f"w_down{l}"] = w_down[l].astype(BF16)
    ids = iter(range(18))
    got = {}
    chain = [None]
    for group in (("w_in_t0", "taps"), ("w_out0",), ("w_up_b0",), ("w_down0",),
                  ("w_in_t1",), ("w_out1",), ("w_up_b1",), ("w_down1",)):
        lands, chain[0] = _sequencer_copies([shards[k] for k in group], gather=True, name=f"gather_{group[0]}",
                                            collective_id=next(ids), after=None)
        got.update(zip(group, lands))

    def fetch(key, after):
        return got[key]

    def tap_rows(l, lo, hi, width, after):
        tl = fetch("taps", after).reshape(N_DEV, DEPTH, 16, 768)[:, l, lo:hi, 0:width]
        return jnp.transpose(tl, (1, 0, 2)).reshape(hi - lo, N_DEV * width)

    layers = []
    for l in range(DEPTH):
        layers.append(dict(
            g1=norm1_g[l], g2=norm2_g[l], cb=lru_conv_b[l], ba=lru_ba[l], bx=lru_bx[l], lam=lru_lambda[l],
            wa_bd=_block_diag(lru_wa[l]), wx_bd=_block_diag(lru_wx[l]),
            cw=functools.partial(tap_rows, l, 0, 4, 128), scw=functools.partial(tap_rows, l, 4, 7, 64),
            fcw=lambda after, l=l: tap_rows(l, 8, 11, 768, after).reshape(3, 2, D_FF).transpose(1, 0, 2),
            w_in_t=lambda after, l=l: fetch(f"w_in_t{l}", after).reshape(D_IN, D_MODEL),
            w_out=lambda after, l=l: fetch(f"w_out{l}", after).reshape(D_MIX, D_MODEL),
            w_up_b=lambda after, l=l: fetch(f"w_up_b{l}", after),
            w_down=lambda after, l=l: fetch(f"w_down{l}", after).reshape(D_FF, D_MODEL)))

    scatter_handles = {}

    def put(l, key, grad):
        blocks = grad if grad.ndim == 3 else grad.reshape(N_DEV, grad.shape[0] // N_DEV, grad.shape[1])
        (scatter_handles[(l, key)],), chain[0] = _sequencer_copies(
            [blocks], gather=False, name=f"scatter_{key}{l}", collective_id=next(ids), after=chain[0])
        return blocks

    loss_local, dx, smalls, dgf = _local_step(x[0], loss_target[0], layers, final_g, put)

    parts = []
    for l in range(DEPTH):
        for key in ("w_in_t", "w_out", "w_up_b", "w_down"):
            parts.append(scatter_handles[(l, key)])

    flat = [smalls[l][k].reshape(-1) for l in range(DEPTH) for k in SMALL_ORDER] + [dgf.reshape(-1)]
    flat.append(jnp.broadcast_to(loss_local, (128,)))
    sizes = [f.shape[0] for f in flat]
    total = sum(sizes)
    rows = -(-total // (N_DEV * 128 * 8)) * 8
    flat.append(jnp.zeros((N_DEV * rows * 128 - total,), F32))
    (small_parts,), chain[0] = _sequencer_copies([jnp.concatenate(flat).reshape(N_DEV, rows, 128)], gather=False,
                                                 name="scatter_small", collective_id=next(ids), after=chain[0])
    grads, deltas, new_m, new_v = {}, {}, {}, {}
    for slot, k in ((2, "w_up"), (3, "w_down"), (1, "w_out")):
        grads[k], deltas[k], new_m[k], new_v[k] = _sum_adamw(
            [parts[4 * l + slot] for l in range(DEPTH)], w[k], m[k], v[k], name=f"adamw_{k}")
    small_mine = _sum_parts(small_parts, name="sum_small", tokens=(deltas["w_up"], deltas["w_down"]))
    (small_all,), _ = _sequencer_copies([small_mine], gather=True, name="gather_small",
                                        collective_id=next(ids), after=chain[0])
    grads["w_in"] = jnp.stack([jnp.swapaxes(_sum_parts(parts[4 * l], name=f"sum_w_in_l{l}"), 0, 1)
                               for l in range(DEPTH)])
    small_sum = small_all.reshape(-1)
    small_g, off = [], 0
    for sz in sizes:
        small_g.append(small_sum[off:off + sz])
        off += sz
    gs = {}
    for l in range(DEPTH):
        for i, k in enumerate(SMALL_ORDER):
            gs.setdefault(k, []).append(small_g[l * len(SMALL_ORDER) + i])
    g_final = small_g[-2]
    loss = small_g[-1][0]

    for k in ("norm1_g", "lru_conv_b", "lru_ba", "lru_bx", "lru_lambda", "norm2_g"):
        grads[k] = jnp.stack(gs[k]).reshape(DEPTH, -1)
    for k in ("lru_wa", "lru_wx"):
        grads[k] = jnp.stack(gs[k]).reshape(DEPTH, LRU_HEADS, LRU_HEAD_DIM, LRU_HEAD_DIM)
    grads["final_g"] = g_final
    cw_full = jnp.stack(gs["lru_conv_w"]).reshape(DEPTH, 4, N_DEV, 128)
    grads["lru_conv_w"] = lax.dynamic_index_in_dim(cw_full, my_id, axis=2, keepdims=False)
    scw_full = jnp.stack(gs["sc_conv_w"]).reshape(DEPTH, 3, N_DEV, 64)
    grads["sc_conv_w"] = lax.dynamic_index_in_dim(scw_full, my_id, axis=2, keepdims=False)
    fcw_full = jnp.stack(gs["ffn_conv_w"]).reshape(DEPTH, 2, 3, D_FF).transpose(0, 2, 1, 3).reshape(DEPTH, 3, N_DEV, 768)
    grads["ffn_conv_w"] = lax.dynamic_index_in_dim(fcw_full, my_id, axis=2, keepdims=False)

    for k in names:
        if k in deltas:
            continue
        shape = w[k].shape
        cols = shape[-1]
        as2d = lambda a: a.reshape(-1, cols)
        d, nm, nv = _adamw(as2d(w[k]), as2d(grads[k]), as2d(m[k]), as2d(v[k]), name=f"adamw_{k}")
        deltas[k], new_m[k], new_v[k] = d.reshape(shape), nm.reshape(shape), nv.reshape(shape)

    return (loss, dx[None], *[grads[k] for k in names], *[deltas[k] for k in names],
            *[new_m[k] for k in names], *[new_v[k] for k in names])
```

```python
import functools
import math

import jax
import jax.numpy as jnp
from jax import lax
from jax.experimental import pallas as pl
from jax.experimental.pallas import tpu as pltpu
from jax.experimental.pallas import tpu_sc as plsc

F32 = jnp.float32
BF16 = jnp.bfloat16

N_DEV = 8
DEPTH = 2
D_MODEL = 1024
D_LRU = 1024
D_SC = 512
D_MIX = D_LRU + D_SC
D_IN = 2 * D_LRU + 3 * D_SC
D_FF = 3072
LRU_HEADS = 16
LRU_HEAD_DIM = 64
LRU_GROUP = 256
N_GROUPS = D_LRU // LRU_GROUP
HEADS_PER_GROUP = LRU_GROUP // LRU_HEAD_DIM
RG_C = 8.0
EPS = 1e-6
HALO = 8

ADAM_LR = 0.001
ADAM_B1 = 0.9
ADAM_B2 = 0.999
ADAM_EPS = 1e-08
ADAM_WD = 0.01
ADAM_STEP = 10

GELU_C = math.sqrt(2.0 / math.pi)
GELU_A = 0.044715

VMEM_LIMIT = 56 * 1024 * 1024
MESH = pl.DeviceIdType.MESH


def _params(*sem):
    return pltpu.CompilerParams(dimension_semantics=tuple(sem) if sem else None,
                                vmem_limit_bytes=VMEM_LIMIT)


def _gelu_parts(x):
    x2 = x * x
    t = jnp.tanh(GELU_C * (x + GELU_A * x * x2))
    half = 0.5 * (1.0 + t)
    g = x * half
    dg = half + 0.5 * x * (1.0 - t * t) * (GELU_C * (1.0 + 3.0 * GELU_A * x2))
    return g, dg


def _gelu(x):
    t = jnp.tanh(GELU_C * (x + GELU_A * x * x * x))
    return 0.5 * x * (1.0 + t)


def _sigmoid(x):
    return 0.5 * jnp.tanh(0.5 * x) + 0.5


def _softplus(x):
    e = jnp.exp(-jnp.abs(x))
    u = 1.0 + e
    log1p_e = jnp.where(u == 1.0, e, jnp.log(u) * (e / (u - 1.0)))
    return jnp.maximum(x, 0.0) + log1p_e


def _rms(x):
    ms = jnp.mean(x * x, axis=-1, keepdims=True)
    return lax.rsqrt(ms + EPS)


def _dot(a, b, dims):
    return lax.dot_general(a, b, (dims, ((), ())), preferred_element_type=F32)


NN = ((1,), (0,))
NT = ((1,), (1,))
TN = ((0,), (0,))


def _matmul(a, b, *, dims, grid, a_spec, b_spec, o_spec, out_shape, acc_shape, name,
            residual=None, r_spec=None, token=None):
    nk = grid[2]

    def body(*refs):
        a_ref, b_ref = refs[0], refs[1]
        r_ref = refs[2] if residual is not None else None
        o_ref = refs[2 + (residual is not None) + (token is not None)]
        prod = _dot(a_ref[...].astype(BF16), b_ref[...].astype(BF16), dims)

        def finish(total):
            if r_ref is not None:
                total = total + r_ref[...]
            o_ref[...] = total.astype(o_ref.dtype)

        if nk == 1:
            finish(prod)
            return
        acc_ref = refs[-1]
        k = pl.program_id(2)

        @pl.when(k == 0)
        def _():
            acc_ref[...] = prod

        @pl.when(jnp.logical_and(k > 0, k < nk - 1))
        def _():
            acc_ref[...] += prod

        @pl.when(k == nk - 1)
        def _():
            finish(acc_ref[...] + prod)

    in_specs = [a_spec, b_spec]
    args = [a, b]
    if residual is not None:
        in_specs.append(r_spec)
        args.append(residual)
    if token is not None:
        in_specs.append(pl.BlockSpec(memory_space=pl.ANY))
        args.append(token)
    return pl.pallas_call(
        body, name=name, grid=grid, in_specs=in_specs, out_specs=o_spec, out_shape=out_shape,
        scratch_shapes=[pltpu.VMEM(acc_shape, F32)] if nk > 1 else [],
        compiler_params=_params("parallel", "parallel", "arbitrary"),
    )(*args)


def _mm_nn(a, b, *, tm, tn, tk, out_dtype, name, residual=None, token=None):
    m, kd = a.shape
    n = b.shape[1]
    return _matmul(
        a, b, dims=NN, grid=(m // tm, n // tn, kd // tk),
        a_spec=pl.BlockSpec((tm, tk), lambda i, j, k: (i, k)),
        b_spec=pl.BlockSpec((tk, tn), lambda i, j, k: (k, j)),
        o_spec=pl.BlockSpec((tm, tn), lambda i, j, k: (i, j)),
        out_shape=jax.ShapeDtypeStruct((m, n), out_dtype), acc_shape=(tm, tn), name=name,
        residual=residual, r_spec=pl.BlockSpec((tm, tn), lambda i, j, k: (i, j)), token=token)


def _mm_nt(a, b, *, tm, tn, tk, out_dtype, name):
    m, kd = a.shape
    n = b.shape[0]
    return _matmul(
        a, b, dims=NT, grid=(m // tm, n // tn, kd // tk),
        a_spec=pl.BlockSpec((tm, tk), lambda i, j, k: (i, k)),
        b_spec=pl.BlockSpec((tn, tk), lambda i, j, k: (j, k)),
        o_spec=pl.BlockSpec((tm, tn), lambda i, j, k: (i, j)),
        out_shape=jax.ShapeDtypeStruct((m, n), out_dtype), acc_shape=(tm, tn), name=name)


def _mm_tn(a, b, *, tm, tn, tk, out_dtype, name, token=None):
    kd, m = a.shape
    n = b.shape[1]
    return _matmul(
        a, b, dims=TN, grid=(m // tm, n // tn, kd // tk),
        a_spec=pl.BlockSpec((tk, tm), lambda i, j, k: (k, i)),
        b_spec=pl.BlockSpec((tk, tn), lambda i, j, k: (k, j)),
        o_spec=pl.BlockSpec((tm, tn), lambda i, j, k: (i, j)),
        out_shape=jax.ShapeDtypeStruct((m, n), out_dtype), acc_shape=(tm, tn), name=name, token=token)


def _mm_up_bwd_w(h2, dp, *, tm, tk, name):
    s = h2.shape[0]
    nb = D_FF * 2 // N_DEV
    per_half = D_FF // nb
    return _matmul(
        h2, dp, dims=TN, grid=(D_MODEL // tm, N_DEV, s // tk),
        a_spec=pl.BlockSpec((tk, tm), lambda i, j, k: (k, i)),
        b_spec=pl.BlockSpec((None, tk, nb), lambda i, j, k: (j // per_half, k, j % per_half)),
        o_spec=pl.BlockSpec((None, tm, nb), lambda i, j, k: (j, i, 0)),
        out_shape=jax.ShapeDtypeStruct((N_DEV, D_MODEL, nb), BF16), acc_shape=(tm, nb), name=name)


def _behind(token):
    return jnp.zeros((8, 128), F32) if token is None else token


def _norm_in_proj(x, g, w_in_t, *, tm, tn, name):
    s, d = x.shape
    n = w_in_t.shape[0]

    def body(x_ref, g_ref, w_ref, z_ref, h_ref):
        @pl.when(pl.program_id(1) == 0)
        def _():
            xv = x_ref[...]
            h_ref[...] = (xv * _rms(xv) * g_ref[...]).astype(BF16)

        z_ref[...] = _dot(h_ref[...], w_ref[...], NT)

    return pl.pallas_call(
        body, name=name, grid=(s // tm, n // tn),
        in_specs=[pl.BlockSpec((tm, d), lambda i, j: (i, 0)), pl.BlockSpec((1, d), lambda i, j: (0, 0)),
                  pl.BlockSpec((tn, d), lambda i, j: (j, 0))],
        out_specs=[pl.BlockSpec((tm, tn), lambda i, j: (i, j)), pl.BlockSpec((tm, d), lambda i, j: (i, 0))],
        out_shape=[jax.ShapeDtypeStruct((s, n), F32), jax.ShapeDtypeStruct((s, d), BF16)],
        compiler_params=_params("parallel", "arbitrary"),
    )(x, g.reshape(1, d), w_in_t)


def _in_bwd_norm(dz, w_in_t, x, g, dres, *, tm, tk, name, token=None):
    s, kd = dz.shape
    d = w_in_t.shape[1]
    nk = kd // tk

    def body(dz_ref, w_ref, x_ref, g_ref, dres_ref, token_ref, dx_ref, dxb_ref, dg_ref, acc_ref):
        i = pl.program_id(0)
        k = pl.program_id(1)

        @pl.when(jnp.logical_and(i == 0, k == 0))
        def _():
            dg_ref[...] = jnp.zeros_like(dg_ref)

        acc_ref[...] = _dot(dz_ref[...], w_ref[...], NN) + jnp.where(k > 0, acc_ref[...], 0.0)

        @pl.when(k == nk - 1)
        def _():
            dh = acc_ref[...]
            xv = x_ref[...]
            rstd = _rms(xv)
            n = xv * rstd
            dn = dh * g_ref[...]
            dx = dres_ref[...] + rstd * (dn - n * jnp.mean(dn * n, axis=-1, keepdims=True))
            dx_ref[...] = dx
            dxb_ref[...] = dx.astype(BF16)
            dg_ref[0:1, :] += jnp.sum(dh * n, axis=0, keepdims=True)

    row = pl.BlockSpec((tm, d), lambda i, k: (i, 0))
    return pl.pallas_call(
        body, name=name, grid=(s // tm, nk),
        in_specs=[pl.BlockSpec((tm, tk), lambda i, k: (i, k)), pl.BlockSpec((tk, d), lambda i, k: (k, 0)),
                  row, pl.BlockSpec((1, d), lambda i, k: (0, 0)), row, pl.BlockSpec(memory_space=pl.ANY)],
        out_specs=[row, row, pl.BlockSpec((8, d), lambda i, k: (0, 0))],
        out_shape=[jax.ShapeDtypeStruct((s, d), F32), jax.ShapeDtypeStruct((s, d), BF16),
                   jax.ShapeDtypeStruct((8, d), F32)],
        scratch_shapes=[pltpu.VMEM((tm, d), F32)],
        compiler_params=_params("arbitrary", "arbitrary"),
    )(dz, w_in_t, x, g.reshape(1, d), dres, _behind(token))


def _loss_head(x, g, tgt, *, tm, name):
    s, d = x.shape

    def body(x_ref, g_ref, t_ref, loss_ref, dx_ref, dxb_ref, dg_ref):
        @pl.when(pl.program_id(0) == 0)
        def _():
            dg_ref[...] = jnp.zeros_like(dg_ref)
            loss_ref[...] = jnp.zeros_like(loss_ref)

        xv = x_ref[...]
        gv = g_ref[...]
        rstd = _rms(xv)
        n = xv * rstd
        e = n * gv - t_ref[...]
        part = 0.5 * jnp.sum(jnp.mean(e * e, axis=-1, keepdims=True), axis=0, keepdims=True)
        loss_ref[...] += jnp.broadcast_to(part, loss_ref.shape)
        dy = e * (1.0 / d)
        dn = dy * gv
        dx = rstd * (dn - n * jnp.mean(dn * n, axis=-1, keepdims=True))
        dx_ref[...] = dx
        dxb_ref[...] = dx.astype(BF16)
        dg_ref[0:1, :] += jnp.sum(dy * n, axis=0, keepdims=True)

    return pl.pallas_call(
        body, name=name, grid=(s // tm,),
        in_specs=[pl.BlockSpec((tm, d), lambda i: (i, 0)), pl.BlockSpec((1, d), lambda i: (0, 0)),
                  pl.BlockSpec((tm, d), lambda i: (i, 0))],
        out_specs=[pl.BlockSpec((8, 128), lambda i: (0, 0)), pl.BlockSpec((tm, d), lambda i: (i, 0)),
                   pl.BlockSpec((tm, d), lambda i: (i, 0)), pl.BlockSpec((8, d), lambda i: (0, 0))],
        out_shape=[jax.ShapeDtypeStruct((8, 128), F32), jax.ShapeDtypeStruct((s, d), F32),
                   jax.ShapeDtypeStruct((s, d), BF16), jax.ShapeDtypeStruct((8, d), F32)],
        compiler_params=_params("arbitrary"),
    )(x, g.reshape(1, d), tgt)


def _scan_rows(a_ref, b_ref, h_ref, carry, *, rows, reverse):
    width = a_ref.shape[1]
    n_chunks = rows // 8
    row = lax.broadcasted_iota(jnp.int32, (8, width), 0)

    def step(ci, carry):
        chunk = (n_chunks - 1 - ci) if reverse else ci
        off = pl.multiple_of(chunk * 8, 8)
        av = a_ref[pl.ds(off, 8), :]
        bv = b_ref[pl.ds(off, 8), :]
        for sh in (1, 2, 4):
            if reverse:
                a_sh = pltpu.roll(av, 8 - sh, 0)
                b_sh = pltpu.roll(bv, 8 - sh, 0)
                m = row < 8 - sh
            else:
                a_sh = pltpu.roll(av, sh, 0)
                b_sh = pltpu.roll(bv, sh, 0)
                m = row >= sh
            bv = jnp.where(m, av * b_sh + bv, bv)
            av = jnp.where(m, av * a_sh, av)
        h = av * carry + bv
        h_ref[pl.ds(off, 8), :] = h
        return h[0:1, :] if reverse else h[7:8, :]

    return lax.fori_loop(0, n_chunks, step, carry)


P_CB, P_BA, P_BX, P_DECAY, P_DSP, N_PAR = 4, 5, 6, 7, 8, 9


def _spread_mixer_params(par, par_sc, cw_ref, cb_ref, ba_ref, bx_ref, lam_ref, scw_ref):
    rows = par.shape[1:]
    for k in range(4):
        par[k] = jnp.broadcast_to(cw_ref[k:k + 1, :], rows)
    par[P_CB] = jnp.broadcast_to(cb_ref[...], rows)
    par[P_BA] = jnp.broadcast_to(ba_ref[...], rows)
    par[P_BX] = jnp.broadcast_to(bx_ref[...], rows)
    par[P_DECAY] = jnp.broadcast_to(-RG_C * _softplus(-lam_ref[...]), rows)
    par[P_DSP] = jnp.broadcast_to(-_sigmoid(-lam_ref[...]), rows)
    for k in range(3):
        par_sc[k] = jnp.broadcast_to(scw_ref[k:k + 1, :], par_sc.shape[1:])


def _gates_rows(pre_r, pre_i, par):
    r = _sigmoid(pre_r + par[P_BA])
    ig = _sigmoid(pre_i + par[P_BX])
    log_a = r * par[P_DECAY]
    a = jnp.exp(log_a)
    one_minus_a2 = -jnp.tanh(log_a) * (a * a + 1.0)
    return r, ig, a, jnp.sqrt(one_minus_a2), one_minus_a2


def _mixer_fwd(z, cw, cb, wa_bd, wx_bd, ba, bx, lam, scw, *, tile, name):
    s = z.shape[0]
    n_tiles = s // tile

    rows_of = lambda r0: slice(r0, r0 + FFN_ROWS)
    col_gate, col_sb, col_sc, col_sx = (slice(D_LRU, 2 * D_LRU), slice(2 * D_LRU, 2 * D_LRU + D_SC),
                                        slice(2 * D_LRU + D_SC, 2 * D_LRU + 2 * D_SC), slice(2 * D_LRU + 2 * D_SC, D_IN))

    def body(z_ref, cw_ref, cb_ref, wa_ref, wx_ref, ba_ref, bx_ref, lam_ref, scw_ref,
             y_ref, hs_ref, par, par_sc, lx_s, lxb_s, a_s, b_s, car_lx, car_q, h_car):
        i = pl.program_id(0)

        @pl.when(i == 0)
        def _():
            car_lx[...] = jnp.zeros_like(car_lx)
            car_q[...] = jnp.zeros_like(car_q)
            h_car[...] = jnp.zeros_like(h_car)
            _spread_mixer_params(par, par_sc, cw_ref, cb_ref, ba_ref, bx_ref, lam_ref, scw_ref)

        before_lx, before_q = car_lx[...], car_q[...]
        for r0 in range(0, tile, FFN_ROWS):
            cur = z_ref[rows_of(r0), 0:D_LRU]
            lx = par[P_CB] + par[3] * cur
            for k in range(3):
                lx = lx + par[k] * _rows_from(before_lx, cur, FFN_ROWS - 3 + k)
            lx_s[rows_of(r0), :] = lx
            lxb_s[rows_of(r0), :] = lx.astype(BF16)
            before_lx = cur
            q = z_ref[rows_of(r0), col_sc] * z_ref[rows_of(r0), col_sx]
            cq = par_sc[2] * q
            for k in range(2):
                cq = cq + par_sc[k] * _rows_from(before_q, q, FFN_ROWS - 2 + k)
            y_ref[rows_of(r0), D_LRU:D_MIX] = (z_ref[rows_of(r0), col_sb] * cq).astype(BF16)
            before_q = q
        car_lx[...] = before_lx
        car_q[...] = before_q

        for g in range(N_GROUPS):
            cols = slice(g * LRU_GROUP, (g + 1) * LRU_GROUP)
            a_s[:, cols] = _dot(lxb_s[:, cols], wa_ref[g], NN)
            b_s[:, cols] = _dot(lxb_s[:, cols], wx_ref[g], NN)

        for r0 in range(0, tile, FFN_ROWS):
            _, ig, a, mult, _ = _gates_rows(a_s[rows_of(r0), :], b_s[rows_of(r0), :], par)
            a_s[rows_of(r0), :] = a
            b_s[rows_of(r0), :] = mult * (ig * lx_s[rows_of(r0), :])
        h_car[0:1, :] = _scan_rows(a_s, b_s, hs_ref, h_car[0:1, :], rows=tile, reverse=False)

        for r0 in range(0, tile, FFN_ROWS):
            y_ref[rows_of(r0), 0:D_LRU] = (hs_ref[rows_of(r0), :] * _gelu(z_ref[rows_of(r0), col_gate])).astype(BF16)

    full = lambda shape: pl.BlockSpec(shape, lambda i: (0,) * len(shape))
    return pl.pallas_call(
        body, name=name, grid=(n_tiles,),
        in_specs=[pl.BlockSpec((tile, D_IN), lambda i: (i, 0)),
                  full((4, D_LRU)), full((1, D_LRU)),
                  full((N_GROUPS, LRU_GROUP, LRU_GROUP)), full((N_GROUPS, LRU_GROUP, LRU_GROUP)),
                  full((1, D_LRU)), full((1, D_LRU)), full((1, D_LRU)), full((3, D_SC))],
        out_specs=[pl.BlockSpec((tile, D_MIX), lambda i: (i, 0)), pl.BlockSpec((tile, D_LRU), lambda i: (i, 0))],
        out_shape=[jax.ShapeDtypeStruct((s, D_MIX), BF16), jax.ShapeDtypeStruct((s, D_LRU), F32)],
        scratch_shapes=[pltpu.VMEM((N_PAR, FFN_ROWS, D_LRU), F32), pltpu.VMEM((3, FFN_ROWS, D_SC), F32),
                        pltpu.VMEM((tile, D_LRU), F32), pltpu.VMEM((tile, D_LRU), BF16),
                        pltpu.VMEM((tile, D_LRU), F32), pltpu.VMEM((tile, D_LRU), F32),
                        pltpu.VMEM((FFN_ROWS, D_LRU), F32), pltpu.VMEM((FFN_ROWS, D_SC), F32),
                        pltpu.VMEM((8, D_LRU), F32)],
        compiler_params=_params("arbitrary"),
    )(z, cw, cb.reshape(1, -1), wa_bd, wx_bd, ba.reshape(1, -1), bx.reshape(1, -1), lam.reshape(1, -1), scw)


def _fold8(x):
    return sum(x[q:q + 8] for q in range(0, x.shape[0], 8))


def _mixer_bwd_rows(z, hs, dy, cw, cb, wa_bd, wx_bd, ba, bx, lam, scw, *, tile, name, token=None):
    s = z.shape[0]
    n_tiles = s // tile
    per8 = tile // 8
    rows_of = lambda r0: slice(r0, r0 + FFN_ROWS)
    col_gate, col_sb, col_sc, col_sx = (slice(D_LRU, 2 * D_LRU), slice(2 * D_LRU, 2 * D_LRU + D_SC),
                                        slice(2 * D_LRU + D_SC, 2 * D_LRU + 2 * D_SC), slice(2 * D_LRU + 2 * D_SC, D_IN))
    up = range(0, tile, FFN_ROWS)
    down = range(tile - FFN_ROWS, -1, -FFN_ROWS)
    A_CB, A_BA, A_BX, A_SP, A_CW = 0, 1, 2, 3, 4

    def body(z_ref, zp_ref, hs_ref, hsp_ref, dy_ref, cw_ref, cb_ref, wa_ref, wx_ref, ba_ref, bx_ref, lam_ref, scw_ref,
             token_ref, dz_ref, dcw_ref, dvec_ref, dwa_ref, dwx_ref, dscw_ref,
             par, par_sc, lx_s, lxb_s, cq_s, pr_s, pi_s, r_s, ig_s, a_s, mult_s, inv_s, ash_s, b_s, lam_s,
             dlx_s, dpr_b, dpi_b, back_s, car_a, car_dlx, car_dcq, l_car, acc, acc_sc):
        i = pl.program_id(0)

        @pl.when(i == 0)
        def _():
            for ref in (dwa_ref, dwx_ref, l_car, car_a, car_dlx, car_dcq, acc, acc_sc):
                ref[...] = jnp.zeros_like(ref)
            _spread_mixer_params(par, par_sc, cw_ref, cb_ref, ba_ref, bx_ref, lam_ref, scw_ref)

        keep = jnp.where(i == n_tiles - 1, 0.0, 1.0)
        zeros8 = lambda n: jnp.zeros((8, n), F32)

        before_lx = jnp.concatenate([zeros8(D_LRU), zp_ref[:, 0:D_LRU] * keep], axis=0)
        before_q = jnp.concatenate([zeros8(D_SC), zp_ref[:, col_sc] * zp_ref[:, col_sx] * keep], axis=0)
        for r0 in up:
            cur = z_ref[rows_of(r0), 0:D_LRU]
            lx = par[P_CB] + par[3] * cur
            for k in range(3):
                lx = lx + par[k] * _rows_from(before_lx, cur, FFN_ROWS - 3 + k)
            lx_s[rows_of(r0), :] = lx
            lxb_s[rows_of(r0), :] = lx.astype(BF16)
            before_lx = cur
            q = z_ref[rows_of(r0), col_sc] * z_ref[rows_of(r0), col_sx]
            cq = par_sc[2] * q
            for k in range(2):
                cq = cq + par_sc[k] * _rows_from(before_q, q, FFN_ROWS - 2 + k)
            cq_s[rows_of(r0), :] = cq
            before_q = q

        for g in range(N_GROUPS):
            cols = slice(g * LRU_GROUP, (g + 1) * LRU_GROUP)
            pr_s[:, cols] = _dot(lxb_s[:, cols], wa_ref[g], NN)
            pi_s[:, cols] = _dot(lxb_s[:, cols], wx_ref[g], NN)

        after_a = car_a[...]
        for r0 in down:
            r, ig, a, mult, one_minus_a2 = _gates_rows(pr_s[rows_of(r0), :], pi_s[rows_of(r0), :], par)
            r_s[rows_of(r0), :] = r
            ig_s[rows_of(r0), :] = ig
            a_s[rows_of(r0), :] = a
            mult_s[rows_of(r0), :] = mult
            inv_s[rows_of(r0), :] = lax.rsqrt(one_minus_a2)
            ash_s[rows_of(r0), :] = _rows_from(a, after_a, 1)
            after_a = a
            ge, dge = _gelu_parts(z_ref[rows_of(r0), col_gate])
            dy_lru = dy_ref[rows_of(r0), 0:D_LRU]
            dz_ref[rows_of(r0), col_gate] = (dy_lru * hs_ref[rows_of(r0), :] * dge).astype(BF16)
            b_s[rows_of(r0), :] = dy_lru * ge
        car_a[...] = after_a
        l_car[0:1, :] = _scan_rows(ash_s, b_s, lam_s, l_car[0:1, :], rows=tile, reverse=True)

        before_h = jnp.concatenate([zeros8(D_LRU), hsp_ref[...] * keep], axis=0)
        for r0 in up:
            lv = lam_s[rows_of(r0), :]
            h_here = hs_ref[rows_of(r0), :]
            lx, r, ig, a = lx_s[rows_of(r0), :], r_s[rows_of(r0), :], ig_s[rows_of(r0), :], a_s[rows_of(r0), :]
            mult = mult_s[rows_of(r0), :]
            da = lv * _rows_from(before_h, h_here, FFN_ROWS - 1)
            before_h = h_here
            d_mult = lv * ig * lx
            d_i = lv * mult * lx
            dlx_s[rows_of(r0), :] = lv * mult * ig
            dlog_a = da * a - d_mult * (a * a) * inv_s[rows_of(r0), :]
            dpre_r = dlog_a * par[P_DECAY] * r * (1.0 - r)
            dpre_i = d_i * ig * (1.0 - ig)
            acc[A_BA] += _fold8(dpre_r)
            acc[A_BX] += _fold8(dpre_i)
            acc[A_SP] += _fold8(dlog_a * r)
            dpr_b[rows_of(r0), :] = dpre_r.astype(BF16)
            dpi_b[rows_of(r0), :] = dpre_i.astype(BF16)

        for g in range(N_GROUPS):
            cols = slice(g * LRU_GROUP, (g + 1) * LRU_GROUP)
            dwa_ref[g] += _dot(lxb_s[:, cols], dpr_b[:, cols], TN)
            dwx_ref[g] += _dot(lxb_s[:, cols], dpi_b[:, cols], TN)
            back_s[:, cols] = _dot(dpr_b[:, cols], wa_ref[g], NT) + _dot(dpi_b[:, cols], wx_ref[g], NT)

        after_dlx, after_dcq = car_dlx[...], car_dcq[...]
        for r0 in down:
            dlx = dlx_s[rows_of(r0), :] + back_s[rows_of(r0), :]
            lxp = z_ref[rows_of(r0), 0:D_LRU]
            acc[A_CB] += _fold8(dlx)
            acc[A_CW + 3] += _fold8(dlx * lxp)
            dlxp = par[3] * dlx
            for sh in range(1, 4):
                below = _rows_from(dlx, after_dlx, sh)
                dlxp = dlxp + par[3 - sh] * below
                acc[A_CW + 3 - sh] += _fold8(below * lxp)
            dz_ref[rows_of(r0), 0:D_LRU] = dlxp.astype(BF16)
            after_dlx = dlx

            dy_sc = dy_ref[rows_of(r0), D_LRU:D_MIX]
            sb, sc, sx = z_ref[rows_of(r0), col_sb], z_ref[rows_of(r0), col_sc], z_ref[rows_of(r0), col_sx]
            dz_ref[rows_of(r0), col_sb] = (dy_sc * cq_s[rows_of(r0), :]).astype(BF16)
            dcq = dy_sc * sb
            q = sc * sx
            acc_sc[2] += _fold8(dcq * q)
            dq = par_sc[2] * dcq
            for sh in range(1, 3):
                below = _rows_from(dcq, after_dcq, sh)
                dq = dq + par_sc[2 - sh] * below
                acc_sc[2 - sh] += _fold8(below * q)
            dz_ref[rows_of(r0), col_sc] = (dq * sx).astype(BF16)
            dz_ref[rows_of(r0), col_sx] = (dq * sc).astype(BF16)
            after_dcq = dcq
        car_dlx[...] = after_dlx
        car_dcq[...] = after_dcq

        @pl.when(i == n_tiles - 1)
        def _():
            total = lambda x: jnp.sum(x, axis=0, keepdims=True)
            dcw_ref[...] = jnp.zeros_like(dcw_ref)
            dvec_ref[...] = jnp.zeros_like(dvec_ref)
            dscw_ref[...] = jnp.zeros_like(dscw_ref)
            for k in range(4):
                dcw_ref[k:k + 1, :] = total(acc[A_CW + k])
            for k in range(3):
                dvec_ref[k:k + 1, :] = total(acc[k])
                dscw_ref[k:k + 1, :] = total(acc_sc[k])
            dvec_ref[3:4, :] = total(acc[A_SP]) * (-RG_C) * par[P_DSP][0:1, :]

    rev = lambda i: n_tiles - 1 - i
    prev8 = lambda i: jnp.maximum(rev(i) * per8 - 1, 0)
    full = lambda shape: pl.BlockSpec(shape, lambda i: (0,) * len(shape))
    wide = lambda rows, dt=F32: pltpu.VMEM((rows, D_LRU), dt)
    return pl.pallas_call(
        body, name=name, grid=(n_tiles,),
        in_specs=[pl.BlockSpec((tile, D_IN), lambda i: (rev(i), 0)),
                  pl.BlockSpec((HALO, D_IN), lambda i: (prev8(i), 0)),
                  pl.BlockSpec((tile, D_LRU), lambda i: (rev(i), 0)),
                  pl.BlockSpec((HALO, D_LRU), lambda i: (prev8(i), 0)),
                  pl.BlockSpec((tile, D_MIX), lambda i: (rev(i), 0)),
                  full((4, D_LRU)), full((1, D_LRU)),
                  full((N_GROUPS, LRU_GROUP, LRU_GROUP)), full((N_GROUPS, LRU_GROUP, LRU_GROUP)),
                  full((1, D_LRU)), full((1, D_LRU)), full((1, D_LRU)), full((3, D_SC)),
                  pl.BlockSpec(memory_space=pl.ANY)],
        out_specs=[pl.BlockSpec((tile, D_IN), lambda i: (rev(i), 0)),
                   full((8, D_LRU)), full((8, D_LRU)),
                   full((N_GROUPS, LRU_GROUP, LRU_GROUP)), full((N_GROUPS, LRU_GROUP, LRU_GROUP)),
                   full((8, D_SC))],
        out_shape=[jax.ShapeDtypeStruct((s, D_IN), BF16),
                   jax.ShapeDtypeStruct((8, D_LRU), F32), jax.ShapeDtypeStruct((8, D_LRU), F32),
                   jax.ShapeDtypeStruct((N_GROUPS, LRU_GROUP, LRU_GROUP), F32),
                   jax.ShapeDtypeStruct((N_GROUPS, LRU_GROUP, LRU_GROUP), F32),
                   jax.ShapeDtypeStruct((8, D_SC), F32)],
        scratch_shapes=[pltpu.VMEM((N_PAR, FFN_ROWS, D_LRU), F32), pltpu.VMEM((3, FFN_ROWS, D_SC), F32),
                        wide(tile), wide(tile, BF16), pltpu.VMEM((tile, D_SC), F32),
                        wide(tile), wide(tile), wide(tile), wide(tile), wide(tile), wide(tile), wide(tile),
                        wide(tile), wide(tile), wide(tile),
                        wide(tile), wide(tile, BF16), wide(tile, BF16), wide(tile),
                        wide(FFN_ROWS), wide(FFN_ROWS), pltpu.VMEM((FFN_ROWS, D_SC), F32), wide(8),
                        pltpu.VMEM((8, 8, D_LRU), F32), pltpu.VMEM((3, 8, D_SC), F32)],
        compiler_params=_params("arbitrary"),
    )(z, z, hs, hs, dy, cw, cb.reshape(1, -1), wa_bd, wx_bd, ba.reshape(1, -1), bx.reshape(1, -1),
      lam.reshape(1, -1), scw, _behind(token))


FFN_ROWS = 16
FFN_GROUPS = 2


def _spread_taps(fw_ref, taps):
    for half in range(2):
        for k in range(3):
            taps[half, k] = jnp.broadcast_to(fw_ref[half, k:k + 1, :], taps.shape[2:])


def _rows_from(first, second, start):
    stack = jnp.concatenate([first, second], axis=0)
    return pltpu.roll(stack, 2 * FFN_ROWS - start, 0)[0:FFN_ROWS]


def _conv3_rows(taps, ext_ref, half, row):
    before = ext_ref[half, row - FFN_ROWS:row, :]
    here = ext_ref[half, row:row + FFN_ROWS, :]
    acc = taps[half, 2] * here
    for k in range(2):
        acc = acc + taps[half, k] * _rows_from(before, here, FFN_ROWS - 2 + k)
    return acc


HALO_B = 16


def _ffn_block_fwd(x2, g2, w_up_b, fcw, w_down, *, tile, name):
    s = x2.shape[0]
    nb = w_up_b.shape[2]
    blocks = D_FF // nb
    per16 = tile // HALO_B

    def body(x2_ref, x2p_ref, g_ref, wg_ref, wu_ref, fw_ref, wd_ref, x3_ref, h_ref, act_ref, p_ref, u_ref,
             ext_p, acc_ref, taps, lhs):
        i = pl.program_id(0)
        j = pl.program_id(1)
        keep = jnp.where(i == 0, 0.0, 1.0)
        _spread_taps(fw_ref, taps)
        @pl.when(j == 0)
        def _():
            for rows_ref, at in ((x2p_ref, 0), (x2_ref, HALO_B)):
                xv = rows_ref[...]
                lhs[at:at + xv.shape[0], :] = (xv * _rms(xv) * g_ref[...]).astype(BF16)
            h_ref[...] = lhs[HALO_B:HALO_B + tile, :]

        grp = tile // FFN_GROUPS
        for g in range(FFN_GROUPS):
            new = slice(g * grp + (HALO_B if g else 0), (g + 1) * grp + HALO_B)
            for half, w_ref in ((0, wg_ref), (1, wu_ref)):
                pe = _dot(lhs[new, :], w_ref[...], NN)
                if g == 0:
                    ext_p[half, 0:HALO_B, :] = pe[0:HALO_B] * keep
                    ext_p[half, HALO_B:grp + HALO_B, :] = pe[HALO_B:]
                    p_ref[half, 0:grp, :] = pe[HALO_B:].astype(BF16)
                else:
                    ext_p[half, new, :] = pe
                    p_ref[half, g * grp:(g + 1) * grp, :] = pe.astype(BF16)
        for g in range(FFN_GROUPS):
            rows = slice(g * grp, (g + 1) * grp)
            acts = []
            for r0 in range(g * grp, (g + 1) * grp, FFN_ROWS):
                u = [_conv3_rows(taps, ext_p, half, HALO_B + r0) for half in range(2)]
                for half in range(2):
                    u_ref[half, r0:r0 + FFN_ROWS, :] = u[half].astype(BF16)
                acts.append((_gelu(u[0]) * u[1]).astype(BF16))
                act_ref[r0:r0 + FFN_ROWS, :] = acts[-1]
            contrib = _dot(jnp.concatenate(acts, axis=0), wd_ref[...], NN)
            acc_ref[rows, :] = contrib + jnp.where(j > 0, acc_ref[rows, :], 0.0)

        @pl.when(j == blocks - 1)
        def _():
            x3_ref[...] = x2_ref[...] + acc_ref[...]

    return pl.pallas_call(
        body, name=name, grid=(s // tile, blocks),
        in_specs=[pl.BlockSpec((tile, D_MODEL), lambda i, j: (i, 0)),
                  pl.BlockSpec((HALO_B, D_MODEL), lambda i, j: (jnp.maximum(i * per16 - 1, 0), 0)),
                  pl.BlockSpec((1, D_MODEL), lambda i, j: (0, 0)),
                  pl.BlockSpec((None, D_MODEL, nb), lambda i, j: (j, 0, 0)),
                  pl.BlockSpec((None, D_MODEL, nb), lambda i, j: (j + blocks, 0, 0)),
                  pl.BlockSpec((2, 3, nb), lambda i, j: (0, 0, j)),
                  pl.BlockSpec((nb, D_MODEL), lambda i, j: (j, 0))],
        out_specs=[pl.BlockSpec((tile, D_MODEL), lambda i, j: (i, 0)),
                   pl.BlockSpec((tile, D_MODEL), lambda i, j: (i, 0)),
                   pl.BlockSpec((tile, nb), lambda i, j: (i, j)),
                   pl.BlockSpec((2, tile, nb), lambda i, j: (0, i, j)),
                   pl.BlockSpec((2, tile, nb), lambda i, j: (0, i, j))],
        out_shape=[jax.ShapeDtypeStruct((s, D_MODEL), F32), jax.ShapeDtypeStruct((s, D_MODEL), BF16),
                   jax.ShapeDtypeStruct((s, D_FF), BF16),
                   jax.ShapeDtypeStruct((2, s, D_FF), BF16), jax.ShapeDtypeStruct((2, s, D_FF), BF16)],
        scratch_shapes=[pltpu.VMEM((2, tile + HALO_B, nb), F32), pltpu.VMEM((tile, D_MODEL), F32),
                        pltpu.VMEM((2, 3, FFN_ROWS, nb), F32), pltpu.VMEM((tile + HALO_B, D_MODEL), BF16)],
        compiler_params=_params("parallel", "arbitrary"),
    )(x2, x2, g2.reshape(1, -1), w_up_b, w_up_b, fcw, w_down)


def _ffn_block_bwd(dx3, dx3b, p, u, x2, g2, w_up_b, fcw, w_down, *, tile, name, token=None):
    s = x2.shape[0]
    nb = w_up_b.shape[2]
    blocks = D_FF // nb
    n_tiles = s // tile
    per16 = tile // HALO_B
    last16 = s // HALO_B - 1

    def body(dxb_ref, dxbn_ref, wd_ref, p_ref, u_ref, un_ref, fw_ref, wg_ref, wu_ref, x2_ref, g_ref, dx3_ref,
             token_ref, dx2_ref, dx2b_ref, dg_ref, dp_ref, dw_ref, da_s, acc_w, acc_dh, taps):
        i = pl.program_id(0)
        j = pl.program_id(1)

        @pl.when(jnp.logical_and(i == 0, j == 0))
        def _():
            acc_w[...] = jnp.zeros_like(acc_w)
            dg_ref[...] = jnp.zeros_like(dg_ref)

        keep_next = jnp.where(i == n_tiles - 1, 0.0, 1.0)
        _spread_taps(fw_ref, taps)
        lhs = jnp.concatenate([dxb_ref[...], dxbn_ref[...]], axis=0)
        grp = tile // FFN_GROUPS
        for g in reversed(range(FFN_GROUPS)):
            new = slice(g * grp, (g + 1) * grp + (HALO_B if g == FFN_GROUPS - 1 else 0))
            da_s[new, :] = _dot(lhs[new], wd_ref[...], NT)

        def du_rows(da, u_gate, u_up):
            ge, dge = _gelu_parts(u_gate)
            return da * u_up * dge, da * ge

        after = du_rows(da_s[tile:tile + HALO_B, :] * keep_next, un_ref[0].astype(F32), un_ref[1].astype(F32))
        for g in reversed(range(FFN_GROUPS)):
            rows = slice(g * grp, (g + 1) * grp)
            dps = ([], [])
            for r0 in range((g + 1) * grp - FFN_ROWS, g * grp - 1, -FFN_ROWS):
                du = du_rows(da_s[r0:r0 + FFN_ROWS, :], u_ref[0, r0:r0 + FFN_ROWS, :].astype(F32),
                             u_ref[1, r0:r0 + FFN_ROWS, :].astype(F32))
                for half in range(2):
                    below = [du[half], _rows_from(du[half], after[half], 1), _rows_from(du[half], after[half], 2)]
                    acc = taps[half, 2] * below[0]
                    for k in range(2):
                        acc = acc + taps[half, k] * below[2 - k]
                    dps[half].insert(0, acc.astype(BF16))
                    dp_ref[half, r0:r0 + FFN_ROWS, :] = dps[half][0]
                    p_rows = p_ref[half, r0:r0 + FFN_ROWS, :].astype(F32)
                    for k in range(3):
                        prod = below[2 - k] * p_rows
                        acc_w[j, half, k] += sum(prod[q:q + 8] for q in range(0, FFN_ROWS, 8))
                after = du
            contrib = (_dot(jnp.concatenate(dps[0], axis=0), wg_ref[...], NT)
                       + _dot(jnp.concatenate(dps[1], axis=0), wu_ref[...], NT))
            acc_dh[rows, :] = contrib + jnp.where(j > 0, acc_dh[rows, :], 0.0)

        @pl.when(j == blocks - 1)
        def _():
            dh = acc_dh[...]
            xv = x2_ref[...]
            rstd = _rms(xv)
            n = xv * rstd
            dn = dh * g_ref[...]
            dx = dx3_ref[...] + rstd * (dn - n * jnp.mean(dn * n, axis=-1, keepdims=True))
            dx2_ref[...] = dx
            dx2b_ref[...] = dx.astype(BF16)
            dg_ref[0:1, :] += jnp.sum(dh * n, axis=0, keepdims=True)

        @pl.when(jnp.logical_and(i == n_tiles - 1, j == blocks - 1))
        def _():
            dw_ref[...] = jnp.zeros_like(dw_ref)
            for jj in range(blocks):
                for half in range(2):
                    for k in range(3):
                        dw_ref[half, k:k + 1, jj * nb:(jj + 1) * nb] = jnp.sum(acc_w[jj, half, k], axis=0, keepdims=True)

    next16 = lambda i: jnp.minimum((i + 1) * per16, last16)
    return pl.pallas_call(
        body, name=name, grid=(n_tiles, blocks),
        in_specs=[pl.BlockSpec((tile, D_MODEL), lambda i, j: (i, 0)),
                  pl.BlockSpec((HALO_B, D_MODEL), lambda i, j: (next16(i), 0)),
                  pl.BlockSpec((nb, D_MODEL), lambda i, j: (j, 0)),
                  pl.BlockSpec((2, tile, nb), lambda i, j: (0, i, j)),
                  pl.BlockSpec((2, tile, nb), lambda i, j: (0, i, j)),
                  pl.BlockSpec((2, HALO_B, nb), lambda i, j: (0, next16(i), j)),
                  pl.BlockSpec((2, 3, nb), lambda i, j: (0, 0, j)),
                  pl.BlockSpec((None, D_MODEL, nb), lambda i, j: (j, 0, 0)),
                  pl.BlockSpec((None, D_MODEL, nb), lambda i, j: (j + blocks, 0, 0)),
                  pl.BlockSpec((tile, D_MODEL), lambda i, j: (i, 0)),
                  pl.BlockSpec((1, D_MODEL), lambda i, j: (0, 0)),
                  pl.BlockSpec((tile, D_MODEL), lambda i, j: (i, 0)),
                  pl.BlockSpec(memory_space=pl.ANY)],
        out_specs=[pl.BlockSpec((tile, D_MODEL), lambda i, j: (i, 0)),
                   pl.BlockSpec((tile, D_MODEL), lambda i, j: (i, 0)),
                   pl.BlockSpec((8, D_MODEL), lambda i, j: (0, 0)),
                   pl.BlockSpec((2, tile, nb), lambda i, j: (0, i, j)),
                   pl.BlockSpec((2, 8, D_FF), lambda i, j: (0, 0, 0))],
        out_shape=[jax.ShapeDtypeStruct((s, D_MODEL), F32), jax.ShapeDtypeStruct((s, D_MODEL), BF16),
                   jax.ShapeDtypeStruct((8, D_MODEL), F32), jax.ShapeDtypeStruct((2, s, D_FF), BF16),
                   jax.ShapeDtypeStruct((2, 8, D_FF), F32)],
        scratch_shapes=[pltpu.VMEM((tile + HALO_B, nb), F32), pltpu.VMEM((blocks, 2, 3, 8, nb), F32),
                        pltpu.VMEM((tile, D_MODEL), F32), pltpu.VMEM((2, 3, FFN_ROWS, nb), F32)],
        compiler_params=_params("arbitrary", "arbitrary"),
    )(dx3b, dx3b, w_down, p, u, u, fcw, w_up_b, w_up_b, x2, g2.reshape(1, -1), dx3, _behind(token))


def _adamw_math(w, g, m, v):
    m = ADAM_B1 * m + (1.0 - ADAM_B1) * g
    v = ADAM_B2 * v + (1.0 - ADAM_B2) * (g * g)
    m_hat = m / (1.0 - ADAM_B1 ** ADAM_STEP)
    v_hat = v / (1.0 - ADAM_B2 ** ADAM_STEP)
    delta = -ADAM_LR * (m_hat / (jnp.sqrt(v_hat) + ADAM_EPS) + ADAM_WD * w)
    return delta, m, v


def _adamw(w, g, m, v, *, name):
    rows, cols = w.shape
    tr = rows
    for cand in (512, 256, 128, 64, 32, 16, 8):
        if rows % cand == 0 and rows > cand:
            tr = cand
            break

    def body(w_ref, g_ref, m_ref, v_ref, d_ref, nm_ref, nv_ref):
        d, nm, nv = _adamw_math(w_ref[...], g_ref[...], m_ref[...], v_ref[...])
        d_ref[...] = d
        nm_ref[...] = nm
        nv_ref[...] = nv

    spec = pl.BlockSpec((tr, cols), lambda i: (i, 0))
    return pl.pallas_call(
        body, name=name, grid=(rows // tr,), in_specs=[spec] * 4, out_specs=[spec] * 3,
        out_shape=[jax.ShapeDtypeStruct((rows, cols), F32)] * 3,
        compiler_params=_params("parallel"),
    )(w, g, m, v)


def _sum_adamw(parts, w, m, v, *, name):
    depth, rows, cols = w.shape
    tr = rows
    for cand in (256, 128, 64):
        if rows % cand == 0 and rows > cand:
            tr = cand
            break

    def body(*refs):
        part_refs = refs[:depth]
        w_ref, m_ref, v_ref, g_ref, d_ref, nm_ref, nv_ref = refs[depth:]
        layer = pl.program_id(0)
        grad = None
        for k, p_ref in enumerate(part_refs):
            total = p_ref[0].astype(F32)
            for dev in range(1, N_DEV):
                total = total + p_ref[dev].astype(F32)
            grad = total if grad is None else jnp.where(layer == k, total, grad)
        d, nm, nv = _adamw_math(w_ref[...], grad, m_ref[...], v_ref[...])
        g_ref[...] = grad
        d_ref[...] = d
        nm_ref[...] = nm
        nv_ref[...] = nv

    part_spec = lambda k: pl.BlockSpec((N_DEV, tr, cols), lambda l, i: (0, jnp.where(l == k, i, 0), 0))
    spec = pl.BlockSpec((None, tr, cols), lambda l, i: (l, i, 0))
    return pl.pallas_call(
        body, name=name, grid=(depth, rows // tr),
        in_specs=[part_spec(k) for k in range(depth)] + [spec] * 3, out_specs=[spec] * 4,
        out_shape=[jax.ShapeDtypeStruct((depth, rows, cols), F32)] * 4,
        compiler_params=_params("parallel", "parallel"),
    )(*parts, w, m, v)


def _sum_parts(parts, *, name, tokens=()):
    _, rows, cols = parts.shape
    tr = rows
    for cand in (256, 128, 64, 32, 16):
        if rows % cand == 0 and rows > cand:
            tr = cand
            break

    def body(p_ref, *rest):
        acc = p_ref[0].astype(F32)
        for d in range(1, N_DEV):
            acc = acc + p_ref[d].astype(F32)
        rest[-1][...] = acc

    return pl.pallas_call(
        body, name=name, grid=(rows // tr,),
        in_specs=[pl.BlockSpec((N_DEV, tr, cols), lambda i: (0, i, 0))]
        + [pl.BlockSpec(memory_space=pl.ANY)] * len(tokens),
        out_specs=pl.BlockSpec((tr, cols), lambda i: (i, 0)),
        out_shape=jax.ShapeDtypeStruct((rows, cols), F32),
        compiler_params=_params("parallel"),
    )(parts, *tokens)


def _place():
    return lax.axis_index("x"), lax.axis_index("y"), lax.axis_index("c")


def _flip(v, bit):
    return 1 - v if bit else v


N_PEERS = N_DEV - 1


def _peer_copy(k, src_ref, land_ref, send_sem, recv_sem, gather):
    x, y, c = _place()
    my_id = 4 * x + 2 * y + c
    px, py, pc = _flip(x, k & 4), _flip(y, k & 2), _flip(c, k & 1)
    peer_id = 4 * px + 2 * py + pc
    return pltpu.make_async_remote_copy(
        src_ref=src_ref if gather else src_ref.at[peer_id], dst_ref=land_ref.at[my_id],
        send_sem=send_sem.at[k - 1], recv_sem=recv_sem.at[k - 1],
        device_id=(px, py, pc), device_id_type=MESH)


def _sequencer_copies(srcs, *, gather, name, collective_id, after):
    n = len(srcs)
    hbm = pltpu.MemorySpace.HBM
    src_refs = [jax.new_ref(s, memory_space=hbm) for s in srcs]
    land_refs = [jax.empty_ref(jax.ShapeDtypeStruct(((N_DEV,) + s.shape) if gather else s.shape, s.dtype),
                               memory_space=hbm) for s in srcs]
    token_in = jax.new_ref(jnp.zeros((8, 128), F32) if after is None else after, memory_space=hbm)
    token_out = jax.empty_ref(jax.ShapeDtypeStruct((8, 128), F32), memory_space=hbm)

    @pl.kernel(mesh=plsc.ScalarSubcoreMesh(axis_name="seq", num_cores=1), name=name,
               scratch_types=(pltpu.SemaphoreType.DMA((n, N_PEERS)), pltpu.SemaphoreType.DMA((n, N_PEERS)),
                              pltpu.SemaphoreType.DMA((n + 1,))),
               compiler_params=pltpu.CompilerParams(collective_id=collective_id))
    def launch(send_sems, recv_sems, local_sems):
        x, y, c = _place()
        my_id = 4 * x + 2 * y + c
        barrier = pltpu.get_barrier_semaphore()
        own = [pltpu.make_async_copy(src_refs[t] if gather else src_refs[t].at[my_id], land_refs[t].at[my_id],
                                     local_sems.at[t]) for t in range(n)]
        if gather:
            sibling = (x, y, 1 - c)
            chips = [(1 - x, y), (x, 1 - y), (1 - x, 1 - y)]
            for peer in [sibling] + [(*chip, c) for chip in chips]:
                pl.semaphore_signal(barrier, inc=1, device_id=peer, device_id_type=MESH)
            pl.semaphore_wait(barrier, 4)

            def copy(t, k, block, to, src=None):
                dst = land_refs[t].at[4 * block[0] + 2 * block[1] + block[2]]
                return pltpu.make_async_remote_copy(
                    src_ref=dst if src is None else src, dst_ref=dst,
                    send_sem=send_sems.at[t, k], recv_sem=recv_sems.at[t, k], device_id=to, device_id_type=MESH)

            for cp in own:
                cp.start()
            sends = []
            for t in range(n):
                sends.append(copy(t, 0, (x, y, c), sibling, src=src_refs[t]))
                sends += [copy(t, 1 + j, (x, y, c), (*chip, c), src=src_refs[t]) for j, chip in enumerate(chips)]
            for cp in sends:
                cp.start()
            for t in range(n):
                for j, chip in enumerate(chips):
                    copy(t, 1 + j, (*chip, c), (x, y, c)).wait_recv()
                    passed_on = copy(t, 4 + j, (*chip, c), sibling)
                    passed_on.start()
                    sends.append(passed_on)
            for t in range(n):
                copy(t, 0, sibling, (x, y, c)).wait_recv()
                for j, chip in enumerate(chips):
                    copy(t, 4 + j, (*chip, 1 - c), (x, y, c)).wait_recv()
            for cp in sends:
                cp.wait_send()
            for cp in own:
                cp.wait()
        else:
            for k in range(1, N_DEV):
                peer = (_flip(x, k & 4), _flip(y, k & 2), _flip(c, k & 1))
                pl.semaphore_signal(barrier, inc=1, device_id=peer, device_id_type=MESH)
            pl.semaphore_wait(barrier, N_PEERS)
            for cp in own:
                cp.start()
            copies = [_peer_copy(k, src_refs[t], land_refs[t], send_sems.at[t], recv_sems.at[t], gather)
                      for t in range(n) for k in range(1, N_DEV)]
            for cp in copies:
                cp.start()
            for cp in own:
                cp.wait()
            for cp in copies:
                cp.wait()
        passed = pltpu.make_async_copy(token_in, token_out, local_sems.at[n])
        passed.start()
        passed.wait()

    launch()
    return [ref[...] for ref in land_refs], token_out[...]


TM = 512
TMM = 1024
TKW = 2048
MIX_TILE = 256
FFN_FWD_TILE = 1024
FFN_BWD_TILE = 512


def _block_diag(w):
    wg = w.reshape(N_GROUPS, HEADS_PER_GROUP, LRU_HEAD_DIM, LRU_HEAD_DIM)
    eye = jnp.eye(HEADS_PER_GROUP, dtype=w.dtype)
    bd = wg[:, :, :, None, :] * eye[None, :, None, :, None]
    return bd.reshape(N_GROUPS, LRU_GROUP, LRU_GROUP).astype(BF16)


def _head_blocks(bd):
    b5 = bd.reshape(N_GROUPS, HEADS_PER_GROUP, LRU_HEAD_DIM, HEADS_PER_GROUP, LRU_HEAD_DIM)
    blocks = [b5[:, h, :, h, :] for h in range(HEADS_PER_GROUP)]
    return jnp.stack(blocks, axis=1).reshape(LRU_HEADS, LRU_HEAD_DIM, LRU_HEAD_DIM)


def _w(lw, key, after):
    value = lw[key]
    return value(after) if callable(value) else value


def _layer_fwd(x, lw, tag):
    sv_rows = x.shape[0]
    z, h1 = _norm_in_proj(x, lw["g1"], _w(lw, "w_in_t", x), tm=min(2 * TMM, sv_rows), tn=896, name=f"in_proj_{tag}")
    y_mix, hs = _mixer_fwd(z, _w(lw, "cw", z), lw["cb"], lw["wa_bd"], lw["wx_bd"], lw["ba"], lw["bx"], lw["lam"],
                           _w(lw, "scw", z), tile=MIX_TILE, name=f"mixer_fwd_{tag}")
    x2 = _mm_nn(y_mix, _w(lw, "w_out", y_mix), tm=min(TMM, sv_rows), tn=D_MODEL, tk=D_MIX, out_dtype=F32, name=f"out_proj_{tag}",
                residual=x)
    x3, h2, act, p, u = _ffn_block_fwd(x2, lw["g2"], _w(lw, "w_up_b", x2), _w(lw, "fcw", x2), _w(lw, "w_down", x2),
                                       tile=min(FFN_FWD_TILE, sv_rows), name=f"ffn_fwd_{tag}")
    saved = dict(x=x, h1=h1, z=z, y_mix=y_mix, hs=hs, x2=x2, h2=h2, p=p, u=u, act=act)
    return x3, saved


def _layer_bwd(dx3, dx3b, lw, sv, tag, put):
    sv_rows = dx3.shape[0]
    w_in_t, w_out, w_up_b, w_down = (_w(lw, k, dx3) for k in ("w_in_t", "w_out", "w_up_b", "w_down"))
    cw, scw, fcw = (_w(lw, k, dx3) for k in ("cw", "scw", "fcw"))
    g_down = _mm_tn(sv["act"], dx3b, tm=1024, tn=D_MODEL, tk=min(TKW, sv_rows), out_dtype=BF16, name=f"down_bwd_w_{tag}")
    dx2, dx2b, dg2, dp, dfcw = _ffn_block_bwd(dx3, dx3b, sv["p"], sv["u"], sv["x2"], lw["g2"], w_up_b, fcw, w_down,
                                              tile=min(FFN_BWD_TILE, sv_rows), name=f"ffn_bwd_{tag}",
                                              token=put("w_down", g_down))
    g_up = _mm_up_bwd_w(sv["h2"], dp, tm=D_MODEL, tk=min(TKW, sv_rows), name=f"up_bwd_w_{tag}")
    dy = _mm_nt(dx2b, w_out, tm=min(TMM, sv_rows), tn=768, tk=D_MODEL, out_dtype=F32, name=f"out_bwd_x_{tag}")
    dz, dcw, dvec, dwa, dwx, dscw = _mixer_bwd_rows(
        sv["z"], sv["hs"], dy, cw, lw["cb"], lw["wa_bd"], lw["wx_bd"], lw["ba"], lw["bx"], lw["lam"],
        scw, tile=MIX_TILE, name=f"mixer_bwd_{tag}", token=put("w_up_b", g_up))
    g_out = _mm_tn(sv["y_mix"], dx2b, tm=768, tn=D_MODEL, tk=min(TKW, sv_rows), out_dtype=BF16, name=f"out_bwd_w_{tag}",
                   token=dz)
    g_in_t = _mm_tn(dz, sv["h1"], tm=896, tn=D_MODEL, tk=min(TKW, sv_rows), out_dtype=BF16, name=f"in_bwd_w_{tag}",
                    token=put("w_out", g_out))
    dx, dxb, dg1 = _in_bwd_norm(dz, w_in_t, sv["x"], lw["g1"], dx2, tm=min(TMM, sv_rows), tk=896, name=f"in_bwd_x_{tag}",
                                token=put("w_in_t", g_in_t))
    small = dict(norm1_g=dg1[0], lru_conv_w=dcw[0:4], lru_conv_b=dvec[0], lru_wa=_head_blocks(dwa),
                 lru_ba=dvec[1], lru_wx=_head_blocks(dwx), lru_bx=dvec[2], lru_lambda=dvec[3],
                 sc_conv_w=dscw[0:3], norm2_g=dg2[0], ffn_conv_w=dfcw[:, 0:3, :])
    return dx, dxb, small


SMALL_ORDER = ("norm1_g", "lru_conv_w", "lru_conv_b", "lru_wa", "lru_ba", "lru_wx", "lru_bx", "lru_lambda",
               "sc_conv_w", "norm2_g", "ffn_conv_w")


def _local_step(x, tgt, layers, final_g, put):
    saved = []
    h = x
    for l in range(DEPTH):
        h, sv = _layer_fwd(h, layers[l], f"l{l}")
        saved.append(sv)
    loss_blk, dx, dxb, dgf = _loss_head(h, final_g, tgt, tm=TM, name="loss_head")
    smalls = [None] * DEPTH
    for l in reversed(range(DEPTH)):
        dx, dxb, smalls[l] = _layer_bwd(dx, dxb, layers[l], saved[l], f"l{l}", functools.partial(put, l))
    return loss_blk[0, 0], dx, smalls, dgf[0]


def kernel(x, norm1_g, w_in, lru_conv_w, lru_conv_b, lru_wa, lru_ba, lru_wx, lru_bx, lru_lambda, sc_conv_w, w_out, norm2_g, w_up, ffn_conv_w, w_down, final_g, loss_target, m_norm1_g, m_w_in, m_lru_conv_w, m_lru_conv_b, m_lru_wa, m_lru_ba, m_lru_wx, m_lru_bx, m_lru_lambda, m_sc_conv_w, m_w_out, m_norm2_g, m_w_up, m_ffn_conv_w, m_w_down, m_final_g, v_norm1_g, v_w_in, v_lru_conv_w, v_lru_conv_b, v_lru_wa, v_lru_ba, v_lru_wx, v_lru_bx, v_lru_lambda, v_sc_conv_w, v_w_out, v_norm2_g, v_w_up, v_ffn_conv_w, v_w_down, v_final_g):
    names = ["norm1_g", "w_in", "lru_conv_w", "lru_conv_b", "lru_wa", "lru_ba", "lru_wx", "lru_bx", "lru_lambda",
             "sc_conv_w", "w_out", "norm2_g", "w_up", "ffn_conv_w", "w_down", "final_g"]
    w = dict(zip(names, [norm1_g, w_in, lru_conv_w, lru_conv_b, lru_wa, lru_ba, lru_wx, lru_bx, lru_lambda,
                         sc_conv_w, w_out, norm2_g, w_up, ffn_conv_w, w_down, final_g]))
    m = dict(zip(names, [m_norm1_g, m_w_in, m_lru_conv_w, m_lru_conv_b, m_lru_wa, m_lru_ba, m_lru_wx, m_lru_bx,
                         m_lru_lambda, m_sc_conv_w, m_w_out, m_norm2_g, m_w_up, m_ffn_conv_w, m_w_down, m_final_g]))
    v = dict(zip(names, [v_norm1_g, v_w_in, v_lru_conv_w, v_lru_conv_b, v_lru_wa, v_lru_ba, v_lru_wx, v_lru_bx,
                         v_lru_lambda, v_sc_conv_w, v_w_out, v_norm2_g, v_w_up, v_ffn_conv_w, v_w_down, v_final_g]))
    my_id = 4 * lax.axis_index("x") + 2 * lax.axis_index("y") + lax.axis_index("c")

    taps = jnp.zeros((DEPTH, 16, 768), F32)
    taps = taps.at[:, 0:4, 0:128].set(lru_conv_w).at[:, 4:7, 0:64].set(sc_conv_w).at[:, 8:11, :].set(ffn_conv_w)
    shards = {}
    for l in range(DEPTH):
        shards[f"w_in_t{l}"] = jnp.swapaxes(w_in[l], 0, 1).astype(BF16)
        if l == 0:
            shards["taps"] = taps.reshape(DEPTH * 16, 768)
        shards[f"w_out{l}"] = w_out[l].astype(BF16)
        shards[f"w_up_b{l}"] = w_up[l].astype(BF16)
        shards[f"w_down{l}"] = w_down[l].astype(BF16)
    ids = iter(range(18))
    got = {}
    chain = [None]
    for group in (("w_in_t0", "taps"), ("w_out0",), ("w_up_b0",), ("w_down0",),
                  ("w_in_t1",), ("w_out1",), ("w_up_b1",), ("w_down1",)):
        lands, chain[0] = _sequencer_copies([shards[k] for k in group], gather=True, name=f"gather_{group[0]}",
                                            collective_id=next(ids), after=None)
        got.update(zip(group, lands))

    def fetch(key, after):
        return got[key]

    def tap_rows(l, lo, hi, width, after):
        tl = fetch("taps", after).reshape(N_DEV, DEPTH, 16, 768)[:, l, lo:hi, 0:width]
        return jnp.transpose(tl, (1, 0, 2)).reshape(hi - lo, N_DEV * width)

    layers = []
    for l in range(DEPTH):
        layers.append(dict(
            g1=norm1_g[l], g2=norm2_g[l], cb=lru_conv_b[l], ba=lru_ba[l], bx=lru_bx[l], lam=lru_lambda[l],
            wa_bd=_block_diag(lru_wa[l]), wx_bd=_block_diag(lru_wx[l]),
            cw=functools.partial(tap_rows, l, 0, 4, 128), scw=functools.partial(tap_rows, l, 4, 7, 64),
            fcw=lambda after, l=l: tap_rows(l, 8, 11, 768, after).reshape(3, 2, D_FF).transpose(1, 0, 2),
            w_in_t=lambda after, l=l: fetch(f"w_in_t{l}", after).reshape(D_IN, D_MODEL),
            w_out=lambda after, l=l: fetch(f"w_out{l}", after).reshape(D_MIX, D_MODEL),
            w_up_b=lambda after, l=l: fetch(f"w_up_b{l}", after),
            w_down=lambda after, l=l: fetch(f"w_down{l}", after).reshape(D_FF, D_MODEL)))

    scatter_handles = {}

    def put(l, key, grad):
        blocks = grad if grad.ndim == 3 else grad.reshape(N_DEV, grad.shape[0] // N_DEV, grad.shape[1])
        (scatter_handles[(l, key)],), chain[0] = _sequencer_copies(
            [blocks], gather=False, name=f"scatter_{key}{l}", collective_id=next(ids), after=chain[0])
        return blocks

    loss_local, dx, smalls, dgf = _local_step(x[0], loss_target[0], layers, final_g, put)

    parts = []
    for l in range(DEPTH):
        for key in ("w_in_t", "w_out", "w_up_b", "w_down"):
            parts.append(scatter_handles[(l, key)])

    flat = [smalls[l][k].reshape(-1) for l in range(DEPTH) for k in SMALL_ORDER] + [dgf.reshape(-1)]
    flat.append(jnp.broadcast_to(loss_local, (128,)))
    sizes = [f.shape[0] for f in flat]
    total = sum(sizes)
    rows = -(-total // (N_DEV * 128 * 8)) * 8
    flat.append(jnp.zeros((N_DEV * rows * 128 - total,), F32))
    (small_parts,), chain[0] = _sequencer_copies([jnp.concatenate(flat).reshape(N_DEV, rows, 128)], gather=False,
                                                 name="scatter_small", collective_id=next(ids), after=chain[0])
    grads, deltas, new_m, new_v = {}, {}, {}, {}
    for slot, k in ((2, "w_up"), (3, "w_down"), (1, "w_out")):
        grads[k], deltas[k], new_m[k], new_v[k] = _sum_adamw(
            [parts[4 * l + slot] for l in range(DEPTH)], w[k], m[k], v[k], name=f"adamw_{k}")
    small_mine = _sum_parts(small_parts, name="sum_small", tokens=(deltas["w_up"], deltas["w_down"]))
    (small_all,), _ = _sequencer_copies([small_mine], gather=True, name="gather_small",
                                        collective_id=next(ids), after=chain[0])
    grads["w_in"] = jnp.stack([jnp.swapaxes(_sum_parts(parts[4 * l], name=f"sum_w_in_l{l}"), 0, 1)
                               for l in range(DEPTH)])
    small_sum = small_all.reshape(-1)
    small_g, off = [], 0
    for sz in sizes:
        small_g.append(small_sum[off:off + sz])
        off += sz
    gs = {}
    for l in range(DEPTH):
        for i, k in enumerate(SMALL_ORDER):
            gs.setdefault(k, []).append(small_g[l * len(SMALL_ORDER) + i])
    g_final = small_g[-2]
    loss = small_g[-1][0]

    for k in ("norm1_g", "lru_conv_b", "lru_ba", "lru_bx", "lru_lambda", "norm2_g"):
        grads[k] = jnp.stack(gs[k]).reshape(DEPTH, -1)
    for k in ("lru_wa", "lru_wx"):
        grads[k] = jnp.stack(gs[k]).reshape(DEPTH, LRU_HEADS, LRU_HEAD_DIM, LRU_HEAD_DIM)
    grads["final_g"] = g_final
    cw_full = jnp.stack(gs["lru_conv_w"]).reshape(DEPTH, 4, N_DEV, 128)
    grads["lru_conv_w"] = lax.dynamic_index_in_dim(cw_full, my_id, axis=2, keepdims=False)
    scw_full = jnp.stack(gs["sc_conv_w"]).reshape(DEPTH, 3, N_DEV, 64)
    grads["sc_conv_w"] = lax.dynamic_index_in_dim(scw_full, my_id, axis=2, keepdims=False)
    fcw_full = jnp.stack(gs["ffn_conv_w"]).reshape(DEPTH, 2, 3, D_FF).transpose(0, 2, 1, 3).reshape(DEPTH, 3, N_DEV, 768)
    grads["ffn_conv_w"] = lax.dynamic_index_in_dim(fcw_full, my_id, axis=2, keepdims=False)

    for k in names:
        if k in deltas:
            continue
        shape = w[k].shape
        cols = shape[-1]
        as2d = lambda a: a.reshape(-1, cols)
        d, nm, nv = _adamw(as2d(w[k]), as2d(grads[k]), as2d(m[k]), as2d(v[k]), name=f"adamw_{k}")
        deltas[k], new_m[k], new_v[k] = d.reshape(shape), nm.reshape(shape), nv.reshape(shape)

    return (loss, dx[None], *[grads[k] for k in names], *[deltas[k] for k in names],
            *[new_m[k] for k in names], *[new_v[k] for k in names])
```

```python
import functools
import math

import jax
import jax.numpy as jnp
from jax import lax
from jax.experimental import pallas as pl
from jax.experimental.pallas import tpu as pltpu
from jax.experimental.pallas import tpu_sc as plsc

F32 = jnp.float32
BF16 = jnp.bfloat16

N_DEV = 8
DEPTH = 2
D_MODEL = 1024
D_LRU = 1024
D_SC = 512
D_MIX = D_LRU + D_SC
D_IN = 2 * D_LRU + 3 * D_SC
D_FF = 3072
LRU_HEADS = 16
LRU_HEAD_DIM = 64
LRU_GROUP = 256
N_GROUPS = D_LRU // LRU_GROUP
HEADS_PER_GROUP = LRU_GROUP // LRU_HEAD_DIM
RG_C = 8.0
EPS = 1e-6
HALO = 8

ADAM_LR = 0.001
ADAM_B1 = 0.9
ADAM_B2 = 0.999
ADAM_EPS = 1e-08
ADAM_WD = 0.01
ADAM_STEP = 10

GELU_C = math.sqrt(2.0 / math.pi)
GELU_A = 0.044715

VMEM_LIMIT = 62 * 1024 * 1024
MESH = pl.DeviceIdType.MESH


def _params(*sem):
    return pltpu.CompilerParams(dimension_semantics=tuple(sem) if sem else None,
                                vmem_limit_bytes=VMEM_LIMIT)


def _gelu_parts(x):
    x2 = x * x
    t = jnp.tanh(GELU_C * (x + GELU_A * x * x2))
    half = 0.5 * (1.0 + t)
    g = x * half
    dg = half + 0.5 * x * (1.0 - t * t) * (GELU_C * (1.0 + 3.0 * GELU_A * x2))
    return g, dg


def _gelu(x):
    t = jnp.tanh(GELU_C * (x + GELU_A * x * x * x))
    return 0.5 * x * (1.0 + t)


def _sigmoid(x):
    return 0.5 * jnp.tanh(0.5 * x) + 0.5


def _softplus(x):
    e = jnp.exp(-jnp.abs(x))
    u = 1.0 + e
    log1p_e = jnp.where(u == 1.0, e, jnp.log(u) * (e / (u - 1.0)))
    return jnp.maximum(x, 0.0) + log1p_e


def _rms(x):
    ms = jnp.mean(x * x, axis=-1, keepdims=True)
    return lax.rsqrt(ms + EPS)


def _dot(a, b, dims):
    return lax.dot_general(a, b, (dims, ((), ())), preferred_element_type=F32)


NN = ((1,), (0,))
NT = ((1,), (1,))
TN = ((0,), (0,))


def _matmul(a, b, *, dims, grid, a_spec, b_spec, o_spec, out_shape, acc_shape, name,
            residual=None, r_spec=None, token=None):
    nk = grid[2]

    def body(*refs):
        a_ref, b_ref = refs[0], refs[1]
        r_ref = refs[2] if residual is not None else None
        o_ref = refs[2 + (residual is not None) + (token is not None)]
        prod = _dot(a_ref[...].astype(BF16), b_ref[...].astype(BF16), dims)

        def finish(total):
            if r_ref is not None:
                total = total + r_ref[...]
            o_ref[...] = total.astype(o_ref.dtype)

        if nk == 1:
            finish(prod)
            return
        acc_ref = refs[-1]
        k = pl.program_id(2)

        @pl.when(k == 0)
        def _():
            acc_ref[...] = prod

        @pl.when(jnp.logical_and(k > 0, k < nk - 1))
        def _():
            acc_ref[...] += prod

        @pl.when(k == nk - 1)
        def _():
            finish(acc_ref[...] + prod)

    in_specs = [a_spec, b_spec]
    args = [a, b]
    if residual is not None:
        in_specs.append(r_spec)
        args.append(residual)
    if token is not None:
        in_specs.append(pl.BlockSpec(memory_space=pl.ANY))
        args.append(token)
    return pl.pallas_call(
        body, name=name, grid=grid, in_specs=in_specs, out_specs=o_spec, out_shape=out_shape,
        scratch_shapes=[pltpu.VMEM(acc_shape, F32)] if nk > 1 else [],
        compiler_params=_params("parallel", "parallel", "arbitrary"),
    )(*args)


def _mm_nn(a, b, *, tm, tn, tk, out_dtype, name, residual=None, token=None):
    m, kd = a.shape
    n = b.shape[1]
    return _matmul(
        a, b, dims=NN, grid=(m // tm, n // tn, kd // tk),
        a_spec=pl.BlockSpec((tm, tk), lambda i, j, k: (i, k)),
        b_spec=pl.BlockSpec((tk, tn), lambda i, j, k: (k, j)),
        o_spec=pl.BlockSpec((tm, tn), lambda i, j, k: (i, j)),
        out_shape=jax.ShapeDtypeStruct((m, n), out_dtype), acc_shape=(tm, tn), name=name,
        residual=residual, r_spec=pl.BlockSpec((tm, tn), lambda i, j, k: (i, j)), token=token)


def _mm_nt(a, b, *, tm, tn, tk, out_dtype, name):
    m, kd = a.shape
    n = b.shape[0]
    return _matmul(
        a, b, dims=NT, grid=(m // tm, n // tn, kd // tk),
        a_spec=pl.BlockSpec((tm, tk), lambda i, j, k: (i, k)),
        b_spec=pl.BlockSpec((tn, tk), lambda i, j, k: (j, k)),
        o_spec=pl.BlockSpec((tm, tn), lambda i, j, k: (i, j)),
        out_shape=jax.ShapeDtypeStruct((m, n), out_dtype), acc_shape=(tm, tn), name=name)


def _mm_tn(a, b, *, tm, tn, tk, out_dtype, name, token=None):
    kd, m = a.shape
    n = b.shape[1]
    return _matmul(
        a, b, dims=TN, grid=(m // tm, n // tn, kd // tk),
        a_spec=pl.BlockSpec((tk, tm), lambda i, j, k: (k, i)),
        b_spec=pl.BlockSpec((tk, tn), lambda i, j, k: (k, j)),
        o_spec=pl.BlockSpec((tm, tn), lambda i, j, k: (i, j)),
        out_shape=jax.ShapeDtypeStruct((m, n), out_dtype), acc_shape=(tm, tn), name=name, token=token)


def _mm_up_bwd_w(h2, dp, *, tm, tk, name):
    s = h2.shape[0]
    nb = D_FF * 2 // N_DEV
    per_half = D_FF // nb
    return _matmul(
        h2, dp, dims=TN, grid=(D_MODEL // tm, N_DEV, s // tk),
        a_spec=pl.BlockSpec((tk, tm), lambda i, j, k: (k, i)),
        b_spec=pl.BlockSpec((None, tk, nb), lambda i, j, k: (j // per_half, k, j % per_half)),
        o_spec=pl.BlockSpec((None, tm, nb), lambda i, j, k: (j, i, 0)),
        out_shape=jax.ShapeDtypeStruct((N_DEV, D_MODEL, nb), BF16), acc_shape=(tm, nb), name=name)


def _behind(token):
    return jnp.zeros((8, 128), F32) if token is None else token


def _norm_in_proj(x, g, w_in_t, *, tm, tn, name):
    s, d = x.shape
    n = w_in_t.shape[0]

    def body(x_ref, g_ref, w_ref, z_ref, h_ref):
        @pl.when(pl.program_id(1) == 0)
        def _():
            xv = x_ref[...]
            h_ref[...] = (xv * _rms(xv) * g_ref[...]).astype(BF16)

        z_ref[...] = _dot(h_ref[...], w_ref[...], NT)

    return pl.pallas_call(
        body, name=name, grid=(s // tm, n // tn),
        in_specs=[pl.BlockSpec((tm, d), lambda i, j: (i, 0)), pl.BlockSpec((1, d), lambda i, j: (0, 0)),
                  pl.BlockSpec((tn, d), lambda i, j: (j, 0))],
        out_specs=[pl.BlockSpec((tm, tn), lambda i, j: (i, j)), pl.BlockSpec((tm, d), lambda i, j: (i, 0))],
        out_shape=[jax.ShapeDtypeStruct((s, n), F32), jax.ShapeDtypeStruct((s, d), BF16)],
        compiler_params=_params("parallel", "arbitrary"),
    )(x, g.reshape(1, d), w_in_t)


def _in_bwd_norm(dz, w_in_t, x, g, dres, *, tm, tk, name, token=None):
    s, kd = dz.shape
    d = w_in_t.shape[1]
    nk = kd // tk

    def body(dz_ref, w_ref, x_ref, g_ref, dres_ref, token_ref, dx_ref, dxb_ref, dg_ref, acc_ref):
        i = pl.program_id(0)
        k = pl.program_id(1)

        @pl.when(jnp.logical_and(i == 0, k == 0))
        def _():
            dg_ref[...] = jnp.zeros_like(dg_ref)

        acc_ref[...] = _dot(dz_ref[...], w_ref[...], NN) + jnp.where(k > 0, acc_ref[...], 0.0)

        @pl.when(k == nk - 1)
        def _():
            dh = acc_ref[...]
            xv = x_ref[...]
            rstd = _rms(xv)
            n = xv * rstd
            dn = dh * g_ref[...]
            dx = dres_ref[...] + rstd * (dn - n * jnp.mean(dn * n, axis=-1, keepdims=True))
            dx_ref[...] = dx
            dxb_ref[...] = dx.astype(BF16)
            dg_ref[0:1, :] += jnp.sum(dh * n, axis=0, keepdims=True)

    row = pl.BlockSpec((tm, d), lambda i, k: (i, 0))
    return pl.pallas_call(
        body, name=name, grid=(s // tm, nk),
        in_specs=[pl.BlockSpec((tm, tk), lambda i, k: (i, k)), pl.BlockSpec((tk, d), lambda i, k: (k, 0)),
                  row, pl.BlockSpec((1, d), lambda i, k: (0, 0)), row, pl.BlockSpec(memory_space=pl.ANY)],
        out_specs=[row, row, pl.BlockSpec((8, d), lambda i, k: (0, 0))],
        out_shape=[jax.ShapeDtypeStruct((s, d), F32), jax.ShapeDtypeStruct((s, d), BF16),
                   jax.ShapeDtypeStruct((8, d), F32)],
        scratch_shapes=[pltpu.VMEM((tm, d), F32)],
        compiler_params=_params("arbitrary", "arbitrary"),
    )(dz, w_in_t, x, g.reshape(1, d), dres, _behind(token))


def _loss_head(x, g, tgt, *, tm, name):
    s, d = x.shape

    def body(x_ref, g_ref, t_ref, loss_ref, dx_ref, dxb_ref, dg_ref):
        @pl.when(pl.program_id(0) == 0)
        def _():
            dg_ref[...] = jnp.zeros_like(dg_ref)
            loss_ref[...] = jnp.zeros_like(loss_ref)

        xv = x_ref[...]
        gv = g_ref[...]
        rstd = _rms(xv)
        n = xv * rstd
        e = n * gv - t_ref[...]
        part = 0.5 * jnp.sum(jnp.mean(e * e, axis=-1, keepdims=True), axis=0, keepdims=True)
        loss_ref[...] += jnp.broadcast_to(part, loss_ref.shape)
        dy = e * (1.0 / d)
        dn = dy * gv
        dx = rstd * (dn - n * jnp.mean(dn * n, axis=-1, keepdims=True))
        dx_ref[...] = dx
        dxb_ref[...] = dx.astype(BF16)
        dg_ref[0:1, :] += jnp.sum(dy * n, axis=0, keepdims=True)

    return pl.pallas_call(
        body, name=name, grid=(s // tm,),
        in_specs=[pl.BlockSpec((tm, d), lambda i: (i, 0)), pl.BlockSpec((1, d), lambda i: (0, 0)),
                  pl.BlockSpec((tm, d), lambda i: (i, 0))],
        out_specs=[pl.BlockSpec((8, 128), lambda i: (0, 0)), pl.BlockSpec((tm, d), lambda i: (i, 0)),
                   pl.BlockSpec((tm, d), lambda i: (i, 0)), pl.BlockSpec((8, d), lambda i: (0, 0))],
        out_shape=[jax.ShapeDtypeStruct((8, 128), F32), jax.ShapeDtypeStruct((s, d), F32),
                   jax.ShapeDtypeStruct((s, d), BF16), jax.ShapeDtypeStruct((8, d), F32)],
        compiler_params=_params("arbitrary"),
    )(x, g.reshape(1, d), tgt)


def _scan_rows(a_ref, b_ref, h_ref, carry, *, rows, reverse):
    width = a_ref.shape[1]
    n_chunks = rows // 8
    row = lax.broadcasted_iota(jnp.int32, (8, width), 0)

    def step(ci, carry):
        chunk = (n_chunks - 1 - ci) if reverse else ci
        off = pl.multiple_of(chunk * 8, 8)
        av = a_ref[pl.ds(off, 8), :]
        bv = b_ref[pl.ds(off, 8), :]
        for sh in (1, 2, 4):
            if reverse:
                a_sh = pltpu.roll(av, 8 - sh, 0)
                b_sh = pltpu.roll(bv, 8 - sh, 0)
                m = row < 8 - sh
            else:
                a_sh = pltpu.roll(av, sh, 0)
                b_sh = pltpu.roll(bv, sh, 0)
                m = row >= sh
            bv = jnp.where(m, av * b_sh + bv, bv)
            av = jnp.where(m, av * a_sh, av)
        h = av * carry + bv
        h_ref[pl.ds(off, 8), :] = h
        return h[0:1, :] if reverse else h[7:8, :]

    return lax.fori_loop(0, n_chunks, step, carry)


P_CB, P_BA, P_BX, P_DECAY, P_DSP, N_PAR = 4, 5, 6, 7, 8, 9


def _spread_mixer_params(par, par_sc, cw_ref, cb_ref, ba_ref, bx_ref, lam_ref, scw_ref):
    rows = par.shape[1:]
    for k in range(4):
        par[k] = jnp.broadcast_to(cw_ref[k:k + 1, :], rows)
    par[P_CB] = jnp.broadcast_to(cb_ref[...], rows)
    par[P_BA] = jnp.broadcast_to(ba_ref[...], rows)
    par[P_BX] = jnp.broadcast_to(bx_ref[...], rows)
    par[P_DECAY] = jnp.broadcast_to(-RG_C * _softplus(-lam_ref[...]), rows)
    par[P_DSP] = jnp.broadcast_to(-_sigmoid(-lam_ref[...]), rows)
    for k in range(3):
        par_sc[k] = jnp.broadcast_to(scw_ref[k:k + 1, :], par_sc.shape[1:])


def _gates_rows(pre_r, pre_i, par):
    r = _sigmoid(pre_r + par[P_BA])
    ig = _sigmoid(pre_i + par[P_BX])
    log_a = r * par[P_DECAY]
    a = jnp.exp(log_a)
    one_minus_a2 = -jnp.tanh(log_a) * (a * a + 1.0)
    return r, ig, a, jnp.sqrt(one_minus_a2), one_minus_a2


def _mixer_fwd(z, cw, cb, wa_bd, wx_bd, ba, bx, lam, scw, *, tile, name):
    s = z.shape[0]
    n_tiles = s // tile

    rows_of = lambda r0: slice(r0, r0 + FFN_ROWS)
    col_gate, col_sb, col_sc, col_sx = (slice(D_LRU, 2 * D_LRU), slice(2 * D_LRU, 2 * D_LRU + D_SC),
                                        slice(2 * D_LRU + D_SC, 2 * D_LRU + 2 * D_SC), slice(2 * D_LRU + 2 * D_SC, D_IN))

    def body(z_ref, cw_ref, cb_ref, wa_ref, wx_ref, ba_ref, bx_ref, lam_ref, scw_ref,
             y_ref, hs_ref, par, par_sc, lx_s, lxb_s, a_s, b_s, car_lx, car_q, h_car):
        i = pl.program_id(0)

        @pl.when(i == 0)
        def _():
            car_lx[...] = jnp.zeros_like(car_lx)
            car_q[...] = jnp.zeros_like(car_q)
            h_car[...] = jnp.zeros_like(h_car)
            _spread_mixer_params(par, par_sc, cw_ref, cb_ref, ba_ref, bx_ref, lam_ref, scw_ref)

        before_lx, before_q = car_lx[...], car_q[...]
        for r0 in range(0, tile, FFN_ROWS):
            cur = z_ref[rows_of(r0), 0:D_LRU]
            lx = par[P_CB] + par[3] * cur
            for k in range(3):
                lx = lx + par[k] * _rows_from(before_lx, cur, FFN_ROWS - 3 + k)
            lx_s[rows_of(r0), :] = lx
            lxb_s[rows_of(r0), :] = lx.astype(BF16)
            before_lx = cur
            q = z_ref[rows_of(r0), col_sc] * z_ref[rows_of(r0), col_sx]
            cq = par_sc[2] * q
            for k in range(2):
                cq = cq + par_sc[k] * _rows_from(before_q, q, FFN_ROWS - 2 + k)
            y_ref[rows_of(r0), D_LRU:D_MIX] = (z_ref[rows_of(r0), col_sb] * cq).astype(BF16)
            before_q = q
        car_lx[...] = before_lx
        car_q[...] = before_q

        for g in range(N_GROUPS):
            cols = slice(g * LRU_GROUP, (g + 1) * LRU_GROUP)
            a_s[:, cols] = _dot(lxb_s[:, cols], wa_ref[g], NN)
            b_s[:, cols] = _dot(lxb_s[:, cols], wx_ref[g], NN)

        for r0 in range(0, tile, FFN_ROWS):
            _, ig, a, mult, _ = _gates_rows(a_s[rows_of(r0), :], b_s[rows_of(r0), :], par)
            a_s[rows_of(r0), :] = a
            b_s[rows_of(r0), :] = mult * (ig * lx_s[rows_of(r0), :])
        h_car[0:1, :] = _scan_rows(a_s, b_s, hs_ref, h_car[0:1, :], rows=tile, reverse=False)

        for r0 in range(0, tile, FFN_ROWS):
            y_ref[rows_of(r0), 0:D_LRU] = (hs_ref[rows_of(r0), :] * _gelu(z_ref[rows_of(r0), col_gate])).astype(BF16)

    full = lambda shape: pl.BlockSpec(shape, lambda i: (0,) * len(shape))
    return pl.pallas_call(
        body, name=name, grid=(n_tiles,),
        in_specs=[pl.BlockSpec((tile, D_IN), lambda i: (i, 0)),
                  full((4, D_LRU)), full((1, D_LRU)),
                  full((N_GROUPS, LRU_GROUP, LRU_GROUP)), full((N_GROUPS, LRU_GROUP, LRU_GROUP)),
                  full((1, D_LRU)), full((1, D_LRU)), full((1, D_LRU)), full((3, D_SC))],
        out_specs=[pl.BlockSpec((tile, D_MIX), lambda i: (i, 0)), pl.BlockSpec((tile, D_LRU), lambda i: (i, 0))],
        out_shape=[jax.ShapeDtypeStruct((s, D_MIX), BF16), jax.ShapeDtypeStruct((s, D_LRU), F32)],
        scratch_shapes=[pltpu.VMEM((N_PAR, FFN_ROWS, D_LRU), F32), pltpu.VMEM((3, FFN_ROWS, D_SC), F32),
                        pltpu.VMEM((tile, D_LRU), F32), pltpu.VMEM((tile, D_LRU), BF16),
                        pltpu.VMEM((tile, D_LRU), F32), pltpu.VMEM((tile, D_LRU), F32),
                        pltpu.VMEM((FFN_ROWS, D_LRU), F32), pltpu.VMEM((FFN_ROWS, D_SC), F32),
                        pltpu.VMEM((8, D_LRU), F32)],
        compiler_params=_params("arbitrary"),
    )(z, cw, cb.reshape(1, -1), wa_bd, wx_bd, ba.reshape(1, -1), bx.reshape(1, -1), lam.reshape(1, -1), scw)


def _fold8(x):
    return sum(x[q:q + 8] for q in range(0, x.shape[0], 8))


def _mixer_bwd_rows(z, hs, dy, cw, cb, wa_bd, wx_bd, ba, bx, lam, scw, *, tile, name, token=None):
    s = z.shape[0]
    n_tiles = s // tile
    per8 = tile // 8
    rows_of = lambda r0: slice(r0, r0 + FFN_ROWS)
    col_gate, col_sb, col_sc, col_sx = (slice(D_LRU, 2 * D_LRU), slice(2 * D_LRU, 2 * D_LRU + D_SC),
                                        slice(2 * D_LRU + D_SC, 2 * D_LRU + 2 * D_SC), slice(2 * D_LRU + 2 * D_SC, D_IN))
    up = range(0, tile, FFN_ROWS)
    down = range(tile - FFN_ROWS, -1, -FFN_ROWS)
    A_CB, A_BA, A_BX, A_SP, A_CW = 0, 1, 2, 3, 4

    def body(z_ref, zp_ref, hs_ref, hsp_ref, dy_ref, cw_ref, cb_ref, wa_ref, wx_ref, ba_ref, bx_ref, lam_ref, scw_ref,
             token_ref, dz_ref, dcw_ref, dvec_ref, dwa_ref, dwx_ref, dscw_ref,
             par, par_sc, lx_s, lxb_s, cq_s, pr_s, pi_s, r_s, ig_s, a_s, mult_s, inv_s, ash_s, b_s, lam_s,
             dlx_s, dpr_b, dpi_b, back_s, car_a, car_dlx, car_dcq, l_car, acc, acc_sc):
        i = pl.program_id(0)

        @pl.when(i == 0)
        def _():
            for ref in (dwa_ref, dwx_ref, l_car, car_a, car_dlx, car_dcq, acc, acc_sc):
                ref[...] = jnp.zeros_like(ref)
            _spread_mixer_params(par, par_sc, cw_ref, cb_ref, ba_ref, bx_ref, lam_ref, scw_ref)

        keep = jnp.where(i == n_tiles - 1, 0.0, 1.0)
        zeros8 = lambda n: jnp.zeros((8, n), F32)

        before_lx = jnp.concatenate([zeros8(D_LRU), zp_ref[:, 0:D_LRU] * keep], axis=0)
        before_q = jnp.concatenate([zeros8(D_SC), zp_ref[:, col_sc] * zp_ref[:, col_sx] * keep], axis=0)
        for r0 in up:
            cur = z_ref[rows_of(r0), 0:D_LRU]
            lx = par[P_CB] + par[3] * cur
            for k in range(3):
                lx = lx + par[k] * _rows_from(before_lx, cur, FFN_ROWS - 3 + k)
            lx_s[rows_of(r0), :] = lx
            lxb_s[rows_of(r0), :] = lx.astype(BF16)
            before_lx = cur
            q = z_ref[rows_of(r0), col_sc] * z_ref[rows_of(r0), col_sx]
            cq = par_sc[2] * q
            for k in range(2):
                cq = cq + par_sc[k] * _rows_from(before_q, q, FFN_ROWS - 2 + k)
            cq_s[rows_of(r0), :] = cq
            before_q = q

        for g in range(N_GROUPS):
            cols = slice(g * LRU_GROUP, (g + 1) * LRU_GROUP)
            pr_s[:, cols] = _dot(lxb_s[:, cols], wa_ref[g], NN)
            pi_s[:, cols] = _dot(lxb_s[:, cols], wx_ref[g], NN)

        after_a = car_a[...]
        for r0 in down:
            r, ig, a, mult, one_minus_a2 = _gates_rows(pr_s[rows_of(r0), :], pi_s[rows_of(r0), :], par)
            r_s[rows_of(r0), :] = r
            ig_s[rows_of(r0), :] = ig
            a_s[rows_of(r0), :] = a
            mult_s[rows_of(r0), :] = mult
            inv_s[rows_of(r0), :] = lax.rsqrt(one_minus_a2)
            ash_s[rows_of(r0), :] = _rows_from(a, after_a, 1)
            after_a = a
            ge, dge = _gelu_parts(z_ref[rows_of(r0), col_gate])
            dy_lru = dy_ref[rows_of(r0), 0:D_LRU]
            dz_ref[rows_of(r0), col_gate] = (dy_lru * hs_ref[rows_of(r0), :] * dge).astype(BF16)
            b_s[rows_of(r0), :] = dy_lru * ge
        car_a[...] = after_a
        l_car[0:1, :] = _scan_rows(ash_s, b_s, lam_s, l_car[0:1, :], rows=tile, reverse=True)

        before_h = jnp.concatenate([zeros8(D_LRU), hsp_ref[...] * keep], axis=0)
        for r0 in up:
            lv = lam_s[rows_of(r0), :]
            h_here = hs_ref[rows_of(r0), :]
            lx, r, ig, a = lx_s[rows_of(r0), :], r_s[rows_of(r0), :], ig_s[rows_of(r0), :], a_s[rows_of(r0), :]
            mult = mult_s[rows_of(r0), :]
            da = lv * _rows_from(before_h, h_here, FFN_ROWS - 1)
            before_h = h_here
            d_mult = lv * ig * lx
            d_i = lv * mult * lx
            dlx_s[rows_of(r0), :] = lv * mult * ig
            dlog_a = da * a - d_mult * (a * a) * inv_s[rows_of(r0), :]
            dpre_r = dlog_a * par[P_DECAY] * r * (1.0 - r)
            dpre_i = d_i * ig * (1.0 - ig)
            acc[A_BA] += _fold8(dpre_r)
            acc[A_BX] += _fold8(dpre_i)
            acc[A_SP] += _fold8(dlog_a * r)
            dpr_b[rows_of(r0), :] = dpre_r.astype(BF16)
            dpi_b[rows_of(r0), :] = dpre_i.astype(BF16)

        for g in range(N_GROUPS):
            cols = slice(g * LRU_GROUP, (g + 1) * LRU_GROUP)
            dwa_ref[g] += _dot(lxb_s[:, cols], dpr_b[:, cols], TN)
            dwx_ref[g] += _dot(lxb_s[:, cols], dpi_b[:, cols], TN)
            back_s[:, cols] = _dot(dpr_b[:, cols], wa_ref[g], NT) + _dot(dpi_b[:, cols], wx_ref[g], NT)

        after_dlx, after_dcq = car_dlx[...], car_dcq[...]
        for r0 in down:
            dlx = dlx_s[rows_of(r0), :] + back_s[rows_of(r0), :]
            lxp = z_ref[rows_of(r0), 0:D_LRU]
            acc[A_CB] += _fold8(dlx)
            acc[A_CW + 3] += _fold8(dlx * lxp)
            dlxp = par[3] * dlx
            for sh in range(1, 4):
                below = _rows_from(dlx, after_dlx, sh)
                dlxp = dlxp + par[3 - sh] * below
                acc[A_CW + 3 - sh] += _fold8(below * lxp)
            dz_ref[rows_of(r0), 0:D_LRU] = dlxp.astype(BF16)
            after_dlx = dlx

            dy_sc = dy_ref[rows_of(r0), D_LRU:D_MIX]
            sb, sc, sx = z_ref[rows_of(r0), col_sb], z_ref[rows_of(r0), col_sc], z_ref[rows_of(r0), col_sx]
            dz_ref[rows_of(r0), col_sb] = (dy_sc * cq_s[rows_of(r0), :]).astype(BF16)
            dcq = dy_sc * sb
            q = sc * sx
            acc_sc[2] += _fold8(dcq * q)
            dq = par_sc[2] * dcq
            for sh in range(1, 3):
                below = _rows_from(dcq, after_dcq, sh)
                dq = dq + par_sc[2 - sh] * below
                acc_sc[2 - sh] += _fold8(below * q)
            dz_ref[rows_of(r0), col_sc] = (dq * sx).astype(BF16)
            dz_ref[rows_of(r0), col_sx] = (dq * sc).astype(BF16)
            after_dcq = dcq
        car_dlx[...] = after_dlx
        car_dcq[...] = after_dcq

        @pl.when(i == n_tiles - 1)
        def _():
            total = lambda x: jnp.sum(x, axis=0, keepdims=True)
            dcw_ref[...] = jnp.zeros_like(dcw_ref)
            dvec_ref[...] = jnp.zeros_like(dvec_ref)
            dscw_ref[...] = jnp.zeros_like(dscw_ref)
            for k in range(4):
                dcw_ref[k:k + 1, :] = total(acc[A_CW + k])
            for k in range(3):
                dvec_ref[k:k + 1, :] = total(acc[k])
                dscw_ref[k:k + 1, :] = total(acc_sc[k])
            dvec_ref[3:4, :] = total(acc[A_SP]) * (-RG_C) * par[P_DSP][0:1, :]

    rev = lambda i: n_tiles - 1 - i
    prev8 = lambda i: jnp.maximum(rev(i) * per8 - 1, 0)
    full = lambda shape: pl.BlockSpec(shape, lambda i: (0,) * len(shape))
    wide = lambda rows, dt=F32: pltpu.VMEM((rows, D_LRU), dt)
    return pl.pallas_call(
        body, name=name, grid=(n_tiles,),
        in_specs=[pl.BlockSpec((tile, D_IN), lambda i: (rev(i), 0)),
                  pl.BlockSpec((HALO, D_IN), lambda i: (prev8(i), 0)),
                  pl.BlockSpec((tile, D_LRU), lambda i: (rev(i), 0)),
                  pl.BlockSpec((HALO, D_LRU), lambda i: (prev8(i), 0)),
                  pl.BlockSpec((tile, D_MIX), lambda i: (rev(i), 0)),
                  full((4, D_LRU)), full((1, D_LRU)),
                  full((N_GROUPS, LRU_GROUP, LRU_GROUP)), full((N_GROUPS, LRU_GROUP, LRU_GROUP)),
                  full((1, D_LRU)), full((1, D_LRU)), full((1, D_LRU)), full((3, D_SC)),
                  pl.BlockSpec(memory_space=pl.ANY)],
        out_specs=[pl.BlockSpec((tile, D_IN), lambda i: (rev(i), 0)),
                   full((8, D_LRU)), full((8, D_LRU)),
                   full((N_GROUPS, LRU_GROUP, LRU_GROUP)), full((N_GROUPS, LRU_GROUP, LRU_GROUP)),
                   full((8, D_SC))],
        out_shape=[jax.ShapeDtypeStruct((s, D_IN), BF16),
                   jax.ShapeDtypeStruct((8, D_LRU), F32), jax.ShapeDtypeStruct((8, D_LRU), F32),
                   jax.ShapeDtypeStruct((N_GROUPS, LRU_GROUP, LRU_GROUP), F32),
                   jax.ShapeDtypeStruct((N_GROUPS, LRU_GROUP, LRU_GROUP), F32),
                   jax.ShapeDtypeStruct((8, D_SC), F32)],
        scratch_shapes=[pltpu.VMEM((N_PAR, FFN_ROWS, D_LRU), F32), pltpu.VMEM((3, FFN_ROWS, D_SC), F32),
                        wide(tile), wide(tile, BF16), pltpu.VMEM((tile, D_SC), F32),
                        wide(tile), wide(tile), wide(tile), wide(tile), wide(tile), wide(tile), wide(tile),
                        wide(tile), wide(tile), wide(tile),
                        wide(tile), wide(tile, BF16), wide(tile, BF16), wide(tile),
                        wide(FFN_ROWS), wide(FFN_ROWS), pltpu.VMEM((FFN_ROWS, D_SC), F32), wide(8),
                        pltpu.VMEM((8, 8, D_LRU), F32), pltpu.VMEM((3, 8, D_SC), F32)],
        compiler_params=_params("arbitrary"),
    )(z, z, hs, hs, dy, cw, cb.reshape(1, -1), wa_bd, wx_bd, ba.reshape(1, -1), bx.reshape(1, -1),
      lam.reshape(1, -1), scw, _behind(token))


FFN_ROWS = 16
FFN_GROUPS = 2


def _spread_taps(fw_ref, taps):
    for half in range(2):
        for k in range(3):
            taps[half, k] = jnp.broadcast_to(fw_ref[half, k:k + 1, :], taps.shape[2:])


def _rows_from(first, second, start):
    stack = jnp.concatenate([first, second], axis=0)
    return pltpu.roll(stack, 2 * FFN_ROWS - start, 0)[0:FFN_ROWS]


def _conv3_rows(taps, ext_ref, half, row):
    before = ext_ref[half, row - FFN_ROWS:row, :]
    here = ext_ref[half, row:row + FFN_ROWS, :]
    acc = taps[half, 2] * here
    for k in range(2):
        acc = acc + taps[half, k] * _rows_from(before, here, FFN_ROWS - 2 + k)
    return acc


HALO_B = 16


def _ffn_block_fwd(x2, g2, w_up_b, fcw, w_down, *, tile, name):
    s = x2.shape[0]
    nb = w_up_b.shape[2]
    blocks = D_FF // nb
    per16 = tile // HALO_B

    def body(x2_ref, x2p_ref, g_ref, wg_ref, wu_ref, fw_ref, wd_ref, x3_ref, h_ref, act_ref, p_ref, u_ref,
             ext_p, acc_ref, taps, lhs):
        i = pl.program_id(0)
        j = pl.program_id(1)
        keep = jnp.where(i == 0, 0.0, 1.0)
        _spread_taps(fw_ref, taps)
        @pl.when(j == 0)
        def _():
            for rows_ref, at in ((x2p_ref, 0), (x2_ref, HALO_B)):
                xv = rows_ref[...]
                lhs[at:at + xv.shape[0], :] = (xv * _rms(xv) * g_ref[...]).astype(BF16)
            h_ref[...] = lhs[HALO_B:HALO_B + tile, :]

        grp = tile // FFN_GROUPS
        for g in range(FFN_GROUPS):
            new = slice(g * grp + (HALO_B if g else 0), (g + 1) * grp + HALO_B)
            for half, w_ref in ((0, wg_ref), (1, wu_ref)):
                pe = _dot(lhs[new, :], w_ref[...], NN)
                if g == 0:
                    ext_p[half, 0:HALO_B, :] = pe[0:HALO_B] * keep
                    ext_p[half, HALO_B:grp + HALO_B, :] = pe[HALO_B:]
                    p_ref[half, 0:grp, :] = pe[HALO_B:].astype(BF16)
                else:
                    ext_p[half, new, :] = pe
                    p_ref[half, g * grp:(g + 1) * grp, :] = pe.astype(BF16)
        for g in range(FFN_GROUPS):
            rows = slice(g * grp, (g + 1) * grp)
            acts = []
            for r0 in range(g * grp, (g + 1) * grp, FFN_ROWS):
                u = [_conv3_rows(taps, ext_p, half, HALO_B + r0) for half in range(2)]
                for half in range(2):
                    u_ref[half, r0:r0 + FFN_ROWS, :] = u[half].astype(BF16)
                acts.append((_gelu(u[0]) * u[1]).astype(BF16))
                act_ref[r0:r0 + FFN_ROWS, :] = acts[-1]
            contrib = _dot(jnp.concatenate(acts, axis=0), wd_ref[...], NN)
            acc_ref[rows, :] = contrib + jnp.where(j > 0, acc_ref[rows, :], 0.0)

        @pl.when(j == blocks - 1)
        def _():
            x3_ref[...] = x2_ref[...] + acc_ref[...]

    return pl.pallas_call(
        body, name=name, grid=(s // tile, blocks),
        in_specs=[pl.BlockSpec((tile, D_MODEL), lambda i, j: (i, 0)),
                  pl.BlockSpec((HALO_B, D_MODEL), lambda i, j: (jnp.maximum(i * per16 - 1, 0), 0)),
                  pl.BlockSpec((1, D_MODEL), lambda i, j: (0, 0)),
                  pl.BlockSpec((None, D_MODEL, nb), lambda i, j: (j, 0, 0)),
                  pl.BlockSpec((None, D_MODEL, nb), lambda i, j: (j + blocks, 0, 0)),
                  pl.BlockSpec((2, 3, nb), lambda i, j: (0, 0, j)),
                  pl.BlockSpec((nb, D_MODEL), lambda i, j: (j, 0))],
        out_specs=[pl.BlockSpec((tile, D_MODEL), lambda i, j: (i, 0)),
                   pl.BlockSpec((tile, D_MODEL), lambda i, j: (i, 0)),
                   pl.BlockSpec((tile, nb), lambda i, j: (i, j)),
                   pl.BlockSpec((2, tile, nb), lambda i, j: (0, i, j)),
                   pl.BlockSpec((2, tile, nb), lambda i, j: (0, i, j))],
        out_shape=[jax.ShapeDtypeStruct((s, D_MODEL), F32), jax.ShapeDtypeStruct((s, D_MODEL), BF16),
                   jax.ShapeDtypeStruct((s, D_FF), BF16),
                   jax.ShapeDtypeStruct((2, s, D_FF), BF16), jax.ShapeDtypeStruct((2, s, D_FF), BF16)],
        scratch_shapes=[pltpu.VMEM((2, tile + HALO_B, nb), F32), pltpu.VMEM((tile, D_MODEL), F32),
                        pltpu.VMEM((2, 3, FFN_ROWS, nb), F32), pltpu.VMEM((tile + HALO_B, D_MODEL), BF16)],
        compiler_params=_params("parallel", "arbitrary"),
    )(x2, x2, g2.reshape(1, -1), w_up_b, w_up_b, fcw, w_down)


def _ffn_block_bwd(dx3, dx3b, p, u, x2, g2, w_up_b, fcw, w_down, *, tile, cols, name, token=None):
    s = x2.shape[0]
    nb = cols
    blocks = D_FF // nb
    sub = w_up_b.shape[2] // cols
    up_first = D_FF // w_up_b.shape[2]
    n_tiles = s // tile
    per16 = tile // HALO_B
    last16 = s // HALO_B - 1

    def body(dxb_ref, dxbn_ref, wd_ref, p_ref, u_ref, un_ref, fw_ref, wg_ref, wu_ref, x2_ref, g_ref, dx3_ref,
             token_ref, dx2_ref, dx2b_ref, dg_ref, dp_ref, dw_ref, da_s, acc_w, acc_dh, taps):
        i = pl.program_id(0)
        j = pl.program_id(1)

        @pl.when(jnp.logical_and(i == 0, j == 0))
        def _():
            acc_w[...] = jnp.zeros_like(acc_w)
            dg_ref[...] = jnp.zeros_like(dg_ref)

        keep_next = jnp.where(i == n_tiles - 1, 0.0, 1.0)
        _spread_taps(fw_ref, taps)
        lhs = jnp.concatenate([dxb_ref[...], dxbn_ref[...]], axis=0)
        grp = tile // FFN_GROUPS
        for g in reversed(range(FFN_GROUPS)):
            new = slice(g * grp, (g + 1) * grp + (HALO_B if g == FFN_GROUPS - 1 else 0))
            da_s[new, :] = _dot(lhs[new], wd_ref[...], NT)

        def du_rows(da, u_gate, u_up):
            ge, dge = _gelu_parts(u_gate)
            return da * u_up * dge, da * ge

        after = du_rows(da_s[tile:tile + HALO_B, :] * keep_next, un_ref[0].astype(F32), un_ref[1].astype(F32))
        for g in reversed(range(FFN_GROUPS)):
            rows = slice(g * grp, (g + 1) * grp)
            dps = ([], [])
            for r0 in range((g + 1) * grp - FFN_ROWS, g * grp - 1, -FFN_ROWS):
                du = du_rows(da_s[r0:r0 + FFN_ROWS, :], u_ref[0, r0:r0 + FFN_ROWS, :].astype(F32),
                             u_ref[1, r0:r0 + FFN_ROWS, :].astype(F32))
                for half in range(2):
                    below = [du[half], _rows_from(du[half], after[half], 1), _rows_from(du[half], after[half], 2)]
                    acc = taps[half, 2] * below[0]
                    for k in range(2):
                        acc = acc + taps[half, k] * below[2 - k]
                    dps[half].insert(0, acc.astype(BF16))
                    dp_ref[half, r0:r0 + FFN_ROWS, :] = dps[half][0]
                    p_rows = p_ref[half, r0:r0 + FFN_ROWS, :].astype(F32)
                    for k in range(3):
                        prod = below[2 - k] * p_rows
                        acc_w[j, half, k] += sum(prod[q:q + 8] for q in range(0, FFN_ROWS, 8))
                after = du
            contrib = (_dot(jnp.concatenate(dps[0], axis=0), wg_ref[...], NT)
                       + _dot(jnp.concatenate(dps[1], axis=0), wu_ref[...], NT))
            acc_dh[rows, :] = contrib + jnp.where(j > 0, acc_dh[rows, :], 0.0)

        @pl.when(j == blocks - 1)
        def _():
            dh = acc_dh[...]
            xv = x2_ref[...]
            rstd = _rms(xv)
            n = xv * rstd
            dn = dh * g_ref[...]
            dx = dx3_ref[...] + rstd * (dn - n * jnp.mean(dn * n, axis=-1, keepdims=True))
            dx2_ref[...] = dx
            dx2b_ref[...] = dx.astype(BF16)
            dg_ref[0:1, :] += jnp.sum(dh * n, axis=0, keepdims=True)

        @pl.when(jnp.logical_and(i == n_tiles - 1, j == blocks - 1))
        def _():
            dw_ref[...] = jnp.zeros_like(dw_ref)
            for jj in range(blocks):
                for half in range(2):
                    for k in range(3):
                        dw_ref[half, k:k + 1, jj * nb:(jj + 1) * nb] = jnp.sum(acc_w[jj, half, k], axis=0, keepdims=True)

    next16 = lambda i: jnp.minimum((i + 1) * per16, last16)
    return pl.pallas_call(
        body, name=name, grid=(n_tiles, blocks),
        in_specs=[pl.BlockSpec((tile, D_MODEL), lambda i, j: (i, 0)),
                  pl.BlockSpec((HALO_B, D_MODEL), lambda i, j: (next16(i), 0)),
                  pl.BlockSpec((nb, D_MODEL), lambda i, j: (j, 0)),
                  pl.BlockSpec((2, tile, nb), lambda i, j: (0, i, j)),
                  pl.BlockSpec((2, tile, nb), lambda i, j: (0, i, j)),
                  pl.BlockSpec((2, HALO_B, nb), lambda i, j: (0, next16(i), j)),
                  pl.BlockSpec((2, 3, nb), lambda i, j: (0, 0, j)),
                  pl.BlockSpec((None, D_MODEL, nb), lambda i, j: (j // sub, 0, j % sub)),
                  pl.BlockSpec((None, D_MODEL, nb), lambda i, j: (j // sub + up_first, 0, j % sub)),
                  pl.BlockSpec((tile, D_MODEL), lambda i, j: (i, 0)),
                  pl.BlockSpec((1, D_MODEL), lambda i, j: (0, 0)),
                  pl.BlockSpec((tile, D_MODEL), lambda i, j: (i, 0)),
                  pl.BlockSpec(memory_space=pl.ANY)],
        out_specs=[pl.BlockSpec((tile, D_MODEL), lambda i, j: (i, 0)),
                   pl.BlockSpec((tile, D_MODEL), lambda i, j: (i, 0)),
                   pl.BlockSpec((8, D_MODEL), lambda i, j: (0, 0)),
                   pl.BlockSpec((2, tile, nb), lambda i, j: (0, i, j)),
                   pl.BlockSpec((2, 8, D_FF), lambda i, j: (0, 0, 0))],
        out_shape=[jax.ShapeDtypeStruct((s, D_MODEL), F32), jax.ShapeDtypeStruct((s, D_MODEL), BF16),
                   jax.ShapeDtypeStruct((8, D_MODEL), F32), jax.ShapeDtypeStruct((2, s, D_FF), BF16),
                   jax.ShapeDtypeStruct((2, 8, D_FF), F32)],
        scratch_shapes=[pltpu.VMEM((tile + HALO_B, nb), F32), pltpu.VMEM((blocks, 2, 3, 8, nb), F32),
                        pltpu.VMEM((tile, D_MODEL), F32), pltpu.VMEM((2, 3, FFN_ROWS, nb), F32)],
        compiler_params=_params("arbitrary", "arbitrary"),
    )(dx3b, dx3b, w_down, p, u, u, fcw, w_up_b, w_up_b, x2, g2.reshape(1, -1), dx3, _behind(token))


def _adamw_math(w, g, m, v):
    m = ADAM_B1 * m + (1.0 - ADAM_B1) * g
    v = ADAM_B2 * v + (1.0 - ADAM_B2) * (g * g)
    m_hat = m / (1.0 - ADAM_B1 ** ADAM_STEP)
    v_hat = v / (1.0 - ADAM_B2 ** ADAM_STEP)
    delta = -ADAM_LR * (m_hat / (jnp.sqrt(v_hat) + ADAM_EPS) + ADAM_WD * w)
    return delta, m, v


def _adamw(w, g, m, v, *, name):
    rows, cols = w.shape
    tr = rows
    for cand in (512, 256, 128, 64, 32, 16, 8):
        if rows % cand == 0 and rows > cand:
            tr = cand
            break

    def body(w_ref, g_ref, m_ref, v_ref, d_ref, nm_ref, nv_ref):
        d, nm, nv = _adamw_math(w_ref[...], g_ref[...], m_ref[...], v_ref[...])
        d_ref[...] = d
        nm_ref[...] = nm
        nv_ref[...] = nv

    spec = pl.BlockSpec((tr, cols), lambda i: (i, 0))
    return pl.pallas_call(
        body, name=name, grid=(rows // tr,), in_specs=[spec] * 4, out_specs=[spec] * 3,
        out_shape=[jax.ShapeDtypeStruct((rows, cols), F32)] * 3,
        compiler_params=_params("parallel"),
    )(w, g, m, v)


def _sum_adamw(parts, w, m, v, *, name):
    depth, rows, cols = w.shape
    tr = rows
    for cand in (256, 128, 64):
        if rows % cand == 0 and rows > cand:
            tr = cand
            break

    def body(*refs):
        part_refs = refs[:depth]
        w_ref, m_ref, v_ref, g_ref, d_ref, nm_ref, nv_ref = refs[depth:]
        layer = pl.program_id(0)
        grad = None
        for k, p_ref in enumerate(part_refs):
            total = p_ref[0].astype(F32)
            for dev in range(1, N_DEV):
                total = total + p_ref[dev].astype(F32)
            grad = total if grad is None else jnp.where(layer == k, total, grad)
        d, nm, nv = _adamw_math(w_ref[...], grad, m_ref[...], v_ref[...])
        g_ref[...] = grad
        d_ref[...] = d
        nm_ref[...] = nm
        nv_ref[...] = nv

    part_spec = lambda k: pl.BlockSpec((N_DEV, tr, cols), lambda l, i: (0, jnp.where(l == k, i, 0), 0))
    spec = pl.BlockSpec((None, tr, cols), lambda l, i: (l, i, 0))
    return pl.pallas_call(
        body, name=name, grid=(depth, rows // tr),
        in_specs=[part_spec(k) for k in range(depth)] + [spec] * 3, out_specs=[spec] * 4,
        out_shape=[jax.ShapeDtypeStruct((depth, rows, cols), F32)] * 4,
        compiler_params=_params("parallel", "parallel"),
    )(*parts, w, m, v)


def _sum_parts(parts, *, name, tokens=()):
    _, rows, cols = parts.shape
    tr = rows
    for cand in (256, 128, 64, 32, 16):
        if rows % cand == 0 and rows > cand:
            tr = cand
            break

    def body(p_ref, *rest):
        acc = p_ref[0].astype(F32)
        for d in range(1, N_DEV):
            acc = acc + p_ref[d].astype(F32)
        rest[-1][...] = acc

    return pl.pallas_call(
        body, name=name, grid=(rows // tr,),
        in_specs=[pl.BlockSpec((N_DEV, tr, cols), lambda i: (0, i, 0))]
        + [pl.BlockSpec(memory_space=pl.ANY)] * len(tokens),
        out_specs=pl.BlockSpec((tr, cols), lambda i: (i, 0)),
        out_shape=jax.ShapeDtypeStruct((rows, cols), F32),
        compiler_params=_params("parallel"),
    )(parts, *tokens)


def _place():
    return lax.axis_index("x"), lax.axis_index("y"), lax.axis_index("c")


def _flip(v, bit):
    return 1 - v if bit else v


N_PEERS = N_DEV - 1


def _peer_copy(k, src_ref, land_ref, send_sem, recv_sem, gather):
    x, y, c = _place()
    my_id = 4 * x + 2 * y + c
    px, py, pc = _flip(x, k & 4), _flip(y, k & 2), _flip(c, k & 1)
    peer_id = 4 * px + 2 * py + pc
    return pltpu.make_async_remote_copy(
        src_ref=src_ref if gather else src_ref.at[peer_id], dst_ref=land_ref.at[my_id],
        send_sem=send_sem.at[k - 1], recv_sem=recv_sem.at[k - 1],
        device_id=(px, py, pc), device_id_type=MESH)


def _sequencer_copies(srcs, *, gather, name, collective_id, after):
    n = len(srcs)
    hbm = pltpu.MemorySpace.HBM
    src_refs = [jax.new_ref(s, memory_space=hbm) for s in srcs]
    land_refs = [jax.empty_ref(jax.ShapeDtypeStruct(((N_DEV,) + s.shape) if gather else s.shape, s.dtype),
                               memory_space=hbm) for s in srcs]
    token_in = jax.new_ref(jnp.zeros((8, 128), F32) if after is None else after, memory_space=hbm)
    token_out = jax.empty_ref(jax.ShapeDtypeStruct((8, 128), F32), memory_space=hbm)

    @pl.kernel(mesh=plsc.ScalarSubcoreMesh(axis_name="seq", num_cores=1), name=name,
               scratch_types=(pltpu.SemaphoreType.DMA((n, N_PEERS)), pltpu.SemaphoreType.DMA((n, N_PEERS)),
                              pltpu.SemaphoreType.DMA((n + 1,))),
               compiler_params=pltpu.CompilerParams(collective_id=collective_id))
    def launch(send_sems, recv_sems, local_sems):
        x, y, c = _place()
        my_id = 4 * x + 2 * y + c
        barrier = pltpu.get_barrier_semaphore()
        own = [pltpu.make_async_copy(src_refs[t] if gather else src_refs[t].at[my_id], land_refs[t].at[my_id],
                                     local_sems.at[t]) for t in range(n)]
        if gather:
            sibling = (x, y, 1 - c)
            chips = [(1 - x, y), (x, 1 - y), (1 - x, 1 - y)]
            for peer in [sibling] + [(*chip, c) for chip in chips]:
                pl.semaphore_signal(barrier, inc=1, device_id=peer, device_id_type=MESH)
            pl.semaphore_wait(barrier, 4)

            def copy(t, k, block, to, src=None):
                dst = land_refs[t].at[4 * block[0] + 2 * block[1] + block[2]]
                return pltpu.make_async_remote_copy(
                    src_ref=dst if src is None else src, dst_ref=dst,
                    send_sem=send_sems.at[t, k], recv_sem=recv_sems.at[t, k], device_id=to, device_id_type=MESH)

            for cp in own:
                cp.start()
            sends = []
            for t in range(n):
                sends.append(copy(t, 0, (x, y, c), sibling, src=src_refs[t]))
                sends += [copy(t, 1 + j, (x, y, c), (*chip, c), src=src_refs[t]) for j, chip in enumerate(chips)]
            for cp in sends:
                cp.start()
            for t in range(n):
                for j, chip in enumerate(chips):
                    copy(t, 1 + j, (*chip, c), (x, y, c)).wait_recv()
                    passed_on = copy(t, 4 + j, (*chip, c), sibling)
                    passed_on.start()
                    sends.append(passed_on)
            for t in range(n):
                copy(t, 0, sibling, (x, y, c)).wait_recv()
                for j, chip in enumerate(chips):
                    copy(t, 4 + j, (*chip, 1 - c), (x, y, c)).wait_recv()
            for cp in sends:
                cp.wait_send()
            for cp in own:
                cp.wait()
        else:
            for k in range(1, N_DEV):
                peer = (_flip(x, k & 4), _flip(y, k & 2), _flip(c, k & 1))
                pl.semaphore_signal(barrier, inc=1, device_id=peer, device_id_type=MESH)
            pl.semaphore_wait(barrier, N_PEERS)
            for cp in own:
                cp.start()
            copies = [_peer_copy(k, src_refs[t], land_refs[t], send_sems.at[t], recv_sems.at[t], gather)
                      for t in range(n) for k in range(1, N_DEV)]
            for cp in copies:
                cp.start()
            for cp in own:
                cp.wait()
            for cp in copies:
                cp.wait()
        passed = pltpu.make_async_copy(token_in, token_out, local_sems.at[n])
        passed.start()
        passed.wait()

    launch()
    return [ref[...] for ref in land_refs], token_out[...]


TM = 512
TMM = 1024
TKW = 2048
MIX_TILE = 256
FFN_FWD_TILE = 1024
FFN_BWD_TILE = 1024
FFN_BWD_COLS = 384


def _block_diag(w):
    wg = w.reshape(N_GROUPS, HEADS_PER_GROUP, LRU_HEAD_DIM, LRU_HEAD_DIM)
    eye = jnp.eye(HEADS_PER_GROUP, dtype=w.dtype)
    bd = wg[:, :, :, None, :] * eye[None, :, None, :, None]
    return bd.reshape(N_GROUPS, LRU_GROUP, LRU_GROUP).astype(BF16)


def _head_blocks(bd):
    b5 = bd.reshape(N_GROUPS, HEADS_PER_GROUP, LRU_HEAD_DIM, HEADS_PER_GROUP, LRU_HEAD_DIM)
    blocks = [b5[:, h, :, h, :] for h in range(HEADS_PER_GROUP)]
    return jnp.stack(blocks, axis=1).reshape(LRU_HEADS, LRU_HEAD_DIM, LRU_HEAD_DIM)


def _w(lw, key, after):
    value = lw[key]
    return value(after) if callable(value) else value


def _layer_fwd(x, lw, tag):
    sv_rows = x.shape[0]
    z, h1 = _norm_in_proj(x, lw["g1"], _w(lw, "w_in_t", x), tm=min(2 * TMM, sv_rows), tn=896, name=f"in_proj_{tag}")
    y_mix, hs = _mixer_fwd(z, _w(lw, "cw", z), lw["cb"], lw["wa_bd"], lw["wx_bd"], lw["ba"], lw["bx"], lw["lam"],
                           _w(lw, "scw", z), tile=MIX_TILE, name=f"mixer_fwd_{tag}")
    x2 = _mm_nn(y_mix, _w(lw, "w_out", y_mix), tm=min(TMM, sv_rows), tn=D_MODEL, tk=D_MIX, out_dtype=F32, name=f"out_proj_{tag}",
                residual=x)
    x3, h2, act, p, u = _ffn_block_fwd(x2, lw["g2"], _w(lw, "w_up_b", x2), _w(lw, "fcw", x2), _w(lw, "w_down", x2),
                                       tile=min(FFN_FWD_TILE, sv_rows), name=f"ffn_fwd_{tag}")
    saved = dict(x=x, h1=h1, z=z, y_mix=y_mix, hs=hs, x2=x2, h2=h2, p=p, u=u, act=act)
    return x3, saved


def _layer_bwd(dx3, dx3b, lw, sv, tag, put):
    sv_rows = dx3.shape[0]
    w_in_t, w_out, w_up_b, w_down = (_w(lw, k, dx3) for k in ("w_in_t", "w_out", "w_up_b", "w_down"))
    cw, scw, fcw = (_w(lw, k, dx3) for k in ("cw", "scw", "fcw"))
    g_down = _mm_tn(sv["act"], dx3b, tm=1024, tn=D_MODEL, tk=min(TKW, sv_rows), out_dtype=BF16, name=f"down_bwd_w_{tag}")
    dx2, dx2b, dg2, dp, dfcw = _ffn_block_bwd(dx3, dx3b, sv["p"], sv["u"], sv["x2"], lw["g2"], w_up_b, fcw, w_down,
                                              tile=min(FFN_BWD_TILE, sv_rows), cols=FFN_BWD_COLS, name=f"ffn_bwd_{tag}",
                                              token=put("w_down", g_down))
    g_up = _mm_up_bwd_w(sv["h2"], dp, tm=D_MODEL, tk=min(TKW, sv_rows), name=f"up_bwd_w_{tag}")
    dy = _mm_nt(dx2b, w_out, tm=min(TMM, sv_rows), tn=768, tk=D_MODEL, out_dtype=F32, name=f"out_bwd_x_{tag}")
    dz, dcw, dvec, dwa, dwx, dscw = _mixer_bwd_rows(
        sv["z"], sv["hs"], dy, cw, lw["cb"], lw["wa_bd"], lw["wx_bd"], lw["ba"], lw["bx"], lw["lam"],
        scw, tile=MIX_TILE, name=f"mixer_bwd_{tag}", token=put("w_up_b", g_up))
    g_out = _mm_tn(sv["y_mix"], dx2b, tm=768, tn=D_MODEL, tk=min(TKW, sv_rows), out_dtype=BF16, name=f"out_bwd_w_{tag}",
                   token=dz)
    g_in_t = _mm_tn(dz, sv["h1"], tm=896, tn=D_MODEL, tk=min(TKW, sv_rows), out_dtype=BF16, name=f"in_bwd_w_{tag}",
                    token=put("w_out", g_out))
    dx, dxb, dg1 = _in_bwd_norm(dz, w_in_t, sv["x"], lw["g1"], dx2, tm=min(TMM, sv_rows), tk=896, name=f"in_bwd_x_{tag}",
                                token=put("w_in_t", g_in_t))
    small = dict(norm1_g=dg1[0], lru_conv_w=dcw[0:4], lru_conv_b=dvec[0], lru_wa=_head_blocks(dwa),
                 lru_ba=dvec[1], lru_wx=_head_blocks(dwx), lru_bx=dvec[2], lru_lambda=dvec[3],
                 sc_conv_w=dscw[0:3], norm2_g=dg2[0], ffn_conv_w=dfcw[:, 0:3, :])
    return dx, dxb, small


SMALL_ORDER = ("norm1_g", "lru_conv_w", "lru_conv_b", "lru_wa", "lru_ba", "lru_wx", "lru_bx", "lru_lambda",
               "sc_conv_w", "norm2_g", "ffn_conv_w")


def _local_step(x, tgt, layers, final_g, put):
    saved = []
    h = x
    for l in range(DEPTH):
        h, sv = _layer_fwd(h, layers[l], f"l{l}")
        saved.append(sv)
    loss_blk, dx, dxb, dgf = _loss_head(h, final_g, tgt, tm=TM, name="loss_head")
    smalls = [None] * DEPTH
    for l in reversed(range(DEPTH)):
        dx, dxb, smalls[l] = _layer_bwd(dx, dxb, layers[l], saved[l], f"l{l}", functools.partial(put, l))
    return loss_blk[0, 0], dx, smalls, dgf[0]


def kernel(x, norm1_g, w_in, lru_conv_w, lru_conv_b, lru_wa, lru_ba, lru_wx, lru_bx, lru_lambda, sc_conv_w, w_out, norm2_g, w_up, ffn_conv_w, w_down, final_g, loss_target, m_norm1_g, m_w_in, m_lru_conv_w, m_lru_conv_b, m_lru_wa, m_lru_ba, m_lru_wx, m_lru_bx, m_lru_lambda, m_sc_conv_w, m_w_out, m_norm2_g, m_w_up, m_ffn_conv_w, m_w_down, m_final_g, v_norm1_g, v_w_in, v_lru_conv_w, v_lru_conv_b, v_lru_wa, v_lru_ba, v_lru_wx, v_lru_bx, v_lru_lambda, v_sc_conv_w, v_w_out, v_norm2_g, v_w_up, v_ffn_conv_w, v_w_down, v_final_g):
    names = ["norm1_g", "w_in", "lru_conv_w", "lru_conv_b", "lru_wa", "lru_ba", "lru_wx", "lru_bx", "lru_lambda",
             "sc_conv_w", "w_out", "norm2_g", "w_up", "ffn_conv_w", "w_down", "final_g"]
    w = dict(zip(names, [norm1_g, w_in, lru_conv_w, lru_conv_b, lru_wa, lru_ba, lru_wx, lru_bx, lru_lambda,
                         sc_conv_w, w_out, norm2_g, w_up, ffn_conv_w, w_down, final_g]))
    m = dict(zip(names, [m_norm1_g, m_w_in, m_lru_conv_w, m_lru_conv_b, m_lru_wa, m_lru_ba, m_lru_wx, m_lru_bx,
                         m_lru_lambda, m_sc_conv_w, m_w_out, m_norm2_g, m_w_up, m_ffn_conv_w, m_w_down, m_final_g]))
    v = dict(zip(names, [v_norm1_g, v_w_in, v_lru_conv_w, v_lru_conv_b, v_lru_wa, v_lru_ba, v_lru_wx, v_lru_bx,
                         v_lru_lambda, v_sc_conv_w, v_w_out, v_norm2_g, v_w_up, v_ffn_conv_w, v_w_down, v_final_g]))
    my_id = 4 * lax.axis_index("x") + 2 * lax.axis_index("y") + lax.axis_index("c")

    taps = jnp.zeros((DEPTH, 16, 768), F32)
    taps = taps.at[:, 0:4, 0:128].set(lru_conv_w).at[:, 4:7, 0:64].set(sc_conv_w).at[:, 8:11, :].set(ffn_conv_w)
    shards = {}
    for l in range(DEPTH):
        shards[f"w_in_t{l}"] = jnp.swapaxes(w_in[l], 0, 1).astype(BF16)
        if l == 0:
            shards["taps"] = taps.reshape(DEPTH * 16, 768)
        shards[f"w_out{l}"] = w_out[l].astype(BF16)
        shards[f"w_up_b{l}"] = w_up[l].astype(BF16)
        shards[f"w_down{l}"] = w_down[l].astype(BF16)
    ids = iter(range(18))
    got = {}
    chain = [None]
    for group in (("w_in_t0", "taps"), ("w_out0",), ("w_up_b0",), ("w_down0",),
                  ("w_in_t1",), ("w_out1",), ("w_up_b1",), ("w_down1",)):
        lands, chain[0] = _sequencer_copies([shards[k] for k in group], gather=True, name=f"gather_{group[0]}",
                                            collective_id=next(ids), after=None)
        got.update(zip(group, lands))

    def fetch(key, after):
        return got[key]

    def tap_rows(l, lo, hi, width, after):
        tl = fetch("taps", after).reshape(N_DEV, DEPTH, 16, 768)[:, l, lo:hi, 0:width]
        return jnp.transpose(tl, (1, 0, 2)).reshape(hi - lo, N_DEV * width)

    layers = []
    for l in range(DEPTH):
        layers.append(dict(
            g1=norm1_g[l], g2=norm2_g[l], cb=lru_conv_b[l], ba=lru_ba[l], bx=lru_bx[l], lam=lru_lambda[l],
            wa_bd=_block_diag(lru_wa[l]), wx_bd=_block_diag(lru_wx[l]),
            cw=functools.partial(tap_rows, l, 0, 4, 128), scw=functools.partial(tap_rows, l, 4, 7, 64),
            fcw=lambda after, l=l: tap_rows(l, 8, 11, 768, after).reshape(3, 2, D_FF).transpose(1, 0, 2),
            w_in_t=lambda after, l=l: fetch(f"w_in_t{l}", after).reshape(D_IN, D_MODEL),
            w_out=lambda after, l=l: fetch(f"w_out{l}", after).reshape(D_MIX, D_MODEL),
            w_up_b=lambda after, l=l: fetch(f"w_up_b{l}", after),
            w_down=lambda after, l=l: fetch(f"w_down{l}", after).reshape(D_FF, D_MODEL)))

    scatter_handles = {}

    def put(l, key, grad):
        blocks = grad if grad.ndim == 3 else grad.reshape(N_DEV, grad.shape[0] // N_DEV, grad.shape[1])
        (scatter_handles[(l, key)],), chain[0] = _sequencer_copies(
            [blocks], gather=False, name=f"scatter_{key}{l}", collective_id=next(ids), after=chain[0])
        return blocks

    loss_local, dx, smalls, dgf = _local_step(x[0], loss_target[0], layers, final_g, put)

    parts = []
    for l in range(DEPTH):
        for key in ("w_in_t", "w_out", "w_up_b", "w_down"):
            parts.append(scatter_handles[(l, key)])

    flat = [smalls[l][k].reshape(-1) for l in range(DEPTH) for k in SMALL_ORDER] + [dgf.reshape(-1)]
    flat.append(jnp.broadcast_to(loss_local, (128,)))
    sizes = [f.shape[0] for f in flat]
    total = sum(sizes)
    rows = -(-total // (N_DEV * 128 * 8)) * 8
    flat.append(jnp.zeros((N_DEV * rows * 128 - total,), F32))
    (small_parts,), chain[0] = _sequencer_copies([jnp.concatenate(flat).reshape(N_DEV, rows, 128)], gather=False,
                                                 name="scatter_small", collective_id=next(ids), after=chain[0])
    grads, deltas, new_m, new_v = {}, {}, {}, {}
    for slot, k in ((2, "w_up"), (3, "w_down"), (1, "w_out")):
        grads[k], deltas[k], new_m[k], new_v[k] = _sum_adamw(
            [parts[4 * l + slot] for l in range(DEPTH)], w[k], m[k], v[k], name=f"adamw_{k}")
    small_mine = _sum_parts(small_parts, name="sum_small", tokens=(deltas["w_up"], deltas["w_down"]))
    (small_all,), _ = _sequencer_copies([small_mine], gather=True, name="gather_small",
                                        collective_id=next(ids), after=chain[0])
    grads["w_in"] = jnp.stack([jnp.swapaxes(_sum_parts(parts[4 * l], name=f"sum_w_in_l{l}"), 0, 1)
                               for l in range(DEPTH)])
    small_sum = small_all.reshape(-1)
    small_g, off = [], 0
    for sz in sizes:
        small_g.append(small_sum[off:off + sz])
        off += sz
    gs = {}
    for l in range(DEPTH):
        for i, k in enumerate(SMALL_ORDER):
            gs.setdefault(k, []).append(small_g[l * len(SMALL_ORDER) + i])
    g_final = small_g[-2]
    loss = small_g[-1][0]

    for k in ("norm1_g", "lru_conv_b", "lru_ba", "lru_bx", "lru_lambda", "norm2_g"):
        grads[k] = jnp.stack(gs[k]).reshape(DEPTH, -1)
    for k in ("lru_wa", "lru_wx"):
        grads[k] = jnp.stack(gs[k]).reshape(DEPTH, LRU_HEADS, LRU_HEAD_DIM, LRU_HEAD_DIM)
    grads["final_g"] = g_final
    cw_full = jnp.stack(gs["lru_conv_w"]).reshape(DEPTH, 4, N_DEV, 128)
    grads["lru_conv_w"] = lax.dynamic_index_in_dim(cw_full, my_id, axis=2, keepdims=False)
    scw_full = jnp.stack(gs["sc_conv_w"]).reshape(DEPTH, 3, N_DEV, 64)
    grads["sc_conv_w"] = lax.dynamic_index_in_dim(scw_full, my_id, axis=2, keepdims=False)
    fcw_full = jnp.stack(gs["ffn_conv_w"]).reshape(DEPTH, 2, 3, D_FF).transpose(0, 2, 1, 3).reshape(DEPTH, 3, N_DEV, 768)
    grads["ffn_conv_w"] = lax.dynamic_index_in_dim(fcw_full, my_id, axis=2, keepdims=False)

    for k in names:
        if k in deltas:
            continue
        shape = w[k].shape
        cols = shape[-1]
        as2d = lambda a: a.reshape(-1, cols)
        d, nm, nv = _adamw(as2d(w[k]), as2d(grads[k]), as2d(m[k]), as2d(v[k]), name=f"adamw_{k}")
        deltas[k], new_m[k], new_v[k] = d.reshape(shape), nm.reshape(shape), nv.reshape(shape)

    return (loss, dx[None], *[grads[k] for k in names], *[deltas[k] for k in names],
            *[new_m[k] for k in names], *[new_v[k] for k in names])
```

```python
import functools
import math

import jax
import jax.numpy as jnp
from jax import lax
from jax.experimental import pallas as pl
from jax.experimental.pallas import tpu as pltpu
from jax.experimental.pallas import tpu_sc as plsc

F32 = jnp.float32
BF16 = jnp.bfloat16

N_DEV = 8
DEPTH = 2
D_MODEL = 1024
D_LRU = 1024
D_SC = 512
D_MIX = D_LRU + D_SC
D_IN = 2 * D_LRU + 3 * D_SC
D_FF = 3072
LRU_HEADS = 16
LRU_HEAD_DIM = 64
LRU_GROUP = 256
N_GROUPS = D_LRU // LRU_GROUP
HEADS_PER_GROUP = LRU_GROUP // LRU_HEAD_DIM
RG_C = 8.0
EPS = 1e-6
HALO = 8

ADAM_LR = 0.001
ADAM_B1 = 0.9
ADAM_B2 = 0.999
ADAM_EPS = 1e-08
ADAM_WD = 0.01
ADAM_STEP = 10

GELU_C = math.sqrt(2.0 / math.pi)
GELU_A = 0.044715

VMEM_LIMIT = 56 * 1024 * 1024
MESH = pl.DeviceIdType.MESH


def _params(*sem):
    return pltpu.CompilerParams(dimension_semantics=tuple(sem) if sem else None,
                                vmem_limit_bytes=VMEM_LIMIT)


def _gelu_parts(x):
    x2 = x * x
    t = jnp.tanh(GELU_C * (x + GELU_A * x * x2))
    half = 0.5 * (1.0 + t)
    g = x * half
    dg = half + 0.5 * x * (1.0 - t * t) * (GELU_C * (1.0 + 3.0 * GELU_A * x2))
    return g, dg


def _gelu(x):
    t = jnp.tanh(GELU_C * (x + GELU_A * x * x * x))
    return 0.5 * x * (1.0 + t)


def _sigmoid(x):
    return 0.5 * jnp.tanh(0.5 * x) + 0.5


def _softplus(x):
    e = jnp.exp(-jnp.abs(x))
    u = 1.0 + e
    log1p_e = jnp.where(u == 1.0, e, jnp.log(u) * (e / (u - 1.0)))
    return jnp.maximum(x, 0.0) + log1p_e


def _rms(x):
    ms = jnp.mean(x * x, axis=-1, keepdims=True)
    return lax.rsqrt(ms + EPS)


def _dot(a, b, dims):
    return lax.dot_general(a, b, (dims, ((), ())), preferred_element_type=F32)


NN = ((1,), (0,))
NT = ((1,), (1,))
TN = ((0,), (0,))


def _matmul(a, b, *, dims, grid, a_spec, b_spec, o_spec, out_shape, acc_shape, name,
            residual=None, r_spec=None, token=None):
    nk = grid[2]

    def body(*refs):
        a_ref, b_ref = refs[0], refs[1]
        r_ref = refs[2] if residual is not None else None
        o_ref = refs[2 + (residual is not None) + (token is not None)]
        prod = _dot(a_ref[...].astype(BF16), b_ref[...].astype(BF16), dims)

        def finish(total):
            if r_ref is not None:
                total = total + r_ref[...]
            o_ref[...] = total.astype(o_ref.dtype)

        if nk == 1:
            finish(prod)
            return
        acc_ref = refs[-1]
        k = pl.program_id(2)

        @pl.when(k == 0)
        def _():
            acc_ref[...] = prod

        @pl.when(jnp.logical_and(k > 0, k < nk - 1))
        def _():
            acc_ref[...] += prod

        @pl.when(k == nk - 1)
        def _():
            finish(acc_ref[...] + prod)

    in_specs = [a_spec, b_spec]
    args = [a, b]
    if residual is not None:
        in_specs.append(r_spec)
        args.append(residual)
    if token is not None:
        in_specs.append(pl.BlockSpec(memory_space=pl.ANY))
        args.append(token)
    return pl.pallas_call(
        body, name=name, grid=grid, in_specs=in_specs, out_specs=o_spec, out_shape=out_shape,
        scratch_shapes=[pltpu.VMEM(acc_shape, F32)] if nk > 1 else [],
        compiler_params=_params("parallel", "parallel", "arbitrary"),
    )(*args)


def _mm_nn(a, b, *, tm, tn, tk, out_dtype, name, residual=None, token=None):
    m, kd = a.shape
    n = b.shape[1]
    return _matmul(
        a, b, dims=NN, grid=(m // tm, n // tn, kd // tk),
        a_spec=pl.BlockSpec((tm, tk), lambda i, j, k: (i, k)),
        b_spec=pl.BlockSpec((tk, tn), lambda i, j, k: (k, j)),
        o_spec=pl.BlockSpec((tm, tn), lambda i, j, k: (i, j)),
        out_shape=jax.ShapeDtypeStruct((m, n), out_dtype), acc_shape=(tm, tn), name=name,
        residual=residual, r_spec=pl.BlockSpec((tm, tn), lambda i, j, k: (i, j)), token=token)


def _mm_nt(a, b, *, tm, tn, tk, out_dtype, name):
    m, kd = a.shape
    n = b.shape[0]
    return _matmul(
        a, b, dims=NT, grid=(m // tm, n // tn, kd // tk),
        a_spec=pl.BlockSpec((tm, tk), lambda i, j, k: (i, k)),
        b_spec=pl.BlockSpec((tn, tk), lambda i, j, k: (j, k)),
        o_spec=pl.BlockSpec((tm, tn), lambda i, j, k: (i, j)),
        out_shape=jax.ShapeDtypeStruct((m, n), out_dtype), acc_shape=(tm, tn), name=name)


def _mm_tn(a, b, *, tm, tn, tk, out_dtype, name, token=None):
    kd, m = a.shape
    n = b.shape[1]
    return _matmul(
        a, b, dims=TN, grid=(m // tm, n // tn, kd // tk),
        a_spec=pl.BlockSpec((tk, tm), lambda i, j, k: (k, i)),
        b_spec=pl.BlockSpec((tk, tn), lambda i, j, k: (k, j)),
        o_spec=pl.BlockSpec((tm, tn), lambda i, j, k: (i, j)),
        out_shape=jax.ShapeDtypeStruct((m, n), out_dtype), acc_shape=(tm, tn), name=name, token=token)


def _mm_up_bwd_w(h2, dp, *, tm, tk, name):
    s = h2.shape[0]
    nb = D_FF * 2 // N_DEV
    per_half = D_FF // nb
    return _matmul(
        h2, dp, dims=TN, grid=(D_MODEL // tm, N_DEV, s // tk),
        a_spec=pl.BlockSpec((tk, tm), lambda i, j, k: (k, i)),
        b_spec=pl.BlockSpec((None, tk, nb), lambda i, j, k: (j // per_half, k, j % per_half)),
        o_spec=pl.BlockSpec((None, tm, nb), lambda i, j, k: (j, i, 0)),
        out_shape=jax.ShapeDtypeStruct((N_DEV, D_MODEL, nb), BF16), acc_shape=(tm, nb), name=name)


def _behind(token):
    return jnp.zeros((8, 128), F32) if token is None else token


def _norm_in_proj(x, g, w_in_t, *, tm, tn, name):
    s, d = x.shape
    n = w_in_t.shape[0]

    def body(x_ref, g_ref, w_ref, z_ref, h_ref):
        @pl.when(pl.program_id(1) == 0)
        def _():
            xv = x_ref[...]
            h_ref[...] = (xv * _rms(xv) * g_ref[...]).astype(BF16)

        z_ref[...] = _dot(h_ref[...], w_ref[...], NT)

    return pl.pallas_call(
        body, name=name, grid=(s // tm, n // tn),
        in_specs=[pl.BlockSpec((tm, d), lambda i, j: (i, 0)), pl.BlockSpec((1, d), lambda i, j: (0, 0)),
                  pl.BlockSpec((tn, d), lambda i, j: (j, 0))],
        out_specs=[pl.BlockSpec((tm, tn), lambda i, j: (i, j)), pl.BlockSpec((tm, d), lambda i, j: (i, 0))],
        out_shape=[jax.ShapeDtypeStruct((s, n), F32), jax.ShapeDtypeStruct((s, d), BF16)],
        compiler_params=_params("parallel", "arbitrary"),
    )(x, g.reshape(1, d), w_in_t)


def _in_bwd_norm(dz, w_in_t, x, g, dres, *, tm, tk, name, token=None):
    s, kd = dz.shape
    d = w_in_t.shape[1]
    nk = kd // tk

    def body(dz_ref, w_ref, x_ref, g_ref, dres_ref, token_ref, dx_ref, dxb_ref, dg_ref, acc_ref):
        i = pl.program_id(0)
        k = pl.program_id(1)

        @pl.when(jnp.logical_and(i == 0, k == 0))
        def _():
            dg_ref[...] = jnp.zeros_like(dg_ref)

        acc_ref[...] = _dot(dz_ref[...], w_ref[...], NN) + jnp.where(k > 0, acc_ref[...], 0.0)

        @pl.when(k == nk - 1)
        def _():
            dh = acc_ref[...]
            xv = x_ref[...]
            rstd = _rms(xv)
            n = xv * rstd
            dn = dh * g_ref[...]
            dx = dres_ref[...] + rstd * (dn - n * jnp.mean(dn * n, axis=-1, keepdims=True))
            dx_ref[...] = dx
            dxb_ref[...] = dx.astype(BF16)
            dg_ref[0:1, :] += jnp.sum(dh * n, axis=0, keepdims=True)

    row = pl.BlockSpec((tm, d), lambda i, k: (i, 0))
    return pl.pallas_call(
        body, name=name, grid=(s // tm, nk),
        in_specs=[pl.BlockSpec((tm, tk), lambda i, k: (i, k)), pl.BlockSpec((tk, d), lambda i, k: (k, 0)),
                  row, pl.BlockSpec((1, d), lambda i, k: (0, 0)), row, pl.BlockSpec(memory_space=pl.ANY)],
        out_specs=[row, row, pl.BlockSpec((8, d), lambda i, k: (0, 0))],
        out_shape=[jax.ShapeDtypeStruct((s, d), F32), jax.ShapeDtypeStruct((s, d), BF16),
                   jax.ShapeDtypeStruct((8, d), F32)],
        scratch_shapes=[pltpu.VMEM((tm, d), F32)],
        compiler_params=_params("arbitrary", "arbitrary"),
    )(dz, w_in_t, x, g.reshape(1, d), dres, _behind(token))


def _loss_head(x, g, tgt, *, tm, name):
    s, d = x.shape

    def body(x_ref, g_ref, t_ref, loss_ref, dx_ref, dxb_ref, dg_ref):
        @pl.when(pl.program_id(0) == 0)
        def _():
            dg_ref[...] = jnp.zeros_like(dg_ref)
            loss_ref[...] = jnp.zeros_like(loss_ref)

        xv = x_ref[...]
        gv = g_ref[...]
        rstd = _rms(xv)
        n = xv * rstd
        e = n * gv - t_ref[...]
        part = 0.5 * jnp.sum(jnp.mean(e * e, axis=-1, keepdims=True), axis=0, keepdims=True)
        loss_ref[...] += jnp.broadcast_to(part, loss_ref.shape)
        dy = e * (1.0 / d)
        dn = dy * gv
        dx = rstd * (dn - n * jnp.mean(dn * n, axis=-1, keepdims=True))
        dx_ref[...] = dx
        dxb_ref[...] = dx.astype(BF16)
        dg_ref[0:1, :] += jnp.sum(dy * n, axis=0, keepdims=True)

    return pl.pallas_call(
        body, name=name, grid=(s // tm,),
        in_specs=[pl.BlockSpec((tm, d), lambda i: (i, 0)), pl.BlockSpec((1, d), lambda i: (0, 0)),
                  pl.BlockSpec((tm, d), lambda i: (i, 0))],
        out_specs=[pl.BlockSpec((8, 128), lambda i: (0, 0)), pl.BlockSpec((tm, d), lambda i: (i, 0)),
                   pl.BlockSpec((tm, d), lambda i: (i, 0)), pl.BlockSpec((8, d), lambda i: (0, 0))],
        out_shape=[jax.ShapeDtypeStruct((8, 128), F32), jax.ShapeDtypeStruct((s, d), F32),
                   jax.ShapeDtypeStruct((s, d), BF16), jax.ShapeDtypeStruct((8, d), F32)],
        compiler_params=_params("arbitrary"),
    )(x, g.reshape(1, d), tgt)


def _scan_rows(a_ref, b_ref, h_ref, carry, *, rows, reverse):
    width = a_ref.shape[1]
    n_chunks = rows // 8
    row = lax.broadcasted_iota(jnp.int32, (8, width), 0)

    def step(ci, carry):
        chunk = (n_chunks - 1 - ci) if reverse else ci
        off = pl.multiple_of(chunk * 8, 8)
        av = a_ref[pl.ds(off, 8), :]
        bv = b_ref[pl.ds(off, 8), :]
        for sh in (1, 2, 4):
            if reverse:
                a_sh = pltpu.roll(av, 8 - sh, 0)
                b_sh = pltpu.roll(bv, 8 - sh, 0)
                m = row < 8 - sh
            else:
                a_sh = pltpu.roll(av, sh, 0)
                b_sh = pltpu.roll(bv, sh, 0)
                m = row >= sh
            bv = jnp.where(m, av * b_sh + bv, bv)
            av = jnp.where(m, av * a_sh, av)
        h = av * carry + bv
        h_ref[pl.ds(off, 8), :] = h
        return h[0:1, :] if reverse else h[7:8, :]

    return lax.fori_loop(0, n_chunks, step, carry)


P_CB, P_BA, P_BX, P_DECAY, P_DSP, N_PAR = 4, 5, 6, 7, 8, 9


def _spread_mixer_params(par, par_sc, cw_ref, cb_ref, ba_ref, bx_ref, lam_ref, scw_ref):
    rows = par.shape[1:]
    for k in range(4):
        par[k] = jnp.broadcast_to(cw_ref[k:k + 1, :], rows)
    par[P_CB] = jnp.broadcast_to(cb_ref[...], rows)
    par[P_BA] = jnp.broadcast_to(ba_ref[...], rows)
    par[P_BX] = jnp.broadcast_to(bx_ref[...], rows)
    par[P_DECAY] = jnp.broadcast_to(-RG_C * _softplus(-lam_ref[...]), rows)
    par[P_DSP] = jnp.broadcast_to(-_sigmoid(-lam_ref[...]), rows)
    for k in range(3):
        par_sc[k] = jnp.broadcast_to(scw_ref[k:k + 1, :], par_sc.shape[1:])


def _gates_rows(pre_r, pre_i, par):
    r = _sigmoid(pre_r + par[P_BA])
    ig = _sigmoid(pre_i + par[P_BX])
    log_a = r * par[P_DECAY]
    a = jnp.exp(log_a)
    one_minus_a2 = -jnp.tanh(log_a) * (a * a + 1.0)
    return r, ig, a, jnp.sqrt(one_minus_a2), one_minus_a2


def _mixer_fwd(z, cw, cb, wa_bd, wx_bd, ba, bx, lam, scw, *, tile, name):
    s = z.shape[0]
    n_tiles = s // tile

    rows_of = lambda r0: slice(r0, r0 + FFN_ROWS)
    col_gate, col_sb, col_sc, col_sx = (slice(D_LRU, 2 * D_LRU), slice(2 * D_LRU, 2 * D_LRU + D_SC),
                                        slice(2 * D_LRU + D_SC, 2 * D_LRU + 2 * D_SC), slice(2 * D_LRU + 2 * D_SC, D_IN))

    def body(z_ref, cw_ref, cb_ref, wa_ref, wx_ref, ba_ref, bx_ref, lam_ref, scw_ref,
             y_ref, hs_ref, par, par_sc, lx_s, lxb_s, a_s, b_s, car_lx, car_q, h_car):
        i = pl.program_id(0)

        @pl.when(i == 0)
        def _():
            car_lx[...] = jnp.zeros_like(car_lx)
            car_q[...] = jnp.zeros_like(car_q)
            h_car[...] = jnp.zeros_like(h_car)
            _spread_mixer_params(par, par_sc, cw_ref, cb_ref, ba_ref, bx_ref, lam_ref, scw_ref)

        before_lx, before_q = car_lx[...], car_q[...]
        for r0 in range(0, tile, FFN_ROWS):
            cur = z_ref[rows_of(r0), 0:D_LRU]
            lx = par[P_CB] + par[3] * cur
            for k in range(3):
                lx = lx + par[k] * _rows_from(before_lx, cur, FFN_ROWS - 3 + k)
            lx_s[rows_of(r0), :] = lx
            lxb_s[rows_of(r0), :] = lx.astype(BF16)
            before_lx = cur
            q = z_ref[rows_of(r0), col_sc] * z_ref[rows_of(r0), col_sx]
            cq = par_sc[2] * q
            for k in range(2):
                cq = cq + par_sc[k] * _rows_from(before_q, q, FFN_ROWS - 2 + k)
            y_ref[rows_of(r0), D_LRU:D_MIX] = (z_ref[rows_of(r0), col_sb] * cq).astype(BF16)
            before_q = q
        car_lx[...] = before_lx
        car_q[...] = before_q

        for g in range(N_GROUPS):
            cols = slice(g * LRU_GROUP, (g + 1) * LRU_GROUP)
            a_s[:, cols] = _dot(lxb_s[:, cols], wa_ref[g], NN)
            b_s[:, cols] = _dot(lxb_s[:, cols], wx_ref[g], NN)

        for r0 in range(0, tile, FFN_ROWS):
            _, ig, a, mult, _ = _gates_rows(a_s[rows_of(r0), :], b_s[rows_of(r0), :], par)
            a_s[rows_of(r0), :] = a
            b_s[rows_of(r0), :] = mult * (ig * lx_s[rows_of(r0), :])
        h_car[0:1, :] = _scan_rows(a_s, b_s, hs_ref, h_car[0:1, :], rows=tile, reverse=False)

        for r0 in range(0, tile, FFN_ROWS):
            y_ref[rows_of(r0), 0:D_LRU] = (hs_ref[rows_of(r0), :] * _gelu(z_ref[rows_of(r0), col_gate])).astype(BF16)

    full = lambda shape: pl.BlockSpec(shape, lambda i: (0,) * len(shape))
    return pl.pallas_call(
        body, name=name, grid=(n_tiles,),
        in_specs=[pl.BlockSpec((tile, D_IN), lambda i: (i, 0)),
                  full((4, D_LRU)), full((1, D_LRU)),
                  full((N_GROUPS, LRU_GROUP, LRU_GROUP)), full((N_GROUPS, LRU_GROUP, LRU_GROUP)),
                  full((1, D_LRU)), full((1, D_LRU)), full((1, D_LRU)), full((3, D_SC))],
        out_specs=[pl.BlockSpec((tile, D_MIX), lambda i: (i, 0)), pl.BlockSpec((tile, D_LRU), lambda i: (i, 0))],
        out_shape=[jax.ShapeDtypeStruct((s, D_MIX), BF16), jax.ShapeDtypeStruct((s, D_LRU), F32)],
        scratch_shapes=[pltpu.VMEM((N_PAR, FFN_ROWS, D_LRU), F32), pltpu.VMEM((3, FFN_ROWS, D_SC), F32),
                        pltpu.VMEM((tile, D_LRU), F32), pltpu.VMEM((tile, D_LRU), BF16),
                        pltpu.VMEM((tile, D_LRU), F32), pltpu.VMEM((tile, D_LRU), F32),
                        pltpu.VMEM((FFN_ROWS, D_LRU), F32), pltpu.VMEM((FFN_ROWS, D_SC), F32),
                        pltpu.VMEM((8, D_LRU), F32)],
        compiler_params=_params("arbitrary"),
    )(z, cw, cb.reshape(1, -1), wa_bd, wx_bd, ba.reshape(1, -1), bx.reshape(1, -1), lam.reshape(1, -1), scw)


def _fold8(x):
    return sum(x[q:q + 8] for q in range(0, x.shape[0], 8))


def _mixer_bwd_rows(z, hs, dy, cw, cb, wa_bd, wx_bd, ba, bx, lam, scw, *, tile, name, token=None):
    s = z.shape[0]
    n_tiles = s // tile
    per8 = tile // 8
    rows_of = lambda r0: slice(r0, r0 + FFN_ROWS)
    col_gate, col_sb, col_sc, col_sx = (slice(D_LRU, 2 * D_LRU), slice(2 * D_LRU, 2 * D_LRU + D_SC),
                                        slice(2 * D_LRU + D_SC, 2 * D_LRU + 2 * D_SC), slice(2 * D_LRU + 2 * D_SC, D_IN))
    up = range(0, tile, FFN_ROWS)
    down = range(tile - FFN_ROWS, -1, -FFN_ROWS)
    A_CB, A_BA, A_BX, A_SP, A_CW = 0, 1, 2, 3, 4

    def body(z_ref, zp_ref, hs_ref, hsp_ref, dy_ref, cw_ref, cb_ref, wa_ref, wx_ref, ba_ref, bx_ref, lam_ref, scw_ref,
             token_ref, dz_ref, dcw_ref, dvec_ref, dwa_ref, dwx_ref, dscw_ref,
             par, par_sc, lx_s, lxb_s, cq_s, pr_s, pi_s, r_s, ig_s, a_s, mult_s, inv_s, ash_s, b_s, lam_s,
             dlx_s, dpr_b, dpi_b, back_s, car_a, car_dlx, car_dcq, l_car, acc, acc_sc):
        i = pl.program_id(0)

        @pl.when(i == 0)
        def _():
            for ref in (dwa_ref, dwx_ref, l_car, car_a, car_dlx, car_dcq, acc, acc_sc):
                ref[...] = jnp.zeros_like(ref)
            _spread_mixer_params(par, par_sc, cw_ref, cb_ref, ba_ref, bx_ref, lam_ref, scw_ref)

        keep = jnp.where(i == n_tiles - 1, 0.0, 1.0)
        zeros8 = lambda n: jnp.zeros((8, n), F32)

        before_lx = jnp.concatenate([zeros8(D_LRU), zp_ref[:, 0:D_LRU] * keep], axis=0)
        before_q = jnp.concatenate([zeros8(D_SC), zp_ref[:, col_sc] * zp_ref[:, col_sx] * keep], axis=0)
        for r0 in up:
            cur = z_ref[rows_of(r0), 0:D_LRU]
            lx = par[P_CB] + par[3] * cur
            for k in range(3):
                lx = lx + par[k] * _rows_from(before_lx, cur, FFN_ROWS - 3 + k)
            lx_s[rows_of(r0), :] = lx
            lxb_s[rows_of(r0), :] = lx.astype(BF16)
            before_lx = cur
            q = z_ref[rows_of(r0), col_sc] * z_ref[rows_of(r0), col_sx]
            cq = par_sc[2] * q
            for k in range(2):
                cq = cq + par_sc[k] * _rows_from(before_q, q, FFN_ROWS - 2 + k)
            cq_s[rows_of(r0), :] = cq
            before_q = q

        for g in range(N_GROUPS):
            cols = slice(g * LRU_GROUP, (g + 1) * LRU_GROUP)
            pr_s[:, cols] = _dot(lxb_s[:, cols], wa_ref[g], NN)
            pi_s[:, cols] = _dot(lxb_s[:, cols], wx_ref[g], NN)

        after_a = car_a[...]
        for r0 in down:
            r, ig, a, mult, one_minus_a2 = _gates_rows(pr_s[rows_of(r0), :], pi_s[rows_of(r0), :], par)
            r_s[rows_of(r0), :] = r
            ig_s[rows_of(r0), :] = ig
            a_s[rows_of(r0), :] = a
            mult_s[rows_of(r0), :] = mult
            inv_s[rows_of(r0), :] = lax.rsqrt(one_minus_a2)
            ash_s[rows_of(r0), :] = _rows_from(a, after_a, 1)
            after_a = a
            ge, dge = _gelu_parts(z_ref[rows_of(r0), col_gate])
            dy_lru = dy_ref[rows_of(r0), 0:D_LRU].astype(F32)
            dz_ref[rows_of(r0), col_gate] = (dy_lru * hs_ref[rows_of(r0), :] * dge).astype(BF16)
            b_s[rows_of(r0), :] = dy_lru * ge
        car_a[...] = after_a
        l_car[0:1, :] = _scan_rows(ash_s, b_s, lam_s, l_car[0:1, :], rows=tile, reverse=True)

        before_h = jnp.concatenate([zeros8(D_LRU), hsp_ref[...] * keep], axis=0)
        for r0 in up:
            lv = lam_s[rows_of(r0), :]
            h_here = hs_ref[rows_of(r0), :]
            lx, r, ig, a = lx_s[rows_of(r0), :], r_s[rows_of(r0), :], ig_s[rows_of(r0), :], a_s[rows_of(r0), :]
            mult = mult_s[rows_of(r0), :]
            da = lv * _rows_from(before_h, h_here, FFN_ROWS - 1)
            before_h = h_here
            d_mult = lv * ig * lx
            d_i = lv * mult * lx
            dlx_s[rows_of(r0), :] = lv * mult * ig
            dlog_a = da * a - d_mult * (a * a) * inv_s[rows_of(r0), :]
            dpre_r = dlog_a * par[P_DECAY] * r * (1.0 - r)
            dpre_i = d_i * ig * (1.0 - ig)
            acc[A_BA] += _fold8(dpre_r)
            acc[A_BX] += _fold8(dpre_i)
            acc[A_SP] += _fold8(dlog_a * r)
            dpr_b[rows_of(r0), :] = dpre_r.astype(BF16)
            dpi_b[rows_of(r0), :] = dpre_i.astype(BF16)

        for g in range(N_GROUPS):
            cols = slice(g * LRU_GROUP, (g + 1) * LRU_GROUP)
            dwa_ref[g] += _dot(lxb_s[:, cols], dpr_b[:, cols], TN)
            dwx_ref[g] += _dot(lxb_s[:, cols], dpi_b[:, cols], TN)
            back_s[:, cols] = _dot(dpr_b[:, cols], wa_ref[g], NT) + _dot(dpi_b[:, cols], wx_ref[g], NT)

        after_dlx, after_dcq = car_dlx[...], car_dcq[...]
        for r0 in down:
            dlx = dlx_s[rows_of(r0), :] + back_s[rows_of(r0), :]
            lxp = z_ref[rows_of(r0), 0:D_LRU]
            acc[A_CB] += _fold8(dlx)
            acc[A_CW + 3] += _fold8(dlx * lxp)
            dlxp = par[3] * dlx
            for sh in range(1, 4):
                below = _rows_from(dlx, after_dlx, sh)
                dlxp = dlxp + par[3 - sh] * below
                acc[A_CW + 3 - sh] += _fold8(below * lxp)
            dz_ref[rows_of(r0), 0:D_LRU] = dlxp.astype(BF16)
            after_dlx = dlx

            dy_sc = dy_ref[rows_of(r0), D_LRU:D_MIX].astype(F32)
            sb, sc, sx = z_ref[rows_of(r0), col_sb], z_ref[rows_of(r0), col_sc], z_ref[rows_of(r0), col_sx]
            dz_ref[rows_of(r0), col_sb] = (dy_sc * cq_s[rows_of(r0), :]).astype(BF16)
            dcq = dy_sc * sb
            q = sc * sx
            acc_sc[2] += _fold8(dcq * q)
            dq = par_sc[2] * dcq
            for sh in range(1, 3):
                below = _rows_from(dcq, after_dcq, sh)
                dq = dq + par_sc[2 - sh] * below
                acc_sc[2 - sh] += _fold8(below * q)
            dz_ref[rows_of(r0), col_sc] = (dq * sx).astype(BF16)
            dz_ref[rows_of(r0), col_sx] = (dq * sc).astype(BF16)
            after_dcq = dcq
        car_dlx[...] = after_dlx
        car_dcq[...] = after_dcq

        @pl.when(i == n_tiles - 1)
        def _():
            total = lambda x: jnp.sum(x, axis=0, keepdims=True)
            dcw_ref[...] = jnp.zeros_like(dcw_ref)
            dvec_ref[...] = jnp.zeros_like(dvec_ref)
            dscw_ref[...] = jnp.zeros_like(dscw_ref)
            for k in range(4):
                dcw_ref[k:k + 1, :] = total(acc[A_CW + k])
            for k in range(3):
                dvec_ref[k:k + 1, :] = total(acc[k])
                dscw_ref[k:k + 1, :] = total(acc_sc[k])
            dvec_ref[3:4, :] = total(acc[A_SP]) * (-RG_C) * par[P_DSP][0:1, :]

    rev = lambda i: n_tiles - 1 - i
    prev8 = lambda i: jnp.maximum(rev(i) * per8 - 1, 0)
    full = lambda shape: pl.BlockSpec(shape, lambda i: (0,) * len(shape))
    wide = lambda rows, dt=F32: pltpu.VMEM((rows, D_LRU), dt)
    return pl.pallas_call(
        body, name=name, grid=(n_tiles,),
        in_specs=[pl.BlockSpec((tile, D_IN), lambda i: (rev(i), 0)),
                  pl.BlockSpec((HALO, D_IN), lambda i: (prev8(i), 0)),
                  pl.BlockSpec((tile, D_LRU), lambda i: (rev(i), 0)),
                  pl.BlockSpec((HALO, D_LRU), lambda i: (prev8(i), 0)),
                  pl.BlockSpec((tile, D_MIX), lambda i: (rev(i), 0)),
                  full((4, D_LRU)), full((1, D_LRU)),
                  full((N_GROUPS, LRU_GROUP, LRU_GROUP)), full((N_GROUPS, LRU_GROUP, LRU_GROUP)),
                  full((1, D_LRU)), full((1, D_LRU)), full((1, D_LRU)), full((3, D_SC)),
                  pl.BlockSpec(memory_space=pl.ANY)],
        out_specs=[pl.BlockSpec((tile, D_IN), lambda i: (rev(i), 0)),
                   full((8, D_LRU)), full((8, D_LRU)),
                   full((N_GROUPS, LRU_GROUP, LRU_GROUP)), full((N_GROUPS, LRU_GROUP, LRU_GROUP)),
                   full((8, D_SC))],
        out_shape=[jax.ShapeDtypeStruct((s, D_IN), BF16),
                   jax.ShapeDtypeStruct((8, D_LRU), F32), jax.ShapeDtypeStruct((8, D_LRU), F32),
                   jax.ShapeDtypeStruct((N_GROUPS, LRU_GROUP, LRU_GROUP), F32),
                   jax.ShapeDtypeStruct((N_GROUPS, LRU_GROUP, LRU_GROUP), F32),
                   jax.ShapeDtypeStruct((8, D_SC), F32)],
        scratch_shapes=[pltpu.VMEM((N_PAR, FFN_ROWS, D_LRU), F32), pltpu.VMEM((3, FFN_ROWS, D_SC), F32),
                        wide(tile), wide(tile, BF16), pltpu.VMEM((tile, D_SC), F32),
                        wide(tile), wide(tile), wide(tile), wide(tile), wide(tile), wide(tile), wide(tile),
                        wide(tile), wide(tile), wide(tile),
                        wide(tile), wide(tile, BF16), wide(tile, BF16), wide(tile),
                        wide(FFN_ROWS), wide(FFN_ROWS), pltpu.VMEM((FFN_ROWS, D_SC), F32), wide(8),
                        pltpu.VMEM((8, 8, D_LRU), F32), pltpu.VMEM((3, 8, D_SC), F32)],
        compiler_params=_params("arbitrary"),
    )(z, z, hs, hs, dy, cw, cb.reshape(1, -1), wa_bd, wx_bd, ba.reshape(1, -1), bx.reshape(1, -1),
      lam.reshape(1, -1), scw, _behind(token))


FFN_ROWS = 16
FFN_GROUPS = 2


def _spread_taps(fw_ref, taps):
    for half in range(2):
        for k in range(3):
            taps[half, k] = jnp.broadcast_to(fw_ref[half, k:k + 1, :], taps.shape[2:])


def _rows_from(first, second, start):
    stack = jnp.concatenate([first, second], axis=0)
    return pltpu.roll(stack, 2 * FFN_ROWS - start, 0)[0:FFN_ROWS]


def _conv3_rows(taps, ext_ref, half, row):
    before = ext_ref[half, row - FFN_ROWS:row, :]
    here = ext_ref[half, row:row + FFN_ROWS, :]
    acc = taps[half, 2] * here
    for k in range(2):
        acc = acc + taps[half, k] * _rows_from(before, here, FFN_ROWS - 2 + k)
    return acc


HALO_B = 16


def _ffn_block_fwd(x2, g2, w_up_b, fcw, w_down, *, tile, name):
    s = x2.shape[0]
    nb = w_up_b.shape[2]
    blocks = D_FF // nb
    per16 = tile // HALO_B

    def body(x2_ref, x2p_ref, g_ref, wg_ref, wu_ref, fw_ref, wd_ref, x3_ref, h_ref, act_ref, p_ref, u_ref,
             ext_p, acc_ref, taps, lhs):
        i = pl.program_id(0)
        j = pl.program_id(1)
        keep = jnp.where(i == 0, 0.0, 1.0)
        _spread_taps(fw_ref, taps)
        @pl.when(j == 0)
        def _():
            for rows_ref, at in ((x2p_ref, 0), (x2_ref, HALO_B)):
                xv = rows_ref[...]
                lhs[at:at + xv.shape[0], :] = (xv * _rms(xv) * g_ref[...]).astype(BF16)
            h_ref[...] = lhs[HALO_B:HALO_B + tile, :]

        grp = tile // FFN_GROUPS
        for g in range(FFN_GROUPS):
            new = slice(g * grp + (HALO_B if g else 0), (g + 1) * grp + HALO_B)
            for half, w_ref in ((0, wg_ref), (1, wu_ref)):
                pe = _dot(lhs[new, :], w_ref[...], NN)
                if g == 0:
                    ext_p[half, 0:HALO_B, :] = pe[0:HALO_B] * keep
                    ext_p[half, HALO_B:grp + HALO_B, :] = pe[HALO_B:]
                    p_ref[half, 0:grp, :] = pe[HALO_B:].astype(BF16)
                else:
                    ext_p[half, new, :] = pe
                    p_ref[half, g * grp:(g + 1) * grp, :] = pe.astype(BF16)
        for g in range(FFN_GROUPS):
            rows = slice(g * grp, (g + 1) * grp)
            acts = []
            for r0 in range(g * grp, (g + 1) * grp, FFN_ROWS):
                u = [_conv3_rows(taps, ext_p, half, HALO_B + r0) for half in range(2)]
                for half in range(2):
                    u_ref[half, r0:r0 + FFN_ROWS, :] = u[half].astype(BF16)
                acts.append((_gelu(u[0]) * u[1]).astype(BF16))
                act_ref[r0:r0 + FFN_ROWS, :] = acts[-1]
            contrib = _dot(jnp.concatenate(acts, axis=0), wd_ref[...], NN)
            acc_ref[rows, :] = contrib + jnp.where(j > 0, acc_ref[rows, :], 0.0)

        @pl.when(j == blocks - 1)
        def _():
            x3_ref[...] = x2_ref[...] + acc_ref[...]

    return pl.pallas_call(
        body, name=name, grid=(s // tile, blocks),
        in_specs=[pl.BlockSpec((tile, D_MODEL), lambda i, j: (i, 0)),
                  pl.BlockSpec((HALO_B, D_MODEL), lambda i, j: (jnp.maximum(i * per16 - 1, 0), 0)),
                  pl.BlockSpec((1, D_MODEL), lambda i, j: (0, 0)),
                  pl.BlockSpec((None, D_MODEL, nb), lambda i, j: (j, 0, 0)),
                  pl.BlockSpec((None, D_MODEL, nb), lambda i, j: (j + blocks, 0, 0)),
                  pl.BlockSpec((2, 3, nb), lambda i, j: (0, 0, j)),
                  pl.BlockSpec((nb, D_MODEL), lambda i, j: (j, 0))],
        out_specs=[pl.BlockSpec((tile, D_MODEL), lambda i, j: (i, 0)),
                   pl.BlockSpec((tile, D_MODEL), lambda i, j: (i, 0)),
                   pl.BlockSpec((tile, nb), lambda i, j: (i, j)),
                   pl.BlockSpec((2, tile, nb), lambda i, j: (0, i, j)),
                   pl.BlockSpec((2, tile, nb), lambda i, j: (0, i, j))],
        out_shape=[jax.ShapeDtypeStruct((s, D_MODEL), F32), jax.ShapeDtypeStruct((s, D_MODEL), BF16),
                   jax.ShapeDtypeStruct((s, D_FF), BF16),
                   jax.ShapeDtypeStruct((2, s, D_FF), BF16), jax.ShapeDtypeStruct((2, s, D_FF), BF16)],
        scratch_shapes=[pltpu.VMEM((2, tile + HALO_B, nb), F32), pltpu.VMEM((tile, D_MODEL), F32),
                        pltpu.VMEM((2, 3, FFN_ROWS, nb), F32), pltpu.VMEM((tile + HALO_B, D_MODEL), BF16)],
        compiler_params=_params("parallel", "arbitrary"),
    )(x2, x2, g2.reshape(1, -1), w_up_b, w_up_b, fcw, w_down)


def _ffn_block_bwd(dx3, dx3b, p, u, x2, g2, w_up_b, fcw, w_down, *, tile, name, token=None):
    s = x2.shape[0]
    nb = w_up_b.shape[2]
    blocks = D_FF // nb
    n_tiles = s // tile
    per16 = tile // HALO_B
    last16 = s // HALO_B - 1

    def body(dxb_ref, dxbn_ref, wd_ref, p_ref, u_ref, un_ref, fw_ref, wg_ref, wu_ref, x2_ref, g_ref, dx3_ref,
             token_ref, dx2_ref, dx2b_ref, dg_ref, dp_ref, dw_ref, da_s, acc_w, acc_dh, taps):
        i = pl.program_id(0)
        j = pl.program_id(1)

        @pl.when(jnp.logical_and(i == 0, j == 0))
        def _():
            acc_w[...] = jnp.zeros_like(acc_w)
            dg_ref[...] = jnp.zeros_like(dg_ref)

        keep_next = jnp.where(i == n_tiles - 1, 0.0, 1.0)
        _spread_taps(fw_ref, taps)
        lhs = jnp.concatenate([dxb_ref[...], dxbn_ref[...]], axis=0)
        grp = tile // FFN_GROUPS
        for g in reversed(range(FFN_GROUPS)):
            new = slice(g * grp, (g + 1) * grp + (HALO_B if g == FFN_GROUPS - 1 else 0))
            da_s[new, :] = _dot(lhs[new], wd_ref[...], NT)

        def du_rows(da, u_gate, u_up):
            ge, dge = _gelu_parts(u_gate)
            return da * u_up * dge, da * ge

        after = du_rows(da_s[tile:tile + HALO_B, :] * keep_next, un_ref[0].astype(F32), un_ref[1].astype(F32))
        for g in reversed(range(FFN_GROUPS)):
            rows = slice(g * grp, (g + 1) * grp)
            dps = ([], [])
            for r0 in range((g + 1) * grp - FFN_ROWS, g * grp - 1, -FFN_ROWS):
                du = du_rows(da_s[r0:r0 + FFN_ROWS, :], u_ref[0, r0:r0 + FFN_ROWS, :].astype(F32),
                             u_ref[1, r0:r0 + FFN_ROWS, :].astype(F32))
                for half in range(2):
                    below = [du[half], _rows_from(du[half], after[half], 1), _rows_from(du[half], after[half], 2)]
                    acc = taps[half, 2] * below[0]
                    for k in range(2):
                        acc = acc + taps[half, k] * below[2 - k]
                    dps[half].insert(0, acc.astype(BF16))
                    dp_ref[half, r0:r0 + FFN_ROWS, :] = dps[half][0]
                    p_rows = p_ref[half, r0:r0 + FFN_ROWS, :].astype(F32)
                    for k in range(3):
                        prod = below[2 - k] * p_rows
                        acc_w[j, half, k] += sum(prod[q:q + 8] for q in range(0, FFN_ROWS, 8))
                after = du
            contrib = (_dot(jnp.concatenate(dps[0], axis=0), wg_ref[...], NT)
                       + _dot(jnp.concatenate(dps[1], axis=0), wu_ref[...], NT))
            acc_dh[rows, :] = contrib + jnp.where(j > 0, acc_dh[rows, :], 0.0)

        @pl.when(j == blocks - 1)
        def _():
            dh = acc_dh[...]
            xv = x2_ref[...]
            rstd = _rms(xv)
            n = xv * rstd
            dn = dh * g_ref[...]
            dx = dx3_ref[...] + rstd * (dn - n * jnp.mean(dn * n, axis=-1, keepdims=True))
            dx2_ref[...] = dx
            dx2b_ref[...] = dx.astype(BF16)
            dg_ref[0:1, :] += jnp.sum(dh * n, axis=0, keepdims=True)

        @pl.when(jnp.logical_and(i == n_tiles - 1, j == blocks - 1))
        def _():
            dw_ref[...] = jnp.zeros_like(dw_ref)
            for jj in range(blocks):
                for half in range(2):
                    for k in range(3):
                        dw_ref[half, k:k + 1, jj * nb:(jj + 1) * nb] = jnp.sum(acc_w[jj, half, k], axis=0, keepdims=True)

    next16 = lambda i: jnp.minimum((i + 1) * per16, last16)
    return pl.pallas_call(
        body, name=name, grid=(n_tiles, blocks),
        in_specs=[pl.BlockSpec((tile, D_MODEL), lambda i, j: (i, 0)),
                  pl.BlockSpec((HALO_B, D_MODEL), lambda i, j: (next16(i), 0)),
                  pl.BlockSpec((nb, D_MODEL), lambda i, j: (j, 0)),
                  pl.BlockSpec((2, tile, nb), lambda i, j: (0, i, j)),
                  pl.BlockSpec((2, tile, nb), lambda i, j: (0, i, j)),
                  pl.BlockSpec((2, HALO_B, nb), lambda i, j: (0, next16(i), j)),
                  pl.BlockSpec((2, 3, nb), lambda i, j: (0, 0, j)),
                  pl.BlockSpec((None, D_MODEL, nb), lambda i, j: (j, 0, 0)),
                  pl.BlockSpec((None, D_MODEL, nb), lambda i, j: (j + blocks, 0, 0)),
                  pl.BlockSpec((tile, D_MODEL), lambda i, j: (i, 0)),
                  pl.BlockSpec((1, D_MODEL), lambda i, j: (0, 0)),
                  pl.BlockSpec((tile, D_MODEL), lambda i, j: (i, 0)),
                  pl.BlockSpec(memory_space=pl.ANY)],
        out_specs=[pl.BlockSpec((tile, D_MODEL), lambda i, j: (i, 0)),
                   pl.BlockSpec((tile, D_MODEL), lambda i, j: (i, 0)),
                   pl.BlockSpec((8, D_MODEL), lambda i, j: (0, 0)),
                   pl.BlockSpec((2, tile, nb), lambda i, j: (0, i, j)),
                   pl.BlockSpec((2, 8, D_FF), lambda i, j: (0, 0, 0))],
        out_shape=[jax.ShapeDtypeStruct((s, D_MODEL), F32), jax.ShapeDtypeStruct((s, D_MODEL), BF16),
                   jax.ShapeDtypeStruct((8, D_MODEL), F32), jax.ShapeDtypeStruct((2, s, D_FF), BF16),
                   jax.ShapeDtypeStruct((2, 8, D_FF), F32)],
        scratch_shapes=[pltpu.VMEM((tile + HALO_B, nb), F32), pltpu.VMEM((blocks, 2, 3, 8, nb), F32),
                        pltpu.VMEM((tile, D_MODEL), F32), pltpu.VMEM((2, 3, FFN_ROWS, nb), F32)],
        compiler_params=_params("arbitrary", "arbitrary"),
    )(dx3b, dx3b, w_down, p, u, u, fcw, w_up_b, w_up_b, x2, g2.reshape(1, -1), dx3, _behind(token))


def _adamw_math(w, g, m, v):
    m = ADAM_B1 * m + (1.0 - ADAM_B1) * g
    v = ADAM_B2 * v + (1.0 - ADAM_B2) * (g * g)
    m_hat = m / (1.0 - ADAM_B1 ** ADAM_STEP)
    v_hat = v / (1.0 - ADAM_B2 ** ADAM_STEP)
    delta = -ADAM_LR * (m_hat / (jnp.sqrt(v_hat) + ADAM_EPS) + ADAM_WD * w)
    return delta, m, v


def _adamw(w, g, m, v, *, name):
    rows, cols = w.shape
    tr = rows
    for cand in (512, 256, 128, 64, 32, 16, 8):
        if rows % cand == 0 and rows > cand:
            tr = cand
            break

    def body(w_ref, g_ref, m_ref, v_ref, d_ref, nm_ref, nv_ref):
        d, nm, nv = _adamw_math(w_ref[...], g_ref[...], m_ref[...], v_ref[...])
        d_ref[...] = d
        nm_ref[...] = nm
        nv_ref[...] = nv

    spec = pl.BlockSpec((tr, cols), lambda i: (i, 0))
    return pl.pallas_call(
        body, name=name, grid=(rows // tr,), in_specs=[spec] * 4, out_specs=[spec] * 3,
        out_shape=[jax.ShapeDtypeStruct((rows, cols), F32)] * 3,
        compiler_params=_params("parallel"),
    )(w, g, m, v)


def _sum_adamw(parts, w, m, v, *, name):
    depth, rows, cols = w.shape
    tr = rows
    for cand in (256, 128, 64):
        if rows % cand == 0 and rows > cand:
            tr = cand
            break

    def body(*refs):
        part_refs = refs[:depth]
        w_ref, m_ref, v_ref, g_ref, d_ref, nm_ref, nv_ref = refs[depth:]
        layer = pl.program_id(0)
        grad = None
        for k, p_ref in enumerate(part_refs):
            total = p_ref[0].astype(F32)
            for dev in range(1, N_DEV):
                total = total + p_ref[dev].astype(F32)
            grad = total if grad is None else jnp.where(layer == k, total, grad)
        d, nm, nv = _adamw_math(w_ref[...], grad, m_ref[...], v_ref[...])
        g_ref[...] = grad
        d_ref[...] = d
        nm_ref[...] = nm
        nv_ref[...] = nv

    part_spec = lambda k: pl.BlockSpec((N_DEV, tr, cols), lambda l, i: (0, jnp.where(l == k, i, 0), 0))
    spec = pl.BlockSpec((None, tr, cols), lambda l, i: (l, i, 0))
    return pl.pallas_call(
        body, name=name, grid=(depth, rows // tr),
        in_specs=[part_spec(k) for k in range(depth)] + [spec] * 3, out_specs=[spec] * 4,
        out_shape=[jax.ShapeDtypeStruct((depth, rows, cols), F32)] * 4,
        compiler_params=_params("parallel", "parallel"),
    )(*parts, w, m, v)


def _sum_parts(parts, *, name, tokens=()):
    _, rows, cols = parts.shape
    tr = rows
    for cand in (256, 128, 64, 32, 16):
        if rows % cand == 0 and rows > cand:
            tr = cand
            break

    def body(p_ref, *rest):
        acc = p_ref[0].astype(F32)
        for d in range(1, N_DEV):
            acc = acc + p_ref[d].astype(F32)
        rest[-1][...] = acc

    return pl.pallas_call(
        body, name=name, grid=(rows // tr,),
        in_specs=[pl.BlockSpec((N_DEV, tr, cols), lambda i: (0, i, 0))]
        + [pl.BlockSpec(memory_space=pl.ANY)] * len(tokens),
        out_specs=pl.BlockSpec((tr, cols), lambda i: (i, 0)),
        out_shape=jax.ShapeDtypeStruct((rows, cols), F32),
        compiler_params=_params("parallel"),
    )(parts, *tokens)


def _place():
    return lax.axis_index("x"), lax.axis_index("y"), lax.axis_index("c")


def _flip(v, bit):
    return 1 - v if bit else v


N_PEERS = N_DEV - 1


def _peer_copy(k, src_ref, land_ref, send_sem, recv_sem, gather):
    x, y, c = _place()
    my_id = 4 * x + 2 * y + c
    px, py, pc = _flip(x, k & 4), _flip(y, k & 2), _flip(c, k & 1)
    peer_id = 4 * px + 2 * py + pc
    return pltpu.make_async_remote_copy(
        src_ref=src_ref if gather else src_ref.at[peer_id], dst_ref=land_ref.at[my_id],
        send_sem=send_sem.at[k - 1], recv_sem=recv_sem.at[k - 1],
        device_id=(px, py, pc), device_id_type=MESH)


def _sequencer_copies(srcs, *, gather, name, collective_id, after):
    n = len(srcs)
    hbm = pltpu.MemorySpace.HBM
    src_refs = [jax.new_ref(s, memory_space=hbm) for s in srcs]
    land_refs = [jax.empty_ref(jax.ShapeDtypeStruct(((N_DEV,) + s.shape) if gather else s.shape, s.dtype),
                               memory_space=hbm) for s in srcs]
    token_in = jax.new_ref(jnp.zeros((8, 128), F32) if after is None else after, memory_space=hbm)
    token_out = jax.empty_ref(jax.ShapeDtypeStruct((8, 128), F32), memory_space=hbm)

    @pl.kernel(mesh=plsc.ScalarSubcoreMesh(axis_name="seq", num_cores=1), name=name,
               scratch_types=(pltpu.SemaphoreType.DMA((n, N_PEERS)), pltpu.SemaphoreType.DMA((n, N_PEERS)),
                              pltpu.SemaphoreType.DMA((n + 1,))),
               compiler_params=pltpu.CompilerParams(collective_id=collective_id))
    def launch(send_sems, recv_sems, local_sems):
        x, y, c = _place()
        my_id = 4 * x + 2 * y + c
        barrier = pltpu.get_barrier_semaphore()
        own = [pltpu.make_async_copy(src_refs[t] if gather else src_refs[t].at[my_id], land_refs[t].at[my_id],
                                     local_sems.at[t]) for t in range(n)]
        if gather:
            sibling = (x, y, 1 - c)
            chips = [(1 - x, y), (x, 1 - y), (1 - x, 1 - y)]
            for peer in [sibling] + [(*chip, c) for chip in chips]:
                pl.semaphore_signal(barrier, inc=1, device_id=peer, device_id_type=MESH)
            pl.semaphore_wait(barrier, 4)

            def copy(t, k, block, to, src=None):
                dst = land_refs[t].at[4 * block[0] + 2 * block[1] + block[2]]
                return pltpu.make_async_remote_copy(
                    src_ref=dst if src is None else src, dst_ref=dst,
                    send_sem=send_sems.at[t, k], recv_sem=recv_sems.at[t, k], device_id=to, device_id_type=MESH)

            for cp in own:
                cp.start()
            sends = []
            for t in range(n):
                sends.append(copy(t, 0, (x, y, c), sibling, src=src_refs[t]))
                sends += [copy(t, 1 + j, (x, y, c), (*chip, c), src=src_refs[t]) for j, chip in enumerate(chips)]
            for cp in sends:
                cp.start()
            for t in range(n):
                for j, chip in enumerate(chips):
                    copy(t, 1 + j, (*chip, c), (x, y, c)).wait_recv()
                    passed_on = copy(t, 4 + j, (*chip, c), sibling)
                    passed_on.start()
                    sends.append(passed_on)
            for t in range(n):
                copy(t, 0, sibling, (x, y, c)).wait_recv()
                for j, chip in enumerate(chips):
                    copy(t, 4 + j, (*chip, 1 - c), (x, y, c)).wait_recv()
            for cp in sends:
                cp.wait_send()
            for cp in own:
                cp.wait()
        else:
            for k in range(1, N_DEV):
                peer = (_flip(x, k & 4), _flip(y, k & 2), _flip(c, k & 1))
                pl.semaphore_signal(barrier, inc=1, device_id=peer, device_id_type=MESH)
            pl.semaphore_wait(barrier, N_PEERS)
            for cp in own:
                cp.start()
            copies = [_peer_copy(k, src_refs[t], land_refs[t], send_sems.at[t], recv_sems.at[t], gather)
                      for t in range(n) for k in range(1, N_DEV)]
            for cp in copies:
                cp.start()
            for cp in own:
                cp.wait()
            for cp in copies:
                cp.wait()
        passed = pltpu.make_async_copy(token_in, token_out, local_sems.at[n])
        passed.start()
        passed.wait()

    launch()
    return [ref[...] for ref in land_refs], token_out[...]


TM = 512
TMM = 1024
TKW = 2048
MIX_TILE = 256
FFN_FWD_TILE = 1024
FFN_BWD_TILE = 512


def _block_diag(w):
    wg = w.reshape(N_GROUPS, HEADS_PER_GROUP, LRU_HEAD_DIM, LRU_HEAD_DIM)
    eye = jnp.eye(HEADS_PER_GROUP, dtype=w.dtype)
    bd = wg[:, :, :, None, :] * eye[None, :, None, :, None]
    return bd.reshape(N_GROUPS, LRU_GROUP, LRU_GROUP).astype(BF16)


def _head_blocks(bd):
    b5 = bd.reshape(N_GROUPS, HEADS_PER_GROUP, LRU_HEAD_DIM, HEADS_PER_GROUP, LRU_HEAD_DIM)
    blocks = [b5[:, h, :, h, :] for h in range(HEADS_PER_GROUP)]
    return jnp.stack(blocks, axis=1).reshape(LRU_HEADS, LRU_HEAD_DIM, LRU_HEAD_DIM)


def _w(lw, key, after):
    value = lw[key]
    return value(after) if callable(value) else value


def _layer_fwd(x, lw, tag):
    sv_rows = x.shape[0]
    z, h1 = _norm_in_proj(x, lw["g1"], _w(lw, "w_in_t", x), tm=min(2 * TMM, sv_rows), tn=896, name=f"in_proj_{tag}")
    y_mix, hs = _mixer_fwd(z, _w(lw, "cw", z), lw["cb"], lw["wa_bd"], lw["wx_bd"], lw["ba"], lw["bx"], lw["lam"],
                           _w(lw, "scw", z), tile=MIX_TILE, name=f"mixer_fwd_{tag}")
    x2 = _mm_nn(y_mix, _w(lw, "w_out", y_mix), tm=min(2 * TMM, sv_rows), tn=D_MODEL, tk=D_MIX, out_dtype=F32, name=f"out_proj_{tag}",
                residual=x)
    x3, h2, act, p, u = _ffn_block_fwd(x2, lw["g2"], _w(lw, "w_up_b", x2), _w(lw, "fcw", x2), _w(lw, "w_down", x2),
                                       tile=min(FFN_FWD_TILE, sv_rows), name=f"ffn_fwd_{tag}")
    saved = dict(x=x, h1=h1, z=z, y_mix=y_mix, hs=hs, x2=x2, h2=h2, p=p, u=u, act=act)
    return x3, saved


def _layer_bwd(dx3, dx3b, lw, sv, tag, put):
    sv_rows = dx3.shape[0]
    w_in_t, w_out, w_up_b, w_down = (_w(lw, k, dx3) for k in ("w_in_t", "w_out", "w_up_b", "w_down"))
    cw, scw, fcw = (_w(lw, k, dx3) for k in ("cw", "scw", "fcw"))
    g_down = _mm_tn(sv["act"], dx3b, tm=1024, tn=D_MODEL, tk=min(TKW, sv_rows), out_dtype=BF16, name=f"down_bwd_w_{tag}")
    dx2, dx2b, dg2, dp, dfcw = _ffn_block_bwd(dx3, dx3b, sv["p"], sv["u"], sv["x2"], lw["g2"], w_up_b, fcw, w_down,
                                              tile=min(FFN_BWD_TILE, sv_rows), name=f"ffn_bwd_{tag}",
                                              token=put("w_down", g_down))
    g_up = _mm_up_bwd_w(sv["h2"], dp, tm=D_MODEL, tk=min(TKW, sv_rows), name=f"up_bwd_w_{tag}")
    dy = _mm_nt(dx2b, w_out, tm=min(TMM, sv_rows), tn=768, tk=D_MODEL, out_dtype=BF16, name=f"out_bwd_x_{tag}")
    dz, dcw, dvec, dwa, dwx, dscw = _mixer_bwd_rows(
        sv["z"], sv["hs"], dy, cw, lw["cb"], lw["wa_bd"], lw["wx_bd"], lw["ba"], lw["bx"], lw["lam"],
        scw, tile=MIX_TILE, name=f"mixer_bwd_{tag}", token=put("w_up_b", g_up))
    g_out = _mm_tn(sv["y_mix"], dx2b, tm=768, tn=D_MODEL, tk=min(TKW, sv_rows), out_dtype=BF16, name=f"out_bwd_w_{tag}",
                   token=dz)
    g_in_t = _mm_tn(dz, sv["h1"], tm=896, tn=D_MODEL, tk=min(TKW, sv_rows), out_dtype=BF16, name=f"in_bwd_w_{tag}",
                    token=put("w_out", g_out))
    dx, dxb, dg1 = _in_bwd_norm(dz, w_in_t, sv["x"], lw["g1"], dx2, tm=min(TMM, sv_rows), tk=896, name=f"in_bwd_x_{tag}",
                                token=put("w_in_t", g_in_t))
    small = dict(norm1_g=dg1[0], lru_conv_w=dcw[0:4], lru_conv_b=dvec[0], lru_wa=_head_blocks(dwa),
                 lru_ba=dvec[1], lru_wx=_head_blocks(dwx), lru_bx=dvec[2], lru_lambda=dvec[3],
                 sc_conv_w=dscw[0:3], norm2_g=dg2[0], ffn_conv_w=dfcw[:, 0:3, :])
    return dx, dxb, small


SMALL_ORDER = ("norm1_g", "lru_conv_w", "lru_conv_b", "lru_wa", "lru_ba", "lru_wx", "lru_bx", "lru_lambda",
               "sc_conv_w", "norm2_g", "ffn_conv_w")


def _local_step(x, tgt, layers, final_g, put):
    saved = []
    h = x
    for l in range(DEPTH):
        h, sv = _layer_fwd(h, layers[l], f"l{l}")
        saved.append(sv)
    loss_blk, dx, dxb, dgf = _loss_head(h, final_g, tgt, tm=TM, name="loss_head")
    smalls = [None] * DEPTH
    for l in reversed(range(DEPTH)):
        dx, dxb, smalls[l] = _layer_bwd(dx, dxb, layers[l], saved[l], f"l{l}", functools.partial(put, l))
    return loss_blk[0, 0], dx, smalls, dgf[0]


def kernel(x, norm1_g, w_in, lru_conv_w, lru_conv_b, lru_wa, lru_ba, lru_wx, lru_bx, lru_lambda, sc_conv_w, w_out, norm2_g, w_up, ffn_conv_w, w_down, final_g, loss_target, m_norm1_g, m_w_in, m_lru_conv_w, m_lru_conv_b, m_lru_wa, m_lru_ba, m_lru_wx, m_lru_bx, m_lru_lambda, m_sc_conv_w, m_w_out, m_norm2_g, m_w_up, m_ffn_conv_w, m_w_down, m_final_g, v_norm1_g, v_w_in, v_lru_conv_w, v_lru_conv_b, v_lru_wa, v_lru_ba, v_lru_wx, v_lru_bx, v_lru_lambda, v_sc_conv_w, v_w_out, v_norm2_g, v_w_up, v_ffn_conv_w, v_w_down, v_final_g):
    names = ["norm1_g", "w_in", "lru_conv_w", "lru_conv_b", "lru_wa", "lru_ba", "lru_wx", "lru_bx", "lru_lambda",
             "sc_conv_w", "w_out", "norm2_g", "w_up", "ffn_conv_w", "w_down", "final_g"]
    w = dict(zip(names, [norm1_g, w_in, lru_conv_w, lru_conv_b, lru_wa, lru_ba, lru_wx, lru_bx, lru_lambda,
                         sc_conv_w, w_out, norm2_g, w_up, ffn_conv_w, w_down, final_g]))
    m = dict(zip(names, [m_norm1_g, m_w_in, m_lru_conv_w, m_lru_conv_b, m_lru_wa, m_lru_ba, m_lru_wx, m_lru_bx,
                         m_lru_lambda, m_sc_conv_w, m_w_out, m_norm2_g, m_w_up, m_ffn_conv_w, m_w_down, m_final_g]))
    v = dict(zip(names, [v_norm1_g, v_w_in, v_lru_conv_w, v_lru_conv_b, v_lru_wa, v_lru_ba, v_lru_wx, v_lru_bx,
                         v_lru_lambda, v_sc_conv_w, v_w_out, v_norm2_g, v_w_up, v_ffn_conv_w, v_w_down, v_final_g]))
    my_id = 4 * lax.axis_index("x") + 2 * lax.axis_index("y") + lax.axis_index("c")

    taps = jnp.zeros((DEPTH, 16, 768), F32)
    taps = taps.at[:, 0:4, 0:128].set(lru_conv_w).at[:, 4:7, 0:64].set(sc_conv_w).at[:, 8:11, :].set(ffn_conv_w)
    shards = {}
    for l in range(DEPTH):
        shards[f"w_in_t{l}"] = jnp.swapaxes(w_in[l], 0, 1).astype(BF16)
        if l == 0:
            shards["taps"] = taps.reshape(DEPTH * 16, 768)
        shards[f"w_out{l}"] = w_out[l].astype(BF16)
        shards[f"w_up_b{l}"] = w_up[l].astype(BF16)
        shards[f"w_down{l}"] = w_down[l].astype(BF16)
    ids = iter(range(18))
    got = {}
    chain = [None]
    for group in (("w_in_t0", "taps"), ("w_out0",), ("w_up_b0",), ("w_down0",),
                  ("w_in_t1",), ("w_out1",), ("w_up_b1",), ("w_down1",)):
        lands, chain[0] = _sequencer_copies([shards[k] for k in group], gather=True, name=f"gather_{group[0]}",
                                            collective_id=next(ids), after=None)
        got.update(zip(group, lands))

    def fetch(key, after):
        return got[key]

    def tap_rows(l, lo, hi, width, after):
        tl = fetch("taps", after).reshape(N_DEV, DEPTH, 16, 768)[:, l, lo:hi, 0:width]
        return jnp.transpose(tl, (1, 0, 2)).reshape(hi - lo, N_DEV * width)

    layers = []
    for l in range(DEPTH):
        layers.append(dict(
            g1=norm1_g[l], g2=norm2_g[l], cb=lru_conv_b[l], ba=lru_ba[l], bx=lru_bx[l], lam=lru_lambda[l],
            wa_bd=_block_diag(lru_wa[l]), wx_bd=_block_diag(lru_wx[l]),
            cw=functools.partial(tap_rows, l, 0, 4, 128), scw=functools.partial(tap_rows, l, 4, 7, 64),
            fcw=lambda after, l=l: tap_rows(l, 8, 11, 768, after).reshape(3, 2, D_FF).transpose(1, 0, 2),
            w_in_t=lambda after, l=l: fetch(f"w_in_t{l}", after).reshape(D_IN, D_MODEL),
            w_out=lambda after, l=l: fetch(f"w_out{l}", after).reshape(D_MIX, D_MODEL),
            w_up_b=lambda after, l=l: fetch(f"w_up_b{l}", after),
            w_down=lambda after, l=l: fetch(f"w_down{l}", after).reshape(D_FF, D_MODEL)))

    scatter_handles = {}

    def put(l, key, grad):
        blocks = grad if grad.ndim == 3 else grad.reshape(N_DEV, grad.shape[0] // N_DEV, grad.shape[1])
        (scatter_handles[(l, key)],), chain[0] = _sequencer_copies(
            [blocks], gather=False, name=f"scatter_{key}{l}", collective_id=next(ids), after=chain[0])
        return blocks

    loss_local, dx, smalls, dgf = _local_step(x[0], loss_target[0], layers, final_g, put)

    parts = []
    for l in range(DEPTH):
        for key in ("w_in_t", "w_out", "w_up_b", "w_down"):
            parts.append(scatter_handles[(l, key)])

    flat = [smalls[l][k].reshape(-1) for l in range(DEPTH) for k in SMALL_ORDER] + [dgf.reshape(-1)]
    flat.append(jnp.broadcast_to(loss_local, (128,)))
    sizes = [f.shape[0] for f in flat]
    total = sum(sizes)
    rows = -(-total // (N_DEV * 128 * 8)) * 8
    flat.append(jnp.zeros((N_DEV * rows * 128 - total,), F32))
    (small_parts,), chain[0] = _sequencer_copies([jnp.concatenate(flat).reshape(N_DEV, rows, 128)], gather=False,
                                                 name="scatter_small", collective_id=next(ids), after=chain[0])
    grads, deltas, new_m, new_v = {}, {}, {}, {}
    for slot, k in ((2, "w_up"), (3, "w_down"), (1, "w_out")):
        grads[k], deltas[k], new_m[k], new_v[k] = _sum_adamw(
            [parts[4 * l + slot] for l in range(DEPTH)], w[k], m[k], v[k], name=f"adamw_{k}")
    small_mine = _sum_parts(small_parts, name="sum_small", tokens=(deltas["w_up"], deltas["w_down"]))
    (small_all,), _ = _sequencer_copies([small_mine], gather=True, name="gather_small",
                                        collective_id=next(ids), after=chain[0])
    grads["w_in"] = jnp.stack([jnp.swapaxes(_sum_parts(parts[4 * l], name=f"sum_w_in_l{l}"), 0, 1)
                               for l in range(DEPTH)])
    small_sum = small_all.reshape(-1)
    small_g, off = [], 0
    for sz in sizes:
        small_g.append(small_sum[off:off + sz])
        off += sz
    gs = {}
    for l in range(DEPTH):
        for i, k in enumerate(SMALL_ORDER):
            gs.setdefault(k, []).append(small_g[l * len(SMALL_ORDER) + i])
    g_final = small_g[-2]
    loss = small_g[-1][0]

    for k in ("norm1_g", "lru_conv_b", "lru_ba", "lru_bx", "lru_lambda", "norm2_g"):
        grads[k] = jnp.stack(gs[k]).reshape(DEPTH, -1)
    for k in ("lru_wa", "lru_wx"):
        grads[k] = jnp.stack(gs[k]).reshape(DEPTH, LRU_HEADS, LRU_HEAD_DIM, LRU_HEAD_DIM)
    grads["final_g"] = g_final
    cw_full = jnp.stack(gs["lru_conv_w"]).reshape(DEPTH, 4, N_DEV, 128)
    grads["lru_conv_w"] = lax.dynamic_index_in_dim(cw_full, my_id, axis=2, keepdims=False)
    scw_full = jnp.stack(gs["sc_conv_w"]).reshape(DEPTH, 3, N_DEV, 64)
    grads["sc_conv_w"] = lax.dynamic_index_in_dim(scw_full, my_id, axis=2, keepdims=False)
    fcw_full = jnp.stack(gs["ffn_conv_w"]).reshape(DEPTH, 2, 3, D_FF).transpose(0, 2, 1, 3).reshape(DEPTH, 3, N_DEV, 768)
    grads["ffn_conv_w"] = lax.dynamic_index_in_dim(fcw_full, my_id, axis=2, keepdims=False)

    for k in names:
        if k in deltas:
            continue
        shape = w[k].shape
        cols = shape[-1]
        as2d = lambda a: a.reshape(-1, cols)
        d, nm, nv = _adamw(as2d(w[k]), as2d(grads[k]), as2d(m[k]), as2d(v[k]), name=f"adamw_{k}")
        deltas[k], new_m[k], new_v[k] = d.reshape(shape), nm.reshape(shape), nv.reshape(shape)

    return (loss, dx[None], *[grads[k] for k in names], *[deltas[k] for k in names],
            *[new_m[k] for k in names], *[new_v[k] for k in names])
```

```python
import functools
import math

import jax
import jax.numpy as jnp
from jax import lax
from jax.experimental import pallas as pl
from jax.experimental.pallas import tpu as pltpu
from jax.experimental.pallas import tpu_sc as plsc

F32 = jnp.float32
BF16 = jnp.bfloat16

N_DEV = 8
DEPTH = 2
D_MODEL = 1024
D_LRU = 1024
D_SC = 512
D_MIX = D_LRU + D_SC
D_IN = 2 * D_LRU + 3 * D_SC
D_FF = 3072
LRU_HEADS = 16
LRU_HEAD_DIM = 64
LRU_GROUP = 256
N_GROUPS = D_LRU // LRU_GROUP
HEADS_PER_GROUP = LRU_GROUP // LRU_HEAD_DIM
RG_C = 8.0
EPS = 1e-6
HALO = 8

ADAM_LR = 0.001
ADAM_B1 = 0.9
ADAM_B2 = 0.999
ADAM_EPS = 1e-08
ADAM_WD = 0.01
ADAM_STEP = 10

GELU_C = math.sqrt(2.0 / math.pi)
GELU_A = 0.044715

VMEM_LIMIT = 56 * 1024 * 1024
MESH = pl.DeviceIdType.MESH


def _params(*sem):
    return pltpu.CompilerParams(dimension_semantics=tuple(sem) if sem else None,
                                vmem_limit_bytes=VMEM_LIMIT)


def _gelu_parts(x):
    x2 = x * x
    t = jnp.tanh(GELU_C * (x + GELU_A * x * x2))
    half = 0.5 * (1.0 + t)
    g = x * half
    dg = half + 0.5 * x * (1.0 - t * t) * (GELU_C * (1.0 + 3.0 * GELU_A * x2))
    return g, dg


def _gelu(x):
    t = jnp.tanh(GELU_C * (x + GELU_A * x * x * x))
    return 0.5 * x * (1.0 + t)


def _sigmoid(x):
    return 0.5 * jnp.tanh(0.5 * x) + 0.5


def _softplus(x):
    e = jnp.exp(-jnp.abs(x))
    u = 1.0 + e
    log1p_e = jnp.where(u == 1.0, e, jnp.log(u) * (e / (u - 1.0)))
    return jnp.maximum(x, 0.0) + log1p_e


def _rms(x):
    ms = jnp.mean(x * x, axis=-1, keepdims=True)
    return lax.rsqrt(ms + EPS)


def _dot(a, b, dims):
    return lax.dot_general(a, b, (dims, ((), ())), preferred_element_type=F32)


NN = ((1,), (0,))
NT = ((1,), (1,))
TN = ((0,), (0,))


def _matmul(a, b, *, dims, grid, a_spec, b_spec, o_spec, out_shape, acc_shape, name,
            residual=None, r_spec=None, token=None):
    nk = grid[2]

    def body(*refs):
        a_ref, b_ref = refs[0], refs[1]
        r_ref = refs[2] if residual is not None else None
        o_ref = refs[2 + (residual is not None) + (token is not None)]
        prod = _dot(a_ref[...].astype(BF16), b_ref[...].astype(BF16), dims)

        def finish(total):
            if r_ref is not None:
                total = total + r_ref[...]
            o_ref[...] = total.astype(o_ref.dtype)

        if nk == 1:
            finish(prod)
            return
        acc_ref = refs[-1]
        k = pl.program_id(2)

        @pl.when(k == 0)
        def _():
            acc_ref[...] = prod

        @pl.when(jnp.logical_and(k > 0, k < nk - 1))
        def _():
            acc_ref[...] += prod

        @pl.when(k == nk - 1)
        def _():
            finish(acc_ref[...] + prod)

    in_specs = [a_spec, b_spec]
    args = [a, b]
    if residual is not None:
        in_specs.append(r_spec)
        args.append(residual)
    if token is not None:
        in_specs.append(pl.BlockSpec(memory_space=pl.ANY))
        args.append(token)
    return pl.pallas_call(
        body, name=name, grid=grid, in_specs=in_specs, out_specs=o_spec, out_shape=out_shape,
        scratch_shapes=[pltpu.VMEM(acc_shape, F32)] if nk > 1 else [],
        compiler_params=_params("parallel", "parallel", "arbitrary"),
    )(*args)


def _mm_nn(a, b, *, tm, tn, tk, out_dtype, name, residual=None, token=None):
    m, kd = a.shape
    n = b.shape[1]
    return _matmul(
        a, b, dims=NN, grid=(m // tm, n // tn, kd // tk),
        a_spec=pl.BlockSpec((tm, tk), lambda i, j, k: (i, k)),
        b_spec=pl.BlockSpec((tk, tn), lambda i, j, k: (k, j)),
        o_spec=pl.BlockSpec((tm, tn), lambda i, j, k: (i, j)),
        out_shape=jax.ShapeDtypeStruct((m, n), out_dtype), acc_shape=(tm, tn), name=name,
        residual=residual, r_spec=pl.BlockSpec((tm, tn), lambda i, j, k: (i, j)), token=token)


def _mm_nt(a, b, *, tm, tn, tk, out_dtype, name):
    m, kd = a.shape
    n = b.shape[0]
    return _matmul(
        a, b, dims=NT, grid=(m // tm, n // tn, kd // tk),
        a_spec=pl.BlockSpec((tm, tk), lambda i, j, k: (i, k)),
        b_spec=pl.BlockSpec((tn, tk), lambda i, j, k: (j, k)),
        o_spec=pl.BlockSpec((tm, tn), lambda i, j, k: (i, j)),
        out_shape=jax.ShapeDtypeStruct((m, n), out_dtype), acc_shape=(tm, tn), name=name)


def _mm_tn(a, b, *, tm, tn, tk, out_dtype, name, token=None):
    kd, m = a.shape
    n = b.shape[1]
    return _matmul(
        a, b, dims=TN, grid=(m // tm, n // tn, kd // tk),
        a_spec=pl.BlockSpec((tk, tm), lambda i, j, k: (k, i)),
        b_spec=pl.BlockSpec((tk, tn), lambda i, j, k: (k, j)),
        o_spec=pl.BlockSpec((tm, tn), lambda i, j, k: (i, j)),
        out_shape=jax.ShapeDtypeStruct((m, n), out_dtype), acc_shape=(tm, tn), name=name, token=token)


def _mm_up_bwd_w(h2, dp, *, tm, tk, name):
    s = h2.shape[0]
    nb = D_FF * 2 // N_DEV
    per_half = D_FF // nb
    return _matmul(
        h2, dp, dims=TN, grid=(D_MODEL // tm, N_DEV, s // tk),
        a_spec=pl.BlockSpec((tk, tm), lambda i, j, k: (k, i)),
        b_spec=pl.BlockSpec((None, tk, nb), lambda i, j, k: (j // per_half, k, j % per_half)),
        o_spec=pl.BlockSpec((None, tm, nb), lambda i, j, k: (j, i, 0)),
        out_shape=jax.ShapeDtypeStruct((N_DEV, D_MODEL, nb), BF16), acc_shape=(tm, nb), name=name)


def _behind(token):
    return jnp.zeros((8, 128), F32) if token is None else token


def _norm_in_proj(x, g, w_in_t, *, tm, tn, name):
    s, d = x.shape
    n = w_in_t.shape[0]

    def body(x_ref, g_ref, w_ref, z_ref, h_ref):
        @pl.when(pl.program_id(1) == 0)
        def _():
            xv = x_ref[...]
            h_ref[...] = (xv * _rms(xv) * g_ref[...]).astype(BF16)

        z_ref[...] = _dot(h_ref[...], w_ref[...], NT)

    return pl.pallas_call(
        body, name=name, grid=(s // tm, n // tn),
        in_specs=[pl.BlockSpec((tm, d), lambda i, j: (i, 0)), pl.BlockSpec((1, d), lambda i, j: (0, 0)),
                  pl.BlockSpec((tn, d), lambda i, j: (j, 0))],
        out_specs=[pl.BlockSpec((tm, tn), lambda i, j: (i, j)), pl.BlockSpec((tm, d), lambda i, j: (i, 0))],
        out_shape=[jax.ShapeDtypeStruct((s, n), F32), jax.ShapeDtypeStruct((s, d), BF16)],
        compiler_params=_params("parallel", "arbitrary"),
    )(x, g.reshape(1, d), w_in_t)


def _in_bwd_norm(dz, w_in_t, x, g, dres, *, tm, tk, name, token=None):
    s, kd = dz.shape
    d = w_in_t.shape[1]
    nk = kd // tk

    def body(dz_ref, w_ref, x_ref, g_ref, dres_ref, token_ref, dx_ref, dxb_ref, dg_ref, acc_ref):
        i = pl.program_id(0)
        k = pl.program_id(1)

        @pl.when(jnp.logical_and(i == 0, k == 0))
        def _():
            dg_ref[...] = jnp.zeros_like(dg_ref)

        acc_ref[...] = _dot(dz_ref[...], w_ref[...], NN) + jnp.where(k > 0, acc_ref[...], 0.0)

        @pl.when(k == nk - 1)
        def _():
            dh = acc_ref[...]
            xv = x_ref[...]
            rstd = _rms(xv)
            n = xv * rstd
            dn = dh * g_ref[...]
            dx = dres_ref[...] + rstd * (dn - n * jnp.mean(dn * n, axis=-1, keepdims=True))
            dx_ref[...] = dx
            dxb_ref[...] = dx.astype(BF16)
            dg_ref[0:1, :] += jnp.sum(dh * n, axis=0, keepdims=True)

    row = pl.BlockSpec((tm, d), lambda i, k: (i, 0))
    return pl.pallas_call(
        body, name=name, grid=(s // tm, nk),
        in_specs=[pl.BlockSpec((tm, tk), lambda i, k: (i, k)), pl.BlockSpec((tk, d), lambda i, k: (k, 0)),
                  row, pl.BlockSpec((1, d), lambda i, k: (0, 0)), row, pl.BlockSpec(memory_space=pl.ANY)],
        out_specs=[row, row, pl.BlockSpec((8, d), lambda i, k: (0, 0))],
        out_shape=[jax.ShapeDtypeStruct((s, d), F32), jax.ShapeDtypeStruct((s, d), BF16),
                   jax.ShapeDtypeStruct((8, d), F32)],
        scratch_shapes=[pltpu.VMEM((tm, d), F32)],
        compiler_params=_params("arbitrary", "arbitrary"),
    )(dz, w_in_t, x, g.reshape(1, d), dres, _behind(token))


def _loss_head(x, g, tgt, *, tm, name):
    s, d = x.shape

    def body(x_ref, g_ref, t_ref, loss_ref, dx_ref, dxb_ref, dg_ref):
        @pl.when(pl.program_id(0) == 0)
        def _():
            dg_ref[...] = jnp.zeros_like(dg_ref)
            loss_ref[...] = jnp.zeros_like(loss_ref)

        xv = x_ref[...]
        gv = g_ref[...]
        rstd = _rms(xv)
        n = xv * rstd
        e = n * gv - t_ref[...]
        part = 0.5 * jnp.sum(jnp.mean(e * e, axis=-1, keepdims=True), axis=0, keepdims=True)
        loss_ref[...] += jnp.broadcast_to(part, loss_ref.shape)
        dy = e * (1.0 / d)
        dn = dy * gv
        dx = rstd * (dn - n * jnp.mean(dn * n, axis=-1, keepdims=True))
        dx_ref[...] = dx
        dxb_ref[...] = dx.astype(BF16)
        dg_ref[0:1, :] += jnp.sum(dy * n, axis=0, keepdims=True)

    return pl.pallas_call(
        body, name=name, grid=(s // tm,),
        in_specs=[pl.BlockSpec((tm, d), lambda i: (i, 0)), pl.BlockSpec((1, d), lambda i: (0, 0)),
                  pl.BlockSpec((tm, d), lambda i: (i, 0))],
        out_specs=[pl.BlockSpec((8, 128), lambda i: (0, 0)), pl.BlockSpec((tm, d), lambda i: (i, 0)),
                   pl.BlockSpec((tm, d), lambda i: (i, 0)), pl.BlockSpec((8, d), lambda i: (0, 0))],
        out_shape=[jax.ShapeDtypeStruct((8, 128), F32), jax.ShapeDtypeStruct((s, d), F32),
                   jax.ShapeDtypeStruct((s, d), BF16), jax.ShapeDtypeStruct((8, d), F32)],
        compiler_params=_params("arbitrary"),
    )(x, g.reshape(1, d), tgt)


def _scan_rows(a_ref, b_ref, h_ref, carry, *, rows, reverse):
    width = a_ref.shape[1]
    n_chunks = rows // 8
    row = lax.broadcasted_iota(jnp.int32, (8, width), 0)

    def step(ci, carry):
        chunk = (n_chunks - 1 - ci) if reverse else ci
        off = pl.multiple_of(chunk * 8, 8)
        av = a_ref[pl.ds(off, 8), :]
        bv = b_ref[pl.ds(off, 8), :]
        for sh in (1, 2, 4):
            if reverse:
                a_sh = pltpu.roll(av, 8 - sh, 0)
                b_sh = pltpu.roll(bv, 8 - sh, 0)
                m = row < 8 - sh
            else:
                a_sh = pltpu.roll(av, sh, 0)
                b_sh = pltpu.roll(bv, sh, 0)
                m = row >= sh
            bv = jnp.where(m, av * b_sh + bv, bv)
            av = jnp.where(m, av * a_sh, av)
        h = av * carry + bv
        h_ref[pl.ds(off, 8), :] = h
        return h[0:1, :] if reverse else h[7:8, :]

    return lax.fori_loop(0, n_chunks, step, carry)


P_CB, P_BA, P_BX, P_DECAY, P_DSP, N_PAR = 4, 5, 6, 7, 8, 9


def _spread_mixer_params(par, par_sc, cw_ref, cb_ref, ba_ref, bx_ref, lam_ref, scw_ref):
    rows = par.shape[1:]
    for k in range(4):
        par[k] = jnp.broadcast_to(cw_ref[k:k + 1, :], rows)
    par[P_CB] = jnp.broadcast_to(cb_ref[...], rows)
    par[P_BA] = jnp.broadcast_to(ba_ref[...], rows)
    par[P_BX] = jnp.broadcast_to(bx_ref[...], rows)
    par[P_DECAY] = jnp.broadcast_to(-RG_C * _softplus(-lam_ref[...]), rows)
    par[P_DSP] = jnp.broadcast_to(-_sigmoid(-lam_ref[...]), rows)
    for k in range(3):
        par_sc[k] = jnp.broadcast_to(scw_ref[k:k + 1, :], par_sc.shape[1:])


def _gates_rows(pre_r, pre_i, par):
    r = _sigmoid(pre_r + par[P_BA])
    ig = _sigmoid(pre_i + par[P_BX])
    log_a = r * par[P_DECAY]
    a = jnp.exp(log_a)
    one_minus_a2 = -jnp.tanh(log_a) * (a * a + 1.0)
    return r, ig, a, jnp.sqrt(one_minus_a2), one_minus_a2


def _mixer_fwd(z, cw, cb, wa_bd, wx_bd, ba, bx, lam, scw, *, tile, name):
    s = z.shape[0]
    n_tiles = s // tile

    rows_of = lambda r0: slice(r0, r0 + FFN_ROWS)
    col_gate, col_sb, col_sc, col_sx = (slice(D_LRU, 2 * D_LRU), slice(2 * D_LRU, 2 * D_LRU + D_SC),
                                        slice(2 * D_LRU + D_SC, 2 * D_LRU + 2 * D_SC), slice(2 * D_LRU + 2 * D_SC, D_IN))

    def body(z_ref, cw_ref, cb_ref, wa_ref, wx_ref, ba_ref, bx_ref, lam_ref, scw_ref,
             y_ref, hs_ref, par, par_sc, lx_s, lxb_s, a_s, b_s, car_lx, car_q, h_car):
        i = pl.program_id(0)

        @pl.when(i == 0)
        def _():
            car_lx[...] = jnp.zeros_like(car_lx)
            car_q[...] = jnp.zeros_like(car_q)
            h_car[...] = jnp.zeros_like(h_car)
            _spread_mixer_params(par, par_sc, cw_ref, cb_ref, ba_ref, bx_ref, lam_ref, scw_ref)

        before_lx, before_q = car_lx[...], car_q[...]
        for r0 in range(0, tile, FFN_ROWS):
            cur = z_ref[rows_of(r0), 0:D_LRU]
            lx = par[P_CB] + par[3] * cur
            for k in range(3):
                lx = lx + par[k] * _rows_from(before_lx, cur, FFN_ROWS - 3 + k)
            lx_s[rows_of(r0), :] = lx
            lxb_s[rows_of(r0), :] = lx.astype(BF16)
            before_lx = cur
            q = z_ref[rows_of(r0), col_sc] * z_ref[rows_of(r0), col_sx]
            cq = par_sc[2] * q
            for k in range(2):
                cq = cq + par_sc[k] * _rows_from(before_q, q, FFN_ROWS - 2 + k)
            y_ref[rows_of(r0), D_LRU:D_MIX] = (z_ref[rows_of(r0), col_sb] * cq).astype(BF16)
            before_q = q
        car_lx[...] = before_lx
        car_q[...] = before_q

        for g in range(N_GROUPS):
            cols = slice(g * LRU_GROUP, (g + 1) * LRU_GROUP)
            a_s[:, cols] = _dot(lxb_s[:, cols], wa_ref[g], NN)
            b_s[:, cols] = _dot(lxb_s[:, cols], wx_ref[g], NN)

        for r0 in range(0, tile, FFN_ROWS):
            _, ig, a, mult, _ = _gates_rows(a_s[rows_of(r0), :], b_s[rows_of(r0), :], par)
            a_s[rows_of(r0), :] = a
            b_s[rows_of(r0), :] = mult * (ig * lx_s[rows_of(r0), :])
        h_car[0:1, :] = _scan_rows(a_s, b_s, hs_ref, h_car[0:1, :], rows=tile, reverse=False)

        for r0 in range(0, tile, FFN_ROWS):
            y_ref[rows_of(r0), 0:D_LRU] = (hs_ref[rows_of(r0), :] * _gelu(z_ref[rows_of(r0), col_gate])).astype(BF16)

    full = lambda shape: pl.BlockSpec(shape, lambda i: (0,) * len(shape))
    return pl.pallas_call(
        body, name=name, grid=(n_tiles,),
        in_specs=[pl.BlockSpec((tile, D_IN), lambda i: (i, 0)),
                  full((4, D_LRU)), full((1, D_LRU)),
                  full((N_GROUPS, LRU_GROUP, LRU_GROUP)), full((N_GROUPS, LRU_GROUP, LRU_GROUP)),
                  full((1, D_LRU)), full((1, D_LRU)), full((1, D_LRU)), full((3, D_SC))],
        out_specs=[pl.BlockSpec((tile, D_MIX), lambda i: (i, 0)), pl.BlockSpec((tile, D_LRU), lambda i: (i, 0))],
        out_shape=[jax.ShapeDtypeStruct((s, D_MIX), BF16), jax.ShapeDtypeStruct((s, D_LRU), F32)],
        scratch_shapes=[pltpu.VMEM((N_PAR, FFN_ROWS, D_LRU), F32), pltpu.VMEM((3, FFN_ROWS, D_SC), F32),
                        pltpu.VMEM((tile, D_LRU), F32), pltpu.VMEM((tile, D_LRU), BF16),
                        pltpu.VMEM((tile, D_LRU), F32), pltpu.VMEM((tile, D_LRU), F32),
                        pltpu.VMEM((FFN_ROWS, D_LRU), F32), pltpu.VMEM((FFN_ROWS, D_SC), F32),
                        pltpu.VMEM((8, D_LRU), F32)],
        compiler_params=_params("arbitrary"),
    )(z, cw, cb.reshape(1, -1), wa_bd, wx_bd, ba.reshape(1, -1), bx.reshape(1, -1), lam.reshape(1, -1), scw)


def _fold8(x):
    return sum(x[q:q + 8] for q in range(0, x.shape[0], 8))


def _mixer_bwd_rows(z, hs, dy, cw, cb, wa_bd, wx_bd, ba, bx, lam, scw, *, tile, name, token=None):
    s = z.shape[0]
    n_tiles = s // tile
    per8 = tile // 8
    rows_of = lambda r0: slice(r0, r0 + FFN_ROWS)
    col_gate, col_sb, col_sc, col_sx = (slice(D_LRU, 2 * D_LRU), slice(2 * D_LRU, 2 * D_LRU + D_SC),
                                        slice(2 * D_LRU + D_SC, 2 * D_LRU + 2 * D_SC), slice(2 * D_LRU + 2 * D_SC, D_IN))
    up = range(0, tile, FFN_ROWS)
    down = range(tile - FFN_ROWS, -1, -FFN_ROWS)
    A_CB, A_BA, A_BX, A_SP, A_CW = 0, 1, 2, 3, 4

    def body(z_ref, zp_ref, hs_ref, hsp_ref, dy_ref, cw_ref, cb_ref, wa_ref, wx_ref, ba_ref, bx_ref, lam_ref, scw_ref,
             token_ref, dz_ref, dcw_ref, dvec_ref, dwa_ref, dwx_ref, dscw_ref,
             par, par_sc, lx_s, lxb_s, cq_s, pr_s, pi_s, r_s, ig_s, a_s, mult_s, inv_s, ash_s, b_s, lam_s,
             dlx_s, dpr_b, dpi_b, back_s, car_a, car_dlx, car_dcq, l_car, acc, acc_sc):
        i = pl.program_id(0)

        @pl.when(i == 0)
        def _():
            for ref in (dwa_ref, dwx_ref, l_car, car_a, car_dlx, car_dcq, acc, acc_sc):
                ref[...] = jnp.zeros_like(ref)
            _spread_mixer_params(par, par_sc, cw_ref, cb_ref, ba_ref, bx_ref, lam_ref, scw_ref)

        keep = jnp.where(i == n_tiles - 1, 0.0, 1.0)
        zeros8 = lambda n: jnp.zeros((8, n), F32)

        before_lx = jnp.concatenate([zeros8(D_LRU), zp_ref[:, 0:D_LRU] * keep], axis=0)
        before_q = jnp.concatenate([zeros8(D_SC), zp_ref[:, col_sc] * zp_ref[:, col_sx] * keep], axis=0)
        for r0 in up:
            cur = z_ref[rows_of(r0), 0:D_LRU]
            lx = par[P_CB] + par[3] * cur
            for k in range(3):
                lx = lx + par[k] * _rows_from(before_lx, cur, FFN_ROWS - 3 + k)
            lx_s[rows_of(r0), :] = lx
            lxb_s[rows_of(r0), :] = lx.astype(BF16)
            before_lx = cur
            q = z_ref[rows_of(r0), col_sc] * z_ref[rows_of(r0), col_sx]
            cq = par_sc[2] * q
            for k in range(2):
                cq = cq + par_sc[k] * _rows_from(before_q, q, FFN_ROWS - 2 + k)
            cq_s[rows_of(r0), :] = cq
            before_q = q

        for g in range(N_GROUPS):
            cols = slice(g * LRU_GROUP, (g + 1) * LRU_GROUP)
            pr_s[:, cols] = _dot(lxb_s[:, cols], wa_ref[g], NN)
            pi_s[:, cols] = _dot(lxb_s[:, cols], wx_ref[g], NN)

        after_a = car_a[...]
        for r0 in down:
            r, ig, a, mult, one_minus_a2 = _gates_rows(pr_s[rows_of(r0), :], pi_s[rows_of(r0), :], par)
            r_s[rows_of(r0), :] = r
            ig_s[rows_of(r0), :] = ig
            a_s[rows_of(r0), :] = a
            mult_s[rows_of(r0), :] = mult
            inv_s[rows_of(r0), :] = lax.rsqrt(one_minus_a2)
            ash_s[rows_of(r0), :] = _rows_from(a, after_a, 1)
            after_a = a
            ge, dge = _gelu_parts(z_ref[rows_of(r0), col_gate])
            dy_lru = dy_ref[rows_of(r0), 0:D_LRU]
            dz_ref[rows_of(r0), col_gate] = (dy_lru * hs_ref[rows_of(r0), :] * dge).astype(BF16)
            b_s[rows_of(r0), :] = dy_lru * ge
        car_a[...] = after_a
        l_car[0:1, :] = _scan_rows(ash_s, b_s, lam_s, l_car[0:1, :], rows=tile, reverse=True)

        before_h = jnp.concatenate([zeros8(D_LRU), hsp_ref[...] * keep], axis=0)
        for r0 in up:
            lv = lam_s[rows_of(r0), :]
            h_here = hs_ref[rows_of(r0), :]
            lx, r, ig, a = lx_s[rows_of(r0), :], r_s[rows_of(r0), :], ig_s[rows_of(r0), :], a_s[rows_of(r0), :]
            mult = mult_s[rows_of(r0), :]
            da = lv * _rows_from(before_h, h_here, FFN_ROWS - 1)
            before_h = h_here
            d_mult = lv * ig * lx
            d_i = lv * mult * lx
            dlx_s[rows_of(r0), :] = lv * mult * ig
            dlog_a = da * a - d_mult * (a * a) * inv_s[rows_of(r0), :]
            dpre_r = dlog_a * par[P_DECAY] * r * (1.0 - r)
            dpre_i = d_i * ig * (1.0 - ig)
            acc[A_BA] += _fold8(dpre_r)
            acc[A_BX] += _fold8(dpre_i)
            acc[A_SP] += _fold8(dlog_a * r)
            dpr_b[rows_of(r0), :] = dpre_r.astype(BF16)
            dpi_b[rows_of(r0), :] = dpre_i.astype(BF16)

        for g in range(N_GROUPS):
            cols = slice(g * LRU_GROUP, (g + 1) * LRU_GROUP)
            dwa_ref[g] += _dot(lxb_s[:, cols], dpr_b[:, cols], TN)
            dwx_ref[g] += _dot(lxb_s[:, cols], dpi_b[:, cols], TN)
            back_s[:, cols] = _dot(dpr_b[:, cols], wa_ref[g], NT) + _dot(dpi_b[:, cols], wx_ref[g], NT)

        after_dlx, after_dcq = car_dlx[...], car_dcq[...]
        for r0 in down:
            dlx = dlx_s[rows_of(r0), :] + back_s[rows_of(r0), :]
            lxp = z_ref[rows_of(r0), 0:D_LRU]
            acc[A_CB] += _fold8(dlx)
            acc[A_CW + 3] += _fold8(dlx * lxp)
            dlxp = par[3] * dlx
            for sh in range(1, 4):
                below = _rows_from(dlx, after_dlx, sh)
                dlxp = dlxp + par[3 - sh] * below
                acc[A_CW + 3 - sh] += _fold8(below * lxp)
            dz_ref[rows_of(r0), 0:D_LRU] = dlxp.astype(BF16)
            after_dlx = dlx

            dy_sc = dy_ref[rows_of(r0), D_LRU:D_MIX]
            sb, sc, sx = z_ref[rows_of(r0), col_sb], z_ref[rows_of(r0), col_sc], z_ref[rows_of(r0), col_sx]
            dz_ref[rows_of(r0), col_sb] = (dy_sc * cq_s[rows_of(r0), :]).astype(BF16)
            dcq = dy_sc * sb
            q = sc * sx
            acc_sc[2] += _fold8(dcq * q)
            dq = par_sc[2] * dcq
            for sh in range(1, 3):
                below = _rows_from(dcq, after_dcq, sh)
                dq = dq + par_sc[2 - sh] * below
                acc_sc[2 - sh] += _fold8(below * q)
            dz_ref[rows_of(r0), col_sc] = (dq * sx).astype(BF16)
            dz_ref[rows_of(r0), col_sx] = (dq * sc).astype(BF16)
            after_dcq = dcq
        car_dlx[...] = after_dlx
        car_dcq[...] = after_dcq

        @pl.when(i == n_tiles - 1)
        def _():
            total = lambda x: jnp.sum(x, axis=0, keepdims=True)
            dcw_ref[...] = jnp.zeros_like(dcw_ref)
            dvec_ref[...] = jnp.zeros_like(dvec_ref)
            dscw_ref[...] = jnp.zeros_like(dscw_ref)
            for k in range(4):
                dcw_ref[k:k + 1, :] = total(acc[A_CW + k])
            for k in range(3):
                dvec_ref[k:k + 1, :] = total(acc[k])
                dscw_ref[k:k + 1, :] = total(acc_sc[k])
            dvec_ref[3:4, :] = total(acc[A_SP]) * (-RG_C) * par[P_DSP][0:1, :]

    rev = lambda i: n_tiles - 1 - i
    prev8 = lambda i: jnp.maximum(rev(i) * per8 - 1, 0)
    full = lambda shape: pl.BlockSpec(shape, lambda i: (0,) * len(shape))
    wide = lambda rows, dt=F32: pltpu.VMEM((rows, D_LRU), dt)
    return pl.pallas_call(
        body, name=name, grid=(n_tiles,),
        in_specs=[pl.BlockSpec((tile, D_IN), lambda i: (rev(i), 0)),
                  pl.BlockSpec((HALO, D_IN), lambda i: (prev8(i), 0)),
                  pl.BlockSpec((tile, D_LRU), lambda i: (rev(i), 0)),
                  pl.BlockSpec((HALO, D_LRU), lambda i: (prev8(i), 0)),
                  pl.BlockSpec((tile, D_MIX), lambda i: (rev(i), 0)),
                  full((4, D_LRU)), full((1, D_LRU)),
                  full((N_GROUPS, LRU_GROUP, LRU_GROUP)), full((N_GROUPS, LRU_GROUP, LRU_GROUP)),
                  full((1, D_LRU)), full((1, D_LRU)), full((1, D_LRU)), full((3, D_SC)),
                  pl.BlockSpec(memory_space=pl.ANY)],
        out_specs=[pl.BlockSpec((tile, D_IN), lambda i: (rev(i), 0)),
                   full((8, D_LRU)), full((8, D_LRU)),
                   full((N_GROUPS, LRU_GROUP, LRU_GROUP)), full((N_GROUPS, LRU_GROUP, LRU_GROUP)),
                   full((8, D_SC))],
        out_shape=[jax.ShapeDtypeStruct((s, D_IN), BF16),
                   jax.ShapeDtypeStruct((8, D_LRU), F32), jax.ShapeDtypeStruct((8, D_LRU), F32),
                   jax.ShapeDtypeStruct((N_GROUPS, LRU_GROUP, LRU_GROUP), F32),
                   jax.ShapeDtypeStruct((N_GROUPS, LRU_GROUP, LRU_GROUP), F32),
                   jax.ShapeDtypeStruct((8, D_SC), F32)],
        scratch_shapes=[pltpu.VMEM((N_PAR, FFN_ROWS, D_LRU), F32), pltpu.VMEM((3, FFN_ROWS, D_SC), F32),
                        wide(tile), wide(tile, BF16), pltpu.VMEM((tile, D_SC), F32),
                        wide(tile), wide(tile), wide(tile), wide(tile), wide(tile), wide(tile), wide(tile),
                        wide(tile), wide(tile), wide(tile),
                        wide(tile), wide(tile, BF16), wide(tile, BF16), wide(tile),
                        wide(FFN_ROWS), wide(FFN_ROWS), pltpu.VMEM((FFN_ROWS, D_SC), F32), wide(8),
                        pltpu.VMEM((8, 8, D_LRU), F32), pltpu.VMEM((3, 8, D_SC), F32)],
        compiler_params=_params("arbitrary"),
    )(z, z, hs, hs, dy, cw, cb.reshape(1, -1), wa_bd, wx_bd, ba.reshape(1, -1), bx.reshape(1, -1),
      lam.reshape(1, -1), scw, _behind(token))


FFN_ROWS = 16
FFN_GROUPS = 2


def _spread_taps(fw_ref, taps):
    for half in range(2):
        for k in range(3):
            taps[half, k] = jnp.broadcast_to(fw_ref[half, k:k + 1, :], taps.shape[2:])


def _rows_from(first, second, start):
    stack = jnp.concatenate([first, second], axis=0)
    return pltpu.roll(stack, 2 * FFN_ROWS - start, 0)[0:FFN_ROWS]


def _conv3_rows(taps, ext_ref, half, row):
    before = ext_ref[half, row - FFN_ROWS:row, :]
    here = ext_ref[half, row:row + FFN_ROWS, :]
    acc = taps[half, 2] * here
    for k in range(2):
        acc = acc + taps[half, k] * _rows_from(before, here, FFN_ROWS - 2 + k)
    return acc


HALO_B = 16


def _ffn_block_fwd(x2, g2, w_up_b, fcw, w_down, *, tile, name):
    s = x2.shape[0]
    nb = w_up_b.shape[2]
    blocks = D_FF // nb
    per16 = tile // HALO_B

    def body(x2_ref, x2p_ref, g_ref, wg_ref, wu_ref, fw_ref, wd_ref, x3_ref, h_ref, act_ref, p_ref, u_ref,
             ext_p, acc_ref, taps, lhs):
        i = pl.program_id(0)
        j = pl.program_id(1)
        keep = jnp.where(i == 0, 0.0, 1.0)
        _spread_taps(fw_ref, taps)
        @pl.when(j == 0)
        def _():
            for rows_ref, at in ((x2p_ref, 0), (x2_ref, HALO_B)):
                xv = rows_ref[...]
                lhs[at:at + xv.shape[0], :] = (xv * _rms(xv) * g_ref[...]).astype(BF16)
            h_ref[...] = lhs[HALO_B:HALO_B + tile, :]

        groups = FFN_GROUPS * tile // FFN_BWD_TILE
        grp = tile // groups
        for g in range(groups):
            new = slice(g * grp + (HALO_B if g else 0), (g + 1) * grp + HALO_B)
            for half, w_ref in ((0, wg_ref), (1, wu_ref)):
                pe = _dot(lhs[new, :], w_ref[...], NN)
                if g == 0:
                    ext_p[half, 0:HALO_B, :] = pe[0:HALO_B] * keep
                    ext_p[half, HALO_B:grp + HALO_B, :] = pe[HALO_B:]
                    p_ref[half, 0:grp, :] = pe[HALO_B:].astype(BF16)
                else:
                    ext_p[half, new, :] = pe
                    p_ref[half, g * grp:(g + 1) * grp, :] = pe.astype(BF16)
        for g in range(groups):
            rows = slice(g * grp, (g + 1) * grp)
            acts = []
            for r0 in range(g * grp, (g + 1) * grp, FFN_ROWS):
                u = [_conv3_rows(taps, ext_p, half, HALO_B + r0) for half in range(2)]
                for half in range(2):
                    u_ref[half, r0:r0 + FFN_ROWS, :] = u[half].astype(BF16)
                acts.append((_gelu(u[0]) * u[1]).astype(BF16))
                act_ref[r0:r0 + FFN_ROWS, :] = acts[-1]
            contrib = _dot(jnp.concatenate(acts, axis=0), wd_ref[...], NN)
            acc_ref[rows, :] = contrib + jnp.where(j > 0, acc_ref[rows, :], 0.0)

        @pl.when(j == blocks - 1)
        def _():
            x3_ref[...] = x2_ref[...] + acc_ref[...]

    return pl.pallas_call(
        body, name=name, grid=(s // tile, blocks),
        in_specs=[pl.BlockSpec((tile, D_MODEL), lambda i, j: (i, 0)),
                  pl.BlockSpec((HALO_B, D_MODEL), lambda i, j: (jnp.maximum(i * per16 - 1, 0), 0)),
                  pl.BlockSpec((1, D_MODEL), lambda i, j: (0, 0)),
                  pl.BlockSpec((None, D_MODEL, nb), lambda i, j: (j, 0, 0)),
                  pl.BlockSpec((None, D_MODEL, nb), lambda i, j: (j + blocks, 0, 0)),
                  pl.BlockSpec((2, 3, nb), lambda i, j: (0, 0, j)),
                  pl.BlockSpec((nb, D_MODEL), lambda i, j: (j, 0))],
        out_specs=[pl.BlockSpec((tile, D_MODEL), lambda i, j: (i, 0)),
                   pl.BlockSpec((tile, D_MODEL), lambda i, j: (i, 0)),
                   pl.BlockSpec((tile, nb), lambda i, j: (i, j)),
                   pl.BlockSpec((2, tile, nb), lambda i, j: (0, i, j)),
                   pl.BlockSpec((2, tile, nb), lambda i, j: (0, i, j))],
        out_shape=[jax.ShapeDtypeStruct((s, D_MODEL), F32), jax.ShapeDtypeStruct((s, D_MODEL), BF16),
                   jax.ShapeDtypeStruct((s, D_FF), BF16),
                   jax.ShapeDtypeStruct((2, s, D_FF), BF16), jax.ShapeDtypeStruct((2, s, D_FF), BF16)],
        scratch_shapes=[pltpu.VMEM((2, tile + HALO_B, nb), F32), pltpu.VMEM((tile, D_MODEL), F32),
                        pltpu.VMEM((2, 3, FFN_ROWS, nb), F32), pltpu.VMEM((tile + HALO_B, D_MODEL), BF16)],
        compiler_params=_params("parallel", "arbitrary"),
    )(x2, x2, g2.reshape(1, -1), w_up_b, w_up_b, fcw, w_down)


def _ffn_block_bwd(dx3, dx3b, p, u, x2, g2, w_up_b, fcw, w_down, *, tile, name, token=None):
    s = x2.shape[0]
    nb = w_up_b.shape[2]
    blocks = D_FF // nb
    n_tiles = s // tile
    per16 = tile // HALO_B
    last16 = s // HALO_B - 1

    def body(dxb_ref, dxbn_ref, wd_ref, p_ref, u_ref, un_ref, fw_ref, wg_ref, wu_ref, x2_ref, g_ref, dx3_ref,
             token_ref, dx2_ref, dx2b_ref, dg_ref, dp_ref, dw_ref, da_s, acc_w, acc_dh, taps):
        i = pl.program_id(0)
        j = pl.program_id(1)

        @pl.when(jnp.logical_and(i == 0, j == 0))
        def _():
            acc_w[...] = jnp.zeros_like(acc_w)
            dg_ref[...] = jnp.zeros_like(dg_ref)

        keep_next = jnp.where(i == n_tiles - 1, 0.0, 1.0)
        _spread_taps(fw_ref, taps)
        lhs = jnp.concatenate([dxb_ref[...], dxbn_ref[...]], axis=0)
        grp = tile // FFN_GROUPS
        for g in reversed(range(FFN_GROUPS)):
            new = slice(g * grp, (g + 1) * grp + (HALO_B if g == FFN_GROUPS - 1 else 0))
            da_s[new, :] = _dot(lhs[new], wd_ref[...], NT)

        def du_rows(da, u_gate, u_up):
            ge, dge = _gelu_parts(u_gate)
            return da * u_up * dge, da * ge

        after = du_rows(da_s[tile:tile + HALO_B, :] * keep_next, un_ref[0].astype(F32), un_ref[1].astype(F32))
        for g in reversed(range(FFN_GROUPS)):
            rows = slice(g * grp, (g + 1) * grp)
            dps = ([], [])
            for r0 in range((g + 1) * grp - FFN_ROWS, g * grp - 1, -FFN_ROWS):
                du = du_rows(da_s[r0:r0 + FFN_ROWS, :], u_ref[0, r0:r0 + FFN_ROWS, :].astype(F32),
                             u_ref[1, r0:r0 + FFN_ROWS, :].astype(F32))
                for half in range(2):
                    below = [du[half], _rows_from(du[half], after[half], 1), _rows_from(du[half], after[half], 2)]
                    acc = taps[half, 2] * below[0]
                    for k in range(2):
                        acc = acc + taps[half, k] * below[2 - k]
                    dps[half].insert(0, acc.astype(BF16))
                    dp_ref[half, r0:r0 + FFN_ROWS, :] = dps[half][0]
                    p_rows = p_ref[half, r0:r0 + FFN_ROWS, :].astype(F32)
                    for k in range(3):
                        prod = below[2 - k] * p_rows
                        acc_w[j, half, k] += sum(prod[q:q + 8] for q in range(0, FFN_ROWS, 8))
                after = du
            contrib = (_dot(jnp.concatenate(dps[0], axis=0), wg_ref[...], NT)
                       + _dot(jnp.concatenate(dps[1], axis=0), wu_ref[...], NT))
            acc_dh[rows, :] = contrib + jnp.where(j > 0, acc_dh[rows, :], 0.0)

        @pl.when(j == blocks - 1)
        def _():
            dh = acc_dh[...]
            xv = x2_ref[...]
            rstd = _rms(xv)
            n = xv * rstd
            dn = dh * g_ref[...]
            dx = dx3_ref[...] + rstd * (dn - n * jnp.mean(dn * n, axis=-1, keepdims=True))
            dx2_ref[...] = dx
            dx2b_ref[...] = dx.astype(BF16)
            dg_ref[0:1, :] += jnp.sum(dh * n, axis=0, keepdims=True)

        @pl.when(jnp.logical_and(i == n_tiles - 1, j == blocks - 1))
        def _():
            dw_ref[...] = jnp.zeros_like(dw_ref)
            for jj in range(blocks):
                for half in range(2):
                    for k in range(3):
                        dw_ref[half, k:k + 1, jj * nb:(jj + 1) * nb] = jnp.sum(acc_w[jj, half, k], axis=0, keepdims=True)

    next16 = lambda i: jnp.minimum((i + 1) * per16, last16)
    return pl.pallas_call(
        body, name=name, grid=(n_tiles, blocks),
        in_specs=[pl.BlockSpec((tile, D_MODEL), lambda i, j: (i, 0)),
                  pl.BlockSpec((HALO_B, D_MODEL), lambda i, j: (next16(i), 0)),
                  pl.BlockSpec((nb, D_MODEL), lambda i, j: (j, 0)),
                  pl.BlockSpec((2, tile, nb), lambda i, j: (0, i, j)),
                  pl.BlockSpec((2, tile, nb), lambda i, j: (0, i, j)),
                  pl.BlockSpec((2, HALO_B, nb), lambda i, j: (0, next16(i), j)),
                  pl.BlockSpec((2, 3, nb), lambda i, j: (0, 0, j)),
                  pl.BlockSpec((None, D_MODEL, nb), lambda i, j: (j, 0, 0)),
                  pl.BlockSpec((None, D_MODEL, nb), lambda i, j: (j + blocks, 0, 0)),
                  pl.BlockSpec((tile, D_MODEL), lambda i, j: (i, 0)),
                  pl.BlockSpec((1, D_MODEL), lambda i, j: (0, 0)),
                  pl.BlockSpec((tile, D_MODEL), lambda i, j: (i, 0)),
                  pl.BlockSpec(memory_space=pl.ANY)],
        out_specs=[pl.BlockSpec((tile, D_MODEL), lambda i, j: (i, 0)),
                   pl.BlockSpec((tile, D_MODEL), lambda i, j: (i, 0)),
                   pl.BlockSpec((8, D_MODEL), lambda i, j: (0, 0)),
                   pl.BlockSpec((2, tile, nb), lambda i, j: (0, i, j)),
                   pl.BlockSpec((2, 8, D_FF), lambda i, j: (0, 0, 0))],
        out_shape=[jax.ShapeDtypeStruct((s, D_MODEL), F32), jax.ShapeDtypeStruct((s, D_MODEL), BF16),
                   jax.ShapeDtypeStruct((8, D_MODEL), F32), jax.ShapeDtypeStruct((2, s, D_FF), BF16),
                   jax.ShapeDtypeStruct((2, 8, D_FF), F32)],
        scratch_shapes=[pltpu.VMEM((tile + HALO_B, nb), F32), pltpu.VMEM((blocks, 2, 3, 8, nb), F32),
                        pltpu.VMEM((tile, D_MODEL), F32), pltpu.VMEM((2, 3, FFN_ROWS, nb), F32)],
        compiler_params=_params("arbitrary", "arbitrary"),
    )(dx3b, dx3b, w_down, p, u, u, fcw, w_up_b, w_up_b, x2, g2.reshape(1, -1), dx3, _behind(token))


def _adamw_math(w, g, m, v):
    m = ADAM_B1 * m + (1.0 - ADAM_B1) * g
    v = ADAM_B2 * v + (1.0 - ADAM_B2) * (g * g)
    m_hat = m / (1.0 - ADAM_B1 ** ADAM_STEP)
    v_hat = v / (1.0 - ADAM_B2 ** ADAM_STEP)
    delta = -ADAM_LR * (m_hat / (jnp.sqrt(v_hat) + ADAM_EPS) + ADAM_WD * w)
    return delta, m, v


def _adamw(w, g, m, v, *, name):
    rows, cols = w.shape
    tr = rows
    for cand in (512, 256, 128, 64, 32, 16, 8):
        if rows % cand == 0 and rows > cand:
            tr = cand
            break

    def body(w_ref, g_ref, m_ref, v_ref, d_ref, nm_ref, nv_ref):
        d, nm, nv = _adamw_math(w_ref[...], g_ref[...], m_ref[...], v_ref[...])
        d_ref[...] = d
        nm_ref[...] = nm
        nv_ref[...] = nv

    spec = pl.BlockSpec((tr, cols), lambda i: (i, 0))
    return pl.pallas_call(
        body, name=name, grid=(rows // tr,), in_specs=[spec] * 4, out_specs=[spec] * 3,
        out_shape=[jax.ShapeDtypeStruct((rows, cols), F32)] * 3,
        compiler_params=_params("parallel"),
    )(w, g, m, v)


def _sum_adamw(parts, w, m, v, *, name):
    depth, rows, cols = w.shape
    tr = rows
    for cand in (256, 128, 64):
        if rows % cand == 0 and rows > cand:
            tr = cand
            break

    def body(*refs):
        part_refs = refs[:depth]
        w_ref, m_ref, v_ref, g_ref, d_ref, nm_ref, nv_ref = refs[depth:]
        layer = pl.program_id(0)
        grad = None
        for k, p_ref in enumerate(part_refs):
            total = p_ref[0].astype(F32)
            for dev in range(1, N_DEV):
                total = total + p_ref[dev].astype(F32)
            grad = total if grad is None else jnp.where(layer == k, total, grad)
        d, nm, nv = _adamw_math(w_ref[...], grad, m_ref[...], v_ref[...])
        g_ref[...] = grad
        d_ref[...] = d
        nm_ref[...] = nm
        nv_ref[...] = nv

    part_spec = lambda k: pl.BlockSpec((N_DEV, tr, cols), lambda l, i: (0, jnp.where(l == k, i, 0), 0))
    spec = pl.BlockSpec((None, tr, cols), lambda l, i: (l, i, 0))
    return pl.pallas_call(
        body, name=name, grid=(depth, rows // tr),
        in_specs=[part_spec(k) for k in range(depth)] + [spec] * 3, out_specs=[spec] * 4,
        out_shape=[jax.ShapeDtypeStruct((depth, rows, cols), F32)] * 4,
        compiler_params=_params("parallel", "parallel"),
    )(*parts, w, m, v)


def _sum_parts(parts, *, name, tokens=()):
    _, rows, cols = parts.shape
    tr = rows
    for cand in (256, 128, 64, 32, 16):
        if rows % cand == 0 and rows > cand:
            tr = cand
            break

    def body(p_ref, *rest):
        acc = p_ref[0].astype(F32)
        for d in range(1, N_DEV):
            acc = acc + p_ref[d].astype(F32)
        rest[-1][...] = acc

    return pl.pallas_call(
        body, name=name, grid=(rows // tr,),
        in_specs=[pl.BlockSpec((N_DEV, tr, cols), lambda i: (0, i, 0))]
        + [pl.BlockSpec(memory_space=pl.ANY)] * len(tokens),
        out_specs=pl.BlockSpec((tr, cols), lambda i: (i, 0)),
        out_shape=jax.ShapeDtypeStruct((rows, cols), F32),
        compiler_params=_params("parallel"),
    )(parts, *tokens)


def _place():
    return lax.axis_index("x"), lax.axis_index("y"), lax.axis_index("c")


def _flip(v, bit):
    return 1 - v if bit else v


N_PEERS = N_DEV - 1


def _peer_copy(k, src_ref, land_ref, send_sem, recv_sem, gather):
    x, y, c = _place()
    my_id = 4 * x + 2 * y + c
    px, py, pc = _flip(x, k & 4), _flip(y, k & 2), _flip(c, k & 1)
    peer_id = 4 * px + 2 * py + pc
    return pltpu.make_async_remote_copy(
        src_ref=src_ref if gather else src_ref.at[peer_id], dst_ref=land_ref.at[my_id],
        send_sem=send_sem.at[k - 1], recv_sem=recv_sem.at[k - 1],
        device_id=(px, py, pc), device_id_type=MESH)


def _sequencer_copies(srcs, *, gather, name, collective_id, after):
    n = len(srcs)
    hbm = pltpu.MemorySpace.HBM
    src_refs = [jax.new_ref(s, memory_space=hbm) for s in srcs]
    land_refs = [jax.empty_ref(jax.ShapeDtypeStruct(((N_DEV,) + s.shape) if gather else s.shape, s.dtype),
                               memory_space=hbm) for s in srcs]
    token_in = jax.new_ref(jnp.zeros((8, 128), F32) if after is None else after, memory_space=hbm)
    token_out = jax.empty_ref(jax.ShapeDtypeStruct((8, 128), F32), memory_space=hbm)

    @pl.kernel(mesh=plsc.ScalarSubcoreMesh(axis_name="seq", num_cores=1), name=name,
               scratch_types=(pltpu.SemaphoreType.DMA((n, N_PEERS)), pltpu.SemaphoreType.DMA((n, N_PEERS)),
                              pltpu.SemaphoreType.DMA((n + 1,))),
               compiler_params=pltpu.CompilerParams(collective_id=collective_id))
    def launch(send_sems, recv_sems, local_sems):
        x, y, c = _place()
        my_id = 4 * x + 2 * y + c
        barrier = pltpu.get_barrier_semaphore()
        own = [pltpu.make_async_copy(src_refs[t] if gather else src_refs[t].at[my_id], land_refs[t].at[my_id],
                                     local_sems.at[t]) for t in range(n)]
        if gather:
            sibling = (x, y, 1 - c)
            chips = [(1 - x, y), (x, 1 - y), (1 - x, 1 - y)]
            for peer in [sibling] + [(*chip, c) for chip in chips]:
                pl.semaphore_signal(barrier, inc=1, device_id=peer, device_id_type=MESH)
            pl.semaphore_wait(barrier, 4)

            def copy(t, k, block, to, src=None):
                dst = land_refs[t].at[4 * block[0] + 2 * block[1] + block[2]]
                return pltpu.make_async_remote_copy(
                    src_ref=dst if src is None else src, dst_ref=dst,
                    send_sem=send_sems.at[t, k], recv_sem=recv_sems.at[t, k], device_id=to, device_id_type=MESH)

            for cp in own:
                cp.start()
            sends = []
            for t in range(n):
                sends.append(copy(t, 0, (x, y, c), sibling, src=src_refs[t]))
                sends += [copy(t, 1 + j, (x, y, c), (*chip, c), src=src_refs[t]) for j, chip in enumerate(chips)]
            for cp in sends:
                cp.start()
            for t in range(n):
                for j, chip in enumerate(chips):
                    copy(t, 1 + j, (*chip, c), (x, y, c)).wait_recv()
                    passed_on = copy(t, 4 + j, (*chip, c), sibling)
                    passed_on.start()
                    sends.append(passed_on)
            for t in range(n):
                copy(t, 0, sibling, (x, y, c)).wait_recv()
                for j, chip in enumerate(chips):
                    copy(t, 4 + j, (*chip, 1 - c), (x, y, c)).wait_recv()
            for cp in sends:
                cp.wait_send()
            for cp in own:
                cp.wait()
        else:
            for k in range(1, N_DEV):
                peer = (_flip(x, k & 4), _flip(y, k & 2), _flip(c, k & 1))
                pl.semaphore_signal(barrier, inc=1, device_id=peer, device_id_type=MESH)
            pl.semaphore_wait(barrier, N_PEERS)
            for cp in own:
                cp.start()
            copies = [_peer_copy(k, src_refs[t], land_refs[t], send_sems.at[t], recv_sems.at[t], gather)
                      for t in range(n) for k in range(1, N_DEV)]
            for cp in copies:
                cp.start()
            for cp in own:
                cp.wait()
            for cp in copies:
                cp.wait()
        passed = pltpu.make_async_copy(token_in, token_out, local_sems.at[n])
        passed.start()
        passed.wait()

    launch()
    return [ref[...] for ref in land_refs], token_out[...]


TM = 512
TMM = 1024
TKW = 2048
MIX_TILE = 256
FFN_FWD_TILE = 1024
FFN_BWD_TILE = 512


def _block_diag(w):
    wg = w.reshape(N_GROUPS, HEADS_PER_GROUP, LRU_HEAD_DIM, LRU_HEAD_DIM)
    eye = jnp.eye(HEADS_PER_GROUP, dtype=w.dtype)
    bd = wg[:, :, :, None, :] * eye[None, :, None, :, None]
    return bd.reshape(N_GROUPS, LRU_GROUP, LRU_GROUP).astype(BF16)


def _head_blocks(bd):
    b5 = bd.reshape(N_GROUPS, HEADS_PER_GROUP, LRU_HEAD_DIM, HEADS_PER_GROUP, LRU_HEAD_DIM)
    blocks = [b5[:, h, :, h, :] for h in range(HEADS_PER_GROUP)]
    return jnp.stack(blocks, axis=1).reshape(LRU_HEADS, LRU_HEAD_DIM, LRU_HEAD_DIM)


def _w(lw, key, after):
    value = lw[key]
    return value(after) if callable(value) else value


def _layer_fwd(x, lw, tag):
    sv_rows = x.shape[0]
    z, h1 = _norm_in_proj(x, lw["g1"], _w(lw, "w_in_t", x), tm=min(2 * TMM, sv_rows), tn=896, name=f"in_proj_{tag}")
    y_mix, hs = _mixer_fwd(z, _w(lw, "cw", z), lw["cb"], lw["wa_bd"], lw["wx_bd"], lw["ba"], lw["bx"], lw["lam"],
                           _w(lw, "scw", z), tile=MIX_TILE, name=f"mixer_fwd_{tag}")
    x2 = _mm_nn(y_mix, _w(lw, "w_out", y_mix), tm=min(TMM, sv_rows), tn=D_MODEL, tk=D_MIX, out_dtype=F32, name=f"out_proj_{tag}",
                residual=x)
    x3, h2, act, p, u = _ffn_block_fwd(x2, lw["g2"], _w(lw, "w_up_b", x2), _w(lw, "fcw", x2), _w(lw, "w_down", x2),
                                       tile=min(FFN_FWD_TILE, sv_rows), name=f"ffn_fwd_{tag}")
    saved = dict(x=x, h1=h1, z=z, y_mix=y_mix, hs=hs, x2=x2, h2=h2, p=p, u=u, act=act)
    return x3, saved


def _layer_bwd(dx3, dx3b, lw, sv, tag, put):
    sv_rows = dx3.shape[0]
    w_in_t, w_out, w_up_b, w_down = (_w(lw, k, dx3) for k in ("w_in_t", "w_out", "w_up_b", "w_down"))
    cw, scw, fcw = (_w(lw, k, dx3) for k in ("cw", "scw", "fcw"))
    g_down = _mm_tn(sv["act"], dx3b, tm=1024, tn=D_MODEL, tk=min(TKW, sv_rows), out_dtype=BF16, name=f"down_bwd_w_{tag}")
    dx2, dx2b, dg2, dp, dfcw = _ffn_block_bwd(dx3, dx3b, sv["p"], sv["u"], sv["x2"], lw["g2"], w_up_b, fcw, w_down,
                                              tile=min(FFN_BWD_TILE, sv_rows), name=f"ffn_bwd_{tag}",
                                              token=put("w_down", g_down))
    g_up = _mm_up_bwd_w(sv["h2"], dp, tm=D_MODEL, tk=min(TKW, sv_rows), name=f"up_bwd_w_{tag}")
    dy = _mm_nt(dx2b, w_out, tm=min(TMM, sv_rows), tn=768, tk=D_MODEL, out_dtype=F32, name=f"out_bwd_x_{tag}")
    dz, dcw, dvec, dwa, dwx, dscw = _mixer_bwd_rows(
        sv["z"], sv["hs"], dy, cw, lw["cb"], lw["wa_bd"], lw["wx_bd"], lw["ba"], lw["bx"], lw["lam"],
        scw, tile=MIX_TILE, name=f"mixer_bwd_{tag}", token=put("w_up_b", g_up))
    g_out = _mm_tn(sv["y_mix"], dx2b, tm=768, tn=D_MODEL, tk=min(TKW, sv_rows), out_dtype=BF16, name=f"out_bwd_w_{tag}",
                   token=dz)
    g_in_t = _mm_tn(dz, sv["h1"], tm=896, tn=D_MODEL, tk=min(TKW, sv_rows), out_dtype=BF16, name=f"in_bwd_w_{tag}",
                    token=put("w_out", g_out))
    dx, dxb, dg1 = _in_bwd_norm(dz, w_in_t, sv["x"], lw["g1"], dx2, tm=min(TMM, sv_rows), tk=896, name=f"in_bwd_x_{tag}",
                                token=put("w_in_t", g_in_t))
    small = dict(norm1_g=dg1[0], lru_conv_w=dcw[0:4], lru_conv_b=dvec[0], lru_wa=_head_blocks(dwa),
                 lru_ba=dvec[1], lru_wx=_head_blocks(dwx), lru_bx=dvec[2], lru_lambda=dvec[3],
                 sc_conv_w=dscw[0:3], norm2_g=dg2[0], ffn_conv_w=dfcw[:, 0:3, :])
    return dx, dxb, small


SMALL_ORDER = ("norm1_g", "lru_conv_w", "lru_conv_b", "lru_wa", "lru_ba", "lru_wx", "lru_bx", "lru_lambda",
               "sc_conv_w", "norm2_g", "ffn_conv_w")


def _local_step(x, tgt, layers, final_g, put):
    saved = []
    h = x
    for l in range(DEPTH):
        h, sv = _layer_fwd(h, layers[l], f"l{l}")
        saved.append(sv)
    loss_blk, dx, dxb, dgf = _loss_head(h, final_g, tgt, tm=TM, name="loss_head")
    smalls = [None] * DEPTH
    for l in reversed(range(DEPTH)):
        dx, dxb, smalls[l] = _layer_bwd(dx, dxb, layers[l], saved[l], f"l{l}", functools.partial(put, l))
    return loss_blk[0, 0], dx, smalls, dgf[0]


def kernel(x, norm1_g, w_in, lru_conv_w, lru_conv_b, lru_wa, lru_ba, lru_wx, lru_bx, lru_lambda, sc_conv_w, w_out, norm2_g, w_up, ffn_conv_w, w_down, final_g, loss_target, m_norm1_g, m_w_in, m_lru_conv_w, m_lru_conv_b, m_lru_wa, m_lru_ba, m_lru_wx, m_lru_bx, m_lru_lambda, m_sc_conv_w, m_w_out, m_norm2_g, m_w_up, m_ffn_conv_w, m_w_down, m_final_g, v_norm1_g, v_w_in, v_lru_conv_w, v_lru_conv_b, v_lru_wa, v_lru_ba, v_lru_wx, v_lru_bx, v_lru_lambda, v_sc_conv_w, v_w_out, v_norm2_g, v_w_up, v_ffn_conv_w, v_w_down, v_final_g):
    names = ["norm1_g", "w_in", "lru_conv_w", "lru_conv_b", "lru_wa", "lru_ba", "lru_wx", "lru_bx", "lru_lambda",
             "sc_conv_w", "w_out", "norm2_g", "w_up", "ffn_conv_w", "w_down", "final_g"]
    w = dict(zip(names, [norm1_g, w_in, lru_conv_w, lru_conv_b, lru_wa, lru_ba, lru_wx, lru_bx, lru_lambda,
                         sc_conv_w, w_out, norm2_g, w_up, ffn_conv_w, w_down, final_g]))
    m = dict(zip(names, [m_norm1_g, m_w_in, m_lru_conv_w, m_lru_conv_b, m_lru_wa, m_lru_ba, m_lru_wx, m_lru_bx,
                         m_lru_lambda, m_sc_conv_w, m_w_out, m_norm2_g, m_w_up, m_ffn_conv_w, m_w_down, m_final_g]))
    v = dict(zip(names, [v_norm1_g, v_w_in, v_lru_conv_w, v_lru_conv_b, v_lru_wa, v_lru_ba, v_lru_wx, v_lru_bx,
                         v_lru_lambda, v_sc_conv_w, v_w_out, v_norm2_g, v_w_up, v_ffn_conv_w, v_w_down, v_final_g]))
    my_id = 4 * lax.axis_index("x") + 2 * lax.axis_index("y") + lax.axis_index("c")

    taps = jnp.zeros((DEPTH, 16, 768), F32)
    taps = taps.at[:, 0:4, 0:128].set(lru_conv_w).at[:, 4:7, 0:64].set(sc_conv_w).at[:, 8:11, :].set(ffn_conv_w)
    shards = {}
    for l in range(DEPTH):
        shards[f"w_in_t{l}"] = jnp.swapaxes(w_in[l], 0, 1).astype(BF16)
        if l == 0:
            shards["taps"] = taps.reshape(DEPTH * 16, 768)
        shards[f"w_out{l}"] = w_out[l].astype(BF16)
        shards[f"w_up_b{l}"] = w_up[l].astype(BF16)
        shards[f"w_down{l}"] = w_down[l].astype(BF16)
    ids = iter(range(18))
    got = {}
    chain = [None]
    for group in (("w_in_t0", "taps"), ("w_out0",), ("w_up_b0",), ("w_down0",),
                  ("w_in_t1",), ("w_out1",), ("w_up_b1",), ("w_down1",)):
        lands, chain[0] = _sequencer_copies([shards[k] for k in group], gather=True, name=f"gather_{group[0]}",
                                            collective_id=next(ids), after=None)
        got.update(zip(group, lands))

    def fetch(key, after):
        return got[key]

    def tap_rows(l, lo, hi, width, after):
        tl = fetch("taps", after).reshape(N_DEV, DEPTH, 16, 768)[:, l, lo:hi, 0:width]
        return jnp.transpose(tl, (1, 0, 2)).reshape(hi - lo, N_DEV * width)

    layers = []
    for l in range(DEPTH):
        layers.append(dict(
            g1=norm1_g[l], g2=norm2_g[l], cb=lru_conv_b[l], ba=lru_ba[l], bx=lru_bx[l], lam=lru_lambda[l],
            wa_bd=_block_diag(lru_wa[l]), wx_bd=_block_diag(lru_wx[l]),
            cw=functools.partial(tap_rows, l, 0, 4, 128), scw=functools.partial(tap_rows, l, 4, 7, 64),
            fcw=lambda after, l=l: tap_rows(l, 8, 11, 768, after).reshape(3, 2, D_FF).transpose(1, 0, 2),
            w_in_t=lambda after, l=l: fetch(f"w_in_t{l}", after).reshape(D_IN, D_MODEL),
            w_out=lambda after, l=l: fetch(f"w_out{l}", after).reshape(D_MIX, D_MODEL),
            w_up_b=lambda after, l=l: fetch(f"w_up_b{l}", after),
            w_down=lambda after, l=l: fetch(f"w_down{l}", after).reshape(D_FF, D_MODEL)))

    scatter_handles = {}

    def put(l, key, grad):
        blocks = grad if grad.ndim == 3 else grad.reshape(N_DEV, grad.shape[0] // N_DEV, grad.shape[1])
        (scatter_handles[(l, key)],), chain[0] = _sequencer_copies(
            [blocks], gather=False, name=f"scatter_{key}{l}", collective_id=next(ids), after=chain[0])
        return blocks

    loss_local, dx, smalls, dgf = _local_step(x[0], loss_target[0], layers, final_g, put)

    parts = []
    for l in range(DEPTH):
        for key in ("w_in_t", "w_out", "w_up_b", "w_down"):
            parts.append(scatter_handles[(l, key)])

    flat = [smalls[l][k].reshape(-1) for l in range(DEPTH) for k in SMALL_ORDER] + [dgf.reshape(-1)]
    flat.append(jnp.broadcast_to(loss_local, (128,)))
    sizes = [f.shape[0] for f in flat]
    total = sum(sizes)
    rows = -(-total // (N_DEV * 128 * 8)) * 8
    flat.append(jnp.zeros((N_DEV * rows * 128 - total,), F32))
    (small_parts,), chain[0] = _sequencer_copies([jnp.concatenate(flat).reshape(N_DEV, rows, 128)], gather=False,
                                                 name="scatter_small", collective_id=next(ids), after=chain[0])
    grads, deltas, new_m, new_v = {}, {}, {}, {}
    for slot, k in ((2, "w_up"), (3, "w_down"), (1, "w_out")):
        grads[k], deltas[k], new_m[k], new_v[k] = _sum_adamw(
            [parts[4 * l + slot] for l in range(DEPTH)], w[k], m[k], v[k], name=f"adamw_{k}")
    small_mine = _sum_parts(small_parts, name="sum_small", tokens=(deltas["w_up"], deltas["w_down"]))
    (small_all,), _ = _sequencer_copies([small_mine], gather=True, name="gather_small",
                                        collective_id=next(ids), after=chain[0])
    grads["w_in"] = jnp.stack([jnp.swapaxes(_sum_parts(parts[4 * l], name=f"sum_w_in_l{l}"), 0, 1)
                               for l in range(DEPTH)])
    small_sum = small_all.reshape(-1)
    small_g, off = [], 0
    for sz in sizes:
        small_g.append(small_sum[off:off + sz])
        off += sz
    gs = {}
    for l in range(DEPTH):
        for i, k in enumerate(SMALL_ORDER):
            gs.setdefault(k, []).append(small_g[l * len(SMALL_ORDER) + i])
    g_final = small_g[-2]
    loss = small_g[-1][0]

    for k in ("norm1_g", "lru_conv_b", "lru_ba", "lru_bx", "lru_lambda", "norm2_g"):
        grads[k] = jnp.stack(gs[k]).reshape(DEPTH, -1)
    for k in ("lru_wa", "lru_wx"):
        grads[k] = jnp.stack(gs[k]).reshape(DEPTH, LRU_HEADS, LRU_HEAD_DIM, LRU_HEAD_DIM)
    grads["final_g"] = g_final
    cw_full = jnp.stack(gs["lru_conv_w"]).reshape(DEPTH, 4, N_DEV, 128)
    grads["lru_conv_w"] = lax.dynamic_index_in_dim(cw_full, my_id, axis=2, keepdims=False)
    scw_full = jnp.stack(gs["sc_conv_w"]).reshape(DEPTH, 3, N_DEV, 64)
    grads["sc_conv_w"] = lax.dynamic_index_in_dim(scw_full, my_id, axis=2, keepdims=False)
    fcw_full = jnp.stack(gs["ffn_conv_w"]).reshape(DEPTH, 2, 3, D_FF).transpose(0, 2, 1, 3).reshape(DEPTH, 3, N_DEV, 768)
    grads["ffn_conv_w"] = lax.dynamic_index_in_dim(fcw_full, my_id, axis=2, keepdims=False)

    for k in names:
        if k in deltas:
            continue
        shape = w[k].shape
        cols = shape[-1]
        as2d = lambda a: a.reshape(-1, cols)
        d, nm, nv = _adamw(as2d(w[k]), as2d(grads[k]), as2d(m[k]), as2d(v[k]), name=f"adamw_{k}")
        deltas[k], new_m[k], new_v[k] = d.reshape(shape), nm.reshape(shape), nv.reshape(shape)

    return (loss, dx[None], *[grads[k] for k in names], *[deltas[k] for k in names],
            *[new_m[k] for k in names], *[new_v[k] for k in names])
```
